```python
import math
import jax, jax.numpy as jnp
from jax import lax
import numpy as np

D_MODEL = 1024
BATCH = 2
SEQ = 8192
DEPTH = 1
DEC_BATCH = 32
DEC_SEQ = 4
PAST_LEN = 16384
PAGE_SIZE = 128

N_HEADS = 8
HEAD_DIM = 64
N_KV_HEADS = 2
GROUP = N_HEADS // N_KV_HEADS
NSA_WIDTH = N_HEADS * HEAD_DIM
KV_WIDTH = N_KV_HEADS * HEAD_DIM
CMP_LEN = 32
CMP_STRIDE = 16
CMP_RATIO = CMP_LEN // CMP_STRIDE
CMP_HIDDEN = 256
SEL_BLOCK = 64
N_SELECT = 16
WINDOW = 512
CONV_DIM = 512
CONV_WIDTH = 3
D_FF = 4 * D_MODEL
N_BUCKETS = 32
MAX_DISTANCE = 128
Q_BLOCK = 128
RMS_EPS = 1e-6
NEG_INF = -1e30
FORCE_BONUS = 1e4
SPLIT_SIZES = (NSA_WIDTH,) + (KV_WIDTH,) * 6 + (3 * N_HEADS, CONV_DIM, CONV_DIM, CONV_DIM, D_MODEL, D_MODEL)
IN_COLS = sum(SPLIT_SIZES)

kernel_name = "hybrid_nsa_shortconv_decode_step"


def rms_norm(x, g):
    x32 = x.astype(jnp.float32)
    y = x32 * lax.rsqrt(jnp.mean(x32 * x32, axis=-1, keepdims=True) + RMS_EPS)
    return (y * g.astype(jnp.float32)).astype(x.dtype)


def rel_bucket(dist):
    max_exact = N_BUCKETS // 2
    n = jnp.maximum(dist, 0)
    nf = jnp.maximum(n, max_exact).astype(jnp.float32)
    large = max_exact + (jnp.log(nf / max_exact) / math.log(MAX_DISTANCE / max_exact)
                         * (N_BUCKETS - max_exact)).astype(jnp.int32)
    return jnp.where(n < max_exact, n, jnp.minimum(large, N_BUCKETS - 1))


def masked_softmax(logits, valid):
    p = jax.nn.softmax(jnp.where(valid, logits, NEG_INF), axis=-1)
    return p * jnp.any(valid, axis=-1, keepdims=True)


def compress(k, pe, w1, w2):
    b, t, hk, dh = k.shape
    n_chunk = t // CMP_STRIDE
    n_cmp = n_chunk - CMP_RATIO + 1
    chunks = (k[:, :n_chunk * CMP_STRIDE]
              .reshape(b, n_chunk, CMP_STRIDE, hk, dh)
              .transpose(0, 1, 3, 2, 4)
              .reshape(b, n_chunk, hk, CMP_STRIDE * dh))
    w1r = w1.reshape(CMP_RATIO, CMP_STRIDE * dh, CMP_HIDDEN)
    proj = jnp.einsum('bckx,mxh->mbckh', chunks, w1r)
    hid = pe.reshape(-1) @ w1
    for m in range(CMP_RATIO):
        hid = hid + proj[m, :, m:m + n_cmp]
    return jax.nn.gelu(hid) @ w2


def cmp_to_sel_map(n_cmp, n_blocks):
    c0 = jnp.arange(n_cmp)[:, None] * CMP_STRIDE
    s0 = jnp.arange(n_blocks)[None, :] * SEL_BLOCK
    ov = jnp.minimum(c0 + CMP_LEN, s0 + SEL_BLOCK) - jnp.maximum(c0, s0)
    return jnp.clip(ov, 0).astype(jnp.float32) / CMP_LEN


def nsa_group(q, t, kc, vc, cend, ksb, vsb, kw, vw, wpos, gates, table_g, sel_map):
    b, g, nq, dh = q.shape
    scale = dh ** -0.5
    f32 = jnp.float32
    cdist = t[:, None] - cend[None, :]
    lc = jnp.einsum('bgqd,bcd->bgqc', q, kc).astype(f32) * scale + table_g[:, rel_bucket(cdist)]
    pc = masked_softmax(lc, cdist >= 0)
    o_cmp = jnp.einsum('bgqc,bcd->bgqd', pc.astype(vc.dtype), vc)
    n_blocks = ksb.shape[1]
    blk = jnp.arange(n_blocks)[None, :]
    cur = (t // SEL_BLOCK)[:, None]
    imp = jnp.einsum('bgqc,cj->bqj', pc, sel_map)
    forced = (blk == 0) | (blk == cur) | (blk == cur - 1)
    imp = jnp.where(blk > cur, NEG_INF, imp + FORCE_BONUS * forced)
    _, idx = lax.top_k(imp, min(N_SELECT, n_blocks))
    bidx = jnp.arange(b)[:, None, None]
    ks, vs = ksb[bidx, idx], vsb[bidx, idx]
    sdist = t[None, :, None, None] - (idx[..., None] * SEL_BLOCK + jnp.arange(SEL_BLOCK))
    ls = (jnp.einsum('bgqd,bqnld->bgqnl', q, ks).astype(f32) * scale
          + jnp.moveaxis(table_g[:, rel_bucket(sdist)], 0, 1))
    n_keys = idx.shape[-1] * SEL_BLOCK
    ps = masked_softmax(ls.reshape(b, g, nq, n_keys), (sdist >= 0).reshape(b, 1, nq, n_keys))
    o_slc = jnp.einsum('bgqm,bqmd->bgqd', ps.astype(vs.dtype), vs.reshape(b, nq, n_keys, dh))
    wdist = t[:, None] - wpos[None, :]
    lw = jnp.einsum('bgqd,bkd->bgqk', q, kw).astype(f32) * scale + table_g[:, rel_bucket(wdist)]
    pw = masked_softmax(lw, (wdist >= 0) & (wdist < WINDOW) & (wpos >= 0)[None, :])
    o_win = jnp.einsum('bgqk,bkd->bgqd', pw.astype(vw.dtype), vw)
    gs = jax.nn.sigmoid(gates.astype(f32)).astype(o_cmp.dtype)
    return gs[..., 0:1] * o_cmp + gs[..., 1:2] * o_slc + gs[..., 2:3] * o_win


nsa_attend = jax.vmap(nsa_group, in_axes=(0, None, 0, 0, None, 0, 0, 0, 0, None, 0, 0, None))


def kv_rows(z):
    return z.reshape(z.shape[0], z.shape[1], N_KV_HEADS, HEAD_DIM)


def to_groups(z, last):
    return z.reshape(z.shape[0], z.shape[1], N_KV_HEADS, GROUP, last).transpose(2, 0, 3, 1, 4)


def nsa_memory(kc, vc, ks, vs, cmp_w):
    pe_k, w1_k, w2_k, pe_v, w1_v, w2_v = cmp_w
    b, t, hk, dh = kc.shape
    kcmp = compress(kc, pe_k, w1_k, w2_k).transpose(2, 0, 1, 3)
    vcmp = compress(vc, pe_v, w1_v, w2_v).transpose(2, 0, 1, 3)
    n_cmp = kcmp.shape[2]
    cend = jnp.arange(n_cmp) * CMP_STRIDE + CMP_LEN - 1
    n_blocks = -(-t // SEL_BLOCK)
    pad = ((0, 0), (0, n_blocks * SEL_BLOCK - t), (0, 0), (0, 0))

    def blocks(z):
        return jnp.pad(z, pad).reshape(b, n_blocks, SEL_BLOCK, hk, dh).transpose(3, 0, 1, 2, 4)

    return kcmp, vcmp, cend, blocks(ks), blocks(vs), cmp_to_sel_map(n_cmp, n_blocks)


def nsa_prompt(q, kc, vc, ks, vs, kw, vw, gates, table, cmp_w):
    b, t, _ = q.shape
    kcmp, vcmp, cend, ksb, vsb, sel_map = nsa_memory(kc, vc, ks, vs, cmp_w)
    qg, gg = to_groups(q, HEAD_DIM), to_groups(gates, 3)
    pad = ((0, 0), (0, 0), (WINDOW, 0), (0, 0))
    kwp = jnp.pad(kw.transpose(2, 0, 1, 3), pad)
    vwp = jnp.pad(vw.transpose(2, 0, 1, 3), pad)

    def block(qb):
        s0 = qb * Q_BLOCK
        pos = s0 + jnp.arange(Q_BLOCK)
        wpos = s0 - WINDOW + jnp.arange(Q_BLOCK + WINDOW)
        return nsa_attend(
            lax.dynamic_slice_in_dim(qg, s0, Q_BLOCK, axis=3), pos, kcmp, vcmp, cend, ksb, vsb,
            lax.dynamic_slice_in_dim(kwp, s0, Q_BLOCK + WINDOW, axis=2),
            lax.dynamic_slice_in_dim(vwp, s0, Q_BLOCK + WINDOW, axis=2), wpos,
            lax.dynamic_slice_in_dim(gg, s0, Q_BLOCK, axis=3), table, sel_map)

    outs = lax.map(block, jnp.arange(t // Q_BLOCK))
    return outs.transpose(2, 0, 4, 1, 3, 5).reshape(b, t, NSA_WIDTH)


def nsa_sample(q, kc, vc, ks, vs, kw, vw, gates, table, cmp_w, past_len):
    b, s, _ = q.shape
    kcmp, vcmp, cend, ksb, vsb, sel_map = nsa_memory(kc, vc, ks, vs, cmp_w)
    lw = kw.shape[1]
    pos = past_len + jnp.arange(s)
    wpos = past_len + s - lw + jnp.arange(lw)
    o = nsa_attend(to_groups(q, HEAD_DIM), pos, kcmp, vcmp, cend, ksb, vsb,
                   kw.transpose(2, 0, 1, 3), vw.transpose(2, 0, 1, 3), wpos,
                   to_groups(gates, 3), table, sel_map)
    return o.transpose(1, 3, 0, 2, 4).reshape(b, s, NSA_WIDTH)


def short_conv(b_gate, c_gate, h, past, conv_w, conv_b):
    u = c_gate * h
    up = jnp.concatenate([past.astype(u.dtype), u], axis=1)
    t = u.shape[1]
    y = conv_b
    for k in range(CONV_WIDTH):
        y = y + conv_w[k] * up[:, k:k + t]
    return b_gate * y, up[:, t:]


def mixer_inputs(x, g_attn, w_in):
    h = rms_norm(x, g_attn)
    offsets = np.cumsum(SPLIT_SIZES)[:-1].tolist()
    return jnp.split(h @ w_in, offsets, axis=-1)


def merge_and_mlp(x, a, c, ga, gb, w_nsa_out, w_conv_out, w_o, g_mlp, w_up, w_down):
    m = jax.nn.sigmoid(ga) * (a @ w_nsa_out) + jax.nn.sigmoid(gb) * (c @ w_conv_out)
    x = x + m @ w_o
    h = rms_norm(x, g_mlp)
    return x + jnp.square(jax.nn.relu(h @ w_up)) @ w_down


def setup_inputs(seed: int = 0) -> dict:
    key = jax.random.key(seed)
    keys = iter(jax.random.split(key, 40))

    def nrm(shape, scale):
        return scale * jax.random.normal(next(keys), shape, jnp.float32)

    n_pages = PAST_LEN // PAGE_SIZE
    n_phys = (DEC_BATCH * n_pages * 5) // 4
    win_buf = min(WINDOW, PAST_LEN)
    perm = jax.random.permutation(next(keys), n_phys)
    page_table = perm[:DEC_BATCH * n_pages].reshape(DEC_BATCH, n_pages).astype(jnp.int32)
    page_shape = (DEPTH, n_phys, PAGE_SIZE, N_KV_HEADS, HEAD_DIM)
    win_shape = (DEPTH, DEC_BATCH, win_buf, N_KV_HEADS, HEAD_DIM)
    L = DEPTH
    return {
        "x_prompt": nrm((BATCH, SEQ, D_MODEL), 1.0),
        "x_sample": nrm((DEC_BATCH, DEC_SEQ, D_MODEL), 1.0),
        "cache_cmp_k": nrm(page_shape, 1.0),
        "cache_cmp_v": nrm(page_shape, 1.0),
        "cache_slc_k": nrm(page_shape, 1.0),
        "cache_slc_v": nrm(page_shape, 1.0),
        "state_win_k": nrm(win_shape, 1.0),
        "state_win_v": nrm(win_shape, 1.0),
        "state_conv": nrm((DEPTH, DEC_BATCH, CONV_WIDTH - 1, CONV_DIM), 1.0),
        "page_table": page_table,
        "g_attn": 1.0 + nrm((L, D_MODEL), 0.02),
        "w_in": nrm((L, D_MODEL, IN_COLS), D_MODEL ** -0.5),
        "cmp_pe_k": nrm((L, CMP_LEN, HEAD_DIM), 0.1),
        "cmp_w1_k": nrm((L, CMP_LEN * HEAD_DIM, CMP_HIDDEN), (CMP_LEN * HEAD_DIM) ** -0.5),
        "cmp_w2_k": nrm((L, CMP_HIDDEN, HEAD_DIM), CMP_HIDDEN ** -0.5),
        "cmp_pe_v": nrm((L, CMP_LEN, HEAD_DIM), 0.1),
        "cmp_w1_v": nrm((L, CMP_LEN * HEAD_DIM, CMP_HIDDEN), (CMP_LEN * HEAD_DIM) ** -0.5),
        "cmp_w2_v": nrm((L, CMP_HIDDEN, HEAD_DIM), CMP_HIDDEN ** -0.5),
        "conv_w": nrm((L, CONV_WIDTH, CONV_DIM), CONV_WIDTH ** -0.5),
        "conv_b": nrm((L, CONV_DIM), 0.01),
        "w_nsa_out": nrm((L, NSA_WIDTH, D_MODEL), NSA_WIDTH ** -0.5),
        "w_conv_out": nrm((L, CONV_DIM, D_MODEL), CONV_DIM ** -0.5),
        "w_o": nrm((L, D_MODEL, D_MODEL), D_MODEL ** -0.5),
        "g_mlp": 1.0 + nrm((L, D_MODEL), 0.02),
        "w_up": nrm((L, D_MODEL, D_FF), D_MODEL ** -0.5),
        "w_down": nrm((L, D_FF, D_MODEL), D_FF ** -0.5),
        "rel_bias": nrm((N_BUCKETS, N_HEADS), 0.5),
        "g_final": 1.0 + nrm((D_MODEL,), 0.02),
    }


def reference(x_prompt, x_sample, cache_cmp_k, cache_cmp_v, cache_slc_k, cache_slc_v,
              state_win_k, state_win_v, state_conv, page_table,
              g_attn, w_in, cmp_pe_k, cmp_w1_k, cmp_w2_k, cmp_pe_v, cmp_w1_v, cmp_w2_v,
              conv_w, conv_b, w_nsa_out, w_conv_out, w_o, g_mlp, w_up, w_down,
              rel_bias, g_final):
    table = rel_bias.astype(jnp.float32).T.reshape(N_KV_HEADS, GROUP, N_BUCKETS)
    n_prompt, len_prompt, _ = x_prompt.shape
    n_sample = x_sample.shape[0]
    past_len = page_table.shape[1] * cache_cmp_k.shape[2]
    win_buf = state_win_k.shape[2]
    win_prompt = min(WINDOW, len_prompt)

    def paged(cache):
        return cache[page_table].reshape(n_sample, past_len, N_KV_HEADS, HEAD_DIM)

    xp, xs = x_prompt, x_sample
    new_p = [[] for _ in range(7)]
    new_s = [[] for _ in range(7)]
    for l in range(DEPTH):
        cmp_w = (cmp_pe_k[l], cmp_w1_k[l], cmp_w2_k[l], cmp_pe_v[l], cmp_w1_v[l], cmp_w2_v[l])
        tail = (w_nsa_out[l], w_conv_out[l], w_o[l], g_mlp[l], w_up[l], w_down[l])

        q, kc, vc, ks, vs, kw, vw, gn, bg, cg, hc, ga, gb = mixer_inputs(xp, g_attn[l], w_in[l])
        kc, vc, ks, vs, kw, vw = (kv_rows(z) for z in (kc, vc, ks, vs, kw, vw))
        a = nsa_prompt(q, kc, vc, ks, vs, kw, vw, gn, table, cmp_w)
        c, conv_p = short_conv(bg, cg, hc, jnp.zeros((n_prompt, CONV_WIDTH - 1, CONV_DIM), hc.dtype),
                               conv_w[l], conv_b[l])
        xp = merge_and_mlp(xp, a, c, ga, gb, *tail)
        for lst, z in zip(new_p, (kc, vc, ks, vs, kw[:, -win_prompt:], vw[:, -win_prompt:], conv_p)):
            lst.append(z)

        q, kc, vc, ks, vs, kw, vw, gn, bg, cg, hc, ga, gb = mixer_inputs(xs, g_attn[l], w_in[l])
        kc, vc, ks, vs, kw, vw = (kv_rows(z) for z in (kc, vc, ks, vs, kw, vw))
        kc_all = jnp.concatenate([paged(cache_cmp_k[l]).astype(kc.dtype), kc], axis=1)
        vc_all = jnp.concatenate([paged(cache_cmp_v[l]).astype(vc.dtype), vc], axis=1)
        ks_all = jnp.concatenate([paged(cache_slc_k[l]).astype(ks.dtype), ks], axis=1)
        vs_all = jnp.concatenate([paged(cache_slc_v[l]).astype(vs.dtype), vs], axis=1)
        kw_all = jnp.concatenate([state_win_k[l].astype(kw.dtype), kw], axis=1)
        vw_all = jnp.concatenate([state_win_v[l].astype(vw.dtype), vw], axis=1)
        a = nsa_sample(q, kc_all, vc_all, ks_all, vs_all, kw_all, vw_all, gn, table, cmp_w, past_len)
        c, conv_s = short_conv(bg, cg, hc, state_conv[l], conv_w[l], conv_b[l])
        xs = merge_and_mlp(xs, a, c, ga, gb, *tail)
        for lst, z in zip(new_s, (kc, vc, ks, vs, kw_all[:, -win_buf:], vw_all[:, -win_buf:], conv_s)):
            lst.append(z)

    p_cmp_k, p_cmp_v, p_slc_k, p_slc_v, p_win_k, p_win_v, p_conv = (jnp.stack(z) for z in new_p)
    s_cmp_k, s_cmp_v, s_slc_k, s_slc_v, s_win_k, s_win_v, s_conv = (jnp.stack(z) for z in new_s)
    y_prompt = rms_norm(xp, g_final)
    y_sample = rms_norm(xs, g_final)
    return (y_prompt, y_sample,
            p_cmp_k, p_cmp_v, p_slc_k, p_slc_v, p_win_k, p_win_v, p_conv,
            s_cmp_k, s_cmp_v, s_slc_k, s_slc_v, s_win_k, s_win_v, s_conv)
```

```python
import functools
import math

import numpy as np
import jax
import jax.numpy as jnp
from jax import lax
from jax.experimental import pallas as pl
from jax.experimental.pallas import tpu as pltpu

F32 = jnp.float32
BF16 = jnp.bfloat16

D_MODEL = 1024
N_HEADS = 8
HEAD_DIM = 64
N_KV_HEADS = 2
GROUP = N_HEADS // N_KV_HEADS
KV_WIDTH = N_KV_HEADS * HEAD_DIM
NSA_WIDTH = N_HEADS * HEAD_DIM
CMP_LEN = 32
CMP_STRIDE = 16
CMP_RATIO = CMP_LEN // CMP_STRIDE
CMP_HIDDEN = 256
SEL_BLOCK = 64
N_SELECT = 16
WINDOW = 512
CONV_DIM = 512
CONV_WIDTH = 3
D_FF = 4 * D_MODEL
N_BUCKETS = 32
MAX_DISTANCE = 128
Q_BLOCK = 128
PAGE = 128
RMS_EPS = 1e-6
NEG_INF = -1e30
FORCE_BONUS = 1e4
LANES = 128
VMEM_LIMIT = 56 * 1024 * 1024


def _bucket_thresholds():
    max_exact = N_BUCKETS // 2
    n = np.arange(max_exact, 4 * MAX_DISTANCE).astype(np.float32)
    large = max_exact + (np.log(n / np.float32(max_exact)) / np.float32(math.log(MAX_DISTANCE / max_exact))
                         * np.float32(N_BUCKETS - max_exact)).astype(np.int32)
    large = np.minimum(large, N_BUCKETS - 1)
    thr = []
    for k in range(max_exact + 1, N_BUCKETS):
        thr.append(int(np.arange(max_exact, 4 * MAX_DISTANCE)[np.argmax(large >= k)]))
    return tuple(thr)


BUCKET_THR = _bucket_thresholds()


def _bucket(dist):
    n = jnp.maximum(dist, 0)
    big = jnp.full(n.shape, N_BUCKETS // 2, jnp.int32)
    for thr in BUCKET_THR:
        big = big + (n >= thr).astype(jnp.int32)
    return jnp.where(n < N_BUCKETS // 2, n, big)


def _bias_from_scalars(bucket, table_fn):
    out = jnp.zeros(bucket.shape, F32)
    for k in range(N_BUCKETS):
        out = jnp.where(bucket == k, table_fn(k), out)
    return out


def _bias_from_rows(bucket, rowtab):
    out = jnp.zeros(bucket.shape, F32)
    for k in range(N_BUCKETS):
        out = jnp.where(bucket == k, rowtab[:, k:k + 1], out)
    return out


def _dot(a, b):
    return jnp.dot(a, b, preferred_element_type=F32)


def _dot_t(a, b):
    return lax.dot_general(a, b, (((1,), (1,)), ((), ())), preferred_element_type=F32)


def _split_dot(p, m):
    hi = p.astype(BF16)
    lo = (p - hi.astype(F32)).astype(BF16)
    return _dot(hi, m) + _dot(lo, m)


def _cparams(sem):
    return pltpu.CompilerParams(dimension_semantics=sem, vmem_limit_bytes=VMEM_LIMIT)


PROJ_WIDTHS = (NSA_WIDTH, 6 * KV_WIDTH, LANES, 3 * CONV_DIM, 2 * D_MODEL)
PROJ_COLS = sum(PROJ_WIDTHS)


def _proj_kernel(x_ref, g_ref, w_ref, q_ref, kc_ref, vc_ref, ks_ref, vs_ref, kw_ref, vw_ref,
                 kvb_ref, gn_ref, conv_ref, gate_ref):
    x = x_ref[...]
    y = x * lax.rsqrt(jnp.mean(x * x, axis=-1, keepdims=True) + RMS_EPS)
    hb = (y * g_ref[...]).astype(BF16)
    q_ref[...] = _dot(hb, w_ref[:, 0:NSA_WIDTH]).astype(q_ref.dtype)
    off = NSA_WIDTH
    kv = _dot(hb, w_ref[:, off:off + 6 * KV_WIDTH])
    for i, r in enumerate((kc_ref, vc_ref, ks_ref, vs_ref, kw_ref, vw_ref)):
        r[...] = kv[:, i * KV_WIDTH:(i + 1) * KV_WIDTH]
    kvb_ref[...] = kv.astype(BF16)
    off += 6 * KV_WIDTH
    gn_ref[...] = _dot(hb, w_ref[:, off:off + LANES])
    off += LANES
    conv_ref[...] = _dot(hb, w_ref[:, off:off + 3 * CONV_DIM])
    off += 3 * CONV_DIM
    gate_ref[...] = _dot(hb, w_ref[:, off:off + 2 * D_MODEL])


def _proj(x2d, g, w_all):
    n = x2d.shape[0]
    tm = min(256, n)
    assert n % tm == 0
    row = lambda w: pl.BlockSpec((tm, w), lambda i: (i, 0))
    out_shapes = ([jax.ShapeDtypeStruct((n, NSA_WIDTH), BF16)]
                  + [jax.ShapeDtypeStruct((n, KV_WIDTH), F32)] * 6
                  + [jax.ShapeDtypeStruct((n, 6 * KV_WIDTH), BF16),
                     jax.ShapeDtypeStruct((n, LANES), F32),
                     jax.ShapeDtypeStruct((n, 3 * CONV_DIM), F32),
                     jax.ShapeDtypeStruct((n, 2 * D_MODEL), F32)])
    out_specs = ([row(NSA_WIDTH)] + [row(KV_WIDTH)] * 6
                 + [row(6 * KV_WIDTH), row(LANES), row(3 * CONV_DIM), row(2 * D_MODEL)])
    return pl.pallas_call(
        _proj_kernel,
        grid=(n // tm,),
        in_specs=[row(D_MODEL),
                  pl.BlockSpec((1, D_MODEL), lambda i: (0, 0)),
                  pl.BlockSpec((D_MODEL, PROJ_COLS), lambda i: (0, 0))],
        out_specs=out_specs,
        out_shape=out_shapes,
        compiler_params=_cparams(("arbitrary",)),
        name="proj",
    )(x2d, g, w_all)


def _prep_w_in(w_in):
    sizes = (NSA_WIDTH,) + (KV_WIDTH,) * 6 + (3 * N_HEADS, CONV_DIM, CONV_DIM, CONV_DIM, D_MODEL, D_MODEL)
    offs = np.concatenate([[0], np.cumsum(sizes)])
    wq = w_in[:, :NSA_WIDTH].reshape(D_MODEL, N_KV_HEADS, GROUP, HEAD_DIM)
    wq = wq.transpose(0, 2, 1, 3).reshape(D_MODEL, NSA_WIDTH)
    wkv = w_in[:, offs[1]:offs[7]]
    wgn = jnp.pad(w_in[:, offs[7]:offs[8]], ((0, 0), (0, LANES - 3 * N_HEADS)))
    wrest = w_in[:, offs[8]:]
    return jnp.concatenate([wq, wkv, wgn, wrest], axis=1).astype(BF16)


CHUNKS_PER_PAGE = PAGE // CMP_STRIDE


def _compress_kernel(pt_ref, *refs, pp):
    del pt_ref
    page_refs = refs[:pp + 1]
    wbd_ref, pe_ref, w1_ref, w2_ref, out_ref, acc_ref, pehid_ref = refs[pp + 1:]

    @pl.when((pl.program_id(0) == 0) & (pl.program_id(1) == 0))
    def _():
        pehid_ref[...] = _dot(pe_ref[...], w1_ref[...])

    cols = []
    for r in range(CMP_STRIDE):
        xr = jnp.concatenate([p[0, pl.ds(r, CHUNKS_PER_PAGE, stride=CMP_STRIDE), :] for p in page_refs], axis=0)
        cols.append(xr.astype(BF16))
    x = jnp.concatenate(cols, axis=1)
    acc_ref[...] = _dot(x, wbd_ref[...])
    nc = CHUNKS_PER_PAGE * pp
    pe = pehid_ref[0:1, :]
    hids = []
    for h in range(N_KV_HEADS):
        base = h * CMP_RATIO * CMP_HIDDEN
        p0 = acc_ref[0:nc, base:base + CMP_HIDDEN]
        p1 = acc_ref[pl.ds(1, nc), base + CMP_HIDDEN:base + 2 * CMP_HIDDEN]
        hids.append(jax.nn.gelu((pe + p0) + p1).astype(BF16))
    out_ref[0] = _dot(jnp.concatenate(hids, axis=1), w2_ref[...])


def _compress(pages, colblk, page_table, pe, w1, w2):
    nb, npages = page_table.shape
    pp = min(32, npages)
    assert npages % pp == 0
    nc = CHUNKS_PER_PAGE * pp
    w1r = w1.reshape(CMP_RATIO, CMP_STRIDE, HEAD_DIM, CMP_HIDDEN)
    blk = w1r.transpose(1, 2, 0, 3).reshape(CMP_STRIDE, HEAD_DIM, CMP_RATIO * CMP_HIDDEN)
    z = jnp.zeros_like(blk)
    wbd = jnp.concatenate([jnp.concatenate([blk, z], axis=2), jnp.concatenate([z, blk], axis=2)], axis=1)
    wbd = wbd.reshape(CMP_STRIDE * KV_WIDTH, N_KV_HEADS * CMP_RATIO * CMP_HIDDEN).astype(BF16)
    z2 = jnp.zeros_like(w2)
    w2bd = jnp.concatenate([jnp.concatenate([w2, z2], axis=1), jnp.concatenate([z2, w2], axis=1)], axis=0).astype(BF16)
    pe8 = jnp.broadcast_to(pe.reshape(1, CMP_LEN * HEAD_DIM), (8, CMP_LEN * HEAD_DIM)).astype(BF16)

    def page_spec(p):
        def imap(b, s, pt):
            return (pt[b, jnp.minimum(s * pp + p, npages - 1)], 0, colblk)
        return pl.BlockSpec((1, PAGE, KV_WIDTH), imap)

    const = lambda shape: pl.BlockSpec(shape, lambda b, s, pt: (0,) * len(shape))
    grid_spec = pltpu.PrefetchScalarGridSpec(
        num_scalar_prefetch=1,
        grid=(nb, npages // pp),
        in_specs=[page_spec(p) for p in range(pp + 1)]
        + [const(wbd.shape), const(pe8.shape), const((CMP_LEN * HEAD_DIM, CMP_HIDDEN)), const(w2bd.shape)],
        out_specs=pl.BlockSpec((1, nc, KV_WIDTH), lambda b, s, pt: (b, s, 0)),
        scratch_shapes=[pltpu.VMEM((nc + 8, N_KV_HEADS * CMP_RATIO * CMP_HIDDEN), F32),
                        pltpu.VMEM((8, CMP_HIDDEN), F32)],
    )
    return pl.pallas_call(
        functools.partial(_compress_kernel, pp=pp),
        grid_spec=grid_spec,
        out_shape=jax.ShapeDtypeStruct((nb, npages * CHUNKS_PER_PAGE, KV_WIDTH), F32),
        compiler_params=_cparams(("arbitrary", "arbitrary")),
        name="compress",
    )(page_table, *([pages] * (pp + 1)), wbd, pe8, w1.astype(BF16), w2bd)


def _sel_map(n_rows, n_cmp, n_cols, n_blocks):
    c0 = np.arange(n_rows)[:, None] * CMP_STRIDE
    s0 = np.arange(n_cols)[None, :] * SEL_BLOCK
    ov = np.minimum(c0 + CMP_LEN, s0 + SEL_BLOCK) - np.maximum(c0, s0)
    m = np.clip(ov, 0, None).astype(np.float32) / CMP_LEN
    m[n_cmp:] = 0.0
    m[:, n_blocks:] = 0.0
    return m


CMP_FRONT = 16
CMP_NEAR = 24
SLC_NEAR = 2 * Q_BLOCK
SLC_TILE = 512
WIN_KEYS = WINDOW + Q_BLOCK


def _softmax_update(state, s, mask, v):
    m_old, l_old, acc = state
    m_new = jnp.maximum(m_old, jnp.max(s, axis=-1, keepdims=True))
    p = jnp.where(mask, jnp.exp(s - m_new), 0.0)
    alpha = jnp.exp(m_old - m_new)
    l_new = alpha * l_old + jnp.sum(p, axis=-1, keepdims=True)
    acc = alpha * acc + _dot(p.astype(BF16), v)
    return m_new, l_new, acc


def _softmax_finish(state):
    _, l, acc = state
    return jnp.where(l > 0.0, acc / jnp.where(l > 0.0, l, 1.0), 0.0)


def _tile_rows(x, n):
    return jnp.concatenate([x] * n, axis=0)


def _nsa_prompt_kernel(rb_ref, q_ref, gn_ref, kcmp_ref, vcmp_ref, selmap_ref, ks_ref, vs_ref, kw_ref, vw_ref,
                       e_ref, out_ref, nbc_ref, nbs_ref, nbw_ref):
    hk = pl.program_id(1)
    qb = pl.program_id(2)
    s0 = qb * Q_BLOCK
    gq = GROUP * Q_BLOCK

    @pl.when(qb == 0)
    def _():
        for g in range(GROUP):
            tab = lambda k, g=g: rb_ref[k, hk * GROUP + g]
            ql = lax.broadcasted_iota(jnp.int32, (Q_BLOCK, LANES), 0)
            cl = lax.broadcasted_iota(jnp.int32, (Q_BLOCK, LANES), 1)
            dist = ql - CMP_STRIDE * (cl - CMP_FRONT) - (CMP_LEN - 1)
            ok = (dist >= 0) & (cl < CMP_NEAR)
            nbc_ref[g] = jnp.where(ok, _bias_from_scalars(_bucket(dist), tab), NEG_INF)
            ql = lax.broadcasted_iota(jnp.int32, (Q_BLOCK, SLC_NEAR), 0)
            kl = lax.broadcasted_iota(jnp.int32, (Q_BLOCK, SLC_NEAR), 1)
            dist = ql - kl + Q_BLOCK
            nbs_ref[g] = jnp.where(dist >= 0, _bias_from_scalars(_bucket(dist), tab), NEG_INF)
            ql = lax.broadcasted_iota(jnp.int32, (Q_BLOCK, WIN_KEYS), 0)
            kl = lax.broadcasted_iota(jnp.int32, (Q_BLOCK, WIN_KEYS), 1)
            dist = ql - kl + WINDOW
            ok = (dist >= 0) & (dist < WINDOW)
            nbw_ref[g] = jnp.where(ok, _bias_from_scalars(_bucket(dist), tab), NEG_INF)

    lane = lax.broadcasted_iota(jnp.int32, (Q_BLOCK, LANES), 1)
    lmask = (lane >= hk * HEAD_DIM) & (lane < (hk + 1) * HEAD_DIM)
    qblk = q_ref[0].astype(F32)
    qp = jnp.concatenate(
        [jnp.where(lmask, qblk[:, g * LANES:(g + 1) * LANES], 0.0) for g in range(GROUP)], axis=0)
    qp = (qp * (HEAD_DIM ** -0.5)).astype(BF16)

    far_bias = jnp.concatenate(
        [jnp.full((Q_BLOCK, 1), rb_ref[N_BUCKETS - 1, hk * GROUP + g], F32) for g in range(GROUP)], axis=0)

    n_cmp_pad = kcmp_ref.shape[1] - CMP_FRONT - (LANES - CMP_FRONT)
    kc_far = kcmp_ref[0, CMP_FRONT:CMP_FRONT + n_cmp_pad, :].astype(BF16)
    vc_far = vcmp_ref[0, CMP_FRONT:CMP_FRONT + n_cmp_pad, :].astype(BF16)
    near0 = pl.multiple_of(qb * (Q_BLOCK // CMP_STRIDE), 8)
    kc_near = kcmp_ref[0, pl.ds(near0, LANES), :].astype(BF16)
    vc_near = vcmp_ref[0, pl.ds(near0, LANES), :].astype(BF16)
    s_far = _dot_t(qp, kc_far) + far_bias
    ccol = lax.broadcasted_iota(jnp.int32, (gq, n_cmp_pad), 1)
    far_ok = ccol < qb * (Q_BLOCK // CMP_STRIDE) - CMP_FRONT
    s_far = jnp.where(far_ok, s_far, NEG_INF)
    s_near = _dot_t(qp, kc_near) + nbc_ref[...].reshape(gq, LANES)
    ncol = lax.broadcasted_iota(jnp.int32, (gq, LANES), 1)
    near_ok = (s_near > 0.5 * NEG_INF) & (ncol >= CMP_FRONT - qb * (Q_BLOCK // CMP_STRIDE))
    s_near = jnp.where(near_ok, s_near, NEG_INF)
    m = jnp.maximum(jnp.max(s_far, axis=-1, keepdims=True), jnp.max(s_near, axis=-1, keepdims=True))
    p_far = jnp.where(far_ok, jnp.exp(s_far - m), 0.0)
    p_near = jnp.where(near_ok, jnp.exp(s_near - m), 0.0)
    l = jnp.sum(p_far, axis=-1, keepdims=True) + jnp.sum(p_near, axis=-1, keepdims=True)
    inv = jnp.where(l > 0.0, 1.0 / jnp.where(l > 0.0, l, 1.0), 0.0)
    pc_far = p_far * inv
    pc_near = p_near * inv
    o_cmp = _dot(pc_far.astype(BF16), vc_far) + _dot(pc_near.astype(BF16), vc_near)

    ps_far = sum(pc_far[g * Q_BLOCK:(g + 1) * Q_BLOCK] for g in range(GROUP))
    ps_near = sum(pc_near[g * Q_BLOCK:(g + 1) * Q_BLOCK] for g in range(GROUP))
    sm_far = selmap_ref[CMP_FRONT:CMP_FRONT + n_cmp_pad, :].astype(BF16)
    sm_near = selmap_ref[pl.ds(near0, LANES), :].astype(BF16)
    imp = _split_dot(ps_far, sm_far) + _split_dot(ps_near, sm_near)
    imp_t = imp.T
    n_sel = imp_t.shape[0]
    blk = lax.broadcasted_iota(jnp.int32, (n_sel, Q_BLOCK), 0)
    cur = (s0 + lax.broadcasted_iota(jnp.int32, (n_sel, Q_BLOCK), 1)) // SEL_BLOCK
    forced = (blk == 0) | (blk == cur) | (blk == cur - 1)
    val = jnp.where(blk > cur, NEG_INF, imp_t + FORCE_BONUS * forced.astype(F32))
    sel_t = jnp.zeros((n_sel, Q_BLOCK), F32)
    for _ in range(N_SELECT):
        mx = jnp.max(val, axis=0, keepdims=True)
        first = jnp.min(jnp.where(val == mx, blk, n_sel), axis=0, keepdims=True)
        hit = blk == first
        sel_t = jnp.where(hit, 1.0, sel_t)
        val = jnp.where(hit, -3e38, val)
    sel = sel_t.T.astype(BF16)

    def key_mask(p0, width):
        hit = _dot(sel, e_ref[:, pl.ds(p0, width)]) > 0.5
        return _tile_rows(hit, GROUP)

    state = (jnp.full((gq, 1), NEG_INF, F32), jnp.zeros((gq, 1), F32), jnp.zeros((gq, LANES), F32))

    def far_step(kt, state):
        p0 = pl.multiple_of(Q_BLOCK + kt * SLC_TILE, LANES)
        k = ks_ref[0, pl.ds(p0, SLC_TILE), :]
        v = vs_ref[0, pl.ds(p0, SLC_TILE), :]
        s = _dot_t(qp, k) + far_bias
        key = kt * SLC_TILE + lax.broadcasted_iota(jnp.int32, (gq, SLC_TILE), 1)
        mask = key_mask(p0, SLC_TILE) & (key < s0 - Q_BLOCK)
        return _softmax_update(state, jnp.where(mask, s, NEG_INF), mask, v)

    n_far = (qb + 2) // (SLC_TILE // Q_BLOCK)
    state = lax.fori_loop(0, n_far, far_step, state)
    p0 = pl.multiple_of(s0, LANES)
    s = _dot_t(qp, ks_ref[0, pl.ds(p0, SLC_NEAR), :]) + nbs_ref[...].reshape(gq, SLC_NEAR)
    mask = key_mask(p0, SLC_NEAR) & (s > 0.5 * NEG_INF)
    state = _softmax_update(state, jnp.where(mask, s, NEG_INF), mask, vs_ref[0, pl.ds(p0, SLC_NEAR), :])
    o_slc = _softmax_finish(state)

    s = _dot_t(qp, kw_ref[0, pl.ds(p0, WIN_KEYS), :]) + nbw_ref[...].reshape(gq, WIN_KEYS)
    wcol = lax.broadcasted_iota(jnp.int32, (gq, WIN_KEYS), 1)
    mask = (s > 0.5 * NEG_INF) & (wcol >= WINDOW - s0)
    state = (jnp.full((gq, 1), NEG_INF, F32), jnp.zeros((gq, 1), F32), jnp.zeros((gq, LANES), F32))
    state = _softmax_update(state, jnp.where(mask, s, NEG_INF), mask, vw_ref[0, pl.ds(p0, WIN_KEYS), :])
    o_win = _softmax_finish(state)

    gs = jax.nn.sigmoid(gn_ref[0, 0])
    for g in range(GROUP):
        rows = slice(g * Q_BLOCK, (g + 1) * Q_BLOCK)
        o = (gs[:, 3 * g:3 * g + 1] * o_cmp[rows] + gs[:, 3 * g + 1:3 * g + 2] * o_slc[rows]
             + gs[:, 3 * g + 2:3 * g + 3] * o_win[rows])
        out_ref[0, 0, :, g * LANES:(g + 1) * LANES] = jnp.where(lmask, o, 0.0).astype(out_ref.dtype)


def _nsa_prompt(q, gn, kcmp, vcmp, kvb, rel_bias):
    b, t, _ = q.shape
    assert t % SLC_TILE == 0
    nqb = t // Q_BLOCK
    n_chunk = t // CMP_STRIDE
    n_cmp = n_chunk - CMP_RATIO + 1
    n_blocks = t // SEL_BLOCK
    assert n_blocks <= LANES
    cpad = ((0, 0), (CMP_FRONT, LANES - CMP_FRONT), (0, 0))
    kcmp_p = jnp.pad(kcmp, cpad)
    vcmp_p = jnp.pad(vcmp, cpad)
    selmap = jnp.asarray(np.pad(_sel_map(n_chunk, n_cmp, LANES, n_blocks),
                                ((CMP_FRONT, LANES - CMP_FRONT), (0, 0))), F32)
    ks_p = jnp.pad(kvb[:, :, 2 * KV_WIDTH:3 * KV_WIDTH], ((0, 0), (Q_BLOCK, 0), (0, 0)))
    vs_p = jnp.pad(kvb[:, :, 3 * KV_WIDTH:4 * KV_WIDTH], ((0, 0), (Q_BLOCK, 0), (0, 0)))
    kw_p = jnp.pad(kvb[:, :, 4 * KV_WIDTH:5 * KV_WIDTH], ((0, 0), (WINDOW, 0), (0, 0)))
    vw_p = jnp.pad(kvb[:, :, 5 * KV_WIDTH:6 * KV_WIDTH], ((0, 0), (WINDOW, 0), (0, 0)))
    pidx = np.arange(t + Q_BLOCK)
    e = (pidx[None, :] // SEL_BLOCK - Q_BLOCK // SEL_BLOCK == np.arange(LANES)[:, None])
    e = jnp.asarray(e.astype(np.float32), BF16)
    gn2 = jnp.stack([jnp.pad(gn[:, :, h * 3 * GROUP:(h + 1) * 3 * GROUP], ((0, 0), (0, 0), (0, LANES - 3 * GROUP)))
                     for h in range(N_KV_HEADS)], axis=1)
    full = lambda arr: pl.BlockSpec((1,) + arr.shape[1:], lambda bi, h, i: (bi, 0, 0))
    return pl.pallas_call(
        _nsa_prompt_kernel,
        grid=(b, N_KV_HEADS, nqb),
        in_specs=[pl.BlockSpec(memory_space=pltpu.SMEM),
                  pl.BlockSpec((1, Q_BLOCK, NSA_WIDTH), lambda bi, h, i: (bi, i, 0)),
                  pl.BlockSpec((1, 1, Q_BLOCK, LANES), lambda bi, h, i: (bi, h, i, 0)),
                  full(kcmp_p), full(vcmp_p),
                  pl.BlockSpec(selmap.shape, lambda bi, h, i: (0, 0)),
                  full(ks_p), full(vs_p), full(kw_p), full(vw_p),
                  pl.BlockSpec(e.shape, lambda bi, h, i: (0, 0))],
        out_specs=pl.BlockSpec((1, 1, Q_BLOCK, NSA_WIDTH), lambda bi, h, i: (h, bi, i, 0)),
        out_shape=jax.ShapeDtypeStruct((N_KV_HEADS, b, t, NSA_WIDTH), BF16),
        scratch_shapes=[pltpu.VMEM((GROUP, Q_BLOCK, LANES), F32),
                        pltpu.VMEM((GROUP, Q_BLOCK, SLC_NEAR), F32),
                        pltpu.VMEM((GROUP, Q_BLOCK, WIN_KEYS), F32)],
        compiler_params=_cparams(("arbitrary", "arbitrary", "arbitrary")),
        name="nsa_prompt",
    )(rel_bias, q, gn2, kcmp_p, vcmp_p, selmap, ks_p, vs_p, kw_p, vw_p, e)


ROWS_PER_TOKEN = 8


def _nsa_sample_cmp_kernel(q_ref, kcmp_ref, vcmp_ref, selmap_ref, rowtab_ref, ocmp_ref, idx_ref, *,
                           past_len, n_tok, n_cmp, n_blocks):
    rows = n_tok * ROWS_PER_TOKEN
    n_cpad = kcmp_ref.shape[1]
    kc = kcmp_ref[0].astype(BF16)
    vc = vcmp_ref[0].astype(BF16)
    ridx = lax.broadcasted_iota(jnp.int32, (rows, n_cpad), 0)
    ccol = lax.broadcasted_iota(jnp.int32, (rows, n_cpad), 1)
    dist = past_len + ridx // ROWS_PER_TOKEN - (CMP_STRIDE * ccol + CMP_LEN - 1)
    bucket = _bucket(dist)
    valid = (dist >= 0) & (ccol < n_cmp) & (ridx % ROWS_PER_TOKEN < GROUP)
    n_bpad = selmap_ref.shape[1]
    blk = lax.broadcasted_iota(jnp.int32, (ROWS_PER_TOKEN, n_bpad), 1)
    cur = (past_len + lax.broadcasted_iota(jnp.int32, (ROWS_PER_TOKEN, n_bpad), 0)) // SEL_BLOCK
    forced = (blk == 0) | (blk == cur) | (blk == cur - 1)
    olane = lax.broadcasted_iota(jnp.int32, (ROWS_PER_TOKEN, LANES), 1)
    for hk in range(N_KV_HEADS):
        qp = (q_ref[0, hk].astype(F32) * (HEAD_DIM ** -0.5)).astype(BF16)
        s = _dot_t(qp, kc) + _bias_from_rows(bucket, rowtab_ref[hk])
        s = jnp.where(valid, s, NEG_INF)
        m = jnp.max(s, axis=-1, keepdims=True)
        p = jnp.where(valid, jnp.exp(s - m), 0.0)
        l = jnp.sum(p, axis=-1, keepdims=True)
        pc = p * jnp.where(l > 0.0, 1.0 / jnp.where(l > 0.0, l, 1.0), 0.0)
        ocmp_ref[0, hk] = _dot(pc.astype(BF16), vc)
        ps = jnp.concatenate(
            [jnp.sum(pc[i * ROWS_PER_TOKEN:(i + 1) * ROWS_PER_TOKEN], axis=0, keepdims=True) for i in range(n_tok)]
            + [jnp.zeros((ROWS_PER_TOKEN - n_tok, n_cpad), F32)], axis=0)
        imp = _split_dot(ps, selmap_ref[...])
        val = jnp.where((blk > cur) | (blk >= n_blocks), NEG_INF, imp + FORCE_BONUS * forced.astype(F32))
        picks = jnp.zeros((ROWS_PER_TOKEN, LANES), jnp.int32)
        for n in range(N_SELECT):
            mx = jnp.max(val, axis=-1, keepdims=True)
            first = jnp.min(jnp.where(val == mx, blk, n_bpad), axis=-1, keepdims=True)
            picks = jnp.where(olane == n, first, picks)
            val = jnp.where(blk == first, -3e38, val)
        idx_ref[0, hk] = picks


def _nsa_sample_cmp(qs, kcmp, vcmp, rowtab, past_len, n_tok, n_cmp, n_blocks):
    nb = qs.shape[0]
    n_cpad = kcmp.shape[1]
    n_bpad = -(-n_blocks // LANES) * LANES
    selmap = jnp.asarray(_sel_map(n_cpad, n_cmp, n_bpad, n_blocks), BF16)
    rows = n_tok * ROWS_PER_TOKEN
    return pl.pallas_call(
        functools.partial(_nsa_sample_cmp_kernel, past_len=past_len, n_tok=n_tok, n_cmp=n_cmp, n_blocks=n_blocks),
        grid=(nb,),
        in_specs=[pl.BlockSpec((1, N_KV_HEADS, rows, LANES), lambda b: (b, 0, 0, 0)),
                  pl.BlockSpec((1, n_cpad, LANES), lambda b: (b, 0, 0)),
                  pl.BlockSpec((1, n_cpad, LANES), lambda b: (b, 0, 0)),
                  pl.BlockSpec(selmap.shape, lambda b: (0, 0)),
                  pl.BlockSpec(rowtab.shape, lambda b: (0, 0, 0))],
        out_specs=[pl.BlockSpec((1, N_KV_HEADS, rows, LANES), lambda b: (b, 0, 0, 0)),
                   pl.BlockSpec((1, N_KV_HEADS, ROWS_PER_TOKEN, LANES), lambda b: (b, 0, 0, 0))],
        out_shape=[jax.ShapeDtypeStruct((nb, N_KV_HEADS, rows, LANES), F32),
                   jax.ShapeDtypeStruct((nb, N_KV_HEADS, ROWS_PER_TOKEN, LANES), jnp.int32)],
        compiler_params=_cparams(("arbitrary",)),
        name="nsa_sample_cmp",
    )(qs, kcmp, vcmp, selmap, rowtab)


def _nsa_sample_slc_kernel(idx_ref, pt_ref, *refs, past_len, n_tok, n_cache_blocks):
    del pt_ref
    k_refs = refs[:N_SELECT]
    v_refs = refs[N_SELECT:2 * N_SELECT]
    (q_ref, ocmp_ref, gn_ref, rowtab_ref, newk_ref, newv_ref, swk_ref, swv_ref, nwk_ref, nwv_ref,
     out_ref) = refs[2 * N_SELECT:]
    b = pl.program_id(0)
    hk = pl.program_id(1)
    tok = pl.program_id(2)
    t = past_len + tok
    base = ((b * N_KV_HEADS + hk) * n_tok + tok) * N_SELECT
    qp = (q_ref[0, 0, 0].astype(F32) * (HEAD_DIM ** -0.5)).astype(BF16)
    rowtab = rowtab_ref[0]
    rvalid = lax.broadcasted_iota(jnp.int32, (ROWS_PER_TOKEN, 1), 0) < GROUP

    n_keys = N_SELECT * SEL_BLOCK
    lane = lax.broadcasted_iota(jnp.int32, (ROWS_PER_TOKEN, n_keys), 1)
    pos = lane % SEL_BLOCK
    ks, vs = [], []
    for n in range(N_SELECT):
        blk = idx_ref[base + n]
        is_new = blk >= n_cache_blocks
        ks.append(jnp.where(is_new, newk_ref[0], k_refs[n][0]).astype(BF16))
        vs.append(jnp.where(is_new, newv_ref[0], v_refs[n][0]).astype(BF16))
        pos = jnp.where(lane // SEL_BLOCK == n, pos + blk * SEL_BLOCK, pos)
    k = jnp.concatenate(ks, axis=0)
    v = jnp.concatenate(vs, axis=0)
    dist = t - pos
    s = _dot_t(qp, k) + _bias_from_rows(_bucket(dist), rowtab)
    mask = (dist >= 0) & rvalid
    state = (jnp.full((ROWS_PER_TOKEN, 1), NEG_INF, F32), jnp.zeros((ROWS_PER_TOKEN, 1), F32),
             jnp.zeros((ROWS_PER_TOKEN, LANES), F32))
    o_slc = _softmax_finish(_softmax_update(state, jnp.where(mask, s, NEG_INF), mask, v))

    n_win = swk_ref.shape[1]
    kw = jnp.concatenate([swk_ref[0], nwk_ref[0]], axis=0).astype(BF16)
    vw = jnp.concatenate([swv_ref[0], nwv_ref[0]], axis=0).astype(BF16)
    wl = n_win + nwk_ref.shape[1]
    wpos = past_len + n_tok - (n_win + n_tok) + lax.broadcasted_iota(jnp.int32, (ROWS_PER_TOKEN, wl), 1)
    wdist = t - wpos
    s = _dot_t(qp, kw) + _bias_from_rows(_bucket(wdist), rowtab)
    mask = (wdist >= 0) & (wdist < WINDOW) & (wpos >= 0) & rvalid
    o_win = _softmax_finish(_softmax_update(state, jnp.where(mask, s, NEG_INF), mask, vw))

    gs = jax.nn.sigmoid(gn_ref[0, 0, 0])
    out_ref[0, 0, 0] = gs[:, 0:1] * ocmp_ref[0, 0, 0] + gs[:, 1:2] * o_slc + gs[:, 2:3] * o_win


def _nsa_sample_slc(idx, page_table, cache_k, cache_v, q3, ocmp, gn3, rowtab, newk, newv, swk, swv, nwk, nwv,
                    past_len, n_tok):
    nb, npages = page_table.shape
    halves = PAGE // SEL_BLOCK
    n_cache_blocks = npages * halves
    ck = cache_k.reshape(-1, SEL_BLOCK, KV_WIDTH)
    cv = cache_v.reshape(-1, SEL_BLOCK, KV_WIDTH)

    def blk_spec(n):
        def imap(b, h, s, idx_ref, pt_ref):
            j = jnp.minimum(idx_ref[((b * N_KV_HEADS + h) * n_tok + s) * N_SELECT + n], n_cache_blocks - 1)
            return (pt_ref[b * npages + j // halves] * halves + j % halves, 0, 0)
        return pl.BlockSpec((1, SEL_BLOCK, KV_WIDTH), imap)

    tile = pl.BlockSpec((1, 1, 1, ROWS_PER_TOKEN, LANES), lambda b, h, s, i, p: (b, h, s, 0, 0))
    per_b = lambda arr: pl.BlockSpec((1,) + arr.shape[1:], lambda b, h, s, i, p: (b, 0, 0))
    grid_spec = pltpu.PrefetchScalarGridSpec(
        num_scalar_prefetch=2,
        grid=(nb, N_KV_HEADS, n_tok),
        in_specs=[blk_spec(n) for n in range(N_SELECT)] * 2
        + [tile, tile, tile,
           pl.BlockSpec((1, ROWS_PER_TOKEN, N_BUCKETS), lambda b, h, s, i, p: (h, 0, 0)),
           per_b(newk), per_b(newv), per_b(swk), per_b(swv), per_b(nwk), per_b(nwv)],
        out_specs=tile,
    )
    return pl.pallas_call(
        functools.partial(_nsa_sample_slc_kernel, past_len=past_len, n_tok=n_tok, n_cache_blocks=n_cache_blocks),
        grid_spec=grid_spec,
        out_shape=jax.ShapeDtypeStruct((nb, N_KV_HEADS, n_tok, ROWS_PER_TOKEN, LANES), F32),
        compiler_params=_cparams(("arbitrary", "arbitrary", "arbitrary")),
        name="nsa_sample_slc",
    )(idx.reshape(-1), page_table.reshape(-1), *([ck] * N_SELECT), *([cv] * N_SELECT),
      q3, ocmp, gn3, rowtab, newk, newv, swk, swv, nwk, nwv)


def _merge_tail(x, a, c, gates, wn_ref, wc_ref, wo_ref):
    ga = gates[:, :D_MODEL]
    gb = gates[:, D_MODEL:]
    m = (jax.nn.sigmoid(ga) * _dot(a, wn_ref[...])
         + jax.nn.sigmoid(gb) * _dot(c.astype(BF16), wc_ref[...]))
    return x + _dot(m.astype(BF16), wo_ref[...])


def _merge_prompt_kernel(x_ref, a_ref, conv_ref, halo_ref, gate_ref, cw_ref, cb_ref, wn_ref, wc_ref, wo_ref,
                         out_ref, cst_ref, *, tiles_per_batch):
    i = pl.program_id(0)
    first = i % tiles_per_batch == 0
    conv = conv_ref[...]
    bg = conv[:, :CONV_DIM]
    u = conv[:, CONV_DIM:2 * CONV_DIM] * conv[:, 2 * CONV_DIM:]
    halo = halo_ref[...]
    uh = jnp.where(first, 0.0, halo[:, CONV_DIM:2 * CONV_DIM] * halo[:, 2 * CONV_DIM:])
    tm = u.shape[0]
    row = lax.broadcasted_iota(jnp.int32, (tm, CONV_DIM), 0)
    u1 = jnp.where(row == 0, uh[7:8], pltpu.roll(u, 1, axis=0))
    u2 = pltpu.roll(u, 2, axis=0)
    u2 = jnp.where(row == 0, uh[6:7], jnp.where(row == 1, uh[7:8], u2))
    cw = cw_ref[...]
    y = ((cb_ref[...] + cw[0:1] * u2) + cw[1:2] * u1) + cw[2:3] * u
    c = bg * y
    a = a_ref[0] + a_ref[1]
    out_ref[...] = _merge_tail(x_ref[...], a, c, gate_ref[...], wn_ref, wc_ref, wo_ref)
    cst_ref[0] = u[tm - 8:, :]


def _merge_prompt(x2d, a2, conv3, gates, conv_w, conv_b, wn, wc, wo, t):
    n = x2d.shape[0]
    tm = min(512, t)
    assert t % tm == 0
    tpb = t // tm
    row = lambda w: pl.BlockSpec((tm, w), lambda i: (i, 0))
    const = lambda arr: pl.BlockSpec(arr.shape, lambda i: (0,) * arr.ndim)
    return pl.pallas_call(
        functools.partial(_merge_prompt_kernel, tiles_per_batch=tpb),
        grid=(n // tm,),
        in_specs=[row(D_MODEL),
                  pl.BlockSpec((N_KV_HEADS, tm, NSA_WIDTH), lambda i: (0, i, 0)),
                  row(3 * CONV_DIM),
                  pl.BlockSpec((8, 3 * CONV_DIM), lambda i: (jnp.maximum(i * (tm // 8) - 1, 0), 0)),
                  row(2 * D_MODEL),
                  const(conv_w), const(conv_b), const(wn), const(wc), const(wo)],
        out_specs=[row(D_MODEL), pl.BlockSpec((1, 8, CONV_DIM), lambda i: (i // tpb, 0, 0))],
        out_shape=[jax.ShapeDtypeStruct((n, D_MODEL), F32),
                   jax.ShapeDtypeStruct((n // t, 8, CONV_DIM), F32)],
        compiler_params=_cparams(("arbitrary",)),
        name="merge_prompt",
    )(x2d, a2, conv3, conv3, gates, conv_w, conv_b, wn, wc, wo)


def _merge_sample_kernel(x_ref, a_ref, conv_ref, past_ref, gate_ref, cw_ref, cb_ref, wn_ref, wc_ref, wo_ref,
                         out_ref, cst_ref, *, n_tok, nb):
    conv = conv_ref[...]
    bg = conv[:, :CONV_DIM]
    u = conv[:, CONV_DIM:2 * CONV_DIM] * conv[:, 2 * CONV_DIM:]
    up = jnp.concatenate([past_ref[...], u], axis=0)
    cw = cw_ref[...]
    y = cb_ref[...]
    for k in range(CONV_WIDTH):
        y = y + cw[k:k + 1] * up[k * nb:(k + n_tok) * nb]
    c = bg * y
    out_ref[...] = _merge_tail(x_ref[...], a_ref[...].astype(BF16), c, gate_ref[...], wn_ref, wc_ref, wo_ref)
    cst_ref[...] = up[n_tok * nb:]


def _merge_sample(x_tm, a_tm, conv_tm, past_tm, gates_tm, conv_w, conv_b, wn, wc, wo, n_tok, nb):
    n = x_tm.shape[0]
    full = lambda arr: pl.BlockSpec(arr.shape, lambda i: (0,) * arr.ndim)
    args = (x_tm, a_tm, conv_tm, past_tm, gates_tm, conv_w, conv_b, wn, wc, wo)
    return pl.pallas_call(
        functools.partial(_merge_sample_kernel, n_tok=n_tok, nb=nb),
        grid=(1,),
        in_specs=[full(a) for a in args],
        out_specs=[pl.BlockSpec((n, D_MODEL), lambda i: (0, 0)),
                   pl.BlockSpec(((CONV_WIDTH - 1) * nb, CONV_DIM), lambda i: (0, 0))],
        out_shape=[jax.ShapeDtypeStruct((n, D_MODEL), F32),
                   jax.ShapeDtypeStruct(((CONV_WIDTH - 1) * nb, CONV_DIM), F32)],
        compiler_params=_cparams(("arbitrary",)),
        name="merge_sample",
    )(*args)


def _rms(x, g):
    return (x * lax.rsqrt(jnp.mean(x * x, axis=-1, keepdims=True) + RMS_EPS)) * g


def _mlp_kernel(x_ref, g_ref, wu_ref, wd_ref, gf_ref, out_ref, h_ref, acc_ref):
    j = pl.program_id(1)

    @pl.when(j == 0)
    def _():
        h_ref[...] = _rms(x_ref[...], g_ref[...]).astype(BF16)

    up = jnp.square(jnp.maximum(_dot(h_ref[...], wu_ref[...]), 0.0)).astype(BF16)
    part = _dot(up, wd_ref[...])

    @pl.when(j == 0)
    def _():
        acc_ref[...] = part

    @pl.when(j > 0)
    def _():
        acc_ref[...] += part

    @pl.when(j == pl.num_programs(1) - 1)
    def _():
        out_ref[...] = _rms(x_ref[...] + acc_ref[...], gf_ref[...])


def _mlp(x2d, g_mlp, w_up, w_down, g_final):
    n = x2d.shape[0]
    tm = min(512, n)
    tf = 1024
    assert n % tm == 0 and D_FF % tf == 0
    return pl.pallas_call(
        _mlp_kernel,
        grid=(n // tm, D_FF // tf),
        in_specs=[pl.BlockSpec((tm, D_MODEL), lambda i, j: (i, 0)),
                  pl.BlockSpec((1, D_MODEL), lambda i, j: (0, 0)),
                  pl.BlockSpec((D_MODEL, tf), lambda i, j: (0, j)),
                  pl.BlockSpec((tf, D_MODEL), lambda i, j: (j, 0)),
                  pl.BlockSpec((1, D_MODEL), lambda i, j: (0, 0))],
        out_specs=pl.BlockSpec((tm, D_MODEL), lambda i, j: (i, 0)),
        out_shape=jax.ShapeDtypeStruct((n, D_MODEL), F32),
        scratch_shapes=[pltpu.VMEM((tm, D_MODEL), BF16), pltpu.VMEM((tm, D_MODEL), F32)],
        compiler_params=_cparams(("arbitrary", "arbitrary")),
        name="mlp",
    )(x2d, g_mlp, w_up, w_down, g_final)


def _kv_rows(z, lead):
    return z.reshape(lead + (N_KV_HEADS, HEAD_DIM))


def _layer(xp, xs, caches, states, page_table, w, rel_bias):
    (cache_cmp_k, cache_cmp_v, cache_slc_k, cache_slc_v) = caches
    (state_win_k, state_win_v, state_conv) = states
    nbp, t, _ = xp.shape
    nbs, n_tok, _ = xs.shape
    npages = page_table.shape[1]
    past_len = npages * PAGE
    assert t % PAGE == 0 and n_tok <= GROUP and state_win_k.shape[1] >= n_tok

    w_all = _prep_w_in(w["w_in"])
    g_attn = w["g_attn"].reshape(1, D_MODEL)
    wn = w["w_nsa_out"].reshape(N_KV_HEADS, GROUP, HEAD_DIM, D_MODEL).transpose(1, 0, 2, 3)
    wn = wn.reshape(NSA_WIDTH, D_MODEL).astype(BF16)
    wc = w["w_conv_out"].astype(BF16)
    wo = w["w_o"].astype(BF16)
    wu = w["w_up"].astype(BF16)
    wd = w["w_down"].astype(BF16)
    g_mlp = w["g_mlp"].reshape(1, D_MODEL)
    g_final = w["g_final"].reshape(1, D_MODEL)
    conv_w = w["conv_w"]
    conv_b = w["conv_b"].reshape(1, CONV_DIM)
    cmp_k = (w["cmp_pe_k"], w["cmp_w1_k"], w["cmp_w2_k"])
    cmp_v = (w["cmp_pe_v"], w["cmp_w1_v"], w["cmp_w2_v"])

    n_p = nbp * t
    (q, kc, vc, ks, vs, kw, vw, kvb, gn, conv3, gates) = _proj(xp.reshape(n_p, D_MODEL), g_attn, w_all)
    ident = jnp.arange(nbp * (t // PAGE), dtype=jnp.int32).reshape(nbp, t // PAGE)
    kcmp = _compress(kc.reshape(n_p // PAGE, PAGE, KV_WIDTH), 0, ident, *cmp_k)
    vcmp = _compress(vc.reshape(n_p // PAGE, PAGE, KV_WIDTH), 0, ident, *cmp_v)
    a2 = _nsa_prompt(q.reshape(nbp, t, NSA_WIDTH), gn.reshape(nbp, t, LANES), kcmp, vcmp,
                     kvb.reshape(nbp, t, 6 * KV_WIDTH), rel_bias)
    x1, cst = _merge_prompt(xp.reshape(n_p, D_MODEL), a2.reshape(N_KV_HEADS, n_p, NSA_WIDTH),
                            conv3, gates, conv_w, conv_b, wn, wc, wo, t)
    yp = _mlp(x1, g_mlp, wu, wd, g_final).reshape(nbp, t, D_MODEL)
    win_p = min(WINDOW, t)
    new_p = tuple(_kv_rows(z, (nbp, t)) for z in (kc, vc, ks, vs)) + (
        _kv_rows(kw, (nbp, t))[:, t - win_p:], _kv_rows(vw, (nbp, t))[:, t - win_p:],
        cst[:, 8 - (CONV_WIDTH - 1):, :])

    n_s = nbs * n_tok
    (qs, kcs, vcs, kss, vss, kws, vws, _, gns, conv3s, gatess) = _proj(xs.reshape(n_s, D_MODEL), g_attn, w_all)
    n_rows = past_len + n_tok
    n_chunk = n_rows // CMP_STRIDE
    assert n_chunk == past_len // CMP_STRIDE, "new rows must not complete a compression chunk"
    n_cmp = n_chunk - CMP_RATIO + 1
    n_blocks = -(-n_rows // SEL_BLOCK)
    kcmp_s = _compress(cache_cmp_k.reshape(-1, PAGE, KV_WIDTH), 0, page_table, *cmp_k)
    vcmp_s = _compress(cache_cmp_v.reshape(-1, PAGE, KV_WIDTH), 0, page_table, *cmp_v)

    lane = np.arange(LANES)
    q5 = qs.reshape(nbs, n_tok, GROUP, LANES)
    q3 = jnp.stack([jnp.where(jnp.asarray((lane // HEAD_DIM) == h), q5, jnp.zeros_like(q5))
                    for h in range(N_KV_HEADS)], axis=1)
    q3 = jnp.pad(q3, ((0, 0), (0, 0), (0, 0), (0, ROWS_PER_TOKEN - GROUP), (0, 0)))
    tab = rel_bias.T.reshape(N_KV_HEADS, GROUP, N_BUCKETS)
    rowtab3 = jnp.pad(tab, ((0, 0), (0, ROWS_PER_TOKEN - GROUP), (0, 0)))
    rowtab = jnp.tile(rowtab3, (1, n_tok, 1))
    ocmp, idx = _nsa_sample_cmp(q3.reshape(nbs, N_KV_HEADS, n_tok * ROWS_PER_TOKEN, LANES), kcmp_s, vcmp_s,
                                rowtab, past_len, n_tok, n_cmp, n_blocks)
    idx = idx[:, :, :n_tok, :N_SELECT]
    gn3 = gns[:, :3 * N_HEADS].reshape(nbs, n_tok, N_KV_HEADS, GROUP, 3).transpose(0, 2, 1, 3, 4)
    gn3 = jnp.pad(gn3, ((0, 0), (0, 0), (0, 0), (0, ROWS_PER_TOKEN - GROUP), (0, LANES - 3)))
    pad_rows = lambda z, r: jnp.pad(z.reshape(nbs, n_tok, KV_WIDTH), ((0, 0), (0, r - n_tok), (0, 0)))
    a_s = _nsa_sample_slc(idx, page_table, cache_slc_k, cache_slc_v, q3,
                          ocmp.reshape(nbs, N_KV_HEADS, n_tok, ROWS_PER_TOKEN, LANES), gn3, rowtab3,
                          pad_rows(kss, SEL_BLOCK), pad_rows(vss, SEL_BLOCK),
                          state_win_k.reshape(nbs, -1, KV_WIDTH), state_win_v.reshape(nbs, -1, KV_WIDTH),
                          pad_rows(kws, 8), pad_rows(vws, 8), past_len, n_tok)
    a_s = a_s[:, :, :, :GROUP, :]
    a_tm = jnp.stack([a_s[:, 0, :, :, :HEAD_DIM], a_s[:, 1, :, :, HEAD_DIM:]], axis=3)
    a_tm = a_tm.transpose(1, 0, 2, 3, 4).reshape(n_s, NSA_WIDTH)
    to_tm = lambda z: z.reshape(nbs, n_tok, -1).transpose(1, 0, 2).reshape(n_s, -1)
    past_tm = state_conv.transpose(1, 0, 2).reshape((CONV_WIDTH - 1) * nbs, CONV_DIM)
    x1s, csts = _merge_sample(to_tm(xs), a_tm, to_tm(conv3s), past_tm, to_tm(gatess),
                              conv_w, conv_b, wn, wc, wo, n_tok, nbs)
    ys = _mlp(x1s, g_mlp, wu, wd, g_final).reshape(n_tok, nbs, D_MODEL).transpose(1, 0, 2)
    win_buf = state_win_k.shape[1]
    kw_all = jnp.concatenate([state_win_k, _kv_rows(kws, (nbs, n_tok))], axis=1)[:, -win_buf:]
    vw_all = jnp.concatenate([state_win_v, _kv_rows(vws, (nbs, n_tok))], axis=1)[:, -win_buf:]
    new_s = tuple(_kv_rows(z, (nbs, n_tok)) for z in (kcs, vcs, kss, vss)) + (
        kw_all, vw_all, csts.reshape(CONV_WIDTH - 1, nbs, CONV_DIM).transpose(1, 0, 2))
    return yp, ys, new_p, new_s


def kernel(x_prompt, x_sample, cache_cmp_k, cache_cmp_v, cache_slc_k, cache_slc_v, state_win_k, state_win_v,
           state_conv, page_table, g_attn, w_in, cmp_pe_k, cmp_w1_k, cmp_w2_k, cmp_pe_v, cmp_w1_v, cmp_w2_v,
           conv_w, conv_b, w_nsa_out, w_conv_out, w_o, g_mlp, w_up, w_down, rel_bias, g_final):
    depth = w_in.shape[0]
    assert depth == 1, "the final norm is fused into the last layer's MLP kernel"
    names = ("g_attn", "w_in", "cmp_pe_k", "cmp_w1_k", "cmp_w2_k", "cmp_pe_v", "cmp_w1_v", "cmp_w2_v",
             "conv_w", "conv_b", "w_nsa_out", "w_conv_out", "w_o", "g_mlp", "w_up", "w_down")
    vals = (g_attn, w_in, cmp_pe_k, cmp_w1_k, cmp_w2_k, cmp_pe_v, cmp_w1_v, cmp_w2_v,
            conv_w, conv_b, w_nsa_out, w_conv_out, w_o, g_mlp, w_up, w_down)
    l = 0
    w = {k: v[l] for k, v in zip(names, vals)}
    w["g_final"] = g_final
    yp, ys, new_p, new_s = _layer(
        x_prompt, x_sample,
        (cache_cmp_k[l], cache_cmp_v[l], cache_slc_k[l], cache_slc_v[l]),
        (state_win_k[l], state_win_v[l], state_conv[l]), page_table, w, rel_bias)
    return (yp, ys) + tuple(z[None] for z in new_p) + tuple(z[None] for z in new_s)
```

```python
import functools
import math

import numpy as np
import jax
import jax.numpy as jnp
from jax import lax
from jax.experimental import pallas as pl
from jax.experimental.pallas import tpu as pltpu

F32 = jnp.float32
BF16 = jnp.bfloat16

D_MODEL = 1024
N_HEADS = 8
HEAD_DIM = 64
N_KV_HEADS = 2
GROUP = N_HEADS // N_KV_HEADS
KV_WIDTH = N_KV_HEADS * HEAD_DIM
NSA_WIDTH = N_HEADS * HEAD_DIM
CMP_LEN = 32
CMP_STRIDE = 16
CMP_RATIO = CMP_LEN // CMP_STRIDE
CMP_HIDDEN = 256
SEL_BLOCK = 64
N_SELECT = 16
WINDOW = 512
CONV_DIM = 512
CONV_WIDTH = 3
D_FF = 4 * D_MODEL
N_BUCKETS = 32
MAX_DISTANCE = 128
Q_BLOCK = 128
PAGE = 128
RMS_EPS = 1e-6
NEG_INF = -1e30
FORCE_BONUS = 1e4
LANES = 128
VMEM_LIMIT = 56 * 1024 * 1024


def _bucket_thresholds():
    max_exact = N_BUCKETS // 2
    n = np.arange(max_exact, 4 * MAX_DISTANCE).astype(np.float32)
    large = max_exact + (np.log(n / np.float32(max_exact)) / np.float32(math.log(MAX_DISTANCE / max_exact))
                         * np.float32(N_BUCKETS - max_exact)).astype(np.int32)
    large = np.minimum(large, N_BUCKETS - 1)
    thr = []
    for k in range(max_exact + 1, N_BUCKETS):
        thr.append(int(np.arange(max_exact, 4 * MAX_DISTANCE)[np.argmax(large >= k)]))
    return tuple(thr)


BUCKET_THR = _bucket_thresholds()


def _bucket(dist):
    n = jnp.maximum(dist, 0)
    big = jnp.full(n.shape, N_BUCKETS // 2, jnp.int32)
    for thr in BUCKET_THR:
        big = big + (n >= thr).astype(jnp.int32)
    return jnp.where(n < N_BUCKETS // 2, n, big)


def _bias_from_scalars(bucket, table_fn):
    out = jnp.zeros(bucket.shape, F32)
    for k in range(N_BUCKETS):
        out = jnp.where(bucket == k, table_fn(k), out)
    return out


def _bias_from_rows(bucket, rowtab):
    out = jnp.zeros(bucket.shape, F32)
    for k in range(N_BUCKETS):
        out = jnp.where(bucket == k, rowtab[:, k:k + 1], out)
    return out


def _dot(a, b):
    return jnp.dot(a, b, preferred_element_type=F32)


def _dot_t(a, b):
    return lax.dot_general(a, b, (((1,), (1,)), ((), ())), preferred_element_type=F32)


def _split_dot(p, m):
    hi = p.astype(BF16)
    lo = (p - hi.astype(F32)).astype(BF16)
    return _dot(hi, m) + _dot(lo, m)


def _cparams(sem):
    return pltpu.CompilerParams(dimension_semantics=sem, vmem_limit_bytes=VMEM_LIMIT)


PROJ_WIDTHS = (NSA_WIDTH, 6 * KV_WIDTH, LANES, 3 * CONV_DIM, 2 * D_MODEL)
PROJ_COLS = sum(PROJ_WIDTHS)


def _proj_kernel(x_ref, g_ref, w_ref, q_ref, kc_ref, vc_ref, ks_ref, vs_ref, kw_ref, vw_ref,
                 kvb_ref, gn_ref, conv_ref, gate_ref):
    x = x_ref[...]
    y = x * lax.rsqrt(jnp.mean(x * x, axis=-1, keepdims=True) + RMS_EPS)
    hb = (y * g_ref[...]).astype(BF16)
    q_ref[...] = _dot(hb, w_ref[:, 0:NSA_WIDTH]).astype(q_ref.dtype)
    off = NSA_WIDTH
    kv = _dot(hb, w_ref[:, off:off + 6 * KV_WIDTH])
    for i, r in enumerate((kc_ref, vc_ref, ks_ref, vs_ref, kw_ref, vw_ref)):
        r[...] = kv[:, i * KV_WIDTH:(i + 1) * KV_WIDTH]
    kvb_ref[...] = kv.astype(BF16)
    off += 6 * KV_WIDTH
    gn_ref[...] = _dot(hb, w_ref[:, off:off + LANES])
    off += LANES
    conv_ref[...] = _dot(hb, w_ref[:, off:off + 3 * CONV_DIM])
    off += 3 * CONV_DIM
    gate_ref[...] = _dot(hb, w_ref[:, off:off + 2 * D_MODEL])


def _proj(x2d, g, w_all):
    n = x2d.shape[0]
    tm = min(256, n)
    assert n % tm == 0
    row = lambda w: pl.BlockSpec((tm, w), lambda i: (i, 0))
    out_shapes = ([jax.ShapeDtypeStruct((n, NSA_WIDTH), BF16)]
                  + [jax.ShapeDtypeStruct((n, KV_WIDTH), F32)] * 6
                  + [jax.ShapeDtypeStruct((n, 6 * KV_WIDTH), BF16),
                     jax.ShapeDtypeStruct((n, LANES), F32),
                     jax.ShapeDtypeStruct((n, 3 * CONV_DIM), F32),
                     jax.ShapeDtypeStruct((n, 2 * D_MODEL), F32)])
    out_specs = ([row(NSA_WIDTH)] + [row(KV_WIDTH)] * 6
                 + [row(6 * KV_WIDTH), row(LANES), row(3 * CONV_DIM), row(2 * D_MODEL)])
    return pl.pallas_call(
        _proj_kernel,
        grid=(n // tm,),
        in_specs=[row(D_MODEL),
                  pl.BlockSpec((1, D_MODEL), lambda i: (0, 0)),
                  pl.BlockSpec((D_MODEL, PROJ_COLS), lambda i: (0, 0))],
        out_specs=out_specs,
        out_shape=out_shapes,
        compiler_params=_cparams(("arbitrary",)),
        name="proj",
    )(x2d, g, w_all)


def _prep_w_in(w_in):
    sizes = (NSA_WIDTH,) + (KV_WIDTH,) * 6 + (3 * N_HEADS, CONV_DIM, CONV_DIM, CONV_DIM, D_MODEL, D_MODEL)
    offs = np.concatenate([[0], np.cumsum(sizes)])
    wq = w_in[:, :NSA_WIDTH].reshape(D_MODEL, N_KV_HEADS, GROUP, HEAD_DIM)
    wq = wq.transpose(0, 2, 1, 3).reshape(D_MODEL, NSA_WIDTH)
    wkv = w_in[:, offs[1]:offs[7]]
    wgn = jnp.pad(w_in[:, offs[7]:offs[8]], ((0, 0), (0, LANES - 3 * N_HEADS)))
    wrest = w_in[:, offs[8]:]
    return jnp.concatenate([wq, wkv, wgn, wrest], axis=1).astype(BF16)


CHUNKS_PER_PAGE = PAGE // CMP_STRIDE


def _compress_kernel(pt_ref, *refs, pp):
    del pt_ref
    page_refs = refs[:pp + 1]
    wbd_ref, pe_ref, w1_ref, w2_ref, out_ref, acc_ref, pehid_ref = refs[pp + 1:]

    @pl.when((pl.program_id(0) == 0) & (pl.program_id(1) == 0))
    def _():
        pehid_ref[...] = _dot(pe_ref[...], w1_ref[...])

    cols = []
    for r in range(CMP_STRIDE):
        xr = jnp.concatenate([p[0, pl.ds(r, CHUNKS_PER_PAGE, stride=CMP_STRIDE), :] for p in page_refs], axis=0)
        cols.append(xr.astype(BF16))
    x = jnp.concatenate(cols, axis=1)
    acc_ref[...] = _dot(x, wbd_ref[...])
    nc = CHUNKS_PER_PAGE * pp
    pe = pehid_ref[0:1, :]
    hids = []
    for h in range(N_KV_HEADS):
        base = h * CMP_RATIO * CMP_HIDDEN
        p0 = acc_ref[0:nc, base:base + CMP_HIDDEN]
        p1 = acc_ref[pl.ds(1, nc), base + CMP_HIDDEN:base + 2 * CMP_HIDDEN]
        hids.append(jax.nn.gelu((pe + p0) + p1).astype(BF16))
    out_ref[0] = _dot(jnp.concatenate(hids, axis=1), w2_ref[...])


def _compress(pages, colblk, page_table, pe, w1, w2):
    nb, npages = page_table.shape
    pp = min(32, npages)
    assert npages % pp == 0
    nc = CHUNKS_PER_PAGE * pp
    w1r = w1.reshape(CMP_RATIO, CMP_STRIDE, HEAD_DIM, CMP_HIDDEN)
    blk = w1r.transpose(1, 2, 0, 3).reshape(CMP_STRIDE, HEAD_DIM, CMP_RATIO * CMP_HIDDEN)
    z = jnp.zeros_like(blk)
    wbd = jnp.concatenate([jnp.concatenate([blk, z], axis=2), jnp.concatenate([z, blk], axis=2)], axis=1)
    wbd = wbd.reshape(CMP_STRIDE * KV_WIDTH, N_KV_HEADS * CMP_RATIO * CMP_HIDDEN).astype(BF16)
    z2 = jnp.zeros_like(w2)
    w2bd = jnp.concatenate([jnp.concatenate([w2, z2], axis=1), jnp.concatenate([z2, w2], axis=1)], axis=0).astype(BF16)
    pe8 = jnp.broadcast_to(pe.reshape(1, CMP_LEN * HEAD_DIM), (8, CMP_LEN * HEAD_DIM)).astype(BF16)

    def page_spec(p):
        def imap(b, s, pt):
            return (pt[b, jnp.minimum(s * pp + p, npages - 1)], 0, colblk)
        return pl.BlockSpec((1, PAGE, KV_WIDTH), imap)

    const = lambda shape: pl.BlockSpec(shape, lambda b, s, pt: (0,) * len(shape))
    grid_spec = pltpu.PrefetchScalarGridSpec(
        num_scalar_prefetch=1,
        grid=(nb, npages // pp),
        in_specs=[page_spec(p) for p in range(pp + 1)]
        + [const(wbd.shape), const(pe8.shape), const((CMP_LEN * HEAD_DIM, CMP_HIDDEN)), const(w2bd.shape)],
        out_specs=pl.BlockSpec((1, nc, KV_WIDTH), lambda b, s, pt: (b, s, 0)),
        scratch_shapes=[pltpu.VMEM((nc + 8, N_KV_HEADS * CMP_RATIO * CMP_HIDDEN), F32),
                        pltpu.VMEM((8, CMP_HIDDEN), F32)],
    )
    return pl.pallas_call(
        functools.partial(_compress_kernel, pp=pp),
        grid_spec=grid_spec,
        out_shape=jax.ShapeDtypeStruct((nb, npages * CHUNKS_PER_PAGE, KV_WIDTH), F32),
        compiler_params=_cparams(("arbitrary", "arbitrary")),
        name="compress",
    )(page_table, *([pages] * (pp + 1)), wbd, pe8, w1.astype(BF16), w2bd)


def _sel_map(n_rows, n_cmp, n_cols, n_blocks):
    c0 = np.arange(n_rows)[:, None] * CMP_STRIDE
    s0 = np.arange(n_cols)[None, :] * SEL_BLOCK
    ov = np.minimum(c0 + CMP_LEN, s0 + SEL_BLOCK) - np.maximum(c0, s0)
    m = np.clip(ov, 0, None).astype(np.float32) / CMP_LEN
    m[n_cmp:] = 0.0
    m[:, n_blocks:] = 0.0
    return m


CMP_FRONT = 16
CMP_NEAR = 24
SLC_NEAR = 2 * Q_BLOCK
SLC_TILE = 512
WIN_KEYS = WINDOW + Q_BLOCK


def _softmax_update(state, s, v, mask=None, v_transposed=False):
    m_old, l_old, acc = state
    m_new = jnp.maximum(m_old, jnp.max(s, axis=-1, keepdims=True))
    p = jnp.exp(s - m_new)
    if mask is not None:
        p = jnp.where(mask, p, 0.0)
    alpha = jnp.exp(m_old - m_new)
    l_new = alpha * l_old + jnp.sum(p, axis=-1, keepdims=True)
    pv = _dot_t(p.astype(BF16), v) if v_transposed else _dot(p.astype(BF16), v)
    return m_new, l_new, alpha * acc + pv


def _softmax_finish(state):
    _, l, acc = state
    return jnp.where(l > 0.0, acc / jnp.where(l > 0.0, l, 1.0), 0.0)


def _tile_rows(x, n):
    return jnp.concatenate([x] * n, axis=0)


def _nsa_prompt_kernel(rb_ref, q_ref, gn_ref, kcmp_ref, vcmp_ref, selmap_ref, ks_ref, vs_ref, kw_ref, vw_ref,
                       et_ref, out_ref, nbc_ref, nbs_ref, nbw_ref):
    hk = pl.program_id(1)
    qb = pl.program_id(2)
    s0 = qb * Q_BLOCK
    gq = GROUP * Q_BLOCK

    @pl.when(qb == 0)
    def _():
        for g in range(GROUP):
            tab = lambda k, g=g: rb_ref[k, hk * GROUP + g]
            ql = lax.broadcasted_iota(jnp.int32, (Q_BLOCK, LANES), 0)
            cl = lax.broadcasted_iota(jnp.int32, (Q_BLOCK, LANES), 1)
            dist = ql - CMP_STRIDE * (cl - CMP_FRONT) - (CMP_LEN - 1)
            ok = (dist >= 0) & (cl < CMP_NEAR)
            nbc_ref[g] = jnp.where(ok, _bias_from_scalars(_bucket(dist), tab), NEG_INF)
            ql = lax.broadcasted_iota(jnp.int32, (Q_BLOCK, SLC_NEAR), 0)
            kl = lax.broadcasted_iota(jnp.int32, (Q_BLOCK, SLC_NEAR), 1)
            dist = ql - kl + Q_BLOCK
            nbs_ref[g] = jnp.where(dist >= 0, _bias_from_scalars(_bucket(dist), tab), NEG_INF)
            ql = lax.broadcasted_iota(jnp.int32, (Q_BLOCK, WIN_KEYS), 0)
            kl = lax.broadcasted_iota(jnp.int32, (Q_BLOCK, WIN_KEYS), 1)
            dist = ql - kl + WINDOW
            ok = (dist >= 0) & (dist < WINDOW)
            nbw_ref[g] = jnp.where(ok, _bias_from_scalars(_bucket(dist), tab), NEG_INF)

    lane = lax.broadcasted_iota(jnp.int32, (Q_BLOCK, LANES), 1)
    lmask = (lane >= hk * HEAD_DIM) & (lane < (hk + 1) * HEAD_DIM)
    qblk = q_ref[0].astype(F32)
    qp = jnp.concatenate(
        [jnp.where(lmask, qblk[:, g * LANES:(g + 1) * LANES], 0.0) for g in range(GROUP)], axis=0)
    qp = (qp * (HEAD_DIM ** -0.5)).astype(BF16)

    far_bias = jnp.concatenate(
        [jnp.full((Q_BLOCK, 1), rb_ref[N_BUCKETS - 1, hk * GROUP + g], F32) for g in range(GROUP)], axis=0)

    n_cmp_pad = kcmp_ref.shape[1] - CMP_FRONT - (LANES - CMP_FRONT)
    kc_far = kcmp_ref[0, CMP_FRONT:CMP_FRONT + n_cmp_pad, :].astype(BF16)
    vc_far = vcmp_ref[0, CMP_FRONT:CMP_FRONT + n_cmp_pad, :].astype(BF16)
    near0 = pl.multiple_of(qb * (Q_BLOCK // CMP_STRIDE), 8)
    kc_near = kcmp_ref[0, pl.ds(near0, LANES), :].astype(BF16)
    vc_near = vcmp_ref[0, pl.ds(near0, LANES), :].astype(BF16)
    c8 = qb * (Q_BLOCK // CMP_STRIDE)
    ccol = lax.broadcasted_iota(jnp.int32, (1, n_cmp_pad), 1)
    s_far = _dot_t(qp, kc_far) + jnp.where(ccol < c8 - CMP_FRONT, 0.0, NEG_INF)
    ncol = lax.broadcasted_iota(jnp.int32, (1, LANES), 1)
    s_near = _dot_t(qp, kc_near) + (nbc_ref[...].reshape(gq, LANES)
                                    + jnp.where(ncol >= CMP_FRONT - c8, 0.0, NEG_INF))
    m = jnp.maximum(jnp.max(s_far, axis=-1, keepdims=True) + far_bias, jnp.max(s_near, axis=-1, keepdims=True))
    p_far = jnp.exp(s_far - (m - far_bias))
    p_near = jnp.exp(s_near - m)
    l = jnp.sum(p_far, axis=-1, keepdims=True) + jnp.sum(p_near, axis=-1, keepdims=True)
    inv = jnp.where(m > 0.5 * NEG_INF, 1.0 / l, 0.0)
    pc_far = p_far * inv
    pc_near = p_near * inv
    o_cmp = _dot(pc_far.astype(BF16), vc_far) + _dot(pc_near.astype(BF16), vc_near)

    ps_far = sum(pc_far[g * Q_BLOCK:(g + 1) * Q_BLOCK] for g in range(GROUP))
    ps_near = sum(pc_near[g * Q_BLOCK:(g + 1) * Q_BLOCK] for g in range(GROUP))
    sm_far = selmap_ref[CMP_FRONT:CMP_FRONT + n_cmp_pad, :].astype(BF16)
    sm_near = selmap_ref[pl.ds(near0, LANES), :].astype(BF16)
    imp = _split_dot(ps_far, sm_far) + _split_dot(ps_near, sm_near)
    imp_t = imp.T
    n_sel = imp_t.shape[0]
    blk = lax.broadcasted_iota(jnp.int32, (n_sel, Q_BLOCK), 0)
    cur = (s0 + lax.broadcasted_iota(jnp.int32, (n_sel, Q_BLOCK), 1)) // SEL_BLOCK
    forced = (blk == 0) | (blk == cur) | (blk == cur - 1)
    val = jnp.where(blk > cur, NEG_INF, imp_t + FORCE_BONUS * forced.astype(F32))
    sel_t = jnp.zeros((n_sel, Q_BLOCK), F32)
    for _ in range(N_SELECT):
        mx = jnp.max(val, axis=0, keepdims=True)
        first = jnp.min(jnp.where(val == mx, blk, n_sel), axis=0, keepdims=True)
        hit = blk == first
        sel_t = jnp.where(hit, 1.0, sel_t)
        val = jnp.where(hit, -3e38, val)
    sel = sel_t.T > 0.5

    jcol = lax.broadcasted_iota(jnp.int32, (Q_BLOCK, n_sel), 1)
    far_blocks = (s0 - Q_BLOCK) // SEL_BLOCK
    pen_far = jnp.where(sel & (jcol < far_blocks), 0.0, NEG_INF).astype(BF16)
    pen_near = jnp.where(sel, 0.0, NEG_INF).astype(BF16)
    q_far = jnp.concatenate([qp, _tile_rows(pen_far, GROUP)], axis=1)
    q_near = jnp.concatenate([qp, _tile_rows(pen_near, GROUP)], axis=1)
    init = (jnp.full((gq, 1), NEG_INF, F32), jnp.zeros((gq, 1), F32), jnp.zeros((gq, LANES), F32))

    def far_step(kt, state):
        p0 = pl.multiple_of(Q_BLOCK + kt * SLC_TILE, LANES)
        k = jnp.concatenate([ks_ref[0, pl.ds(p0, SLC_TILE), :], et_ref[pl.ds(p0, SLC_TILE), :]], axis=1)
        return _softmax_update(state, _dot_t(q_far, k), vs_ref[0, pl.ds(p0, SLC_TILE), :])

    n_far = (qb + 2) // (SLC_TILE // Q_BLOCK)
    m_far, l_far, acc_far = lax.fori_loop(0, n_far, far_step, init)
    state = (m_far + far_bias, l_far, acc_far)
    p0 = pl.multiple_of(s0, LANES)
    k = jnp.concatenate([ks_ref[0, pl.ds(p0, SLC_NEAR), :], et_ref[pl.ds(p0, SLC_NEAR), :]], axis=1)
    kcol = lax.broadcasted_iota(jnp.int32, (1, SLC_NEAR), 1)
    s = _dot_t(q_near, k) + (nbs_ref[...].reshape(gq, SLC_NEAR) + jnp.where(kcol >= Q_BLOCK - s0, 0.0, NEG_INF))
    o_slc = _softmax_finish(_softmax_update(state, s, vs_ref[0, pl.ds(p0, SLC_NEAR), :]))

    wcol = lax.broadcasted_iota(jnp.int32, (1, WIN_KEYS), 1)
    s = _dot_t(qp, kw_ref[0, pl.ds(p0, WIN_KEYS), :]) + (nbw_ref[...].reshape(gq, WIN_KEYS)
                                                         + jnp.where(wcol >= WINDOW - s0, 0.0, NEG_INF))
    o_win = _softmax_finish(_softmax_update(init, s, vw_ref[0, pl.ds(p0, WIN_KEYS), :]))

    gs = jax.nn.sigmoid(gn_ref[0, 0])
    for g in range(GROUP):
        rows = slice(g * Q_BLOCK, (g + 1) * Q_BLOCK)
        o = (gs[:, 3 * g:3 * g + 1] * o_cmp[rows] + gs[:, 3 * g + 1:3 * g + 2] * o_slc[rows]
             + gs[:, 3 * g + 2:3 * g + 3] * o_win[rows])
        out_ref[0, 0, :, g * LANES:(g + 1) * LANES] = jnp.where(lmask, o, 0.0).astype(out_ref.dtype)


def _nsa_prompt(q, gn, kcmp, vcmp, kvb, rel_bias):
    b, t, _ = q.shape
    assert t % SLC_TILE == 0
    nqb = t // Q_BLOCK
    n_chunk = t // CMP_STRIDE
    n_cmp = n_chunk - CMP_RATIO + 1
    n_blocks = t // SEL_BLOCK
    assert n_blocks <= LANES
    cpad = ((0, 0), (CMP_FRONT, LANES - CMP_FRONT), (0, 0))
    kcmp_p = jnp.pad(kcmp, cpad)
    vcmp_p = jnp.pad(vcmp, cpad)
    selmap = jnp.asarray(np.pad(_sel_map(n_chunk, n_cmp, LANES, n_blocks),
                                ((CMP_FRONT, LANES - CMP_FRONT), (0, 0))), F32)
    ks_p = jnp.pad(kvb[:, :, 2 * KV_WIDTH:3 * KV_WIDTH], ((0, 0), (Q_BLOCK, 0), (0, 0)))
    vs_p = jnp.pad(kvb[:, :, 3 * KV_WIDTH:4 * KV_WIDTH], ((0, 0), (Q_BLOCK, 0), (0, 0)))
    kw_p = jnp.pad(kvb[:, :, 4 * KV_WIDTH:5 * KV_WIDTH], ((0, 0), (WINDOW, 0), (0, 0)))
    vw_p = jnp.pad(kvb[:, :, 5 * KV_WIDTH:6 * KV_WIDTH], ((0, 0), (WINDOW, 0), (0, 0)))
    pidx = np.arange(t + Q_BLOCK)
    e = (pidx[:, None] // SEL_BLOCK - Q_BLOCK // SEL_BLOCK == np.arange(LANES)[None, :])
    e = jnp.asarray(e.astype(np.float32), BF16)
    gn2 = jnp.stack([jnp.pad(gn[:, :, h * 3 * GROUP:(h + 1) * 3 * GROUP], ((0, 0), (0, 0), (0, LANES - 3 * GROUP)))
                     for h in range(N_KV_HEADS)], axis=1)
    full = lambda arr: pl.BlockSpec((1,) + arr.shape[1:], lambda bi, h, i: (bi, 0, 0))
    return pl.pallas_call(
        _nsa_prompt_kernel,
        grid=(b, N_KV_HEADS, nqb),
        in_specs=[pl.BlockSpec(memory_space=pltpu.SMEM),
                  pl.BlockSpec((1, Q_BLOCK, NSA_WIDTH), lambda bi, h, i: (bi, i, 0)),
                  pl.BlockSpec((1, 1, Q_BLOCK, LANES), lambda bi, h, i: (bi, h, i, 0)),
                  full(kcmp_p), full(vcmp_p),
                  pl.BlockSpec(selmap.shape, lambda bi, h, i: (0, 0)),
                  full(ks_p), full(vs_p), full(kw_p), full(vw_p),
                  pl.BlockSpec(e.shape, lambda bi, h, i: (0, 0))],
        out_specs=pl.BlockSpec((1, 1, Q_BLOCK, NSA_WIDTH), lambda bi, h, i: (h, bi, i, 0)),
        out_shape=jax.ShapeDtypeStruct((N_KV_HEADS, b, t, NSA_WIDTH), BF16),
        scratch_shapes=[pltpu.VMEM((GROUP, Q_BLOCK, LANES), F32),
                        pltpu.VMEM((GROUP, Q_BLOCK, SLC_NEAR), F32),
                        pltpu.VMEM((GROUP, Q_BLOCK, WIN_KEYS), F32)],
        compiler_params=_cparams(("arbitrary", "arbitrary", "arbitrary")),
        name="nsa_prompt",
    )(rel_bias, q, gn2, kcmp_p, vcmp_p, selmap, ks_p, vs_p, kw_p, vw_p, e)


ROWS_PER_TOKEN = 8


def _nsa_sample_cmp_kernel(q_ref, kcmp_ref, vcmp_ref, selmap_ref, rowtab_ref, ocmp_ref, idx_ref, *,
                           past_len, n_tok, n_cmp, n_blocks):
    rows = n_tok * ROWS_PER_TOKEN
    n_cpad = kcmp_ref.shape[1]
    kc = kcmp_ref[0].astype(BF16)
    vc = vcmp_ref[0].astype(BF16)
    ridx = lax.broadcasted_iota(jnp.int32, (rows, n_cpad), 0)
    ccol = lax.broadcasted_iota(jnp.int32, (rows, n_cpad), 1)
    dist = past_len + ridx // ROWS_PER_TOKEN - (CMP_STRIDE * ccol + CMP_LEN - 1)
    bucket = _bucket(dist)
    valid = (dist >= 0) & (ccol < n_cmp) & (ridx % ROWS_PER_TOKEN < GROUP)
    n_bpad = selmap_ref.shape[1]
    blk = lax.broadcasted_iota(jnp.int32, (ROWS_PER_TOKEN, n_bpad), 1)
    cur = (past_len + lax.broadcasted_iota(jnp.int32, (ROWS_PER_TOKEN, n_bpad), 0)) // SEL_BLOCK
    forced = (blk == 0) | (blk == cur) | (blk == cur - 1)
    olane = lax.broadcasted_iota(jnp.int32, (ROWS_PER_TOKEN, LANES), 1)
    for hk in range(N_KV_HEADS):
        qp = (q_ref[0, hk].astype(F32) * (HEAD_DIM ** -0.5)).astype(BF16)
        s = _dot_t(qp, kc) + _bias_from_rows(bucket, rowtab_ref[hk])
        s = jnp.where(valid, s, NEG_INF)
        m = jnp.max(s, axis=-1, keepdims=True)
        p = jnp.where(valid, jnp.exp(s - m), 0.0)
        l = jnp.sum(p, axis=-1, keepdims=True)
        pc = p * jnp.where(l > 0.0, 1.0 / jnp.where(l > 0.0, l, 1.0), 0.0)
        ocmp_ref[0, hk] = _dot(pc.astype(BF16), vc)
        ps = jnp.concatenate(
            [jnp.sum(pc[i * ROWS_PER_TOKEN:(i + 1) * ROWS_PER_TOKEN], axis=0, keepdims=True) for i in range(n_tok)]
            + [jnp.zeros((ROWS_PER_TOKEN - n_tok, n_cpad), F32)], axis=0)
        imp = _split_dot(ps, selmap_ref[...])
        val = jnp.where((blk > cur) | (blk >= n_blocks), NEG_INF, imp + FORCE_BONUS * forced.astype(F32))
        picks = jnp.zeros((ROWS_PER_TOKEN, LANES), jnp.int32)
        for n in range(N_SELECT):
            mx = jnp.max(val, axis=-1, keepdims=True)
            first = jnp.min(jnp.where(val == mx, blk, n_bpad), axis=-1, keepdims=True)
            picks = jnp.where(olane == n, first, picks)
            val = jnp.where(blk == first, -3e38, val)
        idx_ref[0, hk] = picks


def _nsa_sample_cmp(qs, kcmp, vcmp, rowtab, past_len, n_tok, n_cmp, n_blocks):
    nb = qs.shape[0]
    n_cpad = kcmp.shape[1]
    n_bpad = -(-n_blocks // LANES) * LANES
    selmap = jnp.asarray(_sel_map(n_cpad, n_cmp, n_bpad, n_blocks), BF16)
    rows = n_tok * ROWS_PER_TOKEN
    return pl.pallas_call(
        functools.partial(_nsa_sample_cmp_kernel, past_len=past_len, n_tok=n_tok, n_cmp=n_cmp, n_blocks=n_blocks),
        grid=(nb,),
        in_specs=[pl.BlockSpec((1, N_KV_HEADS, rows, LANES), lambda b: (b, 0, 0, 0)),
                  pl.BlockSpec((1, n_cpad, LANES), lambda b: (b, 0, 0)),
                  pl.BlockSpec((1, n_cpad, LANES), lambda b: (b, 0, 0)),
                  pl.BlockSpec(selmap.shape, lambda b: (0, 0)),
                  pl.BlockSpec(rowtab.shape, lambda b: (0, 0, 0))],
        out_specs=[pl.BlockSpec((1, N_KV_HEADS, rows, LANES), lambda b: (b, 0, 0, 0)),
                   pl.BlockSpec((1, N_KV_HEADS, ROWS_PER_TOKEN, LANES), lambda b: (b, 0, 0, 0))],
        out_shape=[jax.ShapeDtypeStruct((nb, N_KV_HEADS, rows, LANES), F32),
                   jax.ShapeDtypeStruct((nb, N_KV_HEADS, ROWS_PER_TOKEN, LANES), jnp.int32)],
        compiler_params=_cparams(("arbitrary",)),
        name="nsa_sample_cmp",
    )(qs, kcmp, vcmp, selmap, rowtab)


def _nsa_sample_slc_kernel(idx_ref, pt_ref, *refs, past_len, n_tok, n_cache_blocks):
    del pt_ref
    k_refs = refs[:N_SELECT]
    v_refs = refs[N_SELECT:2 * N_SELECT]
    (q_ref, ocmp_ref, gn_ref, rowtab_ref, newk_ref, newv_ref, swk_ref, swv_ref, nwk_ref, nwv_ref,
     out_ref) = refs[2 * N_SELECT:]
    b = pl.program_id(0)
    hk = pl.program_id(1)
    tok = pl.program_id(2)
    t = past_len + tok
    base = ((b * N_KV_HEADS + hk) * n_tok + tok) * N_SELECT
    qp = (q_ref[0, 0, 0].astype(F32) * (HEAD_DIM ** -0.5)).astype(BF16)
    rowtab = rowtab_ref[0]
    rvalid = lax.broadcasted_iota(jnp.int32, (ROWS_PER_TOKEN, 1), 0) < GROUP
    init = (jnp.full((ROWS_PER_TOKEN, 1), NEG_INF, F32), jnp.zeros((ROWS_PER_TOKEN, 1), F32),
            jnp.zeros((ROWS_PER_TOKEN, HEAD_DIM), F32))
    halves = PAGE // SEL_BLOCK

    n_keys = N_SELECT * PAGE
    lane = lax.broadcasted_iota(jnp.int32, (ROWS_PER_TOKEN, n_keys), 1)
    row = lane % PAGE
    pos = row
    in_block = jnp.zeros((ROWS_PER_TOKEN, n_keys), jnp.bool_)
    kts, vts = [], []
    for n in range(N_SELECT):
        blk = idx_ref[base + n]
        is_new = blk >= n_cache_blocks
        kts.append(jnp.where(is_new, newk_ref[0, 0], k_refs[n][0, 0]).astype(BF16))
        vts.append(jnp.where(is_new, newv_ref[0, 0], v_refs[n][0, 0]).astype(BF16))
        slot = lane // PAGE == n
        pos = jnp.where(slot, row + (blk // halves) * PAGE, pos)
        in_block = in_block | (slot & (row // SEL_BLOCK == blk % halves))
    dist = t - pos
    s = _dot(qp, jnp.concatenate(kts, axis=1)) + _bias_from_rows(_bucket(dist), rowtab)
    mask = (dist >= 0) & in_block & rvalid
    o_slc = _softmax_finish(_softmax_update(init, jnp.where(mask, s, NEG_INF), jnp.concatenate(vts, axis=1),
                                            mask=mask, v_transposed=True))

    n_win = swk_ref.shape[3]
    kw = jnp.concatenate([swk_ref[0, 0], nwk_ref[0, 0]], axis=1).astype(BF16)
    vw = jnp.concatenate([swv_ref[0, 0], nwv_ref[0, 0]], axis=1).astype(BF16)
    wl = n_win + nwk_ref.shape[3]
    wpos = past_len - n_win + lax.broadcasted_iota(jnp.int32, (ROWS_PER_TOKEN, wl), 1)
    wdist = t - wpos
    s = _dot(qp, kw) + _bias_from_rows(_bucket(wdist), rowtab)
    mask = (wdist >= 0) & (wdist < WINDOW) & (wpos >= 0) & rvalid
    o_win = _softmax_finish(_softmax_update(init, jnp.where(mask, s, NEG_INF), vw, mask=mask, v_transposed=True))

    gs = jax.nn.sigmoid(gn_ref[0, 0, 0])
    out_ref[0, 0, 0] = gs[:, 0:1] * ocmp_ref[0, 0, 0] + gs[:, 1:2] * o_slc + gs[:, 2:3] * o_win


def _nsa_sample_slc(idx, page_table, cache_k, cache_v, q3, ocmp, gn3, rowtab, newk, newv, swk, swv, nwk, nwv,
                    past_len, n_tok):
    nb, npages = page_table.shape
    halves = PAGE // SEL_BLOCK
    n_cache_blocks = npages * halves
    ck = cache_k.transpose(0, 2, 3, 1)
    cv = cache_v.transpose(0, 2, 3, 1)

    def blk_spec(n):
        def imap(b, h, s, idx_ref, pt_ref):
            j = jnp.minimum(idx_ref[((b * N_KV_HEADS + h) * n_tok + s) * N_SELECT + n], n_cache_blocks - 1)
            return (pt_ref[b * npages + j // halves], h, 0, 0)
        return pl.BlockSpec((1, 1, HEAD_DIM, PAGE), imap)

    tile = lambda w: pl.BlockSpec((1, 1, 1, ROWS_PER_TOKEN, w), lambda b, h, s, i, p: (b, h, s, 0, 0))
    per_bh = lambda arr: pl.BlockSpec((1, 1) + arr.shape[2:], lambda b, h, s, i, p: (b, h, 0, 0))
    grid_spec = pltpu.PrefetchScalarGridSpec(
        num_scalar_prefetch=2,
        grid=(nb, N_KV_HEADS, n_tok),
        in_specs=[blk_spec(n) for n in range(N_SELECT)] * 2
        + [tile(HEAD_DIM), tile(HEAD_DIM), tile(LANES),
           pl.BlockSpec((1, ROWS_PER_TOKEN, N_BUCKETS), lambda b, h, s, i, p: (h, 0, 0)),
           per_bh(newk), per_bh(newv), per_bh(swk), per_bh(swv), per_bh(nwk), per_bh(nwv)],
        out_specs=tile(HEAD_DIM),
    )
    return pl.pallas_call(
        functools.partial(_nsa_sample_slc_kernel, past_len=past_len, n_tok=n_tok, n_cache_blocks=n_cache_blocks),
        grid_spec=grid_spec,
        out_shape=jax.ShapeDtypeStruct((nb, N_KV_HEADS, n_tok, ROWS_PER_TOKEN, HEAD_DIM), F32),
        compiler_params=_cparams(("arbitrary", "arbitrary", "arbitrary")),
        name="nsa_sample_slc",
    )(idx.reshape(-1), page_table.reshape(-1), *([ck] * N_SELECT), *([cv] * N_SELECT),
      q3, ocmp, gn3, rowtab, newk, newv, swk, swv, nwk, nwv)


def _merge_tail(x, a, c, gates, wn_ref, wc_ref, wo_ref):
    ga = gates[:, :D_MODEL]
    gb = gates[:, D_MODEL:]
    m = (jax.nn.sigmoid(ga) * _dot(a, wn_ref[...])
         + jax.nn.sigmoid(gb) * _dot(c.astype(BF16), wc_ref[...]))
    return x + _dot(m.astype(BF16), wo_ref[...])


def _merge_prompt_kernel(x_ref, a_ref, conv_ref, halo_ref, gate_ref, cw_ref, cb_ref, wn_ref, wc_ref, wo_ref,
                         out_ref, cst_ref, *, tiles_per_batch):
    i = pl.program_id(0)
    first = i % tiles_per_batch == 0
    conv = conv_ref[...]
    bg = conv[:, :CONV_DIM]
    u = conv[:, CONV_DIM:2 * CONV_DIM] * conv[:, 2 * CONV_DIM:]
    halo = halo_ref[...]
    uh = jnp.where(first, 0.0, halo[:, CONV_DIM:2 * CONV_DIM] * halo[:, 2 * CONV_DIM:])
    tm = u.shape[0]
    row = lax.broadcasted_iota(jnp.int32, (tm, CONV_DIM), 0)
    u1 = jnp.where(row == 0, uh[7:8], pltpu.roll(u, 1, axis=0))
    u2 = pltpu.roll(u, 2, axis=0)
    u2 = jnp.where(row == 0, uh[6:7], jnp.where(row == 1, uh[7:8], u2))
    cw = cw_ref[...]
    y = ((cb_ref[...] + cw[0:1] * u2) + cw[1:2] * u1) + cw[2:3] * u
    c = bg * y
    a = a_ref[0] + a_ref[1]
    out_ref[...] = _merge_tail(x_ref[...], a, c, gate_ref[...], wn_ref, wc_ref, wo_ref)
    cst_ref[0] = u[tm - 8:, :]


def _merge_prompt(x2d, a2, conv3, gates, conv_w, conv_b, wn, wc, wo, t):
    n = x2d.shape[0]
    tm = min(512, t)
    assert t % tm == 0
    tpb = t // tm
    row = lambda w: pl.BlockSpec((tm, w), lambda i: (i, 0))
    const = lambda arr: pl.BlockSpec(arr.shape, lambda i: (0,) * arr.ndim)
    return pl.pallas_call(
        functools.partial(_merge_prompt_kernel, tiles_per_batch=tpb),
        grid=(n // tm,),
        in_specs=[row(D_MODEL),
                  pl.BlockSpec((N_KV_HEADS, tm, NSA_WIDTH), lambda i: (0, i, 0)),
                  row(3 * CONV_DIM),
                  pl.BlockSpec((8, 3 * CONV_DIM), lambda i: (jnp.maximum(i * (tm // 8) - 1, 0), 0)),
                  row(2 * D_MODEL),
                  const(conv_w), const(conv_b), const(wn), const(wc), const(wo)],
        out_specs=[row(D_MODEL), pl.BlockSpec((1, 8, CONV_DIM), lambda i: (i // tpb, 0, 0))],
        out_shape=[jax.ShapeDtypeStruct((n, D_MODEL), F32),
                   jax.ShapeDtypeStruct((n // t, 8, CONV_DIM), F32)],
        compiler_params=_cparams(("arbitrary",)),
        name="merge_prompt",
    )(x2d, a2, conv3, conv3, gates, conv_w, conv_b, wn, wc, wo)


def _merge_sample_kernel(x_ref, a_ref, conv_ref, past_ref, gate_ref, cw_ref, cb_ref, wn_ref, wc_ref, wo_ref,
                         out_ref, cst_ref, *, n_tok, nb):
    conv = conv_ref[...]
    bg = conv[:, :CONV_DIM]
    u = conv[:, CONV_DIM:2 * CONV_DIM] * conv[:, 2 * CONV_DIM:]
    up = jnp.concatenate([past_ref[...], u], axis=0)
    cw = cw_ref[...]
    y = cb_ref[...]
    for k in range(CONV_WIDTH):
        y = y + cw[k:k + 1] * up[k * nb:(k + n_tok) * nb]
    c = bg * y
    out_ref[...] = _merge_tail(x_ref[...], a_ref[...].astype(BF16), c, gate_ref[...], wn_ref, wc_ref, wo_ref)
    cst_ref[...] = up[n_tok * nb:]


def _merge_sample(x_tm, a_tm, conv_tm, past_tm, gates_tm, conv_w, conv_b, wn, wc, wo, n_tok, nb):
    n = x_tm.shape[0]
    full = lambda arr: pl.BlockSpec(arr.shape, lambda i: (0,) * arr.ndim)
    args = (x_tm, a_tm, conv_tm, past_tm, gates_tm, conv_w, conv_b, wn, wc, wo)
    return pl.pallas_call(
        functools.partial(_merge_sample_kernel, n_tok=n_tok, nb=nb),
        grid=(1,),
        in_specs=[full(a) for a in args],
        out_specs=[pl.BlockSpec((n, D_MODEL), lambda i: (0, 0)),
                   pl.BlockSpec(((CONV_WIDTH - 1) * nb, CONV_DIM), lambda i: (0, 0))],
        out_shape=[jax.ShapeDtypeStruct((n, D_MODEL), F32),
                   jax.ShapeDtypeStruct(((CONV_WIDTH - 1) * nb, CONV_DIM), F32)],
        compiler_params=_cparams(("arbitrary",)),
        name="merge_sample",
    )(*args)


def _rms(x, g):
    return (x * lax.rsqrt(jnp.mean(x * x, axis=-1, keepdims=True) + RMS_EPS)) * g


def _mlp_kernel(x_ref, g_ref, wu_ref, wd_ref, gf_ref, out_ref, h_ref, acc_ref):
    j = pl.program_id(1)

    @pl.when(j == 0)
    def _():
        h_ref[...] = _rms(x_ref[...], g_ref[...]).astype(BF16)

    up = jnp.square(jnp.maximum(_dot(h_ref[...], wu_ref[...]), 0.0)).astype(BF16)
    part = _dot(up, wd_ref[...])

    @pl.when(j == 0)
    def _():
        acc_ref[...] = part

    @pl.when(j > 0)
    def _():
        acc_ref[...] += part

    @pl.when(j == pl.num_programs(1) - 1)
    def _():
        out_ref[...] = _rms(x_ref[...] + acc_ref[...], gf_ref[...])


def _mlp(x2d, g_mlp, w_up, w_down, g_final):
    n = x2d.shape[0]
    tm = min(512, n)
    tf = 1024
    assert n % tm == 0 and D_FF % tf == 0
    return pl.pallas_call(
        _mlp_kernel,
        grid=(n // tm, D_FF // tf),
        in_specs=[pl.BlockSpec((tm, D_MODEL), lambda i, j: (i, 0)),
                  pl.BlockSpec((1, D_MODEL), lambda i, j: (0, 0)),
                  pl.BlockSpec((D_MODEL, tf), lambda i, j: (0, j)),
                  pl.BlockSpec((tf, D_MODEL), lambda i, j: (j, 0)),
                  pl.BlockSpec((1, D_MODEL), lambda i, j: (0, 0))],
        out_specs=pl.BlockSpec((tm, D_MODEL), lambda i, j: (i, 0)),
        out_shape=jax.ShapeDtypeStruct((n, D_MODEL), F32),
        scratch_shapes=[pltpu.VMEM((tm, D_MODEL), BF16), pltpu.VMEM((tm, D_MODEL), F32)],
        compiler_params=_cparams(("arbitrary", "arbitrary")),
        name="mlp",
    )(x2d, g_mlp, w_up, w_down, g_final)


def _kv_rows(z, lead):
    return z.reshape(lead + (N_KV_HEADS, HEAD_DIM))


def _layer(xp, xs, caches, states, page_table, w, rel_bias):
    (cache_cmp_k, cache_cmp_v, cache_slc_k, cache_slc_v) = caches
    (state_win_k, state_win_v, state_conv) = states
    nbp, t, _ = xp.shape
    nbs, n_tok, _ = xs.shape
    npages = page_table.shape[1]
    past_len = npages * PAGE
    assert t % PAGE == 0 and n_tok <= GROUP and state_win_k.shape[1] >= n_tok

    w_all = _prep_w_in(w["w_in"])
    g_attn = w["g_attn"].reshape(1, D_MODEL)
    wn = w["w_nsa_out"].reshape(N_KV_HEADS, GROUP, HEAD_DIM, D_MODEL).transpose(1, 0, 2, 3)
    wn = wn.reshape(NSA_WIDTH, D_MODEL).astype(BF16)
    wc = w["w_conv_out"].astype(BF16)
    wo = w["w_o"].astype(BF16)
    wu = w["w_up"].astype(BF16)
    wd = w["w_down"].astype(BF16)
    g_mlp = w["g_mlp"].reshape(1, D_MODEL)
    g_final = w["g_final"].reshape(1, D_MODEL)
    conv_w = w["conv_w"]
    conv_b = w["conv_b"].reshape(1, CONV_DIM)
    cmp_k = (w["cmp_pe_k"], w["cmp_w1_k"], w["cmp_w2_k"])
    cmp_v = (w["cmp_pe_v"], w["cmp_w1_v"], w["cmp_w2_v"])

    n_p = nbp * t
    (q, kc, vc, ks, vs, kw, vw, kvb, gn, conv3, gates) = _proj(xp.reshape(n_p, D_MODEL), g_attn, w_all)
    ident = jnp.arange(nbp * (t // PAGE), dtype=jnp.int32).reshape(nbp, t // PAGE)
    kcmp = _compress(kc.reshape(n_p // PAGE, PAGE, KV_WIDTH), 0, ident, *cmp_k)
    vcmp = _compress(vc.reshape(n_p // PAGE, PAGE, KV_WIDTH), 0, ident, *cmp_v)
    a2 = _nsa_prompt(q.reshape(nbp, t, NSA_WIDTH), gn.reshape(nbp, t, LANES), kcmp, vcmp,
                     kvb.reshape(nbp, t, 6 * KV_WIDTH), rel_bias)
    x1, cst = _merge_prompt(xp.reshape(n_p, D_MODEL), a2.reshape(N_KV_HEADS, n_p, NSA_WIDTH),
                            conv3, gates, conv_w, conv_b, wn, wc, wo, t)
    yp = _mlp(x1, g_mlp, wu, wd, g_final).reshape(nbp, t, D_MODEL)
    win_p = min(WINDOW, t)
    new_p = tuple(_kv_rows(z, (nbp, t)) for z in (kc, vc, ks, vs)) + (
        _kv_rows(kw, (nbp, t))[:, t - win_p:], _kv_rows(vw, (nbp, t))[:, t - win_p:],
        cst[:, 8 - (CONV_WIDTH - 1):, :])

    n_s = nbs * n_tok
    (qs, kcs, vcs, kss, vss, kws, vws, _, gns, conv3s, gatess) = _proj(xs.reshape(n_s, D_MODEL), g_attn, w_all)
    n_rows = past_len + n_tok
    n_chunk = n_rows // CMP_STRIDE
    assert n_chunk == past_len // CMP_STRIDE, "new rows must not complete a compression chunk"
    n_cmp = n_chunk - CMP_RATIO + 1
    n_blocks = -(-n_rows // SEL_BLOCK)
    kcmp_s = _compress(cache_cmp_k.reshape(-1, PAGE, KV_WIDTH), 0, page_table, *cmp_k)
    vcmp_s = _compress(cache_cmp_v.reshape(-1, PAGE, KV_WIDTH), 0, page_table, *cmp_v)

    lane = np.arange(LANES)
    q5 = qs.reshape(nbs, n_tok, GROUP, LANES)
    q3 = jnp.stack([jnp.where(jnp.asarray((lane // HEAD_DIM) == h), q5, jnp.zeros_like(q5))
                    for h in range(N_KV_HEADS)], axis=1)
    q3 = jnp.pad(q3, ((0, 0), (0, 0), (0, 0), (0, ROWS_PER_TOKEN - GROUP), (0, 0)))
    tab = rel_bias.T.reshape(N_KV_HEADS, GROUP, N_BUCKETS)
    rowtab3 = jnp.pad(tab, ((0, 0), (0, ROWS_PER_TOKEN - GROUP), (0, 0)))
    rowtab = jnp.tile(rowtab3, (1, n_tok, 1))
    ocmp, idx = _nsa_sample_cmp(q3.reshape(nbs, N_KV_HEADS, n_tok * ROWS_PER_TOKEN, LANES), kcmp_s, vcmp_s,
                                rowtab, past_len, n_tok, n_cmp, n_blocks)
    idx = idx[:, :, :n_tok, :N_SELECT]
    gn3 = gns[:, :3 * N_HEADS].reshape(nbs, n_tok, N_KV_HEADS, GROUP, 3).transpose(0, 2, 1, 3, 4)
    gn3 = jnp.pad(gn3, ((0, 0), (0, 0), (0, 0), (0, ROWS_PER_TOKEN - GROUP), (0, LANES - 3)))
    q3s = jnp.pad(qs.reshape(nbs, n_tok, GROUP, N_KV_HEADS, HEAD_DIM).transpose(0, 3, 1, 2, 4),
                  ((0, 0), (0, 0), (0, 0), (0, ROWS_PER_TOKEN - GROUP), (0, 0)))
    ocmp5 = ocmp.reshape(nbs, N_KV_HEADS, n_tok, ROWS_PER_TOKEN, LANES)
    ocmp5 = jnp.stack([ocmp5[:, h, :, :, h * HEAD_DIM:(h + 1) * HEAD_DIM] for h in range(N_KV_HEADS)], axis=1)
    new_t = lambda z: jnp.pad(z.reshape(nbs, n_tok, N_KV_HEADS, HEAD_DIM).transpose(0, 2, 3, 1),
                              ((0, 0), (0, 0), (0, 0), (0, LANES - n_tok)))
    a_s = _nsa_sample_slc(idx, page_table, cache_slc_k, cache_slc_v, q3s, ocmp5, gn3, rowtab3,
                          new_t(kss), new_t(vss),
                          state_win_k.transpose(0, 2, 3, 1), state_win_v.transpose(0, 2, 3, 1),
                          new_t(kws), new_t(vws), past_len, n_tok)
    a_tm = a_s[:, :, :, :GROUP, :].transpose(2, 0, 3, 1, 4).reshape(n_s, NSA_WIDTH)
    to_tm = lambda z: z.reshape(nbs, n_tok, -1).transpose(1, 0, 2).reshape(n_s, -1)
    past_tm = state_conv.transpose(1, 0, 2).reshape((CONV_WIDTH - 1) * nbs, CONV_DIM)
    x1s, csts = _merge_sample(to_tm(xs), a_tm, to_tm(conv3s), past_tm, to_tm(gatess),
                              conv_w, conv_b, wn, wc, wo, n_tok, nbs)
    ys = _mlp(x1s, g_mlp, wu, wd, g_final).reshape(n_tok, nbs, D_MODEL).transpose(1, 0, 2)
    win_buf = state_win_k.shape[1]
    kw_all = jnp.concatenate([state_win_k, _kv_rows(kws, (nbs, n_tok))], axis=1)[:, -win_buf:]
    vw_all = jnp.concatenate([state_win_v, _kv_rows(vws, (nbs, n_tok))], axis=1)[:, -win_buf:]
    new_s = tuple(_kv_rows(z, (nbs, n_tok)) for z in (kcs, vcs, kss, vss)) + (
        kw_all, vw_all, csts.reshape(CONV_WIDTH - 1, nbs, CONV_DIM).transpose(1, 0, 2))
    return yp, ys, new_p, new_s


def kernel(x_prompt, x_sample, cache_cmp_k, cache_cmp_v, cache_slc_k, cache_slc_v, state_win_k, state_win_v,
           state_conv, page_table, g_attn, w_in, cmp_pe_k, cmp_w1_k, cmp_w2_k, cmp_pe_v, cmp_w1_v, cmp_w2_v,
           conv_w, conv_b, w_nsa_out, w_conv_out, w_o, g_mlp, w_up, w_down, rel_bias, g_final):
    depth = w_in.shape[0]
    assert depth == 1, "the final norm is fused into the last layer's MLP kernel"
    names = ("g_attn", "w_in", "cmp_pe_k", "cmp_w1_k", "cmp_w2_k", "cmp_pe_v", "cmp_w1_v", "cmp_w2_v",
             "conv_w", "conv_b", "w_nsa_out", "w_conv_out", "w_o", "g_mlp", "w_up", "w_down")
    vals = (g_attn, w_in, cmp_pe_k, cmp_w1_k, cmp_w2_k, cmp_pe_v, cmp_w1_v, cmp_w2_v,
            conv_w, conv_b, w_nsa_out, w_conv_out, w_o, g_mlp, w_up, w_down)
    l = 0
    w = {k: v[l] for k, v in zip(names, vals)}
    w["g_final"] = g_final
    yp, ys, new_p, new_s = _layer(
        x_prompt, x_sample,
        (cache_cmp_k[l], cache_cmp_v[l], cache_slc_k[l], cache_slc_v[l]),
        (state_win_k[l], state_win_v[l], state_conv[l]), page_table, w, rel_bias)
    return (yp, ys) + tuple(z[None] for z in new_p) + tuple(z[None] for z in new_s)
```

```python
import functools
import math

import numpy as np
import jax
import jax.numpy as jnp
from jax import lax
from jax.experimental import pallas as pl
from jax.experimental.pallas import tpu as pltpu

F32 = jnp.float32
BF16 = jnp.bfloat16

D_MODEL = 1024
N_HEADS = 8
HEAD_DIM = 64
N_KV_HEADS = 2
GROUP = N_HEADS // N_KV_HEADS
KV_WIDTH = N_KV_HEADS * HEAD_DIM
NSA_WIDTH = N_HEADS * HEAD_DIM
CMP_LEN = 32
CMP_STRIDE = 16
CMP_RATIO = CMP_LEN // CMP_STRIDE
CMP_HIDDEN = 256
SEL_BLOCK = 64
N_SELECT = 16
WINDOW = 512
CONV_DIM = 512
CONV_WIDTH = 3
D_FF = 4 * D_MODEL
N_BUCKETS = 32
MAX_DISTANCE = 128
Q_BLOCK = 128
PAGE = 128
RMS_EPS = 1e-6
NEG_INF = -1e30
FORCE_BONUS = 1e4
LANES = 128
VMEM_LIMIT = 56 * 1024 * 1024


def _bucket_thresholds():
    max_exact = N_BUCKETS // 2
    n = np.arange(max_exact, 4 * MAX_DISTANCE).astype(np.float32)
    large = max_exact + (np.log(n / np.float32(max_exact)) / np.float32(math.log(MAX_DISTANCE / max_exact))
                         * np.float32(N_BUCKETS - max_exact)).astype(np.int32)
    large = np.minimum(large, N_BUCKETS - 1)
    thr = []
    for k in range(max_exact + 1, N_BUCKETS):
        thr.append(int(np.arange(max_exact, 4 * MAX_DISTANCE)[np.argmax(large >= k)]))
    return tuple(thr)


BUCKET_THR = _bucket_thresholds()


def _bucket(dist):
    n = jnp.maximum(dist, 0)
    big = jnp.full(n.shape, N_BUCKETS // 2, jnp.int32)
    for thr in BUCKET_THR:
        big = big + (n >= thr).astype(jnp.int32)
    return jnp.where(n < N_BUCKETS // 2, n, big)


def _bias_from_scalars(bucket, table_fn):
    out = jnp.zeros(bucket.shape, F32)
    for k in range(N_BUCKETS):
        out = jnp.where(bucket == k, table_fn(k), out)
    return out


def _bias_from_rows(bucket, rowtab):
    out = jnp.zeros(bucket.shape, F32)
    for k in range(N_BUCKETS):
        out = jnp.where(bucket == k, rowtab[:, k:k + 1], out)
    return out


def _dot(a, b):
    return jnp.dot(a, b, preferred_element_type=F32)


def _dot_t(a, b):
    return lax.dot_general(a, b, (((1,), (1,)), ((), ())), preferred_element_type=F32)


def _split_dot(p, m):
    hi = p.astype(BF16)
    lo = (p - hi.astype(F32)).astype(BF16)
    return _dot(hi, m) + _dot(lo, m)


def _cparams(sem):
    return pltpu.CompilerParams(dimension_semantics=sem, vmem_limit_bytes=VMEM_LIMIT)


PROJ_WIDTHS = (NSA_WIDTH, 6 * KV_WIDTH, LANES, 3 * CONV_DIM, 2 * D_MODEL)
PROJ_COLS = sum(PROJ_WIDTHS)


def _proj_kernel(x_ref, g_ref, w_ref, q_ref, kc_ref, vc_ref, ks_ref, vs_ref, kw_ref, vw_ref,
                 kvb_ref, gn_ref, conv_ref, gate_ref):
    x = x_ref[...]
    y = x * lax.rsqrt(jnp.mean(x * x, axis=-1, keepdims=True) + RMS_EPS)
    hb = (y * g_ref[...]).astype(BF16)
    q_ref[...] = _dot(hb, w_ref[:, 0:NSA_WIDTH]).astype(q_ref.dtype)
    off = NSA_WIDTH
    kv = _dot(hb, w_ref[:, off:off + 6 * KV_WIDTH])
    for i, r in enumerate((kc_ref, vc_ref, ks_ref, vs_ref, kw_ref, vw_ref)):
        r[...] = kv[:, i * KV_WIDTH:(i + 1) * KV_WIDTH]
    kvb_ref[...] = kv.astype(BF16)
    off += 6 * KV_WIDTH
    gn_ref[...] = _dot(hb, w_ref[:, off:off + LANES])
    off += LANES
    conv_ref[...] = _dot(hb, w_ref[:, off:off + 3 * CONV_DIM])
    off += 3 * CONV_DIM
    gate_ref[...] = _dot(hb, w_ref[:, off:off + 2 * D_MODEL])


def _proj(x2d, g, w_all):
    n = x2d.shape[0]
    tm = min(256, n)
    assert n % tm == 0
    row = lambda w: pl.BlockSpec((tm, w), lambda i: (i, 0))
    out_shapes = ([jax.ShapeDtypeStruct((n, NSA_WIDTH), BF16)]
                  + [jax.ShapeDtypeStruct((n, KV_WIDTH), F32)] * 6
                  + [jax.ShapeDtypeStruct((n, 6 * KV_WIDTH), BF16),
                     jax.ShapeDtypeStruct((n, LANES), F32),
                     jax.ShapeDtypeStruct((n, 3 * CONV_DIM), F32),
                     jax.ShapeDtypeStruct((n, 2 * D_MODEL), F32)])
    out_specs = ([row(NSA_WIDTH)] + [row(KV_WIDTH)] * 6
                 + [row(6 * KV_WIDTH), row(LANES), row(3 * CONV_DIM), row(2 * D_MODEL)])
    return pl.pallas_call(
        _proj_kernel,
        grid=(n // tm,),
        in_specs=[row(D_MODEL),
                  pl.BlockSpec((1, D_MODEL), lambda i: (0, 0)),
                  pl.BlockSpec((D_MODEL, PROJ_COLS), lambda i: (0, 0))],
        out_specs=out_specs,
        out_shape=out_shapes,
        compiler_params=_cparams(("arbitrary",)),
        name="proj",
    )(x2d, g, w_all)


def _prep_w_in(w_in):
    sizes = (NSA_WIDTH,) + (KV_WIDTH,) * 6 + (3 * N_HEADS, CONV_DIM, CONV_DIM, CONV_DIM, D_MODEL, D_MODEL)
    offs = np.concatenate([[0], np.cumsum(sizes)])
    wq = w_in[:, :NSA_WIDTH].reshape(D_MODEL, N_KV_HEADS, GROUP, HEAD_DIM)
    wq = wq.transpose(0, 2, 1, 3).reshape(D_MODEL, NSA_WIDTH)
    wkv = w_in[:, offs[1]:offs[7]]
    wgn = jnp.pad(w_in[:, offs[7]:offs[8]], ((0, 0), (0, LANES - 3 * N_HEADS)))
    wrest = w_in[:, offs[8]:]
    return jnp.concatenate([wq, wkv, wgn, wrest], axis=1).astype(BF16)


CHUNKS_PER_PAGE = PAGE // CMP_STRIDE


def _compress_kernel(pt_ref, *refs, pp):
    del pt_ref
    page_refs = refs[:pp + 1]
    wbd_ref, pe_ref, w1_ref, w2_ref, out_ref, acc_ref, pehid_ref = refs[pp + 1:]

    @pl.when((pl.program_id(0) == 0) & (pl.program_id(1) == 0))
    def _():
        pehid_ref[...] = _dot(pe_ref[...], w1_ref[...])

    cols = []
    for r in range(CMP_STRIDE):
        xr = jnp.concatenate([p[0, pl.ds(r, CHUNKS_PER_PAGE, stride=CMP_STRIDE), :] for p in page_refs], axis=0)
        cols.append(xr.astype(BF16))
    x = jnp.concatenate(cols, axis=1)
    acc_ref[...] = _dot(x, wbd_ref[...])
    nc = CHUNKS_PER_PAGE * pp
    pe = pehid_ref[0:1, :]
    hids = []
    for h in range(N_KV_HEADS):
        base = h * CMP_RATIO * CMP_HIDDEN
        p0 = acc_ref[0:nc, base:base + CMP_HIDDEN]
        p1 = acc_ref[pl.ds(1, nc), base + CMP_HIDDEN:base + 2 * CMP_HIDDEN]
        hids.append(jax.nn.gelu((pe + p0) + p1).astype(BF16))
    out_ref[0] = _dot(jnp.concatenate(hids, axis=1), w2_ref[...])


def _compress(pages, colblk, page_table, pe, w1, w2):
    nb, npages = page_table.shape
    pp = min(32, npages)
    assert npages % pp == 0
    nc = CHUNKS_PER_PAGE * pp
    w1r = w1.reshape(CMP_RATIO, CMP_STRIDE, HEAD_DIM, CMP_HIDDEN)
    blk = w1r.transpose(1, 2, 0, 3).reshape(CMP_STRIDE, HEAD_DIM, CMP_RATIO * CMP_HIDDEN)
    z = jnp.zeros_like(blk)
    wbd = jnp.concatenate([jnp.concatenate([blk, z], axis=2), jnp.concatenate([z, blk], axis=2)], axis=1)
    wbd = wbd.reshape(CMP_STRIDE * KV_WIDTH, N_KV_HEADS * CMP_RATIO * CMP_HIDDEN).astype(BF16)
    z2 = jnp.zeros_like(w2)
    w2bd = jnp.concatenate([jnp.concatenate([w2, z2], axis=1), jnp.concatenate([z2, w2], axis=1)], axis=0).astype(BF16)
    pe8 = jnp.broadcast_to(pe.reshape(1, CMP_LEN * HEAD_DIM), (8, CMP_LEN * HEAD_DIM)).astype(BF16)

    def page_spec(p):
        def imap(b, s, pt):
            return (pt[b, jnp.minimum(s * pp + p, npages - 1)], 0, colblk)
        return pl.BlockSpec((1, PAGE, KV_WIDTH), imap)

    const = lambda shape: pl.BlockSpec(shape, lambda b, s, pt: (0,) * len(shape))
    grid_spec = pltpu.PrefetchScalarGridSpec(
        num_scalar_prefetch=1,
        grid=(nb, npages // pp),
        in_specs=[page_spec(p) for p in range(pp + 1)]
        + [const(wbd.shape), const(pe8.shape), const((CMP_LEN * HEAD_DIM, CMP_HIDDEN)), const(w2bd.shape)],
        out_specs=pl.BlockSpec((1, nc, KV_WIDTH), lambda b, s, pt: (b, s, 0)),
        scratch_shapes=[pltpu.VMEM((nc + 8, N_KV_HEADS * CMP_RATIO * CMP_HIDDEN), F32),
                        pltpu.VMEM((8, CMP_HIDDEN), F32)],
    )
    return pl.pallas_call(
        functools.partial(_compress_kernel, pp=pp),
        grid_spec=grid_spec,
        out_shape=jax.ShapeDtypeStruct((nb, npages * CHUNKS_PER_PAGE, KV_WIDTH), F32),
        compiler_params=_cparams(("arbitrary", "arbitrary")),
        name="compress",
    )(page_table, *([pages] * (pp + 1)), wbd, pe8, w1.astype(BF16), w2bd)


def _compress_t_kernel(pt_ref, *refs, pp):
    del pt_ref
    page_refs = refs[:pp + 1]
    w1p_ref, pe_ref, w1_ref, w2_ref, out_ref, xt_ref, acc_ref, pehid_ref = refs[pp + 1:]

    @pl.when((pl.program_id(0) == 0) & (pl.program_id(1) == 0))
    def _():
        pehid_ref[...] = _dot(pe_ref[...], w1_ref[...])

    for p, page in enumerate(page_refs):
        for h in range(N_KV_HEADS):
            kt = page[0, h]
            z = jnp.concatenate([kt, pltpu.roll(kt, PAGE - 1, axis=1)], axis=0)
            xt_ref[h, p * PAGE:(p + 1) * PAGE, :] = z.T
    nc = CHUNKS_PER_PAGE * pp
    m = nc + CHUNKS_PER_PAGE
    pe = pehid_ref[0:1, :]
    hids = []
    for h in range(N_KV_HEADS):
        x = jnp.concatenate([xt_ref[h, pl.ds(r, m, stride=CMP_STRIDE), :].astype(BF16)
                             for r in range(0, CMP_STRIDE, 2)], axis=1)
        acc_ref[h] = _dot(x, w1p_ref[...])
        p0 = acc_ref[h, 0:nc, 0:CMP_HIDDEN]
        p1 = acc_ref[h, pl.ds(1, nc), CMP_HIDDEN:2 * CMP_HIDDEN]
        hids.append(jax.nn.gelu((pe + p0) + p1).astype(BF16))
    out_ref[0] = _dot(jnp.concatenate(hids, axis=1), w2_ref[...])


def _compress_t(cache, page_table, pe, w1, w2):
    nb, npages = page_table.shape
    pp = min(32, npages)
    assert npages % pp == 0
    nc = CHUNKS_PER_PAGE * pp
    pages = cache.transpose(0, 2, 3, 1)
    w1p = w1.reshape(CMP_RATIO, CMP_STRIDE * HEAD_DIM, CMP_HIDDEN).transpose(1, 0, 2)
    w1p = w1p.reshape(CMP_STRIDE * HEAD_DIM, CMP_RATIO * CMP_HIDDEN).astype(BF16)
    z2 = jnp.zeros_like(w2)
    w2bd = jnp.concatenate([jnp.concatenate([w2, z2], axis=1), jnp.concatenate([z2, w2], axis=1)], axis=0).astype(BF16)
    pe8 = jnp.broadcast_to(pe.reshape(1, CMP_LEN * HEAD_DIM), (8, CMP_LEN * HEAD_DIM)).astype(BF16)

    def page_spec(p):
        def imap(b, s, pt):
            return (pt[b, jnp.minimum(s * pp + p, npages - 1)], 0, 0, 0)
        return pl.BlockSpec((1, N_KV_HEADS, HEAD_DIM, PAGE), imap)

    const = lambda shape: pl.BlockSpec(shape, lambda b, s, pt: (0,) * len(shape))
    grid_spec = pltpu.PrefetchScalarGridSpec(
        num_scalar_prefetch=1,
        grid=(nb, npages // pp),
        in_specs=[page_spec(p) for p in range(pp + 1)]
        + [const(w1p.shape), const(pe8.shape), const((CMP_LEN * HEAD_DIM, CMP_HIDDEN)), const(w2bd.shape)],
        out_specs=pl.BlockSpec((1, nc, KV_WIDTH), lambda b, s, pt: (b, s, 0)),
        scratch_shapes=[pltpu.VMEM((N_KV_HEADS, (pp + 1) * PAGE, 2 * HEAD_DIM), F32),
                        pltpu.VMEM((N_KV_HEADS, nc + CHUNKS_PER_PAGE, CMP_RATIO * CMP_HIDDEN), F32),
                        pltpu.VMEM((8, CMP_HIDDEN), F32)],
    )
    return pl.pallas_call(
        functools.partial(_compress_t_kernel, pp=pp),
        grid_spec=grid_spec,
        out_shape=jax.ShapeDtypeStruct((nb, npages * CHUNKS_PER_PAGE, KV_WIDTH), F32),
        compiler_params=_cparams(("arbitrary", "arbitrary")),
        name="compress_t",
    )(page_table, *([pages] * (pp + 1)), w1p, pe8, w1.astype(BF16), w2bd)


def _sel_map(n_rows, n_cmp, n_cols, n_blocks):
    c0 = np.arange(n_rows)[:, None] * CMP_STRIDE
    s0 = np.arange(n_cols)[None, :] * SEL_BLOCK
    ov = np.minimum(c0 + CMP_LEN, s0 + SEL_BLOCK) - np.maximum(c0, s0)
    m = np.clip(ov, 0, None).astype(np.float32) / CMP_LEN
    m[n_cmp:] = 0.0
    m[:, n_blocks:] = 0.0
    return m


CMP_FRONT = 16
CMP_NEAR = 24
SLC_NEAR = 2 * Q_BLOCK
SLC_TILE = 512
WIN_KEYS = WINDOW + Q_BLOCK


def _softmax_update(state, s, v, mask=None, v_transposed=False):
    m_old, l_old, acc = state
    m_new = jnp.maximum(m_old, jnp.max(s, axis=-1, keepdims=True))
    p = jnp.exp(s - m_new)
    if mask is not None:
        p = jnp.where(mask, p, 0.0)
    alpha = jnp.exp(m_old - m_new)
    l_new = alpha * l_old + jnp.sum(p, axis=-1, keepdims=True)
    pv = _dot_t(p.astype(BF16), v) if v_transposed else _dot(p.astype(BF16), v)
    return m_new, l_new, alpha * acc + pv


def _softmax_finish(state):
    _, l, acc = state
    return jnp.where(l > 0.0, acc / jnp.where(l > 0.0, l, 1.0), 0.0)


def _tile_rows(x, n):
    return jnp.concatenate([x] * n, axis=0)


def _nsa_prompt_kernel(rb_ref, q_ref, gn_ref, kcmp_ref, vcmp_ref, selmap_ref, ks_ref, vs_ref, kw_ref, vw_ref,
                       et_ref, out_ref, nbc_ref, nbs_ref, nbw_ref):
    hk = pl.program_id(1)
    qb = pl.program_id(2)
    s0 = qb * Q_BLOCK
    gq = GROUP * Q_BLOCK

    @pl.when(qb == 0)
    def _():
        for g in range(GROUP):
            tab = lambda k, g=g: rb_ref[k, hk * GROUP + g]
            ql = lax.broadcasted_iota(jnp.int32, (Q_BLOCK, LANES), 0)
            cl = lax.broadcasted_iota(jnp.int32, (Q_BLOCK, LANES), 1)
            dist = ql - CMP_STRIDE * (cl - CMP_FRONT) - (CMP_LEN - 1)
            ok = (dist >= 0) & (cl < CMP_NEAR)
            nbc_ref[g] = jnp.where(ok, _bias_from_scalars(_bucket(dist), tab), NEG_INF)
            ql = lax.broadcasted_iota(jnp.int32, (Q_BLOCK, SLC_NEAR), 0)
            kl = lax.broadcasted_iota(jnp.int32, (Q_BLOCK, SLC_NEAR), 1)
            dist = ql - kl + Q_BLOCK
            nbs_ref[g] = jnp.where(dist >= 0, _bias_from_scalars(_bucket(dist), tab), NEG_INF)
            ql = lax.broadcasted_iota(jnp.int32, (Q_BLOCK, WIN_KEYS), 0)
            kl = lax.broadcasted_iota(jnp.int32, (Q_BLOCK, WIN_KEYS), 1)
            dist = ql - kl + WINDOW
            ok = (dist >= 0) & (dist < WINDOW)
            nbw_ref[g] = jnp.where(ok, _bias_from_scalars(_bucket(dist), tab), NEG_INF)

    lane = lax.broadcasted_iota(jnp.int32, (Q_BLOCK, LANES), 1)
    lmask = (lane >= hk * HEAD_DIM) & (lane < (hk + 1) * HEAD_DIM)
    qblk = q_ref[0].astype(F32)
    qp = jnp.concatenate(
        [jnp.where(lmask, qblk[:, g * LANES:(g + 1) * LANES], 0.0) for g in range(GROUP)], axis=0)
    qp = (qp * (HEAD_DIM ** -0.5)).astype(BF16)

    far_bias = jnp.concatenate(
        [jnp.full((Q_BLOCK, 1), rb_ref[N_BUCKETS - 1, hk * GROUP + g], F32) for g in range(GROUP)], axis=0)

    n_cmp_pad = kcmp_ref.shape[1] - CMP_FRONT - (LANES - CMP_FRONT)
    kc_far = kcmp_ref[0, CMP_FRONT:CMP_FRONT + n_cmp_pad, :].astype(BF16)
    vc_far = vcmp_ref[0, CMP_FRONT:CMP_FRONT + n_cmp_pad, :].astype(BF16)
    near0 = pl.multiple_of(qb * (Q_BLOCK // CMP_STRIDE), 8)
    kc_near = kcmp_ref[0, pl.ds(near0, LANES), :].astype(BF16)
    vc_near = vcmp_ref[0, pl.ds(near0, LANES), :].astype(BF16)
    c8 = qb * (Q_BLOCK // CMP_STRIDE)
    ccol = lax.broadcasted_iota(jnp.int32, (1, n_cmp_pad), 1)
    s_far = _dot_t(qp, kc_far) + jnp.where(ccol < c8 - CMP_FRONT, 0.0, NEG_INF)
    ncol = lax.broadcasted_iota(jnp.int32, (1, LANES), 1)
    s_near = _dot_t(qp, kc_near) + (nbc_ref[...].reshape(gq, LANES)
                                    + jnp.where(ncol >= CMP_FRONT - c8, 0.0, NEG_INF))
    m = jnp.maximum(jnp.max(s_far, axis=-1, keepdims=True) + far_bias, jnp.max(s_near, axis=-1, keepdims=True))
    p_far = jnp.exp(s_far - (m - far_bias))
    p_near = jnp.exp(s_near - m)
    l = jnp.sum(p_far, axis=-1, keepdims=True) + jnp.sum(p_near, axis=-1, keepdims=True)
    inv = jnp.where(m > 0.5 * NEG_INF, 1.0 / l, 0.0)
    pc_far = p_far * inv
    pc_near = p_near * inv
    o_cmp = _dot(pc_far.astype(BF16), vc_far) + _dot(pc_near.astype(BF16), vc_near)

    init = (jnp.full((gq, 1), NEG_INF, F32), jnp.zeros((gq, 1), F32), jnp.zeros((gq, LANES), F32))
    p0 = pl.multiple_of(s0, LANES)
    wcol = lax.broadcasted_iota(jnp.int32, (1, WIN_KEYS), 1)
    s = _dot_t(qp, kw_ref[0, pl.ds(p0, WIN_KEYS), :]) + (nbw_ref[...].reshape(gq, WIN_KEYS)
                                                         + jnp.where(wcol >= WINDOW - s0, 0.0, NEG_INF))
    o_win = _softmax_finish(_softmax_update(init, s, vw_ref[0, pl.ds(p0, WIN_KEYS), :]))

    ps_far = sum(pc_far[g * Q_BLOCK:(g + 1) * Q_BLOCK] for g in range(GROUP))
    ps_near = sum(pc_near[g * Q_BLOCK:(g + 1) * Q_BLOCK] for g in range(GROUP))
    sm_far = selmap_ref[CMP_FRONT:CMP_FRONT + n_cmp_pad, :].astype(BF16)
    sm_near = selmap_ref[pl.ds(near0, LANES), :].astype(BF16)
    imp = _split_dot(ps_far, sm_far) + _split_dot(ps_near, sm_near)
    imp_t = imp.T
    n_sel = imp_t.shape[0]
    blk = lax.broadcasted_iota(jnp.int32, (n_sel, Q_BLOCK), 0)
    cur = (s0 + lax.broadcasted_iota(jnp.int32, (n_sel, Q_BLOCK), 1)) // SEL_BLOCK
    forced = (blk == 0) | (blk == cur) | (blk == cur - 1)
    val = jnp.where(blk > cur, NEG_INF, imp_t + FORCE_BONUS * forced.astype(F32))
    sel_t = jnp.zeros((n_sel, Q_BLOCK), F32)
    for _ in range(N_SELECT):
        mx = jnp.max(val, axis=0, keepdims=True)
        first = jnp.min(jnp.where(val == mx, blk, n_sel), axis=0, keepdims=True)
        hit = blk == first
        sel_t = jnp.where(hit, 1.0, sel_t)
        val = jnp.where(hit, -3e38, val)
    sel = sel_t.T > 0.5

    jcol = lax.broadcasted_iota(jnp.int32, (Q_BLOCK, n_sel), 1)
    far_blocks = (s0 - Q_BLOCK) // SEL_BLOCK
    pen_far = jnp.where(sel & (jcol < far_blocks), 0.0, NEG_INF).astype(BF16)
    pen_near = jnp.where(sel, 0.0, NEG_INF).astype(BF16)
    q_far = jnp.concatenate([qp, _tile_rows(pen_far, GROUP)], axis=1)
    q_near = jnp.concatenate([qp, _tile_rows(pen_near, GROUP)], axis=1)

    def far_step(kt, state):
        p0 = pl.multiple_of(Q_BLOCK + kt * SLC_TILE, LANES)
        k = jnp.concatenate([ks_ref[0, pl.ds(p0, SLC_TILE), :], et_ref[pl.ds(p0, SLC_TILE), :]], axis=1)
        return _softmax_update(state, _dot_t(q_far, k), vs_ref[0, pl.ds(p0, SLC_TILE), :])

    n_far = (qb + 2) // (SLC_TILE // Q_BLOCK)
    m_far, l_far, acc_far = lax.fori_loop(0, n_far, far_step, init)
    state = (m_far + far_bias, l_far, acc_far)
    k = jnp.concatenate([ks_ref[0, pl.ds(p0, SLC_NEAR), :], et_ref[pl.ds(p0, SLC_NEAR), :]], axis=1)
    kcol = lax.broadcasted_iota(jnp.int32, (1, SLC_NEAR), 1)
    s = _dot_t(q_near, k) + (nbs_ref[...].reshape(gq, SLC_NEAR) + jnp.where(kcol >= Q_BLOCK - s0, 0.0, NEG_INF))
    o_slc = _softmax_finish(_softmax_update(state, s, vs_ref[0, pl.ds(p0, SLC_NEAR), :]))

    gs = jax.nn.sigmoid(gn_ref[0])

    def gate(g, i):
        c0, c1 = 3 * g + i, 3 * (GROUP + g) + i
        return jnp.where(hk == 0, gs[:, c0:c0 + 1], gs[:, c1:c1 + 1])

    for g in range(GROUP):
        rows = slice(g * Q_BLOCK, (g + 1) * Q_BLOCK)
        o = gate(g, 0) * o_cmp[rows] + gate(g, 1) * o_slc[rows] + gate(g, 2) * o_win[rows]
        out_ref[0, 0, :, g * LANES:(g + 1) * LANES] = jnp.where(lmask, o, 0.0).astype(out_ref.dtype)


def _nsa_prompt(q, gn, kcmp, vcmp, kvb, rel_bias):
    b, t, _ = q.shape
    assert t % SLC_TILE == 0
    nqb = t // Q_BLOCK
    n_chunk = t // CMP_STRIDE
    n_cmp = n_chunk - CMP_RATIO + 1
    n_blocks = t // SEL_BLOCK
    assert n_blocks <= LANES
    cpad = ((0, 0), (CMP_FRONT, LANES - CMP_FRONT), (0, 0))
    kcmp_p = jnp.pad(kcmp, cpad)
    vcmp_p = jnp.pad(vcmp, cpad)
    selmap = jnp.asarray(np.pad(_sel_map(n_chunk, n_cmp, LANES, n_blocks),
                                ((CMP_FRONT, LANES - CMP_FRONT), (0, 0))), F32)
    ks_p = jnp.pad(kvb[:, :, 2 * KV_WIDTH:3 * KV_WIDTH], ((0, 0), (Q_BLOCK, 0), (0, 0)))
    vs_p = jnp.pad(kvb[:, :, 3 * KV_WIDTH:4 * KV_WIDTH], ((0, 0), (Q_BLOCK, 0), (0, 0)))
    kw_p = jnp.pad(kvb[:, :, 4 * KV_WIDTH:5 * KV_WIDTH], ((0, 0), (WINDOW, 0), (0, 0)))
    vw_p = jnp.pad(kvb[:, :, 5 * KV_WIDTH:6 * KV_WIDTH], ((0, 0), (WINDOW, 0), (0, 0)))
    pidx = np.arange(t + Q_BLOCK)
    e = (pidx[:, None] // SEL_BLOCK - Q_BLOCK // SEL_BLOCK == np.arange(LANES)[None, :])
    e = jnp.asarray(e.astype(np.float32), BF16)
    full = lambda arr: pl.BlockSpec((1,) + arr.shape[1:], lambda bi, h, i: (bi, 0, 0))
    return pl.pallas_call(
        _nsa_prompt_kernel,
        grid=(b, N_KV_HEADS, nqb),
        in_specs=[pl.BlockSpec(memory_space=pltpu.SMEM),
                  pl.BlockSpec((1, Q_BLOCK, NSA_WIDTH), lambda bi, h, i: (bi, i, 0)),
                  pl.BlockSpec((1, Q_BLOCK, LANES), lambda bi, h, i: (bi, i, 0)),
                  full(kcmp_p), full(vcmp_p),
                  pl.BlockSpec(selmap.shape, lambda bi, h, i: (0, 0)),
                  full(ks_p), full(vs_p), full(kw_p), full(vw_p),
                  pl.BlockSpec(e.shape, lambda bi, h, i: (0, 0))],
        out_specs=pl.BlockSpec((1, 1, Q_BLOCK, NSA_WIDTH), lambda bi, h, i: (h, bi, i, 0)),
        out_shape=jax.ShapeDtypeStruct((N_KV_HEADS, b, t, NSA_WIDTH), BF16),
        scratch_shapes=[pltpu.VMEM((GROUP, Q_BLOCK, LANES), F32),
                        pltpu.VMEM((GROUP, Q_BLOCK, SLC_NEAR), F32),
                        pltpu.VMEM((GROUP, Q_BLOCK, WIN_KEYS), F32)],
        compiler_params=_cparams(("arbitrary", "arbitrary", "arbitrary")),
        name="nsa_prompt",
    )(rel_bias, q, gn, kcmp_p, vcmp_p, selmap, ks_p, vs_p, kw_p, vw_p, e)


ROWS_PER_TOKEN = 8


def _nsa_sample_cmp_kernel(q_ref, kcmp_ref, vcmp_ref, selmap_ref, rowtab_ref, ocmp_ref, idx_ref, *,
                           past_len, n_tok, n_cmp, n_blocks):
    rows = n_tok * ROWS_PER_TOKEN
    n_cpad = kcmp_ref.shape[1]
    kc = kcmp_ref[0].astype(BF16)
    vc = vcmp_ref[0].astype(BF16)
    ridx = lax.broadcasted_iota(jnp.int32, (rows, n_cpad), 0)
    ccol = lax.broadcasted_iota(jnp.int32, (rows, n_cpad), 1)
    dist = past_len + ridx // ROWS_PER_TOKEN - (CMP_STRIDE * ccol + CMP_LEN - 1)
    bucket = _bucket(dist)
    valid = (dist >= 0) & (ccol < n_cmp) & (ridx % ROWS_PER_TOKEN < GROUP)
    n_bpad = selmap_ref.shape[1]
    blk = lax.broadcasted_iota(jnp.int32, (ROWS_PER_TOKEN, n_bpad), 1)
    cur = (past_len + lax.broadcasted_iota(jnp.int32, (ROWS_PER_TOKEN, n_bpad), 0)) // SEL_BLOCK
    forced = (blk == 0) | (blk == cur) | (blk == cur - 1)
    olane = lax.broadcasted_iota(jnp.int32, (ROWS_PER_TOKEN, LANES), 1)
    for hk in range(N_KV_HEADS):
        qp = (q_ref[0, hk].astype(F32) * (HEAD_DIM ** -0.5)).astype(BF16)
        s = _dot_t(qp, kc) + _bias_from_rows(bucket, rowtab_ref[hk])
        s = jnp.where(valid, s, NEG_INF)
        m = jnp.max(s, axis=-1, keepdims=True)
        p = jnp.where(valid, jnp.exp(s - m), 0.0)
        l = jnp.sum(p, axis=-1, keepdims=True)
        pc = p * jnp.where(l > 0.0, 1.0 / jnp.where(l > 0.0, l, 1.0), 0.0)
        ocmp_ref[0, hk] = _dot(pc.astype(BF16), vc)[:, hk * HEAD_DIM:(hk + 1) * HEAD_DIM]
        ps = jnp.concatenate(
            [jnp.sum(pc[i * ROWS_PER_TOKEN:(i + 1) * ROWS_PER_TOKEN], axis=0, keepdims=True) for i in range(n_tok)]
            + [jnp.zeros((ROWS_PER_TOKEN - n_tok, n_cpad), F32)], axis=0)
        imp = _split_dot(ps, selmap_ref[...])
        val = jnp.where((blk > cur) | (blk >= n_blocks), NEG_INF, imp + FORCE_BONUS * forced.astype(F32))
        picks = jnp.zeros((ROWS_PER_TOKEN, LANES), jnp.int32)
        for n in range(N_SELECT):
            mx = jnp.max(val, axis=-1, keepdims=True)
            first = jnp.min(jnp.where(val == mx, blk, n_bpad), axis=-1, keepdims=True)
            picks = jnp.where(olane == n, first, picks)
            val = jnp.where(blk == first, -3e38, val)
        idx_ref[0, hk] = picks


def _nsa_sample_cmp(qs, kcmp, vcmp, rowtab, past_len, n_tok, n_cmp, n_blocks):
    nb = qs.shape[0]
    n_cpad = kcmp.shape[1]
    n_bpad = -(-n_blocks // LANES) * LANES
    selmap = jnp.asarray(_sel_map(n_cpad, n_cmp, n_bpad, n_blocks), BF16)
    rows = n_tok * ROWS_PER_TOKEN
    return pl.pallas_call(
        functools.partial(_nsa_sample_cmp_kernel, past_len=past_len, n_tok=n_tok, n_cmp=n_cmp, n_blocks=n_blocks),
        grid=(nb,),
        in_specs=[pl.BlockSpec((1, N_KV_HEADS, rows, LANES), lambda b: (b, 0, 0, 0)),
                  pl.BlockSpec((1, n_cpad, LANES), lambda b: (b, 0, 0)),
                  pl.BlockSpec((1, n_cpad, LANES), lambda b: (b, 0, 0)),
                  pl.BlockSpec(selmap.shape, lambda b: (0, 0)),
                  pl.BlockSpec(rowtab.shape, lambda b: (0, 0, 0))],
        out_specs=[pl.BlockSpec((1, N_KV_HEADS, rows, HEAD_DIM), lambda b: (b, 0, 0, 0)),
                   pl.BlockSpec((1, N_KV_HEADS, ROWS_PER_TOKEN, LANES), lambda b: (b, 0, 0, 0))],
        out_shape=[jax.ShapeDtypeStruct((nb, N_KV_HEADS, rows, HEAD_DIM), F32),
                   jax.ShapeDtypeStruct((nb, N_KV_HEADS, ROWS_PER_TOKEN, LANES), jnp.int32)],
        compiler_params=_cparams(("arbitrary",)),
        name="nsa_sample_cmp",
    )(qs, kcmp, vcmp, selmap, rowtab)


def _nsa_sample_slc_kernel(idx_ref, pt_ref, *refs, past_len, n_tok, n_cache_blocks):
    del pt_ref
    k_refs = refs[:N_SELECT]
    v_refs = refs[N_SELECT:2 * N_SELECT]
    (q_ref, ocmp_ref, gn_ref, rowtab_ref, newk_ref, newv_ref, swk_ref, swv_ref, nwk_ref, nwv_ref,
     out_ref) = refs[2 * N_SELECT:]
    b = pl.program_id(0)
    hk = pl.program_id(1)
    tok = pl.program_id(2)
    t = past_len + tok
    base = ((b * N_KV_HEADS + hk) * n_tok + tok) * N_SELECT
    qp = (q_ref[0, 0, 0].astype(F32) * (HEAD_DIM ** -0.5)).astype(BF16)
    rowtab = rowtab_ref[0]
    rvalid = lax.broadcasted_iota(jnp.int32, (ROWS_PER_TOKEN, 1), 0) < GROUP
    init = (jnp.full((ROWS_PER_TOKEN, 1), NEG_INF, F32), jnp.zeros((ROWS_PER_TOKEN, 1), F32),
            jnp.zeros((ROWS_PER_TOKEN, HEAD_DIM), F32))
    halves = PAGE // SEL_BLOCK

    n_keys = N_SELECT * PAGE
    lane = lax.broadcasted_iota(jnp.int32, (ROWS_PER_TOKEN, n_keys), 1)
    row = lane % PAGE
    pos = row
    in_block = jnp.zeros((ROWS_PER_TOKEN, n_keys), jnp.bool_)
    kts, vts = [], []
    for n in range(N_SELECT):
        blk = idx_ref[base + n]
        is_new = blk >= n_cache_blocks
        kts.append(jnp.where(is_new, newk_ref[0, 0], k_refs[n][0, 0]).astype(BF16))
        vts.append(jnp.where(is_new, newv_ref[0, 0], v_refs[n][0, 0]).astype(BF16))
        slot = lane // PAGE == n
        pos = jnp.where(slot, row + (blk // halves) * PAGE, pos)
        in_block = in_block | (slot & (row // SEL_BLOCK == blk % halves))
    dist = t - pos
    s = _dot(qp, jnp.concatenate(kts, axis=1)) + _bias_from_rows(_bucket(dist), rowtab)
    mask = (dist >= 0) & in_block & rvalid
    o_slc = _softmax_finish(_softmax_update(init, jnp.where(mask, s, NEG_INF), jnp.concatenate(vts, axis=1),
                                            mask=mask, v_transposed=True))

    n_win = swk_ref.shape[3]
    kw = jnp.concatenate([swk_ref[0, 0], nwk_ref[0, 0]], axis=1).astype(BF16)
    vw = jnp.concatenate([swv_ref[0, 0], nwv_ref[0, 0]], axis=1).astype(BF16)
    wl = n_win + nwk_ref.shape[3]
    wpos = past_len - n_win + lax.broadcasted_iota(jnp.int32, (ROWS_PER_TOKEN, wl), 1)
    wdist = t - wpos
    s = _dot(qp, kw) + _bias_from_rows(_bucket(wdist), rowtab)
    mask = (wdist >= 0) & (wdist < WINDOW) & (wpos >= 0) & rvalid
    o_win = _softmax_finish(_softmax_update(init, jnp.where(mask, s, NEG_INF), vw, mask=mask, v_transposed=True))

    gs = jax.nn.sigmoid(gn_ref[0, 0, 0])
    out_ref[0, 0, 0] = gs[:, 0:1] * ocmp_ref[0, 0, 0] + gs[:, 1:2] * o_slc + gs[:, 2:3] * o_win


def _nsa_sample_slc(idx, page_table, cache_k, cache_v, q3, ocmp, gn3, rowtab, newk, newv, swk, swv, nwk, nwv,
                    past_len, n_tok):
    nb, npages = page_table.shape
    halves = PAGE // SEL_BLOCK
    n_cache_blocks = npages * halves
    ck = cache_k.transpose(0, 2, 3, 1)
    cv = cache_v.transpose(0, 2, 3, 1)

    def blk_spec(n):
        def imap(b, h, s, idx_ref, pt_ref):
            j = jnp.minimum(idx_ref[((b * N_KV_HEADS + h) * n_tok + s) * N_SELECT + n], n_cache_blocks - 1)
            return (pt_ref[b * npages + j // halves], h, 0, 0)
        return pl.BlockSpec((1, 1, HEAD_DIM, PAGE), imap)

    tile = lambda w: pl.BlockSpec((1, 1, 1, ROWS_PER_TOKEN, w), lambda b, h, s, i, p: (b, h, s, 0, 0))
    per_bh = lambda arr: pl.BlockSpec((1, 1) + arr.shape[2:], lambda b, h, s, i, p: (b, h, 0, 0))
    grid_spec = pltpu.PrefetchScalarGridSpec(
        num_scalar_prefetch=2,
        grid=(nb, N_KV_HEADS, n_tok),
        in_specs=[blk_spec(n) for n in range(N_SELECT)] * 2
        + [tile(HEAD_DIM), tile(HEAD_DIM), tile(LANES),
           pl.BlockSpec((1, ROWS_PER_TOKEN, N_BUCKETS), lambda b, h, s, i, p: (h, 0, 0)),
           per_bh(newk), per_bh(newv), per_bh(swk), per_bh(swv), per_bh(nwk), per_bh(nwv)],
        out_specs=tile(HEAD_DIM),
    )
    return pl.pallas_call(
        functools.partial(_nsa_sample_slc_kernel, past_len=past_len, n_tok=n_tok, n_cache_blocks=n_cache_blocks),
        grid_spec=grid_spec,
        out_shape=jax.ShapeDtypeStruct((nb, N_KV_HEADS, n_tok, ROWS_PER_TOKEN, HEAD_DIM), F32),
        compiler_params=_cparams(("arbitrary", "arbitrary", "arbitrary")),
        name="nsa_sample_slc",
    )(idx.reshape(-1), page_table.reshape(-1), *([ck] * N_SELECT), *([cv] * N_SELECT),
      q3, ocmp, gn3, rowtab, newk, newv, swk, swv, nwk, nwv)


def _merge_tail(x, a, c, gates, wn_ref, wc_ref, wo_ref):
    ga = gates[:, :D_MODEL]
    gb = gates[:, D_MODEL:]
    m = (jax.nn.sigmoid(ga) * _dot(a, wn_ref[...])
         + jax.nn.sigmoid(gb) * _dot(c.astype(BF16), wc_ref[...]))
    return x + _dot(m.astype(BF16), wo_ref[...])


def _merge_prompt_kernel(x_ref, a_ref, conv_ref, halo_ref, gate_ref, cw_ref, cb_ref, wn_ref, wc_ref, wo_ref,
                         out_ref, cst_ref, *, tiles_per_batch):
    i = pl.program_id(0)
    first = i % tiles_per_batch == 0
    conv = conv_ref[...]
    bg = conv[:, :CONV_DIM]
    u = conv[:, CONV_DIM:2 * CONV_DIM] * conv[:, 2 * CONV_DIM:]
    halo = halo_ref[...]
    uh = jnp.where(first, 0.0, halo[:, CONV_DIM:2 * CONV_DIM] * halo[:, 2 * CONV_DIM:])
    tm = u.shape[0]
    row = lax.broadcasted_iota(jnp.int32, (tm, CONV_DIM), 0)
    u1 = jnp.where(row == 0, uh[7:8], pltpu.roll(u, 1, axis=0))
    u2 = pltpu.roll(u, 2, axis=0)
    u2 = jnp.where(row == 0, uh[6:7], jnp.where(row == 1, uh[7:8], u2))
    cw = cw_ref[...]
    y = ((cb_ref[...] + cw[0:1] * u2) + cw[1:2] * u1) + cw[2:3] * u
    c = bg * y
    a = a_ref[0] + a_ref[1]
    out_ref[...] = _merge_tail(x_ref[...], a, c, gate_ref[...], wn_ref, wc_ref, wo_ref)
    cst_ref[0] = u[tm - 8:, :]


def _merge_prompt(x2d, a2, conv3, gates, conv_w, conv_b, wn, wc, wo, t):
    n = x2d.shape[0]
    tm = min(512, t)
    assert t % tm == 0
    tpb = t // tm
    row = lambda w: pl.BlockSpec((tm, w), lambda i: (i, 0))
    const = lambda arr: pl.BlockSpec(arr.shape, lambda i: (0,) * arr.ndim)
    return pl.pallas_call(
        functools.partial(_merge_prompt_kernel, tiles_per_batch=tpb),
        grid=(n // tm,),
        in_specs=[row(D_MODEL),
                  pl.BlockSpec((N_KV_HEADS, tm, NSA_WIDTH), lambda i: (0, i, 0)),
                  row(3 * CONV_DIM),
                  pl.BlockSpec((8, 3 * CONV_DIM), lambda i: (jnp.maximum(i * (tm // 8) - 1, 0), 0)),
                  row(2 * D_MODEL),
                  const(conv_w), const(conv_b), const(wn), const(wc), const(wo)],
        out_specs=[row(D_MODEL), pl.BlockSpec((1, 8, CONV_DIM), lambda i: (i // tpb, 0, 0))],
        out_shape=[jax.ShapeDtypeStruct((n, D_MODEL), F32),
                   jax.ShapeDtypeStruct((n // t, 8, CONV_DIM), F32)],
        compiler_params=_cparams(("arbitrary",)),
        name="merge_prompt",
    )(x2d, a2, conv3, conv3, gates, conv_w, conv_b, wn, wc, wo)


def _merge_sample_kernel(x_ref, a_ref, conv_ref, past_ref, gate_ref, cw_ref, cb_ref, wn_ref, wc_ref, wo_ref,
                         out_ref, cst_ref, *, n_tok, nb):
    conv = conv_ref[...]
    bg = conv[:, :CONV_DIM]
    u = conv[:, CONV_DIM:2 * CONV_DIM] * conv[:, 2 * CONV_DIM:]
    up = jnp.concatenate([past_ref[...], u], axis=0)
    cw = cw_ref[...]
    y = cb_ref[...]
    for k in range(CONV_WIDTH):
        y = y + cw[k:k + 1] * up[k * nb:(k + n_tok) * nb]
    c = bg * y
    out_ref[...] = _merge_tail(x_ref[...], a_ref[...].astype(BF16), c, gate_ref[...], wn_ref, wc_ref, wo_ref)
    cst_ref[...] = up[n_tok * nb:]


def _merge_sample(x_tm, a_tm, conv_tm, past_tm, gates_tm, conv_w, conv_b, wn, wc, wo, n_tok, nb):
    n = x_tm.shape[0]
    full = lambda arr: pl.BlockSpec(arr.shape, lambda i: (0,) * arr.ndim)
    args = (x_tm, a_tm, conv_tm, past_tm, gates_tm, conv_w, conv_b, wn, wc, wo)
    return pl.pallas_call(
        functools.partial(_merge_sample_kernel, n_tok=n_tok, nb=nb),
        grid=(1,),
        in_specs=[full(a) for a in args],
        out_specs=[pl.BlockSpec((n, D_MODEL), lambda i: (0, 0)),
                   pl.BlockSpec(((CONV_WIDTH - 1) * nb, CONV_DIM), lambda i: (0, 0))],
        out_shape=[jax.ShapeDtypeStruct((n, D_MODEL), F32),
                   jax.ShapeDtypeStruct(((CONV_WIDTH - 1) * nb, CONV_DIM), F32)],
        compiler_params=_cparams(("arbitrary",)),
        name="merge_sample",
    )(*args)


def _rms(x, g):
    return (x * lax.rsqrt(jnp.mean(x * x, axis=-1, keepdims=True) + RMS_EPS)) * g


def _mlp_kernel(x_ref, g_ref, wu_ref, wd_ref, gf_ref, out_ref, h_ref, acc_ref):
    j = pl.program_id(1)

    @pl.when(j == 0)
    def _():
        h_ref[...] = _rms(x_ref[...], g_ref[...]).astype(BF16)

    up = jnp.square(jnp.maximum(_dot(h_ref[...], wu_ref[...]), 0.0)).astype(BF16)
    part = _dot(up, wd_ref[...])

    @pl.when(j == 0)
    def _():
        acc_ref[...] = part

    @pl.when(j > 0)
    def _():
        acc_ref[...] += part

    @pl.when(j == pl.num_programs(1) - 1)
    def _():
        out_ref[...] = _rms(x_ref[...] + acc_ref[...], gf_ref[...])


def _mlp(x2d, g_mlp, w_up, w_down, g_final):
    n = x2d.shape[0]
    tm = min(512, n)
    tf = 1024
    assert n % tm == 0 and D_FF % tf == 0
    return pl.pallas_call(
        _mlp_kernel,
        grid=(n // tm, D_FF // tf),
        in_specs=[pl.BlockSpec((tm, D_MODEL), lambda i, j: (i, 0)),
                  pl.BlockSpec((1, D_MODEL), lambda i, j: (0, 0)),
                  pl.BlockSpec((D_MODEL, tf), lambda i, j: (0, j)),
                  pl.BlockSpec((tf, D_MODEL), lambda i, j: (j, 0)),
                  pl.BlockSpec((1, D_MODEL), lambda i, j: (0, 0))],
        out_specs=pl.BlockSpec((tm, D_MODEL), lambda i, j: (i, 0)),
        out_shape=jax.ShapeDtypeStruct((n, D_MODEL), F32),
        scratch_shapes=[pltpu.VMEM((tm, D_MODEL), BF16), pltpu.VMEM((tm, D_MODEL), F32)],
        compiler_params=_cparams(("arbitrary", "arbitrary")),
        name="mlp",
    )(x2d, g_mlp, w_up, w_down, g_final)


def _kv_rows(z, lead):
    return z.reshape(lead + (N_KV_HEADS, HEAD_DIM))


def _layer(xp, xs, caches, states, page_table, w, rel_bias):
    (cache_cmp_k, cache_cmp_v, cache_slc_k, cache_slc_v) = caches
    (state_win_k, state_win_v, state_conv) = states
    nbp, t, _ = xp.shape
    nbs, n_tok, _ = xs.shape
    npages = page_table.shape[1]
    past_len = npages * PAGE
    assert t % PAGE == 0 and n_tok <= GROUP and state_win_k.shape[1] >= n_tok

    w_all = _prep_w_in(w["w_in"])
    g_attn = w["g_attn"].reshape(1, D_MODEL)
    wn = w["w_nsa_out"].reshape(N_KV_HEADS, GROUP, HEAD_DIM, D_MODEL).transpose(1, 0, 2, 3)
    wn = wn.reshape(NSA_WIDTH, D_MODEL).astype(BF16)
    wc = w["w_conv_out"].astype(BF16)
    wo = w["w_o"].astype(BF16)
    wu = w["w_up"].astype(BF16)
    wd = w["w_down"].astype(BF16)
    g_mlp = w["g_mlp"].reshape(1, D_MODEL)
    g_final = w["g_final"].reshape(1, D_MODEL)
    conv_w = w["conv_w"]
    conv_b = w["conv_b"].reshape(1, CONV_DIM)
    cmp_k = (w["cmp_pe_k"], w["cmp_w1_k"], w["cmp_w2_k"])
    cmp_v = (w["cmp_pe_v"], w["cmp_w1_v"], w["cmp_w2_v"])

    n_p = nbp * t
    (q, kc, vc, ks, vs, kw, vw, kvb, gn, conv3, gates) = _proj(xp.reshape(n_p, D_MODEL), g_attn, w_all)
    ident = jnp.arange(nbp * (t // PAGE), dtype=jnp.int32).reshape(nbp, t // PAGE)
    kcmp = _compress(kc.reshape(n_p // PAGE, PAGE, KV_WIDTH), 0, ident, *cmp_k)
    vcmp = _compress(vc.reshape(n_p // PAGE, PAGE, KV_WIDTH), 0, ident, *cmp_v)
    a2 = _nsa_prompt(q.reshape(nbp, t, NSA_WIDTH), gn.reshape(nbp, t, LANES), kcmp, vcmp,
                     kvb.reshape(nbp, t, 6 * KV_WIDTH), rel_bias)
    x1, cst = _merge_prompt(xp.reshape(n_p, D_MODEL), a2.reshape(N_KV_HEADS, n_p, NSA_WIDTH),
                            conv3, gates, conv_w, conv_b, wn, wc, wo, t)
    yp = _mlp(x1, g_mlp, wu, wd, g_final).reshape(nbp, t, D_MODEL)
    win_p = min(WINDOW, t)
    new_p = tuple(_kv_rows(z, (nbp, t)) for z in (kc, vc, ks, vs)) + (
        _kv_rows(kw, (nbp, t))[:, t - win_p:], _kv_rows(vw, (nbp, t))[:, t - win_p:],
        cst[:, 8 - (CONV_WIDTH - 1):, :])

    n_s = nbs * n_tok
    (qs, kcs, vcs, kss, vss, kws, vws, _, gns, conv3s, gatess) = _proj(xs.reshape(n_s, D_MODEL), g_attn, w_all)
    n_rows = past_len + n_tok
    n_chunk = n_rows // CMP_STRIDE
    assert n_chunk == past_len // CMP_STRIDE, "new rows must not complete a compression chunk"
    n_cmp = n_chunk - CMP_RATIO + 1
    n_blocks = -(-n_rows // SEL_BLOCK)
    kcmp_s = _compress_t(cache_cmp_k, page_table, *cmp_k)
    vcmp_s = _compress_t(cache_cmp_v, page_table, *cmp_v)

    lane = np.arange(LANES)
    q5 = qs.reshape(nbs, n_tok, GROUP, LANES)
    q3 = jnp.stack([jnp.where(jnp.asarray((lane // HEAD_DIM) == h), q5, jnp.zeros_like(q5))
                    for h in range(N_KV_HEADS)], axis=1)
    q3 = jnp.pad(q3, ((0, 0), (0, 0), (0, 0), (0, ROWS_PER_TOKEN - GROUP), (0, 0)))
    tab = rel_bias.T.reshape(N_KV_HEADS, GROUP, N_BUCKETS)
    rowtab3 = jnp.pad(tab, ((0, 0), (0, ROWS_PER_TOKEN - GROUP), (0, 0)))
    rowtab = jnp.tile(rowtab3, (1, n_tok, 1))
    ocmp, idx = _nsa_sample_cmp(q3.reshape(nbs, N_KV_HEADS, n_tok * ROWS_PER_TOKEN, LANES), kcmp_s, vcmp_s,
                                rowtab, past_len, n_tok, n_cmp, n_blocks)
    idx = idx[:, :, :n_tok, :N_SELECT]
    gn3 = gns[:, :3 * N_HEADS].reshape(nbs, n_tok, N_KV_HEADS, GROUP, 3).transpose(0, 2, 1, 3, 4)
    gn3 = jnp.pad(gn3, ((0, 0), (0, 0), (0, 0), (0, ROWS_PER_TOKEN - GROUP), (0, LANES - 3)))
    q3s = jnp.pad(qs.reshape(nbs, n_tok, GROUP, N_KV_HEADS, HEAD_DIM).transpose(0, 3, 1, 2, 4),
                  ((0, 0), (0, 0), (0, 0), (0, ROWS_PER_TOKEN - GROUP), (0, 0)))
    ocmp5 = ocmp.reshape(nbs, N_KV_HEADS, n_tok, ROWS_PER_TOKEN, HEAD_DIM)
    new_t = lambda z: jnp.pad(z.reshape(nbs, n_tok, N_KV_HEADS, HEAD_DIM).transpose(0, 2, 3, 1),
                              ((0, 0), (0, 0), (0, 0), (0, LANES - n_tok)))
    a_s = _nsa_sample_slc(idx, page_table, cache_slc_k, cache_slc_v, q3s, ocmp5, gn3, rowtab3,
                          new_t(kss), new_t(vss),
                          state_win_k.transpose(0, 2, 3, 1), state_win_v.transpose(0, 2, 3, 1),
                          new_t(kws), new_t(vws), past_len, n_tok)
    a_tm = a_s[:, :, :, :GROUP, :].transpose(2, 0, 3, 1, 4).reshape(n_s, NSA_WIDTH)
    to_tm = lambda z: z.reshape(nbs, n_tok, -1).transpose(1, 0, 2).reshape(n_s, -1)
    past_tm = state_conv.transpose(1, 0, 2).reshape((CONV_WIDTH - 1) * nbs, CONV_DIM)
    x1s, csts = _merge_sample(to_tm(xs), a_tm, to_tm(conv3s), past_tm, to_tm(gatess),
                              conv_w, conv_b, wn, wc, wo, n_tok, nbs)
    ys = _mlp(x1s, g_mlp, wu, wd, g_final).reshape(n_tok, nbs, D_MODEL).transpose(1, 0, 2)
    win_buf = state_win_k.shape[1]
    kw_all = jnp.concatenate([state_win_k, _kv_rows(kws, (nbs, n_tok))], axis=1)[:, -win_buf:]
    vw_all = jnp.concatenate([state_win_v, _kv_rows(vws, (nbs, n_tok))], axis=1)[:, -win_buf:]
    new_s = tuple(_kv_rows(z, (nbs, n_tok)) for z in (kcs, vcs, kss, vss)) + (
        kw_all, vw_all, csts.reshape(CONV_WIDTH - 1, nbs, CONV_DIM).transpose(1, 0, 2))
    return yp, ys, new_p, new_s


def kernel(x_prompt, x_sample, cache_cmp_k, cache_cmp_v, cache_slc_k, cache_slc_v, state_win_k, state_win_v,
           state_conv, page_table, g_attn, w_in, cmp_pe_k, cmp_w1_k, cmp_w2_k, cmp_pe_v, cmp_w1_v, cmp_w2_v,
           conv_w, conv_b, w_nsa_out, w_conv_out, w_o, g_mlp, w_up, w_down, rel_bias, g_final):
    depth = w_in.shape[0]
    assert depth == 1, "the final norm is fused into the last layer's MLP kernel"
    names = ("g_attn", "w_in", "cmp_pe_k", "cmp_w1_k", "cmp_w2_k", "cmp_pe_v", "cmp_w1_v", "cmp_w2_v",
             "conv_w", "conv_b", "w_nsa_out", "w_conv_out", "w_o", "g_mlp", "w_up", "w_down")
    vals = (g_attn, w_in, cmp_pe_k, cmp_w1_k, cmp_w2_k, cmp_pe_v, cmp_w1_v, cmp_w2_v,
            conv_w, conv_b, w_nsa_out, w_conv_out, w_o, g_mlp, w_up, w_down)
    l = 0
    w = {k: v[l] for k, v in zip(names, vals)}
    w["g_final"] = g_final
    yp, ys, new_p, new_s = _layer(
        x_prompt, x_sample,
        (cache_cmp_k[l], cache_cmp_v[l], cache_slc_k[l], cache_slc_v[l]),
        (state_win_k[l], state_win_v[l], state_conv[l]), page_table, w, rel_bias)
    return (yp, ys) + tuple(z[None] for z in new_p) + tuple(z[None] for z in new_s)
```

```python
import functools
import math

import numpy as np
import jax
import jax.numpy as jnp
from jax import lax
from jax.experimental import pallas as pl
from jax.experimental.pallas import tpu as pltpu

F32 = jnp.float32
BF16 = jnp.bfloat16

D_MODEL = 1024
N_HEADS = 8
HEAD_DIM = 64
N_KV_HEADS = 2
GROUP = N_HEADS // N_KV_HEADS
KV_WIDTH = N_KV_HEADS * HEAD_DIM
NSA_WIDTH = N_HEADS * HEAD_DIM
CMP_LEN = 32
CMP_STRIDE = 16
CMP_RATIO = CMP_LEN // CMP_STRIDE
CMP_HIDDEN = 256
SEL_BLOCK = 64
N_SELECT = 16
WINDOW = 512
CONV_DIM = 512
CONV_WIDTH = 3
D_FF = 4 * D_MODEL
N_BUCKETS = 32
MAX_DISTANCE = 128
Q_BLOCK = 128
PAGE = 128
RMS_EPS = 1e-6
NEG_INF = -1e30
FORCE_BONUS = 1e4
LANES = 128
VMEM_LIMIT = 56 * 1024 * 1024


def _bucket_thresholds():
    max_exact = N_BUCKETS // 2
    n = np.arange(max_exact, 4 * MAX_DISTANCE).astype(np.float32)
    large = max_exact + (np.log(n / np.float32(max_exact)) / np.float32(math.log(MAX_DISTANCE / max_exact))
                         * np.float32(N_BUCKETS - max_exact)).astype(np.int32)
    large = np.minimum(large, N_BUCKETS - 1)
    thr = []
    for k in range(max_exact + 1, N_BUCKETS):
        thr.append(int(np.arange(max_exact, 4 * MAX_DISTANCE)[np.argmax(large >= k)]))
    return tuple(thr)


BUCKET_THR = _bucket_thresholds()


def _bucket(dist):
    n = jnp.maximum(dist, 0)
    big = jnp.full(n.shape, N_BUCKETS // 2, jnp.int32)
    for thr in BUCKET_THR:
        big = big + (n >= thr).astype(jnp.int32)
    return jnp.where(n < N_BUCKETS // 2, n, big)


def _bias_from_scalars(bucket, table_fn):
    out = jnp.zeros(bucket.shape, F32)
    for k in range(N_BUCKETS):
        out = jnp.where(bucket == k, table_fn(k), out)
    return out


def _bias_from_rows(bucket, rowtab):
    out = jnp.zeros(bucket.shape, F32)
    for k in range(N_BUCKETS):
        out = jnp.where(bucket == k, rowtab[:, k:k + 1], out)
    return out


def _dot(a, b):
    return jnp.dot(a, b, preferred_element_type=F32)


def _dot_t(a, b):
    return lax.dot_general(a, b, (((1,), (1,)), ((), ())), preferred_element_type=F32)


def _split_dot(p, m):
    hi = p.astype(BF16)
    lo = (p - hi.astype(F32)).astype(BF16)
    return _dot(hi, m) + _dot(lo, m)


def _cparams(sem):
    return pltpu.CompilerParams(dimension_semantics=sem, vmem_limit_bytes=VMEM_LIMIT)


PROJ_WIDTHS = (NSA_WIDTH, 6 * KV_WIDTH, LANES, 3 * CONV_DIM, 2 * D_MODEL)
PROJ_COLS = sum(PROJ_WIDTHS)


def _proj_kernel(x_ref, g_ref, w_ref, q_ref, kc_ref, vc_ref, ks_ref, vs_ref, kw_ref, vw_ref,
                 kvb_ref, gn_ref, conv_ref, gate_ref):
    x = x_ref[...]
    y = x * lax.rsqrt(jnp.mean(x * x, axis=-1, keepdims=True) + RMS_EPS)
    hb = (y * g_ref[...]).astype(BF16)
    q_ref[...] = _dot(hb, w_ref[:, 0:NSA_WIDTH]).astype(q_ref.dtype)
    off = NSA_WIDTH
    kv = _dot(hb, w_ref[:, off:off + 6 * KV_WIDTH])
    for i, r in enumerate((kc_ref, vc_ref, ks_ref, vs_ref, kw_ref, vw_ref)):
        r[...] = kv[:, i * KV_WIDTH:(i + 1) * KV_WIDTH]
    kvb_ref[...] = kv.astype(BF16)
    off += 6 * KV_WIDTH
    gn_ref[...] = _dot(hb, w_ref[:, off:off + LANES])
    off += LANES
    conv_ref[...] = _dot(hb, w_ref[:, off:off + 3 * CONV_DIM])
    off += 3 * CONV_DIM
    gate_ref[...] = _dot(hb, w_ref[:, off:off + 2 * D_MODEL])


def _proj(x2d, g, w_all):
    n = x2d.shape[0]
    tm = min(256, n)
    assert n % tm == 0
    row = lambda w: pl.BlockSpec((tm, w), lambda i: (i, 0))
    out_shapes = ([jax.ShapeDtypeStruct((n, NSA_WIDTH), BF16)]
                  + [jax.ShapeDtypeStruct((n, KV_WIDTH), F32)] * 6
                  + [jax.ShapeDtypeStruct((n, 6 * KV_WIDTH), BF16),
                     jax.ShapeDtypeStruct((n, LANES), F32),
                     jax.ShapeDtypeStruct((n, 3 * CONV_DIM), F32),
                     jax.ShapeDtypeStruct((n, 2 * D_MODEL), F32)])
    out_specs = ([row(NSA_WIDTH)] + [row(KV_WIDTH)] * 6
                 + [row(6 * KV_WIDTH), row(LANES), row(3 * CONV_DIM), row(2 * D_MODEL)])
    return pl.pallas_call(
        _proj_kernel,
        grid=(n // tm,),
        in_specs=[row(D_MODEL),
                  pl.BlockSpec((1, D_MODEL), lambda i: (0, 0)),
                  pl.BlockSpec((D_MODEL, PROJ_COLS), lambda i: (0, 0))],
        out_specs=out_specs,
        out_shape=out_shapes,
        compiler_params=_cparams(("arbitrary",)),
        name="proj",
    )(x2d, g, w_all)


def _prep_w_in(w_in):
    sizes = (NSA_WIDTH,) + (KV_WIDTH,) * 6 + (3 * N_HEADS, CONV_DIM, CONV_DIM, CONV_DIM, D_MODEL, D_MODEL)
    offs = np.concatenate([[0], np.cumsum(sizes)])
    wq = w_in[:, :NSA_WIDTH].reshape(D_MODEL, N_KV_HEADS, GROUP, HEAD_DIM)
    wq = wq.transpose(0, 2, 1, 3).reshape(D_MODEL, NSA_WIDTH)
    wkv = w_in[:, offs[1]:offs[7]]
    wgn = jnp.pad(w_in[:, offs[7]:offs[8]], ((0, 0), (0, LANES - 3 * N_HEADS)))
    wrest = w_in[:, offs[8]:]
    return jnp.concatenate([wq, wkv, wgn, wrest], axis=1).astype(BF16)


CHUNKS_PER_PAGE = PAGE // CMP_STRIDE


def _compress_kernel(pt_ref, *refs, pp):
    del pt_ref
    page_refs = refs[:pp + 1]
    wbd_ref, pe_ref, w1_ref, w2_ref, out_ref, acc_ref, pehid_ref = refs[pp + 1:]

    @pl.when((pl.program_id(0) == 0) & (pl.program_id(1) == 0))
    def _():
        pehid_ref[...] = _dot(pe_ref[...], w1_ref[...])

    cols = []
    for r in range(CMP_STRIDE):
        xr = jnp.concatenate([p[0, pl.ds(r, CHUNKS_PER_PAGE, stride=CMP_STRIDE), :] for p in page_refs], axis=0)
        cols.append(xr.astype(BF16))
    x = jnp.concatenate(cols, axis=1)
    acc_ref[...] = _dot(x, wbd_ref[...])
    nc = CHUNKS_PER_PAGE * pp
    pe = pehid_ref[0:1, :]
    hids = []
    for h in range(N_KV_HEADS):
        base = h * CMP_RATIO * CMP_HIDDEN
        p0 = acc_ref[0:nc, base:base + CMP_HIDDEN]
        p1 = acc_ref[pl.ds(1, nc), base + CMP_HIDDEN:base + 2 * CMP_HIDDEN]
        hids.append(jax.nn.gelu((pe + p0) + p1).astype(BF16))
    out_ref[0] = _dot(jnp.concatenate(hids, axis=1), w2_ref[...])


def _compress(pages, colblk, page_table, pe, w1, w2):
    nb, npages = page_table.shape
    pp = min(32, npages)
    assert npages % pp == 0
    nc = CHUNKS_PER_PAGE * pp
    w1r = w1.reshape(CMP_RATIO, CMP_STRIDE, HEAD_DIM, CMP_HIDDEN)
    blk = w1r.transpose(1, 2, 0, 3).reshape(CMP_STRIDE, HEAD_DIM, CMP_RATIO * CMP_HIDDEN)
    z = jnp.zeros_like(blk)
    wbd = jnp.concatenate([jnp.concatenate([blk, z], axis=2), jnp.concatenate([z, blk], axis=2)], axis=1)
    wbd = wbd.reshape(CMP_STRIDE * KV_WIDTH, N_KV_HEADS * CMP_RATIO * CMP_HIDDEN).astype(BF16)
    z2 = jnp.zeros_like(w2)
    w2bd = jnp.concatenate([jnp.concatenate([w2, z2], axis=1), jnp.concatenate([z2, w2], axis=1)], axis=0).astype(BF16)
    pe8 = jnp.broadcast_to(pe.reshape(1, CMP_LEN * HEAD_DIM), (8, CMP_LEN * HEAD_DIM)).astype(BF16)

    def page_spec(p):
        def imap(b, s, pt):
            return (pt[b, jnp.minimum(s * pp + p, npages - 1)], 0, colblk)
        return pl.BlockSpec((1, PAGE, KV_WIDTH), imap)

    const = lambda shape: pl.BlockSpec(shape, lambda b, s, pt: (0,) * len(shape))
    grid_spec = pltpu.PrefetchScalarGridSpec(
        num_scalar_prefetch=1,
        grid=(nb, npages // pp),
        in_specs=[page_spec(p) for p in range(pp + 1)]
        + [const(wbd.shape), const(pe8.shape), const((CMP_LEN * HEAD_DIM, CMP_HIDDEN)), const(w2bd.shape)],
        out_specs=pl.BlockSpec((1, nc, KV_WIDTH), lambda b, s, pt: (b, s, 0)),
        scratch_shapes=[pltpu.VMEM((nc + 8, N_KV_HEADS * CMP_RATIO * CMP_HIDDEN), F32),
                        pltpu.VMEM((8, CMP_HIDDEN), F32)],
    )
    return pl.pallas_call(
        functools.partial(_compress_kernel, pp=pp),
        grid_spec=grid_spec,
        out_shape=jax.ShapeDtypeStruct((nb, npages * CHUNKS_PER_PAGE, KV_WIDTH), F32),
        compiler_params=_cparams(("arbitrary", "arbitrary")),
        name="compress",
    )(page_table, *([pages] * (pp + 1)), wbd, pe8, w1.astype(BF16), w2bd)


def _compress_t_kernel(pt_ref, *refs, pp):
    del pt_ref
    page_refs = refs[:pp + 1]
    w1p_ref, pe_ref, w1_ref, w2_ref, out_ref, xt_ref, acc_ref, pehid_ref = refs[pp + 1:]

    @pl.when((pl.program_id(0) == 0) & (pl.program_id(1) == 0))
    def _():
        pehid_ref[...] = _dot(pe_ref[...], w1_ref[...])

    for p, page in enumerate(page_refs):
        for h in range(N_KV_HEADS):
            kt = page[0, h]
            z = jnp.concatenate([kt, pltpu.roll(kt, PAGE - 1, axis=1)], axis=0)
            xt_ref[h, p * PAGE:(p + 1) * PAGE, :] = z.T
    nc = CHUNKS_PER_PAGE * pp
    m = nc + CHUNKS_PER_PAGE
    pe = pehid_ref[0:1, :]
    hids = []
    for h in range(N_KV_HEADS):
        x = jnp.concatenate([xt_ref[h, pl.ds(r, m, stride=CMP_STRIDE), :].astype(BF16)
                             for r in range(0, CMP_STRIDE, 2)], axis=1)
        acc_ref[h] = _dot(x, w1p_ref[...])
        p0 = acc_ref[h, 0:nc, 0:CMP_HIDDEN]
        p1 = acc_ref[h, pl.ds(1, nc), CMP_HIDDEN:2 * CMP_HIDDEN]
        hids.append(jax.nn.gelu((pe + p0) + p1).astype(BF16))
    out_ref[0] = _dot(jnp.concatenate(hids, axis=1), w2_ref[...])


def _compress_t(cache, page_table, pe, w1, w2):
    nb, npages = page_table.shape
    pp = min(32, npages)
    assert npages % pp == 0
    nc = CHUNKS_PER_PAGE * pp
    pages = cache.transpose(0, 2, 3, 1)
    w1p = w1.reshape(CMP_RATIO, CMP_STRIDE * HEAD_DIM, CMP_HIDDEN).transpose(1, 0, 2)
    w1p = w1p.reshape(CMP_STRIDE * HEAD_DIM, CMP_RATIO * CMP_HIDDEN).astype(BF16)
    z2 = jnp.zeros_like(w2)
    w2bd = jnp.concatenate([jnp.concatenate([w2, z2], axis=1), jnp.concatenate([z2, w2], axis=1)], axis=0).astype(BF16)
    pe8 = jnp.broadcast_to(pe.reshape(1, CMP_LEN * HEAD_DIM), (8, CMP_LEN * HEAD_DIM)).astype(BF16)

    def page_spec(p):
        def imap(b, s, pt):
            return (pt[b, jnp.minimum(s * pp + p, npages - 1)], 0, 0, 0)
        return pl.BlockSpec((1, N_KV_HEADS, HEAD_DIM, PAGE), imap)

    const = lambda shape: pl.BlockSpec(shape, lambda b, s, pt: (0,) * len(shape))
    grid_spec = pltpu.PrefetchScalarGridSpec(
        num_scalar_prefetch=1,
        grid=(nb, npages // pp),
        in_specs=[page_spec(p) for p in range(pp + 1)]
        + [const(w1p.shape), const(pe8.shape), const((CMP_LEN * HEAD_DIM, CMP_HIDDEN)), const(w2bd.shape)],
        out_specs=pl.BlockSpec((1, nc, KV_WIDTH), lambda b, s, pt: (b, s, 0)),
        scratch_shapes=[pltpu.VMEM((N_KV_HEADS, (pp + 1) * PAGE, 2 * HEAD_DIM), F32),
                        pltpu.VMEM((N_KV_HEADS, nc + CHUNKS_PER_PAGE, CMP_RATIO * CMP_HIDDEN), F32),
                        pltpu.VMEM((8, CMP_HIDDEN), F32)],
    )
    return pl.pallas_call(
        functools.partial(_compress_t_kernel, pp=pp),
        grid_spec=grid_spec,
        out_shape=jax.ShapeDtypeStruct((nb, npages * CHUNKS_PER_PAGE, KV_WIDTH), F32),
        compiler_params=_cparams(("arbitrary", "arbitrary")),
        name="compress_t",
    )(page_table, *([pages] * (pp + 1)), w1p, pe8, w1.astype(BF16), w2bd)


def _sel_map(n_rows, n_cmp, n_cols, n_blocks):
    c0 = np.arange(n_rows)[:, None] * CMP_STRIDE
    s0 = np.arange(n_cols)[None, :] * SEL_BLOCK
    ov = np.minimum(c0 + CMP_LEN, s0 + SEL_BLOCK) - np.maximum(c0, s0)
    m = np.clip(ov, 0, None).astype(np.float32) / CMP_LEN
    m[n_cmp:] = 0.0
    m[:, n_blocks:] = 0.0
    return m


CMP_FRONT = 16
CMP_NEAR = 24
SLC_NEAR = 2 * Q_BLOCK
SLC_TILE = 512
WIN_KEYS = WINDOW + Q_BLOCK


def _softmax_update(state, s, v, mask=None, v_transposed=False):
    m_old, l_old, acc = state
    m_new = jnp.maximum(m_old, jnp.max(s, axis=-1, keepdims=True))
    p = jnp.exp(s - m_new)
    if mask is not None:
        p = jnp.where(mask, p, 0.0)
    alpha = jnp.exp(m_old - m_new)
    l_new = alpha * l_old + jnp.sum(p, axis=-1, keepdims=True)
    pv = _dot_t(p.astype(BF16), v) if v_transposed else _dot(p.astype(BF16), v)
    return m_new, l_new, alpha * acc + pv


def _softmax_finish(state):
    _, l, acc = state
    return jnp.where(l > 0.0, acc / jnp.where(l > 0.0, l, 1.0), 0.0)


def _tile_rows(x, n):
    return jnp.concatenate([x] * n, axis=0)


def _nsa_prompt_kernel(rb_ref, q_ref, gn_ref, kcmp_ref, vcmp_ref, selmap_ref, ks_ref, vs_ref, kw_ref, vw_ref,
                       et_ref, out_ref, nbc_ref, nbs_ref, nbw_ref):
    hk = pl.program_id(1)
    qb = pl.program_id(2)
    s0 = qb * Q_BLOCK
    gq = GROUP * Q_BLOCK

    @pl.when(qb == 0)
    def _():
        for g in range(GROUP):
            tab = lambda k, g=g: rb_ref[k, hk * GROUP + g]
            ql = lax.broadcasted_iota(jnp.int32, (Q_BLOCK, LANES), 0)
            cl = lax.broadcasted_iota(jnp.int32, (Q_BLOCK, LANES), 1)
            dist = ql - CMP_STRIDE * (cl - CMP_FRONT) - (CMP_LEN - 1)
            ok = (dist >= 0) & (cl < CMP_NEAR)
            nbc_ref[g] = jnp.where(ok, _bias_from_scalars(_bucket(dist), tab), NEG_INF)
            ql = lax.broadcasted_iota(jnp.int32, (Q_BLOCK, SLC_NEAR), 0)
            kl = lax.broadcasted_iota(jnp.int32, (Q_BLOCK, SLC_NEAR), 1)
            dist = ql - kl + Q_BLOCK
            nbs_ref[g] = jnp.where(dist >= 0, _bias_from_scalars(_bucket(dist), tab), NEG_INF)
            ql = lax.broadcasted_iota(jnp.int32, (Q_BLOCK, WIN_KEYS), 0)
            kl = lax.broadcasted_iota(jnp.int32, (Q_BLOCK, WIN_KEYS), 1)
            dist = ql - kl + WINDOW
            ok = (dist >= 0) & (dist < WINDOW)
            nbw_ref[g] = jnp.where(ok, _bias_from_scalars(_bucket(dist), tab), NEG_INF)

    lane = lax.broadcasted_iota(jnp.int32, (Q_BLOCK, LANES), 1)
    lmask = (lane >= hk * HEAD_DIM) & (lane < (hk + 1) * HEAD_DIM)
    qblk = q_ref[0].astype(F32)
    qp = jnp.concatenate(
        [jnp.where(lmask, qblk[:, g * LANES:(g + 1) * LANES], 0.0) for g in range(GROUP)], axis=0)
    qp = (qp * (HEAD_DIM ** -0.5)).astype(BF16)

    far_bias = jnp.concatenate(
        [jnp.full((Q_BLOCK, 1), rb_ref[N_BUCKETS - 1, hk * GROUP + g], F32) for g in range(GROUP)], axis=0)

    n_cmp_pad = kcmp_ref.shape[1] - CMP_FRONT - (LANES - CMP_FRONT)
    kc_far = kcmp_ref[0, CMP_FRONT:CMP_FRONT + n_cmp_pad, :].astype(BF16)
    vc_far = vcmp_ref[0, CMP_FRONT:CMP_FRONT + n_cmp_pad, :].astype(BF16)
    near0 = pl.multiple_of(qb * (Q_BLOCK // CMP_STRIDE), 8)
    kc_near = kcmp_ref[0, pl.ds(near0, LANES), :].astype(BF16)
    vc_near = vcmp_ref[0, pl.ds(near0, LANES), :].astype(BF16)
    c8 = qb * (Q_BLOCK // CMP_STRIDE)
    ccol = lax.broadcasted_iota(jnp.int32, (1, n_cmp_pad), 1)
    s_far = _dot_t(qp, kc_far) + jnp.where(ccol < c8 - CMP_FRONT, 0.0, NEG_INF)
    ncol = lax.broadcasted_iota(jnp.int32, (1, LANES), 1)
    s_near = _dot_t(qp, kc_near) + (nbc_ref[...].reshape(gq, LANES)
                                    + jnp.where(ncol >= CMP_FRONT - c8, 0.0, NEG_INF))
    m = jnp.maximum(jnp.max(s_far, axis=-1, keepdims=True) + far_bias, jnp.max(s_near, axis=-1, keepdims=True))
    p_far = jnp.exp(s_far - (m - far_bias))
    p_near = jnp.exp(s_near - m)
    l = jnp.sum(p_far, axis=-1, keepdims=True) + jnp.sum(p_near, axis=-1, keepdims=True)
    inv = jnp.where(m > 0.5 * NEG_INF, 1.0 / l, 0.0)
    pc_far = p_far * inv
    pc_near = p_near * inv
    o_cmp = _dot(pc_far.astype(BF16), vc_far) + _dot(pc_near.astype(BF16), vc_near)

    init = (jnp.full((gq, 1), NEG_INF, F32), jnp.zeros((gq, 1), F32), jnp.zeros((gq, LANES), F32))
    p0 = pl.multiple_of(s0, LANES)
    wcol = lax.broadcasted_iota(jnp.int32, (1, WIN_KEYS), 1)
    s = _dot_t(qp, kw_ref[0, pl.ds(p0, WIN_KEYS), :]) + (nbw_ref[...].reshape(gq, WIN_KEYS)
                                                         + jnp.where(wcol >= WINDOW - s0, 0.0, NEG_INF))
    o_win = _softmax_finish(_softmax_update(init, s, vw_ref[0, pl.ds(p0, WIN_KEYS), :]))

    ps_far = sum(pc_far[g * Q_BLOCK:(g + 1) * Q_BLOCK] for g in range(GROUP))
    ps_near = sum(pc_near[g * Q_BLOCK:(g + 1) * Q_BLOCK] for g in range(GROUP))
    sm_far = selmap_ref[CMP_FRONT:CMP_FRONT + n_cmp_pad, :].astype(BF16)
    sm_near = selmap_ref[pl.ds(near0, LANES), :].astype(BF16)
    imp = _split_dot(ps_far, sm_far) + _split_dot(ps_near, sm_near)
    imp_t = imp.T
    n_sel = imp_t.shape[0]
    blk = lax.broadcasted_iota(jnp.int32, (n_sel, Q_BLOCK), 0)
    cur = (s0 + lax.broadcasted_iota(jnp.int32, (n_sel, Q_BLOCK), 1)) // SEL_BLOCK
    forced = (blk == 0) | (blk == cur) | (blk == cur - 1)
    val = jnp.where(blk > cur, NEG_INF, imp_t + FORCE_BONUS * forced.astype(F32))
    sel_t = jnp.zeros((n_sel, Q_BLOCK), F32)
    for _ in range(N_SELECT):
        mx = jnp.max(val, axis=0, keepdims=True)
        first = jnp.min(jnp.where(val == mx, blk, n_sel), axis=0, keepdims=True)
        hit = blk == first
        sel_t = jnp.where(hit, 1.0, sel_t)
        val = jnp.where(hit, -3e38, val)
    sel = sel_t.T > 0.5

    jcol = lax.broadcasted_iota(jnp.int32, (Q_BLOCK, n_sel), 1)
    far_blocks = (s0 - Q_BLOCK) // SEL_BLOCK
    pen_far = jnp.where(sel & (jcol < far_blocks), 0.0, NEG_INF).astype(BF16)
    pen_near = jnp.where(sel, 0.0, NEG_INF).astype(BF16)
    q_far = jnp.concatenate([qp, _tile_rows(pen_far, GROUP)], axis=1)
    q_near = jnp.concatenate([qp, _tile_rows(pen_near, GROUP)], axis=1)

    def far_step(kt, state):
        p0 = pl.multiple_of(Q_BLOCK + kt * SLC_TILE, LANES)
        k = jnp.concatenate([ks_ref[0, pl.ds(p0, SLC_TILE), :], et_ref[pl.ds(p0, SLC_TILE), :]], axis=1)
        return _softmax_update(state, _dot_t(q_far, k), vs_ref[0, pl.ds(p0, SLC_TILE), :])

    n_far = (qb + 2) // (SLC_TILE // Q_BLOCK)
    m_far, l_far, acc_far = lax.fori_loop(0, n_far, far_step, init)
    state = (m_far + far_bias, l_far, acc_far)
    k = jnp.concatenate([ks_ref[0, pl.ds(p0, SLC_NEAR), :], et_ref[pl.ds(p0, SLC_NEAR), :]], axis=1)
    kcol = lax.broadcasted_iota(jnp.int32, (1, SLC_NEAR), 1)
    s = _dot_t(q_near, k) + (nbs_ref[...].reshape(gq, SLC_NEAR) + jnp.where(kcol >= Q_BLOCK - s0, 0.0, NEG_INF))
    o_slc = _softmax_finish(_softmax_update(state, s, vs_ref[0, pl.ds(p0, SLC_NEAR), :]))

    gs = jax.nn.sigmoid(gn_ref[0])

    def gate(g, i):
        c0, c1 = 3 * g + i, 3 * (GROUP + g) + i
        return jnp.where(hk == 0, gs[:, c0:c0 + 1], gs[:, c1:c1 + 1])

    for g in range(GROUP):
        rows = slice(g * Q_BLOCK, (g + 1) * Q_BLOCK)
        o = gate(g, 0) * o_cmp[rows] + gate(g, 1) * o_slc[rows] + gate(g, 2) * o_win[rows]
        out_ref[0, 0, :, g * LANES:(g + 1) * LANES] = jnp.where(lmask, o, 0.0).astype(out_ref.dtype)


def _nsa_prompt(q, gn, kcmp, vcmp, kvb, rel_bias):
    b, t, _ = q.shape
    assert t % SLC_TILE == 0
    nqb = t // Q_BLOCK
    n_chunk = t // CMP_STRIDE
    n_cmp = n_chunk - CMP_RATIO + 1
    n_blocks = t // SEL_BLOCK
    assert n_blocks <= LANES
    cpad = ((0, 0), (CMP_FRONT, LANES - CMP_FRONT), (0, 0))
    kcmp_p = jnp.pad(kcmp, cpad)
    vcmp_p = jnp.pad(vcmp, cpad)
    selmap = jnp.asarray(np.pad(_sel_map(n_chunk, n_cmp, LANES, n_blocks),
                                ((CMP_FRONT, LANES - CMP_FRONT), (0, 0))), F32)
    ks_p = jnp.pad(kvb[:, :, 2 * KV_WIDTH:3 * KV_WIDTH], ((0, 0), (Q_BLOCK, 0), (0, 0)))
    vs_p = jnp.pad(kvb[:, :, 3 * KV_WIDTH:4 * KV_WIDTH], ((0, 0), (Q_BLOCK, 0), (0, 0)))
    kw_p = jnp.pad(kvb[:, :, 4 * KV_WIDTH:5 * KV_WIDTH], ((0, 0), (WINDOW, 0), (0, 0)))
    vw_p = jnp.pad(kvb[:, :, 5 * KV_WIDTH:6 * KV_WIDTH], ((0, 0), (WINDOW, 0), (0, 0)))
    pidx = np.arange(t + Q_BLOCK)
    e = (pidx[:, None] // SEL_BLOCK - Q_BLOCK // SEL_BLOCK == np.arange(LANES)[None, :])
    e = jnp.asarray(e.astype(np.float32), BF16)
    full = lambda arr: pl.BlockSpec((1,) + arr.shape[1:], lambda bi, h, i: (bi, 0, 0))
    return pl.pallas_call(
        _nsa_prompt_kernel,
        grid=(b, N_KV_HEADS, nqb),
        in_specs=[pl.BlockSpec(memory_space=pltpu.SMEM),
                  pl.BlockSpec((1, Q_BLOCK, NSA_WIDTH), lambda bi, h, i: (bi, i, 0)),
                  pl.BlockSpec((1, Q_BLOCK, LANES), lambda bi, h, i: (bi, i, 0)),
                  full(kcmp_p), full(vcmp_p),
                  pl.BlockSpec(selmap.shape, lambda bi, h, i: (0, 0)),
                  full(ks_p), full(vs_p), full(kw_p), full(vw_p),
                  pl.BlockSpec(e.shape, lambda bi, h, i: (0, 0))],
        out_specs=pl.BlockSpec((1, 1, Q_BLOCK, NSA_WIDTH), lambda bi, h, i: (h, bi, i, 0)),
        out_shape=jax.ShapeDtypeStruct((N_KV_HEADS, b, t, NSA_WIDTH), BF16),
        scratch_shapes=[pltpu.VMEM((GROUP, Q_BLOCK, LANES), F32),
                        pltpu.VMEM((GROUP, Q_BLOCK, SLC_NEAR), F32),
                        pltpu.VMEM((GROUP, Q_BLOCK, WIN_KEYS), F32)],
        compiler_params=_cparams(("arbitrary", "arbitrary", "arbitrary")),
        name="nsa_prompt",
    )(rel_bias, q, gn, kcmp_p, vcmp_p, selmap, ks_p, vs_p, kw_p, vw_p, e)


def _col_softmax_update(state, s, vt):
    m_old, l_old, acc = state
    m_new = jnp.maximum(m_old, jnp.max(s, axis=0, keepdims=True))
    p = jnp.exp(s - m_new)
    alpha = jnp.exp(m_old - m_new)
    l_new = alpha * l_old + jnp.sum(p, axis=0, keepdims=True)
    return m_new, l_new, alpha * acc + _dot(vt, p.astype(BF16))


def _col_softmax_finish(state):
    _, l, acc = state
    return acc * jnp.where(l > 0.0, 1.0 / jnp.where(l > 0.0, l, 1.0), 0.0)


def _nsa_prompt_t_kernel(rb_ref, q_ref, gn_ref, kcmp_ref, vcmp_ref, vcmpt_ref, selmap_ref, selmapt_ref,
                         ks_ref, vst_ref, kw_ref, vwt_ref, et_ref, out_ref, nbc_ref, nbs_ref, nbw_ref):
    hk = pl.program_id(1)
    qb = pl.program_id(2)
    s0 = qb * Q_BLOCK
    gq = GROUP * Q_BLOCK

    @pl.when(qb == 0)
    def _():
        for g in range(GROUP):
            tab = lambda k, g=g: rb_ref[k, hk * GROUP + g]
            cols = slice(g * Q_BLOCK, (g + 1) * Q_BLOCK)
            cl = lax.broadcasted_iota(jnp.int32, (LANES, Q_BLOCK), 0)
            ql = lax.broadcasted_iota(jnp.int32, (LANES, Q_BLOCK), 1)
            dist = ql - CMP_STRIDE * (cl - CMP_FRONT) - (CMP_LEN - 1)
            ok = (dist >= 0) & (cl < CMP_NEAR)
            nbc_ref[:, cols] = jnp.where(ok, _bias_from_scalars(_bucket(dist), tab), NEG_INF)
            kl = lax.broadcasted_iota(jnp.int32, (SLC_NEAR, Q_BLOCK), 0)
            ql = lax.broadcasted_iota(jnp.int32, (SLC_NEAR, Q_BLOCK), 1)
            dist = ql - kl + Q_BLOCK
            nbs_ref[:, cols] = jnp.where(dist >= 0, _bias_from_scalars(_bucket(dist), tab), NEG_INF)
            kl = lax.broadcasted_iota(jnp.int32, (WIN_KEYS, Q_BLOCK), 0)
            ql = lax.broadcasted_iota(jnp.int32, (WIN_KEYS, Q_BLOCK), 1)
            dist = ql - kl + WINDOW
            ok = (dist >= 0) & (dist < WINDOW)
            nbw_ref[:, cols] = jnp.where(ok, _bias_from_scalars(_bucket(dist), tab), NEG_INF)

    lane = lax.broadcasted_iota(jnp.int32, (Q_BLOCK, LANES), 1)
    lmask = (lane >= hk * HEAD_DIM) & (lane < (hk + 1) * HEAD_DIM)
    qblk = q_ref[0].astype(F32)
    qp = jnp.concatenate(
        [jnp.where(lmask, qblk[:, g * LANES:(g + 1) * LANES], 0.0) for g in range(GROUP)], axis=0)
    qp = (qp * (HEAD_DIM ** -0.5)).astype(BF16)
    far_bias = jnp.concatenate(
        [jnp.full((1, Q_BLOCK), rb_ref[N_BUCKETS - 1, hk * GROUP + g], F32) for g in range(GROUP)], axis=1)
    init = (jnp.full((1, gq), NEG_INF, F32), jnp.zeros((1, gq), F32), jnp.zeros((LANES, gq), F32))
    p0 = pl.multiple_of(s0, LANES)

    n_cmp_pad = kcmp_ref.shape[1] - LANES
    c8 = qb * (Q_BLOCK // CMP_STRIDE)
    near0 = pl.multiple_of(c8, 8)
    kc_far = kcmp_ref[0, CMP_FRONT:CMP_FRONT + n_cmp_pad, :].astype(BF16)
    kc_near = kcmp_ref[0, pl.ds(near0, LANES), :].astype(BF16)
    crow = lax.broadcasted_iota(jnp.int32, (n_cmp_pad, gq), 0)
    s_far = jnp.where(crow < c8 - CMP_FRONT, _dot_t(kc_far, qp), NEG_INF)
    nrow = lax.broadcasted_iota(jnp.int32, (LANES, gq), 0)
    s_near = jnp.where(nrow >= CMP_FRONT - c8, _dot_t(kc_near, qp) + nbc_ref[...], NEG_INF)
    m = jnp.maximum(jnp.max(s_far, axis=0, keepdims=True) + far_bias, jnp.max(s_near, axis=0, keepdims=True))
    p_far = jnp.exp(s_far - (m - far_bias))
    p_near = jnp.exp(s_near - m)
    l = jnp.sum(p_far, axis=0, keepdims=True) + jnp.sum(p_near, axis=0, keepdims=True)
    inv = jnp.where(m > 0.5 * NEG_INF, 1.0 / l, 0.0)
    pc_far = p_far * inv
    pc_near = p_near * inv
    vct_near = vcmp_ref[0, pl.ds(near0, LANES), :].T.astype(BF16)
    o_cmp = _dot(vcmpt_ref[0], pc_far.astype(BF16)) + _dot(vct_near, pc_near.astype(BF16))

    wrow = lax.broadcasted_iota(jnp.int32, (WIN_KEYS, gq), 0)
    s = jnp.where(wrow >= WINDOW - s0, _dot_t(kw_ref[0, pl.ds(p0, WIN_KEYS), :], qp) + nbw_ref[...], NEG_INF)
    o_win = _col_softmax_finish(_col_softmax_update(init, s, vwt_ref[0, :, pl.ds(p0, WIN_KEYS)]))

    ps_far = sum(pc_far[:, g * Q_BLOCK:(g + 1) * Q_BLOCK] for g in range(GROUP))
    ps_near = sum(pc_near[:, g * Q_BLOCK:(g + 1) * Q_BLOCK] for g in range(GROUP))

    def split_dot(a, p):
        hi = p.astype(BF16)
        return _dot(a, hi) + _dot(a, (p - hi.astype(F32)).astype(BF16))

    sm_near_t = selmap_ref[pl.ds(near0, LANES), :].T.astype(BF16)
    imp_t = split_dot(selmapt_ref[...], ps_far) + split_dot(sm_near_t, ps_near)
    n_sel = imp_t.shape[0]
    blk = lax.broadcasted_iota(jnp.int32, (n_sel, Q_BLOCK), 0)
    cur = (s0 + lax.broadcasted_iota(jnp.int32, (n_sel, Q_BLOCK), 1)) // SEL_BLOCK
    forced = (blk == 0) | (blk == cur) | (blk == cur - 1)
    val = jnp.where(blk > cur, NEG_INF, imp_t + FORCE_BONUS * forced.astype(F32))
    sel_t = jnp.zeros((n_sel, Q_BLOCK), F32)
    for _ in range(N_SELECT):
        mx = jnp.max(val, axis=0, keepdims=True)
        first = jnp.min(jnp.where(val == mx, blk, n_sel), axis=0, keepdims=True)
        hit = blk == first
        sel_t = jnp.where(hit, 1.0, sel_t)
        val = jnp.where(hit, -3e38, val)
    sel = sel_t.T > 0.5

    jcol = lax.broadcasted_iota(jnp.int32, (Q_BLOCK, n_sel), 1)
    far_blocks = (s0 - Q_BLOCK) // SEL_BLOCK
    pen_far = jnp.where(sel & (jcol < far_blocks), 0.0, NEG_INF).astype(BF16)
    pen_near = jnp.where(sel, 0.0, NEG_INF).astype(BF16)
    q_far = jnp.concatenate([qp, _tile_rows(pen_far, GROUP)], axis=1)
    q_near = jnp.concatenate([qp, _tile_rows(pen_near, GROUP)], axis=1)

    n_tiles = (ks_ref.shape[1] - Q_BLOCK) // SLC_TILE

    def far_logits(kt):
        f0 = pl.multiple_of(Q_BLOCK + kt * SLC_TILE, LANES)
        k = jnp.concatenate([ks_ref[0, pl.ds(f0, SLC_TILE), :], et_ref[pl.ds(f0, SLC_TILE), :]], axis=1)
        return _dot_t(k, q_far)

    def far_step(kt, carry):
        s_cur, state = carry
        s_next = far_logits(jnp.minimum(kt + 1, n_tiles - 1))
        f0 = pl.multiple_of(Q_BLOCK + kt * SLC_TILE, LANES)
        return s_next, _col_softmax_update(state, s_cur, vst_ref[0, :, pl.ds(f0, SLC_TILE)])

    n_far = (qb + 2) // (SLC_TILE // Q_BLOCK)
    _, (m_far, l_far, acc_far) = lax.fori_loop(0, n_far, far_step, (far_logits(0), init))
    state = (m_far + far_bias, l_far, acc_far)
    k = jnp.concatenate([ks_ref[0, pl.ds(p0, SLC_NEAR), :], et_ref[pl.ds(p0, SLC_NEAR), :]], axis=1)
    krow = lax.broadcasted_iota(jnp.int32, (SLC_NEAR, gq), 0)
    s = jnp.where(krow >= Q_BLOCK - s0, _dot_t(k, q_near) + nbs_ref[...], NEG_INF)
    o_slc = _col_softmax_finish(_col_softmax_update(state, s, vst_ref[0, :, pl.ds(p0, SLC_NEAR)]))

    gst = jax.nn.sigmoid(gn_ref[0]).T

    def gate(g, i):
        c0, c1 = 3 * g + i, 3 * (GROUP + g) + i
        return jnp.where(hk == 0, gst[c0:c0 + 1, :], gst[c1:c1 + 1, :])

    for g in range(GROUP):
        cols = slice(g * Q_BLOCK, (g + 1) * Q_BLOCK)
        o = gate(g, 0) * o_cmp[:, cols] + gate(g, 1) * o_slc[:, cols] + gate(g, 2) * o_win[:, cols]
        out_ref[0, 0, :, g * LANES:(g + 1) * LANES] = jnp.where(lmask, o.T, 0.0).astype(out_ref.dtype)


def _nsa_prompt_t(q, gn, kcmp, vcmp, kvb, rel_bias):
    b, t, _ = q.shape
    assert t % SLC_TILE == 0
    nqb = t // Q_BLOCK
    n_chunk = t // CMP_STRIDE
    n_cmp = n_chunk - CMP_RATIO + 1
    n_blocks = t // SEL_BLOCK
    assert n_blocks <= LANES
    cpad = ((0, 0), (CMP_FRONT, LANES - CMP_FRONT), (0, 0))
    kcmp_p = jnp.pad(kcmp, cpad)
    vcmp_p = jnp.pad(vcmp, cpad)
    vcmp_t = vcmp.transpose(0, 2, 1).astype(BF16)
    sm = _sel_map(n_chunk, n_cmp, LANES, n_blocks)
    selmap = jnp.asarray(np.pad(sm, ((CMP_FRONT, LANES - CMP_FRONT), (0, 0))), F32)
    selmap_t = jnp.asarray(sm.T, BF16)
    front = lambda z, n: jnp.pad(z, ((0, 0), (n, 0), (0, 0)))
    ks_p = front(kvb[:, :, 2 * KV_WIDTH:3 * KV_WIDTH], Q_BLOCK)
    vs_t = front(kvb[:, :, 3 * KV_WIDTH:4 * KV_WIDTH], Q_BLOCK).transpose(0, 2, 1)
    kw_p = front(kvb[:, :, 4 * KV_WIDTH:5 * KV_WIDTH], WINDOW)
    vw_t = front(kvb[:, :, 5 * KV_WIDTH:6 * KV_WIDTH], WINDOW).transpose(0, 2, 1)
    pidx = np.arange(t + Q_BLOCK)
    e = (pidx[:, None] // SEL_BLOCK - Q_BLOCK // SEL_BLOCK == np.arange(LANES)[None, :])
    e = jnp.asarray(e.astype(np.float32), BF16)
    full = lambda arr: pl.BlockSpec((1,) + arr.shape[1:], lambda bi, h, i: (bi, 0, 0))
    const = lambda arr: pl.BlockSpec(arr.shape, lambda bi, h, i: (0, 0))
    return pl.pallas_call(
        _nsa_prompt_t_kernel,
        grid=(b, N_KV_HEADS, nqb),
        in_specs=[pl.BlockSpec(memory_space=pltpu.SMEM),
                  pl.BlockSpec((1, Q_BLOCK, NSA_WIDTH), lambda bi, h, i: (bi, i, 0)),
                  pl.BlockSpec((1, Q_BLOCK, LANES), lambda bi, h, i: (bi, i, 0)),
                  full(kcmp_p), full(vcmp_p), full(vcmp_t), const(selmap), const(selmap_t),
                  full(ks_p), full(vs_t), full(kw_p), full(vw_t), const(e)],
        out_specs=pl.BlockSpec((1, 1, Q_BLOCK, NSA_WIDTH), lambda bi, h, i: (h, bi, i, 0)),
        out_shape=jax.ShapeDtypeStruct((N_KV_HEADS, b, t, NSA_WIDTH), BF16),
        scratch_shapes=[pltpu.VMEM((LANES, GROUP * Q_BLOCK), F32),
                        pltpu.VMEM((SLC_NEAR, GROUP * Q_BLOCK), F32),
                        pltpu.VMEM((WIN_KEYS, GROUP * Q_BLOCK), F32)],
        compiler_params=_cparams(("arbitrary", "arbitrary", "arbitrary")),
        name="nsa_prompt",
    )(rel_bias, q, gn, kcmp_p, vcmp_p, vcmp_t, selmap, selmap_t, ks_p, vs_t, kw_p, vw_t, e)


def _nsa_prompt_pair_kernel(rb_ref, q_ref, gn_ref, kcmp_ref, vcmp_ref, vcmpt_ref, selmap_ref, selmapt_ref,
                            ks_ref, vst_ref, kw_ref, vwt_ref, et_ref, out_ref, nbc_ref, nbs_ref, nbw_ref, s_ref):
    qb = pl.program_id(1)
    s0 = qb * Q_BLOCK
    gq = GROUP * Q_BLOCK
    heads = range(N_KV_HEADS)

    @pl.when(qb == 0)
    def _():
        for hk in heads:
            for g in range(GROUP):
                tab = lambda k, h=hk * GROUP + g: rb_ref[k, h]
                cols = slice(g * Q_BLOCK, (g + 1) * Q_BLOCK)
                cl = lax.broadcasted_iota(jnp.int32, (LANES, Q_BLOCK), 0)
                ql = lax.broadcasted_iota(jnp.int32, (LANES, Q_BLOCK), 1)
                dist = ql - CMP_STRIDE * (cl - CMP_FRONT) - (CMP_LEN - 1)
                ok = (dist >= 0) & (cl < CMP_NEAR)
                nbc_ref[hk, :, cols] = jnp.where(ok, _bias_from_scalars(_bucket(dist), tab), NEG_INF)
                kl = lax.broadcasted_iota(jnp.int32, (SLC_NEAR, Q_BLOCK), 0)
                ql = lax.broadcasted_iota(jnp.int32, (SLC_NEAR, Q_BLOCK), 1)
                dist = ql - kl + Q_BLOCK
                nbs_ref[hk, :, cols] = jnp.where(dist >= 0, _bias_from_scalars(_bucket(dist), tab), NEG_INF)
                kl = lax.broadcasted_iota(jnp.int32, (WIN_KEYS, Q_BLOCK), 0)
                ql = lax.broadcasted_iota(jnp.int32, (WIN_KEYS, Q_BLOCK), 1)
                dist = ql - kl + WINDOW
                ok = (dist >= 0) & (dist < WINDOW)
                nbw_ref[hk, :, cols] = jnp.where(ok, _bias_from_scalars(_bucket(dist), tab), NEG_INF)

    lane = lax.broadcasted_iota(jnp.int32, (Q_BLOCK, LANES), 1)
    qblk = q_ref[0].astype(F32)
    init = (jnp.full((1, gq), NEG_INF, F32), jnp.zeros((1, gq), F32), jnp.zeros((LANES, gq), F32))
    p0 = pl.multiple_of(s0, LANES)
    n_cmp_pad = kcmp_ref.shape[1] - LANES
    c8 = qb * (Q_BLOCK // CMP_STRIDE)
    near0 = pl.multiple_of(c8, 8)
    kc_far = kcmp_ref[0, CMP_FRONT:CMP_FRONT + n_cmp_pad, :].astype(BF16)
    kc_near = kcmp_ref[0, pl.ds(near0, LANES), :].astype(BF16)
    vct_near = vcmp_ref[0, pl.ds(near0, LANES), :].T.astype(BF16)
    sm_near_t = selmap_ref[pl.ds(near0, LANES), :].T.astype(BF16)
    kw = kw_ref[0, pl.ds(p0, WIN_KEYS), :]
    vwt = vwt_ref[0, :, pl.ds(p0, WIN_KEYS)]
    crow = lax.broadcasted_iota(jnp.int32, (n_cmp_pad, gq), 0)
    nrow = lax.broadcasted_iota(jnp.int32, (LANES, gq), 0)
    wrow = lax.broadcasted_iota(jnp.int32, (WIN_KEYS, gq), 0)
    krow = lax.broadcasted_iota(jnp.int32, (SLC_NEAR, gq), 0)
    n_sel = selmapt_ref.shape[0]
    blk = lax.broadcasted_iota(jnp.int32, (n_sel, Q_BLOCK), 0)
    cur = (s0 + lax.broadcasted_iota(jnp.int32, (n_sel, Q_BLOCK), 1)) // SEL_BLOCK
    forced = (blk == 0) | (blk == cur) | (blk == cur - 1)
    jcol = lax.broadcasted_iota(jnp.int32, (Q_BLOCK, n_sel), 1)
    far_blocks = (s0 - Q_BLOCK) // SEL_BLOCK

    def split_dot(a, p):
        hi = p.astype(BF16)
        return _dot(a, hi) + _dot(a, (p - hi.astype(F32)).astype(BF16))

    def before_far(hk):
        lmask = (lane // HEAD_DIM) == hk
        qp = jnp.concatenate(
            [jnp.where(lmask, qblk[:, g * LANES:(g + 1) * LANES], 0.0) for g in range(GROUP)], axis=0)
        qp = (qp * (HEAD_DIM ** -0.5)).astype(BF16)
        far_bias = jnp.concatenate(
            [jnp.full((1, Q_BLOCK), rb_ref[N_BUCKETS - 1, hk * GROUP + g], F32) for g in range(GROUP)], axis=1)

        s_far = jnp.where(crow < c8 - CMP_FRONT, _dot_t(kc_far, qp), NEG_INF)
        s_near = jnp.where(nrow >= CMP_FRONT - c8, _dot_t(kc_near, qp) + nbc_ref[hk], NEG_INF)
        m = jnp.maximum(jnp.max(s_far, axis=0, keepdims=True) + far_bias, jnp.max(s_near, axis=0, keepdims=True))
        p_far = jnp.exp(s_far - (m - far_bias))
        p_near = jnp.exp(s_near - m)
        l = jnp.sum(p_far, axis=0, keepdims=True) + jnp.sum(p_near, axis=0, keepdims=True)
        inv = jnp.where(m > 0.5 * NEG_INF, 1.0 / l, 0.0)
        pc_far = p_far * inv
        pc_near = p_near * inv
        o_cmp = _dot(vcmpt_ref[0], pc_far.astype(BF16)) + _dot(vct_near, pc_near.astype(BF16))

        s = jnp.where(wrow >= WINDOW - s0, _dot_t(kw, qp) + nbw_ref[hk], NEG_INF)
        o_win = _col_softmax_finish(_col_softmax_update(init, s, vwt))

        ps_far = sum(pc_far[:, g * Q_BLOCK:(g + 1) * Q_BLOCK] for g in range(GROUP))
        ps_near = sum(pc_near[:, g * Q_BLOCK:(g + 1) * Q_BLOCK] for g in range(GROUP))
        imp_t = split_dot(selmapt_ref[...], ps_far) + split_dot(sm_near_t, ps_near)
        val = jnp.where(blk > cur, NEG_INF, imp_t + FORCE_BONUS * forced.astype(F32))
        sel_t = jnp.zeros((n_sel, Q_BLOCK), F32)
        for _ in range(N_SELECT):
            mx = jnp.max(val, axis=0, keepdims=True)
            first = jnp.min(jnp.where(val == mx, blk, n_sel), axis=0, keepdims=True)
            hit = blk == first
            sel_t = jnp.where(hit, 1.0, sel_t)
            val = jnp.where(hit, -3e38, val)
        sel = sel_t.T > 0.5

        pen_far = jnp.where(sel & (jcol < far_blocks), 0.0, NEG_INF).astype(BF16)
        pen_near = jnp.where(sel, 0.0, NEG_INF).astype(BF16)
        q_far = jnp.concatenate([qp, _tile_rows(pen_far, GROUP)], axis=1)
        q_near = jnp.concatenate([qp, _tile_rows(pen_near, GROUP)], axis=1)
        return dict(far_bias=far_bias, o_cmp=o_cmp, o_win=o_win, q_far=q_far, q_near=q_near)

    pre = [before_far(hk) for hk in heads]
    n_tiles = (ks_ref.shape[1] - Q_BLOCK) // SLC_TILE

    def far_logits(kt):
        f0 = pl.multiple_of(Q_BLOCK + kt * SLC_TILE, LANES)
        k = jnp.concatenate([ks_ref[0, pl.ds(f0, SLC_TILE), :], et_ref[pl.ds(f0, SLC_TILE), :]], axis=1)
        return [_dot_t(k, pre[hk]["q_far"]) for hk in heads]

    def far_pair(i, states):
        for slot in range(2):
            kt = 2 * i + slot
            for hk, s_next in enumerate(far_logits(jnp.minimum(kt + 1, n_tiles - 1))):
                s_ref[1 - slot, hk] = s_next
            f0 = pl.multiple_of(Q_BLOCK + jnp.minimum(kt, n_tiles - 1) * SLC_TILE, LANES)
            vt = vst_ref[0, :, pl.ds(f0, SLC_TILE)]
            states = [_col_softmax_update(states[hk], s_ref[slot, hk], vt) for hk in heads]
        return states

    n_far = (qb + 2) // (SLC_TILE // Q_BLOCK)
    for hk, s_first in enumerate(far_logits(0)):
        s_ref[0, hk] = s_first
    far_states = lax.fori_loop(0, (n_far + 1) // 2, far_pair, [init] * N_KV_HEADS)

    k_near = jnp.concatenate([ks_ref[0, pl.ds(p0, SLC_NEAR), :], et_ref[pl.ds(p0, SLC_NEAR), :]], axis=1)
    vt_near = vst_ref[0, :, pl.ds(p0, SLC_NEAR)]
    gst = jax.nn.sigmoid(gn_ref[0]).T
    outs = []
    for hk in heads:
        m_far, l_far, acc_far = far_states[hk]
        state = (m_far + pre[hk]["far_bias"], l_far, acc_far)
        s = jnp.where(krow >= Q_BLOCK - s0, _dot_t(k_near, pre[hk]["q_near"]) + nbs_ref[hk], NEG_INF)
        o_slc = _col_softmax_finish(_col_softmax_update(state, s, vt_near))
        per_g = []
        for g in range(GROUP):
            cols = slice(g * Q_BLOCK, (g + 1) * Q_BLOCK)
            c = 3 * (hk * GROUP + g)
            o = (gst[c:c + 1, :] * pre[hk]["o_cmp"][:, cols] + gst[c + 1:c + 2, :] * o_slc[:, cols]
                 + gst[c + 2:c + 3, :] * pre[hk]["o_win"][:, cols])
            per_g.append(o.T)
        outs.append(per_g)
    lane_h0 = lane < HEAD_DIM
    for g in range(GROUP):
        out_ref[0, :, g * LANES:(g + 1) * LANES] = jnp.where(lane_h0, outs[0][g], outs[1][g]).astype(out_ref.dtype)


def _nsa_prompt_pair(q, gn, kcmp, vcmp, kvb, rel_bias):
    b, t, _ = q.shape
    assert t % SLC_TILE == 0
    nqb = t // Q_BLOCK
    n_chunk = t // CMP_STRIDE
    n_cmp = n_chunk - CMP_RATIO + 1
    n_blocks = t // SEL_BLOCK
    assert n_blocks <= LANES
    cpad = ((0, 0), (CMP_FRONT, LANES - CMP_FRONT), (0, 0))
    kcmp_p = jnp.pad(kcmp, cpad)
    vcmp_p = jnp.pad(vcmp, cpad)
    vcmp_t = vcmp.transpose(0, 2, 1).astype(BF16)
    sm = _sel_map(n_chunk, n_cmp, LANES, n_blocks)
    selmap = jnp.asarray(np.pad(sm, ((CMP_FRONT, LANES - CMP_FRONT), (0, 0))), F32)
    selmap_t = jnp.asarray(sm.T, BF16)
    front = lambda z, n: jnp.pad(z, ((0, 0), (n, 0), (0, 0)))
    ks_p = front(kvb[:, :, 2 * KV_WIDTH:3 * KV_WIDTH], Q_BLOCK)
    vs_t = front(kvb[:, :, 3 * KV_WIDTH:4 * KV_WIDTH], Q_BLOCK).transpose(0, 2, 1)
    kw_p = front(kvb[:, :, 4 * KV_WIDTH:5 * KV_WIDTH], WINDOW)
    vw_t = front(kvb[:, :, 5 * KV_WIDTH:6 * KV_WIDTH], WINDOW).transpose(0, 2, 1)
    pidx = np.arange(t + Q_BLOCK)
    e = (pidx[:, None] // SEL_BLOCK - Q_BLOCK // SEL_BLOCK == np.arange(LANES)[None, :])
    e = jnp.asarray(e.astype(np.float32), BF16)
    full = lambda arr: pl.BlockSpec((1,) + arr.shape[1:], lambda bi, i: (bi, 0, 0))
    const = lambda arr: pl.BlockSpec(arr.shape, lambda bi, i: (0, 0))
    return pl.pallas_call(
        _nsa_prompt_pair_kernel,
        grid=(b, nqb),
        in_specs=[pl.BlockSpec(memory_space=pltpu.SMEM),
                  pl.BlockSpec((1, Q_BLOCK, NSA_WIDTH), lambda bi, i: (bi, i, 0)),
                  pl.BlockSpec((1, Q_BLOCK, LANES), lambda bi, i: (bi, i, 0)),
                  full(kcmp_p), full(vcmp_p), full(vcmp_t), const(selmap), const(selmap_t),
                  full(ks_p), full(vs_t), full(kw_p), full(vw_t), const(e)],
        out_specs=pl.BlockSpec((1, Q_BLOCK, NSA_WIDTH), lambda bi, i: (bi, i, 0)),
        out_shape=jax.ShapeDtypeStruct((b, t, NSA_WIDTH), BF16),
        scratch_shapes=[pltpu.VMEM((N_KV_HEADS, LANES, GROUP * Q_BLOCK), F32),
                        pltpu.VMEM((N_KV_HEADS, SLC_NEAR, GROUP * Q_BLOCK), F32),
                        pltpu.VMEM((N_KV_HEADS, WIN_KEYS, GROUP * Q_BLOCK), F32),
                        pltpu.VMEM((2, N_KV_HEADS, SLC_TILE, GROUP * Q_BLOCK), F32)],
        compiler_params=_cparams(("arbitrary", "arbitrary")),
        name="nsa_prompt",
    )(rel_bias, q, gn, kcmp_p, vcmp_p, vcmp_t, selmap, selmap_t, ks_p, vs_t, kw_p, vw_t, e)


ROWS_PER_TOKEN = 8


def _nsa_sample_cmp_kernel(q_ref, kcmp_ref, vcmp_ref, selmap_ref, rowtab_ref, ocmp_ref, idx_ref, *,
                           past_len, n_tok, n_cmp, n_blocks):
    rows = n_tok * ROWS_PER_TOKEN
    n_cpad = kcmp_ref.shape[1]
    kc = kcmp_ref[0].astype(BF16)
    vc = vcmp_ref[0].astype(BF16)
    ridx = lax.broadcasted_iota(jnp.int32, (rows, n_cpad), 0)
    ccol = lax.broadcasted_iota(jnp.int32, (rows, n_cpad), 1)
    dist = past_len + ridx // ROWS_PER_TOKEN - (CMP_STRIDE * ccol + CMP_LEN - 1)
    bucket = _bucket(dist)
    valid = (dist >= 0) & (ccol < n_cmp) & (ridx % ROWS_PER_TOKEN < GROUP)
    n_bpad = selmap_ref.shape[1]
    blk = lax.broadcasted_iota(jnp.int32, (ROWS_PER_TOKEN, n_bpad), 1)
    cur = (past_len + lax.broadcasted_iota(jnp.int32, (ROWS_PER_TOKEN, n_bpad), 0)) // SEL_BLOCK
    forced = (blk == 0) | (blk == cur) | (blk == cur - 1)
    olane = lax.broadcasted_iota(jnp.int32, (ROWS_PER_TOKEN, LANES), 1)
    for hk in range(N_KV_HEADS):
        qp = (q_ref[0, hk].astype(F32) * (HEAD_DIM ** -0.5)).astype(BF16)
        s = _dot_t(qp, kc) + _bias_from_rows(bucket, rowtab_ref[hk])
        s = jnp.where(valid, s, NEG_INF)
        m = jnp.max(s, axis=-1, keepdims=True)
        p = jnp.where(valid, jnp.exp(s - m), 0.0)
        l = jnp.sum(p, axis=-1, keepdims=True)
        pc = p * jnp.where(l > 0.0, 1.0 / jnp.where(l > 0.0, l, 1.0), 0.0)
        ocmp_ref[0, hk] = _dot(pc.astype(BF16), vc)[:, hk * HEAD_DIM:(hk + 1) * HEAD_DIM]
        ps = jnp.concatenate(
            [jnp.sum(pc[i * ROWS_PER_TOKEN:(i + 1) * ROWS_PER_TOKEN], axis=0, keepdims=True) for i in range(n_tok)]
            + [jnp.zeros((ROWS_PER_TOKEN - n_tok, n_cpad), F32)], axis=0)
        imp = _split_dot(ps, selmap_ref[...])
        val = jnp.where((blk > cur) | (blk >= n_blocks), NEG_INF, imp + FORCE_BONUS * forced.astype(F32))
        picks = jnp.zeros((ROWS_PER_TOKEN, LANES), jnp.int32)
        for n in range(N_SELECT):
            mx = jnp.max(val, axis=-1, keepdims=True)
            first = jnp.min(jnp.where(val == mx, blk, n_bpad), axis=-1, keepdims=True)
            picks = jnp.where(olane == n, first, picks)
            val = jnp.where(blk == first, -3e38, val)
        idx_ref[0, hk] = picks


def _nsa_sample_cmp(qs, kcmp, vcmp, rowtab, past_len, n_tok, n_cmp, n_blocks):
    nb = qs.shape[0]
    n_cpad = kcmp.shape[1]
    n_bpad = -(-n_blocks // LANES) * LANES
    selmap = jnp.asarray(_sel_map(n_cpad, n_cmp, n_bpad, n_blocks), BF16)
    rows = n_tok * ROWS_PER_TOKEN
    return pl.pallas_call(
        functools.partial(_nsa_sample_cmp_kernel, past_len=past_len, n_tok=n_tok, n_cmp=n_cmp, n_blocks=n_blocks),
        grid=(nb,),
        in_specs=[pl.BlockSpec((1, N_KV_HEADS, rows, LANES), lambda b: (b, 0, 0, 0)),
                  pl.BlockSpec((1, n_cpad, LANES), lambda b: (b, 0, 0)),
                  pl.BlockSpec((1, n_cpad, LANES), lambda b: (b, 0, 0)),
                  pl.BlockSpec(selmap.shape, lambda b: (0, 0)),
                  pl.BlockSpec(rowtab.shape, lambda b: (0, 0, 0))],
        out_specs=[pl.BlockSpec((1, N_KV_HEADS, rows, HEAD_DIM), lambda b: (b, 0, 0, 0)),
                   pl.BlockSpec((1, N_KV_HEADS, ROWS_PER_TOKEN, LANES), lambda b: (b, 0, 0, 0))],
        out_shape=[jax.ShapeDtypeStruct((nb, N_KV_HEADS, rows, HEAD_DIM), F32),
                   jax.ShapeDtypeStruct((nb, N_KV_HEADS, ROWS_PER_TOKEN, LANES), jnp.int32)],
        compiler_params=_cparams(("arbitrary",)),
        name="nsa_sample_cmp",
    )(qs, kcmp, vcmp, selmap, rowtab)


def _nsa_sample_slc_kernel(idx_ref, pt_ref, *refs, past_len, n_tok, n_cache_blocks):
    del pt_ref
    k_refs = refs[:N_SELECT]
    v_refs = refs[N_SELECT:2 * N_SELECT]
    (q_ref, ocmp_ref, gn_ref, rowtab_ref, newk_ref, newv_ref, swk_ref, swv_ref, nwk_ref, nwv_ref,
     out_ref) = refs[2 * N_SELECT:]
    b = pl.program_id(0)
    hk = pl.program_id(1)
    tok = pl.program_id(2)
    t = past_len + tok
    base = ((b * N_KV_HEADS + hk) * n_tok + tok) * N_SELECT
    qp = (q_ref[0, 0, 0].astype(F32) * (HEAD_DIM ** -0.5)).astype(BF16)
    rowtab = rowtab_ref[0]
    rvalid = lax.broadcasted_iota(jnp.int32, (ROWS_PER_TOKEN, 1), 0) < GROUP
    init = (jnp.full((ROWS_PER_TOKEN, 1), NEG_INF, F32), jnp.zeros((ROWS_PER_TOKEN, 1), F32),
            jnp.zeros((ROWS_PER_TOKEN, HEAD_DIM), F32))
    halves = PAGE // SEL_BLOCK

    n_keys = N_SELECT * PAGE
    lane = lax.broadcasted_iota(jnp.int32, (ROWS_PER_TOKEN, n_keys), 1)
    row = lane % PAGE
    pos = row
    in_block = jnp.zeros((ROWS_PER_TOKEN, n_keys), jnp.bool_)
    kts, vts = [], []
    for n in range(N_SELECT):
        blk = idx_ref[base + n]
        is_new = blk >= n_cache_blocks
        kts.append(jnp.where(is_new, newk_ref[0, 0], k_refs[n][0, 0]).astype(BF16))
        vts.append(jnp.where(is_new, newv_ref[0, 0], v_refs[n][0, 0]).astype(BF16))
        slot = lane // PAGE == n
        pos = jnp.where(slot, row + (blk // halves) * PAGE, pos)
        in_block = in_block | (slot & (row // SEL_BLOCK == blk % halves))
    dist = t - pos
    s = _dot(qp, jnp.concatenate(kts, axis=1)) + _bias_from_rows(_bucket(dist), rowtab)
    mask = (dist >= 0) & in_block & rvalid
    o_slc = _softmax_finish(_softmax_update(init, jnp.where(mask, s, NEG_INF), jnp.concatenate(vts, axis=1),
                                            mask=mask, v_transposed=True))

    n_win = swk_ref.shape[3]
    kw = jnp.concatenate([swk_ref[0, 0], nwk_ref[0, 0]], axis=1).astype(BF16)
    vw = jnp.concatenate([swv_ref[0, 0], nwv_ref[0, 0]], axis=1).astype(BF16)
    wl = n_win + nwk_ref.shape[3]
    wpos = past_len - n_win + lax.broadcasted_iota(jnp.int32, (ROWS_PER_TOKEN, wl), 1)
    wdist = t - wpos
    s = _dot(qp, kw) + _bias_from_rows(_bucket(wdist), rowtab)
    mask = (wdist >= 0) & (wdist < WINDOW) & (wpos >= 0) & rvalid
    o_win = _softmax_finish(_softmax_update(init, jnp.where(mask, s, NEG_INF), vw, mask=mask, v_transposed=True))

    gs = jax.nn.sigmoid(gn_ref[0, 0, 0])
    out_ref[0, 0, 0] = gs[:, 0:1] * ocmp_ref[0, 0, 0] + gs[:, 1:2] * o_slc + gs[:, 2:3] * o_win


def _nsa_sample_slc(idx, page_table, cache_k, cache_v, q3, ocmp, gn3, rowtab, newk, newv, swk, swv, nwk, nwv,
                    past_len, n_tok):
    nb, npages = page_table.shape
    halves = PAGE // SEL_BLOCK
    n_cache_blocks = npages * halves
    ck = cache_k.transpose(0, 2, 3, 1)
    cv = cache_v.transpose(0, 2, 3, 1)

    def blk_spec(n):
        def imap(b, h, s, idx_ref, pt_ref):
            j = jnp.minimum(idx_ref[((b * N_KV_HEADS + h) * n_tok + s) * N_SELECT + n], n_cache_blocks - 1)
            return (pt_ref[b * npages + j // halves], h, 0, 0)
        return pl.BlockSpec((1, 1, HEAD_DIM, PAGE), imap)

    tile = lambda w: pl.BlockSpec((1, 1, 1, ROWS_PER_TOKEN, w), lambda b, h, s, i, p: (b, h, s, 0, 0))
    per_bh = lambda arr: pl.BlockSpec((1, 1) + arr.shape[2:], lambda b, h, s, i, p: (b, h, 0, 0))
    grid_spec = pltpu.PrefetchScalarGridSpec(
        num_scalar_prefetch=2,
        grid=(nb, N_KV_HEADS, n_tok),
        in_specs=[blk_spec(n) for n in range(N_SELECT)] * 2
        + [tile(HEAD_DIM), tile(HEAD_DIM), tile(LANES),
           pl.BlockSpec((1, ROWS_PER_TOKEN, N_BUCKETS), lambda b, h, s, i, p: (h, 0, 0)),
           per_bh(newk), per_bh(newv), per_bh(swk), per_bh(swv), per_bh(nwk), per_bh(nwv)],
        out_specs=tile(HEAD_DIM),
    )
    return pl.pallas_call(
        functools.partial(_nsa_sample_slc_kernel, past_len=past_len, n_tok=n_tok, n_cache_blocks=n_cache_blocks),
        grid_spec=grid_spec,
        out_shape=jax.ShapeDtypeStruct((nb, N_KV_HEADS, n_tok, ROWS_PER_TOKEN, HEAD_DIM), F32),
        compiler_params=_cparams(("arbitrary", "arbitrary", "arbitrary")),
        name="nsa_sample_slc",
    )(idx.reshape(-1), page_table.reshape(-1), *([ck] * N_SELECT), *([cv] * N_SELECT),
      q3, ocmp, gn3, rowtab, newk, newv, swk, swv, nwk, nwv)


def _merge_tail(x, a, c, gates, wn_ref, wc_ref, wo_ref):
    ga = gates[:, :D_MODEL]
    gb = gates[:, D_MODEL:]
    m = (jax.nn.sigmoid(ga) * _dot(a, wn_ref[...])
         + jax.nn.sigmoid(gb) * _dot(c.astype(BF16), wc_ref[...]))
    return x + _dot(m.astype(BF16), wo_ref[...])


def _merge_prompt_kernel(x_ref, a_ref, conv_ref, halo_ref, gate_ref, cw_ref, cb_ref, wn_ref, wc_ref, wo_ref,
                         out_ref, cst_ref, *, tiles_per_batch):
    i = pl.program_id(0)
    first = i % tiles_per_batch == 0
    conv = conv_ref[...]
    bg = conv[:, :CONV_DIM]
    u = conv[:, CONV_DIM:2 * CONV_DIM] * conv[:, 2 * CONV_DIM:]
    halo = halo_ref[...]
    uh = jnp.where(first, 0.0, halo[:, CONV_DIM:2 * CONV_DIM] * halo[:, 2 * CONV_DIM:])
    tm = u.shape[0]
    row = lax.broadcasted_iota(jnp.int32, (tm, CONV_DIM), 0)
    u1 = jnp.where(row == 0, uh[7:8], pltpu.roll(u, 1, axis=0))
    u2 = pltpu.roll(u, 2, axis=0)
    u2 = jnp.where(row == 0, uh[6:7], jnp.where(row == 1, uh[7:8], u2))
    cw = cw_ref[...]
    y = ((cb_ref[...] + cw[0:1] * u2) + cw[1:2] * u1) + cw[2:3] * u
    c = bg * y
    out_ref[...] = _merge_tail(x_ref[...], a_ref[...], c, gate_ref[...], wn_ref, wc_ref, wo_ref)
    cst_ref[0] = u[tm - 8:, :]


def _merge_prompt(x2d, a2, conv3, gates, conv_w, conv_b, wn, wc, wo, t):
    n = x2d.shape[0]
    tm = min(512, t)
    assert t % tm == 0
    tpb = t // tm
    row = lambda w: pl.BlockSpec((tm, w), lambda i: (i, 0))
    const = lambda arr: pl.BlockSpec(arr.shape, lambda i: (0,) * arr.ndim)
    return pl.pallas_call(
        functools.partial(_merge_prompt_kernel, tiles_per_batch=tpb),
        grid=(n // tm,),
        in_specs=[row(D_MODEL),
                  row(NSA_WIDTH),
                  row(3 * CONV_DIM),
                  pl.BlockSpec((8, 3 * CONV_DIM), lambda i: (jnp.maximum(i * (tm // 8) - 1, 0), 0)),
                  row(2 * D_MODEL),
                  const(conv_w), const(conv_b), const(wn), const(wc), const(wo)],
        out_specs=[row(D_MODEL), pl.BlockSpec((1, 8, CONV_DIM), lambda i: (i // tpb, 0, 0))],
        out_shape=[jax.ShapeDtypeStruct((n, D_MODEL), F32),
                   jax.ShapeDtypeStruct((n // t, 8, CONV_DIM), F32)],
        compiler_params=_cparams(("arbitrary",)),
        name="merge_prompt",
    )(x2d, a2, conv3, conv3, gates, conv_w, conv_b, wn, wc, wo)


def _merge_sample_kernel(x_ref, a_ref, conv_ref, past_ref, gate_ref, cw_ref, cb_ref, wn_ref, wc_ref, wo_ref,
                         out_ref, cst_ref, *, n_tok, nb):
    conv = conv_ref[...]
    bg = conv[:, :CONV_DIM]
    u = conv[:, CONV_DIM:2 * CONV_DIM] * conv[:, 2 * CONV_DIM:]
    up = jnp.concatenate([past_ref[...], u], axis=0)
    cw = cw_ref[...]
    y = cb_ref[...]
    for k in range(CONV_WIDTH):
        y = y + cw[k:k + 1] * up[k * nb:(k + n_tok) * nb]
    c = bg * y
    out_ref[...] = _merge_tail(x_ref[...], a_ref[...].astype(BF16), c, gate_ref[...], wn_ref, wc_ref, wo_ref)
    cst_ref[...] = up[n_tok * nb:]


def _merge_sample(x_tm, a_tm, conv_tm, past_tm, gates_tm, conv_w, conv_b, wn, wc, wo, n_tok, nb):
    n = x_tm.shape[0]
    full = lambda arr: pl.BlockSpec(arr.shape, lambda i: (0,) * arr.ndim)
    args = (x_tm, a_tm, conv_tm, past_tm, gates_tm, conv_w, conv_b, wn, wc, wo)
    return pl.pallas_call(
        functools.partial(_merge_sample_kernel, n_tok=n_tok, nb=nb),
        grid=(1,),
        in_specs=[full(a) for a in args],
        out_specs=[pl.BlockSpec((n, D_MODEL), lambda i: (0, 0)),
                   pl.BlockSpec(((CONV_WIDTH - 1) * nb, CONV_DIM), lambda i: (0, 0))],
        out_shape=[jax.ShapeDtypeStruct((n, D_MODEL), F32),
                   jax.ShapeDtypeStruct(((CONV_WIDTH - 1) * nb, CONV_DIM), F32)],
        compiler_params=_cparams(("arbitrary",)),
        name="merge_sample",
    )(*args)


def _rms(x, g):
    return (x * lax.rsqrt(jnp.mean(x * x, axis=-1, keepdims=True) + RMS_EPS)) * g


def _mlp_kernel(x_ref, g_ref, wu_ref, wd_ref, gf_ref, out_ref, h_ref, acc_ref):
    j = pl.program_id(1)

    @pl.when(j == 0)
    def _():
        h_ref[...] = _rms(x_ref[...], g_ref[...]).astype(BF16)

    up = jnp.square(jnp.maximum(_dot(h_ref[...], wu_ref[...]), 0.0)).astype(BF16)
    part = _dot(up, wd_ref[...])

    @pl.when(j == 0)
    def _():
        acc_ref[...] = part

    @pl.when(j > 0)
    def _():
        acc_ref[...] += part

    @pl.when(j == pl.num_programs(1) - 1)
    def _():
        out_ref[...] = _rms(x_ref[...] + acc_ref[...], gf_ref[...])


def _mlp(x2d, g_mlp, w_up, w_down, g_final):
    n = x2d.shape[0]
    tm = min(512, n)
    tf = 1024
    assert n % tm == 0 and D_FF % tf == 0
    return pl.pallas_call(
        _mlp_kernel,
        grid=(n // tm, D_FF // tf),
        in_specs=[pl.BlockSpec((tm, D_MODEL), lambda i, j: (i, 0)),
                  pl.BlockSpec((1, D_MODEL), lambda i, j: (0, 0)),
                  pl.BlockSpec((D_MODEL, tf), lambda i, j: (0, j)),
                  pl.BlockSpec((tf, D_MODEL), lambda i, j: (j, 0)),
                  pl.BlockSpec((1, D_MODEL), lambda i, j: (0, 0))],
        out_specs=pl.BlockSpec((tm, D_MODEL), lambda i, j: (i, 0)),
        out_shape=jax.ShapeDtypeStruct((n, D_MODEL), F32),
        scratch_shapes=[pltpu.VMEM((tm, D_MODEL), BF16), pltpu.VMEM((tm, D_MODEL), F32)],
        compiler_params=_cparams(("arbitrary", "arbitrary")),
        name="mlp",
    )(x2d, g_mlp, w_up, w_down, g_final)


def _kv_rows(z, lead):
    return z.reshape(lead + (N_KV_HEADS, HEAD_DIM))


def _layer(xp, xs, caches, states, page_table, w, rel_bias):
    (cache_cmp_k, cache_cmp_v, cache_slc_k, cache_slc_v) = caches
    (state_win_k, state_win_v, state_conv) = states
    nbp, t, _ = xp.shape
    nbs, n_tok, _ = xs.shape
    npages = page_table.shape[1]
    past_len = npages * PAGE
    assert t % PAGE == 0 and n_tok <= GROUP and state_win_k.shape[1] >= n_tok

    w_all = _prep_w_in(w["w_in"])
    g_attn = w["g_attn"].reshape(1, D_MODEL)
    wn = w["w_nsa_out"].reshape(N_KV_HEADS, GROUP, HEAD_DIM, D_MODEL).transpose(1, 0, 2, 3)
    wn = wn.reshape(NSA_WIDTH, D_MODEL).astype(BF16)
    wc = w["w_conv_out"].astype(BF16)
    wo = w["w_o"].astype(BF16)
    wu = w["w_up"].astype(BF16)
    wd = w["w_down"].astype(BF16)
    g_mlp = w["g_mlp"].reshape(1, D_MODEL)
    g_final = w["g_final"].reshape(1, D_MODEL)
    conv_w = w["conv_w"]
    conv_b = w["conv_b"].reshape(1, CONV_DIM)
    cmp_k = (w["cmp_pe_k"], w["cmp_w1_k"], w["cmp_w2_k"])
    cmp_v = (w["cmp_pe_v"], w["cmp_w1_v"], w["cmp_w2_v"])

    n_p = nbp * t
    (q, kc, vc, ks, vs, kw, vw, kvb, gn, conv3, gates) = _proj(xp.reshape(n_p, D_MODEL), g_attn, w_all)
    ident = jnp.arange(nbp * (t // PAGE), dtype=jnp.int32).reshape(nbp, t // PAGE)
    kcmp = _compress(kc.reshape(n_p // PAGE, PAGE, KV_WIDTH), 0, ident, *cmp_k)
    vcmp = _compress(vc.reshape(n_p // PAGE, PAGE, KV_WIDTH), 0, ident, *cmp_v)
    a2 = _nsa_prompt_pair(q.reshape(nbp, t, NSA_WIDTH), gn.reshape(nbp, t, LANES), kcmp, vcmp,
                          kvb.reshape(nbp, t, 6 * KV_WIDTH), rel_bias)
    x1, cst = _merge_prompt(xp.reshape(n_p, D_MODEL), a2.reshape(n_p, NSA_WIDTH),
                            conv3, gates, conv_w, conv_b, wn, wc, wo, t)
    yp = _mlp(x1, g_mlp, wu, wd, g_final).reshape(nbp, t, D_MODEL)
    win_p = min(WINDOW, t)
    new_p = tuple(_kv_rows(z, (nbp, t)) for z in (kc, vc, ks, vs)) + (
        _kv_rows(kw, (nbp, t))[:, t - win_p:], _kv_rows(vw, (nbp, t))[:, t - win_p:],
        cst[:, 8 - (CONV_WIDTH - 1):, :])

    n_s = nbs * n_tok
    (qs, kcs, vcs, kss, vss, kws, vws, _, gns, conv3s, gatess) = _proj(xs.reshape(n_s, D_MODEL), g_attn, w_all)
    n_rows = past_len + n_tok
    n_chunk = n_rows // CMP_STRIDE
    assert n_chunk == past_len // CMP_STRIDE, "new rows must not complete a compression chunk"
    n_cmp = n_chunk - CMP_RATIO + 1
    n_blocks = -(-n_rows // SEL_BLOCK)
    kcmp_s = _compress_t(cache_cmp_k, page_table, *cmp_k)
    vcmp_s = _compress_t(cache_cmp_v, page_table, *cmp_v)

    lane = np.arange(LANES)
    q5 = qs.reshape(nbs, n_tok, GROUP, LANES)
    q3 = jnp.stack([jnp.where(jnp.asarray((lane // HEAD_DIM) == h), q5, jnp.zeros_like(q5))
                    for h in range(N_KV_HEADS)], axis=1)
    q3 = jnp.pad(q3, ((0, 0), (0, 0), (0, 0), (0, ROWS_PER_TOKEN - GROUP), (0, 0)))
    tab = rel_bias.T.reshape(N_KV_HEADS, GROUP, N_BUCKETS)
    rowtab3 = jnp.pad(tab, ((0, 0), (0, ROWS_PER_TOKEN - GROUP), (0, 0)))
    rowtab = jnp.tile(rowtab3, (1, n_tok, 1))
    ocmp, idx = _nsa_sample_cmp(q3.reshape(nbs, N_KV_HEADS, n_tok * ROWS_PER_TOKEN, LANES), kcmp_s, vcmp_s,
                                rowtab, past_len, n_tok, n_cmp, n_blocks)
    idx = idx[:, :, :n_tok, :N_SELECT]
    gn3 = gns[:, :3 * N_HEADS].reshape(nbs, n_tok, N_KV_HEADS, GROUP, 3).transpose(0, 2, 1, 3, 4)
    gn3 = jnp.pad(gn3, ((0, 0), (0, 0), (0, 0), (0, ROWS_PER_TOKEN - GROUP), (0, LANES - 3)))
    q3s = jnp.pad(qs.reshape(nbs, n_tok, GROUP, N_KV_HEADS, HEAD_DIM).transpose(0, 3, 1, 2, 4),
                  ((0, 0), (0, 0), (0, 0), (0, ROWS_PER_TOKEN - GROUP), (0, 0)))
    ocmp5 = ocmp.reshape(nbs, N_KV_HEADS, n_tok, ROWS_PER_TOKEN, HEAD_DIM)
    new_t = lambda z: jnp.pad(z.reshape(nbs, n_tok, N_KV_HEADS, HEAD_DIM).transpose(0, 2, 3, 1),
                              ((0, 0), (0, 0), (0, 0), (0, LANES - n_tok)))
    a_s = _nsa_sample_slc(idx, page_table, cache_slc_k, cache_slc_v, q3s, ocmp5, gn3, rowtab3,
                          new_t(kss), new_t(vss),
                          state_win_k.transpose(0, 2, 3, 1), state_win_v.transpose(0, 2, 3, 1),
                          new_t(kws), new_t(vws), past_len, n_tok)
    a_tm = a_s[:, :, :, :GROUP, :].transpose(2, 0, 3, 1, 4).reshape(n_s, NSA_WIDTH)
    to_tm = lambda z: z.reshape(nbs, n_tok, -1).transpose(1, 0, 2).reshape(n_s, -1)
    past_tm = state_conv.transpose(1, 0, 2).reshape((CONV_WIDTH - 1) * nbs, CONV_DIM)
    x1s, csts = _merge_sample(to_tm(xs), a_tm, to_tm(conv3s), past_tm, to_tm(gatess),
                              conv_w, conv_b, wn, wc, wo, n_tok, nbs)
    ys = _mlp(x1s, g_mlp, wu, wd, g_final).reshape(n_tok, nbs, D_MODEL).transpose(1, 0, 2)
    win_buf = state_win_k.shape[1]
    kw_all = jnp.concatenate([state_win_k, _kv_rows(kws, (nbs, n_tok))], axis=1)[:, -win_buf:]
    vw_all = jnp.concatenate([state_win_v, _kv_rows(vws, (nbs, n_tok))], axis=1)[:, -win_buf:]
    new_s = tuple(_kv_rows(z, (nbs, n_tok)) for z in (kcs, vcs, kss, vss)) + (
        kw_all, vw_all, csts.reshape(CONV_WIDTH - 1, nbs, CONV_DIM).transpose(1, 0, 2))
    return yp, ys, new_p, new_s


def kernel(x_prompt, x_sample, cache_cmp_k, cache_cmp_v, cache_slc_k, cache_slc_v, state_win_k, state_win_v,
           state_conv, page_table, g_attn, w_in, cmp_pe_k, cmp_w1_k, cmp_w2_k, cmp_pe_v, cmp_w1_v, cmp_w2_v,
           conv_w, conv_b, w_nsa_out, w_conv_out, w_o, g_mlp, w_up, w_down, rel_bias, g_final):
    depth = w_in.shape[0]
    assert depth == 1, "the final norm is fused into the last layer's MLP kernel"
    names = ("g_attn", "w_in", "cmp_pe_k", "cmp_w1_k", "cmp_w2_k", "cmp_pe_v", "cmp_w1_v", "cmp_w2_v",
             "conv_w", "conv_b", "w_nsa_out", "w_conv_out", "w_o", "g_mlp", "w_up", "w_down")
    vals = (g_attn, w_in, cmp_pe_k, cmp_w1_k, cmp_w2_k, cmp_pe_v, cmp_w1_v, cmp_w2_v,
            conv_w, conv_b, w_nsa_out, w_conv_out, w_o, g_mlp, w_up, w_down)
    l = 0
    w = {k: v[l] for k, v in zip(names, vals)}
    w["g_final"] = g_final
    yp, ys, new_p, new_s = _layer(
        x_prompt, x_sample,
        (cache_cmp_k[l], cache_cmp_v[l], cache_slc_k[l], cache_slc_v[l]),
        (state_win_k[l], state_win_v[l], state_conv[l]), page_table, w, rel_bias)
    return (yp, ys) + tuple(z[None] for z in new_p) + tuple(z[None] for z in new_s)
```

```python
import functools
import math

import numpy as np
import jax
import jax.numpy as jnp
from jax import lax
from jax.experimental import pallas as pl
from jax.experimental.pallas import tpu as pltpu

F32 = jnp.float32
BF16 = jnp.bfloat16

D_MODEL = 1024
N_HEADS = 8
HEAD_DIM = 64
N_KV_HEADS = 2
GROUP = N_HEADS // N_KV_HEADS
KV_WIDTH = N_KV_HEADS * HEAD_DIM
NSA_WIDTH = N_HEADS * HEAD_DIM
CMP_LEN = 32
CMP_STRIDE = 16
CMP_RATIO = CMP_LEN // CMP_STRIDE
CMP_HIDDEN = 256
SEL_BLOCK = 64
N_SELECT = 16
WINDOW = 512
CONV_DIM = 512
CONV_WIDTH = 3
D_FF = 4 * D_MODEL
N_BUCKETS = 32
MAX_DISTANCE = 128
Q_BLOCK = 128
PAGE = 128
RMS_EPS = 1e-6
NEG_INF = -1e30
FORCE_BONUS = 1e4
LANES = 128
VMEM_LIMIT = 56 * 1024 * 1024


def _bucket_thresholds():
    max_exact = N_BUCKETS // 2
    n = np.arange(max_exact, 4 * MAX_DISTANCE).astype(np.float32)
    large = max_exact + (np.log(n / np.float32(max_exact)) / np.float32(math.log(MAX_DISTANCE / max_exact))
                         * np.float32(N_BUCKETS - max_exact)).astype(np.int32)
    large = np.minimum(large, N_BUCKETS - 1)
    thr = []
    for k in range(max_exact + 1, N_BUCKETS):
        thr.append(int(np.arange(max_exact, 4 * MAX_DISTANCE)[np.argmax(large >= k)]))
    return tuple(thr)


BUCKET_THR = _bucket_thresholds()


def _bucket(dist):
    n = jnp.maximum(dist, 0)
    big = jnp.full(n.shape, N_BUCKETS // 2, jnp.int32)
    for thr in BUCKET_THR:
        big = big + (n >= thr).astype(jnp.int32)
    return jnp.where(n < N_BUCKETS // 2, n, big)


def _bias_from_scalars(bucket, table_fn):
    out = jnp.zeros(bucket.shape, F32)
    for k in range(N_BUCKETS):
        out = jnp.where(bucket == k, table_fn(k), out)
    return out


def _bias_from_rows(bucket, rowtab):
    out = jnp.zeros(bucket.shape, F32)
    for k in range(N_BUCKETS):
        out = jnp.where(bucket == k, rowtab[:, k:k + 1], out)
    return out


def _dot(a, b):
    return jnp.dot(a, b, preferred_element_type=F32)


def _dot_t(a, b):
    return lax.dot_general(a, b, (((1,), (1,)), ((), ())), preferred_element_type=F32)


def _split_dot(p, m):
    hi = p.astype(BF16)
    lo = (p - hi.astype(F32)).astype(BF16)
    return _dot(hi, m) + _dot(lo, m)


def _cparams(sem):
    return pltpu.CompilerParams(dimension_semantics=sem, vmem_limit_bytes=VMEM_LIMIT)


PROJ_WIDTHS = (NSA_WIDTH, 6 * KV_WIDTH, LANES, 3 * CONV_DIM, 2 * D_MODEL)
PROJ_COLS = sum(PROJ_WIDTHS)


def _proj_kernel(x_ref, g_ref, w_ref, q_ref, kc_ref, vc_ref, ks_ref, vs_ref, kw_ref, vw_ref,
                 kvb_ref, gn_ref, conv_ref, gate_ref):
    x = x_ref[...]
    y = x * lax.rsqrt(jnp.mean(x * x, axis=-1, keepdims=True) + RMS_EPS)
    hb = (y * g_ref[...]).astype(BF16)
    q_ref[...] = _dot(hb, w_ref[:, 0:NSA_WIDTH]).astype(q_ref.dtype)
    off = NSA_WIDTH
    kv = _dot(hb, w_ref[:, off:off + 6 * KV_WIDTH])
    for i, r in enumerate((kc_ref, vc_ref, ks_ref, vs_ref, kw_ref, vw_ref)):
        r[...] = kv[:, i * KV_WIDTH:(i + 1) * KV_WIDTH]
    kvb_ref[...] = kv.astype(BF16)
    off += 6 * KV_WIDTH
    gn_ref[...] = _dot(hb, w_ref[:, off:off + LANES])
    off += LANES
    conv_ref[...] = _dot(hb, w_ref[:, off:off + 3 * CONV_DIM])
    off += 3 * CONV_DIM
    gate_ref[...] = _dot(hb, w_ref[:, off:off + 2 * D_MODEL])


def _proj(x2d, g, w_all):
    n = x2d.shape[0]
    tm = min(256, n)
    assert n % tm == 0
    row = lambda w: pl.BlockSpec((tm, w), lambda i: (i, 0))
    out_shapes = ([jax.ShapeDtypeStruct((n, NSA_WIDTH), BF16)]
                  + [jax.ShapeDtypeStruct((n, KV_WIDTH), F32)] * 6
                  + [jax.ShapeDtypeStruct((n, 6 * KV_WIDTH), BF16),
                     jax.ShapeDtypeStruct((n, LANES), F32),
                     jax.ShapeDtypeStruct((n, 3 * CONV_DIM), F32),
                     jax.ShapeDtypeStruct((n, 2 * D_MODEL), F32)])
    out_specs = ([row(NSA_WIDTH)] + [row(KV_WIDTH)] * 6
                 + [row(6 * KV_WIDTH), row(LANES), row(3 * CONV_DIM), row(2 * D_MODEL)])
    return pl.pallas_call(
        _proj_kernel,
        grid=(n // tm,),
        in_specs=[row(D_MODEL),
                  pl.BlockSpec((1, D_MODEL), lambda i: (0, 0)),
                  pl.BlockSpec((D_MODEL, PROJ_COLS), lambda i: (0, 0))],
        out_specs=out_specs,
        out_shape=out_shapes,
        compiler_params=_cparams(("arbitrary",)),
        name="proj",
    )(x2d, g, w_all)


def _prep_w_in(w_in):
    sizes = (NSA_WIDTH,) + (KV_WIDTH,) * 6 + (3 * N_HEADS, CONV_DIM, CONV_DIM, CONV_DIM, D_MODEL, D_MODEL)
    offs = np.concatenate([[0], np.cumsum(sizes)])
    wq = w_in[:, :NSA_WIDTH].reshape(D_MODEL, N_KV_HEADS, GROUP, HEAD_DIM)
    wq = wq.transpose(0, 2, 1, 3).reshape(D_MODEL, NSA_WIDTH)
    wkv = w_in[:, offs[1]:offs[7]]
    wgn = jnp.pad(w_in[:, offs[7]:offs[8]], ((0, 0), (0, LANES - 3 * N_HEADS)))
    wrest = w_in[:, offs[8]:]
    return jnp.concatenate([wq, wkv, wgn, wrest], axis=1).astype(BF16)


CHUNKS_PER_PAGE = PAGE // CMP_STRIDE


def _compress_kernel(pt_ref, *refs, pp):
    del pt_ref
    page_refs = refs[:pp + 1]
    wbd_ref, pe_ref, w1_ref, w2_ref, out_ref, acc_ref, pehid_ref = refs[pp + 1:]

    @pl.when((pl.program_id(0) == 0) & (pl.program_id(1) == 0))
    def _():
        pehid_ref[...] = _dot(pe_ref[...], w1_ref[...])

    cols = []
    for r in range(CMP_STRIDE):
        xr = jnp.concatenate([p[0, pl.ds(r, CHUNKS_PER_PAGE, stride=CMP_STRIDE), :] for p in page_refs], axis=0)
        cols.append(xr.astype(BF16))
    x = jnp.concatenate(cols, axis=1)
    acc_ref[...] = _dot(x, wbd_ref[...])
    nc = CHUNKS_PER_PAGE * pp
    pe = pehid_ref[0:1, :]
    hids = []
    for h in range(N_KV_HEADS):
        base = h * CMP_RATIO * CMP_HIDDEN
        p0 = acc_ref[0:nc, base:base + CMP_HIDDEN]
        p1 = acc_ref[pl.ds(1, nc), base + CMP_HIDDEN:base + 2 * CMP_HIDDEN]
        hids.append(jax.nn.gelu((pe + p0) + p1).astype(BF16))
    out_ref[0] = _dot(jnp.concatenate(hids, axis=1), w2_ref[...])


def _compress(pages, colblk, page_table, pe, w1, w2):
    nb, npages = page_table.shape
    pp = min(32, npages)
    assert npages % pp == 0
    nc = CHUNKS_PER_PAGE * pp
    w1r = w1.reshape(CMP_RATIO, CMP_STRIDE, HEAD_DIM, CMP_HIDDEN)
    blk = w1r.transpose(1, 2, 0, 3).reshape(CMP_STRIDE, HEAD_DIM, CMP_RATIO * CMP_HIDDEN)
    z = jnp.zeros_like(blk)
    wbd = jnp.concatenate([jnp.concatenate([blk, z], axis=2), jnp.concatenate([z, blk], axis=2)], axis=1)
    wbd = wbd.reshape(CMP_STRIDE * KV_WIDTH, N_KV_HEADS * CMP_RATIO * CMP_HIDDEN).astype(BF16)
    z2 = jnp.zeros_like(w2)
    w2bd = jnp.concatenate([jnp.concatenate([w2, z2], axis=1), jnp.concatenate([z2, w2], axis=1)], axis=0).astype(BF16)
    pe8 = jnp.broadcast_to(pe.reshape(1, CMP_LEN * HEAD_DIM), (8, CMP_LEN * HEAD_DIM)).astype(BF16)

    def page_spec(p):
        def imap(b, s, pt):
            return (pt[b, jnp.minimum(s * pp + p, npages - 1)], 0, colblk)
        return pl.BlockSpec((1, PAGE, KV_WIDTH), imap)

    const = lambda shape: pl.BlockSpec(shape, lambda b, s, pt: (0,) * len(shape))
    grid_spec = pltpu.PrefetchScalarGridSpec(
        num_scalar_prefetch=1,
        grid=(nb, npages // pp),
        in_specs=[page_spec(p) for p in range(pp + 1)]
        + [const(wbd.shape), const(pe8.shape), const((CMP_LEN * HEAD_DIM, CMP_HIDDEN)), const(w2bd.shape)],
        out_specs=pl.BlockSpec((1, nc, KV_WIDTH), lambda b, s, pt: (b, s, 0)),
        scratch_shapes=[pltpu.VMEM((nc + 8, N_KV_HEADS * CMP_RATIO * CMP_HIDDEN), F32),
                        pltpu.VMEM((8, CMP_HIDDEN), F32)],
    )
    return pl.pallas_call(
        functools.partial(_compress_kernel, pp=pp),
        grid_spec=grid_spec,
        out_shape=jax.ShapeDtypeStruct((nb, npages * CHUNKS_PER_PAGE, KV_WIDTH), F32),
        compiler_params=_cparams(("arbitrary", "arbitrary")),
        name="compress",
    )(page_table, *([pages] * (pp + 1)), wbd, pe8, w1.astype(BF16), w2bd)


def _compress_t_kernel(pt_ref, *refs, pp, npages):
    (cache_ref, w1p_ref, pe_ref, w1_ref, w2_ref, out_ref, pbuf_ref, sem_ref, xt_ref, acc_ref, pehid_ref) = refs
    b = pl.program_id(0)
    s = pl.program_id(1)
    n_steps = pl.num_programs(1)
    step = b * n_steps + s
    last = pl.num_programs(0) * n_steps - 1
    slot = step % 2

    def page_copy(bb, ss, p, sl):
        page = pt_ref[bb, jnp.minimum(ss * pp + p, npages - 1)]
        return pltpu.make_async_copy(cache_ref.at[page], pbuf_ref.at[sl, p], sem_ref.at[sl])

    @pl.when(step == 0)
    def _():
        pehid_ref[...] = _dot(pe_ref[...], w1_ref[...])
        for p in range(pp + 1):
            page_copy(0, 0, p, 0).start()

    wrap = s + 1 == n_steps
    nxt_b = jnp.minimum(jnp.where(wrap, b + 1, b), pl.num_programs(0) - 1)
    nxt_s = jnp.where(step == last, s, jnp.where(wrap, 0, s + 1))
    for p in range(pp + 1):
        page_copy(nxt_b, nxt_s, p, 1 - slot).start()
    for p in range(pp + 1):
        page_copy(b, s, p, slot).wait()

    for p in range(pp + 1):
        for h in range(N_KV_HEADS):
            kt = pbuf_ref[slot, p, h]
            z = jnp.concatenate([kt, pltpu.roll(kt, PAGE - 1, axis=1)], axis=0)
            xt_ref[h, p * PAGE:(p + 1) * PAGE, :] = z.T
    nc = CHUNKS_PER_PAGE * pp
    m = nc + CHUNKS_PER_PAGE
    pe = pehid_ref[0:1, :]
    hids = []
    for h in range(N_KV_HEADS):
        x = jnp.concatenate([xt_ref[h, pl.ds(r, m, stride=CMP_STRIDE), :].astype(BF16)
                             for r in range(0, CMP_STRIDE, 2)], axis=1)
        acc_ref[h] = _dot(x, w1p_ref[...])
        p0 = acc_ref[h, 0:nc, 0:CMP_HIDDEN]
        p1 = acc_ref[h, pl.ds(1, nc), CMP_HIDDEN:2 * CMP_HIDDEN]
        hids.append(jax.nn.gelu((pe + p0) + p1).astype(BF16))
    out_ref[0] = _dot(jnp.concatenate(hids, axis=1), w2_ref[...])

    @pl.when(step == last)
    def _():
        for p in range(pp + 1):
            page_copy(nxt_b, nxt_s, p, 1 - slot).wait()


def _compress_t(cache, page_table, pe, w1, w2):
    nb, npages = page_table.shape
    pp = min(32, npages)
    assert npages % pp == 0
    nc = CHUNKS_PER_PAGE * pp
    pages = cache.transpose(0, 2, 3, 1)
    w1p = w1.reshape(CMP_RATIO, CMP_STRIDE * HEAD_DIM, CMP_HIDDEN).transpose(1, 0, 2)
    w1p = w1p.reshape(CMP_STRIDE * HEAD_DIM, CMP_RATIO * CMP_HIDDEN).astype(BF16)
    z2 = jnp.zeros_like(w2)
    w2bd = jnp.concatenate([jnp.concatenate([w2, z2], axis=1), jnp.concatenate([z2, w2], axis=1)], axis=0).astype(BF16)
    pe8 = jnp.broadcast_to(pe.reshape(1, CMP_LEN * HEAD_DIM), (8, CMP_LEN * HEAD_DIM)).astype(BF16)

    const = lambda shape: pl.BlockSpec(shape, lambda b, s, pt: (0,) * len(shape))
    grid_spec = pltpu.PrefetchScalarGridSpec(
        num_scalar_prefetch=1,
        grid=(nb, npages // pp),
        in_specs=[pl.BlockSpec(memory_space=pl.ANY),
                  const(w1p.shape), const(pe8.shape), const((CMP_LEN * HEAD_DIM, CMP_HIDDEN)), const(w2bd.shape)],
        out_specs=pl.BlockSpec((1, nc, KV_WIDTH), lambda b, s, pt: (b, s, 0)),
        scratch_shapes=[pltpu.VMEM((2, pp + 1, N_KV_HEADS, HEAD_DIM, PAGE), F32),
                        pltpu.SemaphoreType.DMA((2,)),
                        pltpu.VMEM((N_KV_HEADS, (pp + 1) * PAGE, 2 * HEAD_DIM), F32),
                        pltpu.VMEM((N_KV_HEADS, nc + CHUNKS_PER_PAGE, CMP_RATIO * CMP_HIDDEN), F32),
                        pltpu.VMEM((8, CMP_HIDDEN), F32)],
    )
    return pl.pallas_call(
        functools.partial(_compress_t_kernel, pp=pp, npages=npages),
        grid_spec=grid_spec,
        out_shape=jax.ShapeDtypeStruct((nb, npages * CHUNKS_PER_PAGE, KV_WIDTH), F32),
        compiler_params=_cparams(("arbitrary", "arbitrary")),
        name="compress_t",
    )(page_table, pages, w1p, pe8, w1.astype(BF16), w2bd)


def _sel_map(n_rows, n_cmp, n_cols, n_blocks):
    c0 = np.arange(n_rows)[:, None] * CMP_STRIDE
    s0 = np.arange(n_cols)[None, :] * SEL_BLOCK
    ov = np.minimum(c0 + CMP_LEN, s0 + SEL_BLOCK) - np.maximum(c0, s0)
    m = np.clip(ov, 0, None).astype(np.float32) / CMP_LEN
    m[n_cmp:] = 0.0
    m[:, n_blocks:] = 0.0
    return m


CMP_FRONT = 16
CMP_NEAR = 24
SLC_NEAR = 2 * Q_BLOCK
SLC_TILE = 512
WIN_KEYS = WINDOW + Q_BLOCK


def _softmax_update(state, s, v, mask=None, v_transposed=False):
    m_old, l_old, acc = state
    m_new = jnp.maximum(m_old, jnp.max(s, axis=-1, keepdims=True))
    p = jnp.exp(s - m_new)
    if mask is not None:
        p = jnp.where(mask, p, 0.0)
    alpha = jnp.exp(m_old - m_new)
    l_new = alpha * l_old + jnp.sum(p, axis=-1, keepdims=True)
    pv = _dot_t(p.astype(BF16), v) if v_transposed else _dot(p.astype(BF16), v)
    return m_new, l_new, alpha * acc + pv


def _softmax_finish(state):
    _, l, acc = state
    return jnp.where(l > 0.0, acc / jnp.where(l > 0.0, l, 1.0), 0.0)


def _tile_rows(x, n):
    return jnp.concatenate([x] * n, axis=0)


def _nsa_prompt_kernel(rb_ref, q_ref, gn_ref, kcmp_ref, vcmp_ref, selmap_ref, ks_ref, vs_ref, kw_ref, vw_ref,
                       et_ref, out_ref, nbc_ref, nbs_ref, nbw_ref):
    hk = pl.program_id(1)
    qb = pl.program_id(2)
    s0 = qb * Q_BLOCK
    gq = GROUP * Q_BLOCK

    @pl.when(qb == 0)
    def _():
        for g in range(GROUP):
            tab = lambda k, g=g: rb_ref[k, hk * GROUP + g]
            ql = lax.broadcasted_iota(jnp.int32, (Q_BLOCK, LANES), 0)
            cl = lax.broadcasted_iota(jnp.int32, (Q_BLOCK, LANES), 1)
            dist = ql - CMP_STRIDE * (cl - CMP_FRONT) - (CMP_LEN - 1)
            ok = (dist >= 0) & (cl < CMP_NEAR)
            nbc_ref[g] = jnp.where(ok, _bias_from_scalars(_bucket(dist), tab), NEG_INF)
            ql = lax.broadcasted_iota(jnp.int32, (Q_BLOCK, SLC_NEAR), 0)
            kl = lax.broadcasted_iota(jnp.int32, (Q_BLOCK, SLC_NEAR), 1)
            dist = ql - kl + Q_BLOCK
            nbs_ref[g] = jnp.where(dist >= 0, _bias_from_scalars(_bucket(dist), tab), NEG_INF)
            ql = lax.broadcasted_iota(jnp.int32, (Q_BLOCK, WIN_KEYS), 0)
            kl = lax.broadcasted_iota(jnp.int32, (Q_BLOCK, WIN_KEYS), 1)
            dist = ql - kl + WINDOW
            ok = (dist >= 0) & (dist < WINDOW)
            nbw_ref[g] = jnp.where(ok, _bias_from_scalars(_bucket(dist), tab), NEG_INF)

    lane = lax.broadcasted_iota(jnp.int32, (Q_BLOCK, LANES), 1)
    lmask = (lane >= hk * HEAD_DIM) & (lane < (hk + 1) * HEAD_DIM)
    qblk = q_ref[0].astype(F32)
    qp = jnp.concatenate(
        [jnp.where(lmask, qblk[:, g * LANES:(g + 1) * LANES], 0.0) for g in range(GROUP)], axis=0)
    qp = (qp * (HEAD_DIM ** -0.5)).astype(BF16)

    far_bias = jnp.concatenate(
        [jnp.full((Q_BLOCK, 1), rb_ref[N_BUCKETS - 1, hk * GROUP + g], F32) for g in range(GROUP)], axis=0)

    n_cmp_pad = kcmp_ref.shape[1] - CMP_FRONT - (LANES - CMP_FRONT)
    kc_far = kcmp_ref[0, CMP_FRONT:CMP_FRONT + n_cmp_pad, :].astype(BF16)
    vc_far = vcmp_ref[0, CMP_FRONT:CMP_FRONT + n_cmp_pad, :].astype(BF16)
    near0 = pl.multiple_of(qb * (Q_BLOCK // CMP_STRIDE), 8)
    kc_near = kcmp_ref[0, pl.ds(near0, LANES), :].astype(BF16)
    vc_near = vcmp_ref[0, pl.ds(near0, LANES), :].astype(BF16)
    c8 = qb * (Q_BLOCK // CMP_STRIDE)
    ccol = lax.broadcasted_iota(jnp.int32, (1, n_cmp_pad), 1)
    s_far = _dot_t(qp, kc_far) + jnp.where(ccol < c8 - CMP_FRONT, 0.0, NEG_INF)
    ncol = lax.broadcasted_iota(jnp.int32, (1, LANES), 1)
    s_near = _dot_t(qp, kc_near) + (nbc_ref[...].reshape(gq, LANES)
                                    + jnp.where(ncol >= CMP_FRONT - c8, 0.0, NEG_INF))
    m = jnp.maximum(jnp.max(s_far, axis=-1, keepdims=True) + far_bias, jnp.max(s_near, axis=-1, keepdims=True))
    p_far = jnp.exp(s_far - (m - far_bias))
    p_near = jnp.exp(s_near - m)
    l = jnp.sum(p_far, axis=-1, keepdims=True) + jnp.sum(p_near, axis=-1, keepdims=True)
    inv = jnp.where(m > 0.5 * NEG_INF, 1.0 / l, 0.0)
    pc_far = p_far * inv
    pc_near = p_near * inv
    o_cmp = _dot(pc_far.astype(BF16), vc_far) + _dot(pc_near.astype(BF16), vc_near)

    init = (jnp.full((gq, 1), NEG_INF, F32), jnp.zeros((gq, 1), F32), jnp.zeros((gq, LANES), F32))
    p0 = pl.multiple_of(s0, LANES)
    wcol = lax.broadcasted_iota(jnp.int32, (1, WIN_KEYS), 1)
    s = _dot_t(qp, kw_ref[0, pl.ds(p0, WIN_KEYS), :]) + (nbw_ref[...].reshape(gq, WIN_KEYS)
                                                         + jnp.where(wcol >= WINDOW - s0, 0.0, NEG_INF))
    o_win = _softmax_finish(_softmax_update(init, s, vw_ref[0, pl.ds(p0, WIN_KEYS), :]))

    ps_far = sum(pc_far[g * Q_BLOCK:(g + 1) * Q_BLOCK] for g in range(GROUP))
    ps_near = sum(pc_near[g * Q_BLOCK:(g + 1) * Q_BLOCK] for g in range(GROUP))
    sm_far = selmap_ref[CMP_FRONT:CMP_FRONT + n_cmp_pad, :].astype(BF16)
    sm_near = selmap_ref[pl.ds(near0, LANES), :].astype(BF16)
    imp = _split_dot(ps_far, sm_far) + _split_dot(ps_near, sm_near)
    imp_t = imp.T
    n_sel = imp_t.shape[0]
    blk = lax.broadcasted_iota(jnp.int32, (n_sel, Q_BLOCK), 0)
    cur = (s0 + lax.broadcasted_iota(jnp.int32, (n_sel, Q_BLOCK), 1)) // SEL_BLOCK
    forced = (blk == 0) | (blk == cur) | (blk == cur - 1)
    val = jnp.where(blk > cur, NEG_INF, imp_t + FORCE_BONUS * forced.astype(F32))
    sel_t = jnp.zeros((n_sel, Q_BLOCK), F32)
    for _ in range(N_SELECT):
        mx = jnp.max(val, axis=0, keepdims=True)
        first = jnp.min(jnp.where(val == mx, blk, n_sel), axis=0, keepdims=True)
        hit = blk == first
        sel_t = jnp.where(hit, 1.0, sel_t)
        val = jnp.where(hit, -3e38, val)
    sel = sel_t.T > 0.5

    jcol = lax.broadcasted_iota(jnp.int32, (Q_BLOCK, n_sel), 1)
    far_blocks = (s0 - Q_BLOCK) // SEL_BLOCK
    pen_far = jnp.where(sel & (jcol < far_blocks), 0.0, NEG_INF).astype(BF16)
    pen_near = jnp.where(sel, 0.0, NEG_INF).astype(BF16)
    q_far = jnp.concatenate([qp, _tile_rows(pen_far, GROUP)], axis=1)
    q_near = jnp.concatenate([qp, _tile_rows(pen_near, GROUP)], axis=1)

    def far_step(kt, state):
        p0 = pl.multiple_of(Q_BLOCK + kt * SLC_TILE, LANES)
        k = jnp.concatenate([ks_ref[0, pl.ds(p0, SLC_TILE), :], et_ref[pl.ds(p0, SLC_TILE), :]], axis=1)
        return _softmax_update(state, _dot_t(q_far, k), vs_ref[0, pl.ds(p0, SLC_TILE), :])

    n_far = (qb + 2) // (SLC_TILE // Q_BLOCK)
    m_far, l_far, acc_far = lax.fori_loop(0, n_far, far_step, init)
    state = (m_far + far_bias, l_far, acc_far)
    k = jnp.concatenate([ks_ref[0, pl.ds(p0, SLC_NEAR), :], et_ref[pl.ds(p0, SLC_NEAR), :]], axis=1)
    kcol = lax.broadcasted_iota(jnp.int32, (1, SLC_NEAR), 1)
    s = _dot_t(q_near, k) + (nbs_ref[...].reshape(gq, SLC_NEAR) + jnp.where(kcol >= Q_BLOCK - s0, 0.0, NEG_INF))
    o_slc = _softmax_finish(_softmax_update(state, s, vs_ref[0, pl.ds(p0, SLC_NEAR), :]))

    gs = jax.nn.sigmoid(gn_ref[0])

    def gate(g, i):
        c0, c1 = 3 * g + i, 3 * (GROUP + g) + i
        return jnp.where(hk == 0, gs[:, c0:c0 + 1], gs[:, c1:c1 + 1])

    for g in range(GROUP):
        rows = slice(g * Q_BLOCK, (g + 1) * Q_BLOCK)
        o = gate(g, 0) * o_cmp[rows] + gate(g, 1) * o_slc[rows] + gate(g, 2) * o_win[rows]
        out_ref[0, 0, :, g * LANES:(g + 1) * LANES] = jnp.where(lmask, o, 0.0).astype(out_ref.dtype)


def _nsa_prompt(q, gn, kcmp, vcmp, kvb, rel_bias):
    b, t, _ = q.shape
    assert t % SLC_TILE == 0
    nqb = t // Q_BLOCK
    n_chunk = t // CMP_STRIDE
    n_cmp = n_chunk - CMP_RATIO + 1
    n_blocks = t // SEL_BLOCK
    assert n_blocks <= LANES
    cpad = ((0, 0), (CMP_FRONT, LANES - CMP_FRONT), (0, 0))
    kcmp_p = jnp.pad(kcmp, cpad)
    vcmp_p = jnp.pad(vcmp, cpad)
    selmap = jnp.asarray(np.pad(_sel_map(n_chunk, n_cmp, LANES, n_blocks),
                                ((CMP_FRONT, LANES - CMP_FRONT), (0, 0))), F32)
    ks_p = jnp.pad(kvb[:, :, 2 * KV_WIDTH:3 * KV_WIDTH], ((0, 0), (Q_BLOCK, 0), (0, 0)))
    vs_p = jnp.pad(kvb[:, :, 3 * KV_WIDTH:4 * KV_WIDTH], ((0, 0), (Q_BLOCK, 0), (0, 0)))
    kw_p = jnp.pad(kvb[:, :, 4 * KV_WIDTH:5 * KV_WIDTH], ((0, 0), (WINDOW, 0), (0, 0)))
    vw_p = jnp.pad(kvb[:, :, 5 * KV_WIDTH:6 * KV_WIDTH], ((0, 0), (WINDOW, 0), (0, 0)))
    pidx = np.arange(t + Q_BLOCK)
    e = (pidx[:, None] // SEL_BLOCK - Q_BLOCK // SEL_BLOCK == np.arange(LANES)[None, :])
    e = jnp.asarray(e.astype(np.float32), BF16)
    full = lambda arr: pl.BlockSpec((1,) + arr.shape[1:], lambda bi, h, i: (bi, 0, 0))
    return pl.pallas_call(
        _nsa_prompt_kernel,
        grid=(b, N_KV_HEADS, nqb),
        in_specs=[pl.BlockSpec(memory_space=pltpu.SMEM),
                  pl.BlockSpec((1, Q_BLOCK, NSA_WIDTH), lambda bi, h, i: (bi, i, 0)),
                  pl.BlockSpec((1, Q_BLOCK, LANES), lambda bi, h, i: (bi, i, 0)),
                  full(kcmp_p), full(vcmp_p),
                  pl.BlockSpec(selmap.shape, lambda bi, h, i: (0, 0)),
                  full(ks_p), full(vs_p), full(kw_p), full(vw_p),
                  pl.BlockSpec(e.shape, lambda bi, h, i: (0, 0))],
        out_specs=pl.BlockSpec((1, 1, Q_BLOCK, NSA_WIDTH), lambda bi, h, i: (h, bi, i, 0)),
        out_shape=jax.ShapeDtypeStruct((N_KV_HEADS, b, t, NSA_WIDTH), BF16),
        scratch_shapes=[pltpu.VMEM((GROUP, Q_BLOCK, LANES), F32),
                        pltpu.VMEM((GROUP, Q_BLOCK, SLC_NEAR), F32),
                        pltpu.VMEM((GROUP, Q_BLOCK, WIN_KEYS), F32)],
        compiler_params=_cparams(("arbitrary", "arbitrary", "arbitrary")),
        name="nsa_prompt",
    )(rel_bias, q, gn, kcmp_p, vcmp_p, selmap, ks_p, vs_p, kw_p, vw_p, e)


def _col_softmax_update(state, s, vt):
    m_old, l_old, acc = state
    m_new = jnp.maximum(m_old, jnp.max(s, axis=0, keepdims=True))
    p = jnp.exp(s - m_new)
    alpha = jnp.exp(m_old - m_new)
    l_new = alpha * l_old + jnp.sum(p, axis=0, keepdims=True)
    return m_new, l_new, alpha * acc + _dot(vt, p.astype(BF16))


def _col_softmax_finish(state):
    _, l, acc = state
    return acc * jnp.where(l > 0.0, 1.0 / jnp.where(l > 0.0, l, 1.0), 0.0)


def _nsa_prompt_t_kernel(rb_ref, q_ref, gn_ref, kcmp_ref, vcmp_ref, vcmpt_ref, selmap_ref, selmapt_ref,
                         ks_ref, vst_ref, kw_ref, vwt_ref, et_ref, out_ref, nbc_ref, nbs_ref, nbw_ref):
    hk = pl.program_id(1)
    qb = pl.program_id(2)
    s0 = qb * Q_BLOCK
    gq = GROUP * Q_BLOCK

    @pl.when(qb == 0)
    def _():
        for g in range(GROUP):
            tab = lambda k, g=g: rb_ref[k, hk * GROUP + g]
            cols = slice(g * Q_BLOCK, (g + 1) * Q_BLOCK)
            cl = lax.broadcasted_iota(jnp.int32, (LANES, Q_BLOCK), 0)
            ql = lax.broadcasted_iota(jnp.int32, (LANES, Q_BLOCK), 1)
            dist = ql - CMP_STRIDE * (cl - CMP_FRONT) - (CMP_LEN - 1)
            ok = (dist >= 0) & (cl < CMP_NEAR)
            nbc_ref[:, cols] = jnp.where(ok, _bias_from_scalars(_bucket(dist), tab), NEG_INF)
            kl = lax.broadcasted_iota(jnp.int32, (SLC_NEAR, Q_BLOCK), 0)
            ql = lax.broadcasted_iota(jnp.int32, (SLC_NEAR, Q_BLOCK), 1)
            dist = ql - kl + Q_BLOCK
            nbs_ref[:, cols] = jnp.where(dist >= 0, _bias_from_scalars(_bucket(dist), tab), NEG_INF)
            kl = lax.broadcasted_iota(jnp.int32, (WIN_KEYS, Q_BLOCK), 0)
            ql = lax.broadcasted_iota(jnp.int32, (WIN_KEYS, Q_BLOCK), 1)
            dist = ql - kl + WINDOW
            ok = (dist >= 0) & (dist < WINDOW)
            nbw_ref[:, cols] = jnp.where(ok, _bias_from_scalars(_bucket(dist), tab), NEG_INF)

    lane = lax.broadcasted_iota(jnp.int32, (Q_BLOCK, LANES), 1)
    lmask = (lane >= hk * HEAD_DIM) & (lane < (hk + 1) * HEAD_DIM)
    qblk = q_ref[0].astype(F32)
    qp = jnp.concatenate(
        [jnp.where(lmask, qblk[:, g * LANES:(g + 1) * LANES], 0.0) for g in range(GROUP)], axis=0)
    qp = (qp * (HEAD_DIM ** -0.5)).astype(BF16)
    far_bias = jnp.concatenate(
        [jnp.full((1, Q_BLOCK), rb_ref[N_BUCKETS - 1, hk * GROUP + g], F32) for g in range(GROUP)], axis=1)
    init = (jnp.full((1, gq), NEG_INF, F32), jnp.zeros((1, gq), F32), jnp.zeros((LANES, gq), F32))
    p0 = pl.multiple_of(s0, LANES)

    n_cmp_pad = kcmp_ref.shape[1] - LANES
    c8 = qb * (Q_BLOCK // CMP_STRIDE)
    near0 = pl.multiple_of(c8, 8)
    kc_far = kcmp_ref[0, CMP_FRONT:CMP_FRONT + n_cmp_pad, :].astype(BF16)
    kc_near = kcmp_ref[0, pl.ds(near0, LANES), :].astype(BF16)
    crow = lax.broadcasted_iota(jnp.int32, (n_cmp_pad, gq), 0)
    s_far = jnp.where(crow < c8 - CMP_FRONT, _dot_t(kc_far, qp), NEG_INF)
    nrow = lax.broadcasted_iota(jnp.int32, (LANES, gq), 0)
    s_near = jnp.where(nrow >= CMP_FRONT - c8, _dot_t(kc_near, qp) + nbc_ref[...], NEG_INF)
    m = jnp.maximum(jnp.max(s_far, axis=0, keepdims=True) + far_bias, jnp.max(s_near, axis=0, keepdims=True))
    p_far = jnp.exp(s_far - (m - far_bias))
    p_near = jnp.exp(s_near - m)
    l = jnp.sum(p_far, axis=0, keepdims=True) + jnp.sum(p_near, axis=0, keepdims=True)
    inv = jnp.where(m > 0.5 * NEG_INF, 1.0 / l, 0.0)
    pc_far = p_far * inv
    pc_near = p_near * inv
    vct_near = vcmp_ref[0, pl.ds(near0, LANES), :].T.astype(BF16)
    o_cmp = _dot(vcmpt_ref[0], pc_far.astype(BF16)) + _dot(vct_near, pc_near.astype(BF16))

    wrow = lax.broadcasted_iota(jnp.int32, (WIN_KEYS, gq), 0)
    s = jnp.where(wrow >= WINDOW - s0, _dot_t(kw_ref[0, pl.ds(p0, WIN_KEYS), :], qp) + nbw_ref[...], NEG_INF)
    o_win = _col_softmax_finish(_col_softmax_update(init, s, vwt_ref[0, :, pl.ds(p0, WIN_KEYS)]))

    ps_far = sum(pc_far[:, g * Q_BLOCK:(g + 1) * Q_BLOCK] for g in range(GROUP))
    ps_near = sum(pc_near[:, g * Q_BLOCK:(g + 1) * Q_BLOCK] for g in range(GROUP))

    def split_dot(a, p):
        hi = p.astype(BF16)
        return _dot(a, hi) + _dot(a, (p - hi.astype(F32)).astype(BF16))

    sm_near_t = selmap_ref[pl.ds(near0, LANES), :].T.astype(BF16)
    imp_t = split_dot(selmapt_ref[...], ps_far) + split_dot(sm_near_t, ps_near)
    n_sel = imp_t.shape[0]
    blk = lax.broadcasted_iota(jnp.int32, (n_sel, Q_BLOCK), 0)
    cur = (s0 + lax.broadcasted_iota(jnp.int32, (n_sel, Q_BLOCK), 1)) // SEL_BLOCK
    forced = (blk == 0) | (blk == cur) | (blk == cur - 1)
    val = jnp.where(blk > cur, NEG_INF, imp_t + FORCE_BONUS * forced.astype(F32))
    sel_t = jnp.zeros((n_sel, Q_BLOCK), F32)
    for _ in range(N_SELECT):
        mx = jnp.max(val, axis=0, keepdims=True)
        first = jnp.min(jnp.where(val == mx, blk, n_sel), axis=0, keepdims=True)
        hit = blk == first
        sel_t = jnp.where(hit, 1.0, sel_t)
        val = jnp.where(hit, -3e38, val)
    sel = sel_t.T > 0.5

    jcol = lax.broadcasted_iota(jnp.int32, (Q_BLOCK, n_sel), 1)
    far_blocks = (s0 - Q_BLOCK) // SEL_BLOCK
    pen_far = jnp.where(sel & (jcol < far_blocks), 0.0, NEG_INF).astype(BF16)
    pen_near = jnp.where(sel, 0.0, NEG_INF).astype(BF16)
    q_far = jnp.concatenate([qp, _tile_rows(pen_far, GROUP)], axis=1)
    q_near = jnp.concatenate([qp, _tile_rows(pen_near, GROUP)], axis=1)

    n_tiles = (ks_ref.shape[1] - Q_BLOCK) // SLC_TILE

    def far_logits(kt):
        f0 = pl.multiple_of(Q_BLOCK + kt * SLC_TILE, LANES)
        k = jnp.concatenate([ks_ref[0, pl.ds(f0, SLC_TILE), :], et_ref[pl.ds(f0, SLC_TILE), :]], axis=1)
        return _dot_t(k, q_far)

    def far_step(kt, carry):
        s_cur, state = carry
        s_next = far_logits(jnp.minimum(kt + 1, n_tiles - 1))
        f0 = pl.multiple_of(Q_BLOCK + kt * SLC_TILE, LANES)
        return s_next, _col_softmax_update(state, s_cur, vst_ref[0, :, pl.ds(f0, SLC_TILE)])

    n_far = (qb + 2) // (SLC_TILE // Q_BLOCK)
    _, (m_far, l_far, acc_far) = lax.fori_loop(0, n_far, far_step, (far_logits(0), init))
    state = (m_far + far_bias, l_far, acc_far)
    k = jnp.concatenate([ks_ref[0, pl.ds(p0, SLC_NEAR), :], et_ref[pl.ds(p0, SLC_NEAR), :]], axis=1)
    krow = lax.broadcasted_iota(jnp.int32, (SLC_NEAR, gq), 0)
    s = jnp.where(krow >= Q_BLOCK - s0, _dot_t(k, q_near) + nbs_ref[...], NEG_INF)
    o_slc = _col_softmax_finish(_col_softmax_update(state, s, vst_ref[0, :, pl.ds(p0, SLC_NEAR)]))

    gst = jax.nn.sigmoid(gn_ref[0]).T

    def gate(g, i):
        c0, c1 = 3 * g + i, 3 * (GROUP + g) + i
        return jnp.where(hk == 0, gst[c0:c0 + 1, :], gst[c1:c1 + 1, :])

    for g in range(GROUP):
        cols = slice(g * Q_BLOCK, (g + 1) * Q_BLOCK)
        o = gate(g, 0) * o_cmp[:, cols] + gate(g, 1) * o_slc[:, cols] + gate(g, 2) * o_win[:, cols]
        out_ref[0, 0, :, g * LANES:(g + 1) * LANES] = jnp.where(lmask, o.T, 0.0).astype(out_ref.dtype)


def _nsa_prompt_t(q, gn, kcmp, vcmp, kvb, rel_bias):
    b, t, _ = q.shape
    assert t % SLC_TILE == 0
    nqb = t // Q_BLOCK
    n_chunk = t // CMP_STRIDE
    n_cmp = n_chunk - CMP_RATIO + 1
    n_blocks = t // SEL_BLOCK
    assert n_blocks <= LANES
    cpad = ((0, 0), (CMP_FRONT, LANES - CMP_FRONT), (0, 0))
    kcmp_p = jnp.pad(kcmp, cpad)
    vcmp_p = jnp.pad(vcmp, cpad)
    vcmp_t = vcmp.transpose(0, 2, 1).astype(BF16)
    sm = _sel_map(n_chunk, n_cmp, LANES, n_blocks)
    selmap = jnp.asarray(np.pad(sm, ((CMP_FRONT, LANES - CMP_FRONT), (0, 0))), F32)
    selmap_t = jnp.asarray(sm.T, BF16)
    front = lambda z, n: jnp.pad(z, ((0, 0), (n, 0), (0, 0)))
    ks_p = front(kvb[:, :, 2 * KV_WIDTH:3 * KV_WIDTH], Q_BLOCK)
    vs_t = front(kvb[:, :, 3 * KV_WIDTH:4 * KV_WIDTH], Q_BLOCK).transpose(0, 2, 1)
    kw_p = front(kvb[:, :, 4 * KV_WIDTH:5 * KV_WIDTH], WINDOW)
    vw_t = front(kvb[:, :, 5 * KV_WIDTH:6 * KV_WIDTH], WINDOW).transpose(0, 2, 1)
    pidx = np.arange(t + Q_BLOCK)
    e = (pidx[:, None] // SEL_BLOCK - Q_BLOCK // SEL_BLOCK == np.arange(LANES)[None, :])
    e = jnp.asarray(e.astype(np.float32), BF16)
    full = lambda arr: pl.BlockSpec((1,) + arr.shape[1:], lambda bi, h, i: (bi, 0, 0))
    const = lambda arr: pl.BlockSpec(arr.shape, lambda bi, h, i: (0, 0))
    return pl.pallas_call(
        _nsa_prompt_t_kernel,
        grid=(b, N_KV_HEADS, nqb),
        in_specs=[pl.BlockSpec(memory_space=pltpu.SMEM),
                  pl.BlockSpec((1, Q_BLOCK, NSA_WIDTH), lambda bi, h, i: (bi, i, 0)),
                  pl.BlockSpec((1, Q_BLOCK, LANES), lambda bi, h, i: (bi, i, 0)),
                  full(kcmp_p), full(vcmp_p), full(vcmp_t), const(selmap), const(selmap_t),
                  full(ks_p), full(vs_t), full(kw_p), full(vw_t), const(e)],
        out_specs=pl.BlockSpec((1, 1, Q_BLOCK, NSA_WIDTH), lambda bi, h, i: (h, bi, i, 0)),
        out_shape=jax.ShapeDtypeStruct((N_KV_HEADS, b, t, NSA_WIDTH), BF16),
        scratch_shapes=[pltpu.VMEM((LANES, GROUP * Q_BLOCK), F32),
                        pltpu.VMEM((SLC_NEAR, GROUP * Q_BLOCK), F32),
                        pltpu.VMEM((WIN_KEYS, GROUP * Q_BLOCK), F32)],
        compiler_params=_cparams(("arbitrary", "arbitrary", "arbitrary")),
        name="nsa_prompt",
    )(rel_bias, q, gn, kcmp_p, vcmp_p, vcmp_t, selmap, selmap_t, ks_p, vs_t, kw_p, vw_t, e)


def _nsa_prompt_pair_kernel(rb_ref, q_ref, gn_ref, kcmp_ref, vcmp_ref, vcmpt_ref, selmap_ref, selmapt_ref,
                            ks_ref, vst_ref, kw_ref, vwt_ref, et_ref, out_ref, nbc_ref, nbs_ref, nbw_ref, s_ref):
    qb = pl.program_id(1)
    s0 = qb * Q_BLOCK
    gq = GROUP * Q_BLOCK
    heads = range(N_KV_HEADS)

    @pl.when(qb == 0)
    def _():
        for hk in heads:
            for g in range(GROUP):
                tab = lambda k, h=hk * GROUP + g: rb_ref[k, h]
                cols = slice(g * Q_BLOCK, (g + 1) * Q_BLOCK)
                cl = lax.broadcasted_iota(jnp.int32, (LANES, Q_BLOCK), 0)
                ql = lax.broadcasted_iota(jnp.int32, (LANES, Q_BLOCK), 1)
                dist = ql - CMP_STRIDE * (cl - CMP_FRONT) - (CMP_LEN - 1)
                ok = (dist >= 0) & (cl < CMP_NEAR)
                nbc_ref[hk, :, cols] = jnp.where(ok, _bias_from_scalars(_bucket(dist), tab), NEG_INF)
                kl = lax.broadcasted_iota(jnp.int32, (SLC_NEAR, Q_BLOCK), 0)
                ql = lax.broadcasted_iota(jnp.int32, (SLC_NEAR, Q_BLOCK), 1)
                dist = ql - kl + Q_BLOCK
                nbs_ref[hk, :, cols] = jnp.where(dist >= 0, _bias_from_scalars(_bucket(dist), tab), NEG_INF)
                kl = lax.broadcasted_iota(jnp.int32, (WIN_KEYS, Q_BLOCK), 0)
                ql = lax.broadcasted_iota(jnp.int32, (WIN_KEYS, Q_BLOCK), 1)
                dist = ql - kl + WINDOW
                ok = (dist >= 0) & (dist < WINDOW)
                nbw_ref[hk, :, cols] = jnp.where(ok, _bias_from_scalars(_bucket(dist), tab), NEG_INF)

    lane = lax.broadcasted_iota(jnp.int32, (Q_BLOCK, LANES), 1)
    qblk = q_ref[0].astype(F32)
    init = (jnp.full((1, gq), NEG_INF, F32), jnp.zeros((1, gq), F32), jnp.zeros((LANES, gq), F32))
    p0 = pl.multiple_of(s0, LANES)
    n_cmp_pad = kcmp_ref.shape[1] - LANES
    c8 = qb * (Q_BLOCK // CMP_STRIDE)
    near0 = pl.multiple_of(c8, 8)
    kc_far = kcmp_ref[0, CMP_FRONT:CMP_FRONT + n_cmp_pad, :].astype(BF16)
    kc_near = kcmp_ref[0, pl.ds(near0, LANES), :].astype(BF16)
    vct_near = vcmp_ref[0, pl.ds(near0, LANES), :].T.astype(BF16)
    sm_near_t = selmap_ref[pl.ds(near0, LANES), :].T.astype(BF16)
    kw = kw_ref[0, pl.ds(p0, WIN_KEYS), :]
    vwt = vwt_ref[0, :, pl.ds(p0, WIN_KEYS)]
    crow = lax.broadcasted_iota(jnp.int32, (n_cmp_pad, gq), 0)
    nrow = lax.broadcasted_iota(jnp.int32, (LANES, gq), 0)
    wrow = lax.broadcasted_iota(jnp.int32, (WIN_KEYS, gq), 0)
    krow = lax.broadcasted_iota(jnp.int32, (SLC_NEAR, gq), 0)
    n_sel = selmapt_ref.shape[0]
    blk = lax.broadcasted_iota(jnp.int32, (n_sel, Q_BLOCK), 0)
    cur = (s0 + lax.broadcasted_iota(jnp.int32, (n_sel, Q_BLOCK), 1)) // SEL_BLOCK
    forced = (blk == 0) | (blk == cur) | (blk == cur - 1)
    jcol = lax.broadcasted_iota(jnp.int32, (Q_BLOCK, n_sel), 1)
    far_blocks = (s0 - Q_BLOCK) // SEL_BLOCK

    def split_dot(a, p):
        hi = p.astype(BF16)
        return _dot(a, hi) + _dot(a, (p - hi.astype(F32)).astype(BF16))

    def before_far(hk):
        lmask = (lane // HEAD_DIM) == hk
        qp = jnp.concatenate(
            [jnp.where(lmask, qblk[:, g * LANES:(g + 1) * LANES], 0.0) for g in range(GROUP)], axis=0)
        qp = (qp * (HEAD_DIM ** -0.5)).astype(BF16)
        far_bias = jnp.concatenate(
            [jnp.full((1, Q_BLOCK), rb_ref[N_BUCKETS - 1, hk * GROUP + g], F32) for g in range(GROUP)], axis=1)

        s_far = jnp.where(crow < c8 - CMP_FRONT, _dot_t(kc_far, qp), NEG_INF)
        s_near = jnp.where(nrow >= CMP_FRONT - c8, _dot_t(kc_near, qp) + nbc_ref[hk], NEG_INF)
        m = jnp.maximum(jnp.max(s_far, axis=0, keepdims=True) + far_bias, jnp.max(s_near, axis=0, keepdims=True))
        p_far = jnp.exp(s_far - (m - far_bias))
        p_near = jnp.exp(s_near - m)
        l = jnp.sum(p_far, axis=0, keepdims=True) + jnp.sum(p_near, axis=0, keepdims=True)
        inv = jnp.where(m > 0.5 * NEG_INF, 1.0 / l, 0.0)
        pc_far = p_far * inv
        pc_near = p_near * inv
        o_cmp = _dot(vcmpt_ref[0], pc_far.astype(BF16)) + _dot(vct_near, pc_near.astype(BF16))

        s = jnp.where(wrow >= WINDOW - s0, _dot_t(kw, qp) + nbw_ref[hk], NEG_INF)
        o_win = _col_softmax_finish(_col_softmax_update(init, s, vwt))

        ps_far = sum(pc_far[:, g * Q_BLOCK:(g + 1) * Q_BLOCK] for g in range(GROUP))
        ps_near = sum(pc_near[:, g * Q_BLOCK:(g + 1) * Q_BLOCK] for g in range(GROUP))
        imp_t = split_dot(selmapt_ref[...], ps_far) + split_dot(sm_near_t, ps_near)
        val = jnp.where(blk > cur, NEG_INF, imp_t + FORCE_BONUS * forced.astype(F32))
        sel_t = jnp.zeros((n_sel, Q_BLOCK), F32)
        for _ in range(N_SELECT):
            mx = jnp.max(val, axis=0, keepdims=True)
            first = jnp.min(jnp.where(val == mx, blk, n_sel), axis=0, keepdims=True)
            hit = blk == first
            sel_t = jnp.where(hit, 1.0, sel_t)
            val = jnp.where(hit, -3e38, val)
        sel = sel_t.T > 0.5

        pen_far = jnp.where(sel & (jcol < far_blocks), 0.0, NEG_INF).astype(BF16)
        pen_near = jnp.where(sel, 0.0, NEG_INF).astype(BF16)
        q_far = jnp.concatenate([qp, _tile_rows(pen_far, GROUP)], axis=1)
        q_near = jnp.concatenate([qp, _tile_rows(pen_near, GROUP)], axis=1)
        return dict(far_bias=far_bias, o_cmp=o_cmp, o_win=o_win, q_far=q_far, q_near=q_near)

    pre = [before_far(hk) for hk in heads]
    n_tiles = (ks_ref.shape[1] - Q_BLOCK) // SLC_TILE

    def far_logits(kt):
        f0 = pl.multiple_of(Q_BLOCK + kt * SLC_TILE, LANES)
        k = jnp.concatenate([ks_ref[0, pl.ds(f0, SLC_TILE), :], et_ref[pl.ds(f0, SLC_TILE), :]], axis=1)
        return [_dot_t(k, pre[hk]["q_far"]) for hk in heads]

    def far_pair(i, states):
        for slot in range(2):
            kt = 2 * i + slot
            for hk, s_next in enumerate(far_logits(jnp.minimum(kt + 1, n_tiles - 1))):
                s_ref[1 - slot, hk] = s_next
            f0 = pl.multiple_of(Q_BLOCK + jnp.minimum(kt, n_tiles - 1) * SLC_TILE, LANES)
            vt = vst_ref[0, :, pl.ds(f0, SLC_TILE)]
            states = [_col_softmax_update(states[hk], s_ref[slot, hk], vt) for hk in heads]
        return states

    n_far = (qb + 2) // (SLC_TILE // Q_BLOCK)
    for hk, s_first in enumerate(far_logits(0)):
        s_ref[0, hk] = s_first
    far_states = lax.fori_loop(0, (n_far + 1) // 2, far_pair, [init] * N_KV_HEADS)

    k_near = jnp.concatenate([ks_ref[0, pl.ds(p0, SLC_NEAR), :], et_ref[pl.ds(p0, SLC_NEAR), :]], axis=1)
    vt_near = vst_ref[0, :, pl.ds(p0, SLC_NEAR)]
    gst = jax.nn.sigmoid(gn_ref[0]).T
    outs = []
    for hk in heads:
        m_far, l_far, acc_far = far_states[hk]
        state = (m_far + pre[hk]["far_bias"], l_far, acc_far)
        s = jnp.where(krow >= Q_BLOCK - s0, _dot_t(k_near, pre[hk]["q_near"]) + nbs_ref[hk], NEG_INF)
        o_slc = _col_softmax_finish(_col_softmax_update(state, s, vt_near))
        per_g = []
        for g in range(GROUP):
            cols = slice(g * Q_BLOCK, (g + 1) * Q_BLOCK)
            c = 3 * (hk * GROUP + g)
            o = (gst[c:c + 1, :] * pre[hk]["o_cmp"][:, cols] + gst[c + 1:c + 2, :] * o_slc[:, cols]
                 + gst[c + 2:c + 3, :] * pre[hk]["o_win"][:, cols])
            per_g.append(o.T)
        outs.append(per_g)
    lane_h0 = lane < HEAD_DIM
    for g in range(GROUP):
        out_ref[0, :, g * LANES:(g + 1) * LANES] = jnp.where(lane_h0, outs[0][g], outs[1][g]).astype(out_ref.dtype)


def _nsa_prompt_pair(q, gn, kcmp, vcmp, kvb, rel_bias):
    b, t, _ = q.shape
    assert t % SLC_TILE == 0
    nqb = t // Q_BLOCK
    n_chunk = t // CMP_STRIDE
    n_cmp = n_chunk - CMP_RATIO + 1
    n_blocks = t // SEL_BLOCK
    assert n_blocks <= LANES
    cpad = ((0, 0), (CMP_FRONT, LANES - CMP_FRONT), (0, 0))
    kcmp_p = jnp.pad(kcmp, cpad)
    vcmp_p = jnp.pad(vcmp, cpad)
    vcmp_t = vcmp.transpose(0, 2, 1).astype(BF16)
    sm = _sel_map(n_chunk, n_cmp, LANES, n_blocks)
    selmap = jnp.asarray(np.pad(sm, ((CMP_FRONT, LANES - CMP_FRONT), (0, 0))), F32)
    selmap_t = jnp.asarray(sm.T, BF16)
    front = lambda z, n: jnp.pad(z, ((0, 0), (n, 0), (0, 0)))
    ks_p = front(kvb[:, :, 2 * KV_WIDTH:3 * KV_WIDTH], Q_BLOCK)
    vs_t = front(kvb[:, :, 3 * KV_WIDTH:4 * KV_WIDTH], Q_BLOCK).transpose(0, 2, 1)
    kw_p = front(kvb[:, :, 4 * KV_WIDTH:5 * KV_WIDTH], WINDOW)
    vw_t = front(kvb[:, :, 5 * KV_WIDTH:6 * KV_WIDTH], WINDOW).transpose(0, 2, 1)
    pidx = np.arange(t + Q_BLOCK)
    e = (pidx[:, None] // SEL_BLOCK - Q_BLOCK // SEL_BLOCK == np.arange(LANES)[None, :])
    e = jnp.asarray(e.astype(np.float32), BF16)
    full = lambda arr: pl.BlockSpec((1,) + arr.shape[1:], lambda bi, i: (bi, 0, 0))
    const = lambda arr: pl.BlockSpec(arr.shape, lambda bi, i: (0, 0))
    return pl.pallas_call(
        _nsa_prompt_pair_kernel,
        grid=(b, nqb),
        in_specs=[pl.BlockSpec(memory_space=pltpu.SMEM),
                  pl.BlockSpec((1, Q_BLOCK, NSA_WIDTH), lambda bi, i: (bi, i, 0)),
                  pl.BlockSpec((1, Q_BLOCK, LANES), lambda bi, i: (bi, i, 0)),
                  full(kcmp_p), full(vcmp_p), full(vcmp_t), const(selmap), const(selmap_t),
                  full(ks_p), full(vs_t), full(kw_p), full(vw_t), const(e)],
        out_specs=pl.BlockSpec((1, Q_BLOCK, NSA_WIDTH), lambda bi, i: (bi, i, 0)),
        out_shape=jax.ShapeDtypeStruct((b, t, NSA_WIDTH), BF16),
        scratch_shapes=[pltpu.VMEM((N_KV_HEADS, LANES, GROUP * Q_BLOCK), F32),
                        pltpu.VMEM((N_KV_HEADS, SLC_NEAR, GROUP * Q_BLOCK), F32),
                        pltpu.VMEM((N_KV_HEADS, WIN_KEYS, GROUP * Q_BLOCK), F32),
                        pltpu.VMEM((2, N_KV_HEADS, SLC_TILE, GROUP * Q_BLOCK), F32)],
        compiler_params=_cparams(("arbitrary", "arbitrary")),
        name="nsa_prompt",
    )(rel_bias, q, gn, kcmp_p, vcmp_p, vcmp_t, selmap, selmap_t, ks_p, vs_t, kw_p, vw_t, e)


ROWS_PER_TOKEN = 8


def _nsa_sample_cmp_kernel(q_ref, kcmp_ref, vcmp_ref, selmap_ref, rowtab_ref, ocmp_ref, idx_ref, *,
                           past_len, n_tok, n_cmp, n_blocks):
    rows = n_tok * ROWS_PER_TOKEN
    n_cpad = kcmp_ref.shape[1]
    ridx = lax.broadcasted_iota(jnp.int32, (rows, n_cpad), 0)
    ccol = lax.broadcasted_iota(jnp.int32, (rows, n_cpad), 1)
    dist = past_len + ridx // ROWS_PER_TOKEN - (CMP_STRIDE * ccol + CMP_LEN - 1)
    bucket = _bucket(dist)
    valid = (dist >= 0) & (ccol < n_cmp) & (ridx % ROWS_PER_TOKEN < GROUP)
    n_bpad = selmap_ref.shape[1]
    blk = lax.broadcasted_iota(jnp.int32, (ROWS_PER_TOKEN, n_bpad), 1)
    cur = (past_len + lax.broadcasted_iota(jnp.int32, (ROWS_PER_TOKEN, n_bpad), 0)) // SEL_BLOCK
    forced = (blk == 0) | (blk == cur) | (blk == cur - 1)
    olane = lax.broadcasted_iota(jnp.int32, (ROWS_PER_TOKEN, LANES), 1)
    biases = [_bias_from_rows(bucket, rowtab_ref[hk]) for hk in range(N_KV_HEADS)]
    for bi, hk in [(bi, hk) for bi in range(kcmp_ref.shape[0]) for hk in range(N_KV_HEADS)]:
        kc = kcmp_ref[bi].astype(BF16)
        vc = vcmp_ref[bi].astype(BF16)
        qp = (q_ref[bi, hk].astype(F32) * (HEAD_DIM ** -0.5)).astype(BF16)
        s = _dot_t(qp, kc) + biases[hk]
        s = jnp.where(valid, s, NEG_INF)
        m = jnp.max(s, axis=-1, keepdims=True)
        p = jnp.where(valid, jnp.exp(s - m), 0.0)
        l = jnp.sum(p, axis=-1, keepdims=True)
        pc = p * jnp.where(l > 0.0, 1.0 / jnp.where(l > 0.0, l, 1.0), 0.0)
        ocmp_ref[bi, hk] = _dot(pc.astype(BF16), vc)[:, hk * HEAD_DIM:(hk + 1) * HEAD_DIM]
        ps = jnp.concatenate(
            [jnp.sum(pc[i * ROWS_PER_TOKEN:(i + 1) * ROWS_PER_TOKEN], axis=0, keepdims=True) for i in range(n_tok)]
            + [jnp.zeros((ROWS_PER_TOKEN - n_tok, n_cpad), F32)], axis=0)
        imp = _split_dot(ps, selmap_ref[...])
        val = jnp.where((blk > cur) | (blk >= n_blocks), NEG_INF, imp + FORCE_BONUS * forced.astype(F32))
        picks = jnp.zeros((ROWS_PER_TOKEN, LANES), jnp.int32)
        for n in range(N_SELECT):
            mx = jnp.max(val, axis=-1, keepdims=True)
            first = jnp.min(jnp.where(val == mx, blk, n_bpad), axis=-1, keepdims=True)
            picks = jnp.where(olane == n, first, picks)
            val = jnp.where(blk == first, -3e38, val)
        idx_ref[bi, hk] = picks


def _nsa_sample_cmp(qs, kcmp, vcmp, rowtab, past_len, n_tok, n_cmp, n_blocks):
    nb = qs.shape[0]
    bps = next(d for d in (4, 2, 1) if nb % d == 0)
    n_cpad = kcmp.shape[1]
    n_bpad = -(-n_blocks // LANES) * LANES
    selmap = jnp.asarray(_sel_map(n_cpad, n_cmp, n_bpad, n_blocks), BF16)
    rows = n_tok * ROWS_PER_TOKEN
    return pl.pallas_call(
        functools.partial(_nsa_sample_cmp_kernel, past_len=past_len, n_tok=n_tok, n_cmp=n_cmp, n_blocks=n_blocks),
        grid=(nb // bps,),
        in_specs=[pl.BlockSpec((bps, N_KV_HEADS, rows, LANES), lambda b: (b, 0, 0, 0)),
                  pl.BlockSpec((bps, n_cpad, LANES), lambda b: (b, 0, 0)),
                  pl.BlockSpec((bps, n_cpad, LANES), lambda b: (b, 0, 0)),
                  pl.BlockSpec(selmap.shape, lambda b: (0, 0)),
                  pl.BlockSpec(rowtab.shape, lambda b: (0, 0, 0))],
        out_specs=[pl.BlockSpec((bps, N_KV_HEADS, rows, HEAD_DIM), lambda b: (b, 0, 0, 0)),
                   pl.BlockSpec((bps, N_KV_HEADS, ROWS_PER_TOKEN, LANES), lambda b: (b, 0, 0, 0))],
        out_shape=[jax.ShapeDtypeStruct((nb, N_KV_HEADS, rows, HEAD_DIM), F32),
                   jax.ShapeDtypeStruct((nb, N_KV_HEADS, ROWS_PER_TOKEN, LANES), jnp.int32)],
        compiler_params=_cparams(("arbitrary",)),
        name="nsa_sample_cmp",
    )(qs, kcmp, vcmp, selmap, rowtab)


def _nsa_sample_slc_kernel(idx_ref, pt_ref, ck_ref, cv_ref, q_ref, ocmp_ref, gn_ref, rowtab_ref, newk_ref, newv_ref,
                           swk_ref, swv_ref, nwk_ref, nwv_ref, out_ref, kbuf_ref, vbuf_ref, sem_ref, *,
                           past_len, n_tok, n_cache_blocks, npages):
    b = pl.program_id(0)
    hk = pl.program_id(1)
    tok = pl.program_id(2)
    t = past_len + tok
    step = (b * N_KV_HEADS + hk) * n_tok + tok
    last = pl.num_programs(0) * N_KV_HEADS * n_tok - 1
    slot = step % 2
    base = step * N_SELECT
    pages_per_block = PAGE // SEL_BLOCK

    def block_copies(st, n, sl):
        j = jnp.minimum(idx_ref[st * N_SELECT + n], n_cache_blocks - 1)
        page = pt_ref[(st // (N_KV_HEADS * n_tok)) * npages + j // pages_per_block]
        head = (st // n_tok) % N_KV_HEADS
        return (pltpu.make_async_copy(ck_ref.at[page, head], kbuf_ref.at[sl, n], sem_ref.at[sl]),
                pltpu.make_async_copy(cv_ref.at[page, head], vbuf_ref.at[sl, n], sem_ref.at[sl]))

    @pl.when(step == 0)
    def _():
        for n in range(N_SELECT):
            for c in block_copies(0, n, 0):
                c.start()

    nxt = jnp.minimum(step + 1, last)
    for n in range(N_SELECT):
        for c in block_copies(nxt, n, 1 - slot):
            c.start()
    for n in range(N_SELECT):
        for c in block_copies(step, n, slot):
            c.wait()
    qp = (q_ref[0, 0, 0].astype(F32) * (HEAD_DIM ** -0.5)).astype(BF16)
    rowtab = rowtab_ref[0]
    rvalid = lax.broadcasted_iota(jnp.int32, (ROWS_PER_TOKEN, 1), 0) < GROUP
    init = (jnp.full((ROWS_PER_TOKEN, 1), NEG_INF, F32), jnp.zeros((ROWS_PER_TOKEN, 1), F32),
            jnp.zeros((ROWS_PER_TOKEN, HEAD_DIM), F32))
    halves = PAGE // SEL_BLOCK

    n_keys = N_SELECT * PAGE
    lane = lax.broadcasted_iota(jnp.int32, (ROWS_PER_TOKEN, n_keys), 1)
    row = lane % PAGE
    pos = row
    in_block = jnp.zeros((ROWS_PER_TOKEN, n_keys), jnp.bool_)
    kts, vts = [], []
    for n in range(N_SELECT):
        blk = idx_ref[base + n]
        is_new = blk >= n_cache_blocks
        kts.append(jnp.where(is_new, newk_ref[0, 0], kbuf_ref[slot, n]).astype(BF16))
        vts.append(jnp.where(is_new, newv_ref[0, 0], vbuf_ref[slot, n]).astype(BF16))
        here = lane // PAGE == n
        pos = jnp.where(here, row + (blk // halves) * PAGE, pos)
        in_block = in_block | (here & (row // SEL_BLOCK == blk % halves))
    dist = t - pos
    s = _dot(qp, jnp.concatenate(kts, axis=1)) + _bias_from_rows(_bucket(dist), rowtab)
    mask = (dist >= 0) & in_block & rvalid
    o_slc = _softmax_finish(_softmax_update(init, jnp.where(mask, s, NEG_INF), jnp.concatenate(vts, axis=1),
                                            mask=mask, v_transposed=True))

    n_win = swk_ref.shape[3]
    kw = jnp.concatenate([swk_ref[0, 0], nwk_ref[0, 0]], axis=1).astype(BF16)
    vw = jnp.concatenate([swv_ref[0, 0], nwv_ref[0, 0]], axis=1).astype(BF16)
    wl = n_win + nwk_ref.shape[3]
    wpos = past_len - n_win + lax.broadcasted_iota(jnp.int32, (ROWS_PER_TOKEN, wl), 1)
    wdist = t - wpos
    s = _dot(qp, kw) + _bias_from_rows(_bucket(wdist), rowtab)
    mask = (wdist >= 0) & (wdist < WINDOW) & (wpos >= 0) & rvalid
    o_win = _softmax_finish(_softmax_update(init, jnp.where(mask, s, NEG_INF), vw, mask=mask, v_transposed=True))

    gs = jax.nn.sigmoid(gn_ref[0, 0, 0])
    out_ref[0, 0, 0] = gs[:, 0:1] * ocmp_ref[0, 0, 0] + gs[:, 1:2] * o_slc + gs[:, 2:3] * o_win

    @pl.when(step == last)
    def _():
        for n in range(N_SELECT):
            for c in block_copies(nxt, n, 1 - slot):
                c.wait()


def _nsa_sample_slc(idx, page_table, cache_k, cache_v, q3, ocmp, gn3, rowtab, newk, newv, swk, swv, nwk, nwv,
                    past_len, n_tok):
    nb, npages = page_table.shape
    halves = PAGE // SEL_BLOCK
    n_cache_blocks = npages * halves
    ck = cache_k.transpose(0, 2, 3, 1)
    cv = cache_v.transpose(0, 2, 3, 1)

    tile = lambda w: pl.BlockSpec((1, 1, 1, ROWS_PER_TOKEN, w), lambda b, h, s, i, p: (b, h, s, 0, 0))
    per_bh = lambda arr: pl.BlockSpec((1, 1) + arr.shape[2:], lambda b, h, s, i, p: (b, h, 0, 0))
    grid_spec = pltpu.PrefetchScalarGridSpec(
        num_scalar_prefetch=2,
        grid=(nb, N_KV_HEADS, n_tok),
        in_specs=[pl.BlockSpec(memory_space=pl.ANY), pl.BlockSpec(memory_space=pl.ANY),
                  tile(HEAD_DIM), tile(HEAD_DIM), tile(LANES),
                  pl.BlockSpec((1, ROWS_PER_TOKEN, N_BUCKETS), lambda b, h, s, i, p: (h, 0, 0)),
                  per_bh(newk), per_bh(newv), per_bh(swk), per_bh(swv), per_bh(nwk), per_bh(nwv)],
        out_specs=tile(HEAD_DIM),
        scratch_shapes=[pltpu.VMEM((2, N_SELECT, HEAD_DIM, PAGE), F32),
                        pltpu.VMEM((2, N_SELECT, HEAD_DIM, PAGE), F32),
                        pltpu.SemaphoreType.DMA((2,))],
    )
    return pl.pallas_call(
        functools.partial(_nsa_sample_slc_kernel, past_len=past_len, n_tok=n_tok, n_cache_blocks=n_cache_blocks,
                          npages=npages),
        grid_spec=grid_spec,
        out_shape=jax.ShapeDtypeStruct((nb, N_KV_HEADS, n_tok, ROWS_PER_TOKEN, HEAD_DIM), F32),
        compiler_params=_cparams(("arbitrary", "arbitrary", "arbitrary")),
        name="nsa_sample_slc",
    )(idx.reshape(-1), page_table.reshape(-1), ck, cv, q3, ocmp, gn3, rowtab, newk, newv, swk, swv, nwk, nwv)


def _merge_tail(x, a, c, gates, wn_ref, wc_ref, wo_ref):
    ga = gates[:, :D_MODEL]
    gb = gates[:, D_MODEL:]
    m = (jax.nn.sigmoid(ga) * _dot(a, wn_ref[...])
         + jax.nn.sigmoid(gb) * _dot(c.astype(BF16), wc_ref[...]))
    return x + _dot(m.astype(BF16), wo_ref[...])


def _merge_prompt_kernel(x_ref, a_ref, conv_ref, halo_ref, gate_ref, cw_ref, cb_ref, wn_ref, wc_ref, wo_ref,
                         out_ref, cst_ref, *, tiles_per_batch):
    i = pl.program_id(0)
    first = i % tiles_per_batch == 0
    conv = conv_ref[...]
    bg = conv[:, :CONV_DIM]
    u = conv[:, CONV_DIM:2 * CONV_DIM] * conv[:, 2 * CONV_DIM:]
    halo = halo_ref[...]
    uh = jnp.where(first, 0.0, halo[:, CONV_DIM:2 * CONV_DIM] * halo[:, 2 * CONV_DIM:])
    tm = u.shape[0]
    row = lax.broadcasted_iota(jnp.int32, (tm, CONV_DIM), 0)
    u1 = jnp.where(row == 0, uh[7:8], pltpu.roll(u, 1, axis=0))
    u2 = pltpu.roll(u, 2, axis=0)
    u2 = jnp.where(row == 0, uh[6:7], jnp.where(row == 1, uh[7:8], u2))
    cw = cw_ref[...]
    y = ((cb_ref[...] + cw[0:1] * u2) + cw[1:2] * u1) + cw[2:3] * u
    c = bg * y
    out_ref[...] = _merge_tail(x_ref[...], a_ref[...], c, gate_ref[...], wn_ref, wc_ref, wo_ref)
    cst_ref[0] = u[tm - 8:, :]


def _merge_prompt(x2d, a2, conv3, gates, conv_w, conv_b, wn, wc, wo, t):
    n = x2d.shape[0]
    tm = min(512, t)
    assert t % tm == 0
    tpb = t // tm
    row = lambda w: pl.BlockSpec((tm, w), lambda i: (i, 0))
    const = lambda arr: pl.BlockSpec(arr.shape, lambda i: (0,) * arr.ndim)
    return pl.pallas_call(
        functools.partial(_merge_prompt_kernel, tiles_per_batch=tpb),
        grid=(n // tm,),
        in_specs=[row(D_MODEL),
                  row(NSA_WIDTH),
                  row(3 * CONV_DIM),
                  pl.BlockSpec((8, 3 * CONV_DIM), lambda i: (jnp.maximum(i * (tm // 8) - 1, 0), 0)),
                  row(2 * D_MODEL),
                  const(conv_w), const(conv_b), const(wn), const(wc), const(wo)],
        out_specs=[row(D_MODEL), pl.BlockSpec((1, 8, CONV_DIM), lambda i: (i // tpb, 0, 0))],
        out_shape=[jax.ShapeDtypeStruct((n, D_MODEL), F32),
                   jax.ShapeDtypeStruct((n // t, 8, CONV_DIM), F32)],
        compiler_params=_cparams(("arbitrary",)),
        name="merge_prompt",
    )(x2d, a2, conv3, conv3, gates, conv_w, conv_b, wn, wc, wo)


def _merge_sample_kernel(x_ref, a_ref, conv_ref, past_ref, gate_ref, cw_ref, cb_ref, wn_ref, wc_ref, wo_ref,
                         out_ref, cst_ref, *, n_tok, nb):
    conv = conv_ref[...]
    bg = conv[:, :CONV_DIM]
    u = conv[:, CONV_DIM:2 * CONV_DIM] * conv[:, 2 * CONV_DIM:]
    up = jnp.concatenate([past_ref[...], u], axis=0)
    cw = cw_ref[...]
    y = cb_ref[...]
    for k in range(CONV_WIDTH):
        y = y + cw[k:k + 1] * up[k * nb:(k + n_tok) * nb]
    c = bg * y
    out_ref[...] = _merge_tail(x_ref[...], a_ref[...].astype(BF16), c, gate_ref[...], wn_ref, wc_ref, wo_ref)
    cst_ref[...] = up[n_tok * nb:]


def _merge_sample(x_tm, a_tm, conv_tm, past_tm, gates_tm, conv_w, conv_b, wn, wc, wo, n_tok, nb):
    n = x_tm.shape[0]
    full = lambda arr: pl.BlockSpec(arr.shape, lambda i: (0,) * arr.ndim)
    args = (x_tm, a_tm, conv_tm, past_tm, gates_tm, conv_w, conv_b, wn, wc, wo)
    return pl.pallas_call(
        functools.partial(_merge_sample_kernel, n_tok=n_tok, nb=nb),
        grid=(1,),
        in_specs=[full(a) for a in args],
        out_specs=[pl.BlockSpec((n, D_MODEL), lambda i: (0, 0)),
                   pl.BlockSpec(((CONV_WIDTH - 1) * nb, CONV_DIM), lambda i: (0, 0))],
        out_shape=[jax.ShapeDtypeStruct((n, D_MODEL), F32),
                   jax.ShapeDtypeStruct(((CONV_WIDTH - 1) * nb, CONV_DIM), F32)],
        compiler_params=_cparams(("arbitrary",)),
        name="merge_sample",
    )(*args)


def _rms(x, g):
    return (x * lax.rsqrt(jnp.mean(x * x, axis=-1, keepdims=True) + RMS_EPS)) * g


def _mlp_kernel(x_ref, g_ref, wu_ref, wd_ref, gf_ref, out_ref, h_ref, acc_ref):
    j = pl.program_id(1)

    @pl.when(j == 0)
    def _():
        h_ref[...] = _rms(x_ref[...], g_ref[...]).astype(BF16)

    up = jnp.square(jnp.maximum(_dot(h_ref[...], wu_ref[...]), 0.0)).astype(BF16)
    part = _dot(up, wd_ref[...])

    @pl.when(j == 0)
    def _():
        acc_ref[...] = part

    @pl.when(j > 0)
    def _():
        acc_ref[...] += part

    @pl.when(j == pl.num_programs(1) - 1)
    def _():
        out_ref[...] = _rms(x_ref[...] + acc_ref[...], gf_ref[...])


def _mlp(x2d, g_mlp, w_up, w_down, g_final):
    n = x2d.shape[0]
    tm = min(512, n)
    tf = 1024
    assert n % tm == 0 and D_FF % tf == 0
    return pl.pallas_call(
        _mlp_kernel,
        grid=(n // tm, D_FF // tf),
        in_specs=[pl.BlockSpec((tm, D_MODEL), lambda i, j: (i, 0)),
                  pl.BlockSpec((1, D_MODEL), lambda i, j: (0, 0)),
                  pl.BlockSpec((D_MODEL, tf), lambda i, j: (0, j)),
                  pl.BlockSpec((tf, D_MODEL), lambda i, j: (j, 0)),
                  pl.BlockSpec((1, D_MODEL), lambda i, j: (0, 0))],
        out_specs=pl.BlockSpec((tm, D_MODEL), lambda i, j: (i, 0)),
        out_shape=jax.ShapeDtypeStruct((n, D_MODEL), F32),
        scratch_shapes=[pltpu.VMEM((tm, D_MODEL), BF16), pltpu.VMEM((tm, D_MODEL), F32)],
        compiler_params=_cparams(("arbitrary", "arbitrary")),
        name="mlp",
    )(x2d, g_mlp, w_up, w_down, g_final)


def _kv_rows(z, lead):
    return z.reshape(lead + (N_KV_HEADS, HEAD_DIM))


def _layer(xp, xs, caches, states, page_table, w, rel_bias):
    (cache_cmp_k, cache_cmp_v, cache_slc_k, cache_slc_v) = caches
    (state_win_k, state_win_v, state_conv) = states
    nbp, t, _ = xp.shape
    nbs, n_tok, _ = xs.shape
    npages = page_table.shape[1]
    past_len = npages * PAGE
    assert t % PAGE == 0 and n_tok <= GROUP and state_win_k.shape[1] >= n_tok

    w_all = _prep_w_in(w["w_in"])
    g_attn = w["g_attn"].reshape(1, D_MODEL)
    wn = w["w_nsa_out"].reshape(N_KV_HEADS, GROUP, HEAD_DIM, D_MODEL).transpose(1, 0, 2, 3)
    wn = wn.reshape(NSA_WIDTH, D_MODEL).astype(BF16)
    wc = w["w_conv_out"].astype(BF16)
    wo = w["w_o"].astype(BF16)
    wu = w["w_up"].astype(BF16)
    wd = w["w_down"].astype(BF16)
    g_mlp = w["g_mlp"].reshape(1, D_MODEL)
    g_final = w["g_final"].reshape(1, D_MODEL)
    conv_w = w["conv_w"]
    conv_b = w["conv_b"].reshape(1, CONV_DIM)
    cmp_k = (w["cmp_pe_k"], w["cmp_w1_k"], w["cmp_w2_k"])
    cmp_v = (w["cmp_pe_v"], w["cmp_w1_v"], w["cmp_w2_v"])

    n_p = nbp * t
    (q, kc, vc, ks, vs, kw, vw, kvb, gn, conv3, gates) = _proj(xp.reshape(n_p, D_MODEL), g_attn, w_all)
    ident = jnp.arange(nbp * (t // PAGE), dtype=jnp.int32).reshape(nbp, t // PAGE)
    kcmp = _compress(kc.reshape(n_p // PAGE, PAGE, KV_WIDTH), 0, ident, *cmp_k)
    vcmp = _compress(vc.reshape(n_p // PAGE, PAGE, KV_WIDTH), 0, ident, *cmp_v)
    a2 = _nsa_prompt_pair(q.reshape(nbp, t, NSA_WIDTH), gn.reshape(nbp, t, LANES), kcmp, vcmp,
                          kvb.reshape(nbp, t, 6 * KV_WIDTH), rel_bias)
    x1, cst = _merge_prompt(xp.reshape(n_p, D_MODEL), a2.reshape(n_p, NSA_WIDTH),
                            conv3, gates, conv_w, conv_b, wn, wc, wo, t)
    yp = _mlp(x1, g_mlp, wu, wd, g_final).reshape(nbp, t, D_MODEL)
    win_p = min(WINDOW, t)
    new_p = tuple(_kv_rows(z, (nbp, t)) for z in (kc, vc, ks, vs)) + (
        _kv_rows(kw, (nbp, t))[:, t - win_p:], _kv_rows(vw, (nbp, t))[:, t - win_p:],
        cst[:, 8 - (CONV_WIDTH - 1):, :])

    n_s = nbs * n_tok
    (qs, kcs, vcs, kss, vss, kws, vws, _, gns, conv3s, gatess) = _proj(xs.reshape(n_s, D_MODEL), g_attn, w_all)
    n_rows = past_len + n_tok
    n_chunk = n_rows // CMP_STRIDE
    assert n_chunk == past_len // CMP_STRIDE, "new rows must not complete a compression chunk"
    n_cmp = n_chunk - CMP_RATIO + 1
    n_blocks = -(-n_rows // SEL_BLOCK)
    kcmp_s = _compress_t(cache_cmp_k, page_table, *cmp_k)
    vcmp_s = _compress_t(cache_cmp_v, page_table, *cmp_v)

    lane = np.arange(LANES)
    q5 = qs.reshape(nbs, n_tok, GROUP, LANES)
    q3 = jnp.stack([jnp.where(jnp.asarray((lane // HEAD_DIM) == h), q5, jnp.zeros_like(q5))
                    for h in range(N_KV_HEADS)], axis=1)
    q3 = jnp.pad(q3, ((0, 0), (0, 0), (0, 0), (0, ROWS_PER_TOKEN - GROUP), (0, 0)))
    tab = rel_bias.T.reshape(N_KV_HEADS, GROUP, N_BUCKETS)
    rowtab3 = jnp.pad(tab, ((0, 0), (0, ROWS_PER_TOKEN - GROUP), (0, 0)))
    rowtab = jnp.tile(rowtab3, (1, n_tok, 1))
    ocmp, idx = _nsa_sample_cmp(q3.reshape(nbs, N_KV_HEADS, n_tok * ROWS_PER_TOKEN, LANES), kcmp_s, vcmp_s,
                                rowtab, past_len, n_tok, n_cmp, n_blocks)
    idx = idx[:, :, :n_tok, :N_SELECT]
    gn3 = gns[:, :3 * N_HEADS].reshape(nbs, n_tok, N_KV_HEADS, GROUP, 3).transpose(0, 2, 1, 3, 4)
    gn3 = jnp.pad(gn3, ((0, 0), (0, 0), (0, 0), (0, ROWS_PER_TOKEN - GROUP), (0, LANES - 3)))
    q3s = jnp.pad(qs.reshape(nbs, n_tok, GROUP, N_KV_HEADS, HEAD_DIM).transpose(0, 3, 1, 2, 4),
                  ((0, 0), (0, 0), (0, 0), (0, ROWS_PER_TOKEN - GROUP), (0, 0)))
    ocmp5 = ocmp.reshape(nbs, N_KV_HEADS, n_tok, ROWS_PER_TOKEN, HEAD_DIM)
    new_t = lambda z: jnp.pad(z.reshape(nbs, n_tok, N_KV_HEADS, HEAD_DIM).transpose(0, 2, 3, 1),
                              ((0, 0), (0, 0), (0, 0), (0, LANES - n_tok)))
    a_s = _nsa_sample_slc(idx, page_table, cache_slc_k, cache_slc_v, q3s, ocmp5, gn3, rowtab3,
                          new_t(kss), new_t(vss),
                          state_win_k.transpose(0, 2, 3, 1), state_win_v.transpose(0, 2, 3, 1),
                          new_t(kws), new_t(vws), past_len, n_tok)
    a_tm = a_s[:, :, :, :GROUP, :].transpose(2, 0, 3, 1, 4).reshape(n_s, NSA_WIDTH)
    to_tm = lambda z: z.reshape(nbs, n_tok, -1).transpose(1, 0, 2).reshape(n_s, -1)
    past_tm = state_conv.transpose(1, 0, 2).reshape((CONV_WIDTH - 1) * nbs, CONV_DIM)
    x1s, csts = _merge_sample(to_tm(xs), a_tm, to_tm(conv3s), past_tm, to_tm(gatess),
                              conv_w, conv_b, wn, wc, wo, n_tok, nbs)
    ys = _mlp(x1s, g_mlp, wu, wd, g_final).reshape(n_tok, nbs, D_MODEL).transpose(1, 0, 2)
    win_buf = state_win_k.shape[1]
    kw_all = jnp.concatenate([state_win_k, _kv_rows(kws, (nbs, n_tok))], axis=1)[:, -win_buf:]
    vw_all = jnp.concatenate([state_win_v, _kv_rows(vws, (nbs, n_tok))], axis=1)[:, -win_buf:]
    new_s = tuple(_kv_rows(z, (nbs, n_tok)) for z in (kcs, vcs, kss, vss)) + (
        kw_all, vw_all, csts.reshape(CONV_WIDTH - 1, nbs, CONV_DIM).transpose(1, 0, 2))
    return yp, ys, new_p, new_s


def kernel(x_prompt, x_sample, cache_cmp_k, cache_cmp_v, cache_slc_k, cache_slc_v, state_win_k, state_win_v,
           state_conv, page_table, g_attn, w_in, cmp_pe_k, cmp_w1_k, cmp_w2_k, cmp_pe_v, cmp_w1_v, cmp_w2_v,
           conv_w, conv_b, w_nsa_out, w_conv_out, w_o, g_mlp, w_up, w_down, rel_bias, g_final):
    depth = w_in.shape[0]
    assert depth == 1, "the final norm is fused into the last layer's MLP kernel"
    names = ("g_attn", "w_in", "cmp_pe_k", "cmp_w1_k", "cmp_w2_k", "cmp_pe_v", "cmp_w1_v", "cmp_w2_v",
             "conv_w", "conv_b", "w_nsa_out", "w_conv_out", "w_o", "g_mlp", "w_up", "w_down")
    vals = (g_attn, w_in, cmp_pe_k, cmp_w1_k, cmp_w2_k, cmp_pe_v, cmp_w1_v, cmp_w2_v,
            conv_w, conv_b, w_nsa_out, w_conv_out, w_o, g_mlp, w_up, w_down)
    l = 0
    w = {k: v[l] for k, v in zip(names, vals)}
    w["g_final"] = g_final
    yp, ys, new_p, new_s = _layer(
        x_prompt, x_sample,
        (cache_cmp_k[l], cache_cmp_v[l], cache_slc_k[l], cache_slc_v[l]),
        (state_win_k[l], state_win_v[l], state_conv[l]), page_table, w, rel_bias)
    return (yp, ys) + tuple(z[None] for z in new_p) + tuple(z[None] for z in new_s)
```

```python
import functools
import math

import numpy as np
import jax
import jax.numpy as jnp
from jax import lax
from jax.experimental import pallas as pl
from jax.experimental.pallas import tpu as pltpu

F32 = jnp.float32
BF16 = jnp.bfloat16

D_MODEL = 1024
N_HEADS = 8
HEAD_DIM = 64
N_KV_HEADS = 2
GROUP = N_HEADS // N_KV_HEADS
KV_WIDTH = N_KV_HEADS * HEAD_DIM
NSA_WIDTH = N_HEADS * HEAD_DIM
CMP_LEN = 32
CMP_STRIDE = 16
CMP_RATIO = CMP_LEN // CMP_STRIDE
CMP_HIDDEN = 256
SEL_BLOCK = 64
N_SELECT = 16
WINDOW = 512
CONV_DIM = 512
CONV_WIDTH = 3
D_FF = 4 * D_MODEL
N_BUCKETS = 32
MAX_DISTANCE = 128
Q_BLOCK = 128
PAGE = 128
RMS_EPS = 1e-6
NEG_INF = -1e30
FORCE_BONUS = 1e4
LANES = 128
VMEM_LIMIT = 56 * 1024 * 1024


def _bucket_thresholds():
    max_exact = N_BUCKETS // 2
    n = np.arange(max_exact, 4 * MAX_DISTANCE).astype(np.float32)
    large = max_exact + (np.log(n / np.float32(max_exact)) / np.float32(math.log(MAX_DISTANCE / max_exact))
                         * np.float32(N_BUCKETS - max_exact)).astype(np.int32)
    large = np.minimum(large, N_BUCKETS - 1)
    thr = []
    for k in range(max_exact + 1, N_BUCKETS):
        thr.append(int(np.arange(max_exact, 4 * MAX_DISTANCE)[np.argmax(large >= k)]))
    return tuple(thr)


BUCKET_THR = _bucket_thresholds()


def _bucket(dist):
    n = jnp.maximum(dist, 0)
    big = jnp.full(n.shape, N_BUCKETS // 2, jnp.int32)
    for thr in BUCKET_THR:
        big = big + (n >= thr).astype(jnp.int32)
    return jnp.where(n < N_BUCKETS // 2, n, big)


def _bias_from_scalars(bucket, table_fn):
    out = jnp.zeros(bucket.shape, F32)
    for k in range(N_BUCKETS):
        out = jnp.where(bucket == k, table_fn(k), out)
    return out


def _bias_from_rows(bucket, rowtab):
    out = jnp.zeros(bucket.shape, F32)
    for k in range(N_BUCKETS):
        out = jnp.where(bucket == k, rowtab[:, k:k + 1], out)
    return out


def _dot(a, b):
    return jnp.dot(a, b, preferred_element_type=F32)


def _dot_t(a, b):
    return lax.dot_general(a, b, (((1,), (1,)), ((), ())), preferred_element_type=F32)


def _split_dot(p, m):
    hi = p.astype(BF16)
    lo = (p - hi.astype(F32)).astype(BF16)
    return _dot(hi, m) + _dot(lo, m)


def _cparams(sem):
    return pltpu.CompilerParams(dimension_semantics=sem, vmem_limit_bytes=VMEM_LIMIT)


PROJ_WIDTHS = (NSA_WIDTH, 6 * KV_WIDTH, LANES, 3 * CONV_DIM, 2 * D_MODEL)
PROJ_COLS = sum(PROJ_WIDTHS)


def _proj_kernel(x_ref, g_ref, w_ref, q_ref, kc_ref, vc_ref, ks_ref, vs_ref, kw_ref, vw_ref,
                 kvb_ref, gn_ref, conv_ref, gate_ref):
    x = x_ref[...]
    y = x * lax.rsqrt(jnp.mean(x * x, axis=-1, keepdims=True) + RMS_EPS)
    hb = (y * g_ref[...]).astype(BF16)
    q_ref[...] = _dot(hb, w_ref[:, 0:NSA_WIDTH]).astype(q_ref.dtype)
    off = NSA_WIDTH
    kv = _dot(hb, w_ref[:, off:off + 6 * KV_WIDTH])
    for i, r in enumerate((kc_ref, vc_ref, ks_ref, vs_ref, kw_ref, vw_ref)):
        r[...] = kv[:, i * KV_WIDTH:(i + 1) * KV_WIDTH]
    kvb_ref[...] = kv.astype(BF16)
    off += 6 * KV_WIDTH
    gn_ref[...] = _dot(hb, w_ref[:, off:off + LANES])
    off += LANES
    conv_ref[...] = _dot(hb, w_ref[:, off:off + 3 * CONV_DIM])
    off += 3 * CONV_DIM
    gate_ref[...] = _dot(hb, w_ref[:, off:off + 2 * D_MODEL])


def _proj(x2d, g, w_all):
    n = x2d.shape[0]
    tm = min(512, n)
    assert n % tm == 0
    row = lambda w: pl.BlockSpec((tm, w), lambda i: (i, 0))
    out_shapes = ([jax.ShapeDtypeStruct((n, NSA_WIDTH), BF16)]
                  + [jax.ShapeDtypeStruct((n, KV_WIDTH), F32)] * 6
                  + [jax.ShapeDtypeStruct((n, 6 * KV_WIDTH), BF16),
                     jax.ShapeDtypeStruct((n, LANES), F32),
                     jax.ShapeDtypeStruct((n, 3 * CONV_DIM), F32),
                     jax.ShapeDtypeStruct((n, 2 * D_MODEL), F32)])
    out_specs = ([row(NSA_WIDTH)] + [row(KV_WIDTH)] * 6
                 + [row(6 * KV_WIDTH), row(LANES), row(3 * CONV_DIM), row(2 * D_MODEL)])
    return pl.pallas_call(
        _proj_kernel,
        grid=(n // tm,),
        in_specs=[row(D_MODEL),
                  pl.BlockSpec((1, D_MODEL), lambda i: (0, 0)),
                  pl.BlockSpec((D_MODEL, PROJ_COLS), lambda i: (0, 0))],
        out_specs=out_specs,
        out_shape=out_shapes,
        compiler_params=_cparams(("arbitrary",)),
        name="proj",
    )(x2d, g, w_all)


def _prep_w_in(w_in):
    sizes = (NSA_WIDTH,) + (KV_WIDTH,) * 6 + (3 * N_HEADS, CONV_DIM, CONV_DIM, CONV_DIM, D_MODEL, D_MODEL)
    offs = np.concatenate([[0], np.cumsum(sizes)])
    wq = w_in[:, :NSA_WIDTH].reshape(D_MODEL, N_KV_HEADS, GROUP, HEAD_DIM)
    wq = wq.transpose(0, 2, 1, 3).reshape(D_MODEL, NSA_WIDTH)
    wkv = w_in[:, offs[1]:offs[7]]
    wgn = jnp.pad(w_in[:, offs[7]:offs[8]], ((0, 0), (0, LANES - 3 * N_HEADS)))
    wrest = w_in[:, offs[8]:]
    return jnp.concatenate([wq, wkv, wgn, wrest], axis=1).astype(BF16)


CHUNKS_PER_PAGE = PAGE // CMP_STRIDE


def _compress_kernel(pt_ref, *refs, pp):
    del pt_ref
    page_refs = refs[:pp + 1]
    wbd_ref, pe_ref, w1_ref, w2_ref, out_ref, acc_ref, pehid_ref = refs[pp + 1:]

    @pl.when((pl.program_id(0) == 0) & (pl.program_id(1) == 0))
    def _():
        pehid_ref[...] = _dot(pe_ref[...], w1_ref[...])

    cols = []
    for r in range(CMP_STRIDE):
        xr = jnp.concatenate([p[0, pl.ds(r, CHUNKS_PER_PAGE, stride=CMP_STRIDE), :] for p in page_refs], axis=0)
        cols.append(xr.astype(BF16))
    x = jnp.concatenate(cols, axis=1)
    acc_ref[...] = _dot(x, wbd_ref[...])
    nc = CHUNKS_PER_PAGE * pp
    pe = pehid_ref[0:1, :]
    hids = []
    for h in range(N_KV_HEADS):
        base = h * CMP_RATIO * CMP_HIDDEN
        p0 = acc_ref[0:nc, base:base + CMP_HIDDEN]
        p1 = acc_ref[pl.ds(1, nc), base + CMP_HIDDEN:base + 2 * CMP_HIDDEN]
        hids.append(jax.nn.gelu((pe + p0) + p1).astype(BF16))
    out_ref[0] = _dot(jnp.concatenate(hids, axis=1), w2_ref[...])


def _compress(pages, colblk, page_table, pe, w1, w2):
    nb, npages = page_table.shape
    pp = min(32, npages)
    assert npages % pp == 0
    nc = CHUNKS_PER_PAGE * pp
    w1r = w1.reshape(CMP_RATIO, CMP_STRIDE, HEAD_DIM, CMP_HIDDEN)
    blk = w1r.transpose(1, 2, 0, 3).reshape(CMP_STRIDE, HEAD_DIM, CMP_RATIO * CMP_HIDDEN)
    z = jnp.zeros_like(blk)
    wbd = jnp.concatenate([jnp.concatenate([blk, z], axis=2), jnp.concatenate([z, blk], axis=2)], axis=1)
    wbd = wbd.reshape(CMP_STRIDE * KV_WIDTH, N_KV_HEADS * CMP_RATIO * CMP_HIDDEN).astype(BF16)
    z2 = jnp.zeros_like(w2)
    w2bd = jnp.concatenate([jnp.concatenate([w2, z2], axis=1), jnp.concatenate([z2, w2], axis=1)], axis=0).astype(BF16)
    pe8 = jnp.broadcast_to(pe.reshape(1, CMP_LEN * HEAD_DIM), (8, CMP_LEN * HEAD_DIM)).astype(BF16)

    def page_spec(p):
        def imap(b, s, pt):
            return (pt[b, jnp.minimum(s * pp + p, npages - 1)], 0, colblk)
        return pl.BlockSpec((1, PAGE, KV_WIDTH), imap)

    const = lambda shape: pl.BlockSpec(shape, lambda b, s, pt: (0,) * len(shape))
    grid_spec = pltpu.PrefetchScalarGridSpec(
        num_scalar_prefetch=1,
        grid=(nb, npages // pp),
        in_specs=[page_spec(p) for p in range(pp + 1)]
        + [const(wbd.shape), const(pe8.shape), const((CMP_LEN * HEAD_DIM, CMP_HIDDEN)), const(w2bd.shape)],
        out_specs=pl.BlockSpec((1, nc, KV_WIDTH), lambda b, s, pt: (b, s, 0)),
        scratch_shapes=[pltpu.VMEM((nc + 8, N_KV_HEADS * CMP_RATIO * CMP_HIDDEN), F32),
                        pltpu.VMEM((8, CMP_HIDDEN), F32)],
    )
    return pl.pallas_call(
        functools.partial(_compress_kernel, pp=pp),
        grid_spec=grid_spec,
        out_shape=jax.ShapeDtypeStruct((nb, npages * CHUNKS_PER_PAGE, KV_WIDTH), F32),
        compiler_params=_cparams(("arbitrary", "arbitrary")),
        name="compress",
    )(page_table, *([pages] * (pp + 1)), wbd, pe8, w1.astype(BF16), w2bd)


def _compress_t_kernel(pt_ref, *refs, pp, npages):
    (cache_ref, w1p_ref, pe_ref, w1_ref, w2_ref, out_ref, pbuf_ref, sem_ref, xt_ref, acc_ref, pehid_ref) = refs
    b = pl.program_id(0)
    s = pl.program_id(1)
    n_steps = pl.num_programs(1)
    step = b * n_steps + s
    last = pl.num_programs(0) * n_steps - 1
    slot = step % 2

    def page_copy(bb, ss, p, sl):
        page = pt_ref[bb, jnp.minimum(ss * pp + p, npages - 1)]
        return pltpu.make_async_copy(cache_ref.at[page], pbuf_ref.at[sl, p], sem_ref.at[sl])

    @pl.when(step == 0)
    def _():
        pehid_ref[...] = _dot(pe_ref[...], w1_ref[...])
        for p in range(pp + 1):
            page_copy(0, 0, p, 0).start()

    wrap = s + 1 == n_steps
    nxt_b = jnp.minimum(jnp.where(wrap, b + 1, b), pl.num_programs(0) - 1)
    nxt_s = jnp.where(step == last, s, jnp.where(wrap, 0, s + 1))
    for p in range(pp + 1):
        page_copy(nxt_b, nxt_s, p, 1 - slot).start()
    for p in range(pp + 1):
        page_copy(b, s, p, slot).wait()

    for p in range(pp + 1):
        page = pbuf_ref[slot, p].reshape(KV_WIDTH, PAGE)
        xt_ref[p * PAGE:(p + 1) * PAGE, :] = page.T
    nc = CHUNKS_PER_PAGE * pp
    m = nc + CHUNKS_PER_PAGE
    pe = pehid_ref[0:1, :]
    first_head = lax.broadcasted_iota(jnp.int32, (m, KV_WIDTH), 1) < HEAD_DIM
    xs = [[], []]
    for r in range(0, CMP_STRIDE, 2):
        a = xt_ref[pl.ds(r, m, stride=CMP_STRIDE), :]
        c = xt_ref[pl.ds(r + 1, m, stride=CMP_STRIDE), :]
        xs[0].append(jnp.where(first_head, a, pltpu.roll(c, HEAD_DIM, axis=1)).astype(BF16))
        xs[1].append(jnp.where(first_head, pltpu.roll(a, HEAD_DIM, axis=1), c).astype(BF16))
    hids = []
    for h in range(N_KV_HEADS):
        x = jnp.concatenate(xs[h], axis=1)
        acc_ref[h] = _dot(x, w1p_ref[...])
        p0 = acc_ref[h, 0:nc, 0:CMP_HIDDEN]
        p1 = acc_ref[h, pl.ds(1, nc), CMP_HIDDEN:2 * CMP_HIDDEN]
        hids.append(jax.nn.gelu((pe + p0) + p1).astype(BF16))
    out_ref[0] = _dot(jnp.concatenate(hids, axis=1), w2_ref[...])

    @pl.when(step == last)
    def _():
        for p in range(pp + 1):
            page_copy(nxt_b, nxt_s, p, 1 - slot).wait()


def _compress_t(cache, page_table, pe, w1, w2):
    nb, npages = page_table.shape
    pp = min(32, npages)
    assert npages % pp == 0
    nc = CHUNKS_PER_PAGE * pp
    pages = cache.transpose(0, 2, 3, 1)
    w1p = w1.reshape(CMP_RATIO, CMP_STRIDE * HEAD_DIM, CMP_HIDDEN).transpose(1, 0, 2)
    w1p = w1p.reshape(CMP_STRIDE * HEAD_DIM, CMP_RATIO * CMP_HIDDEN).astype(BF16)
    z2 = jnp.zeros_like(w2)
    w2bd = jnp.concatenate([jnp.concatenate([w2, z2], axis=1), jnp.concatenate([z2, w2], axis=1)], axis=0).astype(BF16)
    pe8 = jnp.broadcast_to(pe.reshape(1, CMP_LEN * HEAD_DIM), (8, CMP_LEN * HEAD_DIM)).astype(BF16)

    const = lambda shape: pl.BlockSpec(shape, lambda b, s, pt: (0,) * len(shape))
    grid_spec = pltpu.PrefetchScalarGridSpec(
        num_scalar_prefetch=1,
        grid=(nb, npages // pp),
        in_specs=[pl.BlockSpec(memory_space=pl.ANY),
                  const(w1p.shape), const(pe8.shape), const((CMP_LEN * HEAD_DIM, CMP_HIDDEN)), const(w2bd.shape)],
        out_specs=pl.BlockSpec((1, nc, KV_WIDTH), lambda b, s, pt: (b, s, 0)),
        scratch_shapes=[pltpu.VMEM((2, pp + 1, N_KV_HEADS, HEAD_DIM, PAGE), F32),
                        pltpu.SemaphoreType.DMA((2,)),
                        pltpu.VMEM(((pp + 1) * PAGE, KV_WIDTH), F32),
                        pltpu.VMEM((N_KV_HEADS, nc + CHUNKS_PER_PAGE, CMP_RATIO * CMP_HIDDEN), F32),
                        pltpu.VMEM((8, CMP_HIDDEN), F32)],
    )
    return pl.pallas_call(
        functools.partial(_compress_t_kernel, pp=pp, npages=npages),
        grid_spec=grid_spec,
        out_shape=jax.ShapeDtypeStruct((nb, npages * CHUNKS_PER_PAGE, KV_WIDTH), F32),
        compiler_params=_cparams(("arbitrary", "arbitrary")),
        name="compress_t",
    )(page_table, pages, w1p, pe8, w1.astype(BF16), w2bd)


def _sel_map(n_rows, n_cmp, n_cols, n_blocks):
    c0 = np.arange(n_rows)[:, None] * CMP_STRIDE
    s0 = np.arange(n_cols)[None, :] * SEL_BLOCK
    ov = np.minimum(c0 + CMP_LEN, s0 + SEL_BLOCK) - np.maximum(c0, s0)
    m = np.clip(ov, 0, None).astype(np.float32) / CMP_LEN
    m[n_cmp:] = 0.0
    m[:, n_blocks:] = 0.0
    return m


CMP_FRONT = 16
CMP_NEAR = 24
SLC_NEAR = 2 * Q_BLOCK
SLC_TILE = 512
WIN_KEYS = WINDOW + Q_BLOCK


def _softmax_update(state, s, v, mask=None, v_transposed=False):
    m_old, l_old, acc = state
    m_new = jnp.maximum(m_old, jnp.max(s, axis=-1, keepdims=True))
    p = jnp.exp(s - m_new)
    if mask is not None:
        p = jnp.where(mask, p, 0.0)
    alpha = jnp.exp(m_old - m_new)
    l_new = alpha * l_old + jnp.sum(p, axis=-1, keepdims=True)
    pv = _dot_t(p.astype(BF16), v) if v_transposed else _dot(p.astype(BF16), v)
    return m_new, l_new, alpha * acc + pv


def _softmax_finish(state):
    _, l, acc = state
    return jnp.where(l > 0.0, acc / jnp.where(l > 0.0, l, 1.0), 0.0)


def _tile_rows(x, n):
    return jnp.concatenate([x] * n, axis=0)


def _nsa_prompt_kernel(rb_ref, q_ref, gn_ref, kcmp_ref, vcmp_ref, selmap_ref, ks_ref, vs_ref, kw_ref, vw_ref,
                       et_ref, out_ref, nbc_ref, nbs_ref, nbw_ref):
    hk = pl.program_id(1)
    qb = pl.program_id(2)
    s0 = qb * Q_BLOCK
    gq = GROUP * Q_BLOCK

    @pl.when(qb == 0)
    def _():
        for g in range(GROUP):
            tab = lambda k, g=g: rb_ref[k, hk * GROUP + g]
            ql = lax.broadcasted_iota(jnp.int32, (Q_BLOCK, LANES), 0)
            cl = lax.broadcasted_iota(jnp.int32, (Q_BLOCK, LANES), 1)
            dist = ql - CMP_STRIDE * (cl - CMP_FRONT) - (CMP_LEN - 1)
            ok = (dist >= 0) & (cl < CMP_NEAR)
            nbc_ref[g] = jnp.where(ok, _bias_from_scalars(_bucket(dist), tab), NEG_INF)
            ql = lax.broadcasted_iota(jnp.int32, (Q_BLOCK, SLC_NEAR), 0)
            kl = lax.broadcasted_iota(jnp.int32, (Q_BLOCK, SLC_NEAR), 1)
            dist = ql - kl + Q_BLOCK
            nbs_ref[g] = jnp.where(dist >= 0, _bias_from_scalars(_bucket(dist), tab), NEG_INF)
            ql = lax.broadcasted_iota(jnp.int32, (Q_BLOCK, WIN_KEYS), 0)
            kl = lax.broadcasted_iota(jnp.int32, (Q_BLOCK, WIN_KEYS), 1)
            dist = ql - kl + WINDOW
            ok = (dist >= 0) & (dist < WINDOW)
            nbw_ref[g] = jnp.where(ok, _bias_from_scalars(_bucket(dist), tab), NEG_INF)

    lane = lax.broadcasted_iota(jnp.int32, (Q_BLOCK, LANES), 1)
    lmask = (lane >= hk * HEAD_DIM) & (lane < (hk + 1) * HEAD_DIM)
    qblk = q_ref[0].astype(F32)
    qp = jnp.concatenate(
        [jnp.where(lmask, qblk[:, g * LANES:(g + 1) * LANES], 0.0) for g in range(GROUP)], axis=0)
    qp = (qp * (HEAD_DIM ** -0.5)).astype(BF16)

    far_bias = jnp.concatenate(
        [jnp.full((Q_BLOCK, 1), rb_ref[N_BUCKETS - 1, hk * GROUP + g], F32) for g in range(GROUP)], axis=0)

    n_cmp_pad = kcmp_ref.shape[1] - CMP_FRONT - (LANES - CMP_FRONT)
    kc_far = kcmp_ref[0, CMP_FRONT:CMP_FRONT + n_cmp_pad, :].astype(BF16)
    vc_far = vcmp_ref[0, CMP_FRONT:CMP_FRONT + n_cmp_pad, :].astype(BF16)
    near0 = pl.multiple_of(qb * (Q_BLOCK // CMP_STRIDE), 8)
    kc_near = kcmp_ref[0, pl.ds(near0, LANES), :].astype(BF16)
    vc_near = vcmp_ref[0, pl.ds(near0, LANES), :].astype(BF16)
    c8 = qb * (Q_BLOCK // CMP_STRIDE)
    ccol = lax.broadcasted_iota(jnp.int32, (1, n_cmp_pad), 1)
    s_far = _dot_t(qp, kc_far) + jnp.where(ccol < c8 - CMP_FRONT, 0.0, NEG_INF)
    ncol = lax.broadcasted_iota(jnp.int32, (1, LANES), 1)
    s_near = _dot_t(qp, kc_near) + (nbc_ref[...].reshape(gq, LANES)
                                    + jnp.where(ncol >= CMP_FRONT - c8, 0.0, NEG_INF))
    m = jnp.maximum(jnp.max(s_far, axis=-1, keepdims=True) + far_bias, jnp.max(s_near, axis=-1, keepdims=True))
    p_far = jnp.exp(s_far - (m - far_bias))
    p_near = jnp.exp(s_near - m)
    l = jnp.sum(p_far, axis=-1, keepdims=True) + jnp.sum(p_near, axis=-1, keepdims=True)
    inv = jnp.where(m > 0.5 * NEG_INF, 1.0 / l, 0.0)
    pc_far = p_far * inv
    pc_near = p_near * inv
    o_cmp = _dot(pc_far.astype(BF16), vc_far) + _dot(pc_near.astype(BF16), vc_near)

    init = (jnp.full((gq, 1), NEG_INF, F32), jnp.zeros((gq, 1), F32), jnp.zeros((gq, LANES), F32))
    p0 = pl.multiple_of(s0, LANES)
    wcol = lax.broadcasted_iota(jnp.int32, (1, WIN_KEYS), 1)
    s = _dot_t(qp, kw_ref[0, pl.ds(p0, WIN_KEYS), :]) + (nbw_ref[...].reshape(gq, WIN_KEYS)
                                                         + jnp.where(wcol >= WINDOW - s0, 0.0, NEG_INF))
    o_win = _softmax_finish(_softmax_update(init, s, vw_ref[0, pl.ds(p0, WIN_KEYS), :]))

    ps_far = sum(pc_far[g * Q_BLOCK:(g + 1) * Q_BLOCK] for g in range(GROUP))
    ps_near = sum(pc_near[g * Q_BLOCK:(g + 1) * Q_BLOCK] for g in range(GROUP))
    sm_far = selmap_ref[CMP_FRONT:CMP_FRONT + n_cmp_pad, :].astype(BF16)
    sm_near = selmap_ref[pl.ds(near0, LANES), :].astype(BF16)
    imp = _split_dot(ps_far, sm_far) + _split_dot(ps_near, sm_near)
    imp_t = imp.T
    n_sel = imp_t.shape[0]
    blk = lax.broadcasted_iota(jnp.int32, (n_sel, Q_BLOCK), 0)
    cur = (s0 + lax.broadcasted_iota(jnp.int32, (n_sel, Q_BLOCK), 1)) // SEL_BLOCK
    forced = (blk == 0) | (blk == cur) | (blk == cur - 1)
    val = jnp.where(blk > cur, NEG_INF, imp_t + FORCE_BONUS * forced.astype(F32))
    sel_t = jnp.zeros((n_sel, Q_BLOCK), F32)
    for _ in range(N_SELECT):
        mx = jnp.max(val, axis=0, keepdims=True)
        first = jnp.min(jnp.where(val == mx, blk, n_sel), axis=0, keepdims=True)
        hit = blk == first
        sel_t = jnp.where(hit, 1.0, sel_t)
        val = jnp.where(hit, -3e38, val)
    sel = sel_t.T > 0.5

    jcol = lax.broadcasted_iota(jnp.int32, (Q_BLOCK, n_sel), 1)
    far_blocks = (s0 - Q_BLOCK) // SEL_BLOCK
    pen_far = jnp.where(sel & (jcol < far_blocks), 0.0, NEG_INF).astype(BF16)
    pen_near = jnp.where(sel, 0.0, NEG_INF).astype(BF16)
    q_far = jnp.concatenate([qp, _tile_rows(pen_far, GROUP)], axis=1)
    q_near = jnp.concatenate([qp, _tile_rows(pen_near, GROUP)], axis=1)

    def far_step(kt, state):
        p0 = pl.multiple_of(Q_BLOCK + kt * SLC_TILE, LANES)
        k = jnp.concatenate([ks_ref[0, pl.ds(p0, SLC_TILE), :], et_ref[pl.ds(p0, SLC_TILE), :]], axis=1)
        return _softmax_update(state, _dot_t(q_far, k), vs_ref[0, pl.ds(p0, SLC_TILE), :])

    n_far = (qb + 2) // (SLC_TILE // Q_BLOCK)
    m_far, l_far, acc_far = lax.fori_loop(0, n_far, far_step, init)
    state = (m_far + far_bias, l_far, acc_far)
    k = jnp.concatenate([ks_ref[0, pl.ds(p0, SLC_NEAR), :], et_ref[pl.ds(p0, SLC_NEAR), :]], axis=1)
    kcol = lax.broadcasted_iota(jnp.int32, (1, SLC_NEAR), 1)
    s = _dot_t(q_near, k) + (nbs_ref[...].reshape(gq, SLC_NEAR) + jnp.where(kcol >= Q_BLOCK - s0, 0.0, NEG_INF))
    o_slc = _softmax_finish(_softmax_update(state, s, vs_ref[0, pl.ds(p0, SLC_NEAR), :]))

    gs = jax.nn.sigmoid(gn_ref[0])

    def gate(g, i):
        c0, c1 = 3 * g + i, 3 * (GROUP + g) + i
        return jnp.where(hk == 0, gs[:, c0:c0 + 1], gs[:, c1:c1 + 1])

    for g in range(GROUP):
        rows = slice(g * Q_BLOCK, (g + 1) * Q_BLOCK)
        o = gate(g, 0) * o_cmp[rows] + gate(g, 1) * o_slc[rows] + gate(g, 2) * o_win[rows]
        out_ref[0, 0, :, g * LANES:(g + 1) * LANES] = jnp.where(lmask, o, 0.0).astype(out_ref.dtype)


def _nsa_prompt(q, gn, kcmp, vcmp, kvb, rel_bias):
    b, t, _ = q.shape
    assert t % SLC_TILE == 0
    nqb = t // Q_BLOCK
    n_chunk = t // CMP_STRIDE
    n_cmp = n_chunk - CMP_RATIO + 1
    n_blocks = t // SEL_BLOCK
    assert n_blocks <= LANES
    cpad = ((0, 0), (CMP_FRONT, LANES - CMP_FRONT), (0, 0))
    kcmp_p = jnp.pad(kcmp, cpad)
    vcmp_p = jnp.pad(vcmp, cpad)
    selmap = jnp.asarray(np.pad(_sel_map(n_chunk, n_cmp, LANES, n_blocks),
                                ((CMP_FRONT, LANES - CMP_FRONT), (0, 0))), F32)
    ks_p = jnp.pad(kvb[:, :, 2 * KV_WIDTH:3 * KV_WIDTH], ((0, 0), (Q_BLOCK, 0), (0, 0)))
    vs_p = jnp.pad(kvb[:, :, 3 * KV_WIDTH:4 * KV_WIDTH], ((0, 0), (Q_BLOCK, 0), (0, 0)))
    kw_p = jnp.pad(kvb[:, :, 4 * KV_WIDTH:5 * KV_WIDTH], ((0, 0), (WINDOW, 0), (0, 0)))
    vw_p = jnp.pad(kvb[:, :, 5 * KV_WIDTH:6 * KV_WIDTH], ((0, 0), (WINDOW, 0), (0, 0)))
    pidx = np.arange(t + Q_BLOCK)
    e = (pidx[:, None] // SEL_BLOCK - Q_BLOCK // SEL_BLOCK == np.arange(LANES)[None, :])
    e = jnp.asarray(e.astype(np.float32), BF16)
    full = lambda arr: pl.BlockSpec((1,) + arr.shape[1:], lambda bi, h, i: (bi, 0, 0))
    return pl.pallas_call(
        _nsa_prompt_kernel,
        grid=(b, N_KV_HEADS, nqb),
        in_specs=[pl.BlockSpec(memory_space=pltpu.SMEM),
                  pl.BlockSpec((1, Q_BLOCK, NSA_WIDTH), lambda bi, h, i: (bi, i, 0)),
                  pl.BlockSpec((1, Q_BLOCK, LANES), lambda bi, h, i: (bi, i, 0)),
                  full(kcmp_p), full(vcmp_p),
                  pl.BlockSpec(selmap.shape, lambda bi, h, i: (0, 0)),
                  full(ks_p), full(vs_p), full(kw_p), full(vw_p),
                  pl.BlockSpec(e.shape, lambda bi, h, i: (0, 0))],
        out_specs=pl.BlockSpec((1, 1, Q_BLOCK, NSA_WIDTH), lambda bi, h, i: (h, bi, i, 0)),
        out_shape=jax.ShapeDtypeStruct((N_KV_HEADS, b, t, NSA_WIDTH), BF16),
        scratch_shapes=[pltpu.VMEM((GROUP, Q_BLOCK, LANES), F32),
                        pltpu.VMEM((GROUP, Q_BLOCK, SLC_NEAR), F32),
                        pltpu.VMEM((GROUP, Q_BLOCK, WIN_KEYS), F32)],
        compiler_params=_cparams(("arbitrary", "arbitrary", "arbitrary")),
        name="nsa_prompt",
    )(rel_bias, q, gn, kcmp_p, vcmp_p, selmap, ks_p, vs_p, kw_p, vw_p, e)


def _col_softmax_update(state, s, vt):
    m_old, l_old, acc = state
    m_new = jnp.maximum(m_old, jnp.max(s, axis=0, keepdims=True))
    p = jnp.exp(s - m_new)
    alpha = jnp.exp(m_old - m_new)
    l_new = alpha * l_old + jnp.sum(p, axis=0, keepdims=True)
    return m_new, l_new, alpha * acc + _dot(vt, p.astype(BF16))


def _col_softmax_finish(state):
    _, l, acc = state
    return acc * jnp.where(l > 0.0, 1.0 / jnp.where(l > 0.0, l, 1.0), 0.0)


def _nsa_prompt_t_kernel(rb_ref, q_ref, gn_ref, kcmp_ref, vcmp_ref, vcmpt_ref, selmap_ref, selmapt_ref,
                         ks_ref, vst_ref, kw_ref, vwt_ref, et_ref, out_ref, nbc_ref, nbs_ref, nbw_ref):
    hk = pl.program_id(1)
    qb = pl.program_id(2)
    s0 = qb * Q_BLOCK
    gq = GROUP * Q_BLOCK

    @pl.when(qb == 0)
    def _():
        for g in range(GROUP):
            tab = lambda k, g=g: rb_ref[k, hk * GROUP + g]
            cols = slice(g * Q_BLOCK, (g + 1) * Q_BLOCK)
            cl = lax.broadcasted_iota(jnp.int32, (LANES, Q_BLOCK), 0)
            ql = lax.broadcasted_iota(jnp.int32, (LANES, Q_BLOCK), 1)
            dist = ql - CMP_STRIDE * (cl - CMP_FRONT) - (CMP_LEN - 1)
            ok = (dist >= 0) & (cl < CMP_NEAR)
            nbc_ref[:, cols] = jnp.where(ok, _bias_from_scalars(_bucket(dist), tab), NEG_INF)
            kl = lax.broadcasted_iota(jnp.int32, (SLC_NEAR, Q_BLOCK), 0)
            ql = lax.broadcasted_iota(jnp.int32, (SLC_NEAR, Q_BLOCK), 1)
            dist = ql - kl + Q_BLOCK
            nbs_ref[:, cols] = jnp.where(dist >= 0, _bias_from_scalars(_bucket(dist), tab), NEG_INF)
            kl = lax.broadcasted_iota(jnp.int32, (WIN_KEYS, Q_BLOCK), 0)
            ql = lax.broadcasted_iota(jnp.int32, (WIN_KEYS, Q_BLOCK), 1)
            dist = ql - kl + WINDOW
            ok = (dist >= 0) & (dist < WINDOW)
            nbw_ref[:, cols] = jnp.where(ok, _bias_from_scalars(_bucket(dist), tab), NEG_INF)

    lane = lax.broadcasted_iota(jnp.int32, (Q_BLOCK, LANES), 1)
    lmask = (lane >= hk * HEAD_DIM) & (lane < (hk + 1) * HEAD_DIM)
    qblk = q_ref[0].astype(F32)
    qp = jnp.concatenate(
        [jnp.where(lmask, qblk[:, g * LANES:(g + 1) * LANES], 0.0) for g in range(GROUP)], axis=0)
    qp = (qp * (HEAD_DIM ** -0.5)).astype(BF16)
    far_bias = jnp.concatenate(
        [jnp.full((1, Q_BLOCK), rb_ref[N_BUCKETS - 1, hk * GROUP + g], F32) for g in range(GROUP)], axis=1)
    init = (jnp.full((1, gq), NEG_INF, F32), jnp.zeros((1, gq), F32), jnp.zeros((LANES, gq), F32))
    p0 = pl.multiple_of(s0, LANES)

    n_cmp_pad = kcmp_ref.shape[1] - LANES
    c8 = qb * (Q_BLOCK // CMP_STRIDE)
    near0 = pl.multiple_of(c8, 8)
    kc_far = kcmp_ref[0, CMP_FRONT:CMP_FRONT + n_cmp_pad, :].astype(BF16)
    kc_near = kcmp_ref[0, pl.ds(near0, LANES), :].astype(BF16)
    crow = lax.broadcasted_iota(jnp.int32, (n_cmp_pad, gq), 0)
    s_far = jnp.where(crow < c8 - CMP_FRONT, _dot_t(kc_far, qp), NEG_INF)
    nrow = lax.broadcasted_iota(jnp.int32, (LANES, gq), 0)
    s_near = jnp.where(nrow >= CMP_FRONT - c8, _dot_t(kc_near, qp) + nbc_ref[...], NEG_INF)
    m = jnp.maximum(jnp.max(s_far, axis=0, keepdims=True) + far_bias, jnp.max(s_near, axis=0, keepdims=True))
    p_far = jnp.exp(s_far - (m - far_bias))
    p_near = jnp.exp(s_near - m)
    l = jnp.sum(p_far, axis=0, keepdims=True) + jnp.sum(p_near, axis=0, keepdims=True)
    inv = jnp.where(m > 0.5 * NEG_INF, 1.0 / l, 0.0)
    pc_far = p_far * inv
    pc_near = p_near * inv
    vct_near = vcmp_ref[0, pl.ds(near0, LANES), :].T.astype(BF16)
    o_cmp = _dot(vcmpt_ref[0], pc_far.astype(BF16)) + _dot(vct_near, pc_near.astype(BF16))

    wrow = lax.broadcasted_iota(jnp.int32, (WIN_KEYS, gq), 0)
    s = jnp.where(wrow >= WINDOW - s0, _dot_t(kw_ref[0, pl.ds(p0, WIN_KEYS), :], qp) + nbw_ref[...], NEG_INF)
    o_win = _col_softmax_finish(_col_softmax_update(init, s, vwt_ref[0, :, pl.ds(p0, WIN_KEYS)]))

    ps_far = sum(pc_far[:, g * Q_BLOCK:(g + 1) * Q_BLOCK] for g in range(GROUP))
    ps_near = sum(pc_near[:, g * Q_BLOCK:(g + 1) * Q_BLOCK] for g in range(GROUP))

    def split_dot(a, p):
        hi = p.astype(BF16)
        return _dot(a, hi) + _dot(a, (p - hi.astype(F32)).astype(BF16))

    sm_near_t = selmap_ref[pl.ds(near0, LANES), :].T.astype(BF16)
    imp_t = split_dot(selmapt_ref[...], ps_far) + split_dot(sm_near_t, ps_near)
    n_sel = imp_t.shape[0]
    blk = lax.broadcasted_iota(jnp.int32, (n_sel, Q_BLOCK), 0)
    cur = (s0 + lax.broadcasted_iota(jnp.int32, (n_sel, Q_BLOCK), 1)) // SEL_BLOCK
    forced = (blk == 0) | (blk == cur) | (blk == cur - 1)
    val = jnp.where(blk > cur, NEG_INF, imp_t + FORCE_BONUS * forced.astype(F32))
    sel_t = jnp.zeros((n_sel, Q_BLOCK), F32)
    for _ in range(N_SELECT):
        mx = jnp.max(val, axis=0, keepdims=True)
        first = jnp.min(jnp.where(val == mx, blk, n_sel), axis=0, keepdims=True)
        hit = blk == first
        sel_t = jnp.where(hit, 1.0, sel_t)
        val = jnp.where(hit, -3e38, val)
    sel = sel_t.T > 0.5

    jcol = lax.broadcasted_iota(jnp.int32, (Q_BLOCK, n_sel), 1)
    far_blocks = (s0 - Q_BLOCK) // SEL_BLOCK
    pen_far = jnp.where(sel & (jcol < far_blocks), 0.0, NEG_INF).astype(BF16)
    pen_near = jnp.where(sel, 0.0, NEG_INF).astype(BF16)
    q_far = jnp.concatenate([qp, _tile_rows(pen_far, GROUP)], axis=1)
    q_near = jnp.concatenate([qp, _tile_rows(pen_near, GROUP)], axis=1)

    n_tiles = (ks_ref.shape[1] - Q_BLOCK) // SLC_TILE

    def far_logits(kt):
        f0 = pl.multiple_of(Q_BLOCK + kt * SLC_TILE, LANES)
        k = jnp.concatenate([ks_ref[0, pl.ds(f0, SLC_TILE), :], et_ref[pl.ds(f0, SLC_TILE), :]], axis=1)
        return _dot_t(k, q_far)

    def far_step(kt, carry):
        s_cur, state = carry
        s_next = far_logits(jnp.minimum(kt + 1, n_tiles - 1))
        f0 = pl.multiple_of(Q_BLOCK + kt * SLC_TILE, LANES)
        return s_next, _col_softmax_update(state, s_cur, vst_ref[0, :, pl.ds(f0, SLC_TILE)])

    n_far = (qb + 2) // (SLC_TILE // Q_BLOCK)
    _, (m_far, l_far, acc_far) = lax.fori_loop(0, n_far, far_step, (far_logits(0), init))
    state = (m_far + far_bias, l_far, acc_far)
    k = jnp.concatenate([ks_ref[0, pl.ds(p0, SLC_NEAR), :], et_ref[pl.ds(p0, SLC_NEAR), :]], axis=1)
    krow = lax.broadcasted_iota(jnp.int32, (SLC_NEAR, gq), 0)
    s = jnp.where(krow >= Q_BLOCK - s0, _dot_t(k, q_near) + nbs_ref[...], NEG_INF)
    o_slc = _col_softmax_finish(_col_softmax_update(state, s, vst_ref[0, :, pl.ds(p0, SLC_NEAR)]))

    gst = jax.nn.sigmoid(gn_ref[0]).T

    def gate(g, i):
        c0, c1 = 3 * g + i, 3 * (GROUP + g) + i
        return jnp.where(hk == 0, gst[c0:c0 + 1, :], gst[c1:c1 + 1, :])

    for g in range(GROUP):
        cols = slice(g * Q_BLOCK, (g + 1) * Q_BLOCK)
        o = gate(g, 0) * o_cmp[:, cols] + gate(g, 1) * o_slc[:, cols] + gate(g, 2) * o_win[:, cols]
        out_ref[0, 0, :, g * LANES:(g + 1) * LANES] = jnp.where(lmask, o.T, 0.0).astype(out_ref.dtype)


def _nsa_prompt_t(q, gn, kcmp, vcmp, kvb, rel_bias):
    b, t, _ = q.shape
    assert t % SLC_TILE == 0
    nqb = t // Q_BLOCK
    n_chunk = t // CMP_STRIDE
    n_cmp = n_chunk - CMP_RATIO + 1
    n_blocks = t // SEL_BLOCK
    assert n_blocks <= LANES
    cpad = ((0, 0), (CMP_FRONT, LANES - CMP_FRONT), (0, 0))
    kcmp_p = jnp.pad(kcmp, cpad)
    vcmp_p = jnp.pad(vcmp, cpad)
    vcmp_t = vcmp.transpose(0, 2, 1).astype(BF16)
    sm = _sel_map(n_chunk, n_cmp, LANES, n_blocks)
    selmap = jnp.asarray(np.pad(sm, ((CMP_FRONT, LANES - CMP_FRONT), (0, 0))), F32)
    selmap_t = jnp.asarray(sm.T, BF16)
    front = lambda z, n: jnp.pad(z, ((0, 0), (n, 0), (0, 0)))
    ks_p = front(kvb[:, :, 2 * KV_WIDTH:3 * KV_WIDTH], Q_BLOCK)
    vs_t = front(kvb[:, :, 3 * KV_WIDTH:4 * KV_WIDTH], Q_BLOCK).transpose(0, 2, 1)
    kw_p = front(kvb[:, :, 4 * KV_WIDTH:5 * KV_WIDTH], WINDOW)
    vw_t = front(kvb[:, :, 5 * KV_WIDTH:6 * KV_WIDTH], WINDOW).transpose(0, 2, 1)
    pidx = np.arange(t + Q_BLOCK)
    e = (pidx[:, None] // SEL_BLOCK - Q_BLOCK // SEL_BLOCK == np.arange(LANES)[None, :])
    e = jnp.asarray(e.astype(np.float32), BF16)
    full = lambda arr: pl.BlockSpec((1,) + arr.shape[1:], lambda bi, h, i: (bi, 0, 0))
    const = lambda arr: pl.BlockSpec(arr.shape, lambda bi, h, i: (0, 0))
    return pl.pallas_call(
        _nsa_prompt_t_kernel,
        grid=(b, N_KV_HEADS, nqb),
        in_specs=[pl.BlockSpec(memory_space=pltpu.SMEM),
                  pl.BlockSpec((1, Q_BLOCK, NSA_WIDTH), lambda bi, h, i: (bi, i, 0)),
                  pl.BlockSpec((1, Q_BLOCK, LANES), lambda bi, h, i: (bi, i, 0)),
                  full(kcmp_p), full(vcmp_p), full(vcmp_t), const(selmap), const(selmap_t),
                  full(ks_p), full(vs_t), full(kw_p), full(vw_t), const(e)],
        out_specs=pl.BlockSpec((1, 1, Q_BLOCK, NSA_WIDTH), lambda bi, h, i: (h, bi, i, 0)),
        out_shape=jax.ShapeDtypeStruct((N_KV_HEADS, b, t, NSA_WIDTH), BF16),
        scratch_shapes=[pltpu.VMEM((LANES, GROUP * Q_BLOCK), F32),
                        pltpu.VMEM((SLC_NEAR, GROUP * Q_BLOCK), F32),
                        pltpu.VMEM((WIN_KEYS, GROUP * Q_BLOCK), F32)],
        compiler_params=_cparams(("arbitrary", "arbitrary", "arbitrary")),
        name="nsa_prompt",
    )(rel_bias, q, gn, kcmp_p, vcmp_p, vcmp_t, selmap, selmap_t, ks_p, vs_t, kw_p, vw_t, e)


def _nsa_prompt_pair_kernel(rb_ref, q_ref, gn_ref, kcmp_ref, vcmp_ref, vcmpt_ref, selmap_ref, selmapt_ref,
                            ks_ref, vst_ref, kw_ref, vwt_ref, et_ref, out_ref, nbc_ref, nbs_ref, nbw_ref, s_ref):
    qb = pl.program_id(1)
    s0 = qb * Q_BLOCK
    gq = GROUP * Q_BLOCK
    heads = range(N_KV_HEADS)

    @pl.when(qb == 0)
    def _():
        for hk in heads:
            for g in range(GROUP):
                tab = lambda k, h=hk * GROUP + g: rb_ref[k, h]
                cols = slice(g * Q_BLOCK, (g + 1) * Q_BLOCK)
                cl = lax.broadcasted_iota(jnp.int32, (LANES, Q_BLOCK), 0)
                ql = lax.broadcasted_iota(jnp.int32, (LANES, Q_BLOCK), 1)
                dist = ql - CMP_STRIDE * (cl - CMP_FRONT) - (CMP_LEN - 1)
                ok = (dist >= 0) & (cl < CMP_NEAR)
                nbc_ref[hk, :, cols] = jnp.where(ok, _bias_from_scalars(_bucket(dist), tab), NEG_INF)
                kl = lax.broadcasted_iota(jnp.int32, (SLC_NEAR, Q_BLOCK), 0)
                ql = lax.broadcasted_iota(jnp.int32, (SLC_NEAR, Q_BLOCK), 1)
                dist = ql - kl + Q_BLOCK
                nbs_ref[hk, :, cols] = jnp.where(dist >= 0, _bias_from_scalars(_bucket(dist), tab), NEG_INF)
                kl = lax.broadcasted_iota(jnp.int32, (WIN_KEYS, Q_BLOCK), 0)
                ql = lax.broadcasted_iota(jnp.int32, (WIN_KEYS, Q_BLOCK), 1)
                dist = ql - kl + WINDOW
                ok = (dist >= 0) & (dist < WINDOW)
                nbw_ref[hk, :, cols] = jnp.where(ok, _bias_from_scalars(_bucket(dist), tab), NEG_INF)

    lane = lax.broadcasted_iota(jnp.int32, (Q_BLOCK, LANES), 1)
    qblk = q_ref[0].astype(F32)
    init = (jnp.full((1, gq), NEG_INF, F32), jnp.zeros((1, gq), F32), jnp.zeros((LANES, gq), F32))
    p0 = pl.multiple_of(s0, LANES)
    n_cmp_pad = kcmp_ref.shape[1] - LANES
    c8 = qb * (Q_BLOCK // CMP_STRIDE)
    near0 = pl.multiple_of(c8, 8)
    kc_far = kcmp_ref[0, CMP_FRONT:CMP_FRONT + n_cmp_pad, :].astype(BF16)
    kc_near = kcmp_ref[0, pl.ds(near0, LANES), :].astype(BF16)
    vct_near = vcmp_ref[0, pl.ds(near0, LANES), :].T.astype(BF16)
    sm_near_t = selmap_ref[pl.ds(near0, LANES), :].T.astype(BF16)
    kw = kw_ref[0, pl.ds(p0, WIN_KEYS), :]
    vwt = vwt_ref[0, :, pl.ds(p0, WIN_KEYS)]
    crow = lax.broadcasted_iota(jnp.int32, (n_cmp_pad, gq), 0)
    nrow = lax.broadcasted_iota(jnp.int32, (LANES, gq), 0)
    wrow = lax.broadcasted_iota(jnp.int32, (WIN_KEYS, gq), 0)
    krow = lax.broadcasted_iota(jnp.int32, (SLC_NEAR, gq), 0)
    n_sel = selmapt_ref.shape[0]
    blk = lax.broadcasted_iota(jnp.int32, (n_sel, Q_BLOCK), 0)
    cur = (s0 + lax.broadcasted_iota(jnp.int32, (n_sel, Q_BLOCK), 1)) // SEL_BLOCK
    forced = (blk == 0) | (blk == cur) | (blk == cur - 1)
    jcol = lax.broadcasted_iota(jnp.int32, (Q_BLOCK, n_sel), 1)
    far_blocks = (s0 - Q_BLOCK) // SEL_BLOCK

    def split_dot(a, p):
        hi = p.astype(BF16)
        return _dot(a, hi) + _dot(a, (p - hi.astype(F32)).astype(BF16))

    def before_far(hk):
        lmask = (lane // HEAD_DIM) == hk
        qp = jnp.concatenate(
            [jnp.where(lmask, qblk[:, g * LANES:(g + 1) * LANES], 0.0) for g in range(GROUP)], axis=0)
        qp = (qp * (HEAD_DIM ** -0.5)).astype(BF16)
        far_bias = jnp.concatenate(
            [jnp.full((1, Q_BLOCK), rb_ref[N_BUCKETS - 1, hk * GROUP + g], F32) for g in range(GROUP)], axis=1)

        s_far = jnp.where(crow < c8 - CMP_FRONT, _dot_t(kc_far, qp), NEG_INF)
        s_near = jnp.where(nrow >= CMP_FRONT - c8, _dot_t(kc_near, qp) + nbc_ref[hk], NEG_INF)
        m = jnp.maximum(jnp.max(s_far, axis=0, keepdims=True) + far_bias, jnp.max(s_near, axis=0, keepdims=True))
        p_far = jnp.exp(s_far - (m - far_bias))
        p_near = jnp.exp(s_near - m)
        l = jnp.sum(p_far, axis=0, keepdims=True) + jnp.sum(p_near, axis=0, keepdims=True)
        inv = jnp.where(m > 0.5 * NEG_INF, 1.0 / l, 0.0)
        pc_far = p_far * inv
        pc_near = p_near * inv
        o_cmp = _dot(vcmpt_ref[0], pc_far.astype(BF16)) + _dot(vct_near, pc_near.astype(BF16))

        s = jnp.where(wrow >= WINDOW - s0, _dot_t(kw, qp) + nbw_ref[hk], NEG_INF)
        o_win = _col_softmax_finish(_col_softmax_update(init, s, vwt))

        ps_far = sum(pc_far[:, g * Q_BLOCK:(g + 1) * Q_BLOCK] for g in range(GROUP))
        ps_near = sum(pc_near[:, g * Q_BLOCK:(g + 1) * Q_BLOCK] for g in range(GROUP))
        imp_t = split_dot(selmapt_ref[...], ps_far) + split_dot(sm_near_t, ps_near)
        val = jnp.where(blk > cur, NEG_INF, imp_t + FORCE_BONUS * forced.astype(F32))
        sel_t = jnp.zeros((n_sel, Q_BLOCK), F32)
        for _ in range(N_SELECT):
            mx = jnp.max(val, axis=0, keepdims=True)
            first = jnp.min(jnp.where(val == mx, blk, n_sel), axis=0, keepdims=True)
            hit = blk == first
            sel_t = jnp.where(hit, 1.0, sel_t)
            val = jnp.where(hit, -3e38, val)
        sel = sel_t.T > 0.5

        pen_far = jnp.where(sel & (jcol < far_blocks), 0.0, NEG_INF).astype(BF16)
        pen_near = jnp.where(sel, 0.0, NEG_INF).astype(BF16)
        q_far = jnp.concatenate([qp, _tile_rows(pen_far, GROUP)], axis=1)
        q_near = jnp.concatenate([qp, _tile_rows(pen_near, GROUP)], axis=1)
        return dict(far_bias=far_bias, o_cmp=o_cmp, o_win=o_win, q_far=q_far, q_near=q_near)

    pre = [before_far(hk) for hk in heads]
    n_tiles = (ks_ref.shape[1] - Q_BLOCK) // SLC_TILE

    def far_logits(kt):
        f0 = pl.multiple_of(Q_BLOCK + kt * SLC_TILE, LANES)
        k = jnp.concatenate([ks_ref[0, pl.ds(f0, SLC_TILE), :], et_ref[pl.ds(f0, SLC_TILE), :]], axis=1)
        return [_dot_t(k, pre[hk]["q_far"]) for hk in heads]

    def far_pair(i, states):
        for slot in range(2):
            kt = 2 * i + slot
            for hk, s_next in enumerate(far_logits(jnp.minimum(kt + 1, n_tiles - 1))):
                s_ref[1 - slot, hk] = s_next
            f0 = pl.multiple_of(Q_BLOCK + jnp.minimum(kt, n_tiles - 1) * SLC_TILE, LANES)
            vt = vst_ref[0, :, pl.ds(f0, SLC_TILE)]
            states = [_col_softmax_update(states[hk], s_ref[slot, hk], vt) for hk in heads]
        return states

    n_far = (qb + 2) // (SLC_TILE // Q_BLOCK)
    for hk, s_first in enumerate(far_logits(0)):
        s_ref[0, hk] = s_first
    far_states = lax.fori_loop(0, (n_far + 1) // 2, far_pair, [init] * N_KV_HEADS)

    k_near = jnp.concatenate([ks_ref[0, pl.ds(p0, SLC_NEAR), :], et_ref[pl.ds(p0, SLC_NEAR), :]], axis=1)
    vt_near = vst_ref[0, :, pl.ds(p0, SLC_NEAR)]
    gst = jax.nn.sigmoid(gn_ref[0]).T
    outs = []
    for hk in heads:
        m_far, l_far, acc_far = far_states[hk]
        state = (m_far + pre[hk]["far_bias"], l_far, acc_far)
        s = jnp.where(krow >= Q_BLOCK - s0, _dot_t(k_near, pre[hk]["q_near"]) + nbs_ref[hk], NEG_INF)
        o_slc = _col_softmax_finish(_col_softmax_update(state, s, vt_near))
        per_g = []
        for g in range(GROUP):
            cols = slice(g * Q_BLOCK, (g + 1) * Q_BLOCK)
            c = 3 * (hk * GROUP + g)
            o = (gst[c:c + 1, :] * pre[hk]["o_cmp"][:, cols] + gst[c + 1:c + 2, :] * o_slc[:, cols]
                 + gst[c + 2:c + 3, :] * pre[hk]["o_win"][:, cols])
            per_g.append(o.T)
        outs.append(per_g)
    lane_h0 = lane < HEAD_DIM
    for g in range(GROUP):
        out_ref[0, :, g * LANES:(g + 1) * LANES] = jnp.where(lane_h0, outs[0][g], outs[1][g]).astype(out_ref.dtype)


def _nsa_prompt_pair(q, gn, kcmp, vcmp, kvb, rel_bias):
    b, t, _ = q.shape
    assert t % SLC_TILE == 0
    nqb = t // Q_BLOCK
    n_chunk = t // CMP_STRIDE
    n_cmp = n_chunk - CMP_RATIO + 1
    n_blocks = t // SEL_BLOCK
    assert n_blocks <= LANES
    cpad = ((0, 0), (CMP_FRONT, LANES - CMP_FRONT), (0, 0))
    kcmp_p = jnp.pad(kcmp, cpad)
    vcmp_p = jnp.pad(vcmp, cpad)
    vcmp_t = vcmp.transpose(0, 2, 1).astype(BF16)
    sm = _sel_map(n_chunk, n_cmp, LANES, n_blocks)
    selmap = jnp.asarray(np.pad(sm, ((CMP_FRONT, LANES - CMP_FRONT), (0, 0))), F32)
    selmap_t = jnp.asarray(sm.T, BF16)
    front = lambda z, n: jnp.pad(z, ((0, 0), (n, 0), (0, 0)))
    ks_p = front(kvb[:, :, 2 * KV_WIDTH:3 * KV_WIDTH], Q_BLOCK)
    vs_t = front(kvb[:, :, 3 * KV_WIDTH:4 * KV_WIDTH], Q_BLOCK).transpose(0, 2, 1)
    kw_p = front(kvb[:, :, 4 * KV_WIDTH:5 * KV_WIDTH], WINDOW)
    vw_t = front(kvb[:, :, 5 * KV_WIDTH:6 * KV_WIDTH], WINDOW).transpose(0, 2, 1)
    pidx = np.arange(t + Q_BLOCK)
    e = (pidx[:, None] // SEL_BLOCK - Q_BLOCK // SEL_BLOCK == np.arange(LANES)[None, :])
    e = jnp.asarray(e.astype(np.float32), BF16)
    full = lambda arr: pl.BlockSpec((1,) + arr.shape[1:], lambda bi, i: (bi, 0, 0))
    const = lambda arr: pl.BlockSpec(arr.shape, lambda bi, i: (0, 0))
    return pl.pallas_call(
        _nsa_prompt_pair_kernel,
        grid=(b, nqb),
        in_specs=[pl.BlockSpec(memory_space=pltpu.SMEM),
                  pl.BlockSpec((1, Q_BLOCK, NSA_WIDTH), lambda bi, i: (bi, i, 0)),
                  pl.BlockSpec((1, Q_BLOCK, LANES), lambda bi, i: (bi, i, 0)),
                  full(kcmp_p), full(vcmp_p), full(vcmp_t), const(selmap), const(selmap_t),
                  full(ks_p), full(vs_t), full(kw_p), full(vw_t), const(e)],
        out_specs=pl.BlockSpec((1, Q_BLOCK, NSA_WIDTH), lambda bi, i: (bi, i, 0)),
        out_shape=jax.ShapeDtypeStruct((b, t, NSA_WIDTH), BF16),
        scratch_shapes=[pltpu.VMEM((N_KV_HEADS, LANES, GROUP * Q_BLOCK), F32),
                        pltpu.VMEM((N_KV_HEADS, SLC_NEAR, GROUP * Q_BLOCK), F32),
                        pltpu.VMEM((N_KV_HEADS, WIN_KEYS, GROUP * Q_BLOCK), F32),
                        pltpu.VMEM((2, N_KV_HEADS, SLC_TILE, GROUP * Q_BLOCK), F32)],
        compiler_params=_cparams(("arbitrary", "arbitrary")),
        name="nsa_prompt",
    )(rel_bias, q, gn, kcmp_p, vcmp_p, vcmp_t, selmap, selmap_t, ks_p, vs_t, kw_p, vw_t, e)


ROWS_PER_TOKEN = 8


def _nsa_sample_cmp_kernel(q_ref, kcmp_ref, vcmp_ref, selmap_ref, rowtab_ref, ocmp_ref, idx_ref, *,
                           past_len, n_tok, n_cmp, n_blocks):
    rows = n_tok * ROWS_PER_TOKEN
    n_cpad = kcmp_ref.shape[1]
    ridx = lax.broadcasted_iota(jnp.int32, (rows, n_cpad), 0)
    ccol = lax.broadcasted_iota(jnp.int32, (rows, n_cpad), 1)
    dist = past_len + ridx // ROWS_PER_TOKEN - (CMP_STRIDE * ccol + CMP_LEN - 1)
    bucket = _bucket(dist)
    valid = (dist >= 0) & (ccol < n_cmp) & (ridx % ROWS_PER_TOKEN < GROUP)
    n_bpad = selmap_ref.shape[1]
    blk = lax.broadcasted_iota(jnp.int32, (ROWS_PER_TOKEN, n_bpad), 1)
    cur = (past_len + lax.broadcasted_iota(jnp.int32, (ROWS_PER_TOKEN, n_bpad), 0)) // SEL_BLOCK
    forced = (blk == 0) | (blk == cur) | (blk == cur - 1)
    olane = lax.broadcasted_iota(jnp.int32, (ROWS_PER_TOKEN, LANES), 1)
    biases = [_bias_from_rows(bucket, rowtab_ref[hk]) for hk in range(N_KV_HEADS)]
    for bi, hk in [(bi, hk) for bi in range(kcmp_ref.shape[0]) for hk in range(N_KV_HEADS)]:
        kc = kcmp_ref[bi].astype(BF16)
        vc = vcmp_ref[bi].astype(BF16)
        qp = (q_ref[bi, hk].astype(F32) * (HEAD_DIM ** -0.5)).astype(BF16)
        s = _dot_t(qp, kc) + biases[hk]
        s = jnp.where(valid, s, NEG_INF)
        m = jnp.max(s, axis=-1, keepdims=True)
        p = jnp.where(valid, jnp.exp(s - m), 0.0)
        l = jnp.sum(p, axis=-1, keepdims=True)
        pc = p * jnp.where(l > 0.0, 1.0 / jnp.where(l > 0.0, l, 1.0), 0.0)
        ocmp_ref[bi, hk] = _dot(pc.astype(BF16), vc)[:, hk * HEAD_DIM:(hk + 1) * HEAD_DIM]
        ps = jnp.concatenate(
            [jnp.sum(pc[i * ROWS_PER_TOKEN:(i + 1) * ROWS_PER_TOKEN], axis=0, keepdims=True) for i in range(n_tok)]
            + [jnp.zeros((ROWS_PER_TOKEN - n_tok, n_cpad), F32)], axis=0)
        imp = _split_dot(ps, selmap_ref[...])
        val = jnp.where((blk > cur) | (blk >= n_blocks), NEG_INF, imp + FORCE_BONUS * forced.astype(F32))
        picks = jnp.zeros((ROWS_PER_TOKEN, LANES), jnp.int32)
        for n in range(N_SELECT):
            mx = jnp.max(val, axis=-1, keepdims=True)
            first = jnp.min(jnp.where(val == mx, blk, n_bpad), axis=-1, keepdims=True)
            picks = jnp.where(olane == n, first, picks)
            val = jnp.where(blk == first, -3e38, val)
        idx_ref[bi, hk] = picks


def _nsa_sample_cmp(qs, kcmp, vcmp, rowtab, past_len, n_tok, n_cmp, n_blocks):
    nb = qs.shape[0]
    bps = next(d for d in (4, 2, 1) if nb % d == 0)
    n_cpad = kcmp.shape[1]
    n_bpad = -(-n_blocks // LANES) * LANES
    selmap = jnp.asarray(_sel_map(n_cpad, n_cmp, n_bpad, n_blocks), BF16)
    rows = n_tok * ROWS_PER_TOKEN
    return pl.pallas_call(
        functools.partial(_nsa_sample_cmp_kernel, past_len=past_len, n_tok=n_tok, n_cmp=n_cmp, n_blocks=n_blocks),
        grid=(nb // bps,),
        in_specs=[pl.BlockSpec((bps, N_KV_HEADS, rows, LANES), lambda b: (b, 0, 0, 0)),
                  pl.BlockSpec((bps, n_cpad, LANES), lambda b: (b, 0, 0)),
                  pl.BlockSpec((bps, n_cpad, LANES), lambda b: (b, 0, 0)),
                  pl.BlockSpec(selmap.shape, lambda b: (0, 0)),
                  pl.BlockSpec(rowtab.shape, lambda b: (0, 0, 0))],
        out_specs=[pl.BlockSpec((bps, N_KV_HEADS, rows, HEAD_DIM), lambda b: (b, 0, 0, 0)),
                   pl.BlockSpec((bps, N_KV_HEADS, ROWS_PER_TOKEN, LANES), lambda b: (b, 0, 0, 0))],
        out_shape=[jax.ShapeDtypeStruct((nb, N_KV_HEADS, rows, HEAD_DIM), F32),
                   jax.ShapeDtypeStruct((nb, N_KV_HEADS, ROWS_PER_TOKEN, LANES), jnp.int32)],
        compiler_params=_cparams(("arbitrary",)),
        name="nsa_sample_cmp",
    )(qs, kcmp, vcmp, selmap, rowtab)


def _nsa_sample_slc_kernel(idx_ref, pt_ref, ck_ref, cv_ref, q_ref, ocmp_ref, gn_ref, rowtab_ref, newk_ref, newv_ref,
                           swk_ref, swv_ref, nwk_ref, nwv_ref, out_ref, kbuf_ref, vbuf_ref, sem_ref, *,
                           past_len, n_tok, n_cache_blocks, npages):
    b = pl.program_id(0)
    hk = pl.program_id(1)
    tok = pl.program_id(2)
    t = past_len + tok
    step = (b * N_KV_HEADS + hk) * n_tok + tok
    last = pl.num_programs(0) * N_KV_HEADS * n_tok - 1
    slot = step % 2
    base = step * N_SELECT
    pages_per_block = PAGE // SEL_BLOCK

    def block_copies(st, n, sl):
        j = jnp.minimum(idx_ref[st * N_SELECT + n], n_cache_blocks - 1)
        page = pt_ref[(st // (N_KV_HEADS * n_tok)) * npages + j // pages_per_block]
        head = (st // n_tok) % N_KV_HEADS
        return (pltpu.make_async_copy(ck_ref.at[page, head], kbuf_ref.at[sl, n], sem_ref.at[sl]),
                pltpu.make_async_copy(cv_ref.at[page, head], vbuf_ref.at[sl, n], sem_ref.at[sl]))

    @pl.when(step == 0)
    def _():
        for n in range(N_SELECT):
            for c in block_copies(0, n, 0):
                c.start()

    nxt = jnp.minimum(step + 1, last)
    for n in range(N_SELECT):
        for c in block_copies(nxt, n, 1 - slot):
            c.start()
    for n in range(N_SELECT):
        for c in block_copies(step, n, slot):
            c.wait()
    qp = (q_ref[0, 0, 0].astype(F32) * (HEAD_DIM ** -0.5)).astype(BF16)
    rowtab = rowtab_ref[0]
    rvalid = lax.broadcasted_iota(jnp.int32, (ROWS_PER_TOKEN, 1), 0) < GROUP
    init = (jnp.full((ROWS_PER_TOKEN, 1), NEG_INF, F32), jnp.zeros((ROWS_PER_TOKEN, 1), F32),
            jnp.zeros((ROWS_PER_TOKEN, HEAD_DIM), F32))
    halves = PAGE // SEL_BLOCK

    n_keys = N_SELECT * PAGE
    lane = lax.broadcasted_iota(jnp.int32, (ROWS_PER_TOKEN, n_keys), 1)
    row = lane % PAGE
    pos = row
    in_block = jnp.zeros((ROWS_PER_TOKEN, n_keys), jnp.bool_)
    kts, vts = [], []
    for n in range(N_SELECT):
        blk = idx_ref[base + n]
        is_new = blk >= n_cache_blocks
        kts.append(jnp.where(is_new, newk_ref[0, 0], kbuf_ref[slot, n]).astype(BF16))
        vts.append(jnp.where(is_new, newv_ref[0, 0], vbuf_ref[slot, n]).astype(BF16))
        here = lane // PAGE == n
        pos = jnp.where(here, row + (blk // halves) * PAGE, pos)
        in_block = in_block | (here & (row // SEL_BLOCK == blk % halves))
    dist = t - pos
    s = _dot(qp, jnp.concatenate(kts, axis=1)) + _bias_from_rows(_bucket(dist), rowtab)
    mask = (dist >= 0) & in_block & rvalid
    o_slc = _softmax_finish(_softmax_update(init, jnp.where(mask, s, NEG_INF), jnp.concatenate(vts, axis=1),
                                            mask=mask, v_transposed=True))

    n_win = swk_ref.shape[3]
    kw = jnp.concatenate([swk_ref[0, 0], nwk_ref[0, 0]], axis=1).astype(BF16)
    vw = jnp.concatenate([swv_ref[0, 0], nwv_ref[0, 0]], axis=1).astype(BF16)
    wl = n_win + nwk_ref.shape[3]
    wpos = past_len - n_win + lax.broadcasted_iota(jnp.int32, (ROWS_PER_TOKEN, wl), 1)
    wdist = t - wpos
    s = _dot(qp, kw) + _bias_from_rows(_bucket(wdist), rowtab)
    mask = (wdist >= 0) & (wdist < WINDOW) & (wpos >= 0) & rvalid
    o_win = _softmax_finish(_softmax_update(init, jnp.where(mask, s, NEG_INF), vw, mask=mask, v_transposed=True))

    gs = jax.nn.sigmoid(gn_ref[0, 0, 0])
    out_ref[0, 0, 0] = gs[:, 0:1] * ocmp_ref[0, 0, 0] + gs[:, 1:2] * o_slc + gs[:, 2:3] * o_win

    @pl.when(step == last)
    def _():
        for n in range(N_SELECT):
            for c in block_copies(nxt, n, 1 - slot):
                c.wait()


def _nsa_sample_slc(idx, page_table, cache_k, cache_v, q3, ocmp, gn3, rowtab, newk, newv, swk, swv, nwk, nwv,
                    past_len, n_tok):
    nb, npages = page_table.shape
    halves = PAGE // SEL_BLOCK
    n_cache_blocks = npages * halves
    ck = cache_k.transpose(0, 2, 3, 1)
    cv = cache_v.transpose(0, 2, 3, 1)

    tile = lambda w: pl.BlockSpec((1, 1, 1, ROWS_PER_TOKEN, w), lambda b, h, s, i, p: (b, h, s, 0, 0))
    per_bh = lambda arr: pl.BlockSpec((1, 1) + arr.shape[2:], lambda b, h, s, i, p: (b, h, 0, 0))
    grid_spec = pltpu.PrefetchScalarGridSpec(
        num_scalar_prefetch=2,
        grid=(nb, N_KV_HEADS, n_tok),
        in_specs=[pl.BlockSpec(memory_space=pl.ANY), pl.BlockSpec(memory_space=pl.ANY),
                  tile(HEAD_DIM), tile(HEAD_DIM), tile(LANES),
                  pl.BlockSpec((1, ROWS_PER_TOKEN, N_BUCKETS), lambda b, h, s, i, p: (h, 0, 0)),
                  per_bh(newk), per_bh(newv), per_bh(swk), per_bh(swv), per_bh(nwk), per_bh(nwv)],
        out_specs=tile(HEAD_DIM),
        scratch_shapes=[pltpu.VMEM((2, N_SELECT, HEAD_DIM, PAGE), F32),
                        pltpu.VMEM((2, N_SELECT, HEAD_DIM, PAGE), F32),
                        pltpu.SemaphoreType.DMA((2,))],
    )
    return pl.pallas_call(
        functools.partial(_nsa_sample_slc_kernel, past_len=past_len, n_tok=n_tok, n_cache_blocks=n_cache_blocks,
                          npages=npages),
        grid_spec=grid_spec,
        out_shape=jax.ShapeDtypeStruct((nb, N_KV_HEADS, n_tok, ROWS_PER_TOKEN, HEAD_DIM), F32),
        compiler_params=_cparams(("arbitrary", "arbitrary", "arbitrary")),
        name="nsa_sample_slc",
    )(idx.reshape(-1), page_table.reshape(-1), ck, cv, q3, ocmp, gn3, rowtab, newk, newv, swk, swv, nwk, nwv)


def _merge_tail(x, a, c, gates, wn_ref, wc_ref, wo_ref):
    ga = gates[:, :D_MODEL]
    gb = gates[:, D_MODEL:]
    m = (jax.nn.sigmoid(ga) * _dot(a, wn_ref[...])
         + jax.nn.sigmoid(gb) * _dot(c.astype(BF16), wc_ref[...]))
    return x + _dot(m.astype(BF16), wo_ref[...])


def _merge_prompt_kernel(x_ref, a_ref, conv_ref, halo_ref, gate_ref, cw_ref, cb_ref, wn_ref, wc_ref, wo_ref,
                         out_ref, cst_ref, *, tiles_per_batch):
    i = pl.program_id(0)
    first = i % tiles_per_batch == 0
    conv = conv_ref[...]
    bg = conv[:, :CONV_DIM]
    u = conv[:, CONV_DIM:2 * CONV_DIM] * conv[:, 2 * CONV_DIM:]
    halo = halo_ref[...]
    uh = jnp.where(first, 0.0, halo[:, CONV_DIM:2 * CONV_DIM] * halo[:, 2 * CONV_DIM:])
    tm = u.shape[0]
    row = lax.broadcasted_iota(jnp.int32, (tm, CONV_DIM), 0)
    u1 = jnp.where(row == 0, uh[7:8], pltpu.roll(u, 1, axis=0))
    u2 = pltpu.roll(u, 2, axis=0)
    u2 = jnp.where(row == 0, uh[6:7], jnp.where(row == 1, uh[7:8], u2))
    cw = cw_ref[...]
    y = ((cb_ref[...] + cw[0:1] * u2) + cw[1:2] * u1) + cw[2:3] * u
    c = bg * y
    out_ref[...] = _merge_tail(x_ref[...], a_ref[...], c, gate_ref[...], wn_ref, wc_ref, wo_ref)
    cst_ref[0] = u[tm - 8:, :]


def _merge_prompt(x2d, a2, conv3, gates, conv_w, conv_b, wn, wc, wo, t):
    n = x2d.shape[0]
    tm = min(512, t)
    assert t % tm == 0
    tpb = t // tm
    row = lambda w: pl.BlockSpec((tm, w), lambda i: (i, 0))
    const = lambda arr: pl.BlockSpec(arr.shape, lambda i: (0,) * arr.ndim)
    return pl.pallas_call(
        functools.partial(_merge_prompt_kernel, tiles_per_batch=tpb),
        grid=(n // tm,),
        in_specs=[row(D_MODEL),
                  row(NSA_WIDTH),
                  row(3 * CONV_DIM),
                  pl.BlockSpec((8, 3 * CONV_DIM), lambda i: (jnp.maximum(i * (tm // 8) - 1, 0), 0)),
                  row(2 * D_MODEL),
                  const(conv_w), const(conv_b), const(wn), const(wc), const(wo)],
        out_specs=[row(D_MODEL), pl.BlockSpec((1, 8, CONV_DIM), lambda i: (i // tpb, 0, 0))],
        out_shape=[jax.ShapeDtypeStruct((n, D_MODEL), F32),
                   jax.ShapeDtypeStruct((n // t, 8, CONV_DIM), F32)],
        compiler_params=_cparams(("arbitrary",)),
        name="merge_prompt",
    )(x2d, a2, conv3, conv3, gates, conv_w, conv_b, wn, wc, wo)


def _merge_sample_kernel(x_ref, a_ref, conv_ref, past_ref, gate_ref, cw_ref, cb_ref, wn_ref, wc_ref, wo_ref,
                         out_ref, cst_ref, *, n_tok, nb):
    conv = conv_ref[...]
    bg = conv[:, :CONV_DIM]
    u = conv[:, CONV_DIM:2 * CONV_DIM] * conv[:, 2 * CONV_DIM:]
    up = jnp.concatenate([past_ref[...], u], axis=0)
    cw = cw_ref[...]
    y = cb_ref[...]
    for k in range(CONV_WIDTH):
        y = y + cw[k:k + 1] * up[k * nb:(k + n_tok) * nb]
    c = bg * y
    out_ref[...] = _merge_tail(x_ref[...], a_ref[...].astype(BF16), c, gate_ref[...], wn_ref, wc_ref, wo_ref)
    cst_ref[...] = up[n_tok * nb:]


def _merge_sample(x_tm, a_tm, conv_tm, past_tm, gates_tm, conv_w, conv_b, wn, wc, wo, n_tok, nb):
    n = x_tm.shape[0]
    full = lambda arr: pl.BlockSpec(arr.shape, lambda i: (0,) * arr.ndim)
    args = (x_tm, a_tm, conv_tm, past_tm, gates_tm, conv_w, conv_b, wn, wc, wo)
    return pl.pallas_call(
        functools.partial(_merge_sample_kernel, n_tok=n_tok, nb=nb),
        grid=(1,),
        in_specs=[full(a) for a in args],
        out_specs=[pl.BlockSpec((n, D_MODEL), lambda i: (0, 0)),
                   pl.BlockSpec(((CONV_WIDTH - 1) * nb, CONV_DIM), lambda i: (0, 0))],
        out_shape=[jax.ShapeDtypeStruct((n, D_MODEL), F32),
                   jax.ShapeDtypeStruct(((CONV_WIDTH - 1) * nb, CONV_DIM), F32)],
        compiler_params=_cparams(("arbitrary",)),
        name="merge_sample",
    )(*args)


def _rms(x, g):
    return (x * lax.rsqrt(jnp.mean(x * x, axis=-1, keepdims=True) + RMS_EPS)) * g


def _mlp_kernel(x_ref, g_ref, wu_ref, wd_ref, gf_ref, out_ref, h_ref, acc_ref):
    j = pl.program_id(1)

    @pl.when(j == 0)
    def _():
        h_ref[...] = _rms(x_ref[...], g_ref[...]).astype(BF16)

    up = jnp.square(jnp.maximum(_dot(h_ref[...], wu_ref[...]), 0.0)).astype(BF16)
    part = _dot(up, wd_ref[...])

    @pl.when(j == 0)
    def _():
        acc_ref[...] = part

    @pl.when(j > 0)
    def _():
        acc_ref[...] += part

    @pl.when(j == pl.num_programs(1) - 1)
    def _():
        out_ref[...] = _rms(x_ref[...] + acc_ref[...], gf_ref[...])


def _mlp(x2d, g_mlp, w_up, w_down, g_final):
    n = x2d.shape[0]
    tm = min(1024, n)
    tf = 1024
    assert n % tm == 0 and D_FF % tf == 0
    return pl.pallas_call(
        _mlp_kernel,
        grid=(n // tm, D_FF // tf),
        in_specs=[pl.BlockSpec((tm, D_MODEL), lambda i, j: (i, 0)),
                  pl.BlockSpec((1, D_MODEL), lambda i, j: (0, 0)),
                  pl.BlockSpec((D_MODEL, tf), lambda i, j: (0, j)),
                  pl.BlockSpec((tf, D_MODEL), lambda i, j: (j, 0)),
                  pl.BlockSpec((1, D_MODEL), lambda i, j: (0, 0))],
        out_specs=pl.BlockSpec((tm, D_MODEL), lambda i, j: (i, 0)),
        out_shape=jax.ShapeDtypeStruct((n, D_MODEL), F32),
        scratch_shapes=[pltpu.VMEM((tm, D_MODEL), BF16), pltpu.VMEM((tm, D_MODEL), F32)],
        compiler_params=_cparams(("arbitrary", "arbitrary")),
        name="mlp",
    )(x2d, g_mlp, w_up, w_down, g_final)


def _kv_rows(z, lead):
    return z.reshape(lead + (N_KV_HEADS, HEAD_DIM))


def _layer(xp, xs, caches, states, page_table, w, rel_bias):
    (cache_cmp_k, cache_cmp_v, cache_slc_k, cache_slc_v) = caches
    (state_win_k, state_win_v, state_conv) = states
    nbp, t, _ = xp.shape
    nbs, n_tok, _ = xs.shape
    npages = page_table.shape[1]
    past_len = npages * PAGE
    assert t % PAGE == 0 and n_tok <= GROUP and state_win_k.shape[1] >= n_tok

    w_all = _prep_w_in(w["w_in"])
    g_attn = w["g_attn"].reshape(1, D_MODEL)
    wn = w["w_nsa_out"].reshape(N_KV_HEADS, GROUP, HEAD_DIM, D_MODEL).transpose(1, 0, 2, 3)
    wn = wn.reshape(NSA_WIDTH, D_MODEL).astype(BF16)
    wc = w["w_conv_out"].astype(BF16)
    wo = w["w_o"].astype(BF16)
    wu = w["w_up"].astype(BF16)
    wd = w["w_down"].astype(BF16)
    g_mlp = w["g_mlp"].reshape(1, D_MODEL)
    g_final = w["g_final"].reshape(1, D_MODEL)
    conv_w = w["conv_w"]
    conv_b = w["conv_b"].reshape(1, CONV_DIM)
    cmp_k = (w["cmp_pe_k"], w["cmp_w1_k"], w["cmp_w2_k"])
    cmp_v = (w["cmp_pe_v"], w["cmp_w1_v"], w["cmp_w2_v"])

    n_p = nbp * t
    (q, kc, vc, ks, vs, kw, vw, kvb, gn, conv3, gates) = _proj(xp.reshape(n_p, D_MODEL), g_attn, w_all)
    ident = jnp.arange(nbp * (t // PAGE), dtype=jnp.int32).reshape(nbp, t // PAGE)
    kcmp = _compress(kc.reshape(n_p // PAGE, PAGE, KV_WIDTH), 0, ident, *cmp_k)
    vcmp = _compress(vc.reshape(n_p // PAGE, PAGE, KV_WIDTH), 0, ident, *cmp_v)
    a2 = _nsa_prompt_pair(q.reshape(nbp, t, NSA_WIDTH), gn.reshape(nbp, t, LANES), kcmp, vcmp,
                          kvb.reshape(nbp, t, 6 * KV_WIDTH), rel_bias)
    x1, cst = _merge_prompt(xp.reshape(n_p, D_MODEL), a2.reshape(n_p, NSA_WIDTH),
                            conv3, gates, conv_w, conv_b, wn, wc, wo, t)
    yp = _mlp(x1, g_mlp, wu, wd, g_final).reshape(nbp, t, D_MODEL)
    win_p = min(WINDOW, t)
    new_p = tuple(_kv_rows(z, (nbp, t)) for z in (kc, vc, ks, vs)) + (
        _kv_rows(kw, (nbp, t))[:, t - win_p:], _kv_rows(vw, (nbp, t))[:, t - win_p:],
        cst[:, 8 - (CONV_WIDTH - 1):, :])

    n_s = nbs * n_tok
    (qs, kcs, vcs, kss, vss, kws, vws, _, gns, conv3s, gatess) = _proj(xs.reshape(n_s, D_MODEL), g_attn, w_all)
    n_rows = past_len + n_tok
    n_chunk = n_rows // CMP_STRIDE
    assert n_chunk == past_len // CMP_STRIDE, "new rows must not complete a compression chunk"
    n_cmp = n_chunk - CMP_RATIO + 1
    n_blocks = -(-n_rows // SEL_BLOCK)
    kcmp_s = _compress_t(cache_cmp_k, page_table, *cmp_k)
    vcmp_s = _compress_t(cache_cmp_v, page_table, *cmp_v)

    lane = np.arange(LANES)
    q5 = qs.reshape(nbs, n_tok, GROUP, LANES)
    q3 = jnp.stack([jnp.where(jnp.asarray((lane // HEAD_DIM) == h), q5, jnp.zeros_like(q5))
                    for h in range(N_KV_HEADS)], axis=1)
    q3 = jnp.pad(q3, ((0, 0), (0, 0), (0, 0), (0, ROWS_PER_TOKEN - GROUP), (0, 0)))
    tab = rel_bias.T.reshape(N_KV_HEADS, GROUP, N_BUCKETS)
    rowtab3 = jnp.pad(tab, ((0, 0), (0, ROWS_PER_TOKEN - GROUP), (0, 0)))
    rowtab = jnp.tile(rowtab3, (1, n_tok, 1))
    ocmp, idx = _nsa_sample_cmp(q3.reshape(nbs, N_KV_HEADS, n_tok * ROWS_PER_TOKEN, LANES), kcmp_s, vcmp_s,
                                rowtab, past_len, n_tok, n_cmp, n_blocks)
    idx = idx[:, :, :n_tok, :N_SELECT]
    gn3 = gns[:, :3 * N_HEADS].reshape(nbs, n_tok, N_KV_HEADS, GROUP, 3).transpose(0, 2, 1, 3, 4)
    gn3 = jnp.pad(gn3, ((0, 0), (0, 0), (0, 0), (0, ROWS_PER_TOKEN - GROUP), (0, LANES - 3)))
    q3s = jnp.pad(qs.reshape(nbs, n_tok, GROUP, N_KV_HEADS, HEAD_DIM).transpose(0, 3, 1, 2, 4),
                  ((0, 0), (0, 0), (0, 0), (0, ROWS_PER_TOKEN - GROUP), (0, 0)))
    ocmp5 = ocmp.reshape(nbs, N_KV_HEADS, n_tok, ROWS_PER_TOKEN, HEAD_DIM)
    new_t = lambda z: jnp.pad(z.reshape(nbs, n_tok, N_KV_HEADS, HEAD_DIM).transpose(0, 2, 3, 1),
                              ((0, 0), (0, 0), (0, 0), (0, LANES - n_tok)))
    a_s = _nsa_sample_slc(idx, page_table, cache_slc_k, cache_slc_v, q3s, ocmp5, gn3, rowtab3,
                          new_t(kss), new_t(vss),
                          state_win_k.transpose(0, 2, 3, 1), state_win_v.transpose(0, 2, 3, 1),
                          new_t(kws), new_t(vws), past_len, n_tok)
    a_tm = a_s[:, :, :, :GROUP, :].transpose(2, 0, 3, 1, 4).reshape(n_s, NSA_WIDTH)
    to_tm = lambda z: z.reshape(nbs, n_tok, -1).transpose(1, 0, 2).reshape(n_s, -1)
    past_tm = state_conv.transpose(1, 0, 2).reshape((CONV_WIDTH - 1) * nbs, CONV_DIM)
    x1s, csts = _merge_sample(to_tm(xs), a_tm, to_tm(conv3s), past_tm, to_tm(gatess),
                              conv_w, conv_b, wn, wc, wo, n_tok, nbs)
    ys = _mlp(x1s, g_mlp, wu, wd, g_final).reshape(n_tok, nbs, D_MODEL).transpose(1, 0, 2)
    win_buf = state_win_k.shape[1]
    kw_all = jnp.concatenate([state_win_k, _kv_rows(kws, (nbs, n_tok))], axis=1)[:, -win_buf:]
    vw_all = jnp.concatenate([state_win_v, _kv_rows(vws, (nbs, n_tok))], axis=1)[:, -win_buf:]
    new_s = tuple(_kv_rows(z, (nbs, n_tok)) for z in (kcs, vcs, kss, vss)) + (
        kw_all, vw_all, csts.reshape(CONV_WIDTH - 1, nbs, CONV_DIM).transpose(1, 0, 2))
    return yp, ys, new_p, new_s


def kernel(x_prompt, x_sample, cache_cmp_k, cache_cmp_v, cache_slc_k, cache_slc_v, state_win_k, state_win_v,
           state_conv, page_table, g_attn, w_in, cmp_pe_k, cmp_w1_k, cmp_w2_k, cmp_pe_v, cmp_w1_v, cmp_w2_v,
           conv_w, conv_b, w_nsa_out, w_conv_out, w_o, g_mlp, w_up, w_down, rel_bias, g_final):
    depth = w_in.shape[0]
    assert depth == 1, "the final norm is fused into the last layer's MLP kernel"
    names = ("g_attn", "w_in", "cmp_pe_k", "cmp_w1_k", "cmp_w2_k", "cmp_pe_v", "cmp_w1_v", "cmp_w2_v",
             "conv_w", "conv_b", "w_nsa_out", "w_conv_out", "w_o", "g_mlp", "w_up", "w_down")
    vals = (g_attn, w_in, cmp_pe_k, cmp_w1_k, cmp_w2_k, cmp_pe_v, cmp_w1_v, cmp_w2_v,
            conv_w, conv_b, w_nsa_out, w_conv_out, w_o, g_mlp, w_up, w_down)
    l = 0
    w = {k: v[l] for k, v in zip(names, vals)}
    w["g_final"] = g_final
    yp, ys, new_p, new_s = _layer(
        x_prompt, x_sample,
        (cache_cmp_k[l], cache_cmp_v[l], cache_slc_k[l], cache_slc_v[l]),
        (state_win_k[l], state_win_v[l], state_conv[l]), page_table, w, rel_bias)
    return (yp, ys) + tuple(z[None] for z in new_p) + tuple(z[None] for z in new_s)
```

```python
import functools
import math

import numpy as np
import jax
import jax.numpy as jnp
from jax import lax
from jax.experimental import pallas as pl
from jax.experimental.pallas import tpu as pltpu

F32 = jnp.float32
BF16 = jnp.bfloat16

D_MODEL = 1024
N_HEADS = 8
HEAD_DIM = 64
N_KV_HEADS = 2
GROUP = N_HEADS // N_KV_HEADS
KV_WIDTH = N_KV_HEADS * HEAD_DIM
NSA_WIDTH = N_HEADS * HEAD_DIM
CMP_LEN = 32
CMP_STRIDE = 16
CMP_RATIO = CMP_LEN // CMP_STRIDE
CMP_HIDDEN = 256
SEL_BLOCK = 64
N_SELECT = 16
WINDOW = 512
CONV_DIM = 512
CONV_WIDTH = 3
D_FF = 4 * D_MODEL
N_BUCKETS = 32
MAX_DISTANCE = 128
Q_BLOCK = 128
PAGE = 128
RMS_EPS = 1e-6
NEG_INF = -1e30
FORCE_BONUS = 1e4
LANES = 128
VMEM_LIMIT = 56 * 1024 * 1024


def _bucket_thresholds():
    max_exact = N_BUCKETS // 2
    n = np.arange(max_exact, 4 * MAX_DISTANCE).astype(np.float32)
    large = max_exact + (np.log(n / np.float32(max_exact)) / np.float32(math.log(MAX_DISTANCE / max_exact))
                         * np.float32(N_BUCKETS - max_exact)).astype(np.int32)
    large = np.minimum(large, N_BUCKETS - 1)
    thr = []
    for k in range(max_exact + 1, N_BUCKETS):
        thr.append(int(np.arange(max_exact, 4 * MAX_DISTANCE)[np.argmax(large >= k)]))
    return tuple(thr)


BUCKET_THR = _bucket_thresholds()


def _bucket(dist):
    n = jnp.maximum(dist, 0)
    big = jnp.full(n.shape, N_BUCKETS // 2, jnp.int32)
    for thr in BUCKET_THR:
        big = big + (n >= thr).astype(jnp.int32)
    return jnp.where(n < N_BUCKETS // 2, n, big)


def _bias_from_scalars(bucket, table_fn):
    out = jnp.zeros(bucket.shape, F32)
    for k in range(N_BUCKETS):
        out = jnp.where(bucket == k, table_fn(k), out)
    return out


def _bias_from_rows(bucket, rowtab):
    out = jnp.zeros(bucket.shape, F32)
    for k in range(N_BUCKETS):
        out = jnp.where(bucket == k, rowtab[:, k:k + 1], out)
    return out


def _dot(a, b):
    return jnp.dot(a, b, preferred_element_type=F32)


def _dot_t(a, b):
    return lax.dot_general(a, b, (((1,), (1,)), ((), ())), preferred_element_type=F32)


def _split_dot(p, m):
    hi = p.astype(BF16)
    lo = (p - hi.astype(F32)).astype(BF16)
    return _dot(hi, m) + _dot(lo, m)


def _cparams(sem):
    return pltpu.CompilerParams(dimension_semantics=sem, vmem_limit_bytes=VMEM_LIMIT)


PROJ_WIDTHS = (NSA_WIDTH, 6 * KV_WIDTH, LANES, 3 * CONV_DIM, 2 * D_MODEL)
PROJ_COLS = sum(PROJ_WIDTHS)


def _proj_kernel(x_ref, g_ref, w_ref, *out_refs, head_rows):
    q_ref, kvb_ref, gn_ref, conv_ref, gate_ref = out_refs[0], *out_refs[-4:]
    kv_refs = out_refs[1:-4]
    x = x_ref[...]
    y = x * lax.rsqrt(jnp.mean(x * x, axis=-1, keepdims=True) + RMS_EPS)
    hb = (y * g_ref[...]).astype(BF16)
    q_ref[...] = _dot(hb, w_ref[:, 0:NSA_WIDTH]).astype(q_ref.dtype)
    off = NSA_WIDTH
    kv = _dot(hb, w_ref[:, off:off + 6 * KV_WIDTH])
    tm = kv.shape[0]
    for i in range(6):
        z = kv[:, i * KV_WIDTH:(i + 1) * KV_WIDTH]
        if head_rows:
            kv_refs[2 + i][0] = z.T.reshape(N_KV_HEADS, HEAD_DIM, tm)
        if not head_rows or i < 2:
            kv_refs[i][...] = z
    kvb_ref[...] = kv.astype(BF16)
    off += 6 * KV_WIDTH
    gn_ref[...] = _dot(hb, w_ref[:, off:off + LANES])
    off += LANES
    conv_ref[...] = _dot(hb, w_ref[:, off:off + 3 * CONV_DIM])
    off += 3 * CONV_DIM
    gate_ref[...] = _dot(hb, w_ref[:, off:off + 2 * D_MODEL])


def _proj(x2d, g, w_all, head_rows, t=None):
    n = x2d.shape[0]
    tm = min(512, n if t is None else t)
    assert n % tm == 0
    row = lambda w: pl.BlockSpec((tm, w), lambda i: (i, 0))
    dense = lambda w, dt: (jax.ShapeDtypeStruct((n, w), dt), row(w))
    if head_rows:
        tpb = t // tm
        kv_outs = [dense(KV_WIDTH, F32)] * 2 + [
            (jax.ShapeDtypeStruct((n // t, N_KV_HEADS, HEAD_DIM, t), F32),
             pl.BlockSpec((1, N_KV_HEADS, HEAD_DIM, tm), lambda i: (i // tpb, 0, 0, i % tpb)))] * 6
    else:
        kv_outs = [dense(KV_WIDTH, F32)] * 6
    outs = ([dense(NSA_WIDTH, BF16)] + kv_outs
            + [dense(6 * KV_WIDTH, BF16), dense(LANES, F32), dense(3 * CONV_DIM, F32), dense(2 * D_MODEL, F32)])
    return pl.pallas_call(
        functools.partial(_proj_kernel, head_rows=head_rows),
        grid=(n // tm,),
        in_specs=[row(D_MODEL),
                  pl.BlockSpec((1, D_MODEL), lambda i: (0, 0)),
                  pl.BlockSpec((D_MODEL, PROJ_COLS), lambda i: (0, 0))],
        out_specs=[o[1] for o in outs],
        out_shape=[o[0] for o in outs],
        compiler_params=_cparams(("arbitrary",)),
        name="proj",
    )(x2d, g, w_all)


def _prep_w_in(w_in):
    sizes = (NSA_WIDTH,) + (KV_WIDTH,) * 6 + (3 * N_HEADS, CONV_DIM, CONV_DIM, CONV_DIM, D_MODEL, D_MODEL)
    offs = np.concatenate([[0], np.cumsum(sizes)])
    wq = w_in[:, :NSA_WIDTH].reshape(D_MODEL, N_KV_HEADS, GROUP, HEAD_DIM)
    wq = wq.transpose(0, 2, 1, 3).reshape(D_MODEL, NSA_WIDTH)
    wkv = w_in[:, offs[1]:offs[7]]
    wgn = jnp.pad(w_in[:, offs[7]:offs[8]], ((0, 0), (0, LANES - 3 * N_HEADS)))
    wrest = w_in[:, offs[8]:]
    return jnp.concatenate([wq, wkv, wgn, wrest], axis=1).astype(BF16)


CHUNKS_PER_PAGE = PAGE // CMP_STRIDE


def _compress_kernel(pt_ref, *refs, pp):
    del pt_ref
    page_refs = refs[:pp + 1]
    wbd_ref, pe_ref, w1_ref, w2_ref, out_ref, acc_ref, pehid_ref = refs[pp + 1:]

    @pl.when((pl.program_id(0) == 0) & (pl.program_id(1) == 0))
    def _():
        pehid_ref[...] = _dot(pe_ref[...], w1_ref[...])

    cols = []
    for r in range(CMP_STRIDE):
        xr = jnp.concatenate([p[0, pl.ds(r, CHUNKS_PER_PAGE, stride=CMP_STRIDE), :] for p in page_refs], axis=0)
        cols.append(xr.astype(BF16))
    x = jnp.concatenate(cols, axis=1)
    acc_ref[...] = _dot(x, wbd_ref[...])
    nc = CHUNKS_PER_PAGE * pp
    pe = pehid_ref[0:1, :]
    hids = []
    for h in range(N_KV_HEADS):
        base = h * CMP_RATIO * CMP_HIDDEN
        p0 = acc_ref[0:nc, base:base + CMP_HIDDEN]
        p1 = acc_ref[pl.ds(1, nc), base + CMP_HIDDEN:base + 2 * CMP_HIDDEN]
        hids.append(jax.nn.gelu((pe + p0) + p1).astype(BF16))
    out_ref[0] = _dot(jnp.concatenate(hids, axis=1), w2_ref[...])


def _compress(pages, colblk, page_table, pe, w1, w2):
    nb, npages = page_table.shape
    pp = min(32, npages)
    assert npages % pp == 0
    nc = CHUNKS_PER_PAGE * pp
    w1r = w1.reshape(CMP_RATIO, CMP_STRIDE, HEAD_DIM, CMP_HIDDEN)
    blk = w1r.transpose(1, 2, 0, 3).reshape(CMP_STRIDE, HEAD_DIM, CMP_RATIO * CMP_HIDDEN)
    z = jnp.zeros_like(blk)
    wbd = jnp.concatenate([jnp.concatenate([blk, z], axis=2), jnp.concatenate([z, blk], axis=2)], axis=1)
    wbd = wbd.reshape(CMP_STRIDE * KV_WIDTH, N_KV_HEADS * CMP_RATIO * CMP_HIDDEN).astype(BF16)
    z2 = jnp.zeros_like(w2)
    w2bd = jnp.concatenate([jnp.concatenate([w2, z2], axis=1), jnp.concatenate([z2, w2], axis=1)], axis=0).astype(BF16)
    pe8 = jnp.broadcast_to(pe.reshape(1, CMP_LEN * HEAD_DIM), (8, CMP_LEN * HEAD_DIM)).astype(BF16)

    def page_spec(p):
        def imap(b, s, pt):
            return (pt[b, jnp.minimum(s * pp + p, npages - 1)], 0, colblk)
        return pl.BlockSpec((1, PAGE, KV_WIDTH), imap)

    const = lambda shape: pl.BlockSpec(shape, lambda b, s, pt: (0,) * len(shape))
    grid_spec = pltpu.PrefetchScalarGridSpec(
        num_scalar_prefetch=1,
        grid=(nb, npages // pp),
        in_specs=[page_spec(p) for p in range(pp + 1)]
        + [const(wbd.shape), const(pe8.shape), const((CMP_LEN * HEAD_DIM, CMP_HIDDEN)), const(w2bd.shape)],
        out_specs=pl.BlockSpec((1, nc, KV_WIDTH), lambda b, s, pt: (b, s, 0)),
        scratch_shapes=[pltpu.VMEM((nc + 8, N_KV_HEADS * CMP_RATIO * CMP_HIDDEN), F32),
                        pltpu.VMEM((8, CMP_HIDDEN), F32)],
    )
    return pl.pallas_call(
        functools.partial(_compress_kernel, pp=pp),
        grid_spec=grid_spec,
        out_shape=jax.ShapeDtypeStruct((nb, npages * CHUNKS_PER_PAGE, KV_WIDTH), F32),
        compiler_params=_cparams(("arbitrary", "arbitrary")),
        name="compress",
    )(page_table, *([pages] * (pp + 1)), wbd, pe8, w1.astype(BF16), w2bd)


def _compress_t_kernel(pt_ref, *refs, pp, npages):
    (cache_ref, w1p_ref, pe_ref, w1_ref, w2_ref, out_ref, pbuf_ref, sem_ref, xt_ref, acc_ref, pehid_ref) = refs
    b = pl.program_id(0)
    s = pl.program_id(1)
    n_steps = pl.num_programs(1)
    step = b * n_steps + s
    last = pl.num_programs(0) * n_steps - 1
    slot = step % 2

    def page_copy(bb, ss, p, sl):
        page = pt_ref[bb, jnp.minimum(ss * pp + p, npages - 1)]
        return pltpu.make_async_copy(cache_ref.at[page], pbuf_ref.at[sl, p], sem_ref.at[sl])

    @pl.when(step == 0)
    def _():
        pehid_ref[...] = _dot(pe_ref[...], w1_ref[...])
        for p in range(pp + 1):
            page_copy(0, 0, p, 0).start()

    wrap = s + 1 == n_steps
    nxt_b = jnp.minimum(jnp.where(wrap, b + 1, b), pl.num_programs(0) - 1)
    nxt_s = jnp.where(step == last, s, jnp.where(wrap, 0, s + 1))
    for p in range(pp + 1):
        page_copy(nxt_b, nxt_s, p, 1 - slot).start()
    for p in range(pp + 1):
        page_copy(b, s, p, slot).wait()

    for p in range(pp + 1):
        page = pbuf_ref[slot, p].reshape(KV_WIDTH, PAGE)
        xt_ref[p * PAGE:(p + 1) * PAGE, :] = page.T
    nc = CHUNKS_PER_PAGE * pp
    m = nc + CHUNKS_PER_PAGE
    pe = pehid_ref[0:1, :]
    first_head = lax.broadcasted_iota(jnp.int32, (m, KV_WIDTH), 1) < HEAD_DIM
    xs = [[], []]
    for r in range(0, CMP_STRIDE, 2):
        a = xt_ref[pl.ds(r, m, stride=CMP_STRIDE), :]
        c = xt_ref[pl.ds(r + 1, m, stride=CMP_STRIDE), :]
        xs[0].append(jnp.where(first_head, a, pltpu.roll(c, HEAD_DIM, axis=1)).astype(BF16))
        xs[1].append(jnp.where(first_head, pltpu.roll(a, HEAD_DIM, axis=1), c).astype(BF16))
    hids = []
    for h in range(N_KV_HEADS):
        x = jnp.concatenate(xs[h], axis=1)
        acc_ref[h] = _dot(x, w1p_ref[...])
        p0 = acc_ref[h, 0:nc, 0:CMP_HIDDEN]
        p1 = acc_ref[h, pl.ds(1, nc), CMP_HIDDEN:2 * CMP_HIDDEN]
        hids.append(jax.nn.gelu((pe + p0) + p1).astype(BF16))
    out_ref[0] = _dot(jnp.concatenate(hids, axis=1), w2_ref[...])

    @pl.when(step == last)
    def _():
        for p in range(pp + 1):
            page_copy(nxt_b, nxt_s, p, 1 - slot).wait()


def _compress_t(cache, page_table, pe, w1, w2):
    nb, npages = page_table.shape
    pp = min(32, npages)
    assert npages % pp == 0
    nc = CHUNKS_PER_PAGE * pp
    pages = cache.transpose(0, 2, 3, 1)
    w1p = w1.reshape(CMP_RATIO, CMP_STRIDE * HEAD_DIM, CMP_HIDDEN).transpose(1, 0, 2)
    w1p = w1p.reshape(CMP_STRIDE * HEAD_DIM, CMP_RATIO * CMP_HIDDEN).astype(BF16)
    z2 = jnp.zeros_like(w2)
    w2bd = jnp.concatenate([jnp.concatenate([w2, z2], axis=1), jnp.concatenate([z2, w2], axis=1)], axis=0).astype(BF16)
    pe8 = jnp.broadcast_to(pe.reshape(1, CMP_LEN * HEAD_DIM), (8, CMP_LEN * HEAD_DIM)).astype(BF16)

    const = lambda shape: pl.BlockSpec(shape, lambda b, s, pt: (0,) * len(shape))
    grid_spec = pltpu.PrefetchScalarGridSpec(
        num_scalar_prefetch=1,
        grid=(nb, npages // pp),
        in_specs=[pl.BlockSpec(memory_space=pl.ANY),
                  const(w1p.shape), const(pe8.shape), const((CMP_LEN * HEAD_DIM, CMP_HIDDEN)), const(w2bd.shape)],
        out_specs=pl.BlockSpec((1, nc, KV_WIDTH), lambda b, s, pt: (b, s, 0)),
        scratch_shapes=[pltpu.VMEM((2, pp + 1, N_KV_HEADS, HEAD_DIM, PAGE), F32),
                        pltpu.SemaphoreType.DMA((2,)),
                        pltpu.VMEM(((pp + 1) * PAGE, KV_WIDTH), F32),
                        pltpu.VMEM((N_KV_HEADS, nc + CHUNKS_PER_PAGE, CMP_RATIO * CMP_HIDDEN), F32),
                        pltpu.VMEM((8, CMP_HIDDEN), F32)],
    )
    return pl.pallas_call(
        functools.partial(_compress_t_kernel, pp=pp, npages=npages),
        grid_spec=grid_spec,
        out_shape=jax.ShapeDtypeStruct((nb, npages * CHUNKS_PER_PAGE, KV_WIDTH), F32),
        compiler_params=_cparams(("arbitrary", "arbitrary")),
        name="compress_t",
    )(page_table, pages, w1p, pe8, w1.astype(BF16), w2bd)


def _sel_map(n_rows, n_cmp, n_cols, n_blocks):
    c0 = np.arange(n_rows)[:, None] * CMP_STRIDE
    s0 = np.arange(n_cols)[None, :] * SEL_BLOCK
    ov = np.minimum(c0 + CMP_LEN, s0 + SEL_BLOCK) - np.maximum(c0, s0)
    m = np.clip(ov, 0, None).astype(np.float32) / CMP_LEN
    m[n_cmp:] = 0.0
    m[:, n_blocks:] = 0.0
    return m


CMP_FRONT = 16
CMP_NEAR = 24
SLC_NEAR = 2 * Q_BLOCK
SLC_TILE = 512
WIN_KEYS = WINDOW + Q_BLOCK


def _softmax_update(state, s, v, mask=None, v_transposed=False):
    m_old, l_old, acc = state
    m_new = jnp.maximum(m_old, jnp.max(s, axis=-1, keepdims=True))
    p = jnp.exp(s - m_new)
    if mask is not None:
        p = jnp.where(mask, p, 0.0)
    alpha = jnp.exp(m_old - m_new)
    l_new = alpha * l_old + jnp.sum(p, axis=-1, keepdims=True)
    pv = _dot_t(p.astype(BF16), v) if v_transposed else _dot(p.astype(BF16), v)
    return m_new, l_new, alpha * acc + pv


def _softmax_finish(state):
    _, l, acc = state
    return jnp.where(l > 0.0, acc / jnp.where(l > 0.0, l, 1.0), 0.0)


def _tile_rows(x, n):
    return jnp.concatenate([x] * n, axis=0)


def _nsa_prompt_kernel(rb_ref, q_ref, gn_ref, kcmp_ref, vcmp_ref, selmap_ref, ks_ref, vs_ref, kw_ref, vw_ref,
                       et_ref, out_ref, nbc_ref, nbs_ref, nbw_ref):
    hk = pl.program_id(1)
    qb = pl.program_id(2)
    s0 = qb * Q_BLOCK
    gq = GROUP * Q_BLOCK

    @pl.when(qb == 0)
    def _():
        for g in range(GROUP):
            tab = lambda k, g=g: rb_ref[k, hk * GROUP + g]
            ql = lax.broadcasted_iota(jnp.int32, (Q_BLOCK, LANES), 0)
            cl = lax.broadcasted_iota(jnp.int32, (Q_BLOCK, LANES), 1)
            dist = ql - CMP_STRIDE * (cl - CMP_FRONT) - (CMP_LEN - 1)
            ok = (dist >= 0) & (cl < CMP_NEAR)
            nbc_ref[g] = jnp.where(ok, _bias_from_scalars(_bucket(dist), tab), NEG_INF)
            ql = lax.broadcasted_iota(jnp.int32, (Q_BLOCK, SLC_NEAR), 0)
            kl = lax.broadcasted_iota(jnp.int32, (Q_BLOCK, SLC_NEAR), 1)
            dist = ql - kl + Q_BLOCK
            nbs_ref[g] = jnp.where(dist >= 0, _bias_from_scalars(_bucket(dist), tab), NEG_INF)
            ql = lax.broadcasted_iota(jnp.int32, (Q_BLOCK, WIN_KEYS), 0)
            kl = lax.broadcasted_iota(jnp.int32, (Q_BLOCK, WIN_KEYS), 1)
            dist = ql - kl + WINDOW
            ok = (dist >= 0) & (dist < WINDOW)
            nbw_ref[g] = jnp.where(ok, _bias_from_scalars(_bucket(dist), tab), NEG_INF)

    lane = lax.broadcasted_iota(jnp.int32, (Q_BLOCK, LANES), 1)
    lmask = (lane >= hk * HEAD_DIM) & (lane < (hk + 1) * HEAD_DIM)
    qblk = q_ref[0].astype(F32)
    qp = jnp.concatenate(
        [jnp.where(lmask, qblk[:, g * LANES:(g + 1) * LANES], 0.0) for g in range(GROUP)], axis=0)
    qp = (qp * (HEAD_DIM ** -0.5)).astype(BF16)

    far_bias = jnp.concatenate(
        [jnp.full((Q_BLOCK, 1), rb_ref[N_BUCKETS - 1, hk * GROUP + g], F32) for g in range(GROUP)], axis=0)

    n_cmp_pad = kcmp_ref.shape[1] - CMP_FRONT - (LANES - CMP_FRONT)
    kc_far = kcmp_ref[0, CMP_FRONT:CMP_FRONT + n_cmp_pad, :].astype(BF16)
    vc_far = vcmp_ref[0, CMP_FRONT:CMP_FRONT + n_cmp_pad, :].astype(BF16)
    near0 = pl.multiple_of(qb * (Q_BLOCK // CMP_STRIDE), 8)
    kc_near = kcmp_ref[0, pl.ds(near0, LANES), :].astype(BF16)
    vc_near = vcmp_ref[0, pl.ds(near0, LANES), :].astype(BF16)
    c8 = qb * (Q_BLOCK // CMP_STRIDE)
    ccol = lax.broadcasted_iota(jnp.int32, (1, n_cmp_pad), 1)
    s_far = _dot_t(qp, kc_far) + jnp.where(ccol < c8 - CMP_FRONT, 0.0, NEG_INF)
    ncol = lax.broadcasted_iota(jnp.int32, (1, LANES), 1)
    s_near = _dot_t(qp, kc_near) + (nbc_ref[...].reshape(gq, LANES)
                                    + jnp.where(ncol >= CMP_FRONT - c8, 0.0, NEG_INF))
    m = jnp.maximum(jnp.max(s_far, axis=-1, keepdims=True) + far_bias, jnp.max(s_near, axis=-1, keepdims=True))
    p_far = jnp.exp(s_far - (m - far_bias))
    p_near = jnp.exp(s_near - m)
    l = jnp.sum(p_far, axis=-1, keepdims=True) + jnp.sum(p_near, axis=-1, keepdims=True)
    inv = jnp.where(m > 0.5 * NEG_INF, 1.0 / l, 0.0)
    pc_far = p_far * inv
    pc_near = p_near * inv
    o_cmp = _dot(pc_far.astype(BF16), vc_far) + _dot(pc_near.astype(BF16), vc_near)

    init = (jnp.full((gq, 1), NEG_INF, F32), jnp.zeros((gq, 1), F32), jnp.zeros((gq, LANES), F32))
    p0 = pl.multiple_of(s0, LANES)
    wcol = lax.broadcasted_iota(jnp.int32, (1, WIN_KEYS), 1)
    s = _dot_t(qp, kw_ref[0, pl.ds(p0, WIN_KEYS), :]) + (nbw_ref[...].reshape(gq, WIN_KEYS)
                                                         + jnp.where(wcol >= WINDOW - s0, 0.0, NEG_INF))
    o_win = _softmax_finish(_softmax_update(init, s, vw_ref[0, pl.ds(p0, WIN_KEYS), :]))

    ps_far = sum(pc_far[g * Q_BLOCK:(g + 1) * Q_BLOCK] for g in range(GROUP))
    ps_near = sum(pc_near[g * Q_BLOCK:(g + 1) * Q_BLOCK] for g in range(GROUP))
    sm_far = selmap_ref[CMP_FRONT:CMP_FRONT + n_cmp_pad, :].astype(BF16)
    sm_near = selmap_ref[pl.ds(near0, LANES), :].astype(BF16)
    imp = _split_dot(ps_far, sm_far) + _split_dot(ps_near, sm_near)
    imp_t = imp.T
    n_sel = imp_t.shape[0]
    blk = lax.broadcasted_iota(jnp.int32, (n_sel, Q_BLOCK), 0)
    cur = (s0 + lax.broadcasted_iota(jnp.int32, (n_sel, Q_BLOCK), 1)) // SEL_BLOCK
    forced = (blk == 0) | (blk == cur) | (blk == cur - 1)
    val = jnp.where(blk > cur, NEG_INF, imp_t + FORCE_BONUS * forced.astype(F32))
    sel_t = jnp.zeros((n_sel, Q_BLOCK), F32)
    for _ in range(N_SELECT):
        mx = jnp.max(val, axis=0, keepdims=True)
        first = jnp.min(jnp.where(val == mx, blk, n_sel), axis=0, keepdims=True)
        hit = blk == first
        sel_t = jnp.where(hit, 1.0, sel_t)
        val = jnp.where(hit, -3e38, val)
    sel = sel_t.T > 0.5

    jcol = lax.broadcasted_iota(jnp.int32, (Q_BLOCK, n_sel), 1)
    far_blocks = (s0 - Q_BLOCK) // SEL_BLOCK
    pen_far = jnp.where(sel & (jcol < far_blocks), 0.0, NEG_INF).astype(BF16)
    pen_near = jnp.where(sel, 0.0, NEG_INF).astype(BF16)
    q_far = jnp.concatenate([qp, _tile_rows(pen_far, GROUP)], axis=1)
    q_near = jnp.concatenate([qp, _tile_rows(pen_near, GROUP)], axis=1)

    def far_step(kt, state):
        p0 = pl.multiple_of(Q_BLOCK + kt * SLC_TILE, LANES)
        k = jnp.concatenate([ks_ref[0, pl.ds(p0, SLC_TILE), :], et_ref[pl.ds(p0, SLC_TILE), :]], axis=1)
        return _softmax_update(state, _dot_t(q_far, k), vs_ref[0, pl.ds(p0, SLC_TILE), :])

    n_far = (qb + 2) // (SLC_TILE // Q_BLOCK)
    m_far, l_far, acc_far = lax.fori_loop(0, n_far, far_step, init)
    state = (m_far + far_bias, l_far, acc_far)
    k = jnp.concatenate([ks_ref[0, pl.ds(p0, SLC_NEAR), :], et_ref[pl.ds(p0, SLC_NEAR), :]], axis=1)
    kcol = lax.broadcasted_iota(jnp.int32, (1, SLC_NEAR), 1)
    s = _dot_t(q_near, k) + (nbs_ref[...].reshape(gq, SLC_NEAR) + jnp.where(kcol >= Q_BLOCK - s0, 0.0, NEG_INF))
    o_slc = _softmax_finish(_softmax_update(state, s, vs_ref[0, pl.ds(p0, SLC_NEAR), :]))

    gs = jax.nn.sigmoid(gn_ref[0])

    def gate(g, i):
        c0, c1 = 3 * g + i, 3 * (GROUP + g) + i
        return jnp.where(hk == 0, gs[:, c0:c0 + 1], gs[:, c1:c1 + 1])

    for g in range(GROUP):
        rows = slice(g * Q_BLOCK, (g + 1) * Q_BLOCK)
        o = gate(g, 0) * o_cmp[rows] + gate(g, 1) * o_slc[rows] + gate(g, 2) * o_win[rows]
        out_ref[0, 0, :, g * LANES:(g + 1) * LANES] = jnp.where(lmask, o, 0.0).astype(out_ref.dtype)


def _nsa_prompt(q, gn, kcmp, vcmp, kvb, rel_bias):
    b, t, _ = q.shape
    assert t % SLC_TILE == 0
    nqb = t // Q_BLOCK
    n_chunk = t // CMP_STRIDE
    n_cmp = n_chunk - CMP_RATIO + 1
    n_blocks = t // SEL_BLOCK
    assert n_blocks <= LANES
    cpad = ((0, 0), (CMP_FRONT, LANES - CMP_FRONT), (0, 0))
    kcmp_p = jnp.pad(kcmp, cpad)
    vcmp_p = jnp.pad(vcmp, cpad)
    selmap = jnp.asarray(np.pad(_sel_map(n_chunk, n_cmp, LANES, n_blocks),
                                ((CMP_FRONT, LANES - CMP_FRONT), (0, 0))), F32)
    ks_p = jnp.pad(kvb[:, :, 2 * KV_WIDTH:3 * KV_WIDTH], ((0, 0), (Q_BLOCK, 0), (0, 0)))
    vs_p = jnp.pad(kvb[:, :, 3 * KV_WIDTH:4 * KV_WIDTH], ((0, 0), (Q_BLOCK, 0), (0, 0)))
    kw_p = jnp.pad(kvb[:, :, 4 * KV_WIDTH:5 * KV_WIDTH], ((0, 0), (WINDOW, 0), (0, 0)))
    vw_p = jnp.pad(kvb[:, :, 5 * KV_WIDTH:6 * KV_WIDTH], ((0, 0), (WINDOW, 0), (0, 0)))
    pidx = np.arange(t + Q_BLOCK)
    e = (pidx[:, None] // SEL_BLOCK - Q_BLOCK // SEL_BLOCK == np.arange(LANES)[None, :])
    e = jnp.asarray(e.astype(np.float32), BF16)
    full = lambda arr: pl.BlockSpec((1,) + arr.shape[1:], lambda bi, h, i: (bi, 0, 0))
    return pl.pallas_call(
        _nsa_prompt_kernel,
        grid=(b, N_KV_HEADS, nqb),
        in_specs=[pl.BlockSpec(memory_space=pltpu.SMEM),
                  pl.BlockSpec((1, Q_BLOCK, NSA_WIDTH), lambda bi, h, i: (bi, i, 0)),
                  pl.BlockSpec((1, Q_BLOCK, LANES), lambda bi, h, i: (bi, i, 0)),
                  full(kcmp_p), full(vcmp_p),
                  pl.BlockSpec(selmap.shape, lambda bi, h, i: (0, 0)),
                  full(ks_p), full(vs_p), full(kw_p), full(vw_p),
                  pl.BlockSpec(e.shape, lambda bi, h, i: (0, 0))],
        out_specs=pl.BlockSpec((1, 1, Q_BLOCK, NSA_WIDTH), lambda bi, h, i: (h, bi, i, 0)),
        out_shape=jax.ShapeDtypeStruct((N_KV_HEADS, b, t, NSA_WIDTH), BF16),
        scratch_shapes=[pltpu.VMEM((GROUP, Q_BLOCK, LANES), F32),
                        pltpu.VMEM((GROUP, Q_BLOCK, SLC_NEAR), F32),
                        pltpu.VMEM((GROUP, Q_BLOCK, WIN_KEYS), F32)],
        compiler_params=_cparams(("arbitrary", "arbitrary", "arbitrary")),
        name="nsa_prompt",
    )(rel_bias, q, gn, kcmp_p, vcmp_p, selmap, ks_p, vs_p, kw_p, vw_p, e)


def _col_softmax_update(state, s, vt, smax=None):
    m_old, l_old, acc = state
    m_new = jnp.maximum(m_old, jnp.max(s, axis=0, keepdims=True) if smax is None else smax)
    p = jnp.exp(s - m_new)
    alpha = jnp.exp(m_old - m_new)
    l_new = alpha * l_old + jnp.sum(p, axis=0, keepdims=True)
    return m_new, l_new, alpha * acc + _dot(vt, p.astype(BF16))


def _col_softmax_finish(state):
    _, l, acc = state
    return acc * jnp.where(l > 0.0, 1.0 / jnp.where(l > 0.0, l, 1.0), 0.0)


def _nsa_prompt_t_kernel(rb_ref, q_ref, gn_ref, kcmp_ref, vcmp_ref, vcmpt_ref, selmap_ref, selmapt_ref,
                         ks_ref, vst_ref, kw_ref, vwt_ref, et_ref, out_ref, nbc_ref, nbs_ref, nbw_ref):
    hk = pl.program_id(1)
    qb = pl.program_id(2)
    s0 = qb * Q_BLOCK
    gq = GROUP * Q_BLOCK

    @pl.when(qb == 0)
    def _():
        for g in range(GROUP):
            tab = lambda k, g=g: rb_ref[k, hk * GROUP + g]
            cols = slice(g * Q_BLOCK, (g + 1) * Q_BLOCK)
            cl = lax.broadcasted_iota(jnp.int32, (LANES, Q_BLOCK), 0)
            ql = lax.broadcasted_iota(jnp.int32, (LANES, Q_BLOCK), 1)
            dist = ql - CMP_STRIDE * (cl - CMP_FRONT) - (CMP_LEN - 1)
            ok = (dist >= 0) & (cl < CMP_NEAR)
            nbc_ref[:, cols] = jnp.where(ok, _bias_from_scalars(_bucket(dist), tab), NEG_INF)
            kl = lax.broadcasted_iota(jnp.int32, (SLC_NEAR, Q_BLOCK), 0)
            ql = lax.broadcasted_iota(jnp.int32, (SLC_NEAR, Q_BLOCK), 1)
            dist = ql - kl + Q_BLOCK
            nbs_ref[:, cols] = jnp.where(dist >= 0, _bias_from_scalars(_bucket(dist), tab), NEG_INF)
            kl = lax.broadcasted_iota(jnp.int32, (WIN_KEYS, Q_BLOCK), 0)
            ql = lax.broadcasted_iota(jnp.int32, (WIN_KEYS, Q_BLOCK), 1)
            dist = ql - kl + WINDOW
            ok = (dist >= 0) & (dist < WINDOW)
            nbw_ref[:, cols] = jnp.where(ok, _bias_from_scalars(_bucket(dist), tab), NEG_INF)

    lane = lax.broadcasted_iota(jnp.int32, (Q_BLOCK, LANES), 1)
    lmask = (lane >= hk * HEAD_DIM) & (lane < (hk + 1) * HEAD_DIM)
    qblk = q_ref[0].astype(F32)
    qp = jnp.concatenate(
        [jnp.where(lmask, qblk[:, g * LANES:(g + 1) * LANES], 0.0) for g in range(GROUP)], axis=0)
    qp = (qp * (HEAD_DIM ** -0.5)).astype(BF16)
    far_bias = jnp.concatenate(
        [jnp.full((1, Q_BLOCK), rb_ref[N_BUCKETS - 1, hk * GROUP + g], F32) for g in range(GROUP)], axis=1)
    init = (jnp.full((1, gq), NEG_INF, F32), jnp.zeros((1, gq), F32), jnp.zeros((LANES, gq), F32))
    p0 = pl.multiple_of(s0, LANES)

    n_cmp_pad = kcmp_ref.shape[1] - LANES
    c8 = qb * (Q_BLOCK // CMP_STRIDE)
    near0 = pl.multiple_of(c8, 8)
    kc_far = kcmp_ref[0, CMP_FRONT:CMP_FRONT + n_cmp_pad, :].astype(BF16)
    kc_near = kcmp_ref[0, pl.ds(near0, LANES), :].astype(BF16)
    crow = lax.broadcasted_iota(jnp.int32, (n_cmp_pad, gq), 0)
    s_far = jnp.where(crow < c8 - CMP_FRONT, _dot_t(kc_far, qp), NEG_INF)
    nrow = lax.broadcasted_iota(jnp.int32, (LANES, gq), 0)
    s_near = jnp.where(nrow >= CMP_FRONT - c8, _dot_t(kc_near, qp) + nbc_ref[...], NEG_INF)
    m = jnp.maximum(jnp.max(s_far, axis=0, keepdims=True) + far_bias, jnp.max(s_near, axis=0, keepdims=True))
    p_far = jnp.exp(s_far - (m - far_bias))
    p_near = jnp.exp(s_near - m)
    l = jnp.sum(p_far, axis=0, keepdims=True) + jnp.sum(p_near, axis=0, keepdims=True)
    inv = jnp.where(m > 0.5 * NEG_INF, 1.0 / l, 0.0)
    pc_far = p_far * inv
    pc_near = p_near * inv
    vct_near = vcmp_ref[0, pl.ds(near0, LANES), :].T.astype(BF16)
    o_cmp = _dot(vcmpt_ref[0], pc_far.astype(BF16)) + _dot(vct_near, pc_near.astype(BF16))

    wrow = lax.broadcasted_iota(jnp.int32, (WIN_KEYS, gq), 0)
    s = jnp.where(wrow >= WINDOW - s0, _dot_t(kw_ref[0, pl.ds(p0, WIN_KEYS), :], qp) + nbw_ref[...], NEG_INF)
    o_win = _col_softmax_finish(_col_softmax_update(init, s, vwt_ref[0, :, pl.ds(p0, WIN_KEYS)]))

    ps_far = sum(pc_far[:, g * Q_BLOCK:(g + 1) * Q_BLOCK] for g in range(GROUP))
    ps_near = sum(pc_near[:, g * Q_BLOCK:(g + 1) * Q_BLOCK] for g in range(GROUP))

    def split_dot(a, p):
        hi = p.astype(BF16)
        return _dot(a, hi) + _dot(a, (p - hi.astype(F32)).astype(BF16))

    sm_near_t = selmap_ref[pl.ds(near0, LANES), :].T.astype(BF16)
    imp_t = split_dot(selmapt_ref[...], ps_far) + split_dot(sm_near_t, ps_near)
    n_sel = imp_t.shape[0]
    blk = lax.broadcasted_iota(jnp.int32, (n_sel, Q_BLOCK), 0)
    cur = (s0 + lax.broadcasted_iota(jnp.int32, (n_sel, Q_BLOCK), 1)) // SEL_BLOCK
    forced = (blk == 0) | (blk == cur) | (blk == cur - 1)
    val = jnp.where(blk > cur, NEG_INF, imp_t + FORCE_BONUS * forced.astype(F32))
    sel_t = jnp.zeros((n_sel, Q_BLOCK), F32)
    for _ in range(N_SELECT):
        mx = jnp.max(val, axis=0, keepdims=True)
        first = jnp.min(jnp.where(val == mx, blk, n_sel), axis=0, keepdims=True)
        hit = blk == first
        sel_t = jnp.where(hit, 1.0, sel_t)
        val = jnp.where(hit, -3e38, val)
    sel = sel_t.T > 0.5

    jcol = lax.broadcasted_iota(jnp.int32, (Q_BLOCK, n_sel), 1)
    far_blocks = (s0 - Q_BLOCK) // SEL_BLOCK
    pen_far = jnp.where(sel & (jcol < far_blocks), 0.0, NEG_INF).astype(BF16)
    pen_near = jnp.where(sel, 0.0, NEG_INF).astype(BF16)
    q_far = jnp.concatenate([qp, _tile_rows(pen_far, GROUP)], axis=1)
    q_near = jnp.concatenate([qp, _tile_rows(pen_near, GROUP)], axis=1)

    n_tiles = (ks_ref.shape[1] - Q_BLOCK) // SLC_TILE

    def far_logits(kt):
        f0 = pl.multiple_of(Q_BLOCK + kt * SLC_TILE, LANES)
        k = jnp.concatenate([ks_ref[0, pl.ds(f0, SLC_TILE), :], et_ref[pl.ds(f0, SLC_TILE), :]], axis=1)
        return _dot_t(k, q_far)

    def far_step(kt, carry):
        s_cur, state = carry
        s_next = far_logits(jnp.minimum(kt + 1, n_tiles - 1))
        f0 = pl.multiple_of(Q_BLOCK + kt * SLC_TILE, LANES)
        return s_next, _col_softmax_update(state, s_cur, vst_ref[0, :, pl.ds(f0, SLC_TILE)])

    n_far = (qb + 2) // (SLC_TILE // Q_BLOCK)
    _, (m_far, l_far, acc_far) = lax.fori_loop(0, n_far, far_step, (far_logits(0), init))
    state = (m_far + far_bias, l_far, acc_far)
    k = jnp.concatenate([ks_ref[0, pl.ds(p0, SLC_NEAR), :], et_ref[pl.ds(p0, SLC_NEAR), :]], axis=1)
    krow = lax.broadcasted_iota(jnp.int32, (SLC_NEAR, gq), 0)
    s = jnp.where(krow >= Q_BLOCK - s0, _dot_t(k, q_near) + nbs_ref[...], NEG_INF)
    o_slc = _col_softmax_finish(_col_softmax_update(state, s, vst_ref[0, :, pl.ds(p0, SLC_NEAR)]))

    gst = jax.nn.sigmoid(gn_ref[0]).T

    def gate(g, i):
        c0, c1 = 3 * g + i, 3 * (GROUP + g) + i
        return jnp.where(hk == 0, gst[c0:c0 + 1, :], gst[c1:c1 + 1, :])

    for g in range(GROUP):
        cols = slice(g * Q_BLOCK, (g + 1) * Q_BLOCK)
        o = gate(g, 0) * o_cmp[:, cols] + gate(g, 1) * o_slc[:, cols] + gate(g, 2) * o_win[:, cols]
        out_ref[0, 0, :, g * LANES:(g + 1) * LANES] = jnp.where(lmask, o.T, 0.0).astype(out_ref.dtype)


def _nsa_prompt_t(q, gn, kcmp, vcmp, kvb, rel_bias):
    b, t, _ = q.shape
    assert t % SLC_TILE == 0
    nqb = t // Q_BLOCK
    n_chunk = t // CMP_STRIDE
    n_cmp = n_chunk - CMP_RATIO + 1
    n_blocks = t // SEL_BLOCK
    assert n_blocks <= LANES
    cpad = ((0, 0), (CMP_FRONT, LANES - CMP_FRONT), (0, 0))
    kcmp_p = jnp.pad(kcmp, cpad)
    vcmp_p = jnp.pad(vcmp, cpad)
    vcmp_t = vcmp.transpose(0, 2, 1).astype(BF16)
    sm = _sel_map(n_chunk, n_cmp, LANES, n_blocks)
    selmap = jnp.asarray(np.pad(sm, ((CMP_FRONT, LANES - CMP_FRONT), (0, 0))), F32)
    selmap_t = jnp.asarray(sm.T, BF16)
    front = lambda z, n: jnp.pad(z, ((0, 0), (n, 0), (0, 0)))
    ks_p = front(kvb[:, :, 2 * KV_WIDTH:3 * KV_WIDTH], Q_BLOCK)
    vs_t = front(kvb[:, :, 3 * KV_WIDTH:4 * KV_WIDTH], Q_BLOCK).transpose(0, 2, 1)
    kw_p = front(kvb[:, :, 4 * KV_WIDTH:5 * KV_WIDTH], WINDOW)
    vw_t = front(kvb[:, :, 5 * KV_WIDTH:6 * KV_WIDTH], WINDOW).transpose(0, 2, 1)
    pidx = np.arange(t + Q_BLOCK)
    e = (pidx[:, None] // SEL_BLOCK - Q_BLOCK // SEL_BLOCK == np.arange(LANES)[None, :])
    e = jnp.asarray(e.astype(np.float32), BF16)
    full = lambda arr: pl.BlockSpec((1,) + arr.shape[1:], lambda bi, h, i: (bi, 0, 0))
    const = lambda arr: pl.BlockSpec(arr.shape, lambda bi, h, i: (0, 0))
    return pl.pallas_call(
        _nsa_prompt_t_kernel,
        grid=(b, N_KV_HEADS, nqb),
        in_specs=[pl.BlockSpec(memory_space=pltpu.SMEM),
                  pl.BlockSpec((1, Q_BLOCK, NSA_WIDTH), lambda bi, h, i: (bi, i, 0)),
                  pl.BlockSpec((1, Q_BLOCK, LANES), lambda bi, h, i: (bi, i, 0)),
                  full(kcmp_p), full(vcmp_p), full(vcmp_t), const(selmap), const(selmap_t),
                  full(ks_p), full(vs_t), full(kw_p), full(vw_t), const(e)],
        out_specs=pl.BlockSpec((1, 1, Q_BLOCK, NSA_WIDTH), lambda bi, h, i: (h, bi, i, 0)),
        out_shape=jax.ShapeDtypeStruct((N_KV_HEADS, b, t, NSA_WIDTH), BF16),
        scratch_shapes=[pltpu.VMEM((LANES, GROUP * Q_BLOCK), F32),
                        pltpu.VMEM((SLC_NEAR, GROUP * Q_BLOCK), F32),
                        pltpu.VMEM((WIN_KEYS, GROUP * Q_BLOCK), F32)],
        compiler_params=_cparams(("arbitrary", "arbitrary", "arbitrary")),
        name="nsa_prompt",
    )(rel_bias, q, gn, kcmp_p, vcmp_p, vcmp_t, selmap, selmap_t, ks_p, vs_t, kw_p, vw_t, e)


def _nsa_prompt_pair_kernel(rb_ref, q_ref, gn_ref, kcmp_ref, vcmp_ref, vcmpt_ref, selmap_ref, selmapt_ref,
                            ks_ref, vst_ref, kw_ref, vwt_ref, et_ref, out_ref, nbc_ref, nbs_ref, nbw_ref, s_ref,
                            smax_ref):
    qb = pl.program_id(1)
    s0 = qb * Q_BLOCK
    gq = GROUP * Q_BLOCK
    heads = range(N_KV_HEADS)

    @pl.when(qb == 0)
    def _():
        for hk in heads:
            for g in range(GROUP):
                tab = lambda k, h=hk * GROUP + g: rb_ref[k, h]
                cols = slice(g * Q_BLOCK, (g + 1) * Q_BLOCK)
                cl = lax.broadcasted_iota(jnp.int32, (LANES, Q_BLOCK), 0)
                ql = lax.broadcasted_iota(jnp.int32, (LANES, Q_BLOCK), 1)
                dist = ql - CMP_STRIDE * (cl - CMP_FRONT) - (CMP_LEN - 1)
                ok = (dist >= 0) & (cl < CMP_NEAR)
                nbc_ref[hk, :, cols] = jnp.where(ok, _bias_from_scalars(_bucket(dist), tab), NEG_INF)
                kl = lax.broadcasted_iota(jnp.int32, (SLC_NEAR, Q_BLOCK), 0)
                ql = lax.broadcasted_iota(jnp.int32, (SLC_NEAR, Q_BLOCK), 1)
                dist = ql - kl + Q_BLOCK
                nbs_ref[hk, :, cols] = jnp.where(dist >= 0, _bias_from_scalars(_bucket(dist), tab), NEG_INF)
                kl = lax.broadcasted_iota(jnp.int32, (WIN_KEYS, Q_BLOCK), 0)
                ql = lax.broadcasted_iota(jnp.int32, (WIN_KEYS, Q_BLOCK), 1)
                dist = ql - kl + WINDOW
                ok = (dist >= 0) & (dist < WINDOW)
                nbw_ref[hk, :, cols] = jnp.where(ok, _bias_from_scalars(_bucket(dist), tab), NEG_INF)

    lane = lax.broadcasted_iota(jnp.int32, (Q_BLOCK, LANES), 1)
    qblk = q_ref[0].astype(F32)
    init = (jnp.full((1, gq), NEG_INF, F32), jnp.zeros((1, gq), F32), jnp.zeros((LANES, gq), F32))
    p0 = pl.multiple_of(s0, LANES)
    n_cmp_pad = kcmp_ref.shape[1] - LANES
    c8 = qb * (Q_BLOCK // CMP_STRIDE)
    near0 = pl.multiple_of(c8, 8)
    kc_far = kcmp_ref[0, CMP_FRONT:CMP_FRONT + n_cmp_pad, :].astype(BF16)
    kc_near = kcmp_ref[0, pl.ds(near0, LANES), :].astype(BF16)
    vct_near = vcmp_ref[0, pl.ds(near0, LANES), :].T.astype(BF16)
    sm_near_t = selmap_ref[pl.ds(near0, LANES), :].T.astype(BF16)
    kw = kw_ref[0, pl.ds(p0, WIN_KEYS), :]
    vwt = vwt_ref[0, :, pl.ds(p0, WIN_KEYS)]
    crow = lax.broadcasted_iota(jnp.int32, (n_cmp_pad, gq), 0)
    nrow = lax.broadcasted_iota(jnp.int32, (LANES, gq), 0)
    wrow = lax.broadcasted_iota(jnp.int32, (WIN_KEYS, gq), 0)
    krow = lax.broadcasted_iota(jnp.int32, (SLC_NEAR, gq), 0)
    n_sel = selmapt_ref.shape[0]
    blk = lax.broadcasted_iota(jnp.int32, (n_sel, Q_BLOCK), 0)
    cur = (s0 + lax.broadcasted_iota(jnp.int32, (n_sel, Q_BLOCK), 1)) // SEL_BLOCK
    forced = (blk == 0) | (blk == cur) | (blk == cur - 1)
    jcol = lax.broadcasted_iota(jnp.int32, (Q_BLOCK, n_sel), 1)
    far_blocks = (s0 - Q_BLOCK) // SEL_BLOCK

    def split_dot(a, p):
        hi = p.astype(BF16)
        return _dot(a, hi) + _dot(a, (p - hi.astype(F32)).astype(BF16))

    def before_far(hk):
        lmask = (lane // HEAD_DIM) == hk
        qp = jnp.concatenate(
            [jnp.where(lmask, qblk[:, g * LANES:(g + 1) * LANES], 0.0) for g in range(GROUP)], axis=0)
        qp = (qp * (HEAD_DIM ** -0.5)).astype(BF16)
        far_bias = jnp.concatenate(
            [jnp.full((1, Q_BLOCK), rb_ref[N_BUCKETS - 1, hk * GROUP + g], F32) for g in range(GROUP)], axis=1)

        s_far = jnp.where(crow < c8 - CMP_FRONT, _dot_t(kc_far, qp), NEG_INF)
        s_near = jnp.where(nrow >= CMP_FRONT - c8, _dot_t(kc_near, qp) + nbc_ref[hk], NEG_INF)
        m = jnp.maximum(jnp.max(s_far, axis=0, keepdims=True) + far_bias, jnp.max(s_near, axis=0, keepdims=True))
        p_far = jnp.exp(s_far - (m - far_bias))
        p_near = jnp.exp(s_near - m)
        l = jnp.sum(p_far, axis=0, keepdims=True) + jnp.sum(p_near, axis=0, keepdims=True)
        inv = jnp.where(m > 0.5 * NEG_INF, 1.0 / l, 0.0)
        pc_far = p_far * inv
        pc_near = p_near * inv
        o_cmp = _dot(vcmpt_ref[0], pc_far.astype(BF16)) + _dot(vct_near, pc_near.astype(BF16))

        s = jnp.where(wrow >= WINDOW - s0, _dot_t(kw, qp) + nbw_ref[hk], NEG_INF)
        o_win = _col_softmax_finish(_col_softmax_update(init, s, vwt))

        ps_far = sum(pc_far[:, g * Q_BLOCK:(g + 1) * Q_BLOCK] for g in range(GROUP))
        ps_near = sum(pc_near[:, g * Q_BLOCK:(g + 1) * Q_BLOCK] for g in range(GROUP))
        imp_t = split_dot(selmapt_ref[...], ps_far) + split_dot(sm_near_t, ps_near)
        val = jnp.where(blk > cur, NEG_INF, imp_t + FORCE_BONUS * forced.astype(F32))
        sel_t = jnp.zeros((n_sel, Q_BLOCK), F32)
        for _ in range(N_SELECT):
            mx = jnp.max(val, axis=0, keepdims=True)
            first = jnp.min(jnp.where(val == mx, blk, n_sel), axis=0, keepdims=True)
            hit = blk == first
            sel_t = jnp.where(hit, 1.0, sel_t)
            val = jnp.where(hit, -3e38, val)
        sel = sel_t.T > 0.5

        pen_far = jnp.where(sel & (jcol < far_blocks), 0.0, NEG_INF).astype(BF16)
        pen_near = jnp.where(sel, 0.0, NEG_INF).astype(BF16)
        q_far = jnp.concatenate([qp, _tile_rows(pen_far, GROUP)], axis=1)
        q_near = jnp.concatenate([qp, _tile_rows(pen_near, GROUP)], axis=1)
        return dict(far_bias=far_bias, o_cmp=o_cmp, o_win=o_win, q_far=q_far, q_near=q_near)

    pre = [before_far(hk) for hk in heads]
    n_tiles = (ks_ref.shape[1] - Q_BLOCK) // SLC_TILE

    def far_logits(kt):
        f0 = pl.multiple_of(Q_BLOCK + kt * SLC_TILE, LANES)
        k = jnp.concatenate([ks_ref[0, pl.ds(f0, SLC_TILE), :], et_ref[pl.ds(f0, SLC_TILE), :]], axis=1)
        return [_dot_t(k, pre[hk]["q_far"]) for hk in heads]

    def far_pair(i, states):
        for slot in range(2):
            kt = 2 * i + slot
            put_logits(1 - slot, jnp.minimum(kt + 1, n_tiles - 1))
            f0 = pl.multiple_of(Q_BLOCK + jnp.minimum(kt, n_tiles - 1) * SLC_TILE, LANES)
            vt = vst_ref[0, :, pl.ds(f0, SLC_TILE)]
            states = [_col_softmax_update(states[hk], s_ref[slot, hk], vt, smax=smax_ref[slot, hk, 0:1, :])
                      for hk in heads]
        return states

    def put_logits(slot, kt):
        for hk, s_new in enumerate(far_logits(kt)):
            s_ref[slot, hk] = s_new
            smax_ref[slot, hk] = jnp.broadcast_to(jnp.max(s_new, axis=0, keepdims=True), (8, gq))

    n_far = (qb + 2) // (SLC_TILE // Q_BLOCK)
    put_logits(0, 0)
    far_states = lax.fori_loop(0, (n_far + 1) // 2, far_pair, [init] * N_KV_HEADS)

    k_near = jnp.concatenate([ks_ref[0, pl.ds(p0, SLC_NEAR), :], et_ref[pl.ds(p0, SLC_NEAR), :]], axis=1)
    vt_near = vst_ref[0, :, pl.ds(p0, SLC_NEAR)]
    gst = jax.nn.sigmoid(gn_ref[0]).T
    outs = []
    for hk in heads:
        m_far, l_far, acc_far = far_states[hk]
        state = (m_far + pre[hk]["far_bias"], l_far, acc_far)
        s = jnp.where(krow >= Q_BLOCK - s0, _dot_t(k_near, pre[hk]["q_near"]) + nbs_ref[hk], NEG_INF)
        o_slc = _col_softmax_finish(_col_softmax_update(state, s, vt_near))
        per_g = []
        for g in range(GROUP):
            cols = slice(g * Q_BLOCK, (g + 1) * Q_BLOCK)
            c = 3 * (hk * GROUP + g)
            o = (gst[c:c + 1, :] * pre[hk]["o_cmp"][:, cols] + gst[c + 1:c + 2, :] * o_slc[:, cols]
                 + gst[c + 2:c + 3, :] * pre[hk]["o_win"][:, cols])
            per_g.append(o.T)
        outs.append(per_g)
    lane_h0 = lane < HEAD_DIM
    for g in range(GROUP):
        out_ref[0, :, g * LANES:(g + 1) * LANES] = jnp.where(lane_h0, outs[0][g], outs[1][g]).astype(out_ref.dtype)


def _nsa_prompt_pair(q, gn, kcmp, vcmp, kvb, vs_heads, vw_heads, rel_bias):
    b, t, _ = q.shape
    assert t % SLC_TILE == 0
    nqb = t // Q_BLOCK
    n_chunk = t // CMP_STRIDE
    n_cmp = n_chunk - CMP_RATIO + 1
    n_blocks = t // SEL_BLOCK
    assert n_blocks <= LANES
    cpad = ((0, 0), (CMP_FRONT, LANES - CMP_FRONT), (0, 0))
    kcmp_p = jnp.pad(kcmp, cpad)
    vcmp_p = jnp.pad(vcmp, cpad)
    vcmp_t = vcmp.transpose(0, 2, 1).astype(BF16)
    sm = _sel_map(n_chunk, n_cmp, LANES, n_blocks)
    selmap = jnp.asarray(np.pad(sm, ((CMP_FRONT, LANES - CMP_FRONT), (0, 0))), F32)
    selmap_t = jnp.asarray(sm.T, BF16)
    front = lambda z, n: jnp.pad(z, ((0, 0), (n, 0), (0, 0)))
    keys_front = lambda z, n: jnp.pad(z.reshape(b, KV_WIDTH, t).astype(BF16), ((0, 0), (0, 0), (n, 0)))
    ks_p = front(kvb[:, :, 2 * KV_WIDTH:3 * KV_WIDTH], Q_BLOCK)
    vs_t = keys_front(vs_heads, Q_BLOCK)
    kw_p = front(kvb[:, :, 4 * KV_WIDTH:5 * KV_WIDTH], WINDOW)
    vw_t = keys_front(vw_heads, WINDOW)
    pidx = np.arange(t + Q_BLOCK)
    e = (pidx[:, None] // SEL_BLOCK - Q_BLOCK // SEL_BLOCK == np.arange(LANES)[None, :])
    e = jnp.asarray(e.astype(np.float32), BF16)
    full = lambda arr: pl.BlockSpec((1,) + arr.shape[1:], lambda bi, i: (bi, 0, 0))
    const = lambda arr: pl.BlockSpec(arr.shape, lambda bi, i: (0, 0))
    return pl.pallas_call(
        _nsa_prompt_pair_kernel,
        grid=(b, nqb),
        in_specs=[pl.BlockSpec(memory_space=pltpu.SMEM),
                  pl.BlockSpec((1, Q_BLOCK, NSA_WIDTH), lambda bi, i: (bi, i, 0)),
                  pl.BlockSpec((1, Q_BLOCK, LANES), lambda bi, i: (bi, i, 0)),
                  full(kcmp_p), full(vcmp_p), full(vcmp_t), const(selmap), const(selmap_t),
                  full(ks_p), full(vs_t), full(kw_p), full(vw_t), const(e)],
        out_specs=pl.BlockSpec((1, Q_BLOCK, NSA_WIDTH), lambda bi, i: (bi, i, 0)),
        out_shape=jax.ShapeDtypeStruct((b, t, NSA_WIDTH), BF16),
        scratch_shapes=[pltpu.VMEM((N_KV_HEADS, LANES, GROUP * Q_BLOCK), F32),
                        pltpu.VMEM((N_KV_HEADS, SLC_NEAR, GROUP * Q_BLOCK), F32),
                        pltpu.VMEM((N_KV_HEADS, WIN_KEYS, GROUP * Q_BLOCK), F32),
                        pltpu.VMEM((2, N_KV_HEADS, SLC_TILE, GROUP * Q_BLOCK), F32),
                        pltpu.VMEM((2, N_KV_HEADS, 8, GROUP * Q_BLOCK), F32)],
        compiler_params=_cparams(("arbitrary", "arbitrary")),
        name="nsa_prompt",
    )(rel_bias, q, gn, kcmp_p, vcmp_p, vcmp_t, selmap, selmap_t, ks_p, vs_t, kw_p, vw_t, e)


ROWS_PER_TOKEN = 8


def _nsa_sample_cmp_kernel(q_ref, kcmp_ref, vcmp_ref, selmap_ref, rowtab_ref, ocmp_ref, idx_ref, *,
                           past_len, n_tok, n_cmp, n_blocks):
    rows = n_tok * ROWS_PER_TOKEN
    n_cpad = kcmp_ref.shape[1]
    ridx = lax.broadcasted_iota(jnp.int32, (rows, n_cpad), 0)
    ccol = lax.broadcasted_iota(jnp.int32, (rows, n_cpad), 1)
    dist = past_len + ridx // ROWS_PER_TOKEN - (CMP_STRIDE * ccol + CMP_LEN - 1)
    bucket = _bucket(dist)
    valid = (dist >= 0) & (ccol < n_cmp) & (ridx % ROWS_PER_TOKEN < GROUP)
    n_bpad = selmap_ref.shape[1]
    blk = lax.broadcasted_iota(jnp.int32, (ROWS_PER_TOKEN, n_bpad), 1)
    cur = (past_len + lax.broadcasted_iota(jnp.int32, (ROWS_PER_TOKEN, n_bpad), 0)) // SEL_BLOCK
    forced = (blk == 0) | (blk == cur) | (blk == cur - 1)
    olane = lax.broadcasted_iota(jnp.int32, (ROWS_PER_TOKEN, LANES), 1)
    biases = [_bias_from_rows(bucket, rowtab_ref[hk]) for hk in range(N_KV_HEADS)]
    for bi, hk in [(bi, hk) for bi in range(kcmp_ref.shape[0]) for hk in range(N_KV_HEADS)]:
        kc = kcmp_ref[bi].astype(BF16)
        vc = vcmp_ref[bi].astype(BF16)
        qp = (q_ref[bi, hk].astype(F32) * (HEAD_DIM ** -0.5)).astype(BF16)
        s = _dot_t(qp, kc) + biases[hk]
        s = jnp.where(valid, s, NEG_INF)
        m = jnp.max(s, axis=-1, keepdims=True)
        p = jnp.where(valid, jnp.exp(s - m), 0.0)
        l = jnp.sum(p, axis=-1, keepdims=True)
        pc = p * jnp.where(l > 0.0, 1.0 / jnp.where(l > 0.0, l, 1.0), 0.0)
        ocmp_ref[bi, hk] = _dot(pc.astype(BF16), vc)[:, hk * HEAD_DIM:(hk + 1) * HEAD_DIM]
        ps = jnp.concatenate(
            [jnp.sum(pc[i * ROWS_PER_TOKEN:(i + 1) * ROWS_PER_TOKEN], axis=0, keepdims=True) for i in range(n_tok)]
            + [jnp.zeros((ROWS_PER_TOKEN - n_tok, n_cpad), F32)], axis=0)
        imp = _split_dot(ps, selmap_ref[...])
        val = jnp.where((blk > cur) | (blk >= n_blocks), NEG_INF, imp + FORCE_BONUS * forced.astype(F32))
        picks = jnp.zeros((ROWS_PER_TOKEN, LANES), jnp.int32)
        for n in range(N_SELECT):
            mx = jnp.max(val, axis=-1, keepdims=True)
            first = jnp.min(jnp.where(val == mx, blk, n_bpad), axis=-1, keepdims=True)
            picks = jnp.where(olane == n, first, picks)
            val = jnp.where(blk == first, -3e38, val)
        idx_ref[bi, hk] = picks


def _nsa_sample_cmp(qs, kcmp, vcmp, rowtab, past_len, n_tok, n_cmp, n_blocks):
    nb = qs.shape[0]
    bps = next(d for d in (4, 2, 1) if nb % d == 0)
    n_cpad = kcmp.shape[1]
    n_bpad = -(-n_blocks // LANES) * LANES
    selmap = jnp.asarray(_sel_map(n_cpad, n_cmp, n_bpad, n_blocks), BF16)
    rows = n_tok * ROWS_PER_TOKEN
    return pl.pallas_call(
        functools.partial(_nsa_sample_cmp_kernel, past_len=past_len, n_tok=n_tok, n_cmp=n_cmp, n_blocks=n_blocks),
        grid=(nb // bps,),
        in_specs=[pl.BlockSpec((bps, N_KV_HEADS, rows, LANES), lambda b: (b, 0, 0, 0)),
                  pl.BlockSpec((bps, n_cpad, LANES), lambda b: (b, 0, 0)),
                  pl.BlockSpec((bps, n_cpad, LANES), lambda b: (b, 0, 0)),
                  pl.BlockSpec(selmap.shape, lambda b: (0, 0)),
                  pl.BlockSpec(rowtab.shape, lambda b: (0, 0, 0))],
        out_specs=[pl.BlockSpec((bps, N_KV_HEADS, rows, HEAD_DIM), lambda b: (b, 0, 0, 0)),
                   pl.BlockSpec((bps, N_KV_HEADS, ROWS_PER_TOKEN, LANES), lambda b: (b, 0, 0, 0))],
        out_shape=[jax.ShapeDtypeStruct((nb, N_KV_HEADS, rows, HEAD_DIM), F32),
                   jax.ShapeDtypeStruct((nb, N_KV_HEADS, ROWS_PER_TOKEN, LANES), jnp.int32)],
        compiler_params=_cparams(("arbitrary",)),
        name="nsa_sample_cmp",
    )(qs, kcmp, vcmp, selmap, rowtab)


def _nsa_sample_slc_kernel(idx_ref, pt_ref, ck_ref, cv_ref, q_ref, ocmp_ref, gn_ref, rowtab_ref, newk_ref, newv_ref,
                           swk_ref, swv_ref, nwk_ref, nwv_ref, out_ref, kbuf_ref, vbuf_ref, sem_ref, *,
                           past_len, n_tok, n_cache_blocks, npages):
    b = pl.program_id(0)
    hk = pl.program_id(1)
    step = b * N_KV_HEADS + hk
    last = pl.num_programs(0) * N_KV_HEADS - 1
    slot = step % 2
    per_step = n_tok * N_SELECT
    pages_per_block = PAGE // SEL_BLOCK

    def block_copies(st, i, sl):
        j = jnp.minimum(idx_ref[st * per_step + i], n_cache_blocks - 1)
        page = pt_ref[(st // N_KV_HEADS) * npages + j // pages_per_block]
        head = st % N_KV_HEADS
        return (pltpu.make_async_copy(ck_ref.at[page, head], kbuf_ref.at[sl, i], sem_ref.at[sl]),
                pltpu.make_async_copy(cv_ref.at[page, head], vbuf_ref.at[sl, i], sem_ref.at[sl]))

    @pl.when(step == 0)
    def _():
        for i in range(per_step):
            for c in block_copies(0, i, 0):
                c.start()

    nxt = jnp.minimum(step + 1, last)
    for i in range(per_step):
        for c in block_copies(nxt, i, 1 - slot):
            c.start()
    for i in range(per_step):
        for c in block_copies(step, i, slot):
            c.wait()

    rowtab = rowtab_ref[0]
    rvalid = lax.broadcasted_iota(jnp.int32, (ROWS_PER_TOKEN, 1), 0) < GROUP
    init = (jnp.full((ROWS_PER_TOKEN, 1), NEG_INF, F32), jnp.zeros((ROWS_PER_TOKEN, 1), F32),
            jnp.zeros((ROWS_PER_TOKEN, HEAD_DIM), F32))
    n_keys = N_SELECT * PAGE
    lane = lax.broadcasted_iota(jnp.int32, (ROWS_PER_TOKEN, n_keys), 1)
    row = lane % PAGE
    n_win = swk_ref.shape[3]
    kw = jnp.concatenate([swk_ref[0, 0], nwk_ref[0, 0]], axis=1).astype(BF16)
    vw = jnp.concatenate([swv_ref[0, 0], nwv_ref[0, 0]], axis=1).astype(BF16)
    wl = n_win + nwk_ref.shape[3]
    wpos = past_len - n_win + lax.broadcasted_iota(jnp.int32, (ROWS_PER_TOKEN, wl), 1)
    newk = newk_ref[0, 0]
    newv = newv_ref[0, 0]

    for tok in range(n_tok):
        t = past_len + tok
        qp = (q_ref[0, 0, tok].astype(F32) * (HEAD_DIM ** -0.5)).astype(BF16)
        pos = row
        in_block = jnp.zeros((ROWS_PER_TOKEN, n_keys), jnp.bool_)
        kts, vts = [], []
        for n in range(N_SELECT):
            i = tok * N_SELECT + n
            blk = idx_ref[step * per_step + i]
            is_new = blk >= n_cache_blocks
            kts.append(jnp.where(is_new, newk, kbuf_ref[slot, i]).astype(BF16))
            vts.append(jnp.where(is_new, newv, vbuf_ref[slot, i]).astype(BF16))
            here = lane // PAGE == n
            pos = jnp.where(here, row + (blk // pages_per_block) * PAGE, pos)
            in_block = in_block | (here & (row // SEL_BLOCK == blk % pages_per_block))
        dist = t - pos
        s = _dot(qp, jnp.concatenate(kts, axis=1)) + _bias_from_rows(_bucket(dist), rowtab)
        mask = (dist >= 0) & in_block & rvalid
        o_slc = _softmax_finish(_softmax_update(init, jnp.where(mask, s, NEG_INF), jnp.concatenate(vts, axis=1),
                                                mask=mask, v_transposed=True))
        wdist = t - wpos
        s = _dot(qp, kw) + _bias_from_rows(_bucket(wdist), rowtab)
        mask = (wdist >= 0) & (wdist < WINDOW) & (wpos >= 0) & rvalid
        o_win = _softmax_finish(_softmax_update(init, jnp.where(mask, s, NEG_INF), vw, mask=mask, v_transposed=True))
        gs = jax.nn.sigmoid(gn_ref[0, 0, tok])
        out_ref[0, 0, tok] = gs[:, 0:1] * ocmp_ref[0, 0, tok] + gs[:, 1:2] * o_slc + gs[:, 2:3] * o_win

    @pl.when(step == last)
    def _():
        for i in range(per_step):
            for c in block_copies(nxt, i, 1 - slot):
                c.wait()


def _nsa_sample_slc(idx, page_table, cache_k, cache_v, q3, ocmp, gn3, rowtab, newk, newv, swk, swv, nwk, nwv,
                    past_len, n_tok):
    nb, npages = page_table.shape
    halves = PAGE // SEL_BLOCK
    n_cache_blocks = npages * halves
    ck = cache_k.transpose(0, 2, 3, 1)
    cv = cache_v.transpose(0, 2, 3, 1)

    tile = lambda w: pl.BlockSpec((1, 1, n_tok, ROWS_PER_TOKEN, w), lambda b, h, i, p: (b, h, 0, 0, 0))
    per_bh = lambda arr: pl.BlockSpec((1, 1) + arr.shape[2:], lambda b, h, i, p: (b, h, 0, 0))
    grid_spec = pltpu.PrefetchScalarGridSpec(
        num_scalar_prefetch=2,
        grid=(nb, N_KV_HEADS),
        in_specs=[pl.BlockSpec(memory_space=pl.ANY), pl.BlockSpec(memory_space=pl.ANY),
                  tile(HEAD_DIM), tile(HEAD_DIM), tile(LANES),
                  pl.BlockSpec((1, ROWS_PER_TOKEN, N_BUCKETS), lambda b, h, i, p: (h, 0, 0)),
                  per_bh(newk), per_bh(newv), per_bh(swk), per_bh(swv), per_bh(nwk), per_bh(nwv)],
        out_specs=tile(HEAD_DIM),
        scratch_shapes=[pltpu.VMEM((2, n_tok * N_SELECT, HEAD_DIM, PAGE), F32),
                        pltpu.VMEM((2, n_tok * N_SELECT, HEAD_DIM, PAGE), F32),
                        pltpu.SemaphoreType.DMA((2,))],
    )
    return pl.pallas_call(
        functools.partial(_nsa_sample_slc_kernel, past_len=past_len, n_tok=n_tok, n_cache_blocks=n_cache_blocks,
                          npages=npages),
        grid_spec=grid_spec,
        out_shape=jax.ShapeDtypeStruct((nb, N_KV_HEADS, n_tok, ROWS_PER_TOKEN, HEAD_DIM), F32),
        compiler_params=_cparams(("arbitrary", "arbitrary")),
        name="nsa_sample_slc",
    )(idx.reshape(-1), page_table.reshape(-1), ck, cv, q3, ocmp, gn3, rowtab, newk, newv, swk, swv, nwk, nwv)


def _merge_tail(x, a, c, gates, wn_ref, wc_ref, wo_ref):
    ga = gates[:, :D_MODEL]
    gb = gates[:, D_MODEL:]
    m = (jax.nn.sigmoid(ga) * _dot(a, wn_ref[...])
         + jax.nn.sigmoid(gb) * _dot(c.astype(BF16), wc_ref[...]))
    return x + _dot(m.astype(BF16), wo_ref[...])


def _merge_prompt_kernel(x_ref, a_ref, conv_ref, halo_ref, gate_ref, cw_ref, cb_ref, wn_ref, wc_ref, wo_ref,
                         out_ref, cst_ref, *, tiles_per_batch):
    i = pl.program_id(0)
    first = i % tiles_per_batch == 0
    conv = conv_ref[...]
    bg = conv[:, :CONV_DIM]
    u = conv[:, CONV_DIM:2 * CONV_DIM] * conv[:, 2 * CONV_DIM:]
    halo = halo_ref[...]
    uh = jnp.where(first, 0.0, halo[:, CONV_DIM:2 * CONV_DIM] * halo[:, 2 * CONV_DIM:])
    tm = u.shape[0]
    row = lax.broadcasted_iota(jnp.int32, (tm, CONV_DIM), 0)
    u1 = jnp.where(row == 0, uh[7:8], pltpu.roll(u, 1, axis=0))
    u2 = pltpu.roll(u, 2, axis=0)
    u2 = jnp.where(row == 0, uh[6:7], jnp.where(row == 1, uh[7:8], u2))
    cw = cw_ref[...]
    y = ((cb_ref[...] + cw[0:1] * u2) + cw[1:2] * u1) + cw[2:3] * u
    c = bg * y
    out_ref[...] = _merge_tail(x_ref[...], a_ref[...], c, gate_ref[...], wn_ref, wc_ref, wo_ref)
    cst_ref[0] = u[tm - 8:, :]


def _merge_prompt(x2d, a2, conv3, gates, conv_w, conv_b, wn, wc, wo, t):
    n = x2d.shape[0]
    tm = min(512, t)
    assert t % tm == 0
    tpb = t // tm
    row = lambda w: pl.BlockSpec((tm, w), lambda i: (i, 0))
    const = lambda arr: pl.BlockSpec(arr.shape, lambda i: (0,) * arr.ndim)
    return pl.pallas_call(
        functools.partial(_merge_prompt_kernel, tiles_per_batch=tpb),
        grid=(n // tm,),
        in_specs=[row(D_MODEL),
                  row(NSA_WIDTH),
                  row(3 * CONV_DIM),
                  pl.BlockSpec((8, 3 * CONV_DIM), lambda i: (jnp.maximum(i * (tm // 8) - 1, 0), 0)),
                  row(2 * D_MODEL),
                  const(conv_w), const(conv_b), const(wn), const(wc), const(wo)],
        out_specs=[row(D_MODEL), pl.BlockSpec((1, 8, CONV_DIM), lambda i: (i // tpb, 0, 0))],
        out_shape=[jax.ShapeDtypeStruct((n, D_MODEL), F32),
                   jax.ShapeDtypeStruct((n // t, 8, CONV_DIM), F32)],
        compiler_params=_cparams(("arbitrary",)),
        name="merge_prompt",
    )(x2d, a2, conv3, conv3, gates, conv_w, conv_b, wn, wc, wo)


def _merge_sample_kernel(x_ref, a_ref, conv_ref, past_ref, gate_ref, cw_ref, cb_ref, wn_ref, wc_ref, wo_ref,
                         out_ref, cst_ref, *, n_tok, nb):
    conv = conv_ref[...]
    bg = conv[:, :CONV_DIM]
    u = conv[:, CONV_DIM:2 * CONV_DIM] * conv[:, 2 * CONV_DIM:]
    up = jnp.concatenate([past_ref[...], u], axis=0)
    cw = cw_ref[...]
    y = cb_ref[...]
    for k in range(CONV_WIDTH):
        y = y + cw[k:k + 1] * up[k * nb:(k + n_tok) * nb]
    c = bg * y
    out_ref[...] = _merge_tail(x_ref[...], a_ref[...].astype(BF16), c, gate_ref[...], wn_ref, wc_ref, wo_ref)
    cst_ref[...] = up[n_tok * nb:]


def _merge_sample(x_tm, a_tm, conv_tm, past_tm, gates_tm, conv_w, conv_b, wn, wc, wo, n_tok, nb):
    n = x_tm.shape[0]
    full = lambda arr: pl.BlockSpec(arr.shape, lambda i: (0,) * arr.ndim)
    args = (x_tm, a_tm, conv_tm, past_tm, gates_tm, conv_w, conv_b, wn, wc, wo)
    return pl.pallas_call(
        functools.partial(_merge_sample_kernel, n_tok=n_tok, nb=nb),
        grid=(1,),
        in_specs=[full(a) for a in args],
        out_specs=[pl.BlockSpec((n, D_MODEL), lambda i: (0, 0)),
                   pl.BlockSpec(((CONV_WIDTH - 1) * nb, CONV_DIM), lambda i: (0, 0))],
        out_shape=[jax.ShapeDtypeStruct((n, D_MODEL), F32),
                   jax.ShapeDtypeStruct(((CONV_WIDTH - 1) * nb, CONV_DIM), F32)],
        compiler_params=_cparams(("arbitrary",)),
        name="merge_sample",
    )(*args)


def _rms(x, g):
    return (x * lax.rsqrt(jnp.mean(x * x, axis=-1, keepdims=True) + RMS_EPS)) * g


def _mlp_kernel(x_ref, g_ref, wu_ref, wd_ref, gf_ref, out_ref, h_ref, acc_ref):
    j = pl.program_id(1)

    @pl.when(j == 0)
    def _():
        h_ref[...] = _rms(x_ref[...], g_ref[...]).astype(BF16)

    up = jnp.square(jnp.maximum(_dot(h_ref[...], wu_ref[...]), 0.0)).astype(BF16)
    part = _dot(up, wd_ref[...])

    @pl.when(j == 0)
    def _():
        acc_ref[...] = part

    @pl.when(j > 0)
    def _():
        acc_ref[...] += part

    @pl.when(j == pl.num_programs(1) - 1)
    def _():
        out_ref[...] = _rms(x_ref[...] + acc_ref[...], gf_ref[...])


def _mlp(x2d, g_mlp, w_up, w_down, g_final):
    n = x2d.shape[0]
    tm = min(1024, n)
    tf = 1024
    assert n % tm == 0 and D_FF % tf == 0
    return pl.pallas_call(
        _mlp_kernel,
        grid=(n // tm, D_FF // tf),
        in_specs=[pl.BlockSpec((tm, D_MODEL), lambda i, j: (i, 0)),
                  pl.BlockSpec((1, D_MODEL), lambda i, j: (0, 0)),
                  pl.BlockSpec((D_MODEL, tf), lambda i, j: (0, j)),
                  pl.BlockSpec((tf, D_MODEL), lambda i, j: (j, 0)),
                  pl.BlockSpec((1, D_MODEL), lambda i, j: (0, 0))],
        out_specs=pl.BlockSpec((tm, D_MODEL), lambda i, j: (i, 0)),
        out_shape=jax.ShapeDtypeStruct((n, D_MODEL), F32),
        scratch_shapes=[pltpu.VMEM((tm, D_MODEL), BF16), pltpu.VMEM((tm, D_MODEL), F32)],
        compiler_params=_cparams(("arbitrary", "arbitrary")),
        name="mlp",
    )(x2d, g_mlp, w_up, w_down, g_final)


def _kv_rows(z, lead):
    return z.reshape(lead + (N_KV_HEADS, HEAD_DIM))


def _layer(xp, xs, caches, states, page_table, w, rel_bias):
    (cache_cmp_k, cache_cmp_v, cache_slc_k, cache_slc_v) = caches
    (state_win_k, state_win_v, state_conv) = states
    nbp, t, _ = xp.shape
    nbs, n_tok, _ = xs.shape
    npages = page_table.shape[1]
    past_len = npages * PAGE
    assert t % PAGE == 0 and n_tok <= GROUP and state_win_k.shape[1] >= n_tok

    w_all = _prep_w_in(w["w_in"])
    g_attn = w["g_attn"].reshape(1, D_MODEL)
    wn = w["w_nsa_out"].reshape(N_KV_HEADS, GROUP, HEAD_DIM, D_MODEL).transpose(1, 0, 2, 3)
    wn = wn.reshape(NSA_WIDTH, D_MODEL).astype(BF16)
    wc = w["w_conv_out"].astype(BF16)
    wo = w["w_o"].astype(BF16)
    wu = w["w_up"].astype(BF16)
    wd = w["w_down"].astype(BF16)
    g_mlp = w["g_mlp"].reshape(1, D_MODEL)
    g_final = w["g_final"].reshape(1, D_MODEL)
    conv_w = w["conv_w"]
    conv_b = w["conv_b"].reshape(1, CONV_DIM)
    cmp_k = (w["cmp_pe_k"], w["cmp_w1_k"], w["cmp_w2_k"])
    cmp_v = (w["cmp_pe_v"], w["cmp_w1_v"], w["cmp_w2_v"])

    n_p = nbp * t
    (q, kc, vc, *kv_heads, kvb, gn, conv3, gates) = _proj(xp.reshape(n_p, D_MODEL), g_attn, w_all, True, t)
    ident = jnp.arange(nbp * (t // PAGE), dtype=jnp.int32).reshape(nbp, t // PAGE)
    kcmp = _compress(kc.reshape(n_p // PAGE, PAGE, KV_WIDTH), 0, ident, *cmp_k)
    vcmp = _compress(vc.reshape(n_p // PAGE, PAGE, KV_WIDTH), 0, ident, *cmp_v)
    a2 = _nsa_prompt_pair(q.reshape(nbp, t, NSA_WIDTH), gn.reshape(nbp, t, LANES), kcmp, vcmp,
                          kvb.reshape(nbp, t, 6 * KV_WIDTH), kv_heads[3], kv_heads[5], rel_bias)
    x1, cst = _merge_prompt(xp.reshape(n_p, D_MODEL), a2.reshape(n_p, NSA_WIDTH),
                            conv3, gates, conv_w, conv_b, wn, wc, wo, t)
    yp = _mlp(x1, g_mlp, wu, wd, g_final).reshape(nbp, t, D_MODEL)
    win_p = min(WINDOW, t)
    back = lambda z: z.transpose(0, 3, 1, 2)
    new_p = tuple(back(z) for z in kv_heads[:4]) + (
        back(kv_heads[4][..., t - win_p:]), back(kv_heads[5][..., t - win_p:]), cst[:, 8 - (CONV_WIDTH - 1):, :])

    n_s = nbs * n_tok
    (qs, kcs, vcs, kss, vss, kws, vws, _, gns, conv3s, gatess) = _proj(xs.reshape(n_s, D_MODEL), g_attn, w_all, False)
    n_rows = past_len + n_tok
    n_chunk = n_rows // CMP_STRIDE
    assert n_chunk == past_len // CMP_STRIDE, "new rows must not complete a compression chunk"
    n_cmp = n_chunk - CMP_RATIO + 1
    n_blocks = -(-n_rows // SEL_BLOCK)
    kcmp_s = _compress_t(cache_cmp_k, page_table, *cmp_k)
    vcmp_s = _compress_t(cache_cmp_v, page_table, *cmp_v)

    lane = np.arange(LANES)
    q5 = qs.reshape(nbs, n_tok, GROUP, LANES)
    q3 = jnp.stack([jnp.where(jnp.asarray((lane // HEAD_DIM) == h), q5, jnp.zeros_like(q5))
                    for h in range(N_KV_HEADS)], axis=1)
    q3 = jnp.pad(q3, ((0, 0), (0, 0), (0, 0), (0, ROWS_PER_TOKEN - GROUP), (0, 0)))
    tab = rel_bias.T.reshape(N_KV_HEADS, GROUP, N_BUCKETS)
    rowtab3 = jnp.pad(tab, ((0, 0), (0, ROWS_PER_TOKEN - GROUP), (0, 0)))
    rowtab = jnp.tile(rowtab3, (1, n_tok, 1))
    ocmp, idx = _nsa_sample_cmp(q3.reshape(nbs, N_KV_HEADS, n_tok * ROWS_PER_TOKEN, LANES), kcmp_s, vcmp_s,
                                rowtab, past_len, n_tok, n_cmp, n_blocks)
    idx = idx[:, :, :n_tok, :N_SELECT]
    gn3 = gns[:, :3 * N_HEADS].reshape(nbs, n_tok, N_KV_HEADS, GROUP, 3).transpose(0, 2, 1, 3, 4)
    gn3 = jnp.pad(gn3, ((0, 0), (0, 0), (0, 0), (0, ROWS_PER_TOKEN - GROUP), (0, LANES - 3)))
    q3s = jnp.pad(qs.reshape(nbs, n_tok, GROUP, N_KV_HEADS, HEAD_DIM).transpose(0, 3, 1, 2, 4),
                  ((0, 0), (0, 0), (0, 0), (0, ROWS_PER_TOKEN - GROUP), (0, 0)))
    ocmp5 = ocmp.reshape(nbs, N_KV_HEADS, n_tok, ROWS_PER_TOKEN, HEAD_DIM)
    new_t = lambda z: jnp.pad(z.reshape(nbs, n_tok, N_KV_HEADS, HEAD_DIM).transpose(0, 2, 3, 1),
                              ((0, 0), (0, 0), (0, 0), (0, LANES - n_tok)))
    a_s = _nsa_sample_slc(idx, page_table, cache_slc_k, cache_slc_v, q3s, ocmp5, gn3, rowtab3,
                          new_t(kss), new_t(vss),
                          state_win_k.transpose(0, 2, 3, 1), state_win_v.transpose(0, 2, 3, 1),
                          new_t(kws), new_t(vws), past_len, n_tok)
    a_tm = a_s[:, :, :, :GROUP, :].transpose(2, 0, 3, 1, 4).reshape(n_s, NSA_WIDTH)
    to_tm = lambda z: z.reshape(nbs, n_tok, -1).transpose(1, 0, 2).reshape(n_s, -1)
    past_tm = state_conv.transpose(1, 0, 2).reshape((CONV_WIDTH - 1) * nbs, CONV_DIM)
    x1s, csts = _merge_sample(to_tm(xs), a_tm, to_tm(conv3s), past_tm, to_tm(gatess),
                              conv_w, conv_b, wn, wc, wo, n_tok, nbs)
    ys = _mlp(x1s, g_mlp, wu, wd, g_final).reshape(n_tok, nbs, D_MODEL).transpose(1, 0, 2)
    win_buf = state_win_k.shape[1]
    kw_all = jnp.concatenate([state_win_k, _kv_rows(kws, (nbs, n_tok))], axis=1)[:, -win_buf:]
    vw_all = jnp.concatenate([state_win_v, _kv_rows(vws, (nbs, n_tok))], axis=1)[:, -win_buf:]
    new_s = tuple(_kv_rows(z, (nbs, n_tok)) for z in (kcs, vcs, kss, vss)) + (
        kw_all, vw_all, csts.reshape(CONV_WIDTH - 1, nbs, CONV_DIM).transpose(1, 0, 2))
    return yp, ys, new_p, new_s


def kernel(x_prompt, x_sample, cache_cmp_k, cache_cmp_v, cache_slc_k, cache_slc_v, state_win_k, state_win_v,
           state_conv, page_table, g_attn, w_in, cmp_pe_k, cmp_w1_k, cmp_w2_k, cmp_pe_v, cmp_w1_v, cmp_w2_v,
           conv_w, conv_b, w_nsa_out, w_conv_out, w_o, g_mlp, w_up, w_down, rel_bias, g_final):
    depth = w_in.shape[0]
    assert depth == 1, "the final norm is fused into the last layer's MLP kernel"
    names = ("g_attn", "w_in", "cmp_pe_k", "cmp_w1_k", "cmp_w2_k", "cmp_pe_v", "cmp_w1_v", "cmp_w2_v",
             "conv_w", "conv_b", "w_nsa_out", "w_conv_out", "w_o", "g_mlp", "w_up", "w_down")
    vals = (g_attn, w_in, cmp_pe_k, cmp_w1_k, cmp_w2_k, cmp_pe_v, cmp_w1_v, cmp_w2_v,
            conv_w, conv_b, w_nsa_out, w_conv_out, w_o, g_mlp, w_up, w_down)
    l = 0
    w = {k: v[l] for k, v in zip(names, vals)}
    w["g_final"] = g_final
    yp, ys, new_p, new_s = _layer(
        x_prompt, x_sample,
        (cache_cmp_k[l], cache_cmp_v[l], cache_slc_k[l], cache_slc_v[l]),
        (state_win_k[l], state_win_v[l], state_conv[l]), page_table, w, rel_bias)
    return (yp, ys) + tuple(z[None] for z in new_p) + tuple(z[None] for z in new_s)
```

```python
import functools
import math

import numpy as np
import jax
import jax.numpy as jnp
from jax import lax
from jax.experimental import pallas as pl
from jax.experimental.pallas import tpu as pltpu

F32 = jnp.float32
BF16 = jnp.bfloat16

D_MODEL = 1024
N_HEADS = 8
HEAD_DIM = 64
N_KV_HEADS = 2
GROUP = N_HEADS // N_KV_HEADS
KV_WIDTH = N_KV_HEADS * HEAD_DIM
NSA_WIDTH = N_HEADS * HEAD_DIM
CMP_LEN = 32
CMP_STRIDE = 16
CMP_RATIO = CMP_LEN // CMP_STRIDE
CMP_HIDDEN = 256
SEL_BLOCK = 64
N_SELECT = 16
WINDOW = 512
CONV_DIM = 512
CONV_WIDTH = 3
D_FF = 4 * D_MODEL
N_BUCKETS = 32
MAX_DISTANCE = 128
Q_BLOCK = 128
PAGE = 128
RMS_EPS = 1e-6
NEG_INF = -1e30
FORCE_BONUS = 1e4
LANES = 128
VMEM_LIMIT = 56 * 1024 * 1024


def _bucket_thresholds():
    max_exact = N_BUCKETS // 2
    n = np.arange(max_exact, 4 * MAX_DISTANCE).astype(np.float32)
    large = max_exact + (np.log(n / np.float32(max_exact)) / np.float32(math.log(MAX_DISTANCE / max_exact))
                         * np.float32(N_BUCKETS - max_exact)).astype(np.int32)
    large = np.minimum(large, N_BUCKETS - 1)
    thr = []
    for k in range(max_exact + 1, N_BUCKETS):
        thr.append(int(np.arange(max_exact, 4 * MAX_DISTANCE)[np.argmax(large >= k)]))
    return tuple(thr)


BUCKET_THR = _bucket_thresholds()


def _bucket(dist):
    n = jnp.maximum(dist, 0)
    big = jnp.full(n.shape, N_BUCKETS // 2, jnp.int32)
    for thr in BUCKET_THR:
        big = big + (n >= thr).astype(jnp.int32)
    return jnp.where(n < N_BUCKETS // 2, n, big)


def _bias_from_scalars(bucket, table_fn):
    out = jnp.zeros(bucket.shape, F32)
    for k in range(N_BUCKETS):
        out = jnp.where(bucket == k, table_fn(k), out)
    return out


def _bias_from_rows(bucket, rowtab):
    out = jnp.zeros(bucket.shape, F32)
    for k in range(N_BUCKETS):
        out = jnp.where(bucket == k, rowtab[:, k:k + 1], out)
    return out


def _dot(a, b):
    return jnp.dot(a, b, preferred_element_type=F32)


def _dot_t(a, b):
    return lax.dot_general(a, b, (((1,), (1,)), ((), ())), preferred_element_type=F32)


def _split_dot(p, m):
    hi = p.astype(BF16)
    lo = (p - hi.astype(F32)).astype(BF16)
    return _dot(hi, m) + _dot(lo, m)


def _cparams(sem):
    return pltpu.CompilerParams(dimension_semantics=sem, vmem_limit_bytes=VMEM_LIMIT)


PROJ_WIDTHS = (NSA_WIDTH, 6 * KV_WIDTH, LANES, 3 * CONV_DIM, 2 * D_MODEL)
PROJ_COLS = sum(PROJ_WIDTHS)


def _proj_kernel(x_ref, g_ref, w_ref, *out_refs, head_rows):
    q_ref, kvb_ref, gn_ref, conv_ref, gate_ref = out_refs[0], *out_refs[-4:]
    kv_refs = out_refs[1:-4]
    x = x_ref[...]
    y = x * lax.rsqrt(jnp.mean(x * x, axis=-1, keepdims=True) + RMS_EPS)
    hb = (y * g_ref[...]).astype(BF16)
    q_ref[...] = _dot(hb, w_ref[:, 0:NSA_WIDTH]).astype(q_ref.dtype)
    off = NSA_WIDTH
    kv = _dot(hb, w_ref[:, off:off + 6 * KV_WIDTH])
    tm = kv.shape[0]
    for i in range(6):
        z = kv[:, i * KV_WIDTH:(i + 1) * KV_WIDTH]
        if head_rows:
            kv_refs[2 + i][0] = z.T.reshape(N_KV_HEADS, HEAD_DIM, tm)
        if not head_rows or i < 2:
            kv_refs[i][...] = z
    kvb_ref[...] = kv.astype(BF16)
    off += 6 * KV_WIDTH
    gn_ref[...] = _dot(hb, w_ref[:, off:off + LANES])
    off += LANES
    conv_ref[...] = _dot(hb, w_ref[:, off:off + 3 * CONV_DIM])
    off += 3 * CONV_DIM
    gate_ref[...] = _dot(hb, w_ref[:, off:off + 2 * D_MODEL])


def _proj(x2d, g, w_all, head_rows, t=None):
    n = x2d.shape[0]
    tm = min(512, n if t is None else t)
    assert n % tm == 0
    row = lambda w: pl.BlockSpec((tm, w), lambda i: (i, 0))
    dense = lambda w, dt: (jax.ShapeDtypeStruct((n, w), dt), row(w))
    if head_rows:
        tpb = t // tm
        kv_outs = [dense(KV_WIDTH, F32)] * 2 + [
            (jax.ShapeDtypeStruct((n // t, N_KV_HEADS, HEAD_DIM, t), F32),
             pl.BlockSpec((1, N_KV_HEADS, HEAD_DIM, tm), lambda i: (i // tpb, 0, 0, i % tpb)))] * 6
    else:
        kv_outs = [dense(KV_WIDTH, F32)] * 6
    outs = ([dense(NSA_WIDTH, BF16)] + kv_outs
            + [dense(6 * KV_WIDTH, BF16), dense(LANES, F32), dense(3 * CONV_DIM, F32), dense(2 * D_MODEL, F32)])
    return pl.pallas_call(
        functools.partial(_proj_kernel, head_rows=head_rows),
        grid=(n // tm,),
        in_specs=[row(D_MODEL),
                  pl.BlockSpec((1, D_MODEL), lambda i: (0, 0)),
                  pl.BlockSpec((D_MODEL, PROJ_COLS), lambda i: (0, 0))],
        out_specs=[o[1] for o in outs],
        out_shape=[o[0] for o in outs],
        compiler_params=_cparams(("arbitrary",)),
        name="proj",
    )(x2d, g, w_all)


def _prep_w_in(w_in):
    sizes = (NSA_WIDTH,) + (KV_WIDTH,) * 6 + (3 * N_HEADS, CONV_DIM, CONV_DIM, CONV_DIM, D_MODEL, D_MODEL)
    offs = np.concatenate([[0], np.cumsum(sizes)])
    wq = w_in[:, :NSA_WIDTH].reshape(D_MODEL, N_KV_HEADS, GROUP, HEAD_DIM)
    wq = wq.transpose(0, 2, 1, 3).reshape(D_MODEL, NSA_WIDTH)
    wkv = w_in[:, offs[1]:offs[7]]
    wgn = jnp.pad(w_in[:, offs[7]:offs[8]], ((0, 0), (0, LANES - 3 * N_HEADS)))
    wrest = w_in[:, offs[8]:]
    return jnp.concatenate([wq, wkv, wgn, wrest], axis=1).astype(BF16)


CHUNKS_PER_PAGE = PAGE // CMP_STRIDE


def _compress_kernel(pt_ref, *refs, pp):
    del pt_ref
    page_refs = refs[:pp + 1]
    wbd_ref, pe_ref, w1_ref, w2_ref, out_ref, acc_ref, pehid_ref = refs[pp + 1:]

    @pl.when((pl.program_id(0) == 0) & (pl.program_id(1) == 0))
    def _():
        pehid_ref[...] = _dot(pe_ref[...], w1_ref[...])

    cols = []
    for r in range(CMP_STRIDE):
        xr = jnp.concatenate([p[0, pl.ds(r, CHUNKS_PER_PAGE, stride=CMP_STRIDE), :] for p in page_refs], axis=0)
        cols.append(xr.astype(BF16))
    x = jnp.concatenate(cols, axis=1)
    acc_ref[...] = _dot(x, wbd_ref[...])
    nc = CHUNKS_PER_PAGE * pp
    pe = pehid_ref[0:1, :]
    hids = []
    for h in range(N_KV_HEADS):
        base = h * CMP_RATIO * CMP_HIDDEN
        p0 = acc_ref[0:nc, base:base + CMP_HIDDEN]
        p1 = acc_ref[pl.ds(1, nc), base + CMP_HIDDEN:base + 2 * CMP_HIDDEN]
        hids.append(jax.nn.gelu((pe + p0) + p1).astype(BF16))
    out_ref[0] = _dot(jnp.concatenate(hids, axis=1), w2_ref[...])


def _compress(pages, colblk, page_table, pe, w1, w2):
    nb, npages = page_table.shape
    pp = min(32, npages)
    assert npages % pp == 0
    nc = CHUNKS_PER_PAGE * pp
    w1r = w1.reshape(CMP_RATIO, CMP_STRIDE, HEAD_DIM, CMP_HIDDEN)
    blk = w1r.transpose(1, 2, 0, 3).reshape(CMP_STRIDE, HEAD_DIM, CMP_RATIO * CMP_HIDDEN)
    z = jnp.zeros_like(blk)
    wbd = jnp.concatenate([jnp.concatenate([blk, z], axis=2), jnp.concatenate([z, blk], axis=2)], axis=1)
    wbd = wbd.reshape(CMP_STRIDE * KV_WIDTH, N_KV_HEADS * CMP_RATIO * CMP_HIDDEN).astype(BF16)
    z2 = jnp.zeros_like(w2)
    w2bd = jnp.concatenate([jnp.concatenate([w2, z2], axis=1), jnp.concatenate([z2, w2], axis=1)], axis=0).astype(BF16)
    pe8 = jnp.broadcast_to(pe.reshape(1, CMP_LEN * HEAD_DIM), (8, CMP_LEN * HEAD_DIM)).astype(BF16)

    def page_spec(p):
        def imap(b, s, pt):
            return (pt[b, jnp.minimum(s * pp + p, npages - 1)], 0, colblk)
        return pl.BlockSpec((1, PAGE, KV_WIDTH), imap)

    const = lambda shape: pl.BlockSpec(shape, lambda b, s, pt: (0,) * len(shape))
    grid_spec = pltpu.PrefetchScalarGridSpec(
        num_scalar_prefetch=1,
        grid=(nb, npages // pp),
        in_specs=[page_spec(p) for p in range(pp + 1)]
        + [const(wbd.shape), const(pe8.shape), const((CMP_LEN * HEAD_DIM, CMP_HIDDEN)), const(w2bd.shape)],
        out_specs=pl.BlockSpec((1, nc, KV_WIDTH), lambda b, s, pt: (b, s, 0)),
        scratch_shapes=[pltpu.VMEM((nc + 8, N_KV_HEADS * CMP_RATIO * CMP_HIDDEN), F32),
                        pltpu.VMEM((8, CMP_HIDDEN), F32)],
    )
    return pl.pallas_call(
        functools.partial(_compress_kernel, pp=pp),
        grid_spec=grid_spec,
        out_shape=jax.ShapeDtypeStruct((nb, npages * CHUNKS_PER_PAGE, KV_WIDTH), F32),
        compiler_params=_cparams(("arbitrary", "arbitrary")),
        name="compress",
    )(page_table, *([pages] * (pp + 1)), wbd, pe8, w1.astype(BF16), w2bd)


def _compress_t_kernel(pt_ref, *refs, pp, npages, ns):
    cache_refs = refs[:ns]
    weight_refs = [refs[ns + 4 * c:ns + 4 * c + 4] for c in range(ns)]
    out_refs = refs[5 * ns:6 * ns]
    pbuf_ref, sem_ref, xt_ref, acc_ref, pehid_ref = refs[6 * ns:]
    b = pl.program_id(0)
    s = pl.program_id(1)
    n_steps = pl.num_programs(1)
    step = b * n_steps + s
    last = pl.num_programs(0) * n_steps - 1
    slot = step % 2
    copies = [(c, p) for c in range(ns) for p in range(pp + 1)]

    def page_copy(bb, ss, c, p, sl):
        page = pt_ref[bb, jnp.minimum(ss * pp + p, npages - 1)]
        return pltpu.make_async_copy(cache_refs[c].at[page], pbuf_ref.at[sl, c, p], sem_ref.at[sl])

    @pl.when(step == 0)
    def _():
        for c in range(ns):
            pehid_ref[c] = _dot(weight_refs[c][1][...], weight_refs[c][2][...])
        for c, p in copies:
            page_copy(0, 0, c, p, 0).start()

    wrap = s + 1 == n_steps
    nxt_b = jnp.minimum(jnp.where(wrap, b + 1, b), pl.num_programs(0) - 1)
    nxt_s = jnp.where(step == last, s, jnp.where(wrap, 0, s + 1))
    for c, p in copies:
        page_copy(nxt_b, nxt_s, c, p, 1 - slot).start()
    for c, p in copies:
        page_copy(b, s, c, p, slot).wait()

    nc = CHUNKS_PER_PAGE * pp
    m = nc + CHUNKS_PER_PAGE
    first_head = lax.broadcasted_iota(jnp.int32, (m, KV_WIDTH), 1) < HEAD_DIM
    for c in range(ns):
        w1p_ref, _, _, w2_ref = weight_refs[c]
        for p in range(pp + 1):
            page = pbuf_ref[slot, c, p].reshape(KV_WIDTH, PAGE)
            xt_ref[c, p * PAGE:(p + 1) * PAGE, :] = page.T
        pe = pehid_ref[c, 0:1, :]
        xs = [[], []]
        for r in range(0, CMP_STRIDE, 2):
            a = xt_ref[c, pl.ds(r, m, stride=CMP_STRIDE), :]
            d = xt_ref[c, pl.ds(r + 1, m, stride=CMP_STRIDE), :]
            xs[0].append(jnp.where(first_head, a, pltpu.roll(d, HEAD_DIM, axis=1)).astype(BF16))
            xs[1].append(jnp.where(first_head, pltpu.roll(a, HEAD_DIM, axis=1), d).astype(BF16))
        hids = []
        for h in range(N_KV_HEADS):
            x = jnp.concatenate(xs[h], axis=1)
            acc_ref[c, h] = _dot(x, w1p_ref[...])
            p0 = acc_ref[c, h, 0:nc, 0:CMP_HIDDEN]
            p1 = acc_ref[c, h, pl.ds(1, nc), CMP_HIDDEN:2 * CMP_HIDDEN]
            hids.append(jax.nn.gelu((pe + p0) + p1).astype(BF16))
        out_refs[c][0] = _dot(jnp.concatenate(hids, axis=1), w2_ref[...])

    @pl.when(step == last)
    def _():
        for c, p in copies:
            page_copy(nxt_b, nxt_s, c, p, 1 - slot).wait()


def _compress_t(caches, page_table, weights):
    nb, npages = page_table.shape
    ns = len(caches)
    pp = min(32, npages)
    assert npages % pp == 0
    nc = CHUNKS_PER_PAGE * pp
    pages = [c.transpose(0, 2, 3, 1) for c in caches]
    wargs = []
    for pe, w1, w2 in weights:
        w1p = w1.reshape(CMP_RATIO, CMP_STRIDE * HEAD_DIM, CMP_HIDDEN).transpose(1, 0, 2)
        w1p = w1p.reshape(CMP_STRIDE * HEAD_DIM, CMP_RATIO * CMP_HIDDEN).astype(BF16)
        z2 = jnp.zeros_like(w2)
        w2bd = jnp.concatenate([jnp.concatenate([w2, z2], axis=1), jnp.concatenate([z2, w2], axis=1)], axis=0)
        pe8 = jnp.broadcast_to(pe.reshape(1, CMP_LEN * HEAD_DIM), (8, CMP_LEN * HEAD_DIM)).astype(BF16)
        wargs += [w1p, pe8, w1.astype(BF16), w2bd.astype(BF16)]

    const = lambda arr: pl.BlockSpec(arr.shape, lambda b, s, pt: (0,) * arr.ndim)
    out_spec = pl.BlockSpec((1, nc, KV_WIDTH), lambda b, s, pt: (b, s, 0))
    grid_spec = pltpu.PrefetchScalarGridSpec(
        num_scalar_prefetch=1,
        grid=(nb, npages // pp),
        in_specs=[pl.BlockSpec(memory_space=pl.ANY)] * ns + [const(a) for a in wargs],
        out_specs=[out_spec] * ns,
        scratch_shapes=[pltpu.VMEM((2, ns, pp + 1, N_KV_HEADS, HEAD_DIM, PAGE), F32),
                        pltpu.SemaphoreType.DMA((2,)),
                        pltpu.VMEM((ns, (pp + 1) * PAGE, KV_WIDTH), F32),
                        pltpu.VMEM((ns, N_KV_HEADS, nc + CHUNKS_PER_PAGE, CMP_RATIO * CMP_HIDDEN), F32),
                        pltpu.VMEM((ns, 8, CMP_HIDDEN), F32)],
    )
    return pl.pallas_call(
        functools.partial(_compress_t_kernel, pp=pp, npages=npages, ns=ns),
        grid_spec=grid_spec,
        out_shape=[jax.ShapeDtypeStruct((nb, npages * CHUNKS_PER_PAGE, KV_WIDTH), F32)] * ns,
        compiler_params=_cparams(("arbitrary", "arbitrary")),
        name="compress_t",
    )(page_table, *pages, *wargs)


def _sel_map(n_rows, n_cmp, n_cols, n_blocks):
    c0 = np.arange(n_rows)[:, None] * CMP_STRIDE
    s0 = np.arange(n_cols)[None, :] * SEL_BLOCK
    ov = np.minimum(c0 + CMP_LEN, s0 + SEL_BLOCK) - np.maximum(c0, s0)
    m = np.clip(ov, 0, None).astype(np.float32) / CMP_LEN
    m[n_cmp:] = 0.0
    m[:, n_blocks:] = 0.0
    return m


CMP_FRONT = 16
CMP_NEAR = 24
SLC_NEAR = 2 * Q_BLOCK
SLC_TILE = 512
WIN_KEYS = WINDOW + Q_BLOCK


def _softmax_update(state, s, v, mask=None, v_transposed=False):
    m_old, l_old, acc = state
    m_new = jnp.maximum(m_old, jnp.max(s, axis=-1, keepdims=True))
    p = jnp.exp(s - m_new)
    if mask is not None:
        p = jnp.where(mask, p, 0.0)
    alpha = jnp.exp(m_old - m_new)
    l_new = alpha * l_old + jnp.sum(p, axis=-1, keepdims=True)
    pv = _dot_t(p.astype(BF16), v) if v_transposed else _dot(p.astype(BF16), v)
    return m_new, l_new, alpha * acc + pv


def _softmax_finish(state):
    _, l, acc = state
    return jnp.where(l > 0.0, acc / jnp.where(l > 0.0, l, 1.0), 0.0)


def _tile_rows(x, n):
    return jnp.concatenate([x] * n, axis=0)


def _nsa_prompt_kernel(rb_ref, q_ref, gn_ref, kcmp_ref, vcmp_ref, selmap_ref, ks_ref, vs_ref, kw_ref, vw_ref,
                       et_ref, out_ref, nbc_ref, nbs_ref, nbw_ref):
    hk = pl.program_id(1)
    qb = pl.program_id(2)
    s0 = qb * Q_BLOCK
    gq = GROUP * Q_BLOCK

    @pl.when(qb == 0)
    def _():
        for g in range(GROUP):
            tab = lambda k, g=g: rb_ref[k, hk * GROUP + g]
            ql = lax.broadcasted_iota(jnp.int32, (Q_BLOCK, LANES), 0)
            cl = lax.broadcasted_iota(jnp.int32, (Q_BLOCK, LANES), 1)
            dist = ql - CMP_STRIDE * (cl - CMP_FRONT) - (CMP_LEN - 1)
            ok = (dist >= 0) & (cl < CMP_NEAR)
            nbc_ref[g] = jnp.where(ok, _bias_from_scalars(_bucket(dist), tab), NEG_INF)
            ql = lax.broadcasted_iota(jnp.int32, (Q_BLOCK, SLC_NEAR), 0)
            kl = lax.broadcasted_iota(jnp.int32, (Q_BLOCK, SLC_NEAR), 1)
            dist = ql - kl + Q_BLOCK
            nbs_ref[g] = jnp.where(dist >= 0, _bias_from_scalars(_bucket(dist), tab), NEG_INF)
            ql = lax.broadcasted_iota(jnp.int32, (Q_BLOCK, WIN_KEYS), 0)
            kl = lax.broadcasted_iota(jnp.int32, (Q_BLOCK, WIN_KEYS), 1)
            dist = ql - kl + WINDOW
            ok = (dist >= 0) & (dist < WINDOW)
            nbw_ref[g] = jnp.where(ok, _bias_from_scalars(_bucket(dist), tab), NEG_INF)

    lane = lax.broadcasted_iota(jnp.int32, (Q_BLOCK, LANES), 1)
    lmask = (lane >= hk * HEAD_DIM) & (lane < (hk + 1) * HEAD_DIM)
    qblk = q_ref[0].astype(F32)
    qp = jnp.concatenate(
        [jnp.where(lmask, qblk[:, g * LANES:(g + 1) * LANES], 0.0) for g in range(GROUP)], axis=0)
    qp = (qp * (HEAD_DIM ** -0.5)).astype(BF16)

    far_bias = jnp.concatenate(
        [jnp.full((Q_BLOCK, 1), rb_ref[N_BUCKETS - 1, hk * GROUP + g], F32) for g in range(GROUP)], axis=0)

    n_cmp_pad = kcmp_ref.shape[1] - CMP_FRONT - (LANES - CMP_FRONT)
    kc_far = kcmp_ref[0, CMP_FRONT:CMP_FRONT + n_cmp_pad, :].astype(BF16)
    vc_far = vcmp_ref[0, CMP_FRONT:CMP_FRONT + n_cmp_pad, :].astype(BF16)
    near0 = pl.multiple_of(qb * (Q_BLOCK // CMP_STRIDE), 8)
    kc_near = kcmp_ref[0, pl.ds(near0, LANES), :].astype(BF16)
    vc_near = vcmp_ref[0, pl.ds(near0, LANES), :].astype(BF16)
    c8 = qb * (Q_BLOCK // CMP_STRIDE)
    ccol = lax.broadcasted_iota(jnp.int32, (1, n_cmp_pad), 1)
    s_far = _dot_t(qp, kc_far) + jnp.where(ccol < c8 - CMP_FRONT, 0.0, NEG_INF)
    ncol = lax.broadcasted_iota(jnp.int32, (1, LANES), 1)
    s_near = _dot_t(qp, kc_near) + (nbc_ref[...].reshape(gq, LANES)
                                    + jnp.where(ncol >= CMP_FRONT - c8, 0.0, NEG_INF))
    m = jnp.maximum(jnp.max(s_far, axis=-1, keepdims=True) + far_bias, jnp.max(s_near, axis=-1, keepdims=True))
    p_far = jnp.exp(s_far - (m - far_bias))
    p_near = jnp.exp(s_near - m)
    l = jnp.sum(p_far, axis=-1, keepdims=True) + jnp.sum(p_near, axis=-1, keepdims=True)
    inv = jnp.where(m > 0.5 * NEG_INF, 1.0 / l, 0.0)
    pc_far = p_far * inv
    pc_near = p_near * inv
    o_cmp = _dot(pc_far.astype(BF16), vc_far) + _dot(pc_near.astype(BF16), vc_near)

    init = (jnp.full((gq, 1), NEG_INF, F32), jnp.zeros((gq, 1), F32), jnp.zeros((gq, LANES), F32))
    p0 = pl.multiple_of(s0, LANES)
    wcol = lax.broadcasted_iota(jnp.int32, (1, WIN_KEYS), 1)
    s = _dot_t(qp, kw_ref[0, pl.ds(p0, WIN_KEYS), :]) + (nbw_ref[...].reshape(gq, WIN_KEYS)
                                                         + jnp.where(wcol >= WINDOW - s0, 0.0, NEG_INF))
    o_win = _softmax_finish(_softmax_update(init, s, vw_ref[0, pl.ds(p0, WIN_KEYS), :]))

    ps_far = sum(pc_far[g * Q_BLOCK:(g + 1) * Q_BLOCK] for g in range(GROUP))
    ps_near = sum(pc_near[g * Q_BLOCK:(g + 1) * Q_BLOCK] for g in range(GROUP))
    sm_far = selmap_ref[CMP_FRONT:CMP_FRONT + n_cmp_pad, :].astype(BF16)
    sm_near = selmap_ref[pl.ds(near0, LANES), :].astype(BF16)
    imp = _split_dot(ps_far, sm_far) + _split_dot(ps_near, sm_near)
    imp_t = imp.T
    n_sel = imp_t.shape[0]
    blk = lax.broadcasted_iota(jnp.int32, (n_sel, Q_BLOCK), 0)
    cur = (s0 + lax.broadcasted_iota(jnp.int32, (n_sel, Q_BLOCK), 1)) // SEL_BLOCK
    forced = (blk == 0) | (blk == cur) | (blk == cur - 1)
    val = jnp.where(blk > cur, NEG_INF, imp_t + FORCE_BONUS * forced.astype(F32))
    sel_t = jnp.zeros((n_sel, Q_BLOCK), F32)
    for _ in range(N_SELECT):
        mx = jnp.max(val, axis=0, keepdims=True)
        first = jnp.min(jnp.where(val == mx, blk, n_sel), axis=0, keepdims=True)
        hit = blk == first
        sel_t = jnp.where(hit, 1.0, sel_t)
        val = jnp.where(hit, -3e38, val)
    sel = sel_t.T > 0.5

    jcol = lax.broadcasted_iota(jnp.int32, (Q_BLOCK, n_sel), 1)
    far_blocks = (s0 - Q_BLOCK) // SEL_BLOCK
    pen_far = jnp.where(sel & (jcol < far_blocks), 0.0, NEG_INF).astype(BF16)
    pen_near = jnp.where(sel, 0.0, NEG_INF).astype(BF16)
    q_far = jnp.concatenate([qp, _tile_rows(pen_far, GROUP)], axis=1)
    q_near = jnp.concatenate([qp, _tile_rows(pen_near, GROUP)], axis=1)

    def far_step(kt, state):
        p0 = pl.multiple_of(Q_BLOCK + kt * SLC_TILE, LANES)
        k = jnp.concatenate([ks_ref[0, pl.ds(p0, SLC_TILE), :], et_ref[pl.ds(p0, SLC_TILE), :]], axis=1)
        return _softmax_update(state, _dot_t(q_far, k), vs_ref[0, pl.ds(p0, SLC_TILE), :])

    n_far = (qb + 2) // (SLC_TILE // Q_BLOCK)
    m_far, l_far, acc_far = lax.fori_loop(0, n_far, far_step, init)
    state = (m_far + far_bias, l_far, acc_far)
    k = jnp.concatenate([ks_ref[0, pl.ds(p0, SLC_NEAR), :], et_ref[pl.ds(p0, SLC_NEAR), :]], axis=1)
    kcol = lax.broadcasted_iota(jnp.int32, (1, SLC_NEAR), 1)
    s = _dot_t(q_near, k) + (nbs_ref[...].reshape(gq, SLC_NEAR) + jnp.where(kcol >= Q_BLOCK - s0, 0.0, NEG_INF))
    o_slc = _softmax_finish(_softmax_update(state, s, vs_ref[0, pl.ds(p0, SLC_NEAR), :]))

    gs = jax.nn.sigmoid(gn_ref[0])

    def gate(g, i):
        c0, c1 = 3 * g + i, 3 * (GROUP + g) + i
        return jnp.where(hk == 0, gs[:, c0:c0 + 1], gs[:, c1:c1 + 1])

    for g in range(GROUP):
        rows = slice(g * Q_BLOCK, (g + 1) * Q_BLOCK)
        o = gate(g, 0) * o_cmp[rows] + gate(g, 1) * o_slc[rows] + gate(g, 2) * o_win[rows]
        out_ref[0, 0, :, g * LANES:(g + 1) * LANES] = jnp.where(lmask, o, 0.0).astype(out_ref.dtype)


def _nsa_prompt(q, gn, kcmp, vcmp, kvb, rel_bias):
    b, t, _ = q.shape
    assert t % SLC_TILE == 0
    nqb = t // Q_BLOCK
    n_chunk = t // CMP_STRIDE
    n_cmp = n_chunk - CMP_RATIO + 1
    n_blocks = t // SEL_BLOCK
    assert n_blocks <= LANES
    cpad = ((0, 0), (CMP_FRONT, LANES - CMP_FRONT), (0, 0))
    kcmp_p = jnp.pad(kcmp, cpad)
    vcmp_p = jnp.pad(vcmp, cpad)
    selmap = jnp.asarray(np.pad(_sel_map(n_chunk, n_cmp, LANES, n_blocks),
                                ((CMP_FRONT, LANES - CMP_FRONT), (0, 0))), F32)
    ks_p = jnp.pad(kvb[:, :, 2 * KV_WIDTH:3 * KV_WIDTH], ((0, 0), (Q_BLOCK, 0), (0, 0)))
    vs_p = jnp.pad(kvb[:, :, 3 * KV_WIDTH:4 * KV_WIDTH], ((0, 0), (Q_BLOCK, 0), (0, 0)))
    kw_p = jnp.pad(kvb[:, :, 4 * KV_WIDTH:5 * KV_WIDTH], ((0, 0), (WINDOW, 0), (0, 0)))
    vw_p = jnp.pad(kvb[:, :, 5 * KV_WIDTH:6 * KV_WIDTH], ((0, 0), (WINDOW, 0), (0, 0)))
    pidx = np.arange(t + Q_BLOCK)
    e = (pidx[:, None] // SEL_BLOCK - Q_BLOCK // SEL_BLOCK == np.arange(LANES)[None, :])
    e = jnp.asarray(e.astype(np.float32), BF16)
    full = lambda arr: pl.BlockSpec((1,) + arr.shape[1:], lambda bi, h, i: (bi, 0, 0))
    return pl.pallas_call(
        _nsa_prompt_kernel,
        grid=(b, N_KV_HEADS, nqb),
        in_specs=[pl.BlockSpec(memory_space=pltpu.SMEM),
                  pl.BlockSpec((1, Q_BLOCK, NSA_WIDTH), lambda bi, h, i: (bi, i, 0)),
                  pl.BlockSpec((1, Q_BLOCK, LANES), lambda bi, h, i: (bi, i, 0)),
                  full(kcmp_p), full(vcmp_p),
                  pl.BlockSpec(selmap.shape, lambda bi, h, i: (0, 0)),
                  full(ks_p), full(vs_p), full(kw_p), full(vw_p),
                  pl.BlockSpec(e.shape, lambda bi, h, i: (0, 0))],
        out_specs=pl.BlockSpec((1, 1, Q_BLOCK, NSA_WIDTH), lambda bi, h, i: (h, bi, i, 0)),
        out_shape=jax.ShapeDtypeStruct((N_KV_HEADS, b, t, NSA_WIDTH), BF16),
        scratch_shapes=[pltpu.VMEM((GROUP, Q_BLOCK, LANES), F32),
                        pltpu.VMEM((GROUP, Q_BLOCK, SLC_NEAR), F32),
                        pltpu.VMEM((GROUP, Q_BLOCK, WIN_KEYS), F32)],
        compiler_params=_cparams(("arbitrary", "arbitrary", "arbitrary")),
        name="nsa_prompt",
    )(rel_bias, q, gn, kcmp_p, vcmp_p, selmap, ks_p, vs_p, kw_p, vw_p, e)


def _col_softmax_update(state, s, vt, smax=None):
    m_old, l_old, acc = state
    m_new = jnp.maximum(m_old, jnp.max(s, axis=0, keepdims=True) if smax is None else smax)
    p = jnp.exp(s - m_new)
    alpha = jnp.exp(m_old - m_new)
    l_new = alpha * l_old + jnp.sum(p, axis=0, keepdims=True)
    return m_new, l_new, alpha * acc + _dot(vt, p.astype(BF16))


def _col_softmax_finish(state):
    _, l, acc = state
    return acc * jnp.where(l > 0.0, 1.0 / jnp.where(l > 0.0, l, 1.0), 0.0)


def _nsa_prompt_t_kernel(rb_ref, q_ref, gn_ref, kcmp_ref, vcmp_ref, vcmpt_ref, selmap_ref, selmapt_ref,
                         ks_ref, vst_ref, kw_ref, vwt_ref, et_ref, out_ref, nbc_ref, nbs_ref, nbw_ref):
    hk = pl.program_id(1)
    qb = pl.program_id(2)
    s0 = qb * Q_BLOCK
    gq = GROUP * Q_BLOCK

    @pl.when(qb == 0)
    def _():
        for g in range(GROUP):
            tab = lambda k, g=g: rb_ref[k, hk * GROUP + g]
            cols = slice(g * Q_BLOCK, (g + 1) * Q_BLOCK)
            cl = lax.broadcasted_iota(jnp.int32, (LANES, Q_BLOCK), 0)
            ql = lax.broadcasted_iota(jnp.int32, (LANES, Q_BLOCK), 1)
            dist = ql - CMP_STRIDE * (cl - CMP_FRONT) - (CMP_LEN - 1)
            ok = (dist >= 0) & (cl < CMP_NEAR)
            nbc_ref[:, cols] = jnp.where(ok, _bias_from_scalars(_bucket(dist), tab), NEG_INF)
            kl = lax.broadcasted_iota(jnp.int32, (SLC_NEAR, Q_BLOCK), 0)
            ql = lax.broadcasted_iota(jnp.int32, (SLC_NEAR, Q_BLOCK), 1)
            dist = ql - kl + Q_BLOCK
            nbs_ref[:, cols] = jnp.where(dist >= 0, _bias_from_scalars(_bucket(dist), tab), NEG_INF)
            kl = lax.broadcasted_iota(jnp.int32, (WIN_KEYS, Q_BLOCK), 0)
            ql = lax.broadcasted_iota(jnp.int32, (WIN_KEYS, Q_BLOCK), 1)
            dist = ql - kl + WINDOW
            ok = (dist >= 0) & (dist < WINDOW)
            nbw_ref[:, cols] = jnp.where(ok, _bias_from_scalars(_bucket(dist), tab), NEG_INF)

    lane = lax.broadcasted_iota(jnp.int32, (Q_BLOCK, LANES), 1)
    lmask = (lane >= hk * HEAD_DIM) & (lane < (hk + 1) * HEAD_DIM)
    qblk = q_ref[0].astype(F32)
    qp = jnp.concatenate(
        [jnp.where(lmask, qblk[:, g * LANES:(g + 1) * LANES], 0.0) for g in range(GROUP)], axis=0)
    qp = (qp * (HEAD_DIM ** -0.5)).astype(BF16)
    far_bias = jnp.concatenate(
        [jnp.full((1, Q_BLOCK), rb_ref[N_BUCKETS - 1, hk * GROUP + g], F32) for g in range(GROUP)], axis=1)
    init = (jnp.full((1, gq), NEG_INF, F32), jnp.zeros((1, gq), F32), jnp.zeros((LANES, gq), F32))
    p0 = pl.multiple_of(s0, LANES)

    n_cmp_pad = kcmp_ref.shape[1] - LANES
    c8 = qb * (Q_BLOCK // CMP_STRIDE)
    near0 = pl.multiple_of(c8, 8)
    kc_far = kcmp_ref[0, CMP_FRONT:CMP_FRONT + n_cmp_pad, :].astype(BF16)
    kc_near = kcmp_ref[0, pl.ds(near0, LANES), :].astype(BF16)
    crow = lax.broadcasted_iota(jnp.int32, (n_cmp_pad, gq), 0)
    s_far = jnp.where(crow < c8 - CMP_FRONT, _dot_t(kc_far, qp), NEG_INF)
    nrow = lax.broadcasted_iota(jnp.int32, (LANES, gq), 0)
    s_near = jnp.where(nrow >= CMP_FRONT - c8, _dot_t(kc_near, qp) + nbc_ref[...], NEG_INF)
    m = jnp.maximum(jnp.max(s_far, axis=0, keepdims=True) + far_bias, jnp.max(s_near, axis=0, keepdims=True))
    p_far = jnp.exp(s_far - (m - far_bias))
    p_near = jnp.exp(s_near - m)
    l = jnp.sum(p_far, axis=0, keepdims=True) + jnp.sum(p_near, axis=0, keepdims=True)
    inv = jnp.where(m > 0.5 * NEG_INF, 1.0 / l, 0.0)
    pc_far = p_far * inv
    pc_near = p_near * inv
    vct_near = vcmp_ref[0, pl.ds(near0, LANES), :].T.astype(BF16)
    o_cmp = _dot(vcmpt_ref[0], pc_far.astype(BF16)) + _dot(vct_near, pc_near.astype(BF16))

    wrow = lax.broadcasted_iota(jnp.int32, (WIN_KEYS, gq), 0)
    s = jnp.where(wrow >= WINDOW - s0, _dot_t(kw_ref[0, pl.ds(p0, WIN_KEYS), :], qp) + nbw_ref[...], NEG_INF)
    o_win = _col_softmax_finish(_col_softmax_update(init, s, vwt_ref[0, :, pl.ds(p0, WIN_KEYS)]))

    ps_far = sum(pc_far[:, g * Q_BLOCK:(g + 1) * Q_BLOCK] for g in range(GROUP))
    ps_near = sum(pc_near[:, g * Q_BLOCK:(g + 1) * Q_BLOCK] for g in range(GROUP))

    def split_dot(a, p):
        hi = p.astype(BF16)
        return _dot(a, hi) + _dot(a, (p - hi.astype(F32)).astype(BF16))

    sm_near_t = selmap_ref[pl.ds(near0, LANES), :].T.astype(BF16)
    imp_t = split_dot(selmapt_ref[...], ps_far) + split_dot(sm_near_t, ps_near)
    n_sel = imp_t.shape[0]
    blk = lax.broadcasted_iota(jnp.int32, (n_sel, Q_BLOCK), 0)
    cur = (s0 + lax.broadcasted_iota(jnp.int32, (n_sel, Q_BLOCK), 1)) // SEL_BLOCK
    forced = (blk == 0) | (blk == cur) | (blk == cur - 1)
    val = jnp.where(blk > cur, NEG_INF, imp_t + FORCE_BONUS * forced.astype(F32))
    sel_t = jnp.zeros((n_sel, Q_BLOCK), F32)
    for _ in range(N_SELECT):
        mx = jnp.max(val, axis=0, keepdims=True)
        first = jnp.min(jnp.where(val == mx, blk, n_sel), axis=0, keepdims=True)
        hit = blk == first
        sel_t = jnp.where(hit, 1.0, sel_t)
        val = jnp.where(hit, -3e38, val)
    sel = sel_t.T > 0.5

    jcol = lax.broadcasted_iota(jnp.int32, (Q_BLOCK, n_sel), 1)
    far_blocks = (s0 - Q_BLOCK) // SEL_BLOCK
    pen_far = jnp.where(sel & (jcol < far_blocks), 0.0, NEG_INF).astype(BF16)
    pen_near = jnp.where(sel, 0.0, NEG_INF).astype(BF16)
    q_far = jnp.concatenate([qp, _tile_rows(pen_far, GROUP)], axis=1)
    q_near = jnp.concatenate([qp, _tile_rows(pen_near, GROUP)], axis=1)

    n_tiles = (ks_ref.shape[1] - Q_BLOCK) // SLC_TILE

    def far_logits(kt):
        f0 = pl.multiple_of(Q_BLOCK + kt * SLC_TILE, LANES)
        k = jnp.concatenate([ks_ref[0, pl.ds(f0, SLC_TILE), :], et_ref[pl.ds(f0, SLC_TILE), :]], axis=1)
        return _dot_t(k, q_far)

    def far_step(kt, carry):
        s_cur, state = carry
        s_next = far_logits(jnp.minimum(kt + 1, n_tiles - 1))
        f0 = pl.multiple_of(Q_BLOCK + kt * SLC_TILE, LANES)
        return s_next, _col_softmax_update(state, s_cur, vst_ref[0, :, pl.ds(f0, SLC_TILE)])

    n_far = (qb + 2) // (SLC_TILE // Q_BLOCK)
    _, (m_far, l_far, acc_far) = lax.fori_loop(0, n_far, far_step, (far_logits(0), init))
    state = (m_far + far_bias, l_far, acc_far)
    k = jnp.concatenate([ks_ref[0, pl.ds(p0, SLC_NEAR), :], et_ref[pl.ds(p0, SLC_NEAR), :]], axis=1)
    krow = lax.broadcasted_iota(jnp.int32, (SLC_NEAR, gq), 0)
    s = jnp.where(krow >= Q_BLOCK - s0, _dot_t(k, q_near) + nbs_ref[...], NEG_INF)
    o_slc = _col_softmax_finish(_col_softmax_update(state, s, vst_ref[0, :, pl.ds(p0, SLC_NEAR)]))

    gst = jax.nn.sigmoid(gn_ref[0]).T

    def gate(g, i):
        c0, c1 = 3 * g + i, 3 * (GROUP + g) + i
        return jnp.where(hk == 0, gst[c0:c0 + 1, :], gst[c1:c1 + 1, :])

    for g in range(GROUP):
        cols = slice(g * Q_BLOCK, (g + 1) * Q_BLOCK)
        o = gate(g, 0) * o_cmp[:, cols] + gate(g, 1) * o_slc[:, cols] + gate(g, 2) * o_win[:, cols]
        out_ref[0, 0, :, g * LANES:(g + 1) * LANES] = jnp.where(lmask, o.T, 0.0).astype(out_ref.dtype)


def _nsa_prompt_t(q, gn, kcmp, vcmp, kvb, rel_bias):
    b, t, _ = q.shape
    assert t % SLC_TILE == 0
    nqb = t // Q_BLOCK
    n_chunk = t // CMP_STRIDE
    n_cmp = n_chunk - CMP_RATIO + 1
    n_blocks = t // SEL_BLOCK
    assert n_blocks <= LANES
    cpad = ((0, 0), (CMP_FRONT, LANES - CMP_FRONT), (0, 0))
    kcmp_p = jnp.pad(kcmp, cpad)
    vcmp_p = jnp.pad(vcmp, cpad)
    vcmp_t = vcmp.transpose(0, 2, 1).astype(BF16)
    sm = _sel_map(n_chunk, n_cmp, LANES, n_blocks)
    selmap = jnp.asarray(np.pad(sm, ((CMP_FRONT, LANES - CMP_FRONT), (0, 0))), F32)
    selmap_t = jnp.asarray(sm.T, BF16)
    front = lambda z, n: jnp.pad(z, ((0, 0), (n, 0), (0, 0)))
    ks_p = front(kvb[:, :, 2 * KV_WIDTH:3 * KV_WIDTH], Q_BLOCK)
    vs_t = front(kvb[:, :, 3 * KV_WIDTH:4 * KV_WIDTH], Q_BLOCK).transpose(0, 2, 1)
    kw_p = front(kvb[:, :, 4 * KV_WIDTH:5 * KV_WIDTH], WINDOW)
    vw_t = front(kvb[:, :, 5 * KV_WIDTH:6 * KV_WIDTH], WINDOW).transpose(0, 2, 1)
    pidx = np.arange(t + Q_BLOCK)
    e = (pidx[:, None] // SEL_BLOCK - Q_BLOCK // SEL_BLOCK == np.arange(LANES)[None, :])
    e = jnp.asarray(e.astype(np.float32), BF16)
    full = lambda arr: pl.BlockSpec((1,) + arr.shape[1:], lambda bi, h, i: (bi, 0, 0))
    const = lambda arr: pl.BlockSpec(arr.shape, lambda bi, h, i: (0, 0))
    return pl.pallas_call(
        _nsa_prompt_t_kernel,
        grid=(b, N_KV_HEADS, nqb),
        in_specs=[pl.BlockSpec(memory_space=pltpu.SMEM),
                  pl.BlockSpec((1, Q_BLOCK, NSA_WIDTH), lambda bi, h, i: (bi, i, 0)),
                  pl.BlockSpec((1, Q_BLOCK, LANES), lambda bi, h, i: (bi, i, 0)),
                  full(kcmp_p), full(vcmp_p), full(vcmp_t), const(selmap), const(selmap_t),
                  full(ks_p), full(vs_t), full(kw_p), full(vw_t), const(e)],
        out_specs=pl.BlockSpec((1, 1, Q_BLOCK, NSA_WIDTH), lambda bi, h, i: (h, bi, i, 0)),
        out_shape=jax.ShapeDtypeStruct((N_KV_HEADS, b, t, NSA_WIDTH), BF16),
        scratch_shapes=[pltpu.VMEM((LANES, GROUP * Q_BLOCK), F32),
                        pltpu.VMEM((SLC_NEAR, GROUP * Q_BLOCK), F32),
                        pltpu.VMEM((WIN_KEYS, GROUP * Q_BLOCK), F32)],
        compiler_params=_cparams(("arbitrary", "arbitrary", "arbitrary")),
        name="nsa_prompt",
    )(rel_bias, q, gn, kcmp_p, vcmp_p, vcmp_t, selmap, selmap_t, ks_p, vs_t, kw_p, vw_t, e)


def _nsa_prompt_pair_kernel(rb_ref, q_ref, gn_ref, kcmp_ref, vcmp_ref, vcmpt_ref, selmap_ref, selmapt_ref,
                            ks_ref, vst_ref, kw_ref, vwt_ref, et_ref, out_ref, nbc_ref, nbs_ref, nbw_ref, s_ref,
                            smax_ref):
    qb = pl.program_id(1)
    s0 = qb * Q_BLOCK
    gq = GROUP * Q_BLOCK
    heads = range(N_KV_HEADS)

    @pl.when(qb == 0)
    def _():
        for hk in heads:
            for g in range(GROUP):
                tab = lambda k, h=hk * GROUP + g: rb_ref[k, h]
                cols = slice(g * Q_BLOCK, (g + 1) * Q_BLOCK)
                cl = lax.broadcasted_iota(jnp.int32, (LANES, Q_BLOCK), 0)
                ql = lax.broadcasted_iota(jnp.int32, (LANES, Q_BLOCK), 1)
                dist = ql - CMP_STRIDE * (cl - CMP_FRONT) - (CMP_LEN - 1)
                ok = (dist >= 0) & (cl < CMP_NEAR)
                nbc_ref[hk, :, cols] = jnp.where(ok, _bias_from_scalars(_bucket(dist), tab), NEG_INF)
                kl = lax.broadcasted_iota(jnp.int32, (SLC_NEAR, Q_BLOCK), 0)
                ql = lax.broadcasted_iota(jnp.int32, (SLC_NEAR, Q_BLOCK), 1)
                dist = ql - kl + Q_BLOCK
                nbs_ref[hk, :, cols] = jnp.where(dist >= 0, _bias_from_scalars(_bucket(dist), tab), NEG_INF)
                kl = lax.broadcasted_iota(jnp.int32, (WIN_KEYS, Q_BLOCK), 0)
                ql = lax.broadcasted_iota(jnp.int32, (WIN_KEYS, Q_BLOCK), 1)
                dist = ql - kl + WINDOW
                ok = (dist >= 0) & (dist < WINDOW)
                nbw_ref[hk, :, cols] = jnp.where(ok, _bias_from_scalars(_bucket(dist), tab), NEG_INF)

    lane = lax.broadcasted_iota(jnp.int32, (Q_BLOCK, LANES), 1)
    qblk = q_ref[0].astype(F32)
    init = (jnp.full((1, gq), NEG_INF, F32), jnp.zeros((1, gq), F32), jnp.zeros((LANES, gq), F32))
    p0 = pl.multiple_of(s0, LANES)
    n_cmp_pad = kcmp_ref.shape[1] - LANES
    c8 = qb * (Q_BLOCK // CMP_STRIDE)
    near0 = pl.multiple_of(c8, 8)
    kc_far = kcmp_ref[0, CMP_FRONT:CMP_FRONT + n_cmp_pad, :].astype(BF16)
    kc_near = kcmp_ref[0, pl.ds(near0, LANES), :].astype(BF16)
    vct_near = vcmp_ref[0, pl.ds(near0, LANES), :].T.astype(BF16)
    sm_near_t = selmap_ref[pl.ds(near0, LANES), :].T.astype(BF16)
    kw = kw_ref[0, pl.ds(p0, WIN_KEYS), :]
    vwt = vwt_ref[0, :, pl.ds(p0, WIN_KEYS)]
    crow = lax.broadcasted_iota(jnp.int32, (n_cmp_pad, gq), 0)
    nrow = lax.broadcasted_iota(jnp.int32, (LANES, gq), 0)
    wrow = lax.broadcasted_iota(jnp.int32, (WIN_KEYS, gq), 0)
    krow = lax.broadcasted_iota(jnp.int32, (SLC_NEAR, gq), 0)
    n_sel = selmapt_ref.shape[0]
    blk = lax.broadcasted_iota(jnp.int32, (n_sel, Q_BLOCK), 0)
    cur = (s0 + lax.broadcasted_iota(jnp.int32, (n_sel, Q_BLOCK), 1)) // SEL_BLOCK
    forced = (blk == 0) | (blk == cur) | (blk == cur - 1)
    jcol = lax.broadcasted_iota(jnp.int32, (Q_BLOCK, n_sel), 1)
    far_blocks = (s0 - Q_BLOCK) // SEL_BLOCK

    def split_dot(a, p):
        hi = p.astype(BF16)
        return _dot(a, hi) + _dot(a, (p - hi.astype(F32)).astype(BF16))

    def before_far(hk):
        lmask = (lane // HEAD_DIM) == hk
        qp = jnp.concatenate(
            [jnp.where(lmask, qblk[:, g * LANES:(g + 1) * LANES], 0.0) for g in range(GROUP)], axis=0)
        qp = (qp * (HEAD_DIM ** -0.5)).astype(BF16)
        far_bias = jnp.concatenate(
            [jnp.full((1, Q_BLOCK), rb_ref[N_BUCKETS - 1, hk * GROUP + g], F32) for g in range(GROUP)], axis=1)

        s_far = jnp.where(crow < c8 - CMP_FRONT, _dot_t(kc_far, qp), NEG_INF)
        s_near = jnp.where(nrow >= CMP_FRONT - c8, _dot_t(kc_near, qp) + nbc_ref[hk], NEG_INF)
        m = jnp.maximum(jnp.max(s_far, axis=0, keepdims=True) + far_bias, jnp.max(s_near, axis=0, keepdims=True))
        p_far = jnp.exp(s_far - (m - far_bias))
        p_near = jnp.exp(s_near - m)
        l = jnp.sum(p_far, axis=0, keepdims=True) + jnp.sum(p_near, axis=0, keepdims=True)
        inv = jnp.where(m > 0.5 * NEG_INF, 1.0 / l, 0.0)
        pc_far = p_far * inv
        pc_near = p_near * inv
        o_cmp = _dot(vcmpt_ref[0], pc_far.astype(BF16)) + _dot(vct_near, pc_near.astype(BF16))

        s = jnp.where(wrow >= WINDOW - s0, _dot_t(kw, qp) + nbw_ref[hk], NEG_INF)
        o_win = _col_softmax_finish(_col_softmax_update(init, s, vwt))

        ps_far = sum(pc_far[:, g * Q_BLOCK:(g + 1) * Q_BLOCK] for g in range(GROUP))
        ps_near = sum(pc_near[:, g * Q_BLOCK:(g + 1) * Q_BLOCK] for g in range(GROUP))
        imp_t = split_dot(selmapt_ref[...], ps_far) + split_dot(sm_near_t, ps_near)
        val = jnp.where(blk > cur, NEG_INF, imp_t + FORCE_BONUS * forced.astype(F32))
        sel_t = jnp.zeros((n_sel, Q_BLOCK), F32)
        for _ in range(N_SELECT):
            mx = jnp.max(val, axis=0, keepdims=True)
            first = jnp.min(jnp.where(val == mx, blk, n_sel), axis=0, keepdims=True)
            hit = blk == first
            sel_t = jnp.where(hit, 1.0, sel_t)
            val = jnp.where(hit, -3e38, val)
        sel = sel_t.T > 0.5

        pen_far = jnp.where(sel & (jcol < far_blocks), 0.0, NEG_INF).astype(BF16)
        pen_near = jnp.where(sel, 0.0, NEG_INF).astype(BF16)
        q_far = jnp.concatenate([qp, _tile_rows(pen_far, GROUP)], axis=1)
        q_near = jnp.concatenate([qp, _tile_rows(pen_near, GROUP)], axis=1)
        return dict(far_bias=far_bias, o_cmp=o_cmp, o_win=o_win, q_far=q_far, q_near=q_near)

    pre = [before_far(hk) for hk in heads]
    n_tiles = (ks_ref.shape[1] - Q_BLOCK) // SLC_TILE

    def far_logits(kt):
        f0 = pl.multiple_of(Q_BLOCK + kt * SLC_TILE, LANES)
        k = jnp.concatenate([ks_ref[0, pl.ds(f0, SLC_TILE), :], et_ref[pl.ds(f0, SLC_TILE), :]], axis=1)
        return [_dot_t(k, pre[hk]["q_far"]) for hk in heads]

    def far_pair(i, states):
        for slot in range(2):
            kt = 2 * i + slot
            put_logits(1 - slot, jnp.minimum(kt + 1, n_tiles - 1))
            f0 = pl.multiple_of(Q_BLOCK + jnp.minimum(kt, n_tiles - 1) * SLC_TILE, LANES)
            vt = vst_ref[0, :, pl.ds(f0, SLC_TILE)]
            states = [_col_softmax_update(states[hk], s_ref[slot, hk], vt, smax=smax_ref[slot, hk, 0:1, :])
                      for hk in heads]
        return states

    def put_logits(slot, kt):
        for hk, s_new in enumerate(far_logits(kt)):
            s_ref[slot, hk] = s_new
            smax_ref[slot, hk] = jnp.broadcast_to(jnp.max(s_new, axis=0, keepdims=True), (8, gq))

    n_far = (qb + 2) // (SLC_TILE // Q_BLOCK)
    put_logits(0, 0)
    far_states = lax.fori_loop(0, (n_far + 1) // 2, far_pair, [init] * N_KV_HEADS)

    k_near = jnp.concatenate([ks_ref[0, pl.ds(p0, SLC_NEAR), :], et_ref[pl.ds(p0, SLC_NEAR), :]], axis=1)
    vt_near = vst_ref[0, :, pl.ds(p0, SLC_NEAR)]
    gst = jax.nn.sigmoid(gn_ref[0]).T
    outs = []
    for hk in heads:
        m_far, l_far, acc_far = far_states[hk]
        state = (m_far + pre[hk]["far_bias"], l_far, acc_far)
        s = jnp.where(krow >= Q_BLOCK - s0, _dot_t(k_near, pre[hk]["q_near"]) + nbs_ref[hk], NEG_INF)
        o_slc = _col_softmax_finish(_col_softmax_update(state, s, vt_near))
        per_g = []
        for g in range(GROUP):
            cols = slice(g * Q_BLOCK, (g + 1) * Q_BLOCK)
            c = 3 * (hk * GROUP + g)
            o = (gst[c:c + 1, :] * pre[hk]["o_cmp"][:, cols] + gst[c + 1:c + 2, :] * o_slc[:, cols]
                 + gst[c + 2:c + 3, :] * pre[hk]["o_win"][:, cols])
            per_g.append(o.T)
        outs.append(per_g)
    lane_h0 = lane < HEAD_DIM
    for g in range(GROUP):
        out_ref[0, :, g * LANES:(g + 1) * LANES] = jnp.where(lane_h0, outs[0][g], outs[1][g]).astype(out_ref.dtype)


def _nsa_prompt_pair(q, gn, kcmp, vcmp, kvb, vs_heads, vw_heads, rel_bias):
    b, t, _ = q.shape
    assert t % SLC_TILE == 0
    nqb = t // Q_BLOCK
    n_chunk = t // CMP_STRIDE
    n_cmp = n_chunk - CMP_RATIO + 1
    n_blocks = t // SEL_BLOCK
    assert n_blocks <= LANES
    cpad = ((0, 0), (CMP_FRONT, LANES - CMP_FRONT), (0, 0))
    kcmp_p = jnp.pad(kcmp, cpad)
    vcmp_p = jnp.pad(vcmp, cpad)
    vcmp_t = vcmp.transpose(0, 2, 1).astype(BF16)
    sm = _sel_map(n_chunk, n_cmp, LANES, n_blocks)
    selmap = jnp.asarray(np.pad(sm, ((CMP_FRONT, LANES - CMP_FRONT), (0, 0))), F32)
    selmap_t = jnp.asarray(sm.T, BF16)
    front = lambda z, n: jnp.pad(z, ((0, 0), (n, 0), (0, 0)))
    keys_front = lambda z, n: jnp.pad(z.reshape(b, KV_WIDTH, t).astype(BF16), ((0, 0), (0, 0), (n, 0)))
    ks_p = front(kvb[:, :, 2 * KV_WIDTH:3 * KV_WIDTH], Q_BLOCK)
    vs_t = keys_front(vs_heads, Q_BLOCK)
    kw_p = front(kvb[:, :, 4 * KV_WIDTH:5 * KV_WIDTH], WINDOW)
    vw_t = keys_front(vw_heads, WINDOW)
    pidx = np.arange(t + Q_BLOCK)
    e = (pidx[:, None] // SEL_BLOCK - Q_BLOCK // SEL_BLOCK == np.arange(LANES)[None, :])
    e = jnp.asarray(e.astype(np.float32), BF16)
    full = lambda arr: pl.BlockSpec((1,) + arr.shape[1:], lambda bi, i: (bi, 0, 0))
    const = lambda arr: pl.BlockSpec(arr.shape, lambda bi, i: (0, 0))
    return pl.pallas_call(
        _nsa_prompt_pair_kernel,
        grid=(b, nqb),
        in_specs=[pl.BlockSpec(memory_space=pltpu.SMEM),
                  pl.BlockSpec((1, Q_BLOCK, NSA_WIDTH), lambda bi, i: (bi, i, 0)),
                  pl.BlockSpec((1, Q_BLOCK, LANES), lambda bi, i: (bi, i, 0)),
                  full(kcmp_p), full(vcmp_p), full(vcmp_t), const(selmap), const(selmap_t),
                  full(ks_p), full(vs_t), full(kw_p), full(vw_t), const(e)],
        out_specs=pl.BlockSpec((1, Q_BLOCK, NSA_WIDTH), lambda bi, i: (bi, i, 0)),
        out_shape=jax.ShapeDtypeStruct((b, t, NSA_WIDTH), BF16),
        scratch_shapes=[pltpu.VMEM((N_KV_HEADS, LANES, GROUP * Q_BLOCK), F32),
                        pltpu.VMEM((N_KV_HEADS, SLC_NEAR, GROUP * Q_BLOCK), F32),
                        pltpu.VMEM((N_KV_HEADS, WIN_KEYS, GROUP * Q_BLOCK), F32),
                        pltpu.VMEM((2, N_KV_HEADS, SLC_TILE, GROUP * Q_BLOCK), F32),
                        pltpu.VMEM((2, N_KV_HEADS, 8, GROUP * Q_BLOCK), F32)],
        compiler_params=_cparams(("arbitrary", "arbitrary")),
        name="nsa_prompt",
    )(rel_bias, q, gn, kcmp_p, vcmp_p, vcmp_t, selmap, selmap_t, ks_p, vs_t, kw_p, vw_t, e)


ROWS_PER_TOKEN = 8


def _nsa_sample_cmp_kernel(q_ref, kcmp_ref, vcmp_ref, selmap_ref, rowtab_ref, ocmp_ref, idx_ref, *,
                           past_len, n_tok, n_cmp, n_blocks):
    rows = n_tok * ROWS_PER_TOKEN
    n_cpad = kcmp_ref.shape[1]
    ridx = lax.broadcasted_iota(jnp.int32, (rows, n_cpad), 0)
    ccol = lax.broadcasted_iota(jnp.int32, (rows, n_cpad), 1)
    dist = past_len + ridx // ROWS_PER_TOKEN - (CMP_STRIDE * ccol + CMP_LEN - 1)
    bucket = _bucket(dist)
    valid = (dist >= 0) & (ccol < n_cmp) & (ridx % ROWS_PER_TOKEN < GROUP)
    n_bpad = selmap_ref.shape[1]
    blk = lax.broadcasted_iota(jnp.int32, (ROWS_PER_TOKEN, n_bpad), 1)
    cur = (past_len + lax.broadcasted_iota(jnp.int32, (ROWS_PER_TOKEN, n_bpad), 0)) // SEL_BLOCK
    forced = (blk == 0) | (blk == cur) | (blk == cur - 1)
    olane = lax.broadcasted_iota(jnp.int32, (ROWS_PER_TOKEN, LANES), 1)
    biases = [_bias_from_rows(bucket, rowtab_ref[hk]) for hk in range(N_KV_HEADS)]
    for bi, hk in [(bi, hk) for bi in range(kcmp_ref.shape[0]) for hk in range(N_KV_HEADS)]:
        kc = kcmp_ref[bi].astype(BF16)
        vc = vcmp_ref[bi].astype(BF16)
        qp = (q_ref[bi, hk].astype(F32) * (HEAD_DIM ** -0.5)).astype(BF16)
        s = _dot_t(qp, kc) + biases[hk]
        s = jnp.where(valid, s, NEG_INF)
        m = jnp.max(s, axis=-1, keepdims=True)
        p = jnp.where(valid, jnp.exp(s - m), 0.0)
        l = jnp.sum(p, axis=-1, keepdims=True)
        pc = p * jnp.where(l > 0.0, 1.0 / jnp.where(l > 0.0, l, 1.0), 0.0)
        ocmp_ref[bi, hk] = _dot(pc.astype(BF16), vc)[:, hk * HEAD_DIM:(hk + 1) * HEAD_DIM]
        ps = jnp.concatenate(
            [jnp.sum(pc[i * ROWS_PER_TOKEN:(i + 1) * ROWS_PER_TOKEN], axis=0, keepdims=True) for i in range(n_tok)]
            + [jnp.zeros((ROWS_PER_TOKEN - n_tok, n_cpad), F32)], axis=0)
        imp = _split_dot(ps, selmap_ref[...])
        val = jnp.where((blk > cur) | (blk >= n_blocks), NEG_INF, imp + FORCE_BONUS * forced.astype(F32))
        picks = jnp.zeros((ROWS_PER_TOKEN, LANES), jnp.int32)
        for n in range(N_SELECT):
            mx = jnp.max(val, axis=-1, keepdims=True)
            first = jnp.min(jnp.where(val == mx, blk, n_bpad), axis=-1, keepdims=True)
            picks = jnp.where(olane == n, first, picks)
            val = jnp.where(blk == first, -3e38, val)
        idx_ref[bi, hk] = picks


def _nsa_sample_cmp(qs, kcmp, vcmp, rowtab, past_len, n_tok, n_cmp, n_blocks):
    nb = qs.shape[0]
    bps = next(d for d in (4, 2, 1) if nb % d == 0)
    n_cpad = kcmp.shape[1]
    n_bpad = -(-n_blocks // LANES) * LANES
    selmap = jnp.asarray(_sel_map(n_cpad, n_cmp, n_bpad, n_blocks), BF16)
    rows = n_tok * ROWS_PER_TOKEN
    return pl.pallas_call(
        functools.partial(_nsa_sample_cmp_kernel, past_len=past_len, n_tok=n_tok, n_cmp=n_cmp, n_blocks=n_blocks),
        grid=(nb // bps,),
        in_specs=[pl.BlockSpec((bps, N_KV_HEADS, rows, LANES), lambda b: (b, 0, 0, 0)),
                  pl.BlockSpec((bps, n_cpad, LANES), lambda b: (b, 0, 0)),
                  pl.BlockSpec((bps, n_cpad, LANES), lambda b: (b, 0, 0)),
                  pl.BlockSpec(selmap.shape, lambda b: (0, 0)),
                  pl.BlockSpec(rowtab.shape, lambda b: (0, 0, 0))],
        out_specs=[pl.BlockSpec((bps, N_KV_HEADS, rows, HEAD_DIM), lambda b: (b, 0, 0, 0)),
                   pl.BlockSpec((bps, N_KV_HEADS, ROWS_PER_TOKEN, LANES), lambda b: (b, 0, 0, 0))],
        out_shape=[jax.ShapeDtypeStruct((nb, N_KV_HEADS, rows, HEAD_DIM), F32),
                   jax.ShapeDtypeStruct((nb, N_KV_HEADS, ROWS_PER_TOKEN, LANES), jnp.int32)],
        compiler_params=_cparams(("arbitrary",)),
        name="nsa_sample_cmp",
    )(qs, kcmp, vcmp, selmap, rowtab)


def _nsa_sample_slc_kernel(idx_ref, pt_ref, ck_ref, cv_ref, q_ref, ocmp_ref, gn_ref, rowtab_ref, newk_ref, newv_ref,
                           swk_ref, swv_ref, nwk_ref, nwv_ref, out_ref, kbuf_ref, vbuf_ref, sem_ref, *,
                           past_len, n_tok, n_cache_blocks, npages):
    b = pl.program_id(0)
    hk = pl.program_id(1)
    step = b * N_KV_HEADS + hk
    last = pl.num_programs(0) * N_KV_HEADS - 1
    slot = step % 2
    per_step = n_tok * N_SELECT
    pages_per_block = PAGE // SEL_BLOCK

    def block_copies(st, i, sl):
        j = jnp.minimum(idx_ref[st * per_step + i], n_cache_blocks - 1)
        page = pt_ref[(st // N_KV_HEADS) * npages + j // pages_per_block]
        head = st % N_KV_HEADS
        return (pltpu.make_async_copy(ck_ref.at[page, head], kbuf_ref.at[sl, i], sem_ref.at[sl]),
                pltpu.make_async_copy(cv_ref.at[page, head], vbuf_ref.at[sl, i], sem_ref.at[sl]))

    @pl.when(step == 0)
    def _():
        for i in range(per_step):
            for c in block_copies(0, i, 0):
                c.start()

    nxt = jnp.minimum(step + 1, last)
    for i in range(per_step):
        for c in block_copies(nxt, i, 1 - slot):
            c.start()
    for i in range(per_step):
        for c in block_copies(step, i, slot):
            c.wait()

    rowtab = rowtab_ref[0]
    rvalid = lax.broadcasted_iota(jnp.int32, (ROWS_PER_TOKEN, 1), 0) < GROUP
    init = (jnp.full((ROWS_PER_TOKEN, 1), NEG_INF, F32), jnp.zeros((ROWS_PER_TOKEN, 1), F32),
            jnp.zeros((ROWS_PER_TOKEN, HEAD_DIM), F32))
    n_win = swk_ref.shape[3]
    kw = jnp.concatenate([swk_ref[0, 0], nwk_ref[0, 0]], axis=1).astype(BF16)
    vw = jnp.concatenate([swv_ref[0, 0], nwv_ref[0, 0]], axis=1).astype(BF16)
    wl = n_win + nwk_ref.shape[3]
    wpos = past_len - n_win + lax.broadcasted_iota(jnp.int32, (ROWS_PER_TOKEN, wl), 1)
    newk = newk_ref[0, 0]
    newv = newv_ref[0, 0]

    for tok in range(n_tok):
        t = past_len + tok
        qp = (q_ref[0, 0, tok].astype(F32) * (HEAD_DIM ** -0.5)).astype(BF16)
        page_row = lax.broadcasted_iota(jnp.int32, (ROWS_PER_TOKEN, PAGE), 1)
        bias_far = jnp.broadcast_to(rowtab[:, N_BUCKETS - 1:N_BUCKETS], (ROWS_PER_TOKEN, PAGE))
        bias_last = _bias_from_rows(_bucket(t - ((npages - 1) * PAGE + page_row)), rowtab)
        bias_new = _bias_from_rows(_bucket(t - (npages * PAGE + page_row)), rowtab)
        assert BUCKET_THR[-1] <= PAGE + 1
        kts, vts, biases = [], [], []
        for n in range(N_SELECT):
            i = tok * N_SELECT + n
            blk = idx_ref[step * per_step + i]
            is_new = blk >= n_cache_blocks
            page = blk // pages_per_block
            kts.append(jnp.where(is_new, newk, kbuf_ref[slot, i]).astype(BF16))
            vts.append(jnp.where(is_new, newv, vbuf_ref[slot, i]).astype(BF16))
            bias = jnp.where(is_new, bias_new, jnp.where(page == npages - 1, bias_last, bias_far))
            ok = (page_row // SEL_BLOCK == blk % pages_per_block) & (page * PAGE + page_row <= t)
            biases.append(jnp.where(ok, bias, NEG_INF))
        s = _dot(qp, jnp.concatenate(kts, axis=1)) + jnp.concatenate(biases, axis=1)
        mask = (s > 0.5 * NEG_INF) & rvalid
        o_slc = _softmax_finish(_softmax_update(init, jnp.where(mask, s, NEG_INF), jnp.concatenate(vts, axis=1),
                                                mask=mask, v_transposed=True))
        wdist = t - wpos
        s = _dot(qp, kw) + _bias_from_rows(_bucket(wdist), rowtab)
        mask = (wdist >= 0) & (wdist < WINDOW) & (wpos >= 0) & rvalid
        o_win = _softmax_finish(_softmax_update(init, jnp.where(mask, s, NEG_INF), vw, mask=mask, v_transposed=True))
        gs = jax.nn.sigmoid(gn_ref[0, 0, tok])
        out_ref[0, 0, tok] = gs[:, 0:1] * ocmp_ref[0, 0, tok] + gs[:, 1:2] * o_slc + gs[:, 2:3] * o_win

    @pl.when(step == last)
    def _():
        for i in range(per_step):
            for c in block_copies(nxt, i, 1 - slot):
                c.wait()


def _nsa_sample_slc(idx, page_table, cache_k, cache_v, q3, ocmp, gn3, rowtab, newk, newv, swk, swv, nwk, nwv,
                    past_len, n_tok):
    nb, npages = page_table.shape
    halves = PAGE // SEL_BLOCK
    n_cache_blocks = npages * halves
    ck = cache_k.transpose(0, 2, 3, 1)
    cv = cache_v.transpose(0, 2, 3, 1)

    tile = lambda w: pl.BlockSpec((1, 1, n_tok, ROWS_PER_TOKEN, w), lambda b, h, i, p: (b, h, 0, 0, 0))
    per_bh = lambda arr: pl.BlockSpec((1, 1) + arr.shape[2:], lambda b, h, i, p: (b, h, 0, 0))
    grid_spec = pltpu.PrefetchScalarGridSpec(
        num_scalar_prefetch=2,
        grid=(nb, N_KV_HEADS),
        in_specs=[pl.BlockSpec(memory_space=pl.ANY), pl.BlockSpec(memory_space=pl.ANY),
                  tile(HEAD_DIM), tile(HEAD_DIM), tile(LANES),
                  pl.BlockSpec((1, ROWS_PER_TOKEN, N_BUCKETS), lambda b, h, i, p: (h, 0, 0)),
                  per_bh(newk), per_bh(newv), per_bh(swk), per_bh(swv), per_bh(nwk), per_bh(nwv)],
        out_specs=tile(HEAD_DIM),
        scratch_shapes=[pltpu.VMEM((2, n_tok * N_SELECT, HEAD_DIM, PAGE), F32),
                        pltpu.VMEM((2, n_tok * N_SELECT, HEAD_DIM, PAGE), F32),
                        pltpu.SemaphoreType.DMA((2,))],
    )
    return pl.pallas_call(
        functools.partial(_nsa_sample_slc_kernel, past_len=past_len, n_tok=n_tok, n_cache_blocks=n_cache_blocks,
                          npages=npages),
        grid_spec=grid_spec,
        out_shape=jax.ShapeDtypeStruct((nb, N_KV_HEADS, n_tok, ROWS_PER_TOKEN, HEAD_DIM), F32),
        compiler_params=_cparams(("arbitrary", "arbitrary")),
        name="nsa_sample_slc",
    )(idx.reshape(-1), page_table.reshape(-1), ck, cv, q3, ocmp, gn3, rowtab, newk, newv, swk, swv, nwk, nwv)


def _merge_tail(x, a, c, gates, wn_ref, wc_ref, wo_ref):
    ga = gates[:, :D_MODEL]
    gb = gates[:, D_MODEL:]
    m = (jax.nn.sigmoid(ga) * _dot(a, wn_ref[...])
         + jax.nn.sigmoid(gb) * _dot(c.astype(BF16), wc_ref[...]))
    return x + _dot(m.astype(BF16), wo_ref[...])


def _merge_prompt_kernel(x_ref, a_ref, conv_ref, halo_ref, gate_ref, cw_ref, cb_ref, wn_ref, wc_ref, wo_ref,
                         out_ref, cst_ref, *, tiles_per_batch):
    i = pl.program_id(0)
    first = i % tiles_per_batch == 0
    conv = conv_ref[...]
    bg = conv[:, :CONV_DIM]
    u = conv[:, CONV_DIM:2 * CONV_DIM] * conv[:, 2 * CONV_DIM:]
    halo = halo_ref[...]
    uh = jnp.where(first, 0.0, halo[:, CONV_DIM:2 * CONV_DIM] * halo[:, 2 * CONV_DIM:])
    tm = u.shape[0]
    row = lax.broadcasted_iota(jnp.int32, (tm, CONV_DIM), 0)
    u1 = jnp.where(row == 0, uh[7:8], pltpu.roll(u, 1, axis=0))
    u2 = pltpu.roll(u, 2, axis=0)
    u2 = jnp.where(row == 0, uh[6:7], jnp.where(row == 1, uh[7:8], u2))
    cw = cw_ref[...]
    y = ((cb_ref[...] + cw[0:1] * u2) + cw[1:2] * u1) + cw[2:3] * u
    c = bg * y
    out_ref[...] = _merge_tail(x_ref[...], a_ref[...], c, gate_ref[...], wn_ref, wc_ref, wo_ref)
    cst_ref[0] = u[tm - 8:, :]


def _merge_prompt(x2d, a2, conv3, gates, conv_w, conv_b, wn, wc, wo, t):
    n = x2d.shape[0]
    tm = min(512, t)
    assert t % tm == 0
    tpb = t // tm
    row = lambda w: pl.BlockSpec((tm, w), lambda i: (i, 0))
    const = lambda arr: pl.BlockSpec(arr.shape, lambda i: (0,) * arr.ndim)
    return pl.pallas_call(
        functools.partial(_merge_prompt_kernel, tiles_per_batch=tpb),
        grid=(n // tm,),
        in_specs=[row(D_MODEL),
                  row(NSA_WIDTH),
                  row(3 * CONV_DIM),
                  pl.BlockSpec((8, 3 * CONV_DIM), lambda i: (jnp.maximum(i * (tm // 8) - 1, 0), 0)),
                  row(2 * D_MODEL),
                  const(conv_w), const(conv_b), const(wn), const(wc), const(wo)],
        out_specs=[row(D_MODEL), pl.BlockSpec((1, 8, CONV_DIM), lambda i: (i // tpb, 0, 0))],
        out_shape=[jax.ShapeDtypeStruct((n, D_MODEL), F32),
                   jax.ShapeDtypeStruct((n // t, 8, CONV_DIM), F32)],
        compiler_params=_cparams(("arbitrary",)),
        name="merge_prompt",
    )(x2d, a2, conv3, conv3, gates, conv_w, conv_b, wn, wc, wo)


def _merge_sample_kernel(x_ref, a_ref, conv_ref, past_ref, gate_ref, cw_ref, cb_ref, wn_ref, wc_ref, wo_ref,
                         out_ref, cst_ref, *, n_tok, nb):
    conv = conv_ref[...]
    bg = conv[:, :CONV_DIM]
    u = conv[:, CONV_DIM:2 * CONV_DIM] * conv[:, 2 * CONV_DIM:]
    up = jnp.concatenate([past_ref[...], u], axis=0)
    cw = cw_ref[...]
    y = cb_ref[...]
    for k in range(CONV_WIDTH):
        y = y + cw[k:k + 1] * up[k * nb:(k + n_tok) * nb]
    c = bg * y
    out_ref[...] = _merge_tail(x_ref[...], a_ref[...].astype(BF16), c, gate_ref[...], wn_ref, wc_ref, wo_ref)
    cst_ref[...] = up[n_tok * nb:]


def _merge_sample(x_tm, a_tm, conv_tm, past_tm, gates_tm, conv_w, conv_b, wn, wc, wo, n_tok, nb):
    n = x_tm.shape[0]
    full = lambda arr: pl.BlockSpec(arr.shape, lambda i: (0,) * arr.ndim)
    args = (x_tm, a_tm, conv_tm, past_tm, gates_tm, conv_w, conv_b, wn, wc, wo)
    return pl.pallas_call(
        functools.partial(_merge_sample_kernel, n_tok=n_tok, nb=nb),
        grid=(1,),
        in_specs=[full(a) for a in args],
        out_specs=[pl.BlockSpec((n, D_MODEL), lambda i: (0, 0)),
                   pl.BlockSpec(((CONV_WIDTH - 1) * nb, CONV_DIM), lambda i: (0, 0))],
        out_shape=[jax.ShapeDtypeStruct((n, D_MODEL), F32),
                   jax.ShapeDtypeStruct(((CONV_WIDTH - 1) * nb, CONV_DIM), F32)],
        compiler_params=_cparams(("arbitrary",)),
        name="merge_sample",
    )(*args)


def _rms(x, g):
    return (x * lax.rsqrt(jnp.mean(x * x, axis=-1, keepdims=True) + RMS_EPS)) * g


def _mlp_kernel(x_ref, g_ref, wu_ref, wd_ref, gf_ref, out_ref, h_ref, acc_ref):
    j = pl.program_id(1)

    @pl.when(j == 0)
    def _():
        h_ref[...] = _rms(x_ref[...], g_ref[...]).astype(BF16)

    up = jnp.square(jnp.maximum(_dot(h_ref[...], wu_ref[...]), 0.0)).astype(BF16)
    part = _dot(up, wd_ref[...])

    @pl.when(j == 0)
    def _():
        acc_ref[...] = part

    @pl.when(j > 0)
    def _():
        acc_ref[...] += part

    @pl.when(j == pl.num_programs(1) - 1)
    def _():
        out_ref[...] = _rms(x_ref[...] + acc_ref[...], gf_ref[...])


def _mlp(x2d, g_mlp, w_up, w_down, g_final):
    n = x2d.shape[0]
    tm = min(1024, n)
    tf = 1024
    assert n % tm == 0 and D_FF % tf == 0
    return pl.pallas_call(
        _mlp_kernel,
        grid=(n // tm, D_FF // tf),
        in_specs=[pl.BlockSpec((tm, D_MODEL), lambda i, j: (i, 0)),
                  pl.BlockSpec((1, D_MODEL), lambda i, j: (0, 0)),
                  pl.BlockSpec((D_MODEL, tf), lambda i, j: (0, j)),
                  pl.BlockSpec((tf, D_MODEL), lambda i, j: (j, 0)),
                  pl.BlockSpec((1, D_MODEL), lambda i, j: (0, 0))],
        out_specs=pl.BlockSpec((tm, D_MODEL), lambda i, j: (i, 0)),
        out_shape=jax.ShapeDtypeStruct((n, D_MODEL), F32),
        scratch_shapes=[pltpu.VMEM((tm, D_MODEL), BF16), pltpu.VMEM((tm, D_MODEL), F32)],
        compiler_params=_cparams(("arbitrary", "arbitrary")),
        name="mlp",
    )(x2d, g_mlp, w_up, w_down, g_final)


def _kv_rows(z, lead):
    return z.reshape(lead + (N_KV_HEADS, HEAD_DIM))


def _layer(xp, xs, caches, states, page_table, w, rel_bias):
    (cache_cmp_k, cache_cmp_v, cache_slc_k, cache_slc_v) = caches
    (state_win_k, state_win_v, state_conv) = states
    nbp, t, _ = xp.shape
    nbs, n_tok, _ = xs.shape
    npages = page_table.shape[1]
    past_len = npages * PAGE
    assert t % PAGE == 0 and n_tok <= GROUP and state_win_k.shape[1] >= n_tok

    w_all = _prep_w_in(w["w_in"])
    g_attn = w["g_attn"].reshape(1, D_MODEL)
    wn = w["w_nsa_out"].reshape(N_KV_HEADS, GROUP, HEAD_DIM, D_MODEL).transpose(1, 0, 2, 3)
    wn = wn.reshape(NSA_WIDTH, D_MODEL).astype(BF16)
    wc = w["w_conv_out"].astype(BF16)
    wo = w["w_o"].astype(BF16)
    wu = w["w_up"].astype(BF16)
    wd = w["w_down"].astype(BF16)
    g_mlp = w["g_mlp"].reshape(1, D_MODEL)
    g_final = w["g_final"].reshape(1, D_MODEL)
    conv_w = w["conv_w"]
    conv_b = w["conv_b"].reshape(1, CONV_DIM)
    cmp_k = (w["cmp_pe_k"], w["cmp_w1_k"], w["cmp_w2_k"])
    cmp_v = (w["cmp_pe_v"], w["cmp_w1_v"], w["cmp_w2_v"])

    n_p = nbp * t
    (q, kc, vc, *kv_heads, kvb, gn, conv3, gates) = _proj(xp.reshape(n_p, D_MODEL), g_attn, w_all, True, t)
    ident = jnp.arange(nbp * (t // PAGE), dtype=jnp.int32).reshape(nbp, t // PAGE)
    kcmp = _compress(kc.reshape(n_p // PAGE, PAGE, KV_WIDTH), 0, ident, *cmp_k)
    vcmp = _compress(vc.reshape(n_p // PAGE, PAGE, KV_WIDTH), 0, ident, *cmp_v)
    a2 = _nsa_prompt_pair(q.reshape(nbp, t, NSA_WIDTH), gn.reshape(nbp, t, LANES), kcmp, vcmp,
                          kvb.reshape(nbp, t, 6 * KV_WIDTH), kv_heads[3], kv_heads[5], rel_bias)
    x1, cst = _merge_prompt(xp.reshape(n_p, D_MODEL), a2.reshape(n_p, NSA_WIDTH),
                            conv3, gates, conv_w, conv_b, wn, wc, wo, t)
    yp = _mlp(x1, g_mlp, wu, wd, g_final).reshape(nbp, t, D_MODEL)
    win_p = min(WINDOW, t)
    back = lambda z: z.transpose(0, 3, 1, 2)
    new_p = tuple(back(z) for z in kv_heads[:4]) + (
        back(kv_heads[4][..., t - win_p:]), back(kv_heads[5][..., t - win_p:]), cst[:, 8 - (CONV_WIDTH - 1):, :])

    n_s = nbs * n_tok
    (qs, kcs, vcs, kss, vss, kws, vws, _, gns, conv3s, gatess) = _proj(xs.reshape(n_s, D_MODEL), g_attn, w_all, False)
    n_rows = past_len + n_tok
    n_chunk = n_rows // CMP_STRIDE
    assert n_chunk == past_len // CMP_STRIDE, "new rows must not complete a compression chunk"
    n_cmp = n_chunk - CMP_RATIO + 1
    n_blocks = -(-n_rows // SEL_BLOCK)
    kcmp_s, vcmp_s = _compress_t((cache_cmp_k, cache_cmp_v), page_table, (cmp_k, cmp_v))

    lane = np.arange(LANES)
    q5 = qs.reshape(nbs, n_tok, GROUP, LANES)
    q3 = jnp.stack([jnp.where(jnp.asarray((lane // HEAD_DIM) == h), q5, jnp.zeros_like(q5))
                    for h in range(N_KV_HEADS)], axis=1)
    q3 = jnp.pad(q3, ((0, 0), (0, 0), (0, 0), (0, ROWS_PER_TOKEN - GROUP), (0, 0)))
    tab = rel_bias.T.reshape(N_KV_HEADS, GROUP, N_BUCKETS)
    rowtab3 = jnp.pad(tab, ((0, 0), (0, ROWS_PER_TOKEN - GROUP), (0, 0)))
    rowtab = jnp.tile(rowtab3, (1, n_tok, 1))
    ocmp, idx = _nsa_sample_cmp(q3.reshape(nbs, N_KV_HEADS, n_tok * ROWS_PER_TOKEN, LANES), kcmp_s, vcmp_s,
                                rowtab, past_len, n_tok, n_cmp, n_blocks)
    idx = idx[:, :, :n_tok, :N_SELECT]
    gn3 = gns[:, :3 * N_HEADS].reshape(nbs, n_tok, N_KV_HEADS, GROUP, 3).transpose(0, 2, 1, 3, 4)
    gn3 = jnp.pad(gn3, ((0, 0), (0, 0), (0, 0), (0, ROWS_PER_TOKEN - GROUP), (0, LANES - 3)))
    q3s = jnp.pad(qs.reshape(nbs, n_tok, GROUP, N_KV_HEADS, HEAD_DIM).transpose(0, 3, 1, 2, 4),
                  ((0, 0), (0, 0), (0, 0), (0, ROWS_PER_TOKEN - GROUP), (0, 0)))
    ocmp5 = ocmp.reshape(nbs, N_KV_HEADS, n_tok, ROWS_PER_TOKEN, HEAD_DIM)
    new_t = lambda z: jnp.pad(z.reshape(nbs, n_tok, N_KV_HEADS, HEAD_DIM).transpose(0, 2, 3, 1),
                              ((0, 0), (0, 0), (0, 0), (0, LANES - n_tok)))
    a_s = _nsa_sample_slc(idx, page_table, cache_slc_k, cache_slc_v, q3s, ocmp5, gn3, rowtab3,
                          new_t(kss), new_t(vss),
                          state_win_k.transpose(0, 2, 3, 1), state_win_v.transpose(0, 2, 3, 1),
                          new_t(kws), new_t(vws), past_len, n_tok)
    a_tm = a_s[:, :, :, :GROUP, :].transpose(2, 0, 3, 1, 4).reshape(n_s, NSA_WIDTH)
    to_tm = lambda z: z.reshape(nbs, n_tok, -1).transpose(1, 0, 2).reshape(n_s, -1)
    past_tm = state_conv.transpose(1, 0, 2).reshape((CONV_WIDTH - 1) * nbs, CONV_DIM)
    x1s, csts = _merge_sample(to_tm(xs), a_tm, to_tm(conv3s), past_tm, to_tm(gatess),
                              conv_w, conv_b, wn, wc, wo, n_tok, nbs)
    ys = _mlp(x1s, g_mlp, wu, wd, g_final).reshape(n_tok, nbs, D_MODEL).transpose(1, 0, 2)
    win_buf = state_win_k.shape[1]
    kw_all = jnp.concatenate([state_win_k, _kv_rows(kws, (nbs, n_tok))], axis=1)[:, -win_buf:]
    vw_all = jnp.concatenate([state_win_v, _kv_rows(vws, (nbs, n_tok))], axis=1)[:, -win_buf:]
    new_s = tuple(_kv_rows(z, (nbs, n_tok)) for z in (kcs, vcs, kss, vss)) + (
        kw_all, vw_all, csts.reshape(CONV_WIDTH - 1, nbs, CONV_DIM).transpose(1, 0, 2))
    return yp, ys, new_p, new_s


def kernel(x_prompt, x_sample, cache_cmp_k, cache_cmp_v, cache_slc_k, cache_slc_v, state_win_k, state_win_v,
           state_conv, page_table, g_attn, w_in, cmp_pe_k, cmp_w1_k, cmp_w2_k, cmp_pe_v, cmp_w1_v, cmp_w2_v,
           conv_w, conv_b, w_nsa_out, w_conv_out, w_o, g_mlp, w_up, w_down, rel_bias, g_final):
    depth = w_in.shape[0]
    assert depth == 1, "the final norm is fused into the last layer's MLP kernel"
    names = ("g_attn", "w_in", "cmp_pe_k", "cmp_w1_k", "cmp_w2_k", "cmp_pe_v", "cmp_w1_v", "cmp_w2_v",
             "conv_w", "conv_b", "w_nsa_out", "w_conv_out", "w_o", "g_mlp", "w_up", "w_down")
    vals = (g_attn, w_in, cmp_pe_k, cmp_w1_k, cmp_w2_k, cmp_pe_v, cmp_w1_v, cmp_w2_v,
            conv_w, conv_b, w_nsa_out, w_conv_out, w_o, g_mlp, w_up, w_down)
    l = 0
    w = {k: v[l] for k, v in zip(names, vals)}
    w["g_final"] = g_final
    yp, ys, new_p, new_s = _layer(
        x_prompt, x_sample,
        (cache_cmp_k[l], cache_cmp_v[l], cache_slc_k[l], cache_slc_v[l]),
        (state_win_k[l], state_win_v[l], state_conv[l]), page_table, w, rel_bias)
    return (yp, ys) + tuple(z[None] for z in new_p) + tuple(z[None] for z in new_s)
```

```python
import functools
import math

import numpy as np
import jax
import jax.numpy as jnp
from jax import lax
from jax.experimental import pallas as pl
from jax.experimental.pallas import tpu as pltpu

F32 = jnp.float32
BF16 = jnp.bfloat16

D_MODEL = 1024
N_HEADS = 8
HEAD_DIM = 64
N_KV_HEADS = 2
GROUP = N_HEADS // N_KV_HEADS
KV_WIDTH = N_KV_HEADS * HEAD_DIM
NSA_WIDTH = N_HEADS * HEAD_DIM
CMP_LEN = 32
CMP_STRIDE = 16
CMP_RATIO = CMP_LEN // CMP_STRIDE
CMP_HIDDEN = 256
SEL_BLOCK = 64
N_SELECT = 16
WINDOW = 512
CONV_DIM = 512
CONV_WIDTH = 3
D_FF = 4 * D_MODEL
N_BUCKETS = 32
MAX_DISTANCE = 128
Q_BLOCK = 128
PAGE = 128
RMS_EPS = 1e-6
NEG_INF = -1e30
FORCE_BONUS = 1e4
N_FORCED = 3
LANES = 128
VMEM_LIMIT = 56 * 1024 * 1024


def _bucket_thresholds():
    max_exact = N_BUCKETS // 2
    n = np.arange(max_exact, 4 * MAX_DISTANCE).astype(np.float32)
    large = max_exact + (np.log(n / np.float32(max_exact)) / np.float32(math.log(MAX_DISTANCE / max_exact))
                         * np.float32(N_BUCKETS - max_exact)).astype(np.int32)
    large = np.minimum(large, N_BUCKETS - 1)
    thr = []
    for k in range(max_exact + 1, N_BUCKETS):
        thr.append(int(np.arange(max_exact, 4 * MAX_DISTANCE)[np.argmax(large >= k)]))
    return tuple(thr)


BUCKET_THR = _bucket_thresholds()


def _bucket(dist):
    n = jnp.maximum(dist, 0)
    big = jnp.full(n.shape, N_BUCKETS // 2, jnp.int32)
    for thr in BUCKET_THR:
        big = big + (n >= thr).astype(jnp.int32)
    return jnp.where(n < N_BUCKETS // 2, n, big)


def _bias_from_scalars(bucket, table_fn):
    out = jnp.zeros(bucket.shape, F32)
    for k in range(N_BUCKETS):
        out = jnp.where(bucket == k, table_fn(k), out)
    return out


def _bias_from_rows(bucket, rowtab):
    out = jnp.zeros(bucket.shape, F32)
    for k in range(N_BUCKETS):
        out = jnp.where(bucket == k, rowtab[:, k:k + 1], out)
    return out


def _dot(a, b):
    return jnp.dot(a, b, preferred_element_type=F32)


def _dot_t(a, b):
    return lax.dot_general(a, b, (((1,), (1,)), ((), ())), preferred_element_type=F32)


def _split_dot(p, m):
    hi = p.astype(BF16)
    lo = (p - hi.astype(F32)).astype(BF16)
    return _dot(hi, m) + _dot(lo, m)


def _cparams(sem):
    return pltpu.CompilerParams(dimension_semantics=sem, vmem_limit_bytes=VMEM_LIMIT)


PROJ_WIDTHS = (NSA_WIDTH, 6 * KV_WIDTH, LANES, 3 * CONV_DIM, 2 * D_MODEL)
PROJ_COLS = sum(PROJ_WIDTHS)


def _proj_kernel(x_ref, g_ref, w_ref, *out_refs, head_rows):
    q_ref, kvb_ref, gn_ref, conv_ref, gate_ref = out_refs[0], *out_refs[-4:]
    kv_refs = out_refs[1:-4]
    x = x_ref[...]
    y = x * lax.rsqrt(jnp.mean(x * x, axis=-1, keepdims=True) + RMS_EPS)
    hb = (y * g_ref[...]).astype(BF16)
    q_ref[...] = _dot(hb, w_ref[:, 0:NSA_WIDTH]).astype(q_ref.dtype)
    off = NSA_WIDTH
    kv = _dot(hb, w_ref[:, off:off + 6 * KV_WIDTH])
    tm = kv.shape[0]
    for i in range(6):
        z = kv[:, i * KV_WIDTH:(i + 1) * KV_WIDTH]
        if head_rows:
            kv_refs[2 + i][0] = z.T.reshape(N_KV_HEADS, HEAD_DIM, tm)
        if not head_rows or i < 2:
            kv_refs[i][...] = z
    kvb_ref[...] = kv.astype(BF16)
    off += 6 * KV_WIDTH
    gn_ref[...] = _dot(hb, w_ref[:, off:off + LANES])
    off += LANES
    conv_ref[...] = _dot(hb, w_ref[:, off:off + 3 * CONV_DIM])
    off += 3 * CONV_DIM
    gate_ref[...] = _dot(hb, w_ref[:, off:off + 2 * D_MODEL])


def _proj(x2d, g, w_all, head_rows, t=None):
    n = x2d.shape[0]
    tm = min(512, n if t is None else t)
    assert n % tm == 0
    row = lambda w: pl.BlockSpec((tm, w), lambda i: (i, 0))
    dense = lambda w, dt: (jax.ShapeDtypeStruct((n, w), dt), row(w))
    if head_rows:
        tpb = t // tm
        kv_outs = [dense(KV_WIDTH, F32)] * 2 + [
            (jax.ShapeDtypeStruct((n // t, N_KV_HEADS, HEAD_DIM, t), F32),
             pl.BlockSpec((1, N_KV_HEADS, HEAD_DIM, tm), lambda i: (i // tpb, 0, 0, i % tpb)))] * 6
    else:
        kv_outs = [dense(KV_WIDTH, F32)] * 6
    outs = ([dense(NSA_WIDTH, BF16)] + kv_outs
            + [dense(6 * KV_WIDTH, BF16), dense(LANES, F32), dense(3 * CONV_DIM, F32), dense(2 * D_MODEL, F32)])
    return pl.pallas_call(
        functools.partial(_proj_kernel, head_rows=head_rows),
        grid=(n // tm,),
        in_specs=[row(D_MODEL),
                  pl.BlockSpec((1, D_MODEL), lambda i: (0, 0)),
                  pl.BlockSpec((D_MODEL, PROJ_COLS), lambda i: (0, 0))],
        out_specs=[o[1] for o in outs],
        out_shape=[o[0] for o in outs],
        compiler_params=_cparams(("arbitrary",)),
        name="proj",
    )(x2d, g, w_all)


def _prep_w_in(w_in):
    sizes = (NSA_WIDTH,) + (KV_WIDTH,) * 6 + (3 * N_HEADS, CONV_DIM, CONV_DIM, CONV_DIM, D_MODEL, D_MODEL)
    offs = np.concatenate([[0], np.cumsum(sizes)])
    wq = w_in[:, :NSA_WIDTH].reshape(D_MODEL, N_KV_HEADS, GROUP, HEAD_DIM)
    wq = wq.transpose(0, 2, 1, 3).reshape(D_MODEL, NSA_WIDTH)
    wkv = w_in[:, offs[1]:offs[7]]
    wgn = jnp.pad(w_in[:, offs[7]:offs[8]], ((0, 0), (0, LANES - 3 * N_HEADS)))
    wrest = w_in[:, offs[8]:]
    return jnp.concatenate([wq, wkv, wgn, wrest], axis=1).astype(BF16)


CHUNKS_PER_PAGE = PAGE // CMP_STRIDE


def _compress_kernel(pt_ref, *refs, pp):
    del pt_ref
    page_refs = refs[:pp + 1]
    wbd_ref, pe_ref, w1_ref, w2_ref, out_ref, acc_ref, pehid_ref = refs[pp + 1:]

    @pl.when((pl.program_id(0) == 0) & (pl.program_id(1) == 0))
    def _():
        pehid_ref[...] = _dot(pe_ref[...], w1_ref[...])

    cols = []
    for r in range(CMP_STRIDE):
        xr = jnp.concatenate([p[0, pl.ds(r, CHUNKS_PER_PAGE, stride=CMP_STRIDE), :] for p in page_refs], axis=0)
        cols.append(xr.astype(BF16))
    x = jnp.concatenate(cols, axis=1)
    acc_ref[...] = _dot(x, wbd_ref[...])
    nc = CHUNKS_PER_PAGE * pp
    pe = pehid_ref[0:1, :]
    hids = []
    for h in range(N_KV_HEADS):
        base = h * CMP_RATIO * CMP_HIDDEN
        p0 = acc_ref[0:nc, base:base + CMP_HIDDEN]
        p1 = acc_ref[pl.ds(1, nc), base + CMP_HIDDEN:base + 2 * CMP_HIDDEN]
        hids.append(jax.nn.gelu((pe + p0) + p1).astype(BF16))
    out_ref[0] = _dot(jnp.concatenate(hids, axis=1), w2_ref[...])


def _compress(pages, colblk, page_table, pe, w1, w2):
    nb, npages = page_table.shape
    pp = min(32, npages)
    assert npages % pp == 0
    nc = CHUNKS_PER_PAGE * pp
    w1r = w1.reshape(CMP_RATIO, CMP_STRIDE, HEAD_DIM, CMP_HIDDEN)
    blk = w1r.transpose(1, 2, 0, 3).reshape(CMP_STRIDE, HEAD_DIM, CMP_RATIO * CMP_HIDDEN)
    z = jnp.zeros_like(blk)
    wbd = jnp.concatenate([jnp.concatenate([blk, z], axis=2), jnp.concatenate([z, blk], axis=2)], axis=1)
    wbd = wbd.reshape(CMP_STRIDE * KV_WIDTH, N_KV_HEADS * CMP_RATIO * CMP_HIDDEN).astype(BF16)
    z2 = jnp.zeros_like(w2)
    w2bd = jnp.concatenate([jnp.concatenate([w2, z2], axis=1), jnp.concatenate([z2, w2], axis=1)], axis=0).astype(BF16)
    pe8 = jnp.broadcast_to(pe.reshape(1, CMP_LEN * HEAD_DIM), (8, CMP_LEN * HEAD_DIM)).astype(BF16)

    def page_spec(p):
        def imap(b, s, pt):
            return (pt[b, jnp.minimum(s * pp + p, npages - 1)], 0, colblk)
        return pl.BlockSpec((1, PAGE, KV_WIDTH), imap)

    const = lambda shape: pl.BlockSpec(shape, lambda b, s, pt: (0,) * len(shape))
    grid_spec = pltpu.PrefetchScalarGridSpec(
        num_scalar_prefetch=1,
        grid=(nb, npages // pp),
        in_specs=[page_spec(p) for p in range(pp + 1)]
        + [const(wbd.shape), const(pe8.shape), const((CMP_LEN * HEAD_DIM, CMP_HIDDEN)), const(w2bd.shape)],
        out_specs=pl.BlockSpec((1, nc, KV_WIDTH), lambda b, s, pt: (b, s, 0)),
        scratch_shapes=[pltpu.VMEM((nc + 8, N_KV_HEADS * CMP_RATIO * CMP_HIDDEN), F32),
                        pltpu.VMEM((8, CMP_HIDDEN), F32)],
    )
    return pl.pallas_call(
        functools.partial(_compress_kernel, pp=pp),
        grid_spec=grid_spec,
        out_shape=jax.ShapeDtypeStruct((nb, npages * CHUNKS_PER_PAGE, KV_WIDTH), F32),
        compiler_params=_cparams(("arbitrary", "arbitrary")),
        name="compress",
    )(page_table, *([pages] * (pp + 1)), wbd, pe8, w1.astype(BF16), w2bd)


def _compress_t_kernel(pt_ref, *refs, pp, npages, ns):
    cache_refs = refs[:ns]
    weight_refs = [refs[ns + 4 * c:ns + 4 * c + 4] for c in range(ns)]
    out_refs = refs[5 * ns:6 * ns]
    pbuf_ref, sem_ref, xt_ref, acc_ref, pehid_ref = refs[6 * ns:]
    b = pl.program_id(0)
    s = pl.program_id(1)
    n_steps = pl.num_programs(1)
    step = b * n_steps + s
    last = pl.num_programs(0) * n_steps - 1
    slot = step % 2
    copies = [(c, p) for c in range(ns) for p in range(pp + 1)]

    def page_copy(bb, ss, c, p, sl):
        page = pt_ref[bb, jnp.minimum(ss * pp + p, npages - 1)]
        return pltpu.make_async_copy(cache_refs[c].at[page], pbuf_ref.at[sl, c, p], sem_ref.at[sl])

    @pl.when(step == 0)
    def _():
        for c in range(ns):
            pehid_ref[c] = _dot(weight_refs[c][1][...], weight_refs[c][2][...])
        for c, p in copies:
            page_copy(0, 0, c, p, 0).start()

    wrap = s + 1 == n_steps
    nxt_b = jnp.minimum(jnp.where(wrap, b + 1, b), pl.num_programs(0) - 1)
    nxt_s = jnp.where(step == last, s, jnp.where(wrap, 0, s + 1))
    for c, p in copies:
        page_copy(nxt_b, nxt_s, c, p, 1 - slot).start()
    for c, p in copies:
        page_copy(b, s, c, p, slot).wait()

    nc = CHUNKS_PER_PAGE * pp
    m = nc + CHUNKS_PER_PAGE
    first_head = lax.broadcasted_iota(jnp.int32, (m, KV_WIDTH), 1) < HEAD_DIM
    for c in range(ns):
        w1p_ref, _, _, w2_ref = weight_refs[c]
        for p in range(pp + 1):
            page = pbuf_ref[slot, c, p].reshape(KV_WIDTH, PAGE)
            xt_ref[c, p * PAGE:(p + 1) * PAGE, :] = page.T
        pe = pehid_ref[c, 0:1, :]
        xs = [[], []]
        for r in range(0, CMP_STRIDE, 2):
            a = xt_ref[c, pl.ds(r, m, stride=CMP_STRIDE), :]
            d = xt_ref[c, pl.ds(r + 1, m, stride=CMP_STRIDE), :]
            xs[0].append(jnp.where(first_head, a, pltpu.roll(d, HEAD_DIM, axis=1)).astype(BF16))
            xs[1].append(jnp.where(first_head, pltpu.roll(a, HEAD_DIM, axis=1), d).astype(BF16))
        hids = []
        for h in range(N_KV_HEADS):
            x = jnp.concatenate(xs[h], axis=1)
            acc_ref[c, h] = _dot(x, w1p_ref[...])
            p0 = acc_ref[c, h, 0:nc, 0:CMP_HIDDEN]
            p1 = acc_ref[c, h, pl.ds(1, nc), CMP_HIDDEN:2 * CMP_HIDDEN]
            hids.append(jax.nn.gelu((pe + p0) + p1).astype(BF16))
        out_refs[c][0] = _dot(jnp.concatenate(hids, axis=1), w2_ref[...])

    @pl.when(step == last)
    def _():
        for c, p in copies:
            page_copy(nxt_b, nxt_s, c, p, 1 - slot).wait()


def _compress_t(caches, page_table, weights):
    nb, npages = page_table.shape
    ns = len(caches)
    pp = min(32, npages)
    assert npages % pp == 0
    nc = CHUNKS_PER_PAGE * pp
    pages = [c.transpose(0, 2, 3, 1) for c in caches]
    wargs = []
    for pe, w1, w2 in weights:
        w1p = w1.reshape(CMP_RATIO, CMP_STRIDE * HEAD_DIM, CMP_HIDDEN).transpose(1, 0, 2)
        w1p = w1p.reshape(CMP_STRIDE * HEAD_DIM, CMP_RATIO * CMP_HIDDEN).astype(BF16)
        z2 = jnp.zeros_like(w2)
        w2bd = jnp.concatenate([jnp.concatenate([w2, z2], axis=1), jnp.concatenate([z2, w2], axis=1)], axis=0)
        pe8 = jnp.broadcast_to(pe.reshape(1, CMP_LEN * HEAD_DIM), (8, CMP_LEN * HEAD_DIM)).astype(BF16)
        wargs += [w1p, pe8, w1.astype(BF16), w2bd.astype(BF16)]

    const = lambda arr: pl.BlockSpec(arr.shape, lambda b, s, pt: (0,) * arr.ndim)
    out_spec = pl.BlockSpec((1, nc, KV_WIDTH), lambda b, s, pt: (b, s, 0))
    grid_spec = pltpu.PrefetchScalarGridSpec(
        num_scalar_prefetch=1,
        grid=(nb, npages // pp),
        in_specs=[pl.BlockSpec(memory_space=pl.ANY)] * ns + [const(a) for a in wargs],
        out_specs=[out_spec] * ns,
        scratch_shapes=[pltpu.VMEM((2, ns, pp + 1, N_KV_HEADS, HEAD_DIM, PAGE), F32),
                        pltpu.SemaphoreType.DMA((2,)),
                        pltpu.VMEM((ns, (pp + 1) * PAGE, KV_WIDTH), F32),
                        pltpu.VMEM((ns, N_KV_HEADS, nc + CHUNKS_PER_PAGE, CMP_RATIO * CMP_HIDDEN), F32),
                        pltpu.VMEM((ns, 8, CMP_HIDDEN), F32)],
    )
    return pl.pallas_call(
        functools.partial(_compress_t_kernel, pp=pp, npages=npages, ns=ns),
        grid_spec=grid_spec,
        out_shape=[jax.ShapeDtypeStruct((nb, npages * CHUNKS_PER_PAGE, KV_WIDTH), F32)] * ns,
        compiler_params=_cparams(("arbitrary", "arbitrary")),
        name="compress_t",
    )(page_table, *pages, *wargs)


def _sel_map(n_rows, n_cmp, n_cols, n_blocks):
    c0 = np.arange(n_rows)[:, None] * CMP_STRIDE
    s0 = np.arange(n_cols)[None, :] * SEL_BLOCK
    ov = np.minimum(c0 + CMP_LEN, s0 + SEL_BLOCK) - np.maximum(c0, s0)
    m = np.clip(ov, 0, None).astype(np.float32) / CMP_LEN
    m[n_cmp:] = 0.0
    m[:, n_blocks:] = 0.0
    return m


CMP_FRONT = 16
CMP_NEAR = 24
SLC_NEAR = 2 * Q_BLOCK
SLC_TILE = 512
WIN_KEYS = WINDOW + Q_BLOCK


def _softmax_update(state, s, v, mask=None, v_transposed=False):
    m_old, l_old, acc = state
    m_new = jnp.maximum(m_old, jnp.max(s, axis=-1, keepdims=True))
    p = jnp.exp(s - m_new)
    if mask is not None:
        p = jnp.where(mask, p, 0.0)
    alpha = jnp.exp(m_old - m_new)
    l_new = alpha * l_old + jnp.sum(p, axis=-1, keepdims=True)
    pv = _dot_t(p.astype(BF16), v) if v_transposed else _dot(p.astype(BF16), v)
    return m_new, l_new, alpha * acc + pv


def _softmax_finish(state):
    _, l, acc = state
    return jnp.where(l > 0.0, acc / jnp.where(l > 0.0, l, 1.0), 0.0)


def _tile_rows(x, n):
    return jnp.concatenate([x] * n, axis=0)


def _nsa_prompt_kernel(rb_ref, q_ref, gn_ref, kcmp_ref, vcmp_ref, selmap_ref, ks_ref, vs_ref, kw_ref, vw_ref,
                       et_ref, out_ref, nbc_ref, nbs_ref, nbw_ref):
    hk = pl.program_id(1)
    qb = pl.program_id(2)
    s0 = qb * Q_BLOCK
    gq = GROUP * Q_BLOCK

    @pl.when(qb == 0)
    def _():
        for g in range(GROUP):
            tab = lambda k, g=g: rb_ref[k, hk * GROUP + g]
            ql = lax.broadcasted_iota(jnp.int32, (Q_BLOCK, LANES), 0)
            cl = lax.broadcasted_iota(jnp.int32, (Q_BLOCK, LANES), 1)
            dist = ql - CMP_STRIDE * (cl - CMP_FRONT) - (CMP_LEN - 1)
            ok = (dist >= 0) & (cl < CMP_NEAR)
            nbc_ref[g] = jnp.where(ok, _bias_from_scalars(_bucket(dist), tab), NEG_INF)
            ql = lax.broadcasted_iota(jnp.int32, (Q_BLOCK, SLC_NEAR), 0)
            kl = lax.broadcasted_iota(jnp.int32, (Q_BLOCK, SLC_NEAR), 1)
            dist = ql - kl + Q_BLOCK
            nbs_ref[g] = jnp.where(dist >= 0, _bias_from_scalars(_bucket(dist), tab), NEG_INF)
            ql = lax.broadcasted_iota(jnp.int32, (Q_BLOCK, WIN_KEYS), 0)
            kl = lax.broadcasted_iota(jnp.int32, (Q_BLOCK, WIN_KEYS), 1)
            dist = ql - kl + WINDOW
            ok = (dist >= 0) & (dist < WINDOW)
            nbw_ref[g] = jnp.where(ok, _bias_from_scalars(_bucket(dist), tab), NEG_INF)

    lane = lax.broadcasted_iota(jnp.int32, (Q_BLOCK, LANES), 1)
    lmask = (lane >= hk * HEAD_DIM) & (lane < (hk + 1) * HEAD_DIM)
    qblk = q_ref[0].astype(F32)
    qp = jnp.concatenate(
        [jnp.where(lmask, qblk[:, g * LANES:(g + 1) * LANES], 0.0) for g in range(GROUP)], axis=0)
    qp = (qp * (HEAD_DIM ** -0.5)).astype(BF16)

    far_bias = jnp.concatenate(
        [jnp.full((Q_BLOCK, 1), rb_ref[N_BUCKETS - 1, hk * GROUP + g], F32) for g in range(GROUP)], axis=0)

    n_cmp_pad = kcmp_ref.shape[1] - CMP_FRONT - (LANES - CMP_FRONT)
    kc_far = kcmp_ref[0, CMP_FRONT:CMP_FRONT + n_cmp_pad, :].astype(BF16)
    vc_far = vcmp_ref[0, CMP_FRONT:CMP_FRONT + n_cmp_pad, :].astype(BF16)
    near0 = pl.multiple_of(qb * (Q_BLOCK // CMP_STRIDE), 8)
    kc_near = kcmp_ref[0, pl.ds(near0, LANES), :].astype(BF16)
    vc_near = vcmp_ref[0, pl.ds(near0, LANES), :].astype(BF16)
    c8 = qb * (Q_BLOCK // CMP_STRIDE)
    ccol = lax.broadcasted_iota(jnp.int32, (1, n_cmp_pad), 1)
    s_far = _dot_t(qp, kc_far) + jnp.where(ccol < c8 - CMP_FRONT, 0.0, NEG_INF)
    ncol = lax.broadcasted_iota(jnp.int32, (1, LANES), 1)
    s_near = _dot_t(qp, kc_near) + (nbc_ref[...].reshape(gq, LANES)
                                    + jnp.where(ncol >= CMP_FRONT - c8, 0.0, NEG_INF))
    m = jnp.maximum(jnp.max(s_far, axis=-1, keepdims=True) + far_bias, jnp.max(s_near, axis=-1, keepdims=True))
    p_far = jnp.exp(s_far - (m - far_bias))
    p_near = jnp.exp(s_near - m)
    l = jnp.sum(p_far, axis=-1, keepdims=True) + jnp.sum(p_near, axis=-1, keepdims=True)
    inv = jnp.where(m > 0.5 * NEG_INF, 1.0 / l, 0.0)
    pc_far = p_far * inv
    pc_near = p_near * inv
    o_cmp = _dot(pc_far.astype(BF16), vc_far) + _dot(pc_near.astype(BF16), vc_near)

    init = (jnp.full((gq, 1), NEG_INF, F32), jnp.zeros((gq, 1), F32), jnp.zeros((gq, LANES), F32))
    p0 = pl.multiple_of(s0, LANES)
    wcol = lax.broadcasted_iota(jnp.int32, (1, WIN_KEYS), 1)
    s = _dot_t(qp, kw_ref[0, pl.ds(p0, WIN_KEYS), :]) + (nbw_ref[...].reshape(gq, WIN_KEYS)
                                                         + jnp.where(wcol >= WINDOW - s0, 0.0, NEG_INF))
    o_win = _softmax_finish(_softmax_update(init, s, vw_ref[0, pl.ds(p0, WIN_KEYS), :]))

    ps_far = sum(pc_far[g * Q_BLOCK:(g + 1) * Q_BLOCK] for g in range(GROUP))
    ps_near = sum(pc_near[g * Q_BLOCK:(g + 1) * Q_BLOCK] for g in range(GROUP))
    sm_far = selmap_ref[CMP_FRONT:CMP_FRONT + n_cmp_pad, :].astype(BF16)
    sm_near = selmap_ref[pl.ds(near0, LANES), :].astype(BF16)
    imp = _split_dot(ps_far, sm_far) + _split_dot(ps_near, sm_near)
    imp_t = imp.T
    n_sel = imp_t.shape[0]
    blk = lax.broadcasted_iota(jnp.int32, (n_sel, Q_BLOCK), 0)
    cur = (s0 + lax.broadcasted_iota(jnp.int32, (n_sel, Q_BLOCK), 1)) // SEL_BLOCK
    forced = (blk == 0) | (blk == cur) | (blk == cur - 1)
    val = jnp.where(blk > cur, NEG_INF, imp_t + FORCE_BONUS * forced.astype(F32))
    sel_t = jnp.zeros((n_sel, Q_BLOCK), F32)
    for _ in range(N_SELECT):
        mx = jnp.max(val, axis=0, keepdims=True)
        first = jnp.min(jnp.where(val == mx, blk, n_sel), axis=0, keepdims=True)
        hit = blk == first
        sel_t = jnp.where(hit, 1.0, sel_t)
        val = jnp.where(hit, -3e38, val)
    sel = sel_t.T > 0.5

    jcol = lax.broadcasted_iota(jnp.int32, (Q_BLOCK, n_sel), 1)
    far_blocks = (s0 - Q_BLOCK) // SEL_BLOCK
    pen_far = jnp.where(sel & (jcol < far_blocks), 0.0, NEG_INF).astype(BF16)
    pen_near = jnp.where(sel, 0.0, NEG_INF).astype(BF16)
    q_far = jnp.concatenate([qp, _tile_rows(pen_far, GROUP)], axis=1)
    q_near = jnp.concatenate([qp, _tile_rows(pen_near, GROUP)], axis=1)

    def far_step(kt, state):
        p0 = pl.multiple_of(Q_BLOCK + kt * SLC_TILE, LANES)
        k = jnp.concatenate([ks_ref[0, pl.ds(p0, SLC_TILE), :], et_ref[pl.ds(p0, SLC_TILE), :]], axis=1)
        return _softmax_update(state, _dot_t(q_far, k), vs_ref[0, pl.ds(p0, SLC_TILE), :])

    n_far = (qb + 2) // (SLC_TILE // Q_BLOCK)
    m_far, l_far, acc_far = lax.fori_loop(0, n_far, far_step, init)
    state = (m_far + far_bias, l_far, acc_far)
    k = jnp.concatenate([ks_ref[0, pl.ds(p0, SLC_NEAR), :], et_ref[pl.ds(p0, SLC_NEAR), :]], axis=1)
    kcol = lax.broadcasted_iota(jnp.int32, (1, SLC_NEAR), 1)
    s = _dot_t(q_near, k) + (nbs_ref[...].reshape(gq, SLC_NEAR) + jnp.where(kcol >= Q_BLOCK - s0, 0.0, NEG_INF))
    o_slc = _softmax_finish(_softmax_update(state, s, vs_ref[0, pl.ds(p0, SLC_NEAR), :]))

    gs = jax.nn.sigmoid(gn_ref[0])

    def gate(g, i):
        c0, c1 = 3 * g + i, 3 * (GROUP + g) + i
        return jnp.where(hk == 0, gs[:, c0:c0 + 1], gs[:, c1:c1 + 1])

    for g in range(GROUP):
        rows = slice(g * Q_BLOCK, (g + 1) * Q_BLOCK)
        o = gate(g, 0) * o_cmp[rows] + gate(g, 1) * o_slc[rows] + gate(g, 2) * o_win[rows]
        out_ref[0, 0, :, g * LANES:(g + 1) * LANES] = jnp.where(lmask, o, 0.0).astype(out_ref.dtype)


def _nsa_prompt(q, gn, kcmp, vcmp, kvb, rel_bias):
    b, t, _ = q.shape
    assert t % SLC_TILE == 0
    nqb = t // Q_BLOCK
    n_chunk = t // CMP_STRIDE
    n_cmp = n_chunk - CMP_RATIO + 1
    n_blocks = t // SEL_BLOCK
    assert n_blocks <= LANES
    cpad = ((0, 0), (CMP_FRONT, LANES - CMP_FRONT), (0, 0))
    kcmp_p = jnp.pad(kcmp, cpad)
    vcmp_p = jnp.pad(vcmp, cpad)
    selmap = jnp.asarray(np.pad(_sel_map(n_chunk, n_cmp, LANES, n_blocks),
                                ((CMP_FRONT, LANES - CMP_FRONT), (0, 0))), F32)
    ks_p = jnp.pad(kvb[:, :, 2 * KV_WIDTH:3 * KV_WIDTH], ((0, 0), (Q_BLOCK, 0), (0, 0)))
    vs_p = jnp.pad(kvb[:, :, 3 * KV_WIDTH:4 * KV_WIDTH], ((0, 0), (Q_BLOCK, 0), (0, 0)))
    kw_p = jnp.pad(kvb[:, :, 4 * KV_WIDTH:5 * KV_WIDTH], ((0, 0), (WINDOW, 0), (0, 0)))
    vw_p = jnp.pad(kvb[:, :, 5 * KV_WIDTH:6 * KV_WIDTH], ((0, 0), (WINDOW, 0), (0, 0)))
    pidx = np.arange(t + Q_BLOCK)
    e = (pidx[:, None] // SEL_BLOCK - Q_BLOCK // SEL_BLOCK == np.arange(LANES)[None, :])
    e = jnp.asarray(e.astype(np.float32), BF16)
    full = lambda arr: pl.BlockSpec((1,) + arr.shape[1:], lambda bi, h, i: (bi, 0, 0))
    return pl.pallas_call(
        _nsa_prompt_kernel,
        grid=(b, N_KV_HEADS, nqb),
        in_specs=[pl.BlockSpec(memory_space=pltpu.SMEM),
                  pl.BlockSpec((1, Q_BLOCK, NSA_WIDTH), lambda bi, h, i: (bi, i, 0)),
                  pl.BlockSpec((1, Q_BLOCK, LANES), lambda bi, h, i: (bi, i, 0)),
                  full(kcmp_p), full(vcmp_p),
                  pl.BlockSpec(selmap.shape, lambda bi, h, i: (0, 0)),
                  full(ks_p), full(vs_p), full(kw_p), full(vw_p),
                  pl.BlockSpec(e.shape, lambda bi, h, i: (0, 0))],
        out_specs=pl.BlockSpec((1, 1, Q_BLOCK, NSA_WIDTH), lambda bi, h, i: (h, bi, i, 0)),
        out_shape=jax.ShapeDtypeStruct((N_KV_HEADS, b, t, NSA_WIDTH), BF16),
        scratch_shapes=[pltpu.VMEM((GROUP, Q_BLOCK, LANES), F32),
                        pltpu.VMEM((GROUP, Q_BLOCK, SLC_NEAR), F32),
                        pltpu.VMEM((GROUP, Q_BLOCK, WIN_KEYS), F32)],
        compiler_params=_cparams(("arbitrary", "arbitrary", "arbitrary")),
        name="nsa_prompt",
    )(rel_bias, q, gn, kcmp_p, vcmp_p, selmap, ks_p, vs_p, kw_p, vw_p, e)


def _col_softmax_update(state, s, vt, smax=None):
    m_old, l_old, acc = state
    m_new = jnp.maximum(m_old, jnp.max(s, axis=0, keepdims=True) if smax is None else smax)
    p = jnp.exp(s - m_new)
    alpha = jnp.exp(m_old - m_new)
    l_new = alpha * l_old + jnp.sum(p, axis=0, keepdims=True)
    return m_new, l_new, alpha * acc + _dot(vt, p.astype(BF16))


def _col_softmax_finish(state):
    _, l, acc = state
    return acc * jnp.where(l > 0.0, 1.0 / jnp.where(l > 0.0, l, 1.0), 0.0)


def _nsa_prompt_t_kernel(rb_ref, q_ref, gn_ref, kcmp_ref, vcmp_ref, vcmpt_ref, selmap_ref, selmapt_ref,
                         ks_ref, vst_ref, kw_ref, vwt_ref, et_ref, out_ref, nbc_ref, nbs_ref, nbw_ref):
    hk = pl.program_id(1)
    qb = pl.program_id(2)
    s0 = qb * Q_BLOCK
    gq = GROUP * Q_BLOCK

    @pl.when(qb == 0)
    def _():
        for g in range(GROUP):
            tab = lambda k, g=g: rb_ref[k, hk * GROUP + g]
            cols = slice(g * Q_BLOCK, (g + 1) * Q_BLOCK)
            cl = lax.broadcasted_iota(jnp.int32, (LANES, Q_BLOCK), 0)
            ql = lax.broadcasted_iota(jnp.int32, (LANES, Q_BLOCK), 1)
            dist = ql - CMP_STRIDE * (cl - CMP_FRONT) - (CMP_LEN - 1)
            ok = (dist >= 0) & (cl < CMP_NEAR)
            nbc_ref[:, cols] = jnp.where(ok, _bias_from_scalars(_bucket(dist), tab), NEG_INF)
            kl = lax.broadcasted_iota(jnp.int32, (SLC_NEAR, Q_BLOCK), 0)
            ql = lax.broadcasted_iota(jnp.int32, (SLC_NEAR, Q_BLOCK), 1)
            dist = ql - kl + Q_BLOCK
            nbs_ref[:, cols] = jnp.where(dist >= 0, _bias_from_scalars(_bucket(dist), tab), NEG_INF)
            kl = lax.broadcasted_iota(jnp.int32, (WIN_KEYS, Q_BLOCK), 0)
            ql = lax.broadcasted_iota(jnp.int32, (WIN_KEYS, Q_BLOCK), 1)
            dist = ql - kl + WINDOW
            ok = (dist >= 0) & (dist < WINDOW)
            nbw_ref[:, cols] = jnp.where(ok, _bias_from_scalars(_bucket(dist), tab), NEG_INF)

    lane = lax.broadcasted_iota(jnp.int32, (Q_BLOCK, LANES), 1)
    lmask = (lane >= hk * HEAD_DIM) & (lane < (hk + 1) * HEAD_DIM)
    qblk = q_ref[0].astype(F32)
    qp = jnp.concatenate(
        [jnp.where(lmask, qblk[:, g * LANES:(g + 1) * LANES], 0.0) for g in range(GROUP)], axis=0)
    qp = (qp * (HEAD_DIM ** -0.5)).astype(BF16)
    far_bias = jnp.concatenate(
        [jnp.full((1, Q_BLOCK), rb_ref[N_BUCKETS - 1, hk * GROUP + g], F32) for g in range(GROUP)], axis=1)
    init = (jnp.full((1, gq), NEG_INF, F32), jnp.zeros((1, gq), F32), jnp.zeros((LANES, gq), F32))
    p0 = pl.multiple_of(s0, LANES)

    n_cmp_pad = kcmp_ref.shape[1] - LANES
    c8 = qb * (Q_BLOCK // CMP_STRIDE)
    near0 = pl.multiple_of(c8, 8)
    kc_far = kcmp_ref[0, CMP_FRONT:CMP_FRONT + n_cmp_pad, :].astype(BF16)
    kc_near = kcmp_ref[0, pl.ds(near0, LANES), :].astype(BF16)
    crow = lax.broadcasted_iota(jnp.int32, (n_cmp_pad, gq), 0)
    s_far = jnp.where(crow < c8 - CMP_FRONT, _dot_t(kc_far, qp), NEG_INF)
    nrow = lax.broadcasted_iota(jnp.int32, (LANES, gq), 0)
    s_near = jnp.where(nrow >= CMP_FRONT - c8, _dot_t(kc_near, qp) + nbc_ref[...], NEG_INF)
    m = jnp.maximum(jnp.max(s_far, axis=0, keepdims=True) + far_bias, jnp.max(s_near, axis=0, keepdims=True))
    p_far = jnp.exp(s_far - (m - far_bias))
    p_near = jnp.exp(s_near - m)
    l = jnp.sum(p_far, axis=0, keepdims=True) + jnp.sum(p_near, axis=0, keepdims=True)
    inv = jnp.where(m > 0.5 * NEG_INF, 1.0 / l, 0.0)
    pc_far = p_far * inv
    pc_near = p_near * inv
    vct_near = vcmp_ref[0, pl.ds(near0, LANES), :].T.astype(BF16)
    o_cmp = _dot(vcmpt_ref[0], pc_far.astype(BF16)) + _dot(vct_near, pc_near.astype(BF16))

    wrow = lax.broadcasted_iota(jnp.int32, (WIN_KEYS, gq), 0)
    s = jnp.where(wrow >= WINDOW - s0, _dot_t(kw_ref[0, pl.ds(p0, WIN_KEYS), :], qp) + nbw_ref[...], NEG_INF)
    o_win = _col_softmax_finish(_col_softmax_update(init, s, vwt_ref[0, :, pl.ds(p0, WIN_KEYS)]))

    ps_far = sum(pc_far[:, g * Q_BLOCK:(g + 1) * Q_BLOCK] for g in range(GROUP))
    ps_near = sum(pc_near[:, g * Q_BLOCK:(g + 1) * Q_BLOCK] for g in range(GROUP))

    def split_dot(a, p):
        hi = p.astype(BF16)
        return _dot(a, hi) + _dot(a, (p - hi.astype(F32)).astype(BF16))

    sm_near_t = selmap_ref[pl.ds(near0, LANES), :].T.astype(BF16)
    imp_t = split_dot(selmapt_ref[...], ps_far) + split_dot(sm_near_t, ps_near)
    n_sel = imp_t.shape[0]
    blk = lax.broadcasted_iota(jnp.int32, (n_sel, Q_BLOCK), 0)
    cur = (s0 + lax.broadcasted_iota(jnp.int32, (n_sel, Q_BLOCK), 1)) // SEL_BLOCK
    forced = (blk == 0) | (blk == cur) | (blk == cur - 1)
    val = jnp.where(blk > cur, NEG_INF, imp_t + FORCE_BONUS * forced.astype(F32))
    sel_t = jnp.zeros((n_sel, Q_BLOCK), F32)
    for _ in range(N_SELECT):
        mx = jnp.max(val, axis=0, keepdims=True)
        first = jnp.min(jnp.where(val == mx, blk, n_sel), axis=0, keepdims=True)
        hit = blk == first
        sel_t = jnp.where(hit, 1.0, sel_t)
        val = jnp.where(hit, -3e38, val)
    sel = sel_t.T > 0.5

    jcol = lax.broadcasted_iota(jnp.int32, (Q_BLOCK, n_sel), 1)
    far_blocks = (s0 - Q_BLOCK) // SEL_BLOCK
    pen_far = jnp.where(sel & (jcol < far_blocks), 0.0, NEG_INF).astype(BF16)
    pen_near = jnp.where(sel, 0.0, NEG_INF).astype(BF16)
    q_far = jnp.concatenate([qp, _tile_rows(pen_far, GROUP)], axis=1)
    q_near = jnp.concatenate([qp, _tile_rows(pen_near, GROUP)], axis=1)

    n_tiles = (ks_ref.shape[1] - Q_BLOCK) // SLC_TILE

    def far_logits(kt):
        f0 = pl.multiple_of(Q_BLOCK + kt * SLC_TILE, LANES)
        k = jnp.concatenate([ks_ref[0, pl.ds(f0, SLC_TILE), :], et_ref[pl.ds(f0, SLC_TILE), :]], axis=1)
        return _dot_t(k, q_far)

    def far_step(kt, carry):
        s_cur, state = carry
        s_next = far_logits(jnp.minimum(kt + 1, n_tiles - 1))
        f0 = pl.multiple_of(Q_BLOCK + kt * SLC_TILE, LANES)
        return s_next, _col_softmax_update(state, s_cur, vst_ref[0, :, pl.ds(f0, SLC_TILE)])

    n_far = (qb + 2) // (SLC_TILE // Q_BLOCK)
    _, (m_far, l_far, acc_far) = lax.fori_loop(0, n_far, far_step, (far_logits(0), init))
    state = (m_far + far_bias, l_far, acc_far)
    k = jnp.concatenate([ks_ref[0, pl.ds(p0, SLC_NEAR), :], et_ref[pl.ds(p0, SLC_NEAR), :]], axis=1)
    krow = lax.broadcasted_iota(jnp.int32, (SLC_NEAR, gq), 0)
    s = jnp.where(krow >= Q_BLOCK - s0, _dot_t(k, q_near) + nbs_ref[...], NEG_INF)
    o_slc = _col_softmax_finish(_col_softmax_update(state, s, vst_ref[0, :, pl.ds(p0, SLC_NEAR)]))

    gst = jax.nn.sigmoid(gn_ref[0]).T

    def gate(g, i):
        c0, c1 = 3 * g + i, 3 * (GROUP + g) + i
        return jnp.where(hk == 0, gst[c0:c0 + 1, :], gst[c1:c1 + 1, :])

    for g in range(GROUP):
        cols = slice(g * Q_BLOCK, (g + 1) * Q_BLOCK)
        o = gate(g, 0) * o_cmp[:, cols] + gate(g, 1) * o_slc[:, cols] + gate(g, 2) * o_win[:, cols]
        out_ref[0, 0, :, g * LANES:(g + 1) * LANES] = jnp.where(lmask, o.T, 0.0).astype(out_ref.dtype)


def _nsa_prompt_t(q, gn, kcmp, vcmp, kvb, rel_bias):
    b, t, _ = q.shape
    assert t % SLC_TILE == 0
    nqb = t // Q_BLOCK
    n_chunk = t // CMP_STRIDE
    n_cmp = n_chunk - CMP_RATIO + 1
    n_blocks = t // SEL_BLOCK
    assert n_blocks <= LANES
    cpad = ((0, 0), (CMP_FRONT, LANES - CMP_FRONT), (0, 0))
    kcmp_p = jnp.pad(kcmp, cpad)
    vcmp_p = jnp.pad(vcmp, cpad)
    vcmp_t = vcmp.transpose(0, 2, 1).astype(BF16)
    sm = _sel_map(n_chunk, n_cmp, LANES, n_blocks)
    selmap = jnp.asarray(np.pad(sm, ((CMP_FRONT, LANES - CMP_FRONT), (0, 0))), F32)
    selmap_t = jnp.asarray(sm.T, BF16)
    front = lambda z, n: jnp.pad(z, ((0, 0), (n, 0), (0, 0)))
    ks_p = front(kvb[:, :, 2 * KV_WIDTH:3 * KV_WIDTH], Q_BLOCK)
    vs_t = front(kvb[:, :, 3 * KV_WIDTH:4 * KV_WIDTH], Q_BLOCK).transpose(0, 2, 1)
    kw_p = front(kvb[:, :, 4 * KV_WIDTH:5 * KV_WIDTH], WINDOW)
    vw_t = front(kvb[:, :, 5 * KV_WIDTH:6 * KV_WIDTH], WINDOW).transpose(0, 2, 1)
    pidx = np.arange(t + Q_BLOCK)
    e = (pidx[:, None] // SEL_BLOCK - Q_BLOCK // SEL_BLOCK == np.arange(LANES)[None, :])
    e = jnp.asarray(e.astype(np.float32), BF16)
    full = lambda arr: pl.BlockSpec((1,) + arr.shape[1:], lambda bi, h, i: (bi, 0, 0))
    const = lambda arr: pl.BlockSpec(arr.shape, lambda bi, h, i: (0, 0))
    return pl.pallas_call(
        _nsa_prompt_t_kernel,
        grid=(b, N_KV_HEADS, nqb),
        in_specs=[pl.BlockSpec(memory_space=pltpu.SMEM),
                  pl.BlockSpec((1, Q_BLOCK, NSA_WIDTH), lambda bi, h, i: (bi, i, 0)),
                  pl.BlockSpec((1, Q_BLOCK, LANES), lambda bi, h, i: (bi, i, 0)),
                  full(kcmp_p), full(vcmp_p), full(vcmp_t), const(selmap), const(selmap_t),
                  full(ks_p), full(vs_t), full(kw_p), full(vw_t), const(e)],
        out_specs=pl.BlockSpec((1, 1, Q_BLOCK, NSA_WIDTH), lambda bi, h, i: (h, bi, i, 0)),
        out_shape=jax.ShapeDtypeStruct((N_KV_HEADS, b, t, NSA_WIDTH), BF16),
        scratch_shapes=[pltpu.VMEM((LANES, GROUP * Q_BLOCK), F32),
                        pltpu.VMEM((SLC_NEAR, GROUP * Q_BLOCK), F32),
                        pltpu.VMEM((WIN_KEYS, GROUP * Q_BLOCK), F32)],
        compiler_params=_cparams(("arbitrary", "arbitrary", "arbitrary")),
        name="nsa_prompt",
    )(rel_bias, q, gn, kcmp_p, vcmp_p, vcmp_t, selmap, selmap_t, ks_p, vs_t, kw_p, vw_t, e)


def _nsa_prompt_pair_kernel(rb_ref, q_ref, gn_ref, kcmp_ref, vcmp_ref, vcmpt_ref, selmap_ref, selmapt_ref,
                            ks_ref, vst_ref, kw_ref, vwt_ref, et_ref, out_ref, nbc_ref, nbs_ref, nbw_ref, s_ref,
                            smax_ref):
    qb = pl.program_id(1)
    s0 = qb * Q_BLOCK
    gq = GROUP * Q_BLOCK
    heads = range(N_KV_HEADS)

    @pl.when(qb == 0)
    def _():
        for hk in heads:
            for g in range(GROUP):
                tab = lambda k, h=hk * GROUP + g: rb_ref[k, h]
                cols = slice(g * Q_BLOCK, (g + 1) * Q_BLOCK)
                cl = lax.broadcasted_iota(jnp.int32, (LANES, Q_BLOCK), 0)
                ql = lax.broadcasted_iota(jnp.int32, (LANES, Q_BLOCK), 1)
                dist = ql - CMP_STRIDE * (cl - CMP_FRONT) - (CMP_LEN - 1)
                ok = (dist >= 0) & (cl < CMP_NEAR)
                nbc_ref[hk, :, cols] = jnp.where(ok, _bias_from_scalars(_bucket(dist), tab), NEG_INF)
                kl = lax.broadcasted_iota(jnp.int32, (SLC_NEAR, Q_BLOCK), 0)
                ql = lax.broadcasted_iota(jnp.int32, (SLC_NEAR, Q_BLOCK), 1)
                dist = ql - kl + Q_BLOCK
                nbs_ref[hk, :, cols] = jnp.where(dist >= 0, _bias_from_scalars(_bucket(dist), tab), NEG_INF)
                kl = lax.broadcasted_iota(jnp.int32, (WIN_KEYS, Q_BLOCK), 0)
                ql = lax.broadcasted_iota(jnp.int32, (WIN_KEYS, Q_BLOCK), 1)
                dist = ql - kl + WINDOW
                ok = (dist >= 0) & (dist < WINDOW)
                nbw_ref[hk, :, cols] = jnp.where(ok, _bias_from_scalars(_bucket(dist), tab), NEG_INF)

    lane = lax.broadcasted_iota(jnp.int32, (Q_BLOCK, LANES), 1)
    qblk = q_ref[0].astype(F32)
    init = (jnp.full((1, gq), NEG_INF, F32), jnp.zeros((1, gq), F32), jnp.zeros((LANES, gq), F32))
    p0 = pl.multiple_of(s0, LANES)
    n_cmp_pad = kcmp_ref.shape[1] - LANES
    c8 = qb * (Q_BLOCK // CMP_STRIDE)
    near0 = pl.multiple_of(c8, 8)
    kc_far = kcmp_ref[0, CMP_FRONT:CMP_FRONT + n_cmp_pad, :].astype(BF16)
    kc_near = kcmp_ref[0, pl.ds(near0, LANES), :].astype(BF16)
    vct_near = vcmp_ref[0, pl.ds(near0, LANES), :].T.astype(BF16)
    sm_near_t = selmap_ref[pl.ds(near0, LANES), :].T.astype(BF16)
    kw = kw_ref[0, pl.ds(p0, WIN_KEYS), :]
    vwt = vwt_ref[0, :, pl.ds(p0, WIN_KEYS)]
    crow = lax.broadcasted_iota(jnp.int32, (n_cmp_pad, gq), 0)
    nrow = lax.broadcasted_iota(jnp.int32, (LANES, gq), 0)
    wrow = lax.broadcasted_iota(jnp.int32, (WIN_KEYS, gq), 0)
    krow = lax.broadcasted_iota(jnp.int32, (SLC_NEAR, gq), 0)
    n_sel = selmapt_ref.shape[0]
    blk = lax.broadcasted_iota(jnp.int32, (n_sel, Q_BLOCK), 0)
    cur = (s0 + lax.broadcasted_iota(jnp.int32, (n_sel, Q_BLOCK), 1)) // SEL_BLOCK
    forced = (blk == 0) | (blk == cur) | (blk == cur - 1)
    jcol = lax.broadcasted_iota(jnp.int32, (Q_BLOCK, n_sel), 1)
    far_blocks = (s0 - Q_BLOCK) // SEL_BLOCK

    def split_dot(a, p):
        hi = p.astype(BF16)
        return _dot(a, hi) + _dot(a, (p - hi.astype(F32)).astype(BF16))

    def before_far(hk):
        lmask = (lane // HEAD_DIM) == hk
        qp = jnp.concatenate(
            [jnp.where(lmask, qblk[:, g * LANES:(g + 1) * LANES], 0.0) for g in range(GROUP)], axis=0)
        qp = (qp * (HEAD_DIM ** -0.5)).astype(BF16)
        far_bias = jnp.concatenate(
            [jnp.full((1, Q_BLOCK), rb_ref[N_BUCKETS - 1, hk * GROUP + g], F32) for g in range(GROUP)], axis=1)

        s_far = jnp.where(crow < c8 - CMP_FRONT, _dot_t(kc_far, qp), NEG_INF)
        s_near = jnp.where(nrow >= CMP_FRONT - c8, _dot_t(kc_near, qp) + nbc_ref[hk], NEG_INF)
        m = jnp.maximum(jnp.max(s_far, axis=0, keepdims=True) + far_bias, jnp.max(s_near, axis=0, keepdims=True))
        p_far = jnp.exp(s_far - (m - far_bias))
        p_near = jnp.exp(s_near - m)
        l = jnp.sum(p_far, axis=0, keepdims=True) + jnp.sum(p_near, axis=0, keepdims=True)
        inv = jnp.where(m > 0.5 * NEG_INF, 1.0 / l, 0.0)
        pc_far = p_far * inv
        pc_near = p_near * inv
        o_cmp = _dot(vcmpt_ref[0], pc_far.astype(BF16)) + _dot(vct_near, pc_near.astype(BF16))

        s = jnp.where(wrow >= WINDOW - s0, _dot_t(kw, qp) + nbw_ref[hk], NEG_INF)
        o_win = _col_softmax_finish(_col_softmax_update(init, s, vwt))

        ps_far = sum(pc_far[:, g * Q_BLOCK:(g + 1) * Q_BLOCK] for g in range(GROUP))
        ps_near = sum(pc_near[:, g * Q_BLOCK:(g + 1) * Q_BLOCK] for g in range(GROUP))
        imp_t = split_dot(selmapt_ref[...], ps_far) + split_dot(sm_near_t, ps_near)
        val = jnp.where(forced | (blk > cur), NEG_INF, imp_t)
        for _ in range(N_SELECT - N_FORCED):
            mx = jnp.max(val, axis=0, keepdims=True)
            first = jnp.min(jnp.where(val == mx, blk, n_sel), axis=0, keepdims=True)
            val = jnp.where(blk == first, -3e38, val)
        sel = (forced | (val < -1e38)).astype(F32).T > 0.5

        pen_far = jnp.where(sel & (jcol < far_blocks), 0.0, NEG_INF).astype(BF16)
        pen_near = jnp.where(sel, 0.0, NEG_INF).astype(BF16)
        q_far = jnp.concatenate([qp, _tile_rows(pen_far, GROUP)], axis=1)
        q_near = jnp.concatenate([qp, _tile_rows(pen_near, GROUP)], axis=1)
        return dict(far_bias=far_bias, o_cmp=o_cmp, o_win=o_win, q_far=q_far, q_near=q_near)

    pre = [before_far(hk) for hk in heads]
    n_tiles = (ks_ref.shape[1] - Q_BLOCK) // SLC_TILE

    def far_logits(kt):
        f0 = pl.multiple_of(Q_BLOCK + kt * SLC_TILE, LANES)
        k = jnp.concatenate([ks_ref[0, pl.ds(f0, SLC_TILE), :], et_ref[pl.ds(f0, SLC_TILE), :]], axis=1)
        return [_dot_t(k, pre[hk]["q_far"]) for hk in heads]

    def far_pair(i, states):
        for slot in range(2):
            kt = 2 * i + slot
            put_logits(1 - slot, jnp.minimum(kt + 1, n_tiles - 1))
            f0 = pl.multiple_of(Q_BLOCK + jnp.minimum(kt, n_tiles - 1) * SLC_TILE, LANES)
            vt = vst_ref[0, :, pl.ds(f0, SLC_TILE)]
            states = [_col_softmax_update(states[hk], s_ref[slot, hk], vt, smax=smax_ref[slot, hk, 0:1, :])
                      for hk in heads]
        return states

    def put_logits(slot, kt):
        for hk, s_new in enumerate(far_logits(kt)):
            s_ref[slot, hk] = s_new
            smax_ref[slot, hk] = jnp.broadcast_to(jnp.max(s_new, axis=0, keepdims=True), (8, gq))

    n_far = (qb + 2) // (SLC_TILE // Q_BLOCK)
    put_logits(0, 0)
    far_states = lax.fori_loop(0, (n_far + 1) // 2, far_pair, [init] * N_KV_HEADS)

    k_near = jnp.concatenate([ks_ref[0, pl.ds(p0, SLC_NEAR), :], et_ref[pl.ds(p0, SLC_NEAR), :]], axis=1)
    vt_near = vst_ref[0, :, pl.ds(p0, SLC_NEAR)]
    gst = jax.nn.sigmoid(gn_ref[0]).T
    outs = []
    for hk in heads:
        m_far, l_far, acc_far = far_states[hk]
        state = (m_far + pre[hk]["far_bias"], l_far, acc_far)
        s = jnp.where(krow >= Q_BLOCK - s0, _dot_t(k_near, pre[hk]["q_near"]) + nbs_ref[hk], NEG_INF)
        o_slc = _col_softmax_finish(_col_softmax_update(state, s, vt_near))
        per_g = []
        for g in range(GROUP):
            cols = slice(g * Q_BLOCK, (g + 1) * Q_BLOCK)
            c = 3 * (hk * GROUP + g)
            o = (gst[c:c + 1, :] * pre[hk]["o_cmp"][:, cols] + gst[c + 1:c + 2, :] * o_slc[:, cols]
                 + gst[c + 2:c + 3, :] * pre[hk]["o_win"][:, cols])
            per_g.append(o.T)
        outs.append(per_g)
    lane_h0 = lane < HEAD_DIM
    for g in range(GROUP):
        out_ref[0, :, g * LANES:(g + 1) * LANES] = jnp.where(lane_h0, outs[0][g], outs[1][g]).astype(out_ref.dtype)


def _nsa_prompt_pair(q, gn, kcmp, vcmp, kvb, vs_heads, vw_heads, rel_bias):
    b, t, _ = q.shape
    assert t % SLC_TILE == 0
    nqb = t // Q_BLOCK
    n_chunk = t // CMP_STRIDE
    n_cmp = n_chunk - CMP_RATIO + 1
    n_blocks = t // SEL_BLOCK
    assert n_blocks <= LANES
    cpad = ((0, 0), (CMP_FRONT, LANES - CMP_FRONT), (0, 0))
    kcmp_p = jnp.pad(kcmp, cpad)
    vcmp_p = jnp.pad(vcmp, cpad)
    vcmp_t = vcmp.transpose(0, 2, 1).astype(BF16)
    sm = _sel_map(n_chunk, n_cmp, LANES, n_blocks)
    selmap = jnp.asarray(np.pad(sm, ((CMP_FRONT, LANES - CMP_FRONT), (0, 0))), F32)
    selmap_t = jnp.asarray(sm.T, BF16)
    front = lambda z, n: jnp.pad(z, ((0, 0), (n, 0), (0, 0)))
    keys_front = lambda z, n: jnp.pad(z.reshape(b, KV_WIDTH, t).astype(BF16), ((0, 0), (0, 0), (n, 0)))
    ks_p = front(kvb[:, :, 2 * KV_WIDTH:3 * KV_WIDTH], Q_BLOCK)
    vs_t = keys_front(vs_heads, Q_BLOCK)
    kw_p = front(kvb[:, :, 4 * KV_WIDTH:5 * KV_WIDTH], WINDOW)
    vw_t = keys_front(vw_heads, WINDOW)
    pidx = np.arange(t + Q_BLOCK)
    e = (pidx[:, None] // SEL_BLOCK - Q_BLOCK // SEL_BLOCK == np.arange(LANES)[None, :])
    e = jnp.asarray(e.astype(np.float32), BF16)
    full = lambda arr: pl.BlockSpec((1,) + arr.shape[1:], lambda bi, i: (bi, 0, 0))
    const = lambda arr: pl.BlockSpec(arr.shape, lambda bi, i: (0, 0))
    return pl.pallas_call(
        _nsa_prompt_pair_kernel,
        grid=(b, nqb),
        in_specs=[pl.BlockSpec(memory_space=pltpu.SMEM),
                  pl.BlockSpec((1, Q_BLOCK, NSA_WIDTH), lambda bi, i: (bi, i, 0)),
                  pl.BlockSpec((1, Q_BLOCK, LANES), lambda bi, i: (bi, i, 0)),
                  full(kcmp_p), full(vcmp_p), full(vcmp_t), const(selmap), const(selmap_t),
                  full(ks_p), full(vs_t), full(kw_p), full(vw_t), const(e)],
        out_specs=pl.BlockSpec((1, Q_BLOCK, NSA_WIDTH), lambda bi, i: (bi, i, 0)),
        out_shape=jax.ShapeDtypeStruct((b, t, NSA_WIDTH), BF16),
        scratch_shapes=[pltpu.VMEM((N_KV_HEADS, LANES, GROUP * Q_BLOCK), F32),
                        pltpu.VMEM((N_KV_HEADS, SLC_NEAR, GROUP * Q_BLOCK), F32),
                        pltpu.VMEM((N_KV_HEADS, WIN_KEYS, GROUP * Q_BLOCK), F32),
                        pltpu.VMEM((2, N_KV_HEADS, SLC_TILE, GROUP * Q_BLOCK), F32),
                        pltpu.VMEM((2, N_KV_HEADS, 8, GROUP * Q_BLOCK), F32)],
        compiler_params=_cparams(("arbitrary", "arbitrary")),
        name="nsa_prompt",
    )(rel_bias, q, gn, kcmp_p, vcmp_p, vcmp_t, selmap, selmap_t, ks_p, vs_t, kw_p, vw_t, e)


ROWS_PER_TOKEN = 8


def _nsa_sample_cmp_kernel(q_ref, kcmp_ref, vcmp_ref, selmap_ref, rowtab_ref, ocmp_ref, idx_ref, *,
                           past_len, n_tok, n_cmp, n_blocks):
    rows = n_tok * ROWS_PER_TOKEN
    n_cpad = kcmp_ref.shape[1]
    ridx = lax.broadcasted_iota(jnp.int32, (rows, n_cpad), 0)
    ccol = lax.broadcasted_iota(jnp.int32, (rows, n_cpad), 1)
    dist = past_len + ridx // ROWS_PER_TOKEN - (CMP_STRIDE * ccol + CMP_LEN - 1)
    bucket = _bucket(dist)
    valid = (dist >= 0) & (ccol < n_cmp) & (ridx % ROWS_PER_TOKEN < GROUP)
    n_bpad = selmap_ref.shape[1]
    blk = lax.broadcasted_iota(jnp.int32, (ROWS_PER_TOKEN, n_bpad), 1)
    cur = (past_len + lax.broadcasted_iota(jnp.int32, (ROWS_PER_TOKEN, n_bpad), 0)) // SEL_BLOCK
    forced = (blk == 0) | (blk == cur) | (blk == cur - 1)
    olane = lax.broadcasted_iota(jnp.int32, (ROWS_PER_TOKEN, LANES), 1)
    biases = [_bias_from_rows(bucket, rowtab_ref[hk]) for hk in range(N_KV_HEADS)]
    for bi, hk in [(bi, hk) for bi in range(kcmp_ref.shape[0]) for hk in range(N_KV_HEADS)]:
        kc = kcmp_ref[bi].astype(BF16)
        vc = vcmp_ref[bi].astype(BF16)
        qp = (q_ref[bi, hk].astype(F32) * (HEAD_DIM ** -0.5)).astype(BF16)
        s = _dot_t(qp, kc) + biases[hk]
        s = jnp.where(valid, s, NEG_INF)
        m = jnp.max(s, axis=-1, keepdims=True)
        p = jnp.where(valid, jnp.exp(s - m), 0.0)
        l = jnp.sum(p, axis=-1, keepdims=True)
        pc = p * jnp.where(l > 0.0, 1.0 / jnp.where(l > 0.0, l, 1.0), 0.0)
        ocmp_ref[bi, hk] = _dot(pc.astype(BF16), vc)[:, hk * HEAD_DIM:(hk + 1) * HEAD_DIM]
        ps = jnp.concatenate(
            [jnp.sum(pc[i * ROWS_PER_TOKEN:(i + 1) * ROWS_PER_TOKEN], axis=0, keepdims=True) for i in range(n_tok)]
            + [jnp.zeros((ROWS_PER_TOKEN - n_tok, n_cpad), F32)], axis=0)
        imp = _split_dot(ps, selmap_ref[...])
        val = jnp.where((blk > cur) | (blk >= n_blocks), NEG_INF, imp + FORCE_BONUS * forced.astype(F32))
        picks = jnp.zeros((ROWS_PER_TOKEN, LANES), jnp.int32)
        for n in range(N_SELECT):
            mx = jnp.max(val, axis=-1, keepdims=True)
            first = jnp.min(jnp.where(val == mx, blk, n_bpad), axis=-1, keepdims=True)
            picks = jnp.where(olane == n, first, picks)
            val = jnp.where(blk == first, -3e38, val)
        idx_ref[bi, hk] = picks


def _nsa_sample_cmp(qs, kcmp, vcmp, rowtab, past_len, n_tok, n_cmp, n_blocks):
    nb = qs.shape[0]
    bps = next(d for d in (4, 2, 1) if nb % d == 0)
    n_cpad = kcmp.shape[1]
    n_bpad = -(-n_blocks // LANES) * LANES
    selmap = jnp.asarray(_sel_map(n_cpad, n_cmp, n_bpad, n_blocks), BF16)
    rows = n_tok * ROWS_PER_TOKEN
    return pl.pallas_call(
        functools.partial(_nsa_sample_cmp_kernel, past_len=past_len, n_tok=n_tok, n_cmp=n_cmp, n_blocks=n_blocks),
        grid=(nb // bps,),
        in_specs=[pl.BlockSpec((bps, N_KV_HEADS, rows, LANES), lambda b: (b, 0, 0, 0)),
                  pl.BlockSpec((bps, n_cpad, LANES), lambda b: (b, 0, 0)),
                  pl.BlockSpec((bps, n_cpad, LANES), lambda b: (b, 0, 0)),
                  pl.BlockSpec(selmap.shape, lambda b: (0, 0)),
                  pl.BlockSpec(rowtab.shape, lambda b: (0, 0, 0))],
        out_specs=[pl.BlockSpec((bps, N_KV_HEADS, rows, HEAD_DIM), lambda b: (b, 0, 0, 0)),
                   pl.BlockSpec((bps, N_KV_HEADS, ROWS_PER_TOKEN, LANES), lambda b: (b, 0, 0, 0))],
        out_shape=[jax.ShapeDtypeStruct((nb, N_KV_HEADS, rows, HEAD_DIM), F32),
                   jax.ShapeDtypeStruct((nb, N_KV_HEADS, ROWS_PER_TOKEN, LANES), jnp.int32)],
        compiler_params=_cparams(("arbitrary",)),
        name="nsa_sample_cmp",
    )(qs, kcmp, vcmp, selmap, rowtab)


def _nsa_sample_slc_kernel(idx_ref, pt_ref, ck_ref, cv_ref, q_ref, ocmp_ref, gn_ref, rowtab_ref, newk_ref, newv_ref,
                           swk_ref, swv_ref, nwk_ref, nwv_ref, out_ref, kbuf_ref, vbuf_ref, sem_ref, *,
                           past_len, n_tok, n_cache_blocks, npages):
    b = pl.program_id(0)
    hk = pl.program_id(1)
    step = b * N_KV_HEADS + hk
    last = pl.num_programs(0) * N_KV_HEADS - 1
    slot = step % 2
    per_step = n_tok * N_SELECT
    pages_per_block = PAGE // SEL_BLOCK

    def block_copies(st, i, sl):
        j = jnp.minimum(idx_ref[st * per_step + i], n_cache_blocks - 1)
        page = pt_ref[(st // N_KV_HEADS) * npages + j // pages_per_block]
        head = st % N_KV_HEADS
        return (pltpu.make_async_copy(ck_ref.at[page, head], kbuf_ref.at[sl, i], sem_ref.at[sl]),
                pltpu.make_async_copy(cv_ref.at[page, head], vbuf_ref.at[sl, i], sem_ref.at[sl]))

    @pl.when(step == 0)
    def _():
        for i in range(per_step):
            for c in block_copies(0, i, 0):
                c.start()

    nxt = jnp.minimum(step + 1, last)
    for i in range(per_step):
        for c in block_copies(nxt, i, 1 - slot):
            c.start()
    for i in range(per_step):
        for c in block_copies(step, i, slot):
            c.wait()

    rowtab = rowtab_ref[0]
    rvalid = lax.broadcasted_iota(jnp.int32, (ROWS_PER_TOKEN, 1), 0) < GROUP
    init = (jnp.full((ROWS_PER_TOKEN, 1), NEG_INF, F32), jnp.zeros((ROWS_PER_TOKEN, 1), F32),
            jnp.zeros((ROWS_PER_TOKEN, HEAD_DIM), F32))
    n_win = swk_ref.shape[3]
    kw = jnp.concatenate([swk_ref[0, 0], nwk_ref[0, 0]], axis=1).astype(BF16)
    vw = jnp.concatenate([swv_ref[0, 0], nwv_ref[0, 0]], axis=1).astype(BF16)
    wl = n_win + nwk_ref.shape[3]
    wpos = past_len - n_win + lax.broadcasted_iota(jnp.int32, (ROWS_PER_TOKEN, wl), 1)
    newk = newk_ref[0, 0]
    newv = newv_ref[0, 0]

    for tok in range(n_tok):
        t = past_len + tok
        qp = (q_ref[0, 0, tok].astype(F32) * (HEAD_DIM ** -0.5)).astype(BF16)
        page_row = lax.broadcasted_iota(jnp.int32, (ROWS_PER_TOKEN, PAGE), 1)
        bias_far = jnp.broadcast_to(rowtab[:, N_BUCKETS - 1:N_BUCKETS], (ROWS_PER_TOKEN, PAGE))
        bias_last = _bias_from_rows(_bucket(t - ((npages - 1) * PAGE + page_row)), rowtab)
        bias_new = _bias_from_rows(_bucket(t - (npages * PAGE + page_row)), rowtab)
        assert BUCKET_THR[-1] <= PAGE + 1
        kts, vts, biases = [], [], []
        for n in range(N_SELECT):
            i = tok * N_SELECT + n
            blk = idx_ref[step * per_step + i]
            is_new = blk >= n_cache_blocks
            page = blk // pages_per_block
            kts.append(jnp.where(is_new, newk, kbuf_ref[slot, i]).astype(BF16))
            vts.append(jnp.where(is_new, newv, vbuf_ref[slot, i]).astype(BF16))
            bias = jnp.where(is_new, bias_new, jnp.where(page == npages - 1, bias_last, bias_far))
            ok = (page_row // SEL_BLOCK == blk % pages_per_block) & (page * PAGE + page_row <= t)
            biases.append(jnp.where(ok, bias, NEG_INF))
        s = _dot(qp, jnp.concatenate(kts, axis=1)) + jnp.concatenate(biases, axis=1)
        mask = (s > 0.5 * NEG_INF) & rvalid
        o_slc = _softmax_finish(_softmax_update(init, jnp.where(mask, s, NEG_INF), jnp.concatenate(vts, axis=1),
                                                mask=mask, v_transposed=True))
        wdist = t - wpos
        s = _dot(qp, kw) + _bias_from_rows(_bucket(wdist), rowtab)
        mask = (wdist >= 0) & (wdist < WINDOW) & (wpos >= 0) & rvalid
        o_win = _softmax_finish(_softmax_update(init, jnp.where(mask, s, NEG_INF), vw, mask=mask, v_transposed=True))
        gs = jax.nn.sigmoid(gn_ref[0, 0, tok])
        out_ref[0, 0, tok] = gs[:, 0:1] * ocmp_ref[0, 0, tok] + gs[:, 1:2] * o_slc + gs[:, 2:3] * o_win

    @pl.when(step == last)
    def _():
        for i in range(per_step):
            for c in block_copies(nxt, i, 1 - slot):
                c.wait()


def _nsa_sample_slc(idx, page_table, cache_k, cache_v, q3, ocmp, gn3, rowtab, newk, newv, swk, swv, nwk, nwv,
                    past_len, n_tok):
    nb, npages = page_table.shape
    halves = PAGE // SEL_BLOCK
    n_cache_blocks = npages * halves
    ck = cache_k.transpose(0, 2, 3, 1)
    cv = cache_v.transpose(0, 2, 3, 1)

    tile = lambda w: pl.BlockSpec((1, 1, n_tok, ROWS_PER_TOKEN, w), lambda b, h, i, p: (b, h, 0, 0, 0))
    per_bh = lambda arr: pl.BlockSpec((1, 1) + arr.shape[2:], lambda b, h, i, p: (b, h, 0, 0))
    grid_spec = pltpu.PrefetchScalarGridSpec(
        num_scalar_prefetch=2,
        grid=(nb, N_KV_HEADS),
        in_specs=[pl.BlockSpec(memory_space=pl.ANY), pl.BlockSpec(memory_space=pl.ANY),
                  tile(HEAD_DIM), tile(HEAD_DIM), tile(LANES),
                  pl.BlockSpec((1, ROWS_PER_TOKEN, N_BUCKETS), lambda b, h, i, p: (h, 0, 0)),
                  per_bh(newk), per_bh(newv), per_bh(swk), per_bh(swv), per_bh(nwk), per_bh(nwv)],
        out_specs=tile(HEAD_DIM),
        scratch_shapes=[pltpu.VMEM((2, n_tok * N_SELECT, HEAD_DIM, PAGE), F32),
                        pltpu.VMEM((2, n_tok * N_SELECT, HEAD_DIM, PAGE), F32),
                        pltpu.SemaphoreType.DMA((2,))],
    )
    return pl.pallas_call(
        functools.partial(_nsa_sample_slc_kernel, past_len=past_len, n_tok=n_tok, n_cache_blocks=n_cache_blocks,
                          npages=npages),
        grid_spec=grid_spec,
        out_shape=jax.ShapeDtypeStruct((nb, N_KV_HEADS, n_tok, ROWS_PER_TOKEN, HEAD_DIM), F32),
        compiler_params=_cparams(("arbitrary", "arbitrary")),
        name="nsa_sample_slc",
    )(idx.reshape(-1), page_table.reshape(-1), ck, cv, q3, ocmp, gn3, rowtab, newk, newv, swk, swv, nwk, nwv)


def _merge_tail(x, a, c, gates, wn_ref, wc_ref, wo_ref):
    ga = gates[:, :D_MODEL]
    gb = gates[:, D_MODEL:]
    m = (jax.nn.sigmoid(ga) * _dot(a, wn_ref[...])
         + jax.nn.sigmoid(gb) * _dot(c.astype(BF16), wc_ref[...]))
    return x + _dot(m.astype(BF16), wo_ref[...])


def _merge_prompt_kernel(x_ref, a_ref, conv_ref, halo_ref, gate_ref, cw_ref, cb_ref, wn_ref, wc_ref, wo_ref,
                         out_ref, cst_ref, *, tiles_per_batch):
    i = pl.program_id(0)
    first = i % tiles_per_batch == 0
    conv = conv_ref[...]
    bg = conv[:, :CONV_DIM]
    u = conv[:, CONV_DIM:2 * CONV_DIM] * conv[:, 2 * CONV_DIM:]
    halo = halo_ref[...]
    uh = jnp.where(first, 0.0, halo[:, CONV_DIM:2 * CONV_DIM] * halo[:, 2 * CONV_DIM:])
    tm = u.shape[0]
    row = lax.broadcasted_iota(jnp.int32, (tm, CONV_DIM), 0)
    u1 = jnp.where(row == 0, uh[7:8], pltpu.roll(u, 1, axis=0))
    u2 = pltpu.roll(u, 2, axis=0)
    u2 = jnp.where(row == 0, uh[6:7], jnp.where(row == 1, uh[7:8], u2))
    cw = cw_ref[...]
    y = ((cb_ref[...] + cw[0:1] * u2) + cw[1:2] * u1) + cw[2:3] * u
    c = bg * y
    out_ref[...] = _merge_tail(x_ref[...], a_ref[...], c, gate_ref[...], wn_ref, wc_ref, wo_ref)
    cst_ref[0] = u[tm - 8:, :]


def _merge_prompt(x2d, a2, conv3, gates, conv_w, conv_b, wn, wc, wo, t):
    n = x2d.shape[0]
    tm = min(512, t)
    assert t % tm == 0
    tpb = t // tm
    row = lambda w: pl.BlockSpec((tm, w), lambda i: (i, 0))
    const = lambda arr: pl.BlockSpec(arr.shape, lambda i: (0,) * arr.ndim)
    return pl.pallas_call(
        functools.partial(_merge_prompt_kernel, tiles_per_batch=tpb),
        grid=(n // tm,),
        in_specs=[row(D_MODEL),
                  row(NSA_WIDTH),
                  row(3 * CONV_DIM),
                  pl.BlockSpec((8, 3 * CONV_DIM), lambda i: (jnp.maximum(i * (tm // 8) - 1, 0), 0)),
                  row(2 * D_MODEL),
                  const(conv_w), const(conv_b), const(wn), const(wc), const(wo)],
        out_specs=[row(D_MODEL), pl.BlockSpec((1, 8, CONV_DIM), lambda i: (i // tpb, 0, 0))],
        out_shape=[jax.ShapeDtypeStruct((n, D_MODEL), F32),
                   jax.ShapeDtypeStruct((n // t, 8, CONV_DIM), F32)],
        compiler_params=_cparams(("arbitrary",)),
        name="merge_prompt",
    )(x2d, a2, conv3, conv3, gates, conv_w, conv_b, wn, wc, wo)


def _merge_sample_kernel(x_ref, a_ref, conv_ref, past_ref, gate_ref, cw_ref, cb_ref, wn_ref, wc_ref, wo_ref,
                         out_ref, cst_ref, *, n_tok, nb):
    conv = conv_ref[...]
    bg = conv[:, :CONV_DIM]
    u = conv[:, CONV_DIM:2 * CONV_DIM] * conv[:, 2 * CONV_DIM:]
    up = jnp.concatenate([past_ref[...], u], axis=0)
    cw = cw_ref[...]
    y = cb_ref[...]
    for k in range(CONV_WIDTH):
        y = y + cw[k:k + 1] * up[k * nb:(k + n_tok) * nb]
    c = bg * y
    out_ref[...] = _merge_tail(x_ref[...], a_ref[...].astype(BF16), c, gate_ref[...], wn_ref, wc_ref, wo_ref)
    cst_ref[...] = up[n_tok * nb:]


def _merge_sample(x_tm, a_tm, conv_tm, past_tm, gates_tm, conv_w, conv_b, wn, wc, wo, n_tok, nb):
    n = x_tm.shape[0]
    full = lambda arr: pl.BlockSpec(arr.shape, lambda i: (0,) * arr.ndim)
    args = (x_tm, a_tm, conv_tm, past_tm, gates_tm, conv_w, conv_b, wn, wc, wo)
    return pl.pallas_call(
        functools.partial(_merge_sample_kernel, n_tok=n_tok, nb=nb),
        grid=(1,),
        in_specs=[full(a) for a in args],
        out_specs=[pl.BlockSpec((n, D_MODEL), lambda i: (0, 0)),
                   pl.BlockSpec(((CONV_WIDTH - 1) * nb, CONV_DIM), lambda i: (0, 0))],
        out_shape=[jax.ShapeDtypeStruct((n, D_MODEL), F32),
                   jax.ShapeDtypeStruct(((CONV_WIDTH - 1) * nb, CONV_DIM), F32)],
        compiler_params=_cparams(("arbitrary",)),
        name="merge_sample",
    )(*args)


def _rms(x, g):
    return (x * lax.rsqrt(jnp.mean(x * x, axis=-1, keepdims=True) + RMS_EPS)) * g


def _mlp_kernel(x_ref, g_ref, wu_ref, wd_ref, gf_ref, out_ref, h_ref, acc_ref):
    j = pl.program_id(1)

    @pl.when(j == 0)
    def _():
        h_ref[...] = _rms(x_ref[...], g_ref[...]).astype(BF16)
        acc_ref[...] = jnp.zeros(acc_ref.shape, F32)

    half = h_ref.shape[0] // 2
    for r in range(2):
        rows = slice(r * half, (r + 1) * half)
        up = jnp.square(jnp.maximum(_dot(h_ref[rows, :], wu_ref[...]), 0.0)).astype(BF16)
        acc_ref[rows, :] += _dot(up, wd_ref[...])

    @pl.when(j == pl.num_programs(1) - 1)
    def _():
        out_ref[...] = _rms(x_ref[...] + acc_ref[...], gf_ref[...])


def _mlp(x2d, g_mlp, w_up, w_down, g_final):
    n = x2d.shape[0]
    tm = min(1024, n)
    tf = 1024
    assert n % tm == 0 and D_FF % tf == 0
    return pl.pallas_call(
        _mlp_kernel,
        grid=(n // tm, D_FF // tf),
        in_specs=[pl.BlockSpec((tm, D_MODEL), lambda i, j: (i, 0)),
                  pl.BlockSpec((1, D_MODEL), lambda i, j: (0, 0)),
                  pl.BlockSpec((D_MODEL, tf), lambda i, j: (0, j)),
                  pl.BlockSpec((tf, D_MODEL), lambda i, j: (j, 0)),
                  pl.BlockSpec((1, D_MODEL), lambda i, j: (0, 0))],
        out_specs=pl.BlockSpec((tm, D_MODEL), lambda i, j: (i, 0)),
        out_shape=jax.ShapeDtypeStruct((n, D_MODEL), F32),
        scratch_shapes=[pltpu.VMEM((tm, D_MODEL), BF16), pltpu.VMEM((tm, D_MODEL), F32)],
        compiler_params=_cparams(("arbitrary", "arbitrary")),
        name="mlp",
    )(x2d, g_mlp, w_up, w_down, g_final)


def _kv_rows(z, lead):
    return z.reshape(lead + (N_KV_HEADS, HEAD_DIM))


def _layer(xp, xs, caches, states, page_table, w, rel_bias):
    (cache_cmp_k, cache_cmp_v, cache_slc_k, cache_slc_v) = caches
    (state_win_k, state_win_v, state_conv) = states
    nbp, t, _ = xp.shape
    nbs, n_tok, _ = xs.shape
    npages = page_table.shape[1]
    past_len = npages * PAGE
    assert t % PAGE == 0 and n_tok <= GROUP and state_win_k.shape[1] >= n_tok

    w_all = _prep_w_in(w["w_in"])
    g_attn = w["g_attn"].reshape(1, D_MODEL)
    wn = w["w_nsa_out"].reshape(N_KV_HEADS, GROUP, HEAD_DIM, D_MODEL).transpose(1, 0, 2, 3)
    wn = wn.reshape(NSA_WIDTH, D_MODEL).astype(BF16)
    wc = w["w_conv_out"].astype(BF16)
    wo = w["w_o"].astype(BF16)
    wu = w["w_up"].astype(BF16)
    wd = w["w_down"].astype(BF16)
    g_mlp = w["g_mlp"].reshape(1, D_MODEL)
    g_final = w["g_final"].reshape(1, D_MODEL)
    conv_w = w["conv_w"]
    conv_b = w["conv_b"].reshape(1, CONV_DIM)
    cmp_k = (w["cmp_pe_k"], w["cmp_w1_k"], w["cmp_w2_k"])
    cmp_v = (w["cmp_pe_v"], w["cmp_w1_v"], w["cmp_w2_v"])

    n_p = nbp * t
    (q, kc, vc, *kv_heads, kvb, gn, conv3, gates) = _proj(xp.reshape(n_p, D_MODEL), g_attn, w_all, True, t)
    ident = jnp.arange(nbp * (t // PAGE), dtype=jnp.int32).reshape(nbp, t // PAGE)
    kcmp = _compress(kc.reshape(n_p // PAGE, PAGE, KV_WIDTH), 0, ident, *cmp_k)
    vcmp = _compress(vc.reshape(n_p // PAGE, PAGE, KV_WIDTH), 0, ident, *cmp_v)
    a2 = _nsa_prompt_pair(q.reshape(nbp, t, NSA_WIDTH), gn.reshape(nbp, t, LANES), kcmp, vcmp,
                          kvb.reshape(nbp, t, 6 * KV_WIDTH), kv_heads[3], kv_heads[5], rel_bias)
    x1, cst = _merge_prompt(xp.reshape(n_p, D_MODEL), a2.reshape(n_p, NSA_WIDTH),
                            conv3, gates, conv_w, conv_b, wn, wc, wo, t)
    yp = _mlp(x1, g_mlp, wu, wd, g_final).reshape(nbp, t, D_MODEL)
    win_p = min(WINDOW, t)
    back = lambda z: z.transpose(0, 3, 1, 2)
    new_p = tuple(back(z) for z in kv_heads[:4]) + (
        back(kv_heads[4][..., t - win_p:]), back(kv_heads[5][..., t - win_p:]), cst[:, 8 - (CONV_WIDTH - 1):, :])

    n_s = nbs * n_tok
    (qs, kcs, vcs, kss, vss, kws, vws, _, gns, conv3s, gatess) = _proj(xs.reshape(n_s, D_MODEL), g_attn, w_all, False)
    n_rows = past_len + n_tok
    n_chunk = n_rows // CMP_STRIDE
    assert n_chunk == past_len // CMP_STRIDE, "new rows must not complete a compression chunk"
    n_cmp = n_chunk - CMP_RATIO + 1
    n_blocks = -(-n_rows // SEL_BLOCK)
    kcmp_s, vcmp_s = _compress_t((cache_cmp_k, cache_cmp_v), page_table, (cmp_k, cmp_v))

    lane = np.arange(LANES)
    q5 = qs.reshape(nbs, n_tok, GROUP, LANES)
    q3 = jnp.stack([jnp.where(jnp.asarray((lane // HEAD_DIM) == h), q5, jnp.zeros_like(q5))
                    for h in range(N_KV_HEADS)], axis=1)
    q3 = jnp.pad(q3, ((0, 0), (0, 0), (0, 0), (0, ROWS_PER_TOKEN - GROUP), (0, 0)))
    tab = rel_bias.T.reshape(N_KV_HEADS, GROUP, N_BUCKETS)
    rowtab3 = jnp.pad(tab, ((0, 0), (0, ROWS_PER_TOKEN - GROUP), (0, 0)))
    rowtab = jnp.tile(rowtab3, (1, n_tok, 1))
    ocmp, idx = _nsa_sample_cmp(q3.reshape(nbs, N_KV_HEADS, n_tok * ROWS_PER_TOKEN, LANES), kcmp_s, vcmp_s,
                                rowtab, past_len, n_tok, n_cmp, n_blocks)
    idx = idx[:, :, :n_tok, :N_SELECT]
    gn3 = gns[:, :3 * N_HEADS].reshape(nbs, n_tok, N_KV_HEADS, GROUP, 3).transpose(0, 2, 1, 3, 4)
    gn3 = jnp.pad(gn3, ((0, 0), (0, 0), (0, 0), (0, ROWS_PER_TOKEN - GROUP), (0, LANES - 3)))
    q3s = jnp.pad(qs.reshape(nbs, n_tok, GROUP, N_KV_HEADS, HEAD_DIM).transpose(0, 3, 1, 2, 4),
                  ((0, 0), (0, 0), (0, 0), (0, ROWS_PER_TOKEN - GROUP), (0, 0)))
    ocmp5 = ocmp.reshape(nbs, N_KV_HEADS, n_tok, ROWS_PER_TOKEN, HEAD_DIM)
    new_t = lambda z: jnp.pad(z.reshape(nbs, n_tok, N_KV_HEADS, HEAD_DIM).transpose(0, 2, 3, 1),
                              ((0, 0), (0, 0), (0, 0), (0, LANES - n_tok)))
    a_s = _nsa_sample_slc(idx, page_table, cache_slc_k, cache_slc_v, q3s, ocmp5, gn3, rowtab3,
                          new_t(kss), new_t(vss),
                          state_win_k.transpose(0, 2, 3, 1), state_win_v.transpose(0, 2, 3, 1),
                          new_t(kws), new_t(vws), past_len, n_tok)
    a_tm = a_s[:, :, :, :GROUP, :].transpose(2, 0, 3, 1, 4).reshape(n_s, NSA_WIDTH)
    to_tm = lambda z: z.reshape(nbs, n_tok, -1).transpose(1, 0, 2).reshape(n_s, -1)
    past_tm = state_conv.transpose(1, 0, 2).reshape((CONV_WIDTH - 1) * nbs, CONV_DIM)
    x1s, csts = _merge_sample(to_tm(xs), a_tm, to_tm(conv3s), past_tm, to_tm(gatess),
                              conv_w, conv_b, wn, wc, wo, n_tok, nbs)
    ys = _mlp(x1s, g_mlp, wu, wd, g_final).reshape(n_tok, nbs, D_MODEL).transpose(1, 0, 2)
    win_buf = state_win_k.shape[1]
    kw_all = jnp.concatenate([state_win_k, _kv_rows(kws, (nbs, n_tok))], axis=1)[:, -win_buf:]
    vw_all = jnp.concatenate([state_win_v, _kv_rows(vws, (nbs, n_tok))], axis=1)[:, -win_buf:]
    new_s = tuple(_kv_rows(z, (nbs, n_tok)) for z in (kcs, vcs, kss, vss)) + (
        kw_all, vw_all, csts.reshape(CONV_WIDTH - 1, nbs, CONV_DIM).transpose(1, 0, 2))
    return yp, ys, new_p, new_s


def kernel(x_prompt, x_sample, cache_cmp_k, cache_cmp_v, cache_slc_k, cache_slc_v, state_win_k, state_win_v,
           state_conv, page_table, g_attn, w_in, cmp_pe_k, cmp_w1_k, cmp_w2_k, cmp_pe_v, cmp_w1_v, cmp_w2_v,
           conv_w, conv_b, w_nsa_out, w_conv_out, w_o, g_mlp, w_up, w_down, rel_bias, g_final):
    depth = w_in.shape[0]
    assert depth == 1, "the final norm is fused into the last layer's MLP kernel"
    names = ("g_attn", "w_in", "cmp_pe_k", "cmp_w1_k", "cmp_w2_k", "cmp_pe_v", "cmp_w1_v", "cmp_w2_v",
             "conv_w", "conv_b", "w_nsa_out", "w_conv_out", "w_o", "g_mlp", "w_up", "w_down")
    vals = (g_attn, w_in, cmp_pe_k, cmp_w1_k, cmp_w2_k, cmp_pe_v, cmp_w1_v, cmp_w2_v,
            conv_w, conv_b, w_nsa_out, w_conv_out, w_o, g_mlp, w_up, w_down)
    l = 0
    w = {k: v[l] for k, v in zip(names, vals)}
    w["g_final"] = g_final
    yp, ys, new_p, new_s = _layer(
        x_prompt, x_sample,
        (cache_cmp_k[l], cache_cmp_v[l], cache_slc_k[l], cache_slc_v[l]),
        (state_win_k[l], state_win_v[l], state_conv[l]), page_table, w, rel_bias)
    return (yp, ys) + tuple(z[None] for z in new_p) + tuple(z[None] for z in new_s)
```

```python
import functools
import math

import numpy as np
import jax
import jax.numpy as jnp
from jax import lax
from jax.experimental import pallas as pl
from jax.experimental.pallas import tpu as pltpu

F32 = jnp.float32
BF16 = jnp.bfloat16

D_MODEL = 1024
N_HEADS = 8
HEAD_DIM = 64
N_KV_HEADS = 2
GROUP = N_HEADS // N_KV_HEADS
KV_WIDTH = N_KV_HEADS * HEAD_DIM
NSA_WIDTH = N_HEADS * HEAD_DIM
CMP_LEN = 32
CMP_STRIDE = 16
CMP_RATIO = CMP_LEN // CMP_STRIDE
CMP_HIDDEN = 256
SEL_BLOCK = 64
N_SELECT = 16
WINDOW = 512
CONV_DIM = 512
CONV_WIDTH = 3
D_FF = 4 * D_MODEL
N_BUCKETS = 32
MAX_DISTANCE = 128
Q_BLOCK = 128
PAGE = 128
RMS_EPS = 1e-6
NEG_INF = -1e30
FORCE_BONUS = 1e4
LANES = 128
VMEM_LIMIT = 56 * 1024 * 1024


def _bucket_thresholds():
    max_exact = N_BUCKETS // 2
    n = np.arange(max_exact, 4 * MAX_DISTANCE).astype(np.float32)
    large = max_exact + (np.log(n / np.float32(max_exact)) / np.float32(math.log(MAX_DISTANCE / max_exact))
                         * np.float32(N_BUCKETS - max_exact)).astype(np.int32)
    large = np.minimum(large, N_BUCKETS - 1)
    thr = []
    for k in range(max_exact + 1, N_BUCKETS):
        thr.append(int(np.arange(max_exact, 4 * MAX_DISTANCE)[np.argmax(large >= k)]))
    return tuple(thr)


BUCKET_THR = _bucket_thresholds()


def _bucket(dist):
    n = jnp.maximum(dist, 0)
    big = jnp.full(n.shape, N_BUCKETS // 2, jnp.int32)
    for thr in BUCKET_THR:
        big = big + (n >= thr).astype(jnp.int32)
    return jnp.where(n < N_BUCKETS // 2, n, big)


def _bias_from_scalars(bucket, table_fn):
    out = jnp.zeros(bucket.shape, F32)
    for k in range(N_BUCKETS):
        out = jnp.where(bucket == k, table_fn(k), out)
    return out


def _bias_from_rows(bucket, rowtab):
    out = jnp.zeros(bucket.shape, F32)
    for k in range(N_BUCKETS):
        out = jnp.where(bucket == k, rowtab[:, k:k + 1], out)
    return out


def _dot(a, b):
    return jnp.dot(a, b, preferred_element_type=F32)


def _dot_t(a, b):
    return lax.dot_general(a, b, (((1,), (1,)), ((), ())), preferred_element_type=F32)


def _split_dot(p, m):
    hi = p.astype(BF16)
    lo = (p - hi.astype(F32)).astype(BF16)
    return _dot(hi, m) + _dot(lo, m)


def _cparams(sem):
    return pltpu.CompilerParams(dimension_semantics=sem, vmem_limit_bytes=VMEM_LIMIT)


PROJ_WIDTHS = (NSA_WIDTH, 6 * KV_WIDTH, LANES, 3 * CONV_DIM, 2 * D_MODEL)
PROJ_COLS = sum(PROJ_WIDTHS)


def _proj_kernel(x_ref, g_ref, w_ref, *out_refs, head_rows):
    q_ref, kvb_ref, gn_ref, conv_ref, gate_ref = out_refs[0], *out_refs[-4:]
    kv_refs = out_refs[1:-4]
    x = x_ref[...]
    y = x * lax.rsqrt(jnp.mean(x * x, axis=-1, keepdims=True) + RMS_EPS)
    hb = (y * g_ref[...]).astype(BF16)
    q_ref[...] = _dot(hb, w_ref[:, 0:NSA_WIDTH]).astype(q_ref.dtype)
    off = NSA_WIDTH
    kv = _dot(hb, w_ref[:, off:off + 6 * KV_WIDTH])
    tm = kv.shape[0]
    for i in range(6):
        z = kv[:, i * KV_WIDTH:(i + 1) * KV_WIDTH]
        if head_rows:
            kv_refs[2 + i][0] = z.T.reshape(N_KV_HEADS, HEAD_DIM, tm)
        if not head_rows or i < 2:
            kv_refs[i][...] = z
    kvb_ref[...] = kv.astype(BF16)
    off += 6 * KV_WIDTH
    gn_ref[...] = _dot(hb, w_ref[:, off:off + LANES])
    off += LANES
    conv_ref[...] = _dot(hb, w_ref[:, off:off + 3 * CONV_DIM])
    off += 3 * CONV_DIM
    gate_ref[...] = _dot(hb, w_ref[:, off:off + 2 * D_MODEL])


def _proj(x2d, g, w_all, head_rows, t=None):
    n = x2d.shape[0]
    tm = min(512, n if t is None else t)
    assert n % tm == 0
    row = lambda w: pl.BlockSpec((tm, w), lambda i: (i, 0))
    dense = lambda w, dt: (jax.ShapeDtypeStruct((n, w), dt), row(w))
    if head_rows:
        tpb = t // tm
        kv_outs = [dense(KV_WIDTH, F32)] * 2 + [
            (jax.ShapeDtypeStruct((n // t, N_KV_HEADS, HEAD_DIM, t), F32),
             pl.BlockSpec((1, N_KV_HEADS, HEAD_DIM, tm), lambda i: (i // tpb, 0, 0, i % tpb)))] * 6
    else:
        kv_outs = [dense(KV_WIDTH, F32)] * 6
    outs = ([dense(NSA_WIDTH, BF16)] + kv_outs
            + [dense(6 * KV_WIDTH, BF16), dense(LANES, F32), dense(3 * CONV_DIM, F32), dense(2 * D_MODEL, F32)])
    return pl.pallas_call(
        functools.partial(_proj_kernel, head_rows=head_rows),
        grid=(n // tm,),
        in_specs=[row(D_MODEL),
                  pl.BlockSpec((1, D_MODEL), lambda i: (0, 0)),
                  pl.BlockSpec((D_MODEL, PROJ_COLS), lambda i: (0, 0))],
        out_specs=[o[1] for o in outs],
        out_shape=[o[0] for o in outs],
        compiler_params=_cparams(("arbitrary",)),
        name="proj",
    )(x2d, g, w_all)


def _prep_w_in(w_in):
    sizes = (NSA_WIDTH,) + (KV_WIDTH,) * 6 + (3 * N_HEADS, CONV_DIM, CONV_DIM, CONV_DIM, D_MODEL, D_MODEL)
    offs = np.concatenate([[0], np.cumsum(sizes)])
    wq = w_in[:, :NSA_WIDTH].reshape(D_MODEL, N_KV_HEADS, GROUP, HEAD_DIM)
    wq = wq.transpose(0, 2, 1, 3).reshape(D_MODEL, NSA_WIDTH)
    wkv = w_in[:, offs[1]:offs[7]]
    wgn = jnp.pad(w_in[:, offs[7]:offs[8]], ((0, 0), (0, LANES - 3 * N_HEADS)))
    wrest = w_in[:, offs[8]:]
    return jnp.concatenate([wq, wkv, wgn, wrest], axis=1).astype(BF16)


CHUNKS_PER_PAGE = PAGE // CMP_STRIDE


def _compress_kernel(pt_ref, *refs, pp):
    del pt_ref
    page_refs = refs[:pp + 1]
    wbd_ref, pe_ref, w1_ref, w2_ref, out_ref, acc_ref, pehid_ref = refs[pp + 1:]

    @pl.when((pl.program_id(0) == 0) & (pl.program_id(1) == 0))
    def _():
        pehid_ref[...] = _dot(pe_ref[...], w1_ref[...])

    cols = []
    for r in range(CMP_STRIDE):
        xr = jnp.concatenate([p[0, pl.ds(r, CHUNKS_PER_PAGE, stride=CMP_STRIDE), :] for p in page_refs], axis=0)
        cols.append(xr.astype(BF16))
    x = jnp.concatenate(cols, axis=1)
    acc_ref[...] = _dot(x, wbd_ref[...])
    nc = CHUNKS_PER_PAGE * pp
    pe = pehid_ref[0:1, :]
    hids = []
    for h in range(N_KV_HEADS):
        base = h * CMP_RATIO * CMP_HIDDEN
        p0 = acc_ref[0:nc, base:base + CMP_HIDDEN]
        p1 = acc_ref[pl.ds(1, nc), base + CMP_HIDDEN:base + 2 * CMP_HIDDEN]
        hids.append(jax.nn.gelu((pe + p0) + p1).astype(BF16))
    out_ref[0] = _dot(jnp.concatenate(hids, axis=1), w2_ref[...])


def _compress(pages, colblk, page_table, pe, w1, w2):
    nb, npages = page_table.shape
    pp = min(32, npages)
    assert npages % pp == 0
    nc = CHUNKS_PER_PAGE * pp
    w1r = w1.reshape(CMP_RATIO, CMP_STRIDE, HEAD_DIM, CMP_HIDDEN)
    blk = w1r.transpose(1, 2, 0, 3).reshape(CMP_STRIDE, HEAD_DIM, CMP_RATIO * CMP_HIDDEN)
    z = jnp.zeros_like(blk)
    wbd = jnp.concatenate([jnp.concatenate([blk, z], axis=2), jnp.concatenate([z, blk], axis=2)], axis=1)
    wbd = wbd.reshape(CMP_STRIDE * KV_WIDTH, N_KV_HEADS * CMP_RATIO * CMP_HIDDEN).astype(BF16)
    z2 = jnp.zeros_like(w2)
    w2bd = jnp.concatenate([jnp.concatenate([w2, z2], axis=1), jnp.concatenate([z2, w2], axis=1)], axis=0).astype(BF16)
    pe8 = jnp.broadcast_to(pe.reshape(1, CMP_LEN * HEAD_DIM), (8, CMP_LEN * HEAD_DIM)).astype(BF16)

    def page_spec(p):
        def imap(b, s, pt):
            return (pt[b, jnp.minimum(s * pp + p, npages - 1)], 0, colblk)
        return pl.BlockSpec((1, PAGE, KV_WIDTH), imap)

    const = lambda shape: pl.BlockSpec(shape, lambda b, s, pt: (0,) * len(shape))
    grid_spec = pltpu.PrefetchScalarGridSpec(
        num_scalar_prefetch=1,
        grid=(nb, npages // pp),
        in_specs=[page_spec(p) for p in range(pp + 1)]
        + [const(wbd.shape), const(pe8.shape), const((CMP_LEN * HEAD_DIM, CMP_HIDDEN)), const(w2bd.shape)],
        out_specs=pl.BlockSpec((1, nc, KV_WIDTH), lambda b, s, pt: (b, s, 0)),
        scratch_shapes=[pltpu.VMEM((nc + 8, N_KV_HEADS * CMP_RATIO * CMP_HIDDEN), F32),
                        pltpu.VMEM((8, CMP_HIDDEN), F32)],
    )
    return pl.pallas_call(
        functools.partial(_compress_kernel, pp=pp),
        grid_spec=grid_spec,
        out_shape=jax.ShapeDtypeStruct((nb, npages * CHUNKS_PER_PAGE, KV_WIDTH), F32),
        compiler_params=_cparams(("arbitrary", "arbitrary")),
        name="compress",
    )(page_table, *([pages] * (pp + 1)), wbd, pe8, w1.astype(BF16), w2bd)


def _compress_t_kernel(pt_ref, *refs, pp, npages, ns):
    cache_refs = refs[:ns]
    weight_refs = [refs[ns + 4 * c:ns + 4 * c + 4] for c in range(ns)]
    out_refs = refs[5 * ns:6 * ns]
    pbuf_ref, sem_ref, xt_ref, acc_ref, pehid_ref = refs[6 * ns:]
    b = pl.program_id(0)
    s = pl.program_id(1)
    n_steps = pl.num_programs(1)
    step = b * n_steps + s
    last = pl.num_programs(0) * n_steps - 1
    slot = step % 2
    copies = [(c, p) for c in range(ns) for p in range(pp + 1)]

    def page_copy(bb, ss, c, p, sl):
        page = pt_ref[bb, jnp.minimum(ss * pp + p, npages - 1)]
        return pltpu.make_async_copy(cache_refs[c].at[page], pbuf_ref.at[sl, c, p], sem_ref.at[sl])

    @pl.when(step == 0)
    def _():
        for c in range(ns):
            pehid_ref[c] = _dot(weight_refs[c][1][...], weight_refs[c][2][...])
        for c, p in copies:
            page_copy(0, 0, c, p, 0).start()

    wrap = s + 1 == n_steps
    nxt_b = jnp.minimum(jnp.where(wrap, b + 1, b), pl.num_programs(0) - 1)
    nxt_s = jnp.where(step == last, s, jnp.where(wrap, 0, s + 1))
    for c, p in copies:
        page_copy(nxt_b, nxt_s, c, p, 1 - slot).start()
    for c, p in copies:
        page_copy(b, s, c, p, slot).wait()

    nc = CHUNKS_PER_PAGE * pp
    m = nc + CHUNKS_PER_PAGE
    first_head = lax.broadcasted_iota(jnp.int32, (m, KV_WIDTH), 1) < HEAD_DIM
    for c in range(ns):
        w1p_ref, _, _, w2_ref = weight_refs[c]
        for p in range(pp + 1):
            page = pbuf_ref[slot, c, p].reshape(KV_WIDTH, PAGE)
            xt_ref[c, p * PAGE:(p + 1) * PAGE, :] = page.T
        pe = pehid_ref[c, 0:1, :]
        xs = [[], []]
        for r in range(0, CMP_STRIDE, 2):
            a = xt_ref[c, pl.ds(r, m, stride=CMP_STRIDE), :]
            d = xt_ref[c, pl.ds(r + 1, m, stride=CMP_STRIDE), :]
            xs[0].append(jnp.where(first_head, a, pltpu.roll(d, HEAD_DIM, axis=1)).astype(BF16))
            xs[1].append(jnp.where(first_head, pltpu.roll(a, HEAD_DIM, axis=1), d).astype(BF16))
        hids = []
        for h in range(N_KV_HEADS):
            x = jnp.concatenate(xs[h], axis=1)
            acc_ref[c, h] = _dot(x, w1p_ref[...])
            p0 = acc_ref[c, h, 0:nc, 0:CMP_HIDDEN]
            p1 = acc_ref[c, h, pl.ds(1, nc), CMP_HIDDEN:2 * CMP_HIDDEN]
            hids.append(jax.nn.gelu((pe + p0) + p1).astype(BF16))
        out_refs[c][0] = _dot(jnp.concatenate(hids, axis=1), w2_ref[...])

    @pl.when(step == last)
    def _():
        for c, p in copies:
            page_copy(nxt_b, nxt_s, c, p, 1 - slot).wait()


def _compress_t(caches, page_table, weights):
    nb, npages = page_table.shape
    ns = len(caches)
    pp = min(32, npages)
    assert npages % pp == 0
    nc = CHUNKS_PER_PAGE * pp
    pages = [c.transpose(0, 2, 3, 1) for c in caches]
    wargs = []
    for pe, w1, w2 in weights:
        w1p = w1.reshape(CMP_RATIO, CMP_STRIDE * HEAD_DIM, CMP_HIDDEN).transpose(1, 0, 2)
        w1p = w1p.reshape(CMP_STRIDE * HEAD_DIM, CMP_RATIO * CMP_HIDDEN).astype(BF16)
        z2 = jnp.zeros_like(w2)
        w2bd = jnp.concatenate([jnp.concatenate([w2, z2], axis=1), jnp.concatenate([z2, w2], axis=1)], axis=0)
        pe8 = jnp.broadcast_to(pe.reshape(1, CMP_LEN * HEAD_DIM), (8, CMP_LEN * HEAD_DIM)).astype(BF16)
        wargs += [w1p, pe8, w1.astype(BF16), w2bd.astype(BF16)]

    const = lambda arr: pl.BlockSpec(arr.shape, lambda b, s, pt: (0,) * arr.ndim)
    out_spec = pl.BlockSpec((1, nc, KV_WIDTH), lambda b, s, pt: (b, s, 0))
    grid_spec = pltpu.PrefetchScalarGridSpec(
        num_scalar_prefetch=1,
        grid=(nb, npages // pp),
        in_specs=[pl.BlockSpec(memory_space=pl.ANY)] * ns + [const(a) for a in wargs],
        out_specs=[out_spec] * ns,
        scratch_shapes=[pltpu.VMEM((2, ns, pp + 1, N_KV_HEADS, HEAD_DIM, PAGE), F32),
                        pltpu.SemaphoreType.DMA((2,)),
                        pltpu.VMEM((ns, (pp + 1) * PAGE, KV_WIDTH), F32),
                        pltpu.VMEM((ns, N_KV_HEADS, nc + CHUNKS_PER_PAGE, CMP_RATIO * CMP_HIDDEN), F32),
                        pltpu.VMEM((ns, 8, CMP_HIDDEN), F32)],
    )
    return pl.pallas_call(
        functools.partial(_compress_t_kernel, pp=pp, npages=npages, ns=ns),
        grid_spec=grid_spec,
        out_shape=[jax.ShapeDtypeStruct((nb, npages * CHUNKS_PER_PAGE, KV_WIDTH), F32)] * ns,
        compiler_params=_cparams(("arbitrary", "arbitrary")),
        name="compress_t",
    )(page_table, *pages, *wargs)


def _sel_map(n_rows, n_cmp, n_cols, n_blocks):
    c0 = np.arange(n_rows)[:, None] * CMP_STRIDE
    s0 = np.arange(n_cols)[None, :] * SEL_BLOCK
    ov = np.minimum(c0 + CMP_LEN, s0 + SEL_BLOCK) - np.maximum(c0, s0)
    m = np.clip(ov, 0, None).astype(np.float32) / CMP_LEN
    m[n_cmp:] = 0.0
    m[:, n_blocks:] = 0.0
    return m


CMP_FRONT = 16
CMP_NEAR = 24
SLC_NEAR = 2 * Q_BLOCK
SLC_TILE = 512
WIN_KEYS = WINDOW + Q_BLOCK


def _softmax_update(state, s, v, mask=None, v_transposed=False):
    m_old, l_old, acc = state
    m_new = jnp.maximum(m_old, jnp.max(s, axis=-1, keepdims=True))
    p = jnp.exp(s - m_new)
    if mask is not None:
        p = jnp.where(mask, p, 0.0)
    alpha = jnp.exp(m_old - m_new)
    l_new = alpha * l_old + jnp.sum(p, axis=-1, keepdims=True)
    pv = _dot_t(p.astype(BF16), v) if v_transposed else _dot(p.astype(BF16), v)
    return m_new, l_new, alpha * acc + pv


def _softmax_finish(state):
    _, l, acc = state
    return jnp.where(l > 0.0, acc / jnp.where(l > 0.0, l, 1.0), 0.0)


def _tile_rows(x, n):
    return jnp.concatenate([x] * n, axis=0)


def _col_softmax_update(state, s, vt, smax=None):
    m_old, l_old, acc = state
    m_new = jnp.maximum(m_old, jnp.max(s, axis=0, keepdims=True) if smax is None else smax)
    p = jnp.exp(s - m_new)
    alpha = jnp.exp(m_old - m_new)
    l_new = alpha * l_old + jnp.sum(p, axis=0, keepdims=True)
    return m_new, l_new, alpha * acc + _dot(vt, p.astype(BF16))


def _col_softmax_finish(state):
    _, l, acc = state
    return acc * jnp.where(l > 0.0, 1.0 / jnp.where(l > 0.0, l, 1.0), 0.0)


def _nsa_prompt_pair_kernel(rb_ref, q_ref, gn_ref, kcmp_ref, vcmp_ref, vcmpt_ref, selmap_ref, selmapt_ref,
                            ks_ref, vst_ref, kw_ref, vwt_ref, et_ref, out_ref, nbc_ref, nbs_ref, nbw_ref, s_ref,
                            smax_ref):
    qb = pl.program_id(1)
    s0 = qb * Q_BLOCK
    gq = GROUP * Q_BLOCK
    heads = range(N_KV_HEADS)

    @pl.when(qb == 0)
    def _():
        for hk in heads:
            for g in range(GROUP):
                tab = lambda k, h=hk * GROUP + g: rb_ref[k, h]
                cols = slice(g * Q_BLOCK, (g + 1) * Q_BLOCK)
                cl = lax.broadcasted_iota(jnp.int32, (LANES, Q_BLOCK), 0)
                ql = lax.broadcasted_iota(jnp.int32, (LANES, Q_BLOCK), 1)
                dist = ql - CMP_STRIDE * (cl - CMP_FRONT) - (CMP_LEN - 1)
                ok = (dist >= 0) & (cl < CMP_NEAR)
                nbc_ref[hk, :, cols] = jnp.where(ok, _bias_from_scalars(_bucket(dist), tab), NEG_INF)
                kl = lax.broadcasted_iota(jnp.int32, (SLC_NEAR, Q_BLOCK), 0)
                ql = lax.broadcasted_iota(jnp.int32, (SLC_NEAR, Q_BLOCK), 1)
                dist = ql - kl + Q_BLOCK
                nbs_ref[hk, :, cols] = jnp.where(dist >= 0, _bias_from_scalars(_bucket(dist), tab), NEG_INF)
                kl = lax.broadcasted_iota(jnp.int32, (WIN_KEYS, Q_BLOCK), 0)
                ql = lax.broadcasted_iota(jnp.int32, (WIN_KEYS, Q_BLOCK), 1)
                dist = ql - kl + WINDOW
                ok = (dist >= 0) & (dist < WINDOW)
                nbw_ref[hk, :, cols] = jnp.where(ok, _bias_from_scalars(_bucket(dist), tab), NEG_INF)

    lane = lax.broadcasted_iota(jnp.int32, (Q_BLOCK, LANES), 1)
    qblk = q_ref[0].astype(F32)
    init = (jnp.full((1, gq), NEG_INF, F32), jnp.zeros((1, gq), F32), jnp.zeros((LANES, gq), F32))
    p0 = pl.multiple_of(s0, LANES)
    n_cmp_pad = kcmp_ref.shape[1] - LANES
    c8 = qb * (Q_BLOCK // CMP_STRIDE)
    near0 = pl.multiple_of(c8, 8)
    kc_far = kcmp_ref[0, CMP_FRONT:CMP_FRONT + n_cmp_pad, :].astype(BF16)
    kc_near = kcmp_ref[0, pl.ds(near0, LANES), :].astype(BF16)
    vct_near = vcmp_ref[0, pl.ds(near0, LANES), :].T.astype(BF16)
    sm_near_t = selmap_ref[pl.ds(near0, LANES), :].T.astype(BF16)
    kw = kw_ref[0, pl.ds(p0, WIN_KEYS), :]
    vwt = vwt_ref[0, :, pl.ds(p0, WIN_KEYS)]
    crow = lax.broadcasted_iota(jnp.int32, (n_cmp_pad, gq), 0)
    nrow = lax.broadcasted_iota(jnp.int32, (LANES, gq), 0)
    wrow = lax.broadcasted_iota(jnp.int32, (WIN_KEYS, gq), 0)
    krow = lax.broadcasted_iota(jnp.int32, (SLC_NEAR, gq), 0)
    n_sel = selmapt_ref.shape[0]
    blk = lax.broadcasted_iota(jnp.int32, (n_sel, Q_BLOCK), 0)
    cur = (s0 + lax.broadcasted_iota(jnp.int32, (n_sel, Q_BLOCK), 1)) // SEL_BLOCK
    forced = (blk == 0) | (blk == cur) | (blk == cur - 1)
    jcol = lax.broadcasted_iota(jnp.int32, (Q_BLOCK, n_sel), 1)
    far_blocks = (s0 - Q_BLOCK) // SEL_BLOCK

    def split_dot(a, p):
        hi = p.astype(BF16)
        return _dot(a, hi) + _dot(a, (p - hi.astype(F32)).astype(BF16))

    def before_far(hk):
        lmask = (lane // HEAD_DIM) == hk
        qp = jnp.concatenate(
            [jnp.where(lmask, qblk[:, g * LANES:(g + 1) * LANES], 0.0) for g in range(GROUP)], axis=0)
        qp = (qp * (HEAD_DIM ** -0.5)).astype(BF16)
        far_bias = jnp.concatenate(
            [jnp.full((1, Q_BLOCK), rb_ref[N_BUCKETS - 1, hk * GROUP + g], F32) for g in range(GROUP)], axis=1)

        s_far = jnp.where(crow < c8 - CMP_FRONT, _dot_t(kc_far, qp), NEG_INF)
        s_near = jnp.where(nrow >= CMP_FRONT - c8, _dot_t(kc_near, qp) + nbc_ref[hk], NEG_INF)
        m = jnp.maximum(jnp.max(s_far, axis=0, keepdims=True) + far_bias, jnp.max(s_near, axis=0, keepdims=True))
        p_far = jnp.exp(s_far - (m - far_bias))
        p_near = jnp.exp(s_near - m)
        l = jnp.sum(p_far, axis=0, keepdims=True) + jnp.sum(p_near, axis=0, keepdims=True)
        inv = jnp.where(m > 0.5 * NEG_INF, 1.0 / l, 0.0)
        pc_far = p_far * inv
        pc_near = p_near * inv
        o_cmp = _dot(vcmpt_ref[0], pc_far.astype(BF16)) + _dot(vct_near, pc_near.astype(BF16))

        s = jnp.where(wrow >= WINDOW - s0, _dot_t(kw, qp) + nbw_ref[hk], NEG_INF)
        o_win = _col_softmax_finish(_col_softmax_update(init, s, vwt))

        ps_far = sum(pc_far[:, g * Q_BLOCK:(g + 1) * Q_BLOCK] for g in range(GROUP))
        ps_near = sum(pc_near[:, g * Q_BLOCK:(g + 1) * Q_BLOCK] for g in range(GROUP))
        imp_t = split_dot(selmapt_ref[...], ps_far) + split_dot(sm_near_t, ps_near)
        val = jnp.where(blk > cur, NEG_INF, imp_t + FORCE_BONUS * forced.astype(F32))
        sel_t = jnp.zeros((n_sel, Q_BLOCK), F32)
        for _ in range(N_SELECT):
            mx = jnp.max(val, axis=0, keepdims=True)
            first = jnp.min(jnp.where(val == mx, blk, n_sel), axis=0, keepdims=True)
            hit = blk == first
            sel_t = jnp.where(hit, 1.0, sel_t)
            val = jnp.where(hit, -3e38, val)
        sel = sel_t.T > 0.5

        pen_far = jnp.where(sel & (jcol < far_blocks), 0.0, NEG_INF).astype(BF16)
        pen_near = jnp.where(sel, 0.0, NEG_INF).astype(BF16)
        q_far = jnp.concatenate([qp, _tile_rows(pen_far, GROUP)], axis=1)
        q_near = jnp.concatenate([qp, _tile_rows(pen_near, GROUP)], axis=1)
        return dict(far_bias=far_bias, o_cmp=o_cmp, o_win=o_win, q_far=q_far, q_near=q_near)

    pre = [before_far(hk) for hk in heads]
    n_tiles = (ks_ref.shape[1] - Q_BLOCK) // SLC_TILE

    def far_logits(kt):
        f0 = pl.multiple_of(Q_BLOCK + kt * SLC_TILE, LANES)
        k = jnp.concatenate([ks_ref[0, pl.ds(f0, SLC_TILE), :], et_ref[pl.ds(f0, SLC_TILE), :]], axis=1)
        return [_dot_t(k, pre[hk]["q_far"]) for hk in heads]

    def far_pair(i, states):
        for slot in range(2):
            kt = 2 * i + slot
            put_logits(1 - slot, jnp.minimum(kt + 1, n_tiles - 1))
            f0 = pl.multiple_of(Q_BLOCK + jnp.minimum(kt, n_tiles - 1) * SLC_TILE, LANES)
            vt = vst_ref[0, :, pl.ds(f0, SLC_TILE)]
            states = [_col_softmax_update(states[hk], s_ref[slot, hk], vt, smax=smax_ref[slot, hk, 0:1, :])
                      for hk in heads]
        return states

    def put_logits(slot, kt):
        for hk, s_new in enumerate(far_logits(kt)):
            s_ref[slot, hk] = s_new
            smax_ref[slot, hk] = jnp.broadcast_to(jnp.max(s_new, axis=0, keepdims=True), (8, gq))

    n_far = (qb + 2) // (SLC_TILE // Q_BLOCK)
    put_logits(0, 0)
    far_states = lax.fori_loop(0, (n_far + 1) // 2, far_pair, [init] * N_KV_HEADS)

    k_near = jnp.concatenate([ks_ref[0, pl.ds(p0, SLC_NEAR), :], et_ref[pl.ds(p0, SLC_NEAR), :]], axis=1)
    vt_near = vst_ref[0, :, pl.ds(p0, SLC_NEAR)]
    gst = jax.nn.sigmoid(gn_ref[0]).T
    outs = []
    for hk in heads:
        m_far, l_far, acc_far = far_states[hk]
        state = (m_far + pre[hk]["far_bias"], l_far, acc_far)
        s = jnp.where(krow >= Q_BLOCK - s0, _dot_t(k_near, pre[hk]["q_near"]) + nbs_ref[hk], NEG_INF)
        o_slc = _col_softmax_finish(_col_softmax_update(state, s, vt_near))
        per_g = []
        for g in range(GROUP):
            cols = slice(g * Q_BLOCK, (g + 1) * Q_BLOCK)
            c = 3 * (hk * GROUP + g)
            o = (gst[c:c + 1, :] * pre[hk]["o_cmp"][:, cols] + gst[c + 1:c + 2, :] * o_slc[:, cols]
                 + gst[c + 2:c + 3, :] * pre[hk]["o_win"][:, cols])
            per_g.append(o.T)
        outs.append(per_g)
    lane_h0 = lane < HEAD_DIM
    for g in range(GROUP):
        out_ref[0, :, g * LANES:(g + 1) * LANES] = jnp.where(lane_h0, outs[0][g], outs[1][g]).astype(out_ref.dtype)


def _nsa_prompt_pair(q, gn, kcmp, vcmp, kvb, vs_heads, vw_heads, rel_bias):
    b, t, _ = q.shape
    assert t % SLC_TILE == 0
    nqb = t // Q_BLOCK
    n_chunk = t // CMP_STRIDE
    n_cmp = n_chunk - CMP_RATIO + 1
    n_blocks = t // SEL_BLOCK
    assert n_blocks <= LANES
    cpad = ((0, 0), (CMP_FRONT, LANES - CMP_FRONT), (0, 0))
    kcmp_p = jnp.pad(kcmp, cpad)
    vcmp_p = jnp.pad(vcmp, cpad)
    vcmp_t = vcmp.transpose(0, 2, 1).astype(BF16)
    sm = _sel_map(n_chunk, n_cmp, LANES, n_blocks)
    selmap = jnp.asarray(np.pad(sm, ((CMP_FRONT, LANES - CMP_FRONT), (0, 0))), F32)
    selmap_t = jnp.asarray(sm.T, BF16)
    front = lambda z, n: jnp.pad(z, ((0, 0), (n, 0), (0, 0)))
    keys_front = lambda z, n: jnp.pad(z.reshape(b, KV_WIDTH, t).astype(BF16), ((0, 0), (0, 0), (n, 0)))
    ks_p = front(kvb[:, :, 2 * KV_WIDTH:3 * KV_WIDTH], Q_BLOCK)
    vs_t = keys_front(vs_heads, Q_BLOCK)
    kw_p = front(kvb[:, :, 4 * KV_WIDTH:5 * KV_WIDTH], WINDOW)
    vw_t = keys_front(vw_heads, WINDOW)
    pidx = np.arange(t + Q_BLOCK)
    e = (pidx[:, None] // SEL_BLOCK - Q_BLOCK // SEL_BLOCK == np.arange(LANES)[None, :])
    e = jnp.asarray(e.astype(np.float32), BF16)
    full = lambda arr: pl.BlockSpec((1,) + arr.shape[1:], lambda bi, i: (bi, 0, 0))
    const = lambda arr: pl.BlockSpec(arr.shape, lambda bi, i: (0, 0))
    return pl.pallas_call(
        _nsa_prompt_pair_kernel,
        grid=(b, nqb),
        in_specs=[pl.BlockSpec(memory_space=pltpu.SMEM),
                  pl.BlockSpec((1, Q_BLOCK, NSA_WIDTH), lambda bi, i: (bi, i, 0)),
                  pl.BlockSpec((1, Q_BLOCK, LANES), lambda bi, i: (bi, i, 0)),
                  full(kcmp_p), full(vcmp_p), full(vcmp_t), const(selmap), const(selmap_t),
                  full(ks_p), full(vs_t), full(kw_p), full(vw_t), const(e)],
        out_specs=pl.BlockSpec((1, Q_BLOCK, NSA_WIDTH), lambda bi, i: (bi, i, 0)),
        out_shape=jax.ShapeDtypeStruct((b, t, NSA_WIDTH), BF16),
        scratch_shapes=[pltpu.VMEM((N_KV_HEADS, LANES, GROUP * Q_BLOCK), F32),
                        pltpu.VMEM((N_KV_HEADS, SLC_NEAR, GROUP * Q_BLOCK), F32),
                        pltpu.VMEM((N_KV_HEADS, WIN_KEYS, GROUP * Q_BLOCK), F32),
                        pltpu.VMEM((2, N_KV_HEADS, SLC_TILE, GROUP * Q_BLOCK), F32),
                        pltpu.VMEM((2, N_KV_HEADS, 8, GROUP * Q_BLOCK), F32)],
        compiler_params=_cparams(("arbitrary", "arbitrary")),
        name="nsa_prompt",
    )(rel_bias, q, gn, kcmp_p, vcmp_p, vcmp_t, selmap, selmap_t, ks_p, vs_t, kw_p, vw_t, e)


ROWS_PER_TOKEN = 8


def _nsa_sample_cmp_kernel(q_ref, kcmp_ref, vcmp_ref, selmap_ref, rowtab_ref, ocmp_ref, idx_ref, *,
                           past_len, n_tok, n_cmp, n_blocks):
    rows = n_tok * ROWS_PER_TOKEN
    n_cpad = kcmp_ref.shape[1]
    ridx = lax.broadcasted_iota(jnp.int32, (rows, n_cpad), 0)
    ccol = lax.broadcasted_iota(jnp.int32, (rows, n_cpad), 1)
    dist = past_len + ridx // ROWS_PER_TOKEN - (CMP_STRIDE * ccol + CMP_LEN - 1)
    bucket = _bucket(dist)
    valid = (dist >= 0) & (ccol < n_cmp) & (ridx % ROWS_PER_TOKEN < GROUP)
    n_bpad = selmap_ref.shape[1]
    blk = lax.broadcasted_iota(jnp.int32, (ROWS_PER_TOKEN, n_bpad), 1)
    cur = (past_len + lax.broadcasted_iota(jnp.int32, (ROWS_PER_TOKEN, n_bpad), 0)) // SEL_BLOCK
    forced = (blk == 0) | (blk == cur) | (blk == cur - 1)
    olane = lax.broadcasted_iota(jnp.int32, (ROWS_PER_TOKEN, LANES), 1)
    biases = [_bias_from_rows(bucket, rowtab_ref[hk]) for hk in range(N_KV_HEADS)]
    for bi, hk in [(bi, hk) for bi in range(kcmp_ref.shape[0]) for hk in range(N_KV_HEADS)]:
        kc = kcmp_ref[bi].astype(BF16)
        vc = vcmp_ref[bi].astype(BF16)
        qp = (q_ref[bi, hk].astype(F32) * (HEAD_DIM ** -0.5)).astype(BF16)
        s = _dot_t(qp, kc) + biases[hk]
        s = jnp.where(valid, s, NEG_INF)
        m = jnp.max(s, axis=-1, keepdims=True)
        p = jnp.where(valid, jnp.exp(s - m), 0.0)
        l = jnp.sum(p, axis=-1, keepdims=True)
        pc = p * jnp.where(l > 0.0, 1.0 / jnp.where(l > 0.0, l, 1.0), 0.0)
        ocmp_ref[bi, hk] = _dot(pc.astype(BF16), vc)[:, hk * HEAD_DIM:(hk + 1) * HEAD_DIM]
        ps = jnp.concatenate(
            [jnp.sum(pc[i * ROWS_PER_TOKEN:(i + 1) * ROWS_PER_TOKEN], axis=0, keepdims=True) for i in range(n_tok)]
            + [jnp.zeros((ROWS_PER_TOKEN - n_tok, n_cpad), F32)], axis=0)
        imp = _split_dot(ps, selmap_ref[...])
        val = jnp.where((blk > cur) | (blk >= n_blocks), NEG_INF, imp + FORCE_BONUS * forced.astype(F32))
        picks = jnp.zeros((ROWS_PER_TOKEN, LANES), jnp.int32)
        for n in range(N_SELECT):
            mx = jnp.max(val, axis=-1, keepdims=True)
            first = jnp.min(jnp.where(val == mx, blk, n_bpad), axis=-1, keepdims=True)
            picks = jnp.where(olane == n, first, picks)
            val = jnp.where(blk == first, -3e38, val)
        idx_ref[bi, hk] = picks


def _nsa_sample_cmp(qs, kcmp, vcmp, rowtab, past_len, n_tok, n_cmp, n_blocks):
    nb = qs.shape[0]
    bps = next(d for d in (4, 2, 1) if nb % d == 0)
    n_cpad = kcmp.shape[1]
    n_bpad = -(-n_blocks // LANES) * LANES
    selmap = jnp.asarray(_sel_map(n_cpad, n_cmp, n_bpad, n_blocks), BF16)
    rows = n_tok * ROWS_PER_TOKEN
    return pl.pallas_call(
        functools.partial(_nsa_sample_cmp_kernel, past_len=past_len, n_tok=n_tok, n_cmp=n_cmp, n_blocks=n_blocks),
        grid=(nb // bps,),
        in_specs=[pl.BlockSpec((bps, N_KV_HEADS, rows, LANES), lambda b: (b, 0, 0, 0)),
                  pl.BlockSpec((bps, n_cpad, LANES), lambda b: (b, 0, 0)),
                  pl.BlockSpec((bps, n_cpad, LANES), lambda b: (b, 0, 0)),
                  pl.BlockSpec(selmap.shape, lambda b: (0, 0)),
                  pl.BlockSpec(rowtab.shape, lambda b: (0, 0, 0))],
        out_specs=[pl.BlockSpec((bps, N_KV_HEADS, rows, HEAD_DIM), lambda b: (b, 0, 0, 0)),
                   pl.BlockSpec((bps, N_KV_HEADS, ROWS_PER_TOKEN, LANES), lambda b: (b, 0, 0, 0))],
        out_shape=[jax.ShapeDtypeStruct((nb, N_KV_HEADS, rows, HEAD_DIM), F32),
                   jax.ShapeDtypeStruct((nb, N_KV_HEADS, ROWS_PER_TOKEN, LANES), jnp.int32)],
        compiler_params=_cparams(("arbitrary",)),
        name="nsa_sample_cmp",
    )(qs, kcmp, vcmp, selmap, rowtab)


def _nsa_sample_slc_kernel(idx_ref, pt_ref, ck_ref, cv_ref, q_ref, ocmp_ref, gn_ref, rowtab_ref, newk_ref, newv_ref,
                           swk_ref, swv_ref, nwk_ref, nwv_ref, out_ref, kbuf_ref, vbuf_ref, sem_ref, *,
                           past_len, n_tok, n_cache_blocks, npages):
    b = pl.program_id(0)
    hk = pl.program_id(1)
    step = b * N_KV_HEADS + hk
    last = pl.num_programs(0) * N_KV_HEADS - 1
    slot = step % 2
    per_step = n_tok * N_SELECT
    pages_per_block = PAGE // SEL_BLOCK

    def block_copies(st, i, sl):
        j = jnp.minimum(idx_ref[st * per_step + i], n_cache_blocks - 1)
        page = pt_ref[(st // N_KV_HEADS) * npages + j // pages_per_block]
        head = st % N_KV_HEADS
        return (pltpu.make_async_copy(ck_ref.at[page, head], kbuf_ref.at[sl, i], sem_ref.at[sl]),
                pltpu.make_async_copy(cv_ref.at[page, head], vbuf_ref.at[sl, i], sem_ref.at[sl]))

    @pl.when(step == 0)
    def _():
        for i in range(per_step):
            for c in block_copies(0, i, 0):
                c.start()

    nxt = jnp.minimum(step + 1, last)
    for i in range(per_step):
        for c in block_copies(nxt, i, 1 - slot):
            c.start()
    for i in range(per_step):
        for c in block_copies(step, i, slot):
            c.wait()

    rowtab = rowtab_ref[0]
    rvalid = lax.broadcasted_iota(jnp.int32, (ROWS_PER_TOKEN, 1), 0) < GROUP
    init = (jnp.full((ROWS_PER_TOKEN, 1), NEG_INF, F32), jnp.zeros((ROWS_PER_TOKEN, 1), F32),
            jnp.zeros((ROWS_PER_TOKEN, HEAD_DIM), F32))
    n_win = swk_ref.shape[3]
    kw = jnp.concatenate([swk_ref[0, 0], nwk_ref[0, 0]], axis=1).astype(BF16)
    vw = jnp.concatenate([swv_ref[0, 0], nwv_ref[0, 0]], axis=1).astype(BF16)
    wl = n_win + nwk_ref.shape[3]
    wpos = past_len - n_win + lax.broadcasted_iota(jnp.int32, (ROWS_PER_TOKEN, wl), 1)
    newk = newk_ref[0, 0]
    newv = newv_ref[0, 0]

    for tok in range(n_tok):
        t = past_len + tok
        qp = (q_ref[0, 0, tok].astype(F32) * (HEAD_DIM ** -0.5)).astype(BF16)
        page_row = lax.broadcasted_iota(jnp.int32, (ROWS_PER_TOKEN, PAGE), 1)
        bias_far = jnp.broadcast_to(rowtab[:, N_BUCKETS - 1:N_BUCKETS], (ROWS_PER_TOKEN, PAGE))
        bias_last = _bias_from_rows(_bucket(t - ((npages - 1) * PAGE + page_row)), rowtab)
        bias_new = _bias_from_rows(_bucket(t - (npages * PAGE + page_row)), rowtab)
        assert BUCKET_THR[-1] <= PAGE + 1
        kts, vts, biases = [], [], []
        for n in range(N_SELECT):
            i = tok * N_SELECT + n
            blk = idx_ref[step * per_step + i]
            is_new = blk >= n_cache_blocks
            page = blk // pages_per_block
            kts.append(jnp.where(is_new, newk, kbuf_ref[slot, i]).astype(BF16))
            vts.append(jnp.where(is_new, newv, vbuf_ref[slot, i]).astype(BF16))
            bias = jnp.where(is_new, bias_new, jnp.where(page == npages - 1, bias_last, bias_far))
            ok = (page_row // SEL_BLOCK == blk % pages_per_block) & (page * PAGE + page_row <= t)
            biases.append(jnp.where(ok, bias, NEG_INF))
        s = _dot(qp, jnp.concatenate(kts, axis=1)) + jnp.concatenate(biases, axis=1)
        mask = (s > 0.5 * NEG_INF) & rvalid
        o_slc = _softmax_finish(_softmax_update(init, jnp.where(mask, s, NEG_INF), jnp.concatenate(vts, axis=1),
                                                mask=mask, v_transposed=True))
        wdist = t - wpos
        s = _dot(qp, kw) + _bias_from_rows(_bucket(wdist), rowtab)
        mask = (wdist >= 0) & (wdist < WINDOW) & (wpos >= 0) & rvalid
        o_win = _softmax_finish(_softmax_update(init, jnp.where(mask, s, NEG_INF), vw, mask=mask, v_transposed=True))
        gs = jax.nn.sigmoid(gn_ref[0, 0, tok])
        out_ref[0, 0, tok] = gs[:, 0:1] * ocmp_ref[0, 0, tok] + gs[:, 1:2] * o_slc + gs[:, 2:3] * o_win

    @pl.when(step == last)
    def _():
        for i in range(per_step):
            for c in block_copies(nxt, i, 1 - slot):
                c.wait()


def _nsa_sample_slc(idx, page_table, cache_k, cache_v, q3, ocmp, gn3, rowtab, newk, newv, swk, swv, nwk, nwv,
                    past_len, n_tok):
    nb, npages = page_table.shape
    halves = PAGE // SEL_BLOCK
    n_cache_blocks = npages * halves
    ck = cache_k.transpose(0, 2, 3, 1)
    cv = cache_v.transpose(0, 2, 3, 1)

    tile = lambda w: pl.BlockSpec((1, 1, n_tok, ROWS_PER_TOKEN, w), lambda b, h, i, p: (b, h, 0, 0, 0))
    per_bh = lambda arr: pl.BlockSpec((1, 1) + arr.shape[2:], lambda b, h, i, p: (b, h, 0, 0))
    grid_spec = pltpu.PrefetchScalarGridSpec(
        num_scalar_prefetch=2,
        grid=(nb, N_KV_HEADS),
        in_specs=[pl.BlockSpec(memory_space=pl.ANY), pl.BlockSpec(memory_space=pl.ANY),
                  tile(HEAD_DIM), tile(HEAD_DIM), tile(LANES),
                  pl.BlockSpec((1, ROWS_PER_TOKEN, N_BUCKETS), lambda b, h, i, p: (h, 0, 0)),
                  per_bh(newk), per_bh(newv), per_bh(swk), per_bh(swv), per_bh(nwk), per_bh(nwv)],
        out_specs=tile(HEAD_DIM),
        scratch_shapes=[pltpu.VMEM((2, n_tok * N_SELECT, HEAD_DIM, PAGE), F32),
                        pltpu.VMEM((2, n_tok * N_SELECT, HEAD_DIM, PAGE), F32),
                        pltpu.SemaphoreType.DMA((2,))],
    )
    return pl.pallas_call(
        functools.partial(_nsa_sample_slc_kernel, past_len=past_len, n_tok=n_tok, n_cache_blocks=n_cache_blocks,
                          npages=npages),
        grid_spec=grid_spec,
        out_shape=jax.ShapeDtypeStruct((nb, N_KV_HEADS, n_tok, ROWS_PER_TOKEN, HEAD_DIM), F32),
        compiler_params=_cparams(("arbitrary", "arbitrary")),
        name="nsa_sample_slc",
    )(idx.reshape(-1), page_table.reshape(-1), ck, cv, q3, ocmp, gn3, rowtab, newk, newv, swk, swv, nwk, nwv)


def _merge_tail(x, a, c, gates, wn_ref, wc_ref, wo_ref):
    ga = gates[:, :D_MODEL]
    gb = gates[:, D_MODEL:]
    m = (jax.nn.sigmoid(ga) * _dot(a, wn_ref[...])
         + jax.nn.sigmoid(gb) * _dot(c.astype(BF16), wc_ref[...]))
    return x + _dot(m.astype(BF16), wo_ref[...])


def _merge_prompt_kernel(x_ref, a_ref, conv_ref, halo_ref, gate_ref, cw_ref, cb_ref, wn_ref, wc_ref, wo_ref,
                         out_ref, cst_ref, *, tiles_per_batch):
    i = pl.program_id(0)
    first = i % tiles_per_batch == 0
    conv = conv_ref[...]
    bg = conv[:, :CONV_DIM]
    u = conv[:, CONV_DIM:2 * CONV_DIM] * conv[:, 2 * CONV_DIM:]
    halo = halo_ref[...]
    uh = jnp.where(first, 0.0, halo[:, CONV_DIM:2 * CONV_DIM] * halo[:, 2 * CONV_DIM:])
    tm = u.shape[0]
    row = lax.broadcasted_iota(jnp.int32, (tm, CONV_DIM), 0)
    u1 = jnp.where(row == 0, uh[7:8], pltpu.roll(u, 1, axis=0))
    u2 = pltpu.roll(u, 2, axis=0)
    u2 = jnp.where(row == 0, uh[6:7], jnp.where(row == 1, uh[7:8], u2))
    cw = cw_ref[...]
    y = ((cb_ref[...] + cw[0:1] * u2) + cw[1:2] * u1) + cw[2:3] * u
    c = bg * y
    out_ref[...] = _merge_tail(x_ref[...], a_ref[...], c, gate_ref[...], wn_ref, wc_ref, wo_ref)
    cst_ref[0] = u[tm - 8:, :]


def _merge_prompt(x2d, a2, conv3, gates, conv_w, conv_b, wn, wc, wo, t):
    n = x2d.shape[0]
    tm = min(512, t)
    assert t % tm == 0
    tpb = t // tm
    row = lambda w: pl.BlockSpec((tm, w), lambda i: (i, 0))
    const = lambda arr: pl.BlockSpec(arr.shape, lambda i: (0,) * arr.ndim)
    return pl.pallas_call(
        functools.partial(_merge_prompt_kernel, tiles_per_batch=tpb),
        grid=(n // tm,),
        in_specs=[row(D_MODEL),
                  row(NSA_WIDTH),
                  row(3 * CONV_DIM),
                  pl.BlockSpec((8, 3 * CONV_DIM), lambda i: (jnp.maximum(i * (tm // 8) - 1, 0), 0)),
                  row(2 * D_MODEL),
                  const(conv_w), const(conv_b), const(wn), const(wc), const(wo)],
        out_specs=[row(D_MODEL), pl.BlockSpec((1, 8, CONV_DIM), lambda i: (i // tpb, 0, 0))],
        out_shape=[jax.ShapeDtypeStruct((n, D_MODEL), F32),
                   jax.ShapeDtypeStruct((n // t, 8, CONV_DIM), F32)],
        compiler_params=_cparams(("arbitrary",)),
        name="merge_prompt",
    )(x2d, a2, conv3, conv3, gates, conv_w, conv_b, wn, wc, wo)


def _merge_sample_kernel(x_ref, a_ref, conv_ref, past_ref, gate_ref, cw_ref, cb_ref, wn_ref, wc_ref, wo_ref,
                         out_ref, cst_ref, *, n_tok, nb):
    conv = conv_ref[...]
    bg = conv[:, :CONV_DIM]
    u = conv[:, CONV_DIM:2 * CONV_DIM] * conv[:, 2 * CONV_DIM:]
    up = jnp.concatenate([past_ref[...], u], axis=0)
    cw = cw_ref[...]
    y = cb_ref[...]
    for k in range(CONV_WIDTH):
        y = y + cw[k:k + 1] * up[k * nb:(k + n_tok) * nb]
    c = bg * y
    out_ref[...] = _merge_tail(x_ref[...], a_ref[...].astype(BF16), c, gate_ref[...], wn_ref, wc_ref, wo_ref)
    cst_ref[...] = up[n_tok * nb:]


def _merge_sample(x_tm, a_tm, conv_tm, past_tm, gates_tm, conv_w, conv_b, wn, wc, wo, n_tok, nb):
    n = x_tm.shape[0]
    full = lambda arr: pl.BlockSpec(arr.shape, lambda i: (0,) * arr.ndim)
    args = (x_tm, a_tm, conv_tm, past_tm, gates_tm, conv_w, conv_b, wn, wc, wo)
    return pl.pallas_call(
        functools.partial(_merge_sample_kernel, n_tok=n_tok, nb=nb),
        grid=(1,),
        in_specs=[full(a) for a in args],
        out_specs=[pl.BlockSpec((n, D_MODEL), lambda i: (0, 0)),
                   pl.BlockSpec(((CONV_WIDTH - 1) * nb, CONV_DIM), lambda i: (0, 0))],
        out_shape=[jax.ShapeDtypeStruct((n, D_MODEL), F32),
                   jax.ShapeDtypeStruct(((CONV_WIDTH - 1) * nb, CONV_DIM), F32)],
        compiler_params=_cparams(("arbitrary",)),
        name="merge_sample",
    )(*args)


def _rms(x, g):
    return (x * lax.rsqrt(jnp.mean(x * x, axis=-1, keepdims=True) + RMS_EPS)) * g


def _mlp_kernel(x_ref, g_ref, wu_ref, wd_ref, gf_ref, out_ref, h_ref, acc_ref):
    j = pl.program_id(1)

    @pl.when(j == 0)
    def _():
        h_ref[...] = _rms(x_ref[...], g_ref[...]).astype(BF16)
        acc_ref[...] = jnp.zeros(acc_ref.shape, F32)

    half = h_ref.shape[0] // 2
    for r in range(2):
        rows = slice(r * half, (r + 1) * half)
        up = jnp.square(jnp.maximum(_dot(h_ref[rows, :], wu_ref[...]), 0.0)).astype(BF16)
        acc_ref[rows, :] += _dot(up, wd_ref[...])

    @pl.when(j == pl.num_programs(1) - 1)
    def _():
        out_ref[...] = _rms(x_ref[...] + acc_ref[...], gf_ref[...])


def _mlp(x2d, g_mlp, w_up, w_down, g_final):
    n = x2d.shape[0]
    tm = min(1024, n)
    tf = 1024
    assert n % tm == 0 and D_FF % tf == 0
    return pl.pallas_call(
        _mlp_kernel,
        grid=(n // tm, D_FF // tf),
        in_specs=[pl.BlockSpec((tm, D_MODEL), lambda i, j: (i, 0)),
                  pl.BlockSpec((1, D_MODEL), lambda i, j: (0, 0)),
                  pl.BlockSpec((D_MODEL, tf), lambda i, j: (0, j)),
                  pl.BlockSpec((tf, D_MODEL), lambda i, j: (j, 0)),
                  pl.BlockSpec((1, D_MODEL), lambda i, j: (0, 0))],
        out_specs=pl.BlockSpec((tm, D_MODEL), lambda i, j: (i, 0)),
        out_shape=jax.ShapeDtypeStruct((n, D_MODEL), F32),
        scratch_shapes=[pltpu.VMEM((tm, D_MODEL), BF16), pltpu.VMEM((tm, D_MODEL), F32)],
        compiler_params=_cparams(("arbitrary", "arbitrary")),
        name="mlp",
    )(x2d, g_mlp, w_up, w_down, g_final)


def _kv_rows(z, lead):
    return z.reshape(lead + (N_KV_HEADS, HEAD_DIM))


def _layer(xp, xs, caches, states, page_table, w, rel_bias):
    (cache_cmp_k, cache_cmp_v, cache_slc_k, cache_slc_v) = caches
    (state_win_k, state_win_v, state_conv) = states
    nbp, t, _ = xp.shape
    nbs, n_tok, _ = xs.shape
    npages = page_table.shape[1]
    past_len = npages * PAGE
    assert t % PAGE == 0 and n_tok <= GROUP and state_win_k.shape[1] >= n_tok

    w_all = _prep_w_in(w["w_in"])
    g_attn = w["g_attn"].reshape(1, D_MODEL)
    wn = w["w_nsa_out"].reshape(N_KV_HEADS, GROUP, HEAD_DIM, D_MODEL).transpose(1, 0, 2, 3)
    wn = wn.reshape(NSA_WIDTH, D_MODEL).astype(BF16)
    wc = w["w_conv_out"].astype(BF16)
    wo = w["w_o"].astype(BF16)
    wu = w["w_up"].astype(BF16)
    wd = w["w_down"].astype(BF16)
    g_mlp = w["g_mlp"].reshape(1, D_MODEL)
    g_final = w["g_final"].reshape(1, D_MODEL)
    conv_w = w["conv_w"]
    conv_b = w["conv_b"].reshape(1, CONV_DIM)
    cmp_k = (w["cmp_pe_k"], w["cmp_w1_k"], w["cmp_w2_k"])
    cmp_v = (w["cmp_pe_v"], w["cmp_w1_v"], w["cmp_w2_v"])

    n_p = nbp * t
    (q, kc, vc, *kv_heads, kvb, gn, conv3, gates) = _proj(xp.reshape(n_p, D_MODEL), g_attn, w_all, True, t)
    ident = jnp.arange(nbp * (t // PAGE), dtype=jnp.int32).reshape(nbp, t // PAGE)
    kcmp = _compress(kc.reshape(n_p // PAGE, PAGE, KV_WIDTH), 0, ident, *cmp_k)
    vcmp = _compress(vc.reshape(n_p // PAGE, PAGE, KV_WIDTH), 0, ident, *cmp_v)
    a2 = _nsa_prompt_pair(q.reshape(nbp, t, NSA_WIDTH), gn.reshape(nbp, t, LANES), kcmp, vcmp,
                          kvb.reshape(nbp, t, 6 * KV_WIDTH), kv_heads[3], kv_heads[5], rel_bias)
    x1, cst = _merge_prompt(xp.reshape(n_p, D_MODEL), a2.reshape(n_p, NSA_WIDTH),
                            conv3, gates, conv_w, conv_b, wn, wc, wo, t)
    yp = _mlp(x1, g_mlp, wu, wd, g_final).reshape(nbp, t, D_MODEL)
    win_p = min(WINDOW, t)
    back = lambda z: z.transpose(0, 3, 1, 2)
    new_p = tuple(back(z) for z in kv_heads[:4]) + (
        back(kv_heads[4][..., t - win_p:]), back(kv_heads[5][..., t - win_p:]), cst[:, 8 - (CONV_WIDTH - 1):, :])

    n_s = nbs * n_tok
    (qs, kcs, vcs, kss, vss, kws, vws, _, gns, conv3s, gatess) = _proj(xs.reshape(n_s, D_MODEL), g_attn, w_all, False)
    n_rows = past_len + n_tok
    n_chunk = n_rows // CMP_STRIDE
    assert n_chunk == past_len // CMP_STRIDE, "new rows must not complete a compression chunk"
    n_cmp = n_chunk - CMP_RATIO + 1
    n_blocks = -(-n_rows // SEL_BLOCK)
    kcmp_s, vcmp_s = _compress_t((cache_cmp_k, cache_cmp_v), page_table, (cmp_k, cmp_v))

    lane = np.arange(LANES)
    q5 = qs.reshape(nbs, n_tok, GROUP, LANES)
    q3 = jnp.stack([jnp.where(jnp.asarray((lane // HEAD_DIM) == h), q5, jnp.zeros_like(q5))
                    for h in range(N_KV_HEADS)], axis=1)
    q3 = jnp.pad(q3, ((0, 0), (0, 0), (0, 0), (0, ROWS_PER_TOKEN - GROUP), (0, 0)))
    tab = rel_bias.T.reshape(N_KV_HEADS, GROUP, N_BUCKETS)
    rowtab3 = jnp.pad(tab, ((0, 0), (0, ROWS_PER_TOKEN - GROUP), (0, 0)))
    rowtab = jnp.tile(rowtab3, (1, n_tok, 1))
    ocmp, idx = _nsa_sample_cmp(q3.reshape(nbs, N_KV_HEADS, n_tok * ROWS_PER_TOKEN, LANES), kcmp_s, vcmp_s,
                                rowtab, past_len, n_tok, n_cmp, n_blocks)
    idx = idx[:, :, :n_tok, :N_SELECT]
    gn3 = gns[:, :3 * N_HEADS].reshape(nbs, n_tok, N_KV_HEADS, GROUP, 3).transpose(0, 2, 1, 3, 4)
    gn3 = jnp.pad(gn3, ((0, 0), (0, 0), (0, 0), (0, ROWS_PER_TOKEN - GROUP), (0, LANES - 3)))
    q3s = jnp.pad(qs.reshape(nbs, n_tok, GROUP, N_KV_HEADS, HEAD_DIM).transpose(0, 3, 1, 2, 4),
                  ((0, 0), (0, 0), (0, 0), (0, ROWS_PER_TOKEN - GROUP), (0, 0)))
    ocmp5 = ocmp.reshape(nbs, N_KV_HEADS, n_tok, ROWS_PER_TOKEN, HEAD_DIM)
    new_t = lambda z: jnp.pad(z.reshape(nbs, n_tok, N_KV_HEADS, HEAD_DIM).transpose(0, 2, 3, 1),
                              ((0, 0), (0, 0), (0, 0), (0, LANES - n_tok)))
    a_s = _nsa_sample_slc(idx, page_table, cache_slc_k, cache_slc_v, q3s, ocmp5, gn3, rowtab3,
                          new_t(kss), new_t(vss),
                          state_win_k.transpose(0, 2, 3, 1), state_win_v.transpose(0, 2, 3, 1),
                          new_t(kws), new_t(vws), past_len, n_tok)
    a_tm = a_s[:, :, :, :GROUP, :].transpose(2, 0, 3, 1, 4).reshape(n_s, NSA_WIDTH)
    to_tm = lambda z: z.reshape(nbs, n_tok, -1).transpose(1, 0, 2).reshape(n_s, -1)
    past_tm = state_conv.transpose(1, 0, 2).reshape((CONV_WIDTH - 1) * nbs, CONV_DIM)
    x1s, csts = _merge_sample(to_tm(xs), a_tm, to_tm(conv3s), past_tm, to_tm(gatess),
                              conv_w, conv_b, wn, wc, wo, n_tok, nbs)
    ys = _mlp(x1s, g_mlp, wu, wd, g_final).reshape(n_tok, nbs, D_MODEL).transpose(1, 0, 2)
    win_buf = state_win_k.shape[1]
    kw_all = jnp.concatenate([state_win_k, _kv_rows(kws, (nbs, n_tok))], axis=1)[:, -win_buf:]
    vw_all = jnp.concatenate([state_win_v, _kv_rows(vws, (nbs, n_tok))], axis=1)[:, -win_buf:]
    new_s = tuple(_kv_rows(z, (nbs, n_tok)) for z in (kcs, vcs, kss, vss)) + (
        kw_all, vw_all, csts.reshape(CONV_WIDTH - 1, nbs, CONV_DIM).transpose(1, 0, 2))
    return yp, ys, new_p, new_s


def kernel(x_prompt, x_sample, cache_cmp_k, cache_cmp_v, cache_slc_k, cache_slc_v, state_win_k, state_win_v,
           state_conv, page_table, g_attn, w_in, cmp_pe_k, cmp_w1_k, cmp_w2_k, cmp_pe_v, cmp_w1_v, cmp_w2_v,
           conv_w, conv_b, w_nsa_out, w_conv_out, w_o, g_mlp, w_up, w_down, rel_bias, g_final):
    depth = w_in.shape[0]
    assert depth == 1, "the final norm is fused into the last layer's MLP kernel"
    names = ("g_attn", "w_in", "cmp_pe_k", "cmp_w1_k", "cmp_w2_k", "cmp_pe_v", "cmp_w1_v", "cmp_w2_v",
             "conv_w", "conv_b", "w_nsa_out", "w_conv_out", "w_o", "g_mlp", "w_up", "w_down")
    vals = (g_attn, w_in, cmp_pe_k, cmp_w1_k, cmp_w2_k, cmp_pe_v, cmp_w1_v, cmp_w2_v,
            conv_w, conv_b, w_nsa_out, w_conv_out, w_o, g_mlp, w_up, w_down)
    l = 0
    w = {k: v[l] for k, v in zip(names, vals)}
    w["g_final"] = g_final
    yp, ys, new_p, new_s = _layer(
        x_prompt, x_sample,
        (cache_cmp_k[l], cache_cmp_v[l], cache_slc_k[l], cache_slc_v[l]),
        (state_win_k[l], state_win_v[l], state_conv[l]), page_table, w, rel_bias)
    return (yp, ys) + tuple(z[None] for z in new_p) + tuple(z[None] for z in new_s)
```

```python
import functools
import math

import numpy as np
import jax
import jax.numpy as jnp
from jax import lax
from jax.experimental import pallas as pl
from jax.experimental.pallas import tpu as pltpu

F32 = jnp.float32
BF16 = jnp.bfloat16

D_MODEL = 1024
N_HEADS = 8
HEAD_DIM = 64
N_KV_HEADS = 2
GROUP = N_HEADS // N_KV_HEADS
KV_WIDTH = N_KV_HEADS * HEAD_DIM
NSA_WIDTH = N_HEADS * HEAD_DIM
CMP_LEN = 32
CMP_STRIDE = 16
CMP_RATIO = CMP_LEN // CMP_STRIDE
CMP_HIDDEN = 256
SEL_BLOCK = 64
N_SELECT = 16
WINDOW = 512
CONV_DIM = 512
CONV_WIDTH = 3
D_FF = 4 * D_MODEL
N_BUCKETS = 32
MAX_DISTANCE = 128
Q_BLOCK = 128
PAGE = 128
RMS_EPS = 1e-6
NEG_INF = -1e30
FORCE_BONUS = 1e4
LANES = 128
VMEM_LIMIT = 56 * 1024 * 1024


def _bucket_thresholds():
    max_exact = N_BUCKETS // 2
    n = np.arange(max_exact, 4 * MAX_DISTANCE).astype(np.float32)
    large = max_exact + (np.log(n / np.float32(max_exact)) / np.float32(math.log(MAX_DISTANCE / max_exact))
                         * np.float32(N_BUCKETS - max_exact)).astype(np.int32)
    large = np.minimum(large, N_BUCKETS - 1)
    thr = []
    for k in range(max_exact + 1, N_BUCKETS):
        thr.append(int(np.arange(max_exact, 4 * MAX_DISTANCE)[np.argmax(large >= k)]))
    return tuple(thr)


BUCKET_THR = _bucket_thresholds()


def _bucket(dist):
    n = jnp.maximum(dist, 0)
    big = jnp.full(n.shape, N_BUCKETS // 2, jnp.int32)
    for thr in BUCKET_THR:
        big = big + (n >= thr).astype(jnp.int32)
    return jnp.where(n < N_BUCKETS // 2, n, big)


def _bias_from_scalars(bucket, table_fn):
    out = jnp.zeros(bucket.shape, F32)
    for k in range(N_BUCKETS):
        out = jnp.where(bucket == k, table_fn(k), out)
    return out


def _bias_from_rows(bucket, rowtab):
    out = jnp.zeros(bucket.shape, F32)
    for k in range(N_BUCKETS):
        out = jnp.where(bucket == k, rowtab[:, k:k + 1], out)
    return out


def _dot(a, b):
    return jnp.dot(a, b, preferred_element_type=F32)


def _dot_t(a, b):
    return lax.dot_general(a, b, (((1,), (1,)), ((), ())), preferred_element_type=F32)


def _split_dot(p, m):
    hi = p.astype(BF16)
    lo = (p - hi.astype(F32)).astype(BF16)
    return _dot(hi, m) + _dot(lo, m)


def _cparams(sem):
    return pltpu.CompilerParams(dimension_semantics=sem, vmem_limit_bytes=VMEM_LIMIT)


PROJ_WIDTHS = (NSA_WIDTH, 6 * KV_WIDTH, LANES, 3 * CONV_DIM, 2 * D_MODEL)
PROJ_COLS = sum(PROJ_WIDTHS)


def _proj_kernel(x_ref, g_ref, w_ref, *out_refs, head_rows):
    q_ref, kvb_ref, gn_ref, conv_ref, gate_ref = out_refs[0], *out_refs[-4:]
    kv_refs = out_refs[1:-4]
    x = x_ref[...]
    y = x * lax.rsqrt(jnp.mean(x * x, axis=-1, keepdims=True) + RMS_EPS)
    hb = (y * g_ref[...]).astype(BF16)
    q_ref[...] = _dot(hb, w_ref[:, 0:NSA_WIDTH]).astype(q_ref.dtype)
    off = NSA_WIDTH
    kv = _dot(hb, w_ref[:, off:off + 6 * KV_WIDTH])
    tm = kv.shape[0]
    for i in range(6):
        z = kv[:, i * KV_WIDTH:(i + 1) * KV_WIDTH]
        if head_rows:
            kv_refs[2 + i][0] = z.T.reshape(N_KV_HEADS, HEAD_DIM, tm)
        if not head_rows or i < 2:
            kv_refs[i][...] = z
    kvb_ref[...] = kv.astype(BF16)
    off += 6 * KV_WIDTH
    gn_ref[...] = _dot(hb, w_ref[:, off:off + LANES])
    off += LANES
    conv_ref[...] = _dot(hb, w_ref[:, off:off + 3 * CONV_DIM])
    off += 3 * CONV_DIM
    gate_ref[...] = _dot(hb, w_ref[:, off:off + 2 * D_MODEL])


def _proj(x2d, g, w_all, head_rows, t=None):
    n = x2d.shape[0]
    tm = min(512, n if t is None else t)
    assert n % tm == 0
    row = lambda w: pl.BlockSpec((tm, w), lambda i: (i, 0))
    dense = lambda w, dt: (jax.ShapeDtypeStruct((n, w), dt), row(w))
    if head_rows:
        tpb = t // tm
        kv_outs = [dense(KV_WIDTH, F32)] * 2 + [
            (jax.ShapeDtypeStruct((n // t, N_KV_HEADS, HEAD_DIM, t), F32),
             pl.BlockSpec((1, N_KV_HEADS, HEAD_DIM, tm), lambda i: (i // tpb, 0, 0, i % tpb)))] * 6
    else:
        kv_outs = [dense(KV_WIDTH, F32)] * 6
    outs = ([dense(NSA_WIDTH, BF16)] + kv_outs
            + [dense(6 * KV_WIDTH, BF16), dense(LANES, F32), dense(3 * CONV_DIM, F32), dense(2 * D_MODEL, F32)])
    return pl.pallas_call(
        functools.partial(_proj_kernel, head_rows=head_rows),
        grid=(n // tm,),
        in_specs=[row(D_MODEL),
                  pl.BlockSpec((1, D_MODEL), lambda i: (0, 0)),
                  pl.BlockSpec((D_MODEL, PROJ_COLS), lambda i: (0, 0))],
        out_specs=[o[1] for o in outs],
        out_shape=[o[0] for o in outs],
        compiler_params=_cparams(("arbitrary",)),
        name="proj",
    )(x2d, g, w_all)


def _prep_w_in(w_in):
    sizes = (NSA_WIDTH,) + (KV_WIDTH,) * 6 + (3 * N_HEADS, CONV_DIM, CONV_DIM, CONV_DIM, D_MODEL, D_MODEL)
    offs = np.concatenate([[0], np.cumsum(sizes)])
    wq = w_in[:, :NSA_WIDTH].reshape(D_MODEL, N_KV_HEADS, GROUP, HEAD_DIM)
    wq = wq.transpose(0, 2, 1, 3).reshape(D_MODEL, NSA_WIDTH)
    wkv = w_in[:, offs[1]:offs[7]]
    wgn = jnp.pad(w_in[:, offs[7]:offs[8]], ((0, 0), (0, LANES - 3 * N_HEADS)))
    wrest = w_in[:, offs[8]:]
    return jnp.concatenate([wq, wkv, wgn, wrest], axis=1).astype(BF16)


CHUNKS_PER_PAGE = PAGE // CMP_STRIDE
XT_PITCH = CMP_STRIDE + 1


def _compress_kernel(pt_ref, *refs, pp):
    del pt_ref
    page_refs = refs[:pp + 1]
    wbd_ref, pe_ref, w1_ref, w2_ref, out_ref, acc_ref, pehid_ref = refs[pp + 1:]

    @pl.when((pl.program_id(0) == 0) & (pl.program_id(1) == 0))
    def _():
        pehid_ref[...] = _dot(pe_ref[...], w1_ref[...])

    cols = []
    for r in range(CMP_STRIDE):
        xr = jnp.concatenate([p[0, pl.ds(r, CHUNKS_PER_PAGE, stride=CMP_STRIDE), :] for p in page_refs], axis=0)
        cols.append(xr.astype(BF16))
    x = jnp.concatenate(cols, axis=1)
    acc_ref[...] = _dot(x, wbd_ref[...])
    nc = CHUNKS_PER_PAGE * pp
    pe = pehid_ref[0:1, :]
    hids = []
    for h in range(N_KV_HEADS):
        base = h * CMP_RATIO * CMP_HIDDEN
        p0 = acc_ref[0:nc, base:base + CMP_HIDDEN]
        p1 = acc_ref[pl.ds(1, nc), base + CMP_HIDDEN:base + 2 * CMP_HIDDEN]
        hids.append(jax.nn.gelu((pe + p0) + p1).astype(BF16))
    out_ref[0] = _dot(jnp.concatenate(hids, axis=1), w2_ref[...])


def _compress(pages, colblk, page_table, pe, w1, w2):
    nb, npages = page_table.shape
    pp = min(32, npages)
    assert npages % pp == 0
    nc = CHUNKS_PER_PAGE * pp
    w1r = w1.reshape(CMP_RATIO, CMP_STRIDE, HEAD_DIM, CMP_HIDDEN)
    blk = w1r.transpose(1, 2, 0, 3).reshape(CMP_STRIDE, HEAD_DIM, CMP_RATIO * CMP_HIDDEN)
    z = jnp.zeros_like(blk)
    wbd = jnp.concatenate([jnp.concatenate([blk, z], axis=2), jnp.concatenate([z, blk], axis=2)], axis=1)
    wbd = wbd.reshape(CMP_STRIDE * KV_WIDTH, N_KV_HEADS * CMP_RATIO * CMP_HIDDEN).astype(BF16)
    z2 = jnp.zeros_like(w2)
    w2bd = jnp.concatenate([jnp.concatenate([w2, z2], axis=1), jnp.concatenate([z2, w2], axis=1)], axis=0).astype(BF16)
    pe8 = jnp.broadcast_to(pe.reshape(1, CMP_LEN * HEAD_DIM), (8, CMP_LEN * HEAD_DIM)).astype(BF16)

    def page_spec(p):
        def imap(b, s, pt):
            return (pt[b, jnp.minimum(s * pp + p, npages - 1)], 0, colblk)
        return pl.BlockSpec((1, PAGE, KV_WIDTH), imap)

    const = lambda shape: pl.BlockSpec(shape, lambda b, s, pt: (0,) * len(shape))
    grid_spec = pltpu.PrefetchScalarGridSpec(
        num_scalar_prefetch=1,
        grid=(nb, npages // pp),
        in_specs=[page_spec(p) for p in range(pp + 1)]
        + [const(wbd.shape), const(pe8.shape), const((CMP_LEN * HEAD_DIM, CMP_HIDDEN)), const(w2bd.shape)],
        out_specs=pl.BlockSpec((1, nc, KV_WIDTH), lambda b, s, pt: (b, s, 0)),
        scratch_shapes=[pltpu.VMEM((nc + 8, N_KV_HEADS * CMP_RATIO * CMP_HIDDEN), F32),
                        pltpu.VMEM((8, CMP_HIDDEN), F32)],
    )
    return pl.pallas_call(
        functools.partial(_compress_kernel, pp=pp),
        grid_spec=grid_spec,
        out_shape=jax.ShapeDtypeStruct((nb, npages * CHUNKS_PER_PAGE, KV_WIDTH), F32),
        compiler_params=_cparams(("arbitrary", "arbitrary")),
        name="compress",
    )(page_table, *([pages] * (pp + 1)), wbd, pe8, w1.astype(BF16), w2bd)


def _compress_t_kernel(pt_ref, *refs, pp, npages, ns):
    cache_refs = refs[:ns]
    weight_refs = [refs[ns + 4 * c:ns + 4 * c + 4] for c in range(ns)]
    out_refs = refs[5 * ns:6 * ns]
    pbuf_ref, sem_ref, xt_ref, acc_ref, pehid_ref = refs[6 * ns:]
    b = pl.program_id(0)
    s = pl.program_id(1)
    n_steps = pl.num_programs(1)
    step = b * n_steps + s
    last = pl.num_programs(0) * n_steps - 1
    slot = step % 2
    copies = [(c, p) for c in range(ns) for p in range(pp + 1)]

    def page_copy(bb, ss, c, p, sl):
        page = pt_ref[bb, jnp.minimum(ss * pp + p, npages - 1)]
        return pltpu.make_async_copy(cache_refs[c].at[page], pbuf_ref.at[sl, c, p], sem_ref.at[sl])

    @pl.when(step == 0)
    def _():
        for c in range(ns):
            pehid_ref[c] = _dot(weight_refs[c][1][...], weight_refs[c][2][...])
        for c, p in copies:
            page_copy(0, 0, c, p, 0).start()

    wrap = s + 1 == n_steps
    nxt_b = jnp.minimum(jnp.where(wrap, b + 1, b), pl.num_programs(0) - 1)
    nxt_s = jnp.where(step == last, s, jnp.where(wrap, 0, s + 1))
    for c, p in copies:
        page_copy(nxt_b, nxt_s, c, p, 1 - slot).start()
    for c, p in copies:
        page_copy(b, s, c, p, slot).wait()

    nc = CHUNKS_PER_PAGE * pp
    m = nc + CHUNKS_PER_PAGE
    first_head = lax.broadcasted_iota(jnp.int32, (m, KV_WIDTH), 1) < HEAD_DIM
    for c in range(ns):
        w1p_ref, _, _, w2_ref = weight_refs[c]
        for p in range(pp + 1):
            page = pbuf_ref[slot, c, p].reshape(KV_WIDTH, PAGE)
            pt = page.T
            for k in range(CHUNKS_PER_PAGE):
                base = (p * CHUNKS_PER_PAGE + k) * XT_PITCH
                xt_ref[c, base:base + CMP_STRIDE, :] = pt[k * CMP_STRIDE:(k + 1) * CMP_STRIDE]
        pe = pehid_ref[c, 0:1, :]
        xs = [[], []]
        for r in range(0, CMP_STRIDE, 2):
            a = xt_ref[c, pl.ds(r, m, stride=XT_PITCH), :]
            d = xt_ref[c, pl.ds(r + 1, m, stride=XT_PITCH), :]
            xs[0].append(jnp.where(first_head, a, pltpu.roll(d, HEAD_DIM, axis=1)).astype(BF16))
            xs[1].append(jnp.where(first_head, pltpu.roll(a, HEAD_DIM, axis=1), d).astype(BF16))
        hids = []
        for h in range(N_KV_HEADS):
            x = jnp.concatenate(xs[h], axis=1)
            acc_ref[c, h] = _dot(x, w1p_ref[...])
            p0 = acc_ref[c, h, 0:nc, 0:CMP_HIDDEN]
            p1 = acc_ref[c, h, pl.ds(1, nc), CMP_HIDDEN:2 * CMP_HIDDEN]
            hids.append(jax.nn.gelu((pe + p0) + p1).astype(BF16))
        out_refs[c][0] = _dot(jnp.concatenate(hids, axis=1), w2_ref[...])

    @pl.when(step == last)
    def _():
        for c, p in copies:
            page_copy(nxt_b, nxt_s, c, p, 1 - slot).wait()


def _compress_t(caches, page_table, weights):
    nb, npages = page_table.shape
    ns = len(caches)
    pp = min(32, npages)
    assert npages % pp == 0
    nc = CHUNKS_PER_PAGE * pp
    pages = [c.transpose(0, 2, 3, 1) for c in caches]
    wargs = []
    for pe, w1, w2 in weights:
        w1p = w1.reshape(CMP_RATIO, CMP_STRIDE * HEAD_DIM, CMP_HIDDEN).transpose(1, 0, 2)
        w1p = w1p.reshape(CMP_STRIDE * HEAD_DIM, CMP_RATIO * CMP_HIDDEN).astype(BF16)
        z2 = jnp.zeros_like(w2)
        w2bd = jnp.concatenate([jnp.concatenate([w2, z2], axis=1), jnp.concatenate([z2, w2], axis=1)], axis=0)
        pe8 = jnp.broadcast_to(pe.reshape(1, CMP_LEN * HEAD_DIM), (8, CMP_LEN * HEAD_DIM)).astype(BF16)
        wargs += [w1p, pe8, w1.astype(BF16), w2bd.astype(BF16)]

    const = lambda arr: pl.BlockSpec(arr.shape, lambda b, s, pt: (0,) * arr.ndim)
    out_spec = pl.BlockSpec((1, nc, KV_WIDTH), lambda b, s, pt: (b, s, 0))
    grid_spec = pltpu.PrefetchScalarGridSpec(
        num_scalar_prefetch=1,
        grid=(nb, npages // pp),
        in_specs=[pl.BlockSpec(memory_space=pl.ANY)] * ns + [const(a) for a in wargs],
        out_specs=[out_spec] * ns,
        scratch_shapes=[pltpu.VMEM((2, ns, pp + 1, N_KV_HEADS, HEAD_DIM, PAGE), F32),
                        pltpu.SemaphoreType.DMA((2,)),
                        pltpu.VMEM((ns, (pp + 1) * CHUNKS_PER_PAGE * XT_PITCH + 8, KV_WIDTH), F32),
                        pltpu.VMEM((ns, N_KV_HEADS, nc + CHUNKS_PER_PAGE, CMP_RATIO * CMP_HIDDEN), F32),
                        pltpu.VMEM((ns, 8, CMP_HIDDEN), F32)],
    )
    return pl.pallas_call(
        functools.partial(_compress_t_kernel, pp=pp, npages=npages, ns=ns),
        grid_spec=grid_spec,
        out_shape=[jax.ShapeDtypeStruct((nb, npages * CHUNKS_PER_PAGE, KV_WIDTH), F32)] * ns,
        compiler_params=_cparams(("arbitrary", "arbitrary")),
        name="compress_t",
    )(page_table, *pages, *wargs)


def _sel_map(n_rows, n_cmp, n_cols, n_blocks):
    c0 = np.arange(n_rows)[:, None] * CMP_STRIDE
    s0 = np.arange(n_cols)[None, :] * SEL_BLOCK
    ov = np.minimum(c0 + CMP_LEN, s0 + SEL_BLOCK) - np.maximum(c0, s0)
    m = np.clip(ov, 0, None).astype(np.float32) / CMP_LEN
    m[n_cmp:] = 0.0
    m[:, n_blocks:] = 0.0
    return m


CMP_FRONT = 16
CMP_NEAR = 24
SLC_NEAR = 2 * Q_BLOCK
SLC_TILE = 512
WIN_KEYS = WINDOW + Q_BLOCK


def _softmax_update(state, s, v, mask=None, v_transposed=False):
    m_old, l_old, acc = state
    m_new = jnp.maximum(m_old, jnp.max(s, axis=-1, keepdims=True))
    p = jnp.exp(s - m_new)
    if mask is not None:
        p = jnp.where(mask, p, 0.0)
    alpha = jnp.exp(m_old - m_new)
    l_new = alpha * l_old + jnp.sum(p, axis=-1, keepdims=True)
    pv = _dot_t(p.astype(BF16), v) if v_transposed else _dot(p.astype(BF16), v)
    return m_new, l_new, alpha * acc + pv


def _softmax_finish(state):
    _, l, acc = state
    return jnp.where(l > 0.0, acc / jnp.where(l > 0.0, l, 1.0), 0.0)


def _tile_rows(x, n):
    return jnp.concatenate([x] * n, axis=0)


def _col_softmax_update(state, s, vt, smax=None):
    m_old, l_old, acc = state
    m_new = jnp.maximum(m_old, jnp.max(s, axis=0, keepdims=True) if smax is None else smax)
    p = jnp.exp(s - m_new)
    alpha = jnp.exp(m_old - m_new)
    l_new = alpha * l_old + jnp.sum(p, axis=0, keepdims=True)
    return m_new, l_new, alpha * acc + _dot(vt, p.astype(BF16))


def _col_softmax_finish(state):
    _, l, acc = state
    return acc * jnp.where(l > 0.0, 1.0 / jnp.where(l > 0.0, l, 1.0), 0.0)


def _nsa_prompt_pair_kernel(rb_ref, q_ref, gn_ref, kcmp_ref, vcmp_ref, vcmpt_ref, selmap_ref, selmapt_ref,
                            ks_ref, vst_ref, kw_ref, vwt_ref, et_ref, out_ref, nbc_ref, nbs_ref, nbw_ref, s_ref,
                            smax_ref):
    qb = pl.program_id(1)
    s0 = qb * Q_BLOCK
    gq = GROUP * Q_BLOCK
    heads = range(N_KV_HEADS)

    @pl.when(qb == 0)
    def _():
        for hk in heads:
            for g in range(GROUP):
                tab = lambda k, h=hk * GROUP + g: rb_ref[k, h]
                cols = slice(g * Q_BLOCK, (g + 1) * Q_BLOCK)
                cl = lax.broadcasted_iota(jnp.int32, (LANES, Q_BLOCK), 0)
                ql = lax.broadcasted_iota(jnp.int32, (LANES, Q_BLOCK), 1)
                dist = ql - CMP_STRIDE * (cl - CMP_FRONT) - (CMP_LEN - 1)
                ok = (dist >= 0) & (cl < CMP_NEAR)
                nbc_ref[hk, :, cols] = jnp.where(ok, _bias_from_scalars(_bucket(dist), tab), NEG_INF)
                kl = lax.broadcasted_iota(jnp.int32, (SLC_NEAR, Q_BLOCK), 0)
                ql = lax.broadcasted_iota(jnp.int32, (SLC_NEAR, Q_BLOCK), 1)
                dist = ql - kl + Q_BLOCK
                nbs_ref[hk, :, cols] = jnp.where(dist >= 0, _bias_from_scalars(_bucket(dist), tab), NEG_INF)
                kl = lax.broadcasted_iota(jnp.int32, (WIN_KEYS, Q_BLOCK), 0)
                ql = lax.broadcasted_iota(jnp.int32, (WIN_KEYS, Q_BLOCK), 1)
                dist = ql - kl + WINDOW
                ok = (dist >= 0) & (dist < WINDOW)
                nbw_ref[hk, :, cols] = jnp.where(ok, _bias_from_scalars(_bucket(dist), tab), NEG_INF)

    lane = lax.broadcasted_iota(jnp.int32, (Q_BLOCK, LANES), 1)
    qblk = q_ref[0].astype(F32)
    init = (jnp.full((1, gq), NEG_INF, F32), jnp.zeros((1, gq), F32), jnp.zeros((LANES, gq), F32))
    p0 = pl.multiple_of(s0, LANES)
    n_cmp_pad = kcmp_ref.shape[1] - LANES
    c8 = qb * (Q_BLOCK // CMP_STRIDE)
    near0 = pl.multiple_of(c8, 8)
    kc_far = kcmp_ref[0, CMP_FRONT:CMP_FRONT + n_cmp_pad, :].astype(BF16)
    kc_near = kcmp_ref[0, pl.ds(near0, LANES), :].astype(BF16)
    vct_near = vcmp_ref[0, pl.ds(near0, LANES), :].T.astype(BF16)
    sm_near_t = selmap_ref[pl.ds(near0, LANES), :].T.astype(BF16)
    kw = kw_ref[0, pl.ds(p0, WIN_KEYS), :]
    vwt = vwt_ref[0, :, pl.ds(p0, WIN_KEYS)]
    crow = lax.broadcasted_iota(jnp.int32, (n_cmp_pad, gq), 0)
    nrow = lax.broadcasted_iota(jnp.int32, (LANES, gq), 0)
    wrow = lax.broadcasted_iota(jnp.int32, (WIN_KEYS, gq), 0)
    krow = lax.broadcasted_iota(jnp.int32, (SLC_NEAR, gq), 0)
    n_sel = selmapt_ref.shape[0]
    blk = lax.broadcasted_iota(jnp.int32, (n_sel, Q_BLOCK), 0)
    cur = (s0 + lax.broadcasted_iota(jnp.int32, (n_sel, Q_BLOCK), 1)) // SEL_BLOCK
    forced = (blk == 0) | (blk == cur) | (blk == cur - 1)
    jcol = lax.broadcasted_iota(jnp.int32, (Q_BLOCK, n_sel), 1)
    far_blocks = (s0 - Q_BLOCK) // SEL_BLOCK

    def split_dot(a, p):
        hi = p.astype(BF16)
        return _dot(a, hi) + _dot(a, (p - hi.astype(F32)).astype(BF16))

    def before_far(hk):
        lmask = (lane // HEAD_DIM) == hk
        qp = jnp.concatenate(
            [jnp.where(lmask, qblk[:, g * LANES:(g + 1) * LANES], 0.0) for g in range(GROUP)], axis=0)
        qp = (qp * (HEAD_DIM ** -0.5)).astype(BF16)
        far_bias = jnp.concatenate(
            [jnp.full((1, Q_BLOCK), rb_ref[N_BUCKETS - 1, hk * GROUP + g], F32) for g in range(GROUP)], axis=1)

        s_far = jnp.where(crow < c8 - CMP_FRONT, _dot_t(kc_far, qp), NEG_INF)
        s_near = jnp.where(nrow >= CMP_FRONT - c8, _dot_t(kc_near, qp) + nbc_ref[hk], NEG_INF)
        m = jnp.maximum(jnp.max(s_far, axis=0, keepdims=True) + far_bias, jnp.max(s_near, axis=0, keepdims=True))
        p_far = jnp.exp(s_far - (m - far_bias))
        p_near = jnp.exp(s_near - m)
        l = jnp.sum(p_far, axis=0, keepdims=True) + jnp.sum(p_near, axis=0, keepdims=True)
        inv = jnp.where(m > 0.5 * NEG_INF, 1.0 / l, 0.0)
        pc_far = p_far * inv
        pc_near = p_near * inv
        o_cmp = _dot(vcmpt_ref[0], pc_far.astype(BF16)) + _dot(vct_near, pc_near.astype(BF16))

        s = jnp.where(wrow >= WINDOW - s0, _dot_t(kw, qp) + nbw_ref[hk], NEG_INF)
        o_win = _col_softmax_finish(_col_softmax_update(init, s, vwt))

        ps_far = sum(pc_far[:, g * Q_BLOCK:(g + 1) * Q_BLOCK] for g in range(GROUP))
        ps_near = sum(pc_near[:, g * Q_BLOCK:(g + 1) * Q_BLOCK] for g in range(GROUP))
        imp_t = split_dot(selmapt_ref[...], ps_far) + split_dot(sm_near_t, ps_near)
        val = jnp.where(blk > cur, NEG_INF, imp_t + FORCE_BONUS * forced.astype(F32))
        sel_t = jnp.zeros((n_sel, Q_BLOCK), F32)
        for _ in range(N_SELECT):
            mx = jnp.max(val, axis=0, keepdims=True)
            first = jnp.min(jnp.where(val == mx, blk, n_sel), axis=0, keepdims=True)
            hit = blk == first
            sel_t = jnp.where(hit, 1.0, sel_t)
            val = jnp.where(hit, -3e38, val)
        sel = sel_t.T > 0.5

        pen_far = jnp.where(sel & (jcol < far_blocks), 0.0, NEG_INF).astype(BF16)
        pen_near = jnp.where(sel, 0.0, NEG_INF).astype(BF16)
        q_far = jnp.concatenate([qp, _tile_rows(pen_far, GROUP)], axis=1)
        q_near = jnp.concatenate([qp, _tile_rows(pen_near, GROUP)], axis=1)
        return dict(far_bias=far_bias, o_cmp=o_cmp, o_win=o_win, q_far=q_far, q_near=q_near)

    pre = [before_far(hk) for hk in heads]
    n_tiles = (ks_ref.shape[1] - Q_BLOCK) // SLC_TILE

    def far_logits(kt):
        f0 = pl.multiple_of(Q_BLOCK + kt * SLC_TILE, LANES)
        k = jnp.concatenate([ks_ref[0, pl.ds(f0, SLC_TILE), :], et_ref[pl.ds(f0, SLC_TILE), :]], axis=1)
        return [_dot_t(k, pre[hk]["q_far"]) for hk in heads]

    def far_pair(i, states):
        for slot in range(2):
            kt = 2 * i + slot
            put_logits(1 - slot, jnp.minimum(kt + 1, n_tiles - 1))
            f0 = pl.multiple_of(Q_BLOCK + jnp.minimum(kt, n_tiles - 1) * SLC_TILE, LANES)
            vt = vst_ref[0, :, pl.ds(f0, SLC_TILE)]
            states = [_col_softmax_update(states[hk], s_ref[slot, hk], vt, smax=smax_ref[slot, hk, 0:1, :])
                      for hk in heads]
        return states

    def put_logits(slot, kt):
        for hk, s_new in enumerate(far_logits(kt)):
            s_ref[slot, hk] = s_new
            smax_ref[slot, hk] = jnp.broadcast_to(jnp.max(s_new, axis=0, keepdims=True), (8, gq))

    n_far = (qb + 2) // (SLC_TILE // Q_BLOCK)
    put_logits(0, 0)
    far_states = lax.fori_loop(0, (n_far + 1) // 2, far_pair, [init] * N_KV_HEADS)

    k_near = jnp.concatenate([ks_ref[0, pl.ds(p0, SLC_NEAR), :], et_ref[pl.ds(p0, SLC_NEAR), :]], axis=1)
    vt_near = vst_ref[0, :, pl.ds(p0, SLC_NEAR)]
    gst = jax.nn.sigmoid(gn_ref[0]).T
    outs = []
    for hk in heads:
        m_far, l_far, acc_far = far_states[hk]
        state = (m_far + pre[hk]["far_bias"], l_far, acc_far)
        s = jnp.where(krow >= Q_BLOCK - s0, _dot_t(k_near, pre[hk]["q_near"]) + nbs_ref[hk], NEG_INF)
        o_slc = _col_softmax_finish(_col_softmax_update(state, s, vt_near))
        per_g = []
        for g in range(GROUP):
            cols = slice(g * Q_BLOCK, (g + 1) * Q_BLOCK)
            c = 3 * (hk * GROUP + g)
            o = (gst[c:c + 1, :] * pre[hk]["o_cmp"][:, cols] + gst[c + 1:c + 2, :] * o_slc[:, cols]
                 + gst[c + 2:c + 3, :] * pre[hk]["o_win"][:, cols])
            per_g.append(o.T)
        outs.append(per_g)
    lane_h0 = lane < HEAD_DIM
    for g in range(GROUP):
        out_ref[0, :, g * LANES:(g + 1) * LANES] = jnp.where(lane_h0, outs[0][g], outs[1][g]).astype(out_ref.dtype)


def _nsa_prompt_pair(q, gn, kcmp, vcmp, kvb, vs_heads, vw_heads, rel_bias):
    b, t, _ = q.shape
    assert t % SLC_TILE == 0
    nqb = t // Q_BLOCK
    n_chunk = t // CMP_STRIDE
    n_cmp = n_chunk - CMP_RATIO + 1
    n_blocks = t // SEL_BLOCK
    assert n_blocks <= LANES
    cpad = ((0, 0), (CMP_FRONT, LANES - CMP_FRONT), (0, 0))
    kcmp_p = jnp.pad(kcmp, cpad)
    vcmp_p = jnp.pad(vcmp, cpad)
    vcmp_t = vcmp.transpose(0, 2, 1).astype(BF16)
    sm = _sel_map(n_chunk, n_cmp, LANES, n_blocks)
    selmap = jnp.asarray(np.pad(sm, ((CMP_FRONT, LANES - CMP_FRONT), (0, 0))), F32)
    selmap_t = jnp.asarray(sm.T, BF16)
    front = lambda z, n: jnp.pad(z, ((0, 0), (n, 0), (0, 0)))
    keys_front = lambda z, n: jnp.pad(z.reshape(b, KV_WIDTH, t).astype(BF16), ((0, 0), (0, 0), (n, 0)))
    ks_p = front(kvb[:, :, 2 * KV_WIDTH:3 * KV_WIDTH], Q_BLOCK)
    vs_t = keys_front(vs_heads, Q_BLOCK)
    kw_p = front(kvb[:, :, 4 * KV_WIDTH:5 * KV_WIDTH], WINDOW)
    vw_t = keys_front(vw_heads, WINDOW)
    pidx = np.arange(t + Q_BLOCK)
    e = (pidx[:, None] // SEL_BLOCK - Q_BLOCK // SEL_BLOCK == np.arange(LANES)[None, :])
    e = jnp.asarray(e.astype(np.float32), BF16)
    full = lambda arr: pl.BlockSpec((1,) + arr.shape[1:], lambda bi, i: (bi, 0, 0))
    const = lambda arr: pl.BlockSpec(arr.shape, lambda bi, i: (0, 0))
    return pl.pallas_call(
        _nsa_prompt_pair_kernel,
        grid=(b, nqb),
        in_specs=[pl.BlockSpec(memory_space=pltpu.SMEM),
                  pl.BlockSpec((1, Q_BLOCK, NSA_WIDTH), lambda bi, i: (bi, i, 0)),
                  pl.BlockSpec((1, Q_BLOCK, LANES), lambda bi, i: (bi, i, 0)),
                  full(kcmp_p), full(vcmp_p), full(vcmp_t), const(selmap), const(selmap_t),
                  full(ks_p), full(vs_t), full(kw_p), full(vw_t), const(e)],
        out_specs=pl.BlockSpec((1, Q_BLOCK, NSA_WIDTH), lambda bi, i: (bi, i, 0)),
        out_shape=jax.ShapeDtypeStruct((b, t, NSA_WIDTH), BF16),
        scratch_shapes=[pltpu.VMEM((N_KV_HEADS, LANES, GROUP * Q_BLOCK), F32),
                        pltpu.VMEM((N_KV_HEADS, SLC_NEAR, GROUP * Q_BLOCK), F32),
                        pltpu.VMEM((N_KV_HEADS, WIN_KEYS, GROUP * Q_BLOCK), F32),
                        pltpu.VMEM((2, N_KV_HEADS, SLC_TILE, GROUP * Q_BLOCK), F32),
                        pltpu.VMEM((2, N_KV_HEADS, 8, GROUP * Q_BLOCK), F32)],
        compiler_params=_cparams(("arbitrary", "arbitrary")),
        name="nsa_prompt",
    )(rel_bias, q, gn, kcmp_p, vcmp_p, vcmp_t, selmap, selmap_t, ks_p, vs_t, kw_p, vw_t, e)


ROWS_PER_TOKEN = 8


def _nsa_sample_cmp_kernel(q_ref, kcmp_ref, vcmp_ref, selmap_ref, rowtab_ref, ocmp_ref, idx_ref, *,
                           past_len, n_tok, n_cmp, n_blocks):
    rows = n_tok * ROWS_PER_TOKEN
    n_cpad = kcmp_ref.shape[1]
    ridx = lax.broadcasted_iota(jnp.int32, (rows, n_cpad), 0)
    ccol = lax.broadcasted_iota(jnp.int32, (rows, n_cpad), 1)
    dist = past_len + ridx // ROWS_PER_TOKEN - (CMP_STRIDE * ccol + CMP_LEN - 1)
    bucket = _bucket(dist)
    valid = (dist >= 0) & (ccol < n_cmp) & (ridx % ROWS_PER_TOKEN < GROUP)
    n_bpad = selmap_ref.shape[1]
    blk = lax.broadcasted_iota(jnp.int32, (ROWS_PER_TOKEN, n_bpad), 1)
    cur = (past_len + lax.broadcasted_iota(jnp.int32, (ROWS_PER_TOKEN, n_bpad), 0)) // SEL_BLOCK
    forced = (blk == 0) | (blk == cur) | (blk == cur - 1)
    olane = lax.broadcasted_iota(jnp.int32, (ROWS_PER_TOKEN, LANES), 1)
    biases = [_bias_from_rows(bucket, rowtab_ref[hk]) for hk in range(N_KV_HEADS)]
    for bi, hk in [(bi, hk) for bi in range(kcmp_ref.shape[0]) for hk in range(N_KV_HEADS)]:
        kc = kcmp_ref[bi].astype(BF16)
        vc = vcmp_ref[bi].astype(BF16)
        qp = (q_ref[bi, hk].astype(F32) * (HEAD_DIM ** -0.5)).astype(BF16)
        s = _dot_t(qp, kc) + biases[hk]
        s = jnp.where(valid, s, NEG_INF)
        m = jnp.max(s, axis=-1, keepdims=True)
        p = jnp.where(valid, jnp.exp(s - m), 0.0)
        l = jnp.sum(p, axis=-1, keepdims=True)
        pc = p * jnp.where(l > 0.0, 1.0 / jnp.where(l > 0.0, l, 1.0), 0.0)
        ocmp_ref[bi, hk] = _dot(pc.astype(BF16), vc)[:, hk * HEAD_DIM:(hk + 1) * HEAD_DIM]
        ps = jnp.concatenate(
            [jnp.sum(pc[i * ROWS_PER_TOKEN:(i + 1) * ROWS_PER_TOKEN], axis=0, keepdims=True) for i in range(n_tok)]
            + [jnp.zeros((ROWS_PER_TOKEN - n_tok, n_cpad), F32)], axis=0)
        imp = _split_dot(ps, selmap_ref[...])
        val = jnp.where((blk > cur) | (blk >= n_blocks), NEG_INF, imp + FORCE_BONUS * forced.astype(F32))
        picks = jnp.zeros((ROWS_PER_TOKEN, LANES), jnp.int32)
        for n in range(N_SELECT):
            mx = jnp.max(val, axis=-1, keepdims=True)
            first = jnp.min(jnp.where(val == mx, blk, n_bpad), axis=-1, keepdims=True)
            picks = jnp.where(olane == n, first, picks)
            val = jnp.where(blk == first, -3e38, val)
        idx_ref[bi, hk] = picks


def _nsa_sample_cmp(qs, kcmp, vcmp, rowtab, past_len, n_tok, n_cmp, n_blocks):
    nb = qs.shape[0]
    bps = next(d for d in (4, 2, 1) if nb % d == 0)
    n_cpad = kcmp.shape[1]
    n_bpad = -(-n_blocks // LANES) * LANES
    selmap = jnp.asarray(_sel_map(n_cpad, n_cmp, n_bpad, n_blocks), BF16)
    rows = n_tok * ROWS_PER_TOKEN
    return pl.pallas_call(
        functools.partial(_nsa_sample_cmp_kernel, past_len=past_len, n_tok=n_tok, n_cmp=n_cmp, n_blocks=n_blocks),
        grid=(nb // bps,),
        in_specs=[pl.BlockSpec((bps, N_KV_HEADS, rows, LANES), lambda b: (b, 0, 0, 0)),
                  pl.BlockSpec((bps, n_cpad, LANES), lambda b: (b, 0, 0)),
                  pl.BlockSpec((bps, n_cpad, LANES), lambda b: (b, 0, 0)),
                  pl.BlockSpec(selmap.shape, lambda b: (0, 0)),
                  pl.BlockSpec(rowtab.shape, lambda b: (0, 0, 0))],
        out_specs=[pl.BlockSpec((bps, N_KV_HEADS, rows, HEAD_DIM), lambda b: (b, 0, 0, 0)),
                   pl.BlockSpec((bps, N_KV_HEADS, ROWS_PER_TOKEN, LANES), lambda b: (b, 0, 0, 0))],
        out_shape=[jax.ShapeDtypeStruct((nb, N_KV_HEADS, rows, HEAD_DIM), F32),
                   jax.ShapeDtypeStruct((nb, N_KV_HEADS, ROWS_PER_TOKEN, LANES), jnp.int32)],
        compiler_params=_cparams(("arbitrary",)),
        name="nsa_sample_cmp",
    )(qs, kcmp, vcmp, selmap, rowtab)


def _nsa_sample_slc_kernel(idx_ref, pt_ref, ck_ref, cv_ref, q_ref, ocmp_ref, gn_ref, rowtab_ref, newk_ref, newv_ref,
                           swk_ref, swv_ref, nwk_ref, nwv_ref, out_ref, kbuf_ref, vbuf_ref, sem_ref, *,
                           past_len, n_tok, n_cache_blocks, npages):
    b = pl.program_id(0)
    hk = pl.program_id(1)
    step = b * N_KV_HEADS + hk
    last = pl.num_programs(0) * N_KV_HEADS - 1
    slot = step % 2
    per_step = n_tok * N_SELECT
    pages_per_block = PAGE // SEL_BLOCK

    def block_copies(st, i, sl):
        j = jnp.minimum(idx_ref[st * per_step + i], n_cache_blocks - 1)
        page = pt_ref[(st // N_KV_HEADS) * npages + j // pages_per_block]
        head = st % N_KV_HEADS
        return (pltpu.make_async_copy(ck_ref.at[page, head], kbuf_ref.at[sl, i], sem_ref.at[sl]),
                pltpu.make_async_copy(cv_ref.at[page, head], vbuf_ref.at[sl, i], sem_ref.at[sl]))

    @pl.when(step == 0)
    def _():
        for i in range(per_step):
            for c in block_copies(0, i, 0):
                c.start()

    nxt = jnp.minimum(step + 1, last)
    for i in range(per_step):
        for c in block_copies(nxt, i, 1 - slot):
            c.start()
    for i in range(per_step):
        for c in block_copies(step, i, slot):
            c.wait()

    rowtab = rowtab_ref[0]
    rvalid = lax.broadcasted_iota(jnp.int32, (ROWS_PER_TOKEN, 1), 0) < GROUP
    init = (jnp.full((ROWS_PER_TOKEN, 1), NEG_INF, F32), jnp.zeros((ROWS_PER_TOKEN, 1), F32),
            jnp.zeros((ROWS_PER_TOKEN, HEAD_DIM), F32))
    n_win = swk_ref.shape[3]
    kw = jnp.concatenate([swk_ref[0, 0], nwk_ref[0, 0]], axis=1).astype(BF16)
    vw = jnp.concatenate([swv_ref[0, 0], nwv_ref[0, 0]], axis=1).astype(BF16)
    wl = n_win + nwk_ref.shape[3]
    wpos = past_len - n_win + lax.broadcasted_iota(jnp.int32, (ROWS_PER_TOKEN, wl), 1)
    newk = newk_ref[0, 0]
    newv = newv_ref[0, 0]

    for tok in range(n_tok):
        t = past_len + tok
        qp = (q_ref[0, 0, tok].astype(F32) * (HEAD_DIM ** -0.5)).astype(BF16)
        page_row = lax.broadcasted_iota(jnp.int32, (ROWS_PER_TOKEN, PAGE), 1)
        bias_far = jnp.broadcast_to(rowtab[:, N_BUCKETS - 1:N_BUCKETS], (ROWS_PER_TOKEN, PAGE))
        bias_last = _bias_from_rows(_bucket(t - ((npages - 1) * PAGE + page_row)), rowtab)
        bias_new = _bias_from_rows(_bucket(t - (npages * PAGE + page_row)), rowtab)
        assert BUCKET_THR[-1] <= PAGE + 1
        kts, vts, biases = [], [], []
        for n in range(N_SELECT):
            i = tok * N_SELECT + n
            blk = idx_ref[step * per_step + i]
            is_new = blk >= n_cache_blocks
            page = blk // pages_per_block
            kts.append(jnp.where(is_new, newk, kbuf_ref[slot, i]).astype(BF16))
            vts.append(jnp.where(is_new, newv, vbuf_ref[slot, i]).astype(BF16))
            bias = jnp.where(is_new, bias_new, jnp.where(page == npages - 1, bias_last, bias_far))
            ok = (page_row // SEL_BLOCK == blk % pages_per_block) & (page * PAGE + page_row <= t)
            biases.append(jnp.where(ok, bias, NEG_INF))
        s = _dot(qp, jnp.concatenate(kts, axis=1)) + jnp.concatenate(biases, axis=1)
        mask = (s > 0.5 * NEG_INF) & rvalid
        o_slc = _softmax_finish(_softmax_update(init, jnp.where(mask, s, NEG_INF), jnp.concatenate(vts, axis=1),
                                                mask=mask, v_transposed=True))
        wdist = t - wpos
        s = _dot(qp, kw) + _bias_from_rows(_bucket(wdist), rowtab)
        mask = (wdist >= 0) & (wdist < WINDOW) & (wpos >= 0) & rvalid
        o_win = _softmax_finish(_softmax_update(init, jnp.where(mask, s, NEG_INF), vw, mask=mask, v_transposed=True))
        gs = jax.nn.sigmoid(gn_ref[0, 0, tok])
        out_ref[0, 0, tok] = gs[:, 0:1] * ocmp_ref[0, 0, tok] + gs[:, 1:2] * o_slc + gs[:, 2:3] * o_win

    @pl.when(step == last)
    def _():
        for i in range(per_step):
            for c in block_copies(nxt, i, 1 - slot):
                c.wait()


def _nsa_sample_slc(idx, page_table, cache_k, cache_v, q3, ocmp, gn3, rowtab, newk, newv, swk, swv, nwk, nwv,
                    past_len, n_tok):
    nb, npages = page_table.shape
    halves = PAGE // SEL_BLOCK
    n_cache_blocks = npages * halves
    ck = cache_k.transpose(0, 2, 3, 1)
    cv = cache_v.transpose(0, 2, 3, 1)

    tile = lambda w: pl.BlockSpec((1, 1, n_tok, ROWS_PER_TOKEN, w), lambda b, h, i, p: (b, h, 0, 0, 0))
    per_bh = lambda arr: pl.BlockSpec((1, 1) + arr.shape[2:], lambda b, h, i, p: (b, h, 0, 0))
    grid_spec = pltpu.PrefetchScalarGridSpec(
        num_scalar_prefetch=2,
        grid=(nb, N_KV_HEADS),
        in_specs=[pl.BlockSpec(memory_space=pl.ANY), pl.BlockSpec(memory_space=pl.ANY),
                  tile(HEAD_DIM), tile(HEAD_DIM), tile(LANES),
                  pl.BlockSpec((1, ROWS_PER_TOKEN, N_BUCKETS), lambda b, h, i, p: (h, 0, 0)),
                  per_bh(newk), per_bh(newv), per_bh(swk), per_bh(swv), per_bh(nwk), per_bh(nwv)],
        out_specs=tile(HEAD_DIM),
        scratch_shapes=[pltpu.VMEM((2, n_tok * N_SELECT, HEAD_DIM, PAGE), F32),
                        pltpu.VMEM((2, n_tok * N_SELECT, HEAD_DIM, PAGE), F32),
                        pltpu.SemaphoreType.DMA((2,))],
    )
    return pl.pallas_call(
        functools.partial(_nsa_sample_slc_kernel, past_len=past_len, n_tok=n_tok, n_cache_blocks=n_cache_blocks,
                          npages=npages),
        grid_spec=grid_spec,
        out_shape=jax.ShapeDtypeStruct((nb, N_KV_HEADS, n_tok, ROWS_PER_TOKEN, HEAD_DIM), F32),
        compiler_params=_cparams(("arbitrary", "arbitrary")),
        name="nsa_sample_slc",
    )(idx.reshape(-1), page_table.reshape(-1), ck, cv, q3, ocmp, gn3, rowtab, newk, newv, swk, swv, nwk, nwv)


def _merge_tail(x, a, c, gates, wn_ref, wc_ref, wo_ref):
    ga = gates[:, :D_MODEL]
    gb = gates[:, D_MODEL:]
    m = (jax.nn.sigmoid(ga) * _dot(a, wn_ref[...])
         + jax.nn.sigmoid(gb) * _dot(c.astype(BF16), wc_ref[...]))
    return x + _dot(m.astype(BF16), wo_ref[...])


def _merge_prompt_kernel(x_ref, a_ref, conv_ref, halo_ref, gate_ref, cw_ref, cb_ref, wn_ref, wc_ref, wo_ref,
                         out_ref, cst_ref, *, tiles_per_batch):
    i = pl.program_id(0)
    first = i % tiles_per_batch == 0
    conv = conv_ref[...]
    bg = conv[:, :CONV_DIM]
    u = conv[:, CONV_DIM:2 * CONV_DIM] * conv[:, 2 * CONV_DIM:]
    halo = halo_ref[...]
    uh = jnp.where(first, 0.0, halo[:, CONV_DIM:2 * CONV_DIM] * halo[:, 2 * CONV_DIM:])
    tm = u.shape[0]
    row = lax.broadcasted_iota(jnp.int32, (tm, CONV_DIM), 0)
    u1 = jnp.where(row == 0, uh[7:8], pltpu.roll(u, 1, axis=0))
    u2 = pltpu.roll(u, 2, axis=0)
    u2 = jnp.where(row == 0, uh[6:7], jnp.where(row == 1, uh[7:8], u2))
    cw = cw_ref[...]
    y = ((cb_ref[...] + cw[0:1] * u2) + cw[1:2] * u1) + cw[2:3] * u
    c = bg * y
    out_ref[...] = _merge_tail(x_ref[...], a_ref[...], c, gate_ref[...], wn_ref, wc_ref, wo_ref)
    cst_ref[0] = u[tm - 8:, :]


def _merge_prompt(x2d, a2, conv3, gates, conv_w, conv_b, wn, wc, wo, t):
    n = x2d.shape[0]
    tm = min(512, t)
    assert t % tm == 0
    tpb = t // tm
    row = lambda w: pl.BlockSpec((tm, w), lambda i: (i, 0))
    const = lambda arr: pl.BlockSpec(arr.shape, lambda i: (0,) * arr.ndim)
    return pl.pallas_call(
        functools.partial(_merge_prompt_kernel, tiles_per_batch=tpb),
        grid=(n // tm,),
        in_specs=[row(D_MODEL),
                  row(NSA_WIDTH),
                  row(3 * CONV_DIM),
                  pl.BlockSpec((8, 3 * CONV_DIM), lambda i: (jnp.maximum(i * (tm // 8) - 1, 0), 0)),
                  row(2 * D_MODEL),
                  const(conv_w), const(conv_b), const(wn), const(wc), const(wo)],
        out_specs=[row(D_MODEL), pl.BlockSpec((1, 8, CONV_DIM), lambda i: (i // tpb, 0, 0))],
        out_shape=[jax.ShapeDtypeStruct((n, D_MODEL), F32),
                   jax.ShapeDtypeStruct((n // t, 8, CONV_DIM), F32)],
        compiler_params=_cparams(("arbitrary",)),
        name="merge_prompt",
    )(x2d, a2, conv3, conv3, gates, conv_w, conv_b, wn, wc, wo)


def _merge_sample_kernel(x_ref, a_ref, conv_ref, past_ref, gate_ref, cw_ref, cb_ref, wn_ref, wc_ref, wo_ref,
                         out_ref, cst_ref, *, n_tok, nb):
    conv = conv_ref[...]
    bg = conv[:, :CONV_DIM]
    u = conv[:, CONV_DIM:2 * CONV_DIM] * conv[:, 2 * CONV_DIM:]
    up = jnp.concatenate([past_ref[...], u], axis=0)
    cw = cw_ref[...]
    y = cb_ref[...]
    for k in range(CONV_WIDTH):
        y = y + cw[k:k + 1] * up[k * nb:(k + n_tok) * nb]
    c = bg * y
    out_ref[...] = _merge_tail(x_ref[...], a_ref[...].astype(BF16), c, gate_ref[...], wn_ref, wc_ref, wo_ref)
    cst_ref[...] = up[n_tok * nb:]


def _merge_sample(x_tm, a_tm, conv_tm, past_tm, gates_tm, conv_w, conv_b, wn, wc, wo, n_tok, nb):
    n = x_tm.shape[0]
    full = lambda arr: pl.BlockSpec(arr.shape, lambda i: (0,) * arr.ndim)
    args = (x_tm, a_tm, conv_tm, past_tm, gates_tm, conv_w, conv_b, wn, wc, wo)
    return pl.pallas_call(
        functools.partial(_merge_sample_kernel, n_tok=n_tok, nb=nb),
        grid=(1,),
        in_specs=[full(a) for a in args],
        out_specs=[pl.BlockSpec((n, D_MODEL), lambda i: (0, 0)),
                   pl.BlockSpec(((CONV_WIDTH - 1) * nb, CONV_DIM), lambda i: (0, 0))],
        out_shape=[jax.ShapeDtypeStruct((n, D_MODEL), F32),
                   jax.ShapeDtypeStruct(((CONV_WIDTH - 1) * nb, CONV_DIM), F32)],
        compiler_params=_cparams(("arbitrary",)),
        name="merge_sample",
    )(*args)


def _rms(x, g):
    return (x * lax.rsqrt(jnp.mean(x * x, axis=-1, keepdims=True) + RMS_EPS)) * g


def _mlp_kernel(x_ref, g_ref, wu_ref, wd_ref, gf_ref, out_ref, h_ref, acc_ref):
    j = pl.program_id(1)

    @pl.when(j == 0)
    def _():
        h_ref[...] = _rms(x_ref[...], g_ref[...]).astype(BF16)
        acc_ref[...] = jnp.zeros(acc_ref.shape, F32)

    half = h_ref.shape[0] // 2
    for r in range(2):
        rows = slice(r * half, (r + 1) * half)
        up = jnp.square(jnp.maximum(_dot(h_ref[rows, :], wu_ref[...]), 0.0)).astype(BF16)
        acc_ref[rows, :] += _dot(up, wd_ref[...])

    @pl.when(j == pl.num_programs(1) - 1)
    def _():
        out_ref[...] = _rms(x_ref[...] + acc_ref[...], gf_ref[...])


def _mlp(x2d, g_mlp, w_up, w_down, g_final):
    n = x2d.shape[0]
    tm = min(1024, n)
    tf = 1024
    assert n % tm == 0 and D_FF % tf == 0
    return pl.pallas_call(
        _mlp_kernel,
        grid=(n // tm, D_FF // tf),
        in_specs=[pl.BlockSpec((tm, D_MODEL), lambda i, j: (i, 0)),
                  pl.BlockSpec((1, D_MODEL), lambda i, j: (0, 0)),
                  pl.BlockSpec((D_MODEL, tf), lambda i, j: (0, j)),
                  pl.BlockSpec((tf, D_MODEL), lambda i, j: (j, 0)),
                  pl.BlockSpec((1, D_MODEL), lambda i, j: (0, 0))],
        out_specs=pl.BlockSpec((tm, D_MODEL), lambda i, j: (i, 0)),
        out_shape=jax.ShapeDtypeStruct((n, D_MODEL), F32),
        scratch_shapes=[pltpu.VMEM((tm, D_MODEL), BF16), pltpu.VMEM((tm, D_MODEL), F32)],
        compiler_params=_cparams(("arbitrary", "arbitrary")),
        name="mlp",
    )(x2d, g_mlp, w_up, w_down, g_final)


def _kv_rows(z, lead):
    return z.reshape(lead + (N_KV_HEADS, HEAD_DIM))


def _layer(xp, xs, caches, states, page_table, w, rel_bias):
    (cache_cmp_k, cache_cmp_v, cache_slc_k, cache_slc_v) = caches
    (state_win_k, state_win_v, state_conv) = states
    nbp, t, _ = xp.shape
    nbs, n_tok, _ = xs.shape
    npages = page_table.shape[1]
    past_len = npages * PAGE
    assert t % PAGE == 0 and n_tok <= GROUP and state_win_k.shape[1] >= n_tok

    w_all = _prep_w_in(w["w_in"])
    g_attn = w["g_attn"].reshape(1, D_MODEL)
    wn = w["w_nsa_out"].reshape(N_KV_HEADS, GROUP, HEAD_DIM, D_MODEL).transpose(1, 0, 2, 3)
    wn = wn.reshape(NSA_WIDTH, D_MODEL).astype(BF16)
    wc = w["w_conv_out"].astype(BF16)
    wo = w["w_o"].astype(BF16)
    wu = w["w_up"].astype(BF16)
    wd = w["w_down"].astype(BF16)
    g_mlp = w["g_mlp"].reshape(1, D_MODEL)
    g_final = w["g_final"].reshape(1, D_MODEL)
    conv_w = w["conv_w"]
    conv_b = w["conv_b"].reshape(1, CONV_DIM)
    cmp_k = (w["cmp_pe_k"], w["cmp_w1_k"], w["cmp_w2_k"])
    cmp_v = (w["cmp_pe_v"], w["cmp_w1_v"], w["cmp_w2_v"])

    n_p = nbp * t
    (q, kc, vc, *kv_heads, kvb, gn, conv3, gates) = _proj(xp.reshape(n_p, D_MODEL), g_attn, w_all, True, t)
    ident = jnp.arange(nbp * (t // PAGE), dtype=jnp.int32).reshape(nbp, t // PAGE)
    kcmp = _compress(kc.reshape(n_p // PAGE, PAGE, KV_WIDTH), 0, ident, *cmp_k)
    vcmp = _compress(vc.reshape(n_p // PAGE, PAGE, KV_WIDTH), 0, ident, *cmp_v)
    a2 = _nsa_prompt_pair(q.reshape(nbp, t, NSA_WIDTH), gn.reshape(nbp, t, LANES), kcmp, vcmp,
                          kvb.reshape(nbp, t, 6 * KV_WIDTH), kv_heads[3], kv_heads[5], rel_bias)
    x1, cst = _merge_prompt(xp.reshape(n_p, D_MODEL), a2.reshape(n_p, NSA_WIDTH),
                            conv3, gates, conv_w, conv_b, wn, wc, wo, t)
    yp = _mlp(x1, g_mlp, wu, wd, g_final).reshape(nbp, t, D_MODEL)
    win_p = min(WINDOW, t)
    back = lambda z: z.transpose(0, 3, 1, 2)
    new_p = tuple(back(z) for z in kv_heads[:4]) + (
        back(kv_heads[4][..., t - win_p:]), back(kv_heads[5][..., t - win_p:]), cst[:, 8 - (CONV_WIDTH - 1):, :])

    n_s = nbs * n_tok
    (qs, kcs, vcs, kss, vss, kws, vws, _, gns, conv3s, gatess) = _proj(xs.reshape(n_s, D_MODEL), g_attn, w_all, False)
    n_rows = past_len + n_tok
    n_chunk = n_rows // CMP_STRIDE
    assert n_chunk == past_len // CMP_STRIDE, "new rows must not complete a compression chunk"
    n_cmp = n_chunk - CMP_RATIO + 1
    n_blocks = -(-n_rows // SEL_BLOCK)
    kcmp_s, vcmp_s = _compress_t((cache_cmp_k, cache_cmp_v), page_table, (cmp_k, cmp_v))

    lane = np.arange(LANES)
    q5 = qs.reshape(nbs, n_tok, GROUP, LANES)
    q3 = jnp.stack([jnp.where(jnp.asarray((lane // HEAD_DIM) == h), q5, jnp.zeros_like(q5))
                    for h in range(N_KV_HEADS)], axis=1)
    q3 = jnp.pad(q3, ((0, 0), (0, 0), (0, 0), (0, ROWS_PER_TOKEN - GROUP), (0, 0)))
    tab = rel_bias.T.reshape(N_KV_HEADS, GROUP, N_BUCKETS)
    rowtab3 = jnp.pad(tab, ((0, 0), (0, ROWS_PER_TOKEN - GROUP), (0, 0)))
    rowtab = jnp.tile(rowtab3, (1, n_tok, 1))
    ocmp, idx = _nsa_sample_cmp(q3.reshape(nbs, N_KV_HEADS, n_tok * ROWS_PER_TOKEN, LANES), kcmp_s, vcmp_s,
                                rowtab, past_len, n_tok, n_cmp, n_blocks)
    idx = idx[:, :, :n_tok, :N_SELECT]
    gn3 = gns[:, :3 * N_HEADS].reshape(nbs, n_tok, N_KV_HEADS, GROUP, 3).transpose(0, 2, 1, 3, 4)
    gn3 = jnp.pad(gn3, ((0, 0), (0, 0), (0, 0), (0, ROWS_PER_TOKEN - GROUP), (0, LANES - 3)))
    q3s = jnp.pad(qs.reshape(nbs, n_tok, GROUP, N_KV_HEADS, HEAD_DIM).transpose(0, 3, 1, 2, 4),
                  ((0, 0), (0, 0), (0, 0), (0, ROWS_PER_TOKEN - GROUP), (0, 0)))
    ocmp5 = ocmp.reshape(nbs, N_KV_HEADS, n_tok, ROWS_PER_TOKEN, HEAD_DIM)
    new_t = lambda z: jnp.pad(z.reshape(nbs, n_tok, N_KV_HEADS, HEAD_DIM).transpose(0, 2, 3, 1),
                              ((0, 0), (0, 0), (0, 0), (0, LANES - n_tok)))
    a_s = _nsa_sample_slc(idx, page_table, cache_slc_k, cache_slc_v, q3s, ocmp5, gn3, rowtab3,
                          new_t(kss), new_t(vss),
                          state_win_k.transpose(0, 2, 3, 1), state_win_v.transpose(0, 2, 3, 1),
                          new_t(kws), new_t(vws), past_len, n_tok)
    a_tm = a_s[:, :, :, :GROUP, :].transpose(2, 0, 3, 1, 4).reshape(n_s, NSA_WIDTH)
    to_tm = lambda z: z.reshape(nbs, n_tok, -1).transpose(1, 0, 2).reshape(n_s, -1)
    past_tm = state_conv.transpose(1, 0, 2).reshape((CONV_WIDTH - 1) * nbs, CONV_DIM)
    x1s, csts = _merge_sample(to_tm(xs), a_tm, to_tm(conv3s), past_tm, to_tm(gatess),
                              conv_w, conv_b, wn, wc, wo, n_tok, nbs)
    ys = _mlp(x1s, g_mlp, wu, wd, g_final).reshape(n_tok, nbs, D_MODEL).transpose(1, 0, 2)
    win_buf = state_win_k.shape[1]
    kw_all = jnp.concatenate([state_win_k, _kv_rows(kws, (nbs, n_tok))], axis=1)[:, -win_buf:]
    vw_all = jnp.concatenate([state_win_v, _kv_rows(vws, (nbs, n_tok))], axis=1)[:, -win_buf:]
    new_s = tuple(_kv_rows(z, (nbs, n_tok)) for z in (kcs, vcs, kss, vss)) + (
        kw_all, vw_all, csts.reshape(CONV_WIDTH - 1, nbs, CONV_DIM).transpose(1, 0, 2))
    return yp, ys, new_p, new_s


def kernel(x_prompt, x_sample, cache_cmp_k, cache_cmp_v, cache_slc_k, cache_slc_v, state_win_k, state_win_v,
           state_conv, page_table, g_attn, w_in, cmp_pe_k, cmp_w1_k, cmp_w2_k, cmp_pe_v, cmp_w1_v, cmp_w2_v,
           conv_w, conv_b, w_nsa_out, w_conv_out, w_o, g_mlp, w_up, w_down, rel_bias, g_final):
    depth = w_in.shape[0]
    assert depth == 1, "the final norm is fused into the last layer's MLP kernel"
    names = ("g_attn", "w_in", "cmp_pe_k", "cmp_w1_k", "cmp_w2_k", "cmp_pe_v", "cmp_w1_v", "cmp_w2_v",
             "conv_w", "conv_b", "w_nsa_out", "w_conv_out", "w_o", "g_mlp", "w_up", "w_down")
    vals = (g_attn, w_in, cmp_pe_k, cmp_w1_k, cmp_w2_k, cmp_pe_v, cmp_w1_v, cmp_w2_v,
            conv_w, conv_b, w_nsa_out, w_conv_out, w_o, g_mlp, w_up, w_down)
    l = 0
    w = {k: v[l] for k, v in zip(names, vals)}
    w["g_final"] = g_final
    yp, ys, new_p, new_s = _layer(
        x_prompt, x_sample,
        (cache_cmp_k[l], cache_cmp_v[l], cache_slc_k[l], cache_slc_v[l]),
        (state_win_k[l], state_win_v[l], state_conv[l]), page_table, w, rel_bias)
    return (yp, ys) + tuple(z[None] for z in new_p) + tuple(z[None] for z in new_s)
```

```python
import functools
import math

import numpy as np
import jax
import jax.numpy as jnp
from jax import lax
from jax.experimental import pallas as pl
from jax.experimental.pallas import tpu as pltpu

F32 = jnp.float32
BF16 = jnp.bfloat16

D_MODEL = 1024
N_HEADS = 8
HEAD_DIM = 64
N_KV_HEADS = 2
GROUP = N_HEADS // N_KV_HEADS
KV_WIDTH = N_KV_HEADS * HEAD_DIM
NSA_WIDTH = N_HEADS * HEAD_DIM
CMP_LEN = 32
CMP_STRIDE = 16
CMP_RATIO = CMP_LEN // CMP_STRIDE
CMP_HIDDEN = 256
SEL_BLOCK = 64
N_SELECT = 16
WINDOW = 512
CONV_DIM = 512
CONV_WIDTH = 3
D_FF = 4 * D_MODEL
N_BUCKETS = 32
MAX_DISTANCE = 128
Q_BLOCK = 128
PAGE = 128
RMS_EPS = 1e-6
NEG_INF = -1e30
FORCE_BONUS = 1e4
LANES = 128
VMEM_LIMIT = 56 * 1024 * 1024


def _bucket_thresholds():
    max_exact = N_BUCKETS // 2
    n = np.arange(max_exact, 4 * MAX_DISTANCE).astype(np.float32)
    large = max_exact + (np.log(n / np.float32(max_exact)) / np.float32(math.log(MAX_DISTANCE / max_exact))
                         * np.float32(N_BUCKETS - max_exact)).astype(np.int32)
    large = np.minimum(large, N_BUCKETS - 1)
    thr = []
    for k in range(max_exact + 1, N_BUCKETS):
        thr.append(int(np.arange(max_exact, 4 * MAX_DISTANCE)[np.argmax(large >= k)]))
    return tuple(thr)


BUCKET_THR = _bucket_thresholds()


def _bucket(dist):
    n = jnp.maximum(dist, 0)
    big = jnp.full(n.shape, N_BUCKETS // 2, jnp.int32)
    for thr in BUCKET_THR:
        big = big + (n >= thr).astype(jnp.int32)
    return jnp.where(n < N_BUCKETS // 2, n, big)


def _bias_from_scalars(bucket, table_fn):
    out = jnp.zeros(bucket.shape, F32)
    for k in range(N_BUCKETS):
        out = jnp.where(bucket == k, table_fn(k), out)
    return out


def _bias_from_rows(bucket, rowtab):
    out = jnp.zeros(bucket.shape, F32)
    for k in range(N_BUCKETS):
        out = jnp.where(bucket == k, rowtab[:, k:k + 1], out)
    return out


def _dot(a, b):
    return jnp.dot(a, b, preferred_element_type=F32)


def _dot_t(a, b):
    return lax.dot_general(a, b, (((1,), (1,)), ((), ())), preferred_element_type=F32)


def _split_dot(p, m):
    hi = p.astype(BF16)
    lo = (p - hi.astype(F32)).astype(BF16)
    return _dot(hi, m) + _dot(lo, m)


def _cparams(sem):
    return pltpu.CompilerParams(dimension_semantics=sem, vmem_limit_bytes=VMEM_LIMIT)


PROJ_WIDTHS = (NSA_WIDTH, 6 * KV_WIDTH, LANES, 3 * CONV_DIM, 2 * D_MODEL)
PROJ_COLS = sum(PROJ_WIDTHS)


def _proj_kernel(x_ref, g_ref, w_ref, *out_refs, head_rows):
    q_ref, kvb_ref, gn_ref, conv_ref, gate_ref = out_refs[0], *out_refs[-4:]
    kv_refs = out_refs[1:-4]
    x = x_ref[...]
    y = x * lax.rsqrt(jnp.mean(x * x, axis=-1, keepdims=True) + RMS_EPS)
    hb = (y * g_ref[...]).astype(BF16)
    q_ref[...] = _dot(hb, w_ref[:, 0:NSA_WIDTH]).astype(q_ref.dtype)
    off = NSA_WIDTH
    kv = _dot(hb, w_ref[:, off:off + 6 * KV_WIDTH])
    tm = kv.shape[0]
    for i in range(6):
        z = kv[:, i * KV_WIDTH:(i + 1) * KV_WIDTH]
        if head_rows:
            kv_refs[2 + i][0] = z.T.reshape(N_KV_HEADS, HEAD_DIM, tm)
        if not head_rows or i < 2:
            kv_refs[i][...] = z
    kvb_ref[...] = kv.astype(BF16)
    off += 6 * KV_WIDTH
    gn_ref[...] = _dot(hb, w_ref[:, off:off + LANES])
    off += LANES
    conv_ref[...] = _dot(hb, w_ref[:, off:off + 3 * CONV_DIM])
    off += 3 * CONV_DIM
    gate_ref[...] = _dot(hb, w_ref[:, off:off + 2 * D_MODEL])


def _proj(x2d, g, w_all, head_rows, t=None):
    n = x2d.shape[0]
    tm = min(512, n if t is None else t)
    assert n % tm == 0
    row = lambda w: pl.BlockSpec((tm, w), lambda i: (i, 0))
    dense = lambda w, dt: (jax.ShapeDtypeStruct((n, w), dt), row(w))
    if head_rows:
        tpb = t // tm
        kv_outs = [dense(KV_WIDTH, F32)] * 2 + [
            (jax.ShapeDtypeStruct((n // t, N_KV_HEADS, HEAD_DIM, t), F32),
             pl.BlockSpec((1, N_KV_HEADS, HEAD_DIM, tm), lambda i: (i // tpb, 0, 0, i % tpb)))] * 6
    else:
        kv_outs = [dense(KV_WIDTH, F32)] * 6
    outs = ([dense(NSA_WIDTH, BF16)] + kv_outs
            + [dense(6 * KV_WIDTH, BF16), dense(LANES, F32), dense(3 * CONV_DIM, F32), dense(2 * D_MODEL, F32)])
    return pl.pallas_call(
        functools.partial(_proj_kernel, head_rows=head_rows),
        grid=(n // tm,),
        in_specs=[row(D_MODEL),
                  pl.BlockSpec((1, D_MODEL), lambda i: (0, 0)),
                  pl.BlockSpec((D_MODEL, PROJ_COLS), lambda i: (0, 0))],
        out_specs=[o[1] for o in outs],
        out_shape=[o[0] for o in outs],
        compiler_params=_cparams(("arbitrary",)),
        name="proj",
    )(x2d, g, w_all)


def _prep_w_in(w_in):
    sizes = (NSA_WIDTH,) + (KV_WIDTH,) * 6 + (3 * N_HEADS, CONV_DIM, CONV_DIM, CONV_DIM, D_MODEL, D_MODEL)
    offs = np.concatenate([[0], np.cumsum(sizes)])
    wq = w_in[:, :NSA_WIDTH].reshape(D_MODEL, N_KV_HEADS, GROUP, HEAD_DIM)
    wq = wq.transpose(0, 2, 1, 3).reshape(D_MODEL, NSA_WIDTH)
    wkv = w_in[:, offs[1]:offs[7]]
    wgn = jnp.pad(w_in[:, offs[7]:offs[8]], ((0, 0), (0, LANES - 3 * N_HEADS)))
    wrest = w_in[:, offs[8]:]
    return jnp.concatenate([wq, wkv, wgn, wrest], axis=1).astype(BF16)


CHUNKS_PER_PAGE = PAGE // CMP_STRIDE
XT_PITCH = CMP_STRIDE + 1


def _compress_kernel(pt_ref, *refs, pp):
    del pt_ref
    page_refs = refs[:pp + 1]
    wbd_ref, pe_ref, w1_ref, w2_ref, out_ref, acc_ref, pehid_ref = refs[pp + 1:]

    @pl.when((pl.program_id(0) == 0) & (pl.program_id(1) == 0))
    def _():
        pehid_ref[...] = _dot(pe_ref[...], w1_ref[...])

    cols = []
    for r in range(CMP_STRIDE):
        xr = jnp.concatenate([p[0, pl.ds(r, CHUNKS_PER_PAGE, stride=CMP_STRIDE), :] for p in page_refs], axis=0)
        cols.append(xr.astype(BF16))
    x = jnp.concatenate(cols, axis=1)
    acc_ref[...] = _dot(x, wbd_ref[...])
    nc = CHUNKS_PER_PAGE * pp
    pe = pehid_ref[0:1, :]
    hids = []
    for h in range(N_KV_HEADS):
        base = h * CMP_RATIO * CMP_HIDDEN
        p0 = acc_ref[0:nc, base:base + CMP_HIDDEN]
        p1 = acc_ref[pl.ds(1, nc), base + CMP_HIDDEN:base + 2 * CMP_HIDDEN]
        hids.append(jax.nn.gelu((pe + p0) + p1).astype(BF16))
    out_ref[0] = _dot(jnp.concatenate(hids, axis=1), w2_ref[...])


def _compress(pages, colblk, page_table, pe, w1, w2):
    nb, npages = page_table.shape
    pp = min(32, npages)
    assert npages % pp == 0
    nc = CHUNKS_PER_PAGE * pp
    w1r = w1.reshape(CMP_RATIO, CMP_STRIDE, HEAD_DIM, CMP_HIDDEN)
    blk = w1r.transpose(1, 2, 0, 3).reshape(CMP_STRIDE, HEAD_DIM, CMP_RATIO * CMP_HIDDEN)
    z = jnp.zeros_like(blk)
    wbd = jnp.concatenate([jnp.concatenate([blk, z], axis=2), jnp.concatenate([z, blk], axis=2)], axis=1)
    wbd = wbd.reshape(CMP_STRIDE * KV_WIDTH, N_KV_HEADS * CMP_RATIO * CMP_HIDDEN).astype(BF16)
    z2 = jnp.zeros_like(w2)
    w2bd = jnp.concatenate([jnp.concatenate([w2, z2], axis=1), jnp.concatenate([z2, w2], axis=1)], axis=0).astype(BF16)
    pe8 = jnp.broadcast_to(pe.reshape(1, CMP_LEN * HEAD_DIM), (8, CMP_LEN * HEAD_DIM)).astype(BF16)

    def page_spec(p):
        def imap(b, s, pt):
            return (pt[b, jnp.minimum(s * pp + p, npages - 1)], 0, colblk)
        return pl.BlockSpec((1, PAGE, KV_WIDTH), imap)

    const = lambda shape: pl.BlockSpec(shape, lambda b, s, pt: (0,) * len(shape))
    grid_spec = pltpu.PrefetchScalarGridSpec(
        num_scalar_prefetch=1,
        grid=(nb, npages // pp),
        in_specs=[page_spec(p) for p in range(pp + 1)]
        + [const(wbd.shape), const(pe8.shape), const((CMP_LEN * HEAD_DIM, CMP_HIDDEN)), const(w2bd.shape)],
        out_specs=pl.BlockSpec((1, nc, KV_WIDTH), lambda b, s, pt: (b, s, 0)),
        scratch_shapes=[pltpu.VMEM((nc + 8, N_KV_HEADS * CMP_RATIO * CMP_HIDDEN), F32),
                        pltpu.VMEM((8, CMP_HIDDEN), F32)],
    )
    return pl.pallas_call(
        functools.partial(_compress_kernel, pp=pp),
        grid_spec=grid_spec,
        out_shape=jax.ShapeDtypeStruct((nb, npages * CHUNKS_PER_PAGE, KV_WIDTH), F32),
        compiler_params=_cparams(("arbitrary", "arbitrary")),
        name="compress",
    )(page_table, *([pages] * (pp + 1)), wbd, pe8, w1.astype(BF16), w2bd)


def _compress_t_kernel(pt_ref, *refs, pp, npages, ns):
    cache_refs = refs[:ns]
    weight_refs = [refs[ns + 4 * c:ns + 4 * c + 4] for c in range(ns)]
    out_refs = refs[5 * ns:6 * ns]
    pbuf_ref, sem_ref, xt_ref, acc_ref, pehid_ref = refs[6 * ns:]
    b = pl.program_id(0)
    s = pl.program_id(1)
    n_steps = pl.num_programs(1)
    step = b * n_steps + s
    last = pl.num_programs(0) * n_steps - 1
    slot = step % 2
    copies = [(c, p) for c in range(ns) for p in range(pp + 1)]

    def page_copy(bb, ss, c, p, sl):
        page = pt_ref[bb, jnp.minimum(ss * pp + p, npages - 1)]
        return pltpu.make_async_copy(cache_refs[c].at[page], pbuf_ref.at[sl, c, p], sem_ref.at[sl])

    @pl.when(step == 0)
    def _():
        for c in range(ns):
            pehid_ref[c] = _dot(weight_refs[c][1][...], weight_refs[c][2][...])
        for c, p in copies:
            page_copy(0, 0, c, p, 0).start()

    wrap = s + 1 == n_steps
    nxt_b = jnp.minimum(jnp.where(wrap, b + 1, b), pl.num_programs(0) - 1)
    nxt_s = jnp.where(step == last, s, jnp.where(wrap, 0, s + 1))
    for c, p in copies:
        page_copy(nxt_b, nxt_s, c, p, 1 - slot).start()
    for c, p in copies:
        page_copy(b, s, c, p, slot).wait()

    nc = CHUNKS_PER_PAGE * pp
    m = nc + CHUNKS_PER_PAGE
    first_head = lax.broadcasted_iota(jnp.int32, (m, KV_WIDTH), 1) < HEAD_DIM
    for c in range(ns):
        w1p_ref, _, _, w2_ref = weight_refs[c]
        for p in range(pp + 1):
            page = pbuf_ref[slot, c, p].reshape(KV_WIDTH, PAGE)
            pt = page.T
            for k in range(CHUNKS_PER_PAGE):
                base = (p * CHUNKS_PER_PAGE + k) * XT_PITCH
                xt_ref[c, base:base + CMP_STRIDE, :] = pt[k * CMP_STRIDE:(k + 1) * CMP_STRIDE]
        pe = pehid_ref[c, 0:1, :]
        xs = [[], []]
        for r in range(0, CMP_STRIDE, 2):
            a = xt_ref[c, pl.ds(r, m, stride=XT_PITCH), :]
            d = xt_ref[c, pl.ds(r + 1, m, stride=XT_PITCH), :]
            xs[0].append(jnp.where(first_head, a, pltpu.roll(d, HEAD_DIM, axis=1)).astype(BF16))
            xs[1].append(jnp.where(first_head, pltpu.roll(a, HEAD_DIM, axis=1), d).astype(BF16))
        hids = []
        for h in range(N_KV_HEADS):
            x = jnp.concatenate(xs[h], axis=1)
            acc_ref[c, h] = _dot(x, w1p_ref[...])
            p0 = acc_ref[c, h, 0:nc, 0:CMP_HIDDEN]
            p1 = acc_ref[c, h, pl.ds(1, nc), CMP_HIDDEN:2 * CMP_HIDDEN]
            hids.append(jax.nn.gelu((pe + p0) + p1).astype(BF16))
        out_refs[c][0] = _dot(jnp.concatenate(hids, axis=1), w2_ref[...])

    @pl.when(step == last)
    def _():
        for c, p in copies:
            page_copy(nxt_b, nxt_s, c, p, 1 - slot).wait()


def _compress_t(caches, page_table, weights):
    nb, npages = page_table.shape
    ns = len(caches)
    pp = min(32, npages)
    assert npages % pp == 0
    nc = CHUNKS_PER_PAGE * pp
    pages = [c.transpose(0, 2, 3, 1) for c in caches]
    wargs = []
    for pe, w1, w2 in weights:
        w1p = w1.reshape(CMP_RATIO, CMP_STRIDE * HEAD_DIM, CMP_HIDDEN).transpose(1, 0, 2)
        w1p = w1p.reshape(CMP_STRIDE * HEAD_DIM, CMP_RATIO * CMP_HIDDEN).astype(BF16)
        z2 = jnp.zeros_like(w2)
        w2bd = jnp.concatenate([jnp.concatenate([w2, z2], axis=1), jnp.concatenate([z2, w2], axis=1)], axis=0)
        pe8 = jnp.broadcast_to(pe.reshape(1, CMP_LEN * HEAD_DIM), (8, CMP_LEN * HEAD_DIM)).astype(BF16)
        wargs += [w1p, pe8, w1.astype(BF16), w2bd.astype(BF16)]

    const = lambda arr: pl.BlockSpec(arr.shape, lambda b, s, pt: (0,) * arr.ndim)
    out_spec = pl.BlockSpec((1, nc, KV_WIDTH), lambda b, s, pt: (b, s, 0))
    grid_spec = pltpu.PrefetchScalarGridSpec(
        num_scalar_prefetch=1,
        grid=(nb, npages // pp),
        in_specs=[pl.BlockSpec(memory_space=pl.ANY)] * ns + [const(a) for a in wargs],
        out_specs=[out_spec] * ns,
        scratch_shapes=[pltpu.VMEM((2, ns, pp + 1, N_KV_HEADS, HEAD_DIM, PAGE), F32),
                        pltpu.SemaphoreType.DMA((2,)),
                        pltpu.VMEM((ns, (pp + 1) * CHUNKS_PER_PAGE * XT_PITCH + 8, KV_WIDTH), F32),
                        pltpu.VMEM((ns, N_KV_HEADS, nc + CHUNKS_PER_PAGE, CMP_RATIO * CMP_HIDDEN), F32),
                        pltpu.VMEM((ns, 8, CMP_HIDDEN), F32)],
    )
    return pl.pallas_call(
        functools.partial(_compress_t_kernel, pp=pp, npages=npages, ns=ns),
        grid_spec=grid_spec,
        out_shape=[jax.ShapeDtypeStruct((nb, npages * CHUNKS_PER_PAGE, KV_WIDTH), F32)] * ns,
        compiler_params=_cparams(("arbitrary", "arbitrary")),
        name="compress_t",
    )(page_table, *pages, *wargs)


def _sel_map(n_rows, n_cmp, n_cols, n_blocks):
    c0 = np.arange(n_rows)[:, None] * CMP_STRIDE
    s0 = np.arange(n_cols)[None, :] * SEL_BLOCK
    ov = np.minimum(c0 + CMP_LEN, s0 + SEL_BLOCK) - np.maximum(c0, s0)
    m = np.clip(ov, 0, None).astype(np.float32) / CMP_LEN
    m[n_cmp:] = 0.0
    m[:, n_blocks:] = 0.0
    return m


CMP_FRONT = 16
CMP_NEAR = 24
SLC_NEAR = 2 * Q_BLOCK
SLC_TILE = 512
WIN_KEYS = WINDOW + Q_BLOCK


def _softmax_update(state, s, v, mask=None, v_transposed=False):
    m_old, l_old, acc = state
    m_new = jnp.maximum(m_old, jnp.max(s, axis=-1, keepdims=True))
    p = jnp.exp(s - m_new)
    if mask is not None:
        p = jnp.where(mask, p, 0.0)
    alpha = jnp.exp(m_old - m_new)
    l_new = alpha * l_old + jnp.sum(p, axis=-1, keepdims=True)
    pv = _dot_t(p.astype(BF16), v) if v_transposed else _dot(p.astype(BF16), v)
    return m_new, l_new, alpha * acc + pv


def _softmax_finish(state):
    _, l, acc = state
    return jnp.where(l > 0.0, acc / jnp.where(l > 0.0, l, 1.0), 0.0)


def _tile_rows(x, n):
    return jnp.concatenate([x] * n, axis=0)


def _col_softmax_update(state, s, vt, smax=None):
    m_old, l_old, acc = state
    m_new = jnp.maximum(m_old, jnp.max(s, axis=0, keepdims=True) if smax is None else smax)
    p = jnp.exp(s - m_new)
    alpha = jnp.exp(m_old - m_new)
    l_new = alpha * l_old + jnp.sum(p, axis=0, keepdims=True)
    return m_new, l_new, alpha * acc + _dot(vt, p.astype(BF16))


def _col_softmax_finish(state):
    _, l, acc = state
    return acc * jnp.where(l > 0.0, 1.0 / jnp.where(l > 0.0, l, 1.0), 0.0)


def _nsa_prompt_pair_kernel(rb_ref, q_ref, gn_ref, kcmp_ref, vcmp_ref, vcmpt_ref, selmap_ref, selmapt_ref,
                            ks_ref, vst_ref, kw_ref, vwt_ref, et_ref, out_ref, nbc_ref, nbs_ref, nbw_ref, s_ref,
                            smax_ref):
    qb = pl.program_id(1)
    s0 = qb * Q_BLOCK
    gq = GROUP * Q_BLOCK
    heads = range(N_KV_HEADS)

    @pl.when(qb == 0)
    def _():
        for hk in heads:
            for g in range(GROUP):
                tab = lambda k, h=hk * GROUP + g: rb_ref[k, h]
                cols = slice(g * Q_BLOCK, (g + 1) * Q_BLOCK)
                cl = lax.broadcasted_iota(jnp.int32, (LANES, Q_BLOCK), 0)
                ql = lax.broadcasted_iota(jnp.int32, (LANES, Q_BLOCK), 1)
                dist = ql - CMP_STRIDE * (cl - CMP_FRONT) - (CMP_LEN - 1)
                ok = (dist >= 0) & (cl < CMP_NEAR)
                nbc_ref[hk, :, cols] = jnp.where(ok, _bias_from_scalars(_bucket(dist), tab), NEG_INF)
                kl = lax.broadcasted_iota(jnp.int32, (SLC_NEAR, Q_BLOCK), 0)
                ql = lax.broadcasted_iota(jnp.int32, (SLC_NEAR, Q_BLOCK), 1)
                dist = ql - kl + Q_BLOCK
                nbs_ref[hk, :, cols] = jnp.where(dist >= 0, _bias_from_scalars(_bucket(dist), tab), NEG_INF)
                kl = lax.broadcasted_iota(jnp.int32, (WIN_KEYS, Q_BLOCK), 0)
                ql = lax.broadcasted_iota(jnp.int32, (WIN_KEYS, Q_BLOCK), 1)
                dist = ql - kl + WINDOW
                ok = (dist >= 0) & (dist < WINDOW)
                nbw_ref[hk, :, cols] = jnp.where(ok, _bias_from_scalars(_bucket(dist), tab), NEG_INF)

    lane = lax.broadcasted_iota(jnp.int32, (Q_BLOCK, LANES), 1)
    qblk = q_ref[0].astype(F32)
    init = (jnp.full((1, gq), NEG_INF, F32), jnp.zeros((1, gq), F32), jnp.zeros((LANES, gq), F32))
    p0 = pl.multiple_of(s0, LANES)
    n_cmp_pad = kcmp_ref.shape[1] - LANES
    c8 = qb * (Q_BLOCK // CMP_STRIDE)
    near0 = pl.multiple_of(c8, 8)
    kc_far = kcmp_ref[0, CMP_FRONT:CMP_FRONT + n_cmp_pad, :].astype(BF16)
    kc_near = kcmp_ref[0, pl.ds(near0, LANES), :].astype(BF16)
    vct_near = vcmp_ref[0, pl.ds(near0, LANES), :].T.astype(BF16)
    sm_near_t = selmap_ref[pl.ds(near0, LANES), :].T.astype(BF16)
    kw = kw_ref[0, pl.ds(p0, WIN_KEYS), :]
    vwt = vwt_ref[0, :, pl.ds(p0, WIN_KEYS)]
    crow = lax.broadcasted_iota(jnp.int32, (n_cmp_pad, gq), 0)
    nrow = lax.broadcasted_iota(jnp.int32, (LANES, gq), 0)
    wrow = lax.broadcasted_iota(jnp.int32, (WIN_KEYS, gq), 0)
    krow = lax.broadcasted_iota(jnp.int32, (SLC_NEAR, gq), 0)
    n_sel = selmapt_ref.shape[0]
    blk = lax.broadcasted_iota(jnp.int32, (n_sel, Q_BLOCK), 0)
    cur = (s0 + lax.broadcasted_iota(jnp.int32, (n_sel, Q_BLOCK), 1)) // SEL_BLOCK
    forced = (blk == 0) | (blk == cur) | (blk == cur - 1)
    jcol = lax.broadcasted_iota(jnp.int32, (Q_BLOCK, n_sel), 1)
    far_blocks = (s0 - Q_BLOCK) // SEL_BLOCK

    def split_dot(a, p):
        hi = p.astype(BF16)
        return _dot(a, hi) + _dot(a, (p - hi.astype(F32)).astype(BF16))

    def before_far(hk):
        lmask = (lane // HEAD_DIM) == hk
        qp = jnp.concatenate(
            [jnp.where(lmask, qblk[:, g * LANES:(g + 1) * LANES], 0.0) for g in range(GROUP)], axis=0)
        qp = (qp * (HEAD_DIM ** -0.5)).astype(BF16)
        far_bias = jnp.concatenate(
            [jnp.full((1, Q_BLOCK), rb_ref[N_BUCKETS - 1, hk * GROUP + g], F32) for g in range(GROUP)], axis=1)

        s_far = jnp.where(crow < c8 - CMP_FRONT, _dot_t(kc_far, qp), NEG_INF)
        s_near = jnp.where(nrow >= CMP_FRONT - c8, _dot_t(kc_near, qp) + nbc_ref[hk], NEG_INF)
        m = jnp.maximum(jnp.max(s_far, axis=0, keepdims=True) + far_bias, jnp.max(s_near, axis=0, keepdims=True))
        p_far = jnp.exp(s_far - (m - far_bias))
        p_near = jnp.exp(s_near - m)
        l = jnp.sum(p_far, axis=0, keepdims=True) + jnp.sum(p_near, axis=0, keepdims=True)
        inv = jnp.where(m > 0.5 * NEG_INF, 1.0 / l, 0.0)
        pc_far = p_far * inv
        pc_near = p_near * inv
        o_cmp = _dot(vcmpt_ref[0], pc_far.astype(BF16)) + _dot(vct_near, pc_near.astype(BF16))

        s = jnp.where(wrow >= WINDOW - s0, _dot_t(kw, qp) + nbw_ref[hk], NEG_INF)
        o_win = _col_softmax_finish(_col_softmax_update(init, s, vwt))

        ps_far = sum(pc_far[:, g * Q_BLOCK:(g + 1) * Q_BLOCK] for g in range(GROUP))
        ps_near = sum(pc_near[:, g * Q_BLOCK:(g + 1) * Q_BLOCK] for g in range(GROUP))
        imp_t = split_dot(selmapt_ref[...], ps_far) + split_dot(sm_near_t, ps_near)
        val = jnp.where(blk > cur, NEG_INF, imp_t + FORCE_BONUS * forced.astype(F32))
        sel_t = jnp.zeros((n_sel, Q_BLOCK), F32)
        for _ in range(N_SELECT):
            mx = jnp.max(val, axis=0, keepdims=True)
            first = jnp.min(jnp.where(val == mx, blk, n_sel), axis=0, keepdims=True)
            hit = blk == first
            sel_t = jnp.where(hit, 1.0, sel_t)
            val = jnp.where(hit, -3e38, val)
        sel = sel_t.T > 0.5

        pen_far = jnp.where(sel & (jcol < far_blocks), 0.0, NEG_INF).astype(BF16)
        pen_near = jnp.where(sel, 0.0, NEG_INF).astype(BF16)
        q_far = jnp.concatenate([qp, _tile_rows(pen_far, GROUP)], axis=1)
        q_near = jnp.concatenate([qp, _tile_rows(pen_near, GROUP)], axis=1)
        return dict(far_bias=far_bias, o_cmp=o_cmp, o_win=o_win, q_far=q_far, q_near=q_near)

    pre = [before_far(hk) for hk in heads]
    n_tiles = (ks_ref.shape[1] - Q_BLOCK) // SLC_TILE

    def far_logits(kt):
        f0 = pl.multiple_of(Q_BLOCK + kt * SLC_TILE, LANES)
        k = jnp.concatenate([ks_ref[0, pl.ds(f0, SLC_TILE), :], et_ref[pl.ds(f0, SLC_TILE), :]], axis=1)
        return [_dot_t(k, pre[hk]["q_far"]) for hk in heads]

    def far_pair(i, states):
        for slot in range(2):
            kt = 2 * i + slot
            put_logits(1 - slot, jnp.minimum(kt + 1, n_tiles - 1))
            f0 = pl.multiple_of(Q_BLOCK + jnp.minimum(kt, n_tiles - 1) * SLC_TILE, LANES)
            vt = vst_ref[0, :, pl.ds(f0, SLC_TILE)]
            states = [_col_softmax_update(states[hk], s_ref[slot, hk, :, 0:gq], vt, smax=smax_ref[slot, hk, 0:1, :])
                      for hk in heads]
        return states

    def put_logits(slot, kt):
        for hk, s_new in enumerate(far_logits(kt)):
            s_ref[slot, hk, :, 0:gq] = s_new
            smax_ref[slot, hk] = jnp.broadcast_to(jnp.max(s_new, axis=0, keepdims=True), (8, gq))

    n_far = (qb + 2) // (SLC_TILE // Q_BLOCK)
    put_logits(0, 0)
    far_states = lax.fori_loop(0, (n_far + 1) // 2, far_pair, [init] * N_KV_HEADS)

    k_near = jnp.concatenate([ks_ref[0, pl.ds(p0, SLC_NEAR), :], et_ref[pl.ds(p0, SLC_NEAR), :]], axis=1)
    vt_near = vst_ref[0, :, pl.ds(p0, SLC_NEAR)]
    gst = jax.nn.sigmoid(gn_ref[0]).T
    outs = []
    for hk in heads:
        m_far, l_far, acc_far = far_states[hk]
        state = (m_far + pre[hk]["far_bias"], l_far, acc_far)
        s = jnp.where(krow >= Q_BLOCK - s0, _dot_t(k_near, pre[hk]["q_near"]) + nbs_ref[hk], NEG_INF)
        o_slc = _col_softmax_finish(_col_softmax_update(state, s, vt_near))
        per_g = []
        for g in range(GROUP):
            cols = slice(g * Q_BLOCK, (g + 1) * Q_BLOCK)
            c = 3 * (hk * GROUP + g)
            o = (gst[c:c + 1, :] * pre[hk]["o_cmp"][:, cols] + gst[c + 1:c + 2, :] * o_slc[:, cols]
                 + gst[c + 2:c + 3, :] * pre[hk]["o_win"][:, cols])
            per_g.append(o.T)
        outs.append(per_g)
    lane_h0 = lane < HEAD_DIM
    for g in range(GROUP):
        out_ref[0, :, g * LANES:(g + 1) * LANES] = jnp.where(lane_h0, outs[0][g], outs[1][g]).astype(out_ref.dtype)


def _nsa_prompt_pair(q, gn, kcmp, vcmp, kvb, vs_heads, vw_heads, rel_bias):
    b, t, _ = q.shape
    assert t % SLC_TILE == 0
    nqb = t // Q_BLOCK
    n_chunk = t // CMP_STRIDE
    n_cmp = n_chunk - CMP_RATIO + 1
    n_blocks = t // SEL_BLOCK
    assert n_blocks <= LANES
    cpad = ((0, 0), (CMP_FRONT, LANES - CMP_FRONT), (0, 0))
    kcmp_p = jnp.pad(kcmp, cpad)
    vcmp_p = jnp.pad(vcmp, cpad)
    vcmp_t = vcmp.transpose(0, 2, 1).astype(BF16)
    sm = _sel_map(n_chunk, n_cmp, LANES, n_blocks)
    selmap = jnp.asarray(np.pad(sm, ((CMP_FRONT, LANES - CMP_FRONT), (0, 0))), F32)
    selmap_t = jnp.asarray(sm.T, BF16)
    front = lambda z, n: jnp.pad(z, ((0, 0), (n, 0), (0, 0)))
    keys_front = lambda z, n: jnp.pad(z.reshape(b, KV_WIDTH, t).astype(BF16), ((0, 0), (0, 0), (n, 0)))
    ks_p = front(kvb[:, :, 2 * KV_WIDTH:3 * KV_WIDTH], Q_BLOCK)
    vs_t = keys_front(vs_heads, Q_BLOCK)
    kw_p = front(kvb[:, :, 4 * KV_WIDTH:5 * KV_WIDTH], WINDOW)
    vw_t = keys_front(vw_heads, WINDOW)
    pidx = np.arange(t + Q_BLOCK)
    e = (pidx[:, None] // SEL_BLOCK - Q_BLOCK // SEL_BLOCK == np.arange(LANES)[None, :])
    e = jnp.asarray(e.astype(np.float32), BF16)
    full = lambda arr: pl.BlockSpec((1,) + arr.shape[1:], lambda bi, i: (bi, 0, 0))
    const = lambda arr: pl.BlockSpec(arr.shape, lambda bi, i: (0, 0))
    return pl.pallas_call(
        _nsa_prompt_pair_kernel,
        grid=(b, nqb),
        in_specs=[pl.BlockSpec(memory_space=pltpu.SMEM),
                  pl.BlockSpec((1, Q_BLOCK, NSA_WIDTH), lambda bi, i: (bi, i, 0)),
                  pl.BlockSpec((1, Q_BLOCK, LANES), lambda bi, i: (bi, i, 0)),
                  full(kcmp_p), full(vcmp_p), full(vcmp_t), const(selmap), const(selmap_t),
                  full(ks_p), full(vs_t), full(kw_p), full(vw_t), const(e)],
        out_specs=pl.BlockSpec((1, Q_BLOCK, NSA_WIDTH), lambda bi, i: (bi, i, 0)),
        out_shape=jax.ShapeDtypeStruct((b, t, NSA_WIDTH), BF16),
        scratch_shapes=[pltpu.VMEM((N_KV_HEADS, LANES, GROUP * Q_BLOCK), F32),
                        pltpu.VMEM((N_KV_HEADS, SLC_NEAR, GROUP * Q_BLOCK), F32),
                        pltpu.VMEM((N_KV_HEADS, WIN_KEYS, GROUP * Q_BLOCK), F32),
                        pltpu.VMEM((2, N_KV_HEADS, SLC_TILE, GROUP * Q_BLOCK + LANES), F32),
                        pltpu.VMEM((2, N_KV_HEADS, 8, GROUP * Q_BLOCK), F32)],
        compiler_params=_cparams(("arbitrary", "arbitrary")),
        name="nsa_prompt",
    )(rel_bias, q, gn, kcmp_p, vcmp_p, vcmp_t, selmap, selmap_t, ks_p, vs_t, kw_p, vw_t, e)


ROWS_PER_TOKEN = 8


def _nsa_sample_cmp_kernel(q_ref, kcmp_ref, vcmp_ref, selmap_ref, rowtab_ref, ocmp_ref, idx_ref, *,
                           past_len, n_tok, n_cmp, n_blocks):
    rows = n_tok * ROWS_PER_TOKEN
    n_cpad = kcmp_ref.shape[1]
    ridx = lax.broadcasted_iota(jnp.int32, (rows, n_cpad), 0)
    ccol = lax.broadcasted_iota(jnp.int32, (rows, n_cpad), 1)
    dist = past_len + ridx // ROWS_PER_TOKEN - (CMP_STRIDE * ccol + CMP_LEN - 1)
    bucket = _bucket(dist)
    valid = (dist >= 0) & (ccol < n_cmp) & (ridx % ROWS_PER_TOKEN < GROUP)
    n_bpad = selmap_ref.shape[1]
    blk = lax.broadcasted_iota(jnp.int32, (ROWS_PER_TOKEN, n_bpad), 1)
    cur = (past_len + lax.broadcasted_iota(jnp.int32, (ROWS_PER_TOKEN, n_bpad), 0)) // SEL_BLOCK
    forced = (blk == 0) | (blk == cur) | (blk == cur - 1)
    olane = lax.broadcasted_iota(jnp.int32, (ROWS_PER_TOKEN, LANES), 1)
    biases = [_bias_from_rows(bucket, rowtab_ref[hk]) for hk in range(N_KV_HEADS)]
    for bi, hk in [(bi, hk) for bi in range(kcmp_ref.shape[0]) for hk in range(N_KV_HEADS)]:
        kc = kcmp_ref[bi].astype(BF16)
        vc = vcmp_ref[bi].astype(BF16)
        qp = (q_ref[bi, hk].astype(F32) * (HEAD_DIM ** -0.5)).astype(BF16)
        s = _dot_t(qp, kc) + biases[hk]
        s = jnp.where(valid, s, NEG_INF)
        m = jnp.max(s, axis=-1, keepdims=True)
        p = jnp.where(valid, jnp.exp(s - m), 0.0)
        l = jnp.sum(p, axis=-1, keepdims=True)
        pc = p * jnp.where(l > 0.0, 1.0 / jnp.where(l > 0.0, l, 1.0), 0.0)
        ocmp_ref[bi, hk] = _dot(pc.astype(BF16), vc)[:, hk * HEAD_DIM:(hk + 1) * HEAD_DIM]
        ps = jnp.concatenate(
            [jnp.sum(pc[i * ROWS_PER_TOKEN:(i + 1) * ROWS_PER_TOKEN], axis=0, keepdims=True) for i in range(n_tok)]
            + [jnp.zeros((ROWS_PER_TOKEN - n_tok, n_cpad), F32)], axis=0)
        imp = _split_dot(ps, selmap_ref[...])
        val = jnp.where((blk > cur) | (blk >= n_blocks), NEG_INF, imp + FORCE_BONUS * forced.astype(F32))
        picks = jnp.zeros((ROWS_PER_TOKEN, LANES), jnp.int32)
        for n in range(N_SELECT):
            mx = jnp.max(val, axis=-1, keepdims=True)
            first = jnp.min(jnp.where(val == mx, blk, n_bpad), axis=-1, keepdims=True)
            picks = jnp.where(olane == n, first, picks)
            val = jnp.where(blk == first, -3e38, val)
        idx_ref[bi, hk] = picks


def _nsa_sample_cmp(qs, kcmp, vcmp, rowtab, past_len, n_tok, n_cmp, n_blocks):
    nb = qs.shape[0]
    bps = next(d for d in (4, 2, 1) if nb % d == 0)
    n_cpad = kcmp.shape[1]
    n_bpad = -(-n_blocks // LANES) * LANES
    selmap = jnp.asarray(_sel_map(n_cpad, n_cmp, n_bpad, n_blocks), BF16)
    rows = n_tok * ROWS_PER_TOKEN
    return pl.pallas_call(
        functools.partial(_nsa_sample_cmp_kernel, past_len=past_len, n_tok=n_tok, n_cmp=n_cmp, n_blocks=n_blocks),
        grid=(nb // bps,),
        in_specs=[pl.BlockSpec((bps, N_KV_HEADS, rows, LANES), lambda b: (b, 0, 0, 0)),
                  pl.BlockSpec((bps, n_cpad, LANES), lambda b: (b, 0, 0)),
                  pl.BlockSpec((bps, n_cpad, LANES), lambda b: (b, 0, 0)),
                  pl.BlockSpec(selmap.shape, lambda b: (0, 0)),
                  pl.BlockSpec(rowtab.shape, lambda b: (0, 0, 0))],
        out_specs=[pl.BlockSpec((bps, N_KV_HEADS, rows, HEAD_DIM), lambda b: (b, 0, 0, 0)),
                   pl.BlockSpec((bps, N_KV_HEADS, ROWS_PER_TOKEN, LANES), lambda b: (b, 0, 0, 0))],
        out_shape=[jax.ShapeDtypeStruct((nb, N_KV_HEADS, rows, HEAD_DIM), F32),
                   jax.ShapeDtypeStruct((nb, N_KV_HEADS, ROWS_PER_TOKEN, LANES), jnp.int32)],
        compiler_params=_cparams(("arbitrary",)),
        name="nsa_sample_cmp",
    )(qs, kcmp, vcmp, selmap, rowtab)


def _nsa_sample_slc_kernel(idx_ref, pt_ref, ck_ref, cv_ref, q_ref, ocmp_ref, gn_ref, rowtab_ref, newk_ref, newv_ref,
                           swk_ref, swv_ref, nwk_ref, nwv_ref, out_ref, kbuf_ref, vbuf_ref, sem_ref, *,
                           past_len, n_tok, n_cache_blocks, npages):
    b = pl.program_id(0)
    hk = pl.program_id(1)
    step = b * N_KV_HEADS + hk
    last = pl.num_programs(0) * N_KV_HEADS - 1
    slot = step % 2
    per_step = n_tok * N_SELECT
    pages_per_block = PAGE // SEL_BLOCK

    def block_copies(st, i, sl):
        j = jnp.minimum(idx_ref[st * per_step + i], n_cache_blocks - 1)
        page = pt_ref[(st // N_KV_HEADS) * npages + j // pages_per_block]
        head = st % N_KV_HEADS
        return (pltpu.make_async_copy(ck_ref.at[page, head], kbuf_ref.at[sl, i], sem_ref.at[sl]),
                pltpu.make_async_copy(cv_ref.at[page, head], vbuf_ref.at[sl, i], sem_ref.at[sl]))

    @pl.when(step == 0)
    def _():
        for i in range(per_step):
            for c in block_copies(0, i, 0):
                c.start()

    nxt = jnp.minimum(step + 1, last)
    for i in range(per_step):
        for c in block_copies(nxt, i, 1 - slot):
            c.start()
    for i in range(per_step):
        for c in block_copies(step, i, slot):
            c.wait()

    rowtab = rowtab_ref[0]
    rvalid = lax.broadcasted_iota(jnp.int32, (ROWS_PER_TOKEN, 1), 0) < GROUP
    init = (jnp.full((ROWS_PER_TOKEN, 1), NEG_INF, F32), jnp.zeros((ROWS_PER_TOKEN, 1), F32),
            jnp.zeros((ROWS_PER_TOKEN, HEAD_DIM), F32))
    n_win = swk_ref.shape[3]
    kw = jnp.concatenate([swk_ref[0, 0], nwk_ref[0, 0]], axis=1).astype(BF16)
    vw = jnp.concatenate([swv_ref[0, 0], nwv_ref[0, 0]], axis=1).astype(BF16)
    wl = n_win + nwk_ref.shape[3]
    wpos = past_len - n_win + lax.broadcasted_iota(jnp.int32, (ROWS_PER_TOKEN, wl), 1)
    newk = newk_ref[0, 0]
    newv = newv_ref[0, 0]

    for tok in range(n_tok):
        t = past_len + tok
        qp = (q_ref[0, 0, tok].astype(F32) * (HEAD_DIM ** -0.5)).astype(BF16)
        page_row = lax.broadcasted_iota(jnp.int32, (ROWS_PER_TOKEN, PAGE), 1)
        bias_far = jnp.broadcast_to(rowtab[:, N_BUCKETS - 1:N_BUCKETS], (ROWS_PER_TOKEN, PAGE))
        bias_last = _bias_from_rows(_bucket(t - ((npages - 1) * PAGE + page_row)), rowtab)
        bias_new = _bias_from_rows(_bucket(t - (npages * PAGE + page_row)), rowtab)
        assert BUCKET_THR[-1] <= PAGE + 1
        kts, vts, biases = [], [], []
        for n in range(N_SELECT):
            i = tok * N_SELECT + n
            blk = idx_ref[step * per_step + i]
            is_new = blk >= n_cache_blocks
            page = blk // pages_per_block
            kts.append(jnp.where(is_new, newk, kbuf_ref[slot, i]).astype(BF16))
            vts.append(jnp.where(is_new, newv, vbuf_ref[slot, i]).astype(BF16))
            bias = jnp.where(is_new, bias_new, jnp.where(page == npages - 1, bias_last, bias_far))
            ok = (page_row // SEL_BLOCK == blk % pages_per_block) & (page * PAGE + page_row <= t)
            biases.append(jnp.where(ok, bias, NEG_INF))
        s = _dot(qp, jnp.concatenate(kts, axis=1)) + jnp.concatenate(biases, axis=1)
        mask = (s > 0.5 * NEG_INF) & rvalid
        o_slc = _softmax_finish(_softmax_update(init, jnp.where(mask, s, NEG_INF), jnp.concatenate(vts, axis=1),
                                                mask=mask, v_transposed=True))
        wdist = t - wpos
        s = _dot(qp, kw) + _bias_from_rows(_bucket(wdist), rowtab)
        mask = (wdist >= 0) & (wdist < WINDOW) & (wpos >= 0) & rvalid
        o_win = _softmax_finish(_softmax_update(init, jnp.where(mask, s, NEG_INF), vw, mask=mask, v_transposed=True))
        gs = jax.nn.sigmoid(gn_ref[0, 0, tok])
        out_ref[0, 0, tok] = gs[:, 0:1] * ocmp_ref[0, 0, tok] + gs[:, 1:2] * o_slc + gs[:, 2:3] * o_win

    @pl.when(step == last)
    def _():
        for i in range(per_step):
            for c in block_copies(nxt, i, 1 - slot):
                c.wait()


def _nsa_sample_slc(idx, page_table, cache_k, cache_v, q3, ocmp, gn3, rowtab, newk, newv, swk, swv, nwk, nwv,
                    past_len, n_tok):
    nb, npages = page_table.shape
    halves = PAGE // SEL_BLOCK
    n_cache_blocks = npages * halves
    ck = cache_k.transpose(0, 2, 3, 1)
    cv = cache_v.transpose(0, 2, 3, 1)

    tile = lambda w: pl.BlockSpec((1, 1, n_tok, ROWS_PER_TOKEN, w), lambda b, h, i, p: (b, h, 0, 0, 0))
    per_bh = lambda arr: pl.BlockSpec((1, 1) + arr.shape[2:], lambda b, h, i, p: (b, h, 0, 0))
    grid_spec = pltpu.PrefetchScalarGridSpec(
        num_scalar_prefetch=2,
        grid=(nb, N_KV_HEADS),
        in_specs=[pl.BlockSpec(memory_space=pl.ANY), pl.BlockSpec(memory_space=pl.ANY),
                  tile(HEAD_DIM), tile(HEAD_DIM), tile(LANES),
                  pl.BlockSpec((1, ROWS_PER_TOKEN, N_BUCKETS), lambda b, h, i, p: (h, 0, 0)),
                  per_bh(newk), per_bh(newv), per_bh(swk), per_bh(swv), per_bh(nwk), per_bh(nwv)],
        out_specs=tile(HEAD_DIM),
        scratch_shapes=[pltpu.VMEM((2, n_tok * N_SELECT, HEAD_DIM, PAGE), F32),
                        pltpu.VMEM((2, n_tok * N_SELECT, HEAD_DIM, PAGE), F32),
                        pltpu.SemaphoreType.DMA((2,))],
    )
    return pl.pallas_call(
        functools.partial(_nsa_sample_slc_kernel, past_len=past_len, n_tok=n_tok, n_cache_blocks=n_cache_blocks,
                          npages=npages),
        grid_spec=grid_spec,
        out_shape=jax.ShapeDtypeStruct((nb, N_KV_HEADS, n_tok, ROWS_PER_TOKEN, HEAD_DIM), F32),
        compiler_params=_cparams(("arbitrary", "arbitrary")),
        name="nsa_sample_slc",
    )(idx.reshape(-1), page_table.reshape(-1), ck, cv, q3, ocmp, gn3, rowtab, newk, newv, swk, swv, nwk, nwv)


def _merge_tail(x, a, c, gates, wn_ref, wc_ref, wo_ref):
    ga = gates[:, :D_MODEL]
    gb = gates[:, D_MODEL:]
    m = (jax.nn.sigmoid(ga) * _dot(a, wn_ref[...])
         + jax.nn.sigmoid(gb) * _dot(c.astype(BF16), wc_ref[...]))
    return x + _dot(m.astype(BF16), wo_ref[...])


def _merge_prompt_kernel(x_ref, a_ref, conv_ref, halo_ref, gate_ref, cw_ref, cb_ref, wn_ref, wc_ref, wo_ref,
                         out_ref, cst_ref, *, tiles_per_batch):
    i = pl.program_id(0)
    first = i % tiles_per_batch == 0
    conv = conv_ref[...]
    bg = conv[:, :CONV_DIM]
    u = conv[:, CONV_DIM:2 * CONV_DIM] * conv[:, 2 * CONV_DIM:]
    halo = halo_ref[...]
    uh = jnp.where(first, 0.0, halo[:, CONV_DIM:2 * CONV_DIM] * halo[:, 2 * CONV_DIM:])
    tm = u.shape[0]
    row = lax.broadcasted_iota(jnp.int32, (tm, CONV_DIM), 0)
    u1 = jnp.where(row == 0, uh[7:8], pltpu.roll(u, 1, axis=0))
    u2 = pltpu.roll(u, 2, axis=0)
    u2 = jnp.where(row == 0, uh[6:7], jnp.where(row == 1, uh[7:8], u2))
    cw = cw_ref[...]
    y = ((cb_ref[...] + cw[0:1] * u2) + cw[1:2] * u1) + cw[2:3] * u
    c = bg * y
    out_ref[...] = _merge_tail(x_ref[...], a_ref[...], c, gate_ref[...], wn_ref, wc_ref, wo_ref)
    cst_ref[0] = u[tm - 8:, :]


def _merge_prompt(x2d, a2, conv3, gates, conv_w, conv_b, wn, wc, wo, t):
    n = x2d.shape[0]
    tm = min(512, t)
    assert t % tm == 0
    tpb = t // tm
    row = lambda w: pl.BlockSpec((tm, w), lambda i: (i, 0))
    const = lambda arr: pl.BlockSpec(arr.shape, lambda i: (0,) * arr.ndim)
    return pl.pallas_call(
        functools.partial(_merge_prompt_kernel, tiles_per_batch=tpb),
        grid=(n // tm,),
        in_specs=[row(D_MODEL),
                  row(NSA_WIDTH),
                  row(3 * CONV_DIM),
                  pl.BlockSpec((8, 3 * CONV_DIM), lambda i: (jnp.maximum(i * (tm // 8) - 1, 0), 0)),
                  row(2 * D_MODEL),
                  const(conv_w), const(conv_b), const(wn), const(wc), const(wo)],
        out_specs=[row(D_MODEL), pl.BlockSpec((1, 8, CONV_DIM), lambda i: (i // tpb, 0, 0))],
        out_shape=[jax.ShapeDtypeStruct((n, D_MODEL), F32),
                   jax.ShapeDtypeStruct((n // t, 8, CONV_DIM), F32)],
        compiler_params=_cparams(("arbitrary",)),
        name="merge_prompt",
    )(x2d, a2, conv3, conv3, gates, conv_w, conv_b, wn, wc, wo)


def _merge_sample_kernel(x_ref, a_ref, conv_ref, past_ref, gate_ref, cw_ref, cb_ref, wn_ref, wc_ref, wo_ref,
                         out_ref, cst_ref, *, n_tok, nb):
    conv = conv_ref[...]
    bg = conv[:, :CONV_DIM]
    u = conv[:, CONV_DIM:2 * CONV_DIM] * conv[:, 2 * CONV_DIM:]
    up = jnp.concatenate([past_ref[...], u], axis=0)
    cw = cw_ref[...]
    y = cb_ref[...]
    for k in range(CONV_WIDTH):
        y = y + cw[k:k + 1] * up[k * nb:(k + n_tok) * nb]
    c = bg * y
    out_ref[...] = _merge_tail(x_ref[...], a_ref[...].astype(BF16), c, gate_ref[...], wn_ref, wc_ref, wo_ref)
    cst_ref[...] = up[n_tok * nb:]


def _merge_sample(x_tm, a_tm, conv_tm, past_tm, gates_tm, conv_w, conv_b, wn, wc, wo, n_tok, nb):
    n = x_tm.shape[0]
    full = lambda arr: pl.BlockSpec(arr.shape, lambda i: (0,) * arr.ndim)
    args = (x_tm, a_tm, conv_tm, past_tm, gates_tm, conv_w, conv_b, wn, wc, wo)
    return pl.pallas_call(
        functools.partial(_merge_sample_kernel, n_tok=n_tok, nb=nb),
        grid=(1,),
        in_specs=[full(a) for a in args],
        out_specs=[pl.BlockSpec((n, D_MODEL), lambda i: (0, 0)),
                   pl.BlockSpec(((CONV_WIDTH - 1) * nb, CONV_DIM), lambda i: (0, 0))],
        out_shape=[jax.ShapeDtypeStruct((n, D_MODEL), F32),
                   jax.ShapeDtypeStruct(((CONV_WIDTH - 1) * nb, CONV_DIM), F32)],
        compiler_params=_cparams(("arbitrary",)),
        name="merge_sample",
    )(*args)


def _rms(x, g):
    return (x * lax.rsqrt(jnp.mean(x * x, axis=-1, keepdims=True) + RMS_EPS)) * g


def _mlp_kernel(x_ref, g_ref, wu_ref, wd_ref, gf_ref, out_ref, h_ref, acc_ref):
    j = pl.program_id(1)

    @pl.when(j == 0)
    def _():
        h_ref[...] = _rms(x_ref[...], g_ref[...]).astype(BF16)
        acc_ref[...] = jnp.zeros(acc_ref.shape, F32)

    half = h_ref.shape[0] // 2
    for r in range(2):
        rows = slice(r * half, (r + 1) * half)
        up = jnp.square(jnp.maximum(_dot(h_ref[rows, :], wu_ref[...]), 0.0)).astype(BF16)
        acc_ref[rows, :] += _dot(up, wd_ref[...])

    @pl.when(j == pl.num_programs(1) - 1)
    def _():
        out_ref[...] = _rms(x_ref[...] + acc_ref[...], gf_ref[...])


def _mlp(x2d, g_mlp, w_up, w_down, g_final):
    n = x2d.shape[0]
    tm = min(1024, n)
    tf = 1024
    assert n % tm == 0 and D_FF % tf == 0
    return pl.pallas_call(
        _mlp_kernel,
        grid=(n // tm, D_FF // tf),
        in_specs=[pl.BlockSpec((tm, D_MODEL), lambda i, j: (i, 0)),
                  pl.BlockSpec((1, D_MODEL), lambda i, j: (0, 0)),
                  pl.BlockSpec((D_MODEL, tf), lambda i, j: (0, j)),
                  pl.BlockSpec((tf, D_MODEL), lambda i, j: (j, 0)),
                  pl.BlockSpec((1, D_MODEL), lambda i, j: (0, 0))],
        out_specs=pl.BlockSpec((tm, D_MODEL), lambda i, j: (i, 0)),
        out_shape=jax.ShapeDtypeStruct((n, D_MODEL), F32),
        scratch_shapes=[pltpu.VMEM((tm, D_MODEL), BF16), pltpu.VMEM((tm, D_MODEL), F32)],
        compiler_params=_cparams(("arbitrary", "arbitrary")),
        name="mlp",
    )(x2d, g_mlp, w_up, w_down, g_final)


def _kv_rows(z, lead):
    return z.reshape(lead + (N_KV_HEADS, HEAD_DIM))


def _layer(xp, xs, caches, states, page_table, w, rel_bias):
    (cache_cmp_k, cache_cmp_v, cache_slc_k, cache_slc_v) = caches
    (state_win_k, state_win_v, state_conv) = states
    nbp, t, _ = xp.shape
    nbs, n_tok, _ = xs.shape
    npages = page_table.shape[1]
    past_len = npages * PAGE
    assert t % PAGE == 0 and n_tok <= GROUP and state_win_k.shape[1] >= n_tok

    w_all = _prep_w_in(w["w_in"])
    g_attn = w["g_attn"].reshape(1, D_MODEL)
    wn = w["w_nsa_out"].reshape(N_KV_HEADS, GROUP, HEAD_DIM, D_MODEL).transpose(1, 0, 2, 3)
    wn = wn.reshape(NSA_WIDTH, D_MODEL).astype(BF16)
    wc = w["w_conv_out"].astype(BF16)
    wo = w["w_o"].astype(BF16)
    wu = w["w_up"].astype(BF16)
    wd = w["w_down"].astype(BF16)
    g_mlp = w["g_mlp"].reshape(1, D_MODEL)
    g_final = w["g_final"].reshape(1, D_MODEL)
    conv_w = w["conv_w"]
    conv_b = w["conv_b"].reshape(1, CONV_DIM)
    cmp_k = (w["cmp_pe_k"], w["cmp_w1_k"], w["cmp_w2_k"])
    cmp_v = (w["cmp_pe_v"], w["cmp_w1_v"], w["cmp_w2_v"])

    n_p = nbp * t
    (q, kc, vc, *kv_heads, kvb, gn, conv3, gates) = _proj(xp.reshape(n_p, D_MODEL), g_attn, w_all, True, t)
    ident = jnp.arange(nbp * (t // PAGE), dtype=jnp.int32).reshape(nbp, t // PAGE)
    kcmp = _compress(kc.reshape(n_p // PAGE, PAGE, KV_WIDTH), 0, ident, *cmp_k)
    vcmp = _compress(vc.reshape(n_p // PAGE, PAGE, KV_WIDTH), 0, ident, *cmp_v)
    a2 = _nsa_prompt_pair(q.reshape(nbp, t, NSA_WIDTH), gn.reshape(nbp, t, LANES), kcmp, vcmp,
                          kvb.reshape(nbp, t, 6 * KV_WIDTH), kv_heads[3], kv_heads[5], rel_bias)
    x1, cst = _merge_prompt(xp.reshape(n_p, D_MODEL), a2.reshape(n_p, NSA_WIDTH),
                            conv3, gates, conv_w, conv_b, wn, wc, wo, t)
    yp = _mlp(x1, g_mlp, wu, wd, g_final).reshape(nbp, t, D_MODEL)
    win_p = min(WINDOW, t)
    back = lambda z: z.transpose(0, 3, 1, 2)
    new_p = tuple(back(z) for z in kv_heads[:4]) + (
        back(kv_heads[4][..., t - win_p:]), back(kv_heads[5][..., t - win_p:]), cst[:, 8 - (CONV_WIDTH - 1):, :])

    n_s = nbs * n_tok
    (qs, kcs, vcs, kss, vss, kws, vws, _, gns, conv3s, gatess) = _proj(xs.reshape(n_s, D_MODEL), g_attn, w_all, False)
    n_rows = past_len + n_tok
    n_chunk = n_rows // CMP_STRIDE
    assert n_chunk == past_len // CMP_STRIDE, "new rows must not complete a compression chunk"
    n_cmp = n_chunk - CMP_RATIO + 1
    n_blocks = -(-n_rows // SEL_BLOCK)
    kcmp_s, vcmp_s = _compress_t((cache_cmp_k, cache_cmp_v), page_table, (cmp_k, cmp_v))

    lane = np.arange(LANES)
    q5 = qs.reshape(nbs, n_tok, GROUP, LANES)
    q3 = jnp.stack([jnp.where(jnp.asarray((lane // HEAD_DIM) == h), q5, jnp.zeros_like(q5))
                    for h in range(N_KV_HEADS)], axis=1)
    q3 = jnp.pad(q3, ((0, 0), (0, 0), (0, 0), (0, ROWS_PER_TOKEN - GROUP), (0, 0)))
    tab = rel_bias.T.reshape(N_KV_HEADS, GROUP, N_BUCKETS)
    rowtab3 = jnp.pad(tab, ((0, 0), (0, ROWS_PER_TOKEN - GROUP), (0, 0)))
    rowtab = jnp.tile(rowtab3, (1, n_tok, 1))
    ocmp, idx = _nsa_sample_cmp(q3.reshape(nbs, N_KV_HEADS, n_tok * ROWS_PER_TOKEN, LANES), kcmp_s, vcmp_s,
                                rowtab, past_len, n_tok, n_cmp, n_blocks)
    idx = idx[:, :, :n_tok, :N_SELECT]
    gn3 = gns[:, :3 * N_HEADS].reshape(nbs, n_tok, N_KV_HEADS, GROUP, 3).transpose(0, 2, 1, 3, 4)
    gn3 = jnp.pad(gn3, ((0, 0), (0, 0), (0, 0), (0, ROWS_PER_TOKEN - GROUP), (0, LANES - 3)))
    q3s = jnp.pad(qs.reshape(nbs, n_tok, GROUP, N_KV_HEADS, HEAD_DIM).transpose(0, 3, 1, 2, 4),
                  ((0, 0), (0, 0), (0, 0), (0, ROWS_PER_TOKEN - GROUP), (0, 0)))
    ocmp5 = ocmp.reshape(nbs, N_KV_HEADS, n_tok, ROWS_PER_TOKEN, HEAD_DIM)
    new_t = lambda z: jnp.pad(z.reshape(nbs, n_tok, N_KV_HEADS, HEAD_DIM).transpose(0, 2, 3, 1),
                              ((0, 0), (0, 0), (0, 0), (0, LANES - n_tok)))
    a_s = _nsa_sample_slc(idx, page_table, cache_slc_k, cache_slc_v, q3s, ocmp5, gn3, rowtab3,
                          new_t(kss), new_t(vss),
                          state_win_k.transpose(0, 2, 3, 1), state_win_v.transpose(0, 2, 3, 1),
                          new_t(kws), new_t(vws), past_len, n_tok)
    a_tm = a_s[:, :, :, :GROUP, :].transpose(2, 0, 3, 1, 4).reshape(n_s, NSA_WIDTH)
    to_tm = lambda z: z.reshape(nbs, n_tok, -1).transpose(1, 0, 2).reshape(n_s, -1)
    past_tm = state_conv.transpose(1, 0, 2).reshape((CONV_WIDTH - 1) * nbs, CONV_DIM)
    x1s, csts = _merge_sample(to_tm(xs), a_tm, to_tm(conv3s), past_tm, to_tm(gatess),
                              conv_w, conv_b, wn, wc, wo, n_tok, nbs)
    ys = _mlp(x1s, g_mlp, wu, wd, g_final).reshape(n_tok, nbs, D_MODEL).transpose(1, 0, 2)
    win_buf = state_win_k.shape[1]
    kw_all = jnp.concatenate([state_win_k, _kv_rows(kws, (nbs, n_tok))], axis=1)[:, -win_buf:]
    vw_all = jnp.concatenate([state_win_v, _kv_rows(vws, (nbs, n_tok))], axis=1)[:, -win_buf:]
    new_s = tuple(_kv_rows(z, (nbs, n_tok)) for z in (kcs, vcs, kss, vss)) + (
        kw_all, vw_all, csts.reshape(CONV_WIDTH - 1, nbs, CONV_DIM).transpose(1, 0, 2))
    return yp, ys, new_p, new_s


def kernel(x_prompt, x_sample, cache_cmp_k, cache_cmp_v, cache_slc_k, cache_slc_v, state_win_k, state_win_v,
           state_conv, page_table, g_attn, w_in, cmp_pe_k, cmp_w1_k, cmp_w2_k, cmp_pe_v, cmp_w1_v, cmp_w2_v,
           conv_w, conv_b, w_nsa_out, w_conv_out, w_o, g_mlp, w_up, w_down, rel_bias, g_final):
    depth = w_in.shape[0]
    assert depth == 1, "the final norm is fused into the last layer's MLP kernel"
    names = ("g_attn", "w_in", "cmp_pe_k", "cmp_w1_k", "cmp_w2_k", "cmp_pe_v", "cmp_w1_v", "cmp_w2_v",
             "conv_w", "conv_b", "w_nsa_out", "w_conv_out", "w_o", "g_mlp", "w_up", "w_down")
    vals = (g_attn, w_in, cmp_pe_k, cmp_w1_k, cmp_w2_k, cmp_pe_v, cmp_w1_v, cmp_w2_v,
            conv_w, conv_b, w_nsa_out, w_conv_out, w_o, g_mlp, w_up, w_down)
    l = 0
    w = {k: v[l] for k, v in zip(names, vals)}
    w["g_final"] = g_final
    yp, ys, new_p, new_s = _layer(
        x_prompt, x_sample,
        (cache_cmp_k[l], cache_cmp_v[l], cache_slc_k[l], cache_slc_v[l]),
        (state_win_k[l], state_win_v[l], state_conv[l]), page_table, w, rel_bias)
    return (yp, ys) + tuple(z[None] for z in new_p) + tuple(z[None] for z in new_s)
```

```python
import functools
import math

import numpy as np
import jax
import jax.numpy as jnp
from jax import lax
from jax.experimental import pallas as pl
from jax.experimental.pallas import tpu as pltpu

F32 = jnp.float32
BF16 = jnp.bfloat16

D_MODEL = 1024
N_HEADS = 8
HEAD_DIM = 64
N_KV_HEADS = 2
GROUP = N_HEADS // N_KV_HEADS
KV_WIDTH = N_KV_HEADS * HEAD_DIM
NSA_WIDTH = N_HEADS * HEAD_DIM
CMP_LEN = 32
CMP_STRIDE = 16
CMP_RATIO = CMP_LEN // CMP_STRIDE
CMP_HIDDEN = 256
SEL_BLOCK = 64
N_SELECT = 16
WINDOW = 512
CONV_DIM = 512
CONV_WIDTH = 3
D_FF = 4 * D_MODEL
N_BUCKETS = 32
MAX_DISTANCE = 128
Q_BLOCK = 128
PAGE = 128
RMS_EPS = 1e-6
NEG_INF = -1e30
FORCE_BONUS = 1e4
LANES = 128
VMEM_LIMIT = 56 * 1024 * 1024


def _bucket_thresholds():
    max_exact = N_BUCKETS // 2
    n = np.arange(max_exact, 4 * MAX_DISTANCE).astype(np.float32)
    large = max_exact + (np.log(n / np.float32(max_exact)) / np.float32(math.log(MAX_DISTANCE / max_exact))
                         * np.float32(N_BUCKETS - max_exact)).astype(np.int32)
    large = np.minimum(large, N_BUCKETS - 1)
    thr = []
    for k in range(max_exact + 1, N_BUCKETS):
        thr.append(int(np.arange(max_exact, 4 * MAX_DISTANCE)[np.argmax(large >= k)]))
    return tuple(thr)


BUCKET_THR = _bucket_thresholds()


def _bucket(dist):
    n = jnp.maximum(dist, 0)
    big = jnp.full(n.shape, N_BUCKETS // 2, jnp.int32)
    for thr in BUCKET_THR:
        big = big + (n >= thr).astype(jnp.int32)
    return jnp.where(n < N_BUCKETS // 2, n, big)


def _bias_from_scalars(bucket, table_fn):
    out = jnp.zeros(bucket.shape, F32)
    for k in range(N_BUCKETS):
        out = jnp.where(bucket == k, table_fn(k), out)
    return out


def _bias_from_rows(bucket, rowtab):
    out = jnp.zeros(bucket.shape, F32)
    for k in range(N_BUCKETS):
        out = jnp.where(bucket == k, rowtab[:, k:k + 1], out)
    return out


def _dot(a, b):
    return jnp.dot(a, b, preferred_element_type=F32)


def _dot_t(a, b):
    return lax.dot_general(a, b, (((1,), (1,)), ((), ())), preferred_element_type=F32)


def _split_dot(p, m):
    hi = p.astype(BF16)
    lo = (p - hi.astype(F32)).astype(BF16)
    return _dot(hi, m) + _dot(lo, m)


def _cparams(sem):
    return pltpu.CompilerParams(dimension_semantics=sem, vmem_limit_bytes=VMEM_LIMIT)


PROJ_WIDTHS = (NSA_WIDTH, 6 * KV_WIDTH, LANES, 3 * CONV_DIM, 2 * D_MODEL)
PROJ_COLS = sum(PROJ_WIDTHS)


def _proj_kernel(x_ref, g_ref, w_ref, *out_refs, head_rows):
    q_ref, kvb_ref, gn_ref, conv_ref, gate_ref = out_refs[0], *out_refs[-4:]
    kv_refs = out_refs[1:-4]
    x = x_ref[...]
    y = x * lax.rsqrt(jnp.mean(x * x, axis=-1, keepdims=True) + RMS_EPS)
    hb = (y * g_ref[...]).astype(BF16)
    q_ref[...] = _dot(hb, w_ref[:, 0:NSA_WIDTH]).astype(q_ref.dtype)
    off = NSA_WIDTH
    kv = _dot(hb, w_ref[:, off:off + 6 * KV_WIDTH])
    tm = kv.shape[0]
    for i in range(6):
        z = kv[:, i * KV_WIDTH:(i + 1) * KV_WIDTH]
        if head_rows:
            kv_refs[2 + i][0] = z.T.reshape(N_KV_HEADS, HEAD_DIM, tm)
        if not head_rows or i < 2:
            kv_refs[i][...] = z
    kvb_ref[...] = kv.astype(BF16)
    off += 6 * KV_WIDTH
    gn_ref[...] = _dot(hb, w_ref[:, off:off + LANES])
    off += LANES
    conv_ref[...] = _dot(hb, w_ref[:, off:off + 3 * CONV_DIM])
    off += 3 * CONV_DIM
    gate_ref[...] = _dot(hb, w_ref[:, off:off + 2 * D_MODEL])


def _proj(x2d, g, w_all, head_rows, t=None):
    n = x2d.shape[0]
    tm = min(512, n if t is None else t)
    assert n % tm == 0
    row = lambda w: pl.BlockSpec((tm, w), lambda i: (i, 0))
    dense = lambda w, dt: (jax.ShapeDtypeStruct((n, w), dt), row(w))
    if head_rows:
        tpb = t // tm
        kv_outs = [dense(KV_WIDTH, F32)] * 2 + [
            (jax.ShapeDtypeStruct((n // t, N_KV_HEADS, HEAD_DIM, t), F32),
             pl.BlockSpec((1, N_KV_HEADS, HEAD_DIM, tm), lambda i: (i // tpb, 0, 0, i % tpb)))] * 6
    else:
        kv_outs = [dense(KV_WIDTH, F32)] * 6
    outs = ([dense(NSA_WIDTH, BF16)] + kv_outs
            + [dense(6 * KV_WIDTH, BF16), dense(LANES, F32), dense(3 * CONV_DIM, F32), dense(2 * D_MODEL, F32)])
    return pl.pallas_call(
        functools.partial(_proj_kernel, head_rows=head_rows),
        grid=(n // tm,),
        in_specs=[row(D_MODEL),
                  pl.BlockSpec((1, D_MODEL), lambda i: (0, 0)),
                  pl.BlockSpec((D_MODEL, PROJ_COLS), lambda i: (0, 0))],
        out_specs=[o[1] for o in outs],
        out_shape=[o[0] for o in outs],
        compiler_params=_cparams(("arbitrary",)),
        name="proj",
    )(x2d, g, w_all)


def _prep_w_in(w_in):
    sizes = (NSA_WIDTH,) + (KV_WIDTH,) * 6 + (3 * N_HEADS, CONV_DIM, CONV_DIM, CONV_DIM, D_MODEL, D_MODEL)
    offs = np.concatenate([[0], np.cumsum(sizes)])
    wq = w_in[:, :NSA_WIDTH].reshape(D_MODEL, N_KV_HEADS, GROUP, HEAD_DIM)
    wq = wq.transpose(0, 2, 1, 3).reshape(D_MODEL, NSA_WIDTH)
    wkv = w_in[:, offs[1]:offs[7]]
    wgn = jnp.pad(w_in[:, offs[7]:offs[8]], ((0, 0), (0, LANES - 3 * N_HEADS)))
    wrest = w_in[:, offs[8]:]
    return jnp.concatenate([wq, wkv, wgn, wrest], axis=1).astype(BF16)


CHUNKS_PER_PAGE = PAGE // CMP_STRIDE
XT_PITCH = CMP_STRIDE + 1


def _compress_kernel(pt_ref, *refs, pp):
    del pt_ref
    page_refs = refs[:pp + 1]
    wbd_ref, pe_ref, w1_ref, w2_ref, out_ref, acc_ref, pehid_ref = refs[pp + 1:]

    @pl.when((pl.program_id(0) == 0) & (pl.program_id(1) == 0))
    def _():
        pehid_ref[...] = _dot(pe_ref[...], w1_ref[...])

    cols = []
    for r in range(CMP_STRIDE):
        xr = jnp.concatenate([p[0, pl.ds(r, CHUNKS_PER_PAGE, stride=CMP_STRIDE), :] for p in page_refs], axis=0)
        cols.append(xr.astype(BF16))
    x = jnp.concatenate(cols, axis=1)
    acc_ref[...] = _dot(x, wbd_ref[...])
    nc = CHUNKS_PER_PAGE * pp
    pe = pehid_ref[0:1, :]
    hids = []
    for h in range(N_KV_HEADS):
        base = h * CMP_RATIO * CMP_HIDDEN
        p0 = acc_ref[0:nc, base:base + CMP_HIDDEN]
        p1 = acc_ref[pl.ds(1, nc), base + CMP_HIDDEN:base + 2 * CMP_HIDDEN]
        hids.append(jax.nn.gelu((pe + p0) + p1).astype(BF16))
    out_ref[0] = _dot(jnp.concatenate(hids, axis=1), w2_ref[...])


def _compress(pages, colblk, page_table, pe, w1, w2):
    nb, npages = page_table.shape
    pp = min(32, npages)
    assert npages % pp == 0
    nc = CHUNKS_PER_PAGE * pp
    w1r = w1.reshape(CMP_RATIO, CMP_STRIDE, HEAD_DIM, CMP_HIDDEN)
    blk = w1r.transpose(1, 2, 0, 3).reshape(CMP_STRIDE, HEAD_DIM, CMP_RATIO * CMP_HIDDEN)
    z = jnp.zeros_like(blk)
    wbd = jnp.concatenate([jnp.concatenate([blk, z], axis=2), jnp.concatenate([z, blk], axis=2)], axis=1)
    wbd = wbd.reshape(CMP_STRIDE * KV_WIDTH, N_KV_HEADS * CMP_RATIO * CMP_HIDDEN).astype(BF16)
    z2 = jnp.zeros_like(w2)
    w2bd = jnp.concatenate([jnp.concatenate([w2, z2], axis=1), jnp.concatenate([z2, w2], axis=1)], axis=0).astype(BF16)
    pe8 = jnp.broadcast_to(pe.reshape(1, CMP_LEN * HEAD_DIM), (8, CMP_LEN * HEAD_DIM)).astype(BF16)

    def page_spec(p):
        def imap(b, s, pt):
            return (pt[b, jnp.minimum(s * pp + p, npages - 1)], 0, colblk)
        return pl.BlockSpec((1, PAGE, KV_WIDTH), imap)

    const = lambda shape: pl.BlockSpec(shape, lambda b, s, pt: (0,) * len(shape))
    grid_spec = pltpu.PrefetchScalarGridSpec(
        num_scalar_prefetch=1,
        grid=(nb, npages // pp),
        in_specs=[page_spec(p) for p in range(pp + 1)]
        + [const(wbd.shape), const(pe8.shape), const((CMP_LEN * HEAD_DIM, CMP_HIDDEN)), const(w2bd.shape)],
        out_specs=pl.BlockSpec((1, nc, KV_WIDTH), lambda b, s, pt: (b, s, 0)),
        scratch_shapes=[pltpu.VMEM((nc + 8, N_KV_HEADS * CMP_RATIO * CMP_HIDDEN), F32),
                        pltpu.VMEM((8, CMP_HIDDEN), F32)],
    )
    return pl.pallas_call(
        functools.partial(_compress_kernel, pp=pp),
        grid_spec=grid_spec,
        out_shape=jax.ShapeDtypeStruct((nb, npages * CHUNKS_PER_PAGE, KV_WIDTH), F32),
        compiler_params=_cparams(("arbitrary", "arbitrary")),
        name="compress",
    )(page_table, *([pages] * (pp + 1)), wbd, pe8, w1.astype(BF16), w2bd)


def _compress_t_kernel(pt_ref, *refs, pp, npages, ns):
    cache_refs = refs[:ns]
    weight_refs = [refs[ns + 4 * c:ns + 4 * c + 4] for c in range(ns)]
    out_refs = refs[5 * ns:6 * ns]
    pbuf_ref, sem_ref, xt_ref, acc_ref, pehid_ref = refs[6 * ns:]
    b = pl.program_id(0)
    s = pl.program_id(1)
    n_steps = pl.num_programs(1)
    step = b * n_steps + s
    last = pl.num_programs(0) * n_steps - 1
    slot = step % 2
    copies = [(c, p) for c in range(ns) for p in range(pp + 1)]

    def page_copy(bb, ss, c, p, sl):
        page = pt_ref[bb, jnp.minimum(ss * pp + p, npages - 1)]
        return pltpu.make_async_copy(cache_refs[c].at[page], pbuf_ref.at[sl, c, p], sem_ref.at[sl])

    @pl.when(step == 0)
    def _():
        for c in range(ns):
            pehid_ref[c] = _dot(weight_refs[c][1][...], weight_refs[c][2][...])
        for c, p in copies:
            page_copy(0, 0, c, p, 0).start(priority=p % 2)

    wrap = s + 1 == n_steps
    nxt_b = jnp.minimum(jnp.where(wrap, b + 1, b), pl.num_programs(0) - 1)
    nxt_s = jnp.where(step == last, s, jnp.where(wrap, 0, s + 1))
    for c, p in copies:
        page_copy(nxt_b, nxt_s, c, p, 1 - slot).start(priority=p % 2)
    for c, p in copies:
        page_copy(b, s, c, p, slot).wait()

    nc = CHUNKS_PER_PAGE * pp
    m = nc + CHUNKS_PER_PAGE
    first_head = lax.broadcasted_iota(jnp.int32, (m, KV_WIDTH), 1) < HEAD_DIM
    for c in range(ns):
        w1p_ref, _, _, w2_ref = weight_refs[c]
        for p in range(pp + 1):
            page = pbuf_ref[slot, c, p].reshape(KV_WIDTH, PAGE)
            pt = page.T
            for k in range(CHUNKS_PER_PAGE):
                base = (p * CHUNKS_PER_PAGE + k) * XT_PITCH
                xt_ref[c, base:base + CMP_STRIDE, :] = pt[k * CMP_STRIDE:(k + 1) * CMP_STRIDE]
        pe = pehid_ref[c, 0:1, :]
        xs = [[], []]
        for r in range(0, CMP_STRIDE, 2):
            a = xt_ref[c, pl.ds(r, m, stride=XT_PITCH), :]
            d = xt_ref[c, pl.ds(r + 1, m, stride=XT_PITCH), :]
            xs[0].append(jnp.where(first_head, a, pltpu.roll(d, HEAD_DIM, axis=1)).astype(BF16))
            xs[1].append(jnp.where(first_head, pltpu.roll(a, HEAD_DIM, axis=1), d).astype(BF16))
        hids = []
        for h in range(N_KV_HEADS):
            x = jnp.concatenate(xs[h], axis=1)
            acc_ref[c, h] = _dot(x, w1p_ref[...])
            p0 = acc_ref[c, h, 0:nc, 0:CMP_HIDDEN]
            p1 = acc_ref[c, h, pl.ds(1, nc), CMP_HIDDEN:2 * CMP_HIDDEN]
            hids.append(jax.nn.gelu((pe + p0) + p1).astype(BF16))
        out_refs[c][0] = _dot(jnp.concatenate(hids, axis=1), w2_ref[...])

    @pl.when(step == last)
    def _():
        for c, p in copies:
            page_copy(nxt_b, nxt_s, c, p, 1 - slot).wait()


def _compress_t(caches, page_table, weights):
    nb, npages = page_table.shape
    ns = len(caches)
    pp = min(32, npages)
    assert npages % pp == 0
    nc = CHUNKS_PER_PAGE * pp
    pages = [c.transpose(0, 2, 3, 1) for c in caches]
    wargs = []
    for pe, w1, w2 in weights:
        w1p = w1.reshape(CMP_RATIO, CMP_STRIDE * HEAD_DIM, CMP_HIDDEN).transpose(1, 0, 2)
        w1p = w1p.reshape(CMP_STRIDE * HEAD_DIM, CMP_RATIO * CMP_HIDDEN).astype(BF16)
        z2 = jnp.zeros_like(w2)
        w2bd = jnp.concatenate([jnp.concatenate([w2, z2], axis=1), jnp.concatenate([z2, w2], axis=1)], axis=0)
        pe8 = jnp.broadcast_to(pe.reshape(1, CMP_LEN * HEAD_DIM), (8, CMP_LEN * HEAD_DIM)).astype(BF16)
        wargs += [w1p, pe8, w1.astype(BF16), w2bd.astype(BF16)]

    const = lambda arr: pl.BlockSpec(arr.shape, lambda b, s, pt: (0,) * arr.ndim)
    out_spec = pl.BlockSpec((1, nc, KV_WIDTH), lambda b, s, pt: (b, s, 0))
    grid_spec = pltpu.PrefetchScalarGridSpec(
        num_scalar_prefetch=1,
        grid=(nb, npages // pp),
        in_specs=[pl.BlockSpec(memory_space=pl.ANY)] * ns + [const(a) for a in wargs],
        out_specs=[out_spec] * ns,
        scratch_shapes=[pltpu.VMEM((2, ns, pp + 1, N_KV_HEADS, HEAD_DIM, PAGE), F32),
                        pltpu.SemaphoreType.DMA((2,)),
                        pltpu.VMEM((ns, (pp + 1) * CHUNKS_PER_PAGE * XT_PITCH + 8, KV_WIDTH), F32),
                        pltpu.VMEM((ns, N_KV_HEADS, nc + CHUNKS_PER_PAGE, CMP_RATIO * CMP_HIDDEN), F32),
                        pltpu.VMEM((ns, 8, CMP_HIDDEN), F32)],
    )
    return pl.pallas_call(
        functools.partial(_compress_t_kernel, pp=pp, npages=npages, ns=ns),
        grid_spec=grid_spec,
        out_shape=[jax.ShapeDtypeStruct((nb, npages * CHUNKS_PER_PAGE, KV_WIDTH), F32)] * ns,
        compiler_params=_cparams(("arbitrary", "arbitrary")),
        name="compress_t",
    )(page_table, *pages, *wargs)


def _sel_map(n_rows, n_cmp, n_cols, n_blocks):
    c0 = np.arange(n_rows)[:, None] * CMP_STRIDE
    s0 = np.arange(n_cols)[None, :] * SEL_BLOCK
    ov = np.minimum(c0 + CMP_LEN, s0 + SEL_BLOCK) - np.maximum(c0, s0)
    m = np.clip(ov, 0, None).astype(np.float32) / CMP_LEN
    m[n_cmp:] = 0.0
    m[:, n_blocks:] = 0.0
    return m


CMP_FRONT = 16
CMP_NEAR = 24
SLC_NEAR = 2 * Q_BLOCK
SLC_TILE = 512
WIN_KEYS = WINDOW + Q_BLOCK


def _softmax_update(state, s, v, mask=None, v_transposed=False):
    m_old, l_old, acc = state
    m_new = jnp.maximum(m_old, jnp.max(s, axis=-1, keepdims=True))
    p = jnp.exp(s - m_new)
    if mask is not None:
        p = jnp.where(mask, p, 0.0)
    alpha = jnp.exp(m_old - m_new)
    l_new = alpha * l_old + jnp.sum(p, axis=-1, keepdims=True)
    pv = _dot_t(p.astype(BF16), v) if v_transposed else _dot(p.astype(BF16), v)
    return m_new, l_new, alpha * acc + pv


def _softmax_finish(state):
    _, l, acc = state
    return jnp.where(l > 0.0, acc / jnp.where(l > 0.0, l, 1.0), 0.0)


def _tile_rows(x, n):
    return jnp.concatenate([x] * n, axis=0)


def _col_softmax_update(state, s, vt, smax=None):
    m_old, l_old, acc = state
    m_new = jnp.maximum(m_old, jnp.max(s, axis=0, keepdims=True) if smax is None else smax)
    p = jnp.exp(s - m_new)
    alpha = jnp.exp(m_old - m_new)
    l_new = alpha * l_old + jnp.sum(p, axis=0, keepdims=True)
    return m_new, l_new, alpha * acc + _dot(vt, p.astype(BF16))


def _col_softmax_finish(state):
    _, l, acc = state
    return acc * jnp.where(l > 0.0, 1.0 / jnp.where(l > 0.0, l, 1.0), 0.0)


def _nsa_prompt_pair_kernel(rb_ref, q_ref, gn_ref, kcmp_ref, vcmp_ref, vcmpt_ref, selmap_ref, selmapt_ref,
                            ks_ref, vst_ref, kw_ref, vwt_ref, et_ref, out_ref, nbc_ref, nbs_ref, nbw_ref, s_ref,
                            smax_ref):
    qb = pl.program_id(1)
    s0 = qb * Q_BLOCK
    gq = GROUP * Q_BLOCK
    heads = range(N_KV_HEADS)

    @pl.when(qb == 0)
    def _():
        for hk in heads:
            for g in range(GROUP):
                tab = lambda k, h=hk * GROUP + g: rb_ref[k, h]
                cols = slice(g * Q_BLOCK, (g + 1) * Q_BLOCK)
                cl = lax.broadcasted_iota(jnp.int32, (LANES, Q_BLOCK), 0)
                ql = lax.broadcasted_iota(jnp.int32, (LANES, Q_BLOCK), 1)
                dist = ql - CMP_STRIDE * (cl - CMP_FRONT) - (CMP_LEN - 1)
                ok = (dist >= 0) & (cl < CMP_NEAR)
                nbc_ref[hk, :, cols] = jnp.where(ok, _bias_from_scalars(_bucket(dist), tab), NEG_INF)
                kl = lax.broadcasted_iota(jnp.int32, (SLC_NEAR, Q_BLOCK), 0)
                ql = lax.broadcasted_iota(jnp.int32, (SLC_NEAR, Q_BLOCK), 1)
                dist = ql - kl + Q_BLOCK
                nbs_ref[hk, :, cols] = jnp.where(dist >= 0, _bias_from_scalars(_bucket(dist), tab), NEG_INF)
                kl = lax.broadcasted_iota(jnp.int32, (WIN_KEYS, Q_BLOCK), 0)
                ql = lax.broadcasted_iota(jnp.int32, (WIN_KEYS, Q_BLOCK), 1)
                dist = ql - kl + WINDOW
                ok = (dist >= 0) & (dist < WINDOW)
                nbw_ref[hk, :, cols] = jnp.where(ok, _bias_from_scalars(_bucket(dist), tab), NEG_INF)

    lane = lax.broadcasted_iota(jnp.int32, (Q_BLOCK, LANES), 1)
    qblk = q_ref[0].astype(F32)
    init = (jnp.full((1, gq), NEG_INF, F32), jnp.zeros((1, gq), F32), jnp.zeros((LANES, gq), F32))
    p0 = pl.multiple_of(s0, LANES)
    n_cmp_pad = kcmp_ref.shape[1] - LANES
    c8 = qb * (Q_BLOCK // CMP_STRIDE)
    near0 = pl.multiple_of(c8, 8)
    kc_far = kcmp_ref[0, CMP_FRONT:CMP_FRONT + n_cmp_pad, :].astype(BF16)
    kc_near = kcmp_ref[0, pl.ds(near0, LANES), :].astype(BF16)
    vct_near = vcmp_ref[0, pl.ds(near0, LANES), :].T.astype(BF16)
    sm_near_t = selmap_ref[pl.ds(near0, LANES), :].T.astype(BF16)
    kw = kw_ref[0, pl.ds(p0, WIN_KEYS), :]
    vwt = vwt_ref[0, :, pl.ds(p0, WIN_KEYS)]
    crow = lax.broadcasted_iota(jnp.int32, (n_cmp_pad, gq), 0)
    nrow = lax.broadcasted_iota(jnp.int32, (LANES, gq), 0)
    wrow = lax.broadcasted_iota(jnp.int32, (WIN_KEYS, gq), 0)
    krow = lax.broadcasted_iota(jnp.int32, (SLC_NEAR, gq), 0)
    n_sel = selmapt_ref.shape[0]
    blk = lax.broadcasted_iota(jnp.int32, (n_sel, Q_BLOCK), 0)
    cur = (s0 + lax.broadcasted_iota(jnp.int32, (n_sel, Q_BLOCK), 1)) // SEL_BLOCK
    forced = (blk == 0) | (blk == cur) | (blk == cur - 1)
    jcol = lax.broadcasted_iota(jnp.int32, (Q_BLOCK, n_sel), 1)
    far_blocks = (s0 - Q_BLOCK) // SEL_BLOCK

    def split_dot(a, p):
        hi = p.astype(BF16)
        return _dot(a, hi) + _dot(a, (p - hi.astype(F32)).astype(BF16))

    def before_far(hk):
        lmask = (lane // HEAD_DIM) == hk
        qp = jnp.concatenate(
            [jnp.where(lmask, qblk[:, g * LANES:(g + 1) * LANES], 0.0) for g in range(GROUP)], axis=0)
        qp = (qp * (HEAD_DIM ** -0.5)).astype(BF16)
        far_bias = jnp.concatenate(
            [jnp.full((1, Q_BLOCK), rb_ref[N_BUCKETS - 1, hk * GROUP + g], F32) for g in range(GROUP)], axis=1)

        s_far = jnp.where(crow < c8 - CMP_FRONT, _dot_t(kc_far, qp), NEG_INF)
        s_near = jnp.where(nrow >= CMP_FRONT - c8, _dot_t(kc_near, qp) + nbc_ref[hk], NEG_INF)
        m = jnp.maximum(jnp.max(s_far, axis=0, keepdims=True) + far_bias, jnp.max(s_near, axis=0, keepdims=True))
        p_far = jnp.exp(s_far - (m - far_bias))
        p_near = jnp.exp(s_near - m)
        l = jnp.sum(p_far, axis=0, keepdims=True) + jnp.sum(p_near, axis=0, keepdims=True)
        inv = jnp.where(m > 0.5 * NEG_INF, 1.0 / l, 0.0)
        pc_far = p_far * inv
        pc_near = p_near * inv
        o_cmp = _dot(vcmpt_ref[0], pc_far.astype(BF16)) + _dot(vct_near, pc_near.astype(BF16))

        s = jnp.where(wrow >= WINDOW - s0, _dot_t(kw, qp) + nbw_ref[hk], NEG_INF)
        o_win = _col_softmax_finish(_col_softmax_update(init, s, vwt))

        ps_far = sum(pc_far[:, g * Q_BLOCK:(g + 1) * Q_BLOCK] for g in range(GROUP))
        ps_near = sum(pc_near[:, g * Q_BLOCK:(g + 1) * Q_BLOCK] for g in range(GROUP))
        imp_t = split_dot(selmapt_ref[...], ps_far) + split_dot(sm_near_t, ps_near)
        val = jnp.where(blk > cur, NEG_INF, imp_t + FORCE_BONUS * forced.astype(F32))
        sel_t = jnp.zeros((n_sel, Q_BLOCK), F32)
        for _ in range(N_SELECT):
            mx = jnp.max(val, axis=0, keepdims=True)
            first = jnp.min(jnp.where(val == mx, blk, n_sel), axis=0, keepdims=True)
            hit = blk == first
            sel_t = jnp.where(hit, 1.0, sel_t)
            val = jnp.where(hit, -3e38, val)
        sel = sel_t.T > 0.5

        pen_far = jnp.where(sel & (jcol < far_blocks), 0.0, NEG_INF).astype(BF16)
        pen_near = jnp.where(sel, 0.0, NEG_INF).astype(BF16)
        q_far = jnp.concatenate([qp, _tile_rows(pen_far, GROUP)], axis=1)
        q_near = jnp.concatenate([qp, _tile_rows(pen_near, GROUP)], axis=1)
        return dict(far_bias=far_bias, o_cmp=o_cmp, o_win=o_win, q_far=q_far, q_near=q_near)

    pre = [before_far(hk) for hk in heads]
    n_tiles = (ks_ref.shape[1] - Q_BLOCK) // SLC_TILE

    def far_logits(kt):
        f0 = pl.multiple_of(Q_BLOCK + kt * SLC_TILE, LANES)
        k = jnp.concatenate([ks_ref[0, pl.ds(f0, SLC_TILE), :], et_ref[pl.ds(f0, SLC_TILE), :]], axis=1)
        return [_dot_t(k, pre[hk]["q_far"]) for hk in heads]

    def far_pair(i, states):
        for slot in range(2):
            kt = 2 * i + slot
            put_logits(1 - slot, jnp.minimum(kt + 1, n_tiles - 1))
            f0 = pl.multiple_of(Q_BLOCK + jnp.minimum(kt, n_tiles - 1) * SLC_TILE, LANES)
            vt = vst_ref[0, :, pl.ds(f0, SLC_TILE)]
            states = [_col_softmax_update(states[hk], s_ref[slot, hk], vt, smax=smax_ref[slot, hk, 0:1, :])
                      for hk in heads]
        return states

    def put_logits(slot, kt):
        for hk, s_new in enumerate(far_logits(kt)):
            s_ref[slot, hk] = s_new
            smax_ref[slot, hk] = jnp.broadcast_to(jnp.max(s_new, axis=0, keepdims=True), (8, gq))

    n_far = (qb + 2) // (SLC_TILE // Q_BLOCK)
    put_logits(0, 0)
    far_states = lax.fori_loop(0, (n_far + 1) // 2, far_pair, [init] * N_KV_HEADS)

    k_near = jnp.concatenate([ks_ref[0, pl.ds(p0, SLC_NEAR), :], et_ref[pl.ds(p0, SLC_NEAR), :]], axis=1)
    vt_near = vst_ref[0, :, pl.ds(p0, SLC_NEAR)]
    gst = jax.nn.sigmoid(gn_ref[0]).T
    outs = []
    for hk in heads:
        m_far, l_far, acc_far = far_states[hk]
        state = (m_far + pre[hk]["far_bias"], l_far, acc_far)
        s = jnp.where(krow >= Q_BLOCK - s0, _dot_t(k_near, pre[hk]["q_near"]) + nbs_ref[hk], NEG_INF)
        o_slc = _col_softmax_finish(_col_softmax_update(state, s, vt_near))
        per_g = []
        for g in range(GROUP):
            cols = slice(g * Q_BLOCK, (g + 1) * Q_BLOCK)
            c = 3 * (hk * GROUP + g)
            o = (gst[c:c + 1, :] * pre[hk]["o_cmp"][:, cols] + gst[c + 1:c + 2, :] * o_slc[:, cols]
                 + gst[c + 2:c + 3, :] * pre[hk]["o_win"][:, cols])
            per_g.append(o.T)
        outs.append(per_g)
    lane_h0 = lane < HEAD_DIM
    for g in range(GROUP):
        out_ref[0, :, g * LANES:(g + 1) * LANES] = jnp.where(lane_h0, outs[0][g], outs[1][g]).astype(out_ref.dtype)


def _nsa_prompt_pair(q, gn, kcmp, vcmp, kvb, vs_heads, vw_heads, rel_bias):
    b, t, _ = q.shape
    assert t % SLC_TILE == 0
    nqb = t // Q_BLOCK
    n_chunk = t // CMP_STRIDE
    n_cmp = n_chunk - CMP_RATIO + 1
    n_blocks = t // SEL_BLOCK
    assert n_blocks <= LANES
    cpad = ((0, 0), (CMP_FRONT, LANES - CMP_FRONT), (0, 0))
    kcmp_p = jnp.pad(kcmp, cpad)
    vcmp_p = jnp.pad(vcmp, cpad)
    vcmp_t = vcmp.transpose(0, 2, 1).astype(BF16)
    sm = _sel_map(n_chunk, n_cmp, LANES, n_blocks)
    selmap = jnp.asarray(np.pad(sm, ((CMP_FRONT, LANES - CMP_FRONT), (0, 0))), F32)
    selmap_t = jnp.asarray(sm.T, BF16)
    front = lambda z, n: jnp.pad(z, ((0, 0), (n, 0), (0, 0)))
    keys_front = lambda z, n: jnp.pad(z.reshape(b, KV_WIDTH, t).astype(BF16), ((0, 0), (0, 0), (n, 0)))
    ks_p = front(kvb[:, :, 2 * KV_WIDTH:3 * KV_WIDTH], Q_BLOCK)
    vs_t = keys_front(vs_heads, Q_BLOCK)
    kw_p = front(kvb[:, :, 4 * KV_WIDTH:5 * KV_WIDTH], WINDOW)
    vw_t = keys_front(vw_heads, WINDOW)
    pidx = np.arange(t + Q_BLOCK)
    e = (pidx[:, None] // SEL_BLOCK - Q_BLOCK // SEL_BLOCK == np.arange(LANES)[None, :])
    e = jnp.asarray(e.astype(np.float32), BF16)
    full = lambda arr: pl.BlockSpec((1,) + arr.shape[1:], lambda bi, i: (bi, 0, 0))
    const = lambda arr: pl.BlockSpec(arr.shape, lambda bi, i: (0, 0))
    return pl.pallas_call(
        _nsa_prompt_pair_kernel,
        grid=(b, nqb),
        in_specs=[pl.BlockSpec(memory_space=pltpu.SMEM),
                  pl.BlockSpec((1, Q_BLOCK, NSA_WIDTH), lambda bi, i: (bi, i, 0)),
                  pl.BlockSpec((1, Q_BLOCK, LANES), lambda bi, i: (bi, i, 0)),
                  full(kcmp_p), full(vcmp_p), full(vcmp_t), const(selmap), const(selmap_t),
                  full(ks_p), full(vs_t), full(kw_p), full(vw_t), const(e)],
        out_specs=pl.BlockSpec((1, Q_BLOCK, NSA_WIDTH), lambda bi, i: (bi, i, 0)),
        out_shape=jax.ShapeDtypeStruct((b, t, NSA_WIDTH), BF16),
        scratch_shapes=[pltpu.VMEM((N_KV_HEADS, LANES, GROUP * Q_BLOCK), F32),
                        pltpu.VMEM((N_KV_HEADS, SLC_NEAR, GROUP * Q_BLOCK), F32),
                        pltpu.VMEM((N_KV_HEADS, WIN_KEYS, GROUP * Q_BLOCK), F32),
                        pltpu.VMEM((2, N_KV_HEADS, SLC_TILE, GROUP * Q_BLOCK), F32),
                        pltpu.VMEM((2, N_KV_HEADS, 8, GROUP * Q_BLOCK), F32)],
        compiler_params=_cparams(("arbitrary", "arbitrary")),
        name="nsa_prompt",
    )(rel_bias, q, gn, kcmp_p, vcmp_p, vcmp_t, selmap, selmap_t, ks_p, vs_t, kw_p, vw_t, e)


ROWS_PER_TOKEN = 8


def _nsa_sample_cmp_kernel(q_ref, kcmp_ref, vcmp_ref, selmap_ref, rowtab_ref, ocmp_ref, idx_ref, *,
                           past_len, n_tok, n_cmp, n_blocks):
    rows = n_tok * ROWS_PER_TOKEN
    n_cpad = kcmp_ref.shape[1]
    ridx = lax.broadcasted_iota(jnp.int32, (rows, n_cpad), 0)
    ccol = lax.broadcasted_iota(jnp.int32, (rows, n_cpad), 1)
    dist = past_len + ridx // ROWS_PER_TOKEN - (CMP_STRIDE * ccol + CMP_LEN - 1)
    bucket = _bucket(dist)
    valid = (dist >= 0) & (ccol < n_cmp) & (ridx % ROWS_PER_TOKEN < GROUP)
    n_bpad = selmap_ref.shape[1]
    blk = lax.broadcasted_iota(jnp.int32, (ROWS_PER_TOKEN, n_bpad), 1)
    cur = (past_len + lax.broadcasted_iota(jnp.int32, (ROWS_PER_TOKEN, n_bpad), 0)) // SEL_BLOCK
    forced = (blk == 0) | (blk == cur) | (blk == cur - 1)
    olane = lax.broadcasted_iota(jnp.int32, (ROWS_PER_TOKEN, LANES), 1)
    biases = [_bias_from_rows(bucket, rowtab_ref[hk]) for hk in range(N_KV_HEADS)]
    for bi, hk in [(bi, hk) for bi in range(kcmp_ref.shape[0]) for hk in range(N_KV_HEADS)]:
        kc = kcmp_ref[bi].astype(BF16)
        vc = vcmp_ref[bi].astype(BF16)
        qp = (q_ref[bi, hk].astype(F32) * (HEAD_DIM ** -0.5)).astype(BF16)
        s = _dot_t(qp, kc) + biases[hk]
        s = jnp.where(valid, s, NEG_INF)
        m = jnp.max(s, axis=-1, keepdims=True)
        p = jnp.where(valid, jnp.exp(s - m), 0.0)
        l = jnp.sum(p, axis=-1, keepdims=True)
        pc = p * jnp.where(l > 0.0, 1.0 / jnp.where(l > 0.0, l, 1.0), 0.0)
        ocmp_ref[bi, hk] = _dot(pc.astype(BF16), vc)[:, hk * HEAD_DIM:(hk + 1) * HEAD_DIM]
        ps = jnp.concatenate(
            [jnp.sum(pc[i * ROWS_PER_TOKEN:(i + 1) * ROWS_PER_TOKEN], axis=0, keepdims=True) for i in range(n_tok)]
            + [jnp.zeros((ROWS_PER_TOKEN - n_tok, n_cpad), F32)], axis=0)
        imp = _split_dot(ps, selmap_ref[...])
        val = jnp.where((blk > cur) | (blk >= n_blocks), NEG_INF, imp + FORCE_BONUS * forced.astype(F32))
        picks = jnp.zeros((ROWS_PER_TOKEN, LANES), jnp.int32)
        for n in range(N_SELECT):
            mx = jnp.max(val, axis=-1, keepdims=True)
            first = jnp.min(jnp.where(val == mx, blk, n_bpad), axis=-1, keepdims=True)
            picks = jnp.where(olane == n, first, picks)
            val = jnp.where(blk == first, -3e38, val)
        idx_ref[bi, hk] = picks


def _nsa_sample_cmp(qs, kcmp, vcmp, rowtab, past_len, n_tok, n_cmp, n_blocks):
    nb = qs.shape[0]
    bps = next(d for d in (4, 2, 1) if nb % d == 0)
    n_cpad = kcmp.shape[1]
    n_bpad = -(-n_blocks // LANES) * LANES
    selmap = jnp.asarray(_sel_map(n_cpad, n_cmp, n_bpad, n_blocks), BF16)
    rows = n_tok * ROWS_PER_TOKEN
    return pl.pallas_call(
        functools.partial(_nsa_sample_cmp_kernel, past_len=past_len, n_tok=n_tok, n_cmp=n_cmp, n_blocks=n_blocks),
        grid=(nb // bps,),
        in_specs=[pl.BlockSpec((bps, N_KV_HEADS, rows, LANES), lambda b: (b, 0, 0, 0)),
                  pl.BlockSpec((bps, n_cpad, LANES), lambda b: (b, 0, 0)),
                  pl.BlockSpec((bps, n_cpad, LANES), lambda b: (b, 0, 0)),
                  pl.BlockSpec(selmap.shape, lambda b: (0, 0)),
                  pl.BlockSpec(rowtab.shape, lambda b: (0, 0, 0))],
        out_specs=[pl.BlockSpec((bps, N_KV_HEADS, rows, HEAD_DIM), lambda b: (b, 0, 0, 0)),
                   pl.BlockSpec((bps, N_KV_HEADS, ROWS_PER_TOKEN, LANES), lambda b: (b, 0, 0, 0))],
        out_shape=[jax.ShapeDtypeStruct((nb, N_KV_HEADS, rows, HEAD_DIM), F32),
                   jax.ShapeDtypeStruct((nb, N_KV_HEADS, ROWS_PER_TOKEN, LANES), jnp.int32)],
        compiler_params=_cparams(("arbitrary",)),
        name="nsa_sample_cmp",
    )(qs, kcmp, vcmp, selmap, rowtab)


def _nsa_sample_slc_kernel(idx_ref, pt_ref, ck_ref, cv_ref, q_ref, ocmp_ref, gn_ref, rowtab_ref, newk_ref, newv_ref,
                           swk_ref, swv_ref, nwk_ref, nwv_ref, out_ref, kbuf_ref, vbuf_ref, sem_ref, *,
                           past_len, n_tok, n_cache_blocks, npages):
    b = pl.program_id(0)
    hk = pl.program_id(1)
    step = b * N_KV_HEADS + hk
    last = pl.num_programs(0) * N_KV_HEADS - 1
    slot = step % 2
    per_step = n_tok * N_SELECT
    pages_per_block = PAGE // SEL_BLOCK

    def block_copies(st, i, sl):
        j = jnp.minimum(idx_ref[st * per_step + i], n_cache_blocks - 1)
        page = pt_ref[(st // N_KV_HEADS) * npages + j // pages_per_block]
        head = st % N_KV_HEADS
        return (pltpu.make_async_copy(ck_ref.at[page, head], kbuf_ref.at[sl, i], sem_ref.at[sl]),
                pltpu.make_async_copy(cv_ref.at[page, head], vbuf_ref.at[sl, i], sem_ref.at[sl]))

    @pl.when(step == 0)
    def _():
        for i in range(per_step):
            for queue, c in enumerate(block_copies(0, i, 0)):
                c.start(priority=queue)

    nxt = jnp.minimum(step + 1, last)
    for i in range(per_step):
        for queue, c in enumerate(block_copies(nxt, i, 1 - slot)):
            c.start(priority=queue)
    for i in range(per_step):
        for c in block_copies(step, i, slot):
            c.wait()

    rowtab = rowtab_ref[0]
    rvalid = lax.broadcasted_iota(jnp.int32, (ROWS_PER_TOKEN, 1), 0) < GROUP
    init = (jnp.full((ROWS_PER_TOKEN, 1), NEG_INF, F32), jnp.zeros((ROWS_PER_TOKEN, 1), F32),
            jnp.zeros((ROWS_PER_TOKEN, HEAD_DIM), F32))
    n_win = swk_ref.shape[3]
    kw = jnp.concatenate([swk_ref[0, 0], nwk_ref[0, 0]], axis=1).astype(BF16)
    vw = jnp.concatenate([swv_ref[0, 0], nwv_ref[0, 0]], axis=1).astype(BF16)
    wl = n_win + nwk_ref.shape[3]
    wpos = past_len - n_win + lax.broadcasted_iota(jnp.int32, (ROWS_PER_TOKEN, wl), 1)
    newk = newk_ref[0, 0]
    newv = newv_ref[0, 0]

    for tok in range(n_tok):
        t = past_len + tok
        qp = (q_ref[0, 0, tok].astype(F32) * (HEAD_DIM ** -0.5)).astype(BF16)
        page_row = lax.broadcasted_iota(jnp.int32, (ROWS_PER_TOKEN, PAGE), 1)
        bias_far = jnp.broadcast_to(rowtab[:, N_BUCKETS - 1:N_BUCKETS], (ROWS_PER_TOKEN, PAGE))
        bias_last = _bias_from_rows(_bucket(t - ((npages - 1) * PAGE + page_row)), rowtab)
        bias_new = _bias_from_rows(_bucket(t - (npages * PAGE + page_row)), rowtab)
        assert BUCKET_THR[-1] <= PAGE + 1
        kts, vts, biases = [], [], []
        for n in range(N_SELECT):
            i = tok * N_SELECT + n
            blk = idx_ref[step * per_step + i]
            is_new = blk >= n_cache_blocks
            page = blk // pages_per_block
            kts.append(jnp.where(is_new, newk, kbuf_ref[slot, i]).astype(BF16))
            vts.append(jnp.where(is_new, newv, vbuf_ref[slot, i]).astype(BF16))
            bias = jnp.where(is_new, bias_new, jnp.where(page == npages - 1, bias_last, bias_far))
            ok = (page_row // SEL_BLOCK == blk % pages_per_block) & (page * PAGE + page_row <= t)
            biases.append(jnp.where(ok, bias, NEG_INF))
        s = _dot(qp, jnp.concatenate(kts, axis=1)) + jnp.concatenate(biases, axis=1)
        mask = (s > 0.5 * NEG_INF) & rvalid
        o_slc = _softmax_finish(_softmax_update(init, jnp.where(mask, s, NEG_INF), jnp.concatenate(vts, axis=1),
                                                mask=mask, v_transposed=True))
        wdist = t - wpos
        s = _dot(qp, kw) + _bias_from_rows(_bucket(wdist), rowtab)
        mask = (wdist >= 0) & (wdist < WINDOW) & (wpos >= 0) & rvalid
        o_win = _softmax_finish(_softmax_update(init, jnp.where(mask, s, NEG_INF), vw, mask=mask, v_transposed=True))
        gs = jax.nn.sigmoid(gn_ref[0, 0, tok])
        out_ref[0, 0, tok] = gs[:, 0:1] * ocmp_ref[0, 0, tok] + gs[:, 1:2] * o_slc + gs[:, 2:3] * o_win

    @pl.when(step == last)
    def _():
        for i in range(per_step):
            for c in block_copies(nxt, i, 1 - slot):
                c.wait()


def _nsa_sample_slc(idx, page_table, cache_k, cache_v, q3, ocmp, gn3, rowtab, newk, newv, swk, swv, nwk, nwv,
                    past_len, n_tok):
    nb, npages = page_table.shape
    halves = PAGE // SEL_BLOCK
    n_cache_blocks = npages * halves
    ck = cache_k.transpose(0, 2, 3, 1)
    cv = cache_v.transpose(0, 2, 3, 1)

    tile = lambda w: pl.BlockSpec((1, 1, n_tok, ROWS_PER_TOKEN, w), lambda b, h, i, p: (b, h, 0, 0, 0))
    per_bh = lambda arr: pl.BlockSpec((1, 1) + arr.shape[2:], lambda b, h, i, p: (b, h, 0, 0))
    grid_spec = pltpu.PrefetchScalarGridSpec(
        num_scalar_prefetch=2,
        grid=(nb, N_KV_HEADS),
        in_specs=[pl.BlockSpec(memory_space=pl.ANY), pl.BlockSpec(memory_space=pl.ANY),
                  tile(HEAD_DIM), tile(HEAD_DIM), tile(LANES),
                  pl.BlockSpec((1, ROWS_PER_TOKEN, N_BUCKETS), lambda b, h, i, p: (h, 0, 0)),
                  per_bh(newk), per_bh(newv), per_bh(swk), per_bh(swv), per_bh(nwk), per_bh(nwv)],
        out_specs=tile(HEAD_DIM),
        scratch_shapes=[pltpu.VMEM((2, n_tok * N_SELECT, HEAD_DIM, PAGE), F32),
                        pltpu.VMEM((2, n_tok * N_SELECT, HEAD_DIM, PAGE), F32),
                        pltpu.SemaphoreType.DMA((2,))],
    )
    return pl.pallas_call(
        functools.partial(_nsa_sample_slc_kernel, past_len=past_len, n_tok=n_tok, n_cache_blocks=n_cache_blocks,
                          npages=npages),
        grid_spec=grid_spec,
        out_shape=jax.ShapeDtypeStruct((nb, N_KV_HEADS, n_tok, ROWS_PER_TOKEN, HEAD_DIM), F32),
        compiler_params=_cparams(("arbitrary", "arbitrary")),
        name="nsa_sample_slc",
    )(idx.reshape(-1), page_table.reshape(-1), ck, cv, q3, ocmp, gn3, rowtab, newk, newv, swk, swv, nwk, nwv)


def _merge_tail(x, a, c, gates, wn_ref, wc_ref, wo_ref):
    ga = gates[:, :D_MODEL]
    gb = gates[:, D_MODEL:]
    m = (jax.nn.sigmoid(ga) * _dot(a, wn_ref[...])
         + jax.nn.sigmoid(gb) * _dot(c.astype(BF16), wc_ref[...]))
    return x + _dot(m.astype(BF16), wo_ref[...])


def _merge_prompt_kernel(x_ref, a_ref, conv_ref, halo_ref, gate_ref, cw_ref, cb_ref, wn_ref, wc_ref, wo_ref,
                         out_ref, cst_ref, *, tiles_per_batch):
    i = pl.program_id(0)
    first = i % tiles_per_batch == 0
    conv = conv_ref[...]
    bg = conv[:, :CONV_DIM]
    u = conv[:, CONV_DIM:2 * CONV_DIM] * conv[:, 2 * CONV_DIM:]
    halo = halo_ref[...]
    uh = jnp.where(first, 0.0, halo[:, CONV_DIM:2 * CONV_DIM] * halo[:, 2 * CONV_DIM:])
    tm = u.shape[0]
    row = lax.broadcasted_iota(jnp.int32, (tm, CONV_DIM), 0)
    u1 = jnp.where(row == 0, uh[7:8], pltpu.roll(u, 1, axis=0))
    u2 = pltpu.roll(u, 2, axis=0)
    u2 = jnp.where(row == 0, uh[6:7], jnp.where(row == 1, uh[7:8], u2))
    cw = cw_ref[...]
    y = ((cb_ref[...] + cw[0:1] * u2) + cw[1:2] * u1) + cw[2:3] * u
    c = bg * y
    out_ref[...] = _merge_tail(x_ref[...], a_ref[...], c, gate_ref[...], wn_ref, wc_ref, wo_ref)
    cst_ref[0] = u[tm - 8:, :]


def _merge_prompt(x2d, a2, conv3, gates, conv_w, conv_b, wn, wc, wo, t):
    n = x2d.shape[0]
    tm = min(512, t)
    assert t % tm == 0
    tpb = t // tm
    row = lambda w: pl.BlockSpec((tm, w), lambda i: (i, 0))
    const = lambda arr: pl.BlockSpec(arr.shape, lambda i: (0,) * arr.ndim)
    return pl.pallas_call(
        functools.partial(_merge_prompt_kernel, tiles_per_batch=tpb),
        grid=(n // tm,),
        in_specs=[row(D_MODEL),
                  row(NSA_WIDTH),
                  row(3 * CONV_DIM),
                  pl.BlockSpec((8, 3 * CONV_DIM), lambda i: (jnp.maximum(i * (tm // 8) - 1, 0), 0)),
                  row(2 * D_MODEL),
                  const(conv_w), const(conv_b), const(wn), const(wc), const(wo)],
        out_specs=[row(D_MODEL), pl.BlockSpec((1, 8, CONV_DIM), lambda i: (i // tpb, 0, 0))],
        out_shape=[jax.ShapeDtypeStruct((n, D_MODEL), F32),
                   jax.ShapeDtypeStruct((n // t, 8, CONV_DIM), F32)],
        compiler_params=_cparams(("arbitrary",)),
        name="merge_prompt",
    )(x2d, a2, conv3, conv3, gates, conv_w, conv_b, wn, wc, wo)


def _merge_sample_kernel(x_ref, a_ref, conv_ref, past_ref, gate_ref, cw_ref, cb_ref, wn_ref, wc_ref, wo_ref,
                         out_ref, cst_ref, *, n_tok, nb):
    conv = conv_ref[...]
    bg = conv[:, :CONV_DIM]
    u = conv[:, CONV_DIM:2 * CONV_DIM] * conv[:, 2 * CONV_DIM:]
    up = jnp.concatenate([past_ref[...], u], axis=0)
    cw = cw_ref[...]
    y = cb_ref[...]
    for k in range(CONV_WIDTH):
        y = y + cw[k:k + 1] * up[k * nb:(k + n_tok) * nb]
    c = bg * y
    out_ref[...] = _merge_tail(x_ref[...], a_ref[...].astype(BF16), c, gate_ref[...], wn_ref, wc_ref, wo_ref)
    cst_ref[...] = up[n_tok * nb:]


def _merge_sample(x_tm, a_tm, conv_tm, past_tm, gates_tm, conv_w, conv_b, wn, wc, wo, n_tok, nb):
    n = x_tm.shape[0]
    full = lambda arr: pl.BlockSpec(arr.shape, lambda i: (0,) * arr.ndim)
    args = (x_tm, a_tm, conv_tm, past_tm, gates_tm, conv_w, conv_b, wn, wc, wo)
    return pl.pallas_call(
        functools.partial(_merge_sample_kernel, n_tok=n_tok, nb=nb),
        grid=(1,),
        in_specs=[full(a) for a in args],
        out_specs=[pl.BlockSpec((n, D_MODEL), lambda i: (0, 0)),
                   pl.BlockSpec(((CONV_WIDTH - 1) * nb, CONV_DIM), lambda i: (0, 0))],
        out_shape=[jax.ShapeDtypeStruct((n, D_MODEL), F32),
                   jax.ShapeDtypeStruct(((CONV_WIDTH - 1) * nb, CONV_DIM), F32)],
        compiler_params=_cparams(("arbitrary",)),
        name="merge_sample",
    )(*args)


def _rms(x, g):
    return (x * lax.rsqrt(jnp.mean(x * x, axis=-1, keepdims=True) + RMS_EPS)) * g


def _mlp_kernel(x_ref, g_ref, wu_ref, wd_ref, gf_ref, out_ref, h_ref, acc_ref):
    j = pl.program_id(1)

    @pl.when(j == 0)
    def _():
        h_ref[...] = _rms(x_ref[...], g_ref[...]).astype(BF16)
        acc_ref[...] = jnp.zeros(acc_ref.shape, F32)

    half = h_ref.shape[0] // 2
    for r in range(2):
        rows = slice(r * half, (r + 1) * half)
        up = jnp.square(jnp.maximum(_dot(h_ref[rows, :], wu_ref[...]), 0.0)).astype(BF16)
        acc_ref[rows, :] += _dot(up, wd_ref[...])

    @pl.when(j == pl.num_programs(1) - 1)
    def _():
        out_ref[...] = _rms(x_ref[...] + acc_ref[...], gf_ref[...])


def _mlp(x2d, g_mlp, w_up, w_down, g_final):
    n = x2d.shape[0]
    tm = min(1024, n)
    tf = 1024
    assert n % tm == 0 and D_FF % tf == 0
    return pl.pallas_call(
        _mlp_kernel,
        grid=(n // tm, D_FF // tf),
        in_specs=[pl.BlockSpec((tm, D_MODEL), lambda i, j: (i, 0)),
                  pl.BlockSpec((1, D_MODEL), lambda i, j: (0, 0)),
                  pl.BlockSpec((D_MODEL, tf), lambda i, j: (0, j)),
                  pl.BlockSpec((tf, D_MODEL), lambda i, j: (j, 0)),
                  pl.BlockSpec((1, D_MODEL), lambda i, j: (0, 0))],
        out_specs=pl.BlockSpec((tm, D_MODEL), lambda i, j: (i, 0)),
        out_shape=jax.ShapeDtypeStruct((n, D_MODEL), F32),
        scratch_shapes=[pltpu.VMEM((tm, D_MODEL), BF16), pltpu.VMEM((tm, D_MODEL), F32)],
        compiler_params=_cparams(("arbitrary", "arbitrary")),
        name="mlp",
    )(x2d, g_mlp, w_up, w_down, g_final)


def _kv_rows(z, lead):
    return z.reshape(lead + (N_KV_HEADS, HEAD_DIM))


def _layer(xp, xs, caches, states, page_table, w, rel_bias):
    (cache_cmp_k, cache_cmp_v, cache_slc_k, cache_slc_v) = caches
    (state_win_k, state_win_v, state_conv) = states
    nbp, t, _ = xp.shape
    nbs, n_tok, _ = xs.shape
    npages = page_table.shape[1]
    past_len = npages * PAGE
    assert t % PAGE == 0 and n_tok <= GROUP and state_win_k.shape[1] >= n_tok

    w_all = _prep_w_in(w["w_in"])
    g_attn = w["g_attn"].reshape(1, D_MODEL)
    wn = w["w_nsa_out"].reshape(N_KV_HEADS, GROUP, HEAD_DIM, D_MODEL).transpose(1, 0, 2, 3)
    wn = wn.reshape(NSA_WIDTH, D_MODEL).astype(BF16)
    wc = w["w_conv_out"].astype(BF16)
    wo = w["w_o"].astype(BF16)
    wu = w["w_up"].astype(BF16)
    wd = w["w_down"].astype(BF16)
    g_mlp = w["g_mlp"].reshape(1, D_MODEL)
    g_final = w["g_final"].reshape(1, D_MODEL)
    conv_w = w["conv_w"]
    conv_b = w["conv_b"].reshape(1, CONV_DIM)
    cmp_k = (w["cmp_pe_k"], w["cmp_w1_k"], w["cmp_w2_k"])
    cmp_v = (w["cmp_pe_v"], w["cmp_w1_v"], w["cmp_w2_v"])

    n_p = nbp * t
    (q, kc, vc, *kv_heads, kvb, gn, conv3, gates) = _proj(xp.reshape(n_p, D_MODEL), g_attn, w_all, True, t)
    ident = jnp.arange(nbp * (t // PAGE), dtype=jnp.int32).reshape(nbp, t // PAGE)
    kcmp = _compress(kc.reshape(n_p // PAGE, PAGE, KV_WIDTH), 0, ident, *cmp_k)
    vcmp = _compress(vc.reshape(n_p // PAGE, PAGE, KV_WIDTH), 0, ident, *cmp_v)
    a2 = _nsa_prompt_pair(q.reshape(nbp, t, NSA_WIDTH), gn.reshape(nbp, t, LANES), kcmp, vcmp,
                          kvb.reshape(nbp, t, 6 * KV_WIDTH), kv_heads[3], kv_heads[5], rel_bias)
    x1, cst = _merge_prompt(xp.reshape(n_p, D_MODEL), a2.reshape(n_p, NSA_WIDTH),
                            conv3, gates, conv_w, conv_b, wn, wc, wo, t)
    yp = _mlp(x1, g_mlp, wu, wd, g_final).reshape(nbp, t, D_MODEL)
    win_p = min(WINDOW, t)
    back = lambda z: z.transpose(0, 3, 1, 2)
    new_p = tuple(back(z) for z in kv_heads[:4]) + (
        back(kv_heads[4][..., t - win_p:]), back(kv_heads[5][..., t - win_p:]), cst[:, 8 - (CONV_WIDTH - 1):, :])

    n_s = nbs * n_tok
    (qs, kcs, vcs, kss, vss, kws, vws, _, gns, conv3s, gatess) = _proj(xs.reshape(n_s, D_MODEL), g_attn, w_all, False)
    n_rows = past_len + n_tok
    n_chunk = n_rows // CMP_STRIDE
    assert n_chunk == past_len // CMP_STRIDE, "new rows must not complete a compression chunk"
    n_cmp = n_chunk - CMP_RATIO + 1
    n_blocks = -(-n_rows // SEL_BLOCK)
    kcmp_s, vcmp_s = _compress_t((cache_cmp_k, cache_cmp_v), page_table, (cmp_k, cmp_v))

    lane = np.arange(LANES)
    q5 = qs.reshape(nbs, n_tok, GROUP, LANES)
    q3 = jnp.stack([jnp.where(jnp.asarray((lane // HEAD_DIM) == h), q5, jnp.zeros_like(q5))
                    for h in range(N_KV_HEADS)], axis=1)
    q3 = jnp.pad(q3, ((0, 0), (0, 0), (0, 0), (0, ROWS_PER_TOKEN - GROUP), (0, 0)))
    tab = rel_bias.T.reshape(N_KV_HEADS, GROUP, N_BUCKETS)
    rowtab3 = jnp.pad(tab, ((0, 0), (0, ROWS_PER_TOKEN - GROUP), (0, 0)))
    rowtab = jnp.tile(rowtab3, (1, n_tok, 1))
    ocmp, idx = _nsa_sample_cmp(q3.reshape(nbs, N_KV_HEADS, n_tok * ROWS_PER_TOKEN, LANES), kcmp_s, vcmp_s,
                                rowtab, past_len, n_tok, n_cmp, n_blocks)
    idx = idx[:, :, :n_tok, :N_SELECT]
    gn3 = gns[:, :3 * N_HEADS].reshape(nbs, n_tok, N_KV_HEADS, GROUP, 3).transpose(0, 2, 1, 3, 4)
    gn3 = jnp.pad(gn3, ((0, 0), (0, 0), (0, 0), (0, ROWS_PER_TOKEN - GROUP), (0, LANES - 3)))
    q3s = jnp.pad(qs.reshape(nbs, n_tok, GROUP, N_KV_HEADS, HEAD_DIM).transpose(0, 3, 1, 2, 4),
                  ((0, 0), (0, 0), (0, 0), (0, ROWS_PER_TOKEN - GROUP), (0, 0)))
    ocmp5 = ocmp.reshape(nbs, N_KV_HEADS, n_tok, ROWS_PER_TOKEN, HEAD_DIM)
    new_t = lambda z: jnp.pad(z.reshape(nbs, n_tok, N_KV_HEADS, HEAD_DIM).transpose(0, 2, 3, 1),
                              ((0, 0), (0, 0), (0, 0), (0, LANES - n_tok)))
    a_s = _nsa_sample_slc(idx, page_table, cache_slc_k, cache_slc_v, q3s, ocmp5, gn3, rowtab3,
                          new_t(kss), new_t(vss),
                          state_win_k.transpose(0, 2, 3, 1), state_win_v.transpose(0, 2, 3, 1),
                          new_t(kws), new_t(vws), past_len, n_tok)
    a_tm = a_s[:, :, :, :GROUP, :].transpose(2, 0, 3, 1, 4).reshape(n_s, NSA_WIDTH)
    to_tm = lambda z: z.reshape(nbs, n_tok, -1).transpose(1, 0, 2).reshape(n_s, -1)
    past_tm = state_conv.transpose(1, 0, 2).reshape((CONV_WIDTH - 1) * nbs, CONV_DIM)
    x1s, csts = _merge_sample(to_tm(xs), a_tm, to_tm(conv3s), past_tm, to_tm(gatess),
                              conv_w, conv_b, wn, wc, wo, n_tok, nbs)
    ys = _mlp(x1s, g_mlp, wu, wd, g_final).reshape(n_tok, nbs, D_MODEL).transpose(1, 0, 2)
    win_buf = state_win_k.shape[1]
    kw_all = jnp.concatenate([state_win_k, _kv_rows(kws, (nbs, n_tok))], axis=1)[:, -win_buf:]
    vw_all = jnp.concatenate([state_win_v, _kv_rows(vws, (nbs, n_tok))], axis=1)[:, -win_buf:]
    new_s = tuple(_kv_rows(z, (nbs, n_tok)) for z in (kcs, vcs, kss, vss)) + (
        kw_all, vw_all, csts.reshape(CONV_WIDTH - 1, nbs, CONV_DIM).transpose(1, 0, 2))
    return yp, ys, new_p, new_s


def kernel(x_prompt, x_sample, cache_cmp_k, cache_cmp_v, cache_slc_k, cache_slc_v, state_win_k, state_win_v,
           state_conv, page_table, g_attn, w_in, cmp_pe_k, cmp_w1_k, cmp_w2_k, cmp_pe_v, cmp_w1_v, cmp_w2_v,
           conv_w, conv_b, w_nsa_out, w_conv_out, w_o, g_mlp, w_up, w_down, rel_bias, g_final):
    depth = w_in.shape[0]
    assert depth == 1, "the final norm is fused into the last layer's MLP kernel"
    names = ("g_attn", "w_in", "cmp_pe_k", "cmp_w1_k", "cmp_w2_k", "cmp_pe_v", "cmp_w1_v", "cmp_w2_v",
             "conv_w", "conv_b", "w_nsa_out", "w_conv_out", "w_o", "g_mlp", "w_up", "w_down")
    vals = (g_attn, w_in, cmp_pe_k, cmp_w1_k, cmp_w2_k, cmp_pe_v, cmp_w1_v, cmp_w2_v,
            conv_w, conv_b, w_nsa_out, w_conv_out, w_o, g_mlp, w_up, w_down)
    l = 0
    w = {k: v[l] for k, v in zip(names, vals)}
    w["g_final"] = g_final
    yp, ys, new_p, new_s = _layer(
        x_prompt, x_sample,
        (cache_cmp_k[l], cache_cmp_v[l], cache_slc_k[l], cache_slc_v[l]),
        (state_win_k[l], state_win_v[l], state_conv[l]), page_table, w, rel_bias)
    return (yp, ys) + tuple(z[None] for z in new_p) + tuple(z[None] for z in new_s)
```
